```python
import jax, jax.numpy as jnp
from jax import lax
import numpy as np

D_MODEL = 2048
BATCH = 8
SEQ = 8192
DEPTH = 1

D_MIX = D_MODEL
D_HGRN = D_MIX // 2
HGRN_HEAD = 128
HGRN_HEADS = D_HGRN // HGRN_HEAD
HGRN_CHUNK = 64
D_ATTN = D_MIX - D_HGRN
ATTN_HEAD = 64
ATTN_HEADS = D_ATTN // ATTN_HEAD
DILATED_PATTERNS = ((128, 1), (512, 4), (2048, 16))
NORM_EPS = 1e-6
SPLIT_WIDTHS = (D_HGRN, D_HGRN, D_HGRN, D_HGRN, D_ATTN, D_ATTN, D_ATTN, D_ATTN)
D_IN = sum(SPLIT_WIDTHS)

kernel_name = "hymba_hgrn2_dilated_alibi_block"


def rms_norm(x, gain):
    xf = x.astype(jnp.float32)
    y = xf * lax.rsqrt(jnp.mean(xf * xf, axis=-1, keepdims=True) + NORM_EPS)
    return (y * gain.astype(jnp.float32)).astype(x.dtype)


def alibi_slopes(n_heads):
    return jnp.exp2(-8.0 * jnp.arange(1, n_heads + 1, dtype=jnp.float32) / n_heads)


def chunked_gated_recurrence(q, k, log_f, v):
    B, H, S, Dk = q.shape
    Dv = v.shape[-1]
    C = HGRN_CHUNK
    N = S // C
    q, k, log_f = (a.reshape(B, H, N, C, Dk) for a in (q, k, log_f))
    v = v.reshape(B, H, N, C, Dv)
    b = jnp.cumsum(log_f, axis=3)
    b_last = b[:, :, :, -1:, :]
    q_dec = q * jnp.exp(b)
    k_dec = k * jnp.exp(-b)
    k_end = k * jnp.exp(b_last - b)
    causal = jnp.tril(jnp.ones((C, C), dtype=bool))
    a_intra = jnp.where(causal, jnp.einsum('bhncd,bhnsd->bhncs', q_dec, k_dec), 0.0)
    o_intra = jnp.einsum('bhncs,bhnsv->bhncv', a_intra, v)
    kv_chunk = jnp.einsum('bhncd,bhncv->nbhdv', k_end, v)
    chunk_decay = jnp.exp(b_last[:, :, :, 0, :]).transpose(2, 0, 1, 3)

    def step(state, inp):
        dec, kv_n = inp
        return dec[..., None] * state + kv_n, state

    _, prev_states = lax.scan(step, jnp.zeros((B, H, Dk, Dv), jnp.float32),
                              (chunk_decay, kv_chunk))
    o_inter = jnp.einsum('bhncd,nbhdv->bhncv', q_dec, prev_states)
    return (o_intra + o_inter).reshape(B, H, S, Dv)


def hgrn2_mixer(q_pre, f_pre, i_pre, lb, g_norm):
    B, S, _ = q_pre.shape

    def heads(a):
        return a.astype(jnp.float32).reshape(B, S, HGRN_HEADS, HGRN_HEAD).transpose(0, 2, 1, 3)

    q = jax.nn.silu(heads(q_pre))
    lbh = lb.astype(jnp.float32).reshape(HGRN_HEADS, 1, HGRN_HEAD)
    f = lbh + (1.0 - lbh) * jax.nn.sigmoid(heads(f_pre))
    k = 1.0 - f
    o = chunked_gated_recurrence(q, k, jnp.log(f), heads(i_pre))
    o = o * lax.rsqrt(jnp.mean(o * o, axis=-1, keepdims=True) + NORM_EPS) * g_norm.astype(jnp.float32)
    return o.transpose(0, 2, 1, 3).reshape(B, S, D_HGRN)


def dilated_pattern_attention(q, k, v, slopes, window, dilation):
    B, H, S, Dh = q.shape
    d = dilation
    W = window // d
    L = S // d
    nb = -(-L // W)
    Lp = nb * W

    def regroup(a):
        a = a.reshape(B, H, L, d, Dh).transpose(0, 1, 3, 2, 4)
        a = jnp.pad(a, ((0, 0), (0, 0), (0, 0), (0, Lp - L), (0, 0)))
        return a.reshape(B, H, d, nb, W, Dh)

    def with_prev_block(a):
        prev = jnp.pad(a, ((0, 0), (0, 0), (0, 0), (1, 0), (0, 0), (0, 0)))[:, :, :, :-1]
        return jnp.concatenate([prev, a], axis=4)

    qb = regroup(q)
    kc = with_prev_block(regroup(k))
    vc = with_prev_block(regroup(v))
    s = jnp.einsum('bhrnqd,bhrnkd->bhrnqk', qb, kc) * (Dh ** -0.5)
    i_idx = jnp.arange(W)[:, None]
    j_idx = jnp.arange(2 * W)[None, :]
    delta = W + i_idx - j_idx
    blk = jnp.arange(nb)[:, None, None]
    valid = (delta >= 0) & (delta <= W) & ((blk - 1) * W + j_idx >= 0)
    bias = -slopes[:, None, None] * (d * delta).astype(jnp.float32)
    s = jnp.where(valid, s + bias[None, :, None, None], -jnp.inf)
    m = jnp.max(s, axis=-1, keepdims=True)
    p = jnp.exp(s - m)
    den = jnp.sum(p, axis=-1, keepdims=True)
    o = jnp.einsum('bhrnqk,bhrnkd->bhrnqd', p, vc) / den
    lse = (m + jnp.log(den))[..., 0]
    o = o.reshape(B, H, d, Lp, Dh)[:, :, :, :L].transpose(0, 1, 3, 2, 4).reshape(B, H, S, Dh)
    lse = lse.reshape(B, H, d, Lp)[:, :, :, :L].transpose(0, 1, 3, 2).reshape(B, H, S)
    return o, lse


def dilated_attention_mixer(q_pre, k_pre, v_pre):
    B, S, _ = q_pre.shape

    def heads(a):
        return a.astype(jnp.float32).reshape(B, S, ATTN_HEADS, ATTN_HEAD).transpose(0, 2, 1, 3)

    q, k, v = heads(q_pre), heads(k_pre), heads(v_pre)
    slopes = alibi_slopes(ATTN_HEADS)
    outs, lses = [], []
    for window, dilation in DILATED_PATTERNS:
        o, lse = dilated_pattern_attention(q, k, v, slopes, window, dilation)
        outs.append(o)
        lses.append(lse)
    weights = jax.nn.softmax(jnp.stack(lses, axis=0), axis=0)
    o = jnp.einsum('pbhs,pbhsd->bhsd', weights, jnp.stack(outs, axis=0))
    return o.transpose(0, 2, 1, 3).reshape(B, S, D_ATTN)


def _fwd_setup_inputs(seed: int = 0) -> dict:
    key = jax.random.key(seed)
    ks = jax.random.split(key, 8)
    x = jax.random.normal(ks[0], (BATCH, SEQ, D_MODEL), jnp.float32)
    norm_gain = 1.0 + 0.01 * jax.random.normal(ks[1], (DEPTH, D_MODEL), jnp.float32)
    w_in = jax.random.normal(ks[2], (DEPTH, D_MODEL, D_IN), jnp.float32) * D_MODEL ** -0.5
    lb_logits = 0.1 * jax.random.normal(ks[3], (DEPTH + 1, D_HGRN), jnp.float32)
    hgrn_gnorm = 1.0 + 0.01 * jax.random.normal(ks[4], (DEPTH, HGRN_HEAD), jnp.float32)
    w_out = jax.random.normal(ks[5], (DEPTH, D_MIX, D_MODEL), jnp.float32) * D_MIX ** -0.5
    final_gain = 1.0 + 0.01 * jax.random.normal(ks[6], (D_MODEL,), jnp.float32)
    return {"x": x, "norm_gain": norm_gain, "w_in": w_in, "lb_logits": lb_logits,
            "hgrn_gnorm": hgrn_gnorm, "w_out": w_out, "final_gain": final_gain}


def _fwd_reference(x, norm_gain, w_in, lb_logits, hgrn_gnorm, w_out, final_gain):
    lb_all = jnp.cumsum(jax.nn.softmax(lb_logits.astype(jnp.float32), axis=0), axis=0)
    split_points = [int(v) for v in np.cumsum(SPLIT_WIDTHS)[:-1]]
    for layer in range(DEPTH):
        h = rms_norm(x, norm_gain[layer])
        z = jnp.einsum('bsd,de->bse', h, w_in[layer])
        q_h, f_h, i_h, g_h, q_a, k_a, v_a, g_a = jnp.split(z, split_points, axis=-1)
        y_h = hgrn2_mixer(q_h, f_h, i_h, lb_all[layer], hgrn_gnorm[layer]) * jax.nn.silu(g_h.astype(jnp.float32))
        y_a = dilated_attention_mixer(q_a, k_a, v_a) * jax.nn.silu(g_a.astype(jnp.float32))
        y = jnp.concatenate([y_h, y_a], axis=-1).astype(x.dtype)
        x = x + jnp.einsum('bse,ed->bsd', y, w_out[layer])
    return rms_norm(x, final_gain)


import jax as _jax
import jax.numpy as _jnp

TWIN_FORMAT = 'train_step'
FWD_PARAMS = ['x', 'norm_gain', 'w_in', 'lb_logits', 'hgrn_gnorm', 'w_out', 'final_gain']
TWIN_WEIGHTS = ['norm_gain', 'w_in', 'lb_logits', 'hgrn_gnorm', 'w_out', 'final_gain']
TWIN_DIFF_INPUT = 'x'
TWIN_INPUTS = ['x', 'norm_gain', 'w_in', 'lb_logits', 'hgrn_gnorm', 'w_out', 'final_gain', 'loss_target', 'm_norm_gain', 'm_w_in', 'm_lb_logits', 'm_hgrn_gnorm', 'm_w_out', 'm_final_gain', 'v_norm_gain', 'v_w_in', 'v_lb_logits', 'v_hgrn_gnorm', 'v_w_out', 'v_final_gain']
TWIN_OUTPUTS = ['loss', 'grad_x', 'grad_norm_gain', 'grad_w_in', 'grad_lb_logits', 'grad_hgrn_gnorm', 'grad_w_out', 'grad_final_gain', 'delta_norm_gain', 'delta_w_in', 'delta_lb_logits', 'delta_hgrn_gnorm', 'delta_w_out', 'delta_final_gain', 'new_m_norm_gain', 'new_m_w_in', 'new_m_lb_logits', 'new_m_hgrn_gnorm', 'new_m_w_out', 'new_m_final_gain', 'new_v_norm_gain', 'new_v_w_in', 'new_v_lb_logits', 'new_v_hgrn_gnorm', 'new_v_w_out', 'new_v_final_gain']
TWIN_LEAF_KINDS = {'loss': 'loss', 'grad_x': 'grad_x', 'grad_norm_gain': 'grad_w', 'grad_w_in': 'grad_w', 'grad_lb_logits': 'grad_w', 'grad_hgrn_gnorm': 'grad_w', 'grad_w_out': 'grad_w', 'grad_final_gain': 'grad_w', 'delta_norm_gain': 'delta_w', 'delta_w_in': 'delta_w', 'delta_lb_logits': 'delta_w', 'delta_hgrn_gnorm': 'delta_w', 'delta_w_out': 'delta_w', 'delta_final_gain': 'delta_w', 'new_m_norm_gain': 'new_m', 'new_m_w_in': 'new_m', 'new_m_lb_logits': 'new_m', 'new_m_hgrn_gnorm': 'new_m', 'new_m_w_out': 'new_m', 'new_m_final_gain': 'new_m', 'new_v_norm_gain': 'new_v', 'new_v_w_in': 'new_v', 'new_v_lb_logits': 'new_v', 'new_v_hgrn_gnorm': 'new_v', 'new_v_w_out': 'new_v', 'new_v_final_gain': 'new_v'}


def _forward(args):
    return _fwd_reference(*[args[k] for k in FWD_PARAMS])


def _output_shape():
    def fwd():
        inp = _fwd_setup_inputs(0)
        return _fwd_reference(*[inp[k] for k in FWD_PARAMS])
    out = _jax.eval_shape(fwd)
    return out.shape, out.dtype

N_MICROBATCH = 1
ADAM_LR = 0.001
ADAM_B1 = 0.9
ADAM_B2 = 0.999
ADAM_EPS = 1e-08
ADAM_WD = 0.01
ADAM_STEP = 10
PER_EXAMPLE_BATCH_AXIS = {'x': 0, 'loss_target': 0}
SHARED_INPUTS = []
_WEIGHT_DTYPES = {'norm_gain': _jnp.float32, 'w_in': _jnp.float32, 'lb_logits': _jnp.float32, 'hgrn_gnorm': _jnp.float32, 'w_out': _jnp.float32, 'final_gain': _jnp.float32}
MOMENT_SCALE = {'norm_gain': 7.737895e-02, 'w_in': 3.855910e-02, 'lb_logits': 6.441834e-03, 'hgrn_gnorm': 2.307249e-01, 'w_out': 5.174815e-02, 'final_gain': 3.196581e+01}


def _to_microbatches(a, axis):
    t = _jnp.moveaxis(a, axis, 0)
    t = t.reshape((N_MICROBATCH, t.shape[0] // N_MICROBATCH) + t.shape[1:])
    return _jnp.moveaxis(t, 1, axis + 1)


def setup_inputs(seed: int = 0) -> dict:
    inp = _fwd_setup_inputs(seed)
    key = _jax.random.fold_in(_jax.random.key(seed), 7919)
    shape, _ = _output_shape()
    out = dict(inp)
    out["loss_target"] = _jax.random.normal(_jax.random.fold_in(key, 0), shape, _jnp.float32)
    for i, name in enumerate(TWIN_WEIGHTS):
        w = inp[name].astype(_jnp.float32)
        if MOMENT_SCALE is None:
            s = _jnp.sqrt(_jnp.mean(_jnp.square(w)) + 1e-30)
        else:
            s = MOMENT_SCALE[name]
        km, kv = _jax.random.split(_jax.random.fold_in(key, i + 1))
        out[name] = w
        out["m_" + name] = s * _jax.random.normal(km, w.shape, _jnp.float32)
        out["v_" + name] = (s * s) * _jax.random.uniform(kv, w.shape, _jnp.float32, 0.5, 1.5)
    if N_MICROBATCH > 1:
        for name, axis in PER_EXAMPLE_BATCH_AXIS.items():
            out[name] = _to_microbatches(out[name], axis)
    return {'x': out['x'], 'norm_gain': out['norm_gain'], 'w_in': out['w_in'], 'lb_logits': out['lb_logits'], 'hgrn_gnorm': out['hgrn_gnorm'], 'w_out': out['w_out'], 'final_gain': out['final_gain'], 'loss_target': out['loss_target'], 'm_norm_gain': out['m_norm_gain'], 'm_w_in': out['m_w_in'], 'm_lb_logits': out['m_lb_logits'], 'm_hgrn_gnorm': out['m_hgrn_gnorm'], 'm_w_out': out['m_w_out'], 'm_final_gain': out['m_final_gain'], 'v_norm_gain': out['v_norm_gain'], 'v_w_in': out['v_w_in'], 'v_lb_logits': out['v_lb_logits'], 'v_hgrn_gnorm': out['v_hgrn_gnorm'], 'v_w_out': out['v_w_out'], 'v_final_gain': out['v_final_gain']}


def _loss(weights, diff, rest, loss_target):
    with _jax.named_scope("forward"):
        args = {**rest, TWIN_DIFF_INPUT: diff, **{k: w.astype(_WEIGHT_DTYPES[k]) for k, w in weights.items()}}
        y = _forward(args)
    with _jax.named_scope("loss_head"):
        err = _jnp.square(y.astype(_jnp.float32) - loss_target)
        return 0.5 * _jnp.sum(_jnp.mean(err, axis=-1)) if err.ndim else 0.5 * err


def _adamw(w, g, m, v):
    m = ADAM_B1 * m + (1.0 - ADAM_B1) * g
    v = ADAM_B2 * v + (1.0 - ADAM_B2) * _jnp.square(g)
    m_hat = m / (1.0 - ADAM_B1 ** ADAM_STEP)
    v_hat = v / (1.0 - ADAM_B2 ** ADAM_STEP)
    delta = -ADAM_LR * (m_hat / (_jnp.sqrt(v_hat) + ADAM_EPS) + ADAM_WD * w)
    return delta, m, v


def reference(x, norm_gain, w_in, lb_logits, hgrn_gnorm, w_out, final_gain, loss_target, m_norm_gain, m_w_in, m_lb_logits, m_hgrn_gnorm, m_w_out, m_final_gain, v_norm_gain, v_w_in, v_lb_logits, v_hgrn_gnorm, v_w_out, v_final_gain):
    given = dict(x=x, norm_gain=norm_gain, w_in=w_in, lb_logits=lb_logits, hgrn_gnorm=hgrn_gnorm, w_out=w_out, final_gain=final_gain, loss_target=loss_target, m_norm_gain=m_norm_gain, m_w_in=m_w_in, m_lb_logits=m_lb_logits, m_hgrn_gnorm=m_hgrn_gnorm, m_w_out=m_w_out, m_final_gain=m_final_gain, v_norm_gain=v_norm_gain, v_w_in=v_w_in, v_lb_logits=v_lb_logits, v_hgrn_gnorm=v_hgrn_gnorm, v_w_out=v_w_out, v_final_gain=v_final_gain)
    weights = {n: given[n] for n in TWIN_WEIGHTS}
    shared = {n: given[n] for n in SHARED_INPUTS}
    per_example = {n: given[n] for n in ['x']}
    grad_fn = _jax.value_and_grad(_loss, argnums=(0, 1))

    def one_microbatch(ex, loss_target):
        ex = dict(ex)
        diff = ex.pop(TWIN_DIFF_INPUT)
        return grad_fn(weights, diff, {**shared, **ex}, loss_target)

    if N_MICROBATCH == 1:
        loss, (grad_w, grad_x) = one_microbatch(per_example, given["loss_target"])
    else:
        def body(carry, xs):
            loss_sum, grad_sum = carry
            l_k, (gw_k, gx_k) = one_microbatch(xs[0], xs[1])
            with _jax.named_scope("update"):
                return (loss_sum + l_k, _jax.tree.map(_jnp.add, grad_sum, gw_k)), gx_k

        init = (_jnp.zeros((), _jnp.float32), _jax.tree.map(_jnp.zeros_like, weights))
        (loss, grad_w), grad_x = _jax.lax.scan(body, init, (per_example, given["loss_target"]))
    with _jax.named_scope("update"):
        delta_w, new_m, new_v = {}, {}, {}
        for n in TWIN_WEIGHTS:
            delta_w[n], new_m[n], new_v[n] = _adamw(weights[n], grad_w[n], given["m_" + n], given["v_" + n])
    return (loss, grad_x, *[grad_w[n] for n in TWIN_WEIGHTS], *[delta_w[n] for n in TWIN_WEIGHTS],
            *[new_m[n] for n in TWIN_WEIGHTS], *[new_v[n] for n in TWIN_WEIGHTS])
```

```python
import functools

import jax
import jax.numpy as jnp
from jax import lax
from jax.experimental import pallas as pl
from jax.experimental.pallas import tpu as pltpu

F32 = jnp.float32
BF16 = jnp.bfloat16
MESH = pl.DeviceIdType.MESH

NORM_EPS = 1e-6
HGRN_HEAD = 128
HGRN_CHUNK = 64
ATTN_HEAD = 64
LANES = 128
BAND = 128
DILATIONS = (1, 4, 16)
ATTN_SCALE = ATTN_HEAD ** -0.5
NEG = -1e30

ADAM_LR = 0.001
ADAM_B1 = 0.9
ADAM_B2 = 0.999
ADAM_EPS = 1e-08
ADAM_WD = 0.01
ADAM_STEP = 10

MIB = 1024 * 1024


def _cp(semantics=None, vmem_mib=48):
    return pltpu.CompilerParams(dimension_semantics=semantics, vmem_limit_bytes=vmem_mib * MIB)


def _dot(a, b):
    return jnp.dot(a, b, preferred_element_type=F32)


def _dot_nt(a, b):
    return lax.dot_general(a, b, (((1,), (1,)), ((), ())), preferred_element_type=F32)


def _dot_tn(a, b):
    return lax.dot_general(a, b, (((0,), (0,)), ((), ())), preferred_element_type=F32)


def _exact_dot(t_bf16, x):
    hi = x.astype(BF16)
    r1 = x - hi.astype(F32)
    mid = r1.astype(BF16)
    lo = (r1 - mid.astype(F32)).astype(BF16)
    return _dot(t_bf16, hi) + _dot(t_bf16, mid) + _dot(t_bf16, lo)


def _sigmoid(z):
    return jax.nn.sigmoid(z)


def _silu_and_grad(z):
    s = _sigmoid(z)
    return z * s, s * (1.0 + z * (1.0 - s))


def _seg_select(j, values):
    out = values[0]
    for t, v in enumerate(values[1:], 1):
        out = jnp.where(j == t, v, out)
    return out


def _rms_fwd(x2, gain):
    S, D = x2.shape
    tm = min(512, S)

    def body(x_ref, g_ref, h_ref, r_ref):
        x = x_ref[...]
        r = lax.rsqrt(jnp.mean(x * x, axis=-1, keepdims=True) + NORM_EPS)
        h_ref[...] = ((x * r) * g_ref[...]).astype(BF16)
        r_ref[...] = r

    return pl.pallas_call(
        body, grid=(S // tm,), name="rms_fwd",
        in_specs=[pl.BlockSpec((tm, D), lambda i: (i, 0)), pl.BlockSpec((1, D), lambda i: (0, 0))],
        out_specs=[pl.BlockSpec((tm, D), lambda i: (i, 0)), pl.BlockSpec((tm, 1), lambda i: (i, 0))],
        out_shape=[jax.ShapeDtypeStruct((S, D), BF16), jax.ShapeDtypeStruct((S, 1), F32)],
        compiler_params=_cp(("parallel",)),
    )(x2, gain)


def _in_proj(h, w_all, segs, out_dtype, name):
    S, D = h.shape
    SEG = w_all.shape[2] // 2
    n = len(segs)
    tm = min(512, S)

    def body(h_ref, w_ref, o_ref):
        o_ref[...] = _dot(h_ref[...], w_ref[...]).astype(out_dtype)

    def w_map(j, i):
        seg = _seg_select(j, segs)
        return (seg // 2, 0, seg % 2)

    return pl.pallas_call(
        body, grid=(n, S // tm), name=name,
        in_specs=[pl.BlockSpec((tm, D), lambda j, i: (i, 0)), pl.BlockSpec((None, D, SEG), w_map)],
        out_specs=pl.BlockSpec((None, tm, SEG), lambda j, i: (j, i, 0)),
        out_shape=jax.ShapeDtypeStruct((n, S, SEG), out_dtype),
        compiler_params=_cp(("parallel", "parallel")),
    )(h, w_all)


def _out_proj_loss(yh, ya, w_out, x2, tgt, fgain):
    S, D = x2.shape
    SEG = yh.shape[1]
    tm = min(256, S)

    def body(yh_ref, ya_ref, w_ref, x_ref, t_ref, fg_ref, dout_ref, doutb_ref, loss_ref, dfg_ref):
        i = pl.program_id(0)

        @pl.when(i == 0)
        def _():
            loss_ref[...] = jnp.zeros_like(loss_ref)
            dfg_ref[...] = jnp.zeros_like(dfg_ref)

        out = x_ref[...] + _dot(yh_ref[...], w_ref[pl.ds(0, SEG), :]) + _dot(ya_ref[...], w_ref[pl.ds(SEG, SEG), :])
        r = lax.rsqrt(jnp.mean(out * out, axis=-1, keepdims=True) + NORM_EPS)
        n = out * r
        fg = fg_ref[...]
        err = n * fg - t_ref[...]
        loss_ref[...] += 0.5 * jnp.sum(jnp.mean(err * err, axis=-1, keepdims=True), axis=0, keepdims=True)
        dy = err * (1.0 / D)
        dfg_ref[...] += jnp.sum(dy * n, axis=0, keepdims=True)
        dn = dy * fg
        dout = r * (dn - n * jnp.mean(dn * n, axis=-1, keepdims=True))
        dout_ref[...] = dout
        doutb_ref[...] = dout.astype(BF16)

    row = lambda i: (i, 0)
    fix = lambda i: (0, 0)
    return pl.pallas_call(
        body, grid=(S // tm,), name="out_proj_loss",
        in_specs=[pl.BlockSpec((tm, SEG), row), pl.BlockSpec((tm, SEG), row), pl.BlockSpec((2 * SEG, D), fix),
                  pl.BlockSpec((tm, D), row), pl.BlockSpec((tm, D), row), pl.BlockSpec((1, D), fix)],
        out_specs=[pl.BlockSpec((tm, D), row), pl.BlockSpec((tm, D), row), pl.BlockSpec((1, 1), fix),
                   pl.BlockSpec((1, D), fix)],
        out_shape=[jax.ShapeDtypeStruct((S, D), F32), jax.ShapeDtypeStruct((S, D), BF16),
                   jax.ShapeDtypeStruct((1, 1), F32), jax.ShapeDtypeStruct((1, D), F32)],
        compiler_params=_cp(("arbitrary",)),
    )(yh, ya, w_out, x2, tgt, fgain)


def _dy_proj(doutb, w_out):
    S, D = doutb.shape
    K = w_out.shape[0]
    tm = min(512, S)

    def body(d_ref, w_ref, o_ref):
        o_ref[...] = _dot_nt(d_ref[...], w_ref[...])

    return pl.pallas_call(
        body, grid=(S // tm,), name="dy_proj",
        in_specs=[pl.BlockSpec((tm, D), lambda i: (i, 0)), pl.BlockSpec((K, D), lambda i: (0, 0))],
        out_specs=pl.BlockSpec((tm, K), lambda i: (i, 0)),
        out_shape=jax.ShapeDtypeStruct((S, K), F32),
        compiler_params=_cp(("parallel",)),
    )(doutb, w_out)


def _grad_w_out(yh, ya, doutb):
    S, SEG = yh.shape
    D = doutb.shape[1]
    R = (2 * SEG) // 4
    nb_half = SEG // R
    tk = min(512, S)

    def body(yh_ref, ya_ref, d_ref, o_ref):
        q = pl.program_id(0)
        k = pl.program_id(1)

        @pl.when(k == 0)
        def _():
            o_ref[...] = jnp.zeros_like(o_ref)

        @pl.when(q < nb_half)
        def _():
            o_ref[...] += _dot_tn(yh_ref[...], d_ref[...])

        @pl.when(q >= nb_half)
        def _():
            o_ref[...] += _dot_tn(ya_ref[...], d_ref[...])

    return pl.pallas_call(
        body, grid=(4, S // tk), name="grad_w_out",
        in_specs=[pl.BlockSpec((tk, R), lambda q, k: (k, jnp.minimum(q, nb_half - 1))),
                  pl.BlockSpec((tk, R), lambda q, k: (k, jnp.maximum(q - nb_half, 0))),
                  pl.BlockSpec((tk, D), lambda q, k: (k, 0))],
        out_specs=pl.BlockSpec((None, R, D), lambda q, k: (q, 0, 0)),
        out_shape=jax.ShapeDtypeStruct((4, R, D), F32),
        compiler_params=_cp(("parallel", "arbitrary")),
    )(yh, ya, doutb)


def _dz_sources(sources):
    counts = [s.shape[0] for s in sources]
    starts = [sum(counts[:k]) for k in range(len(counts))]
    assert sum(counts) == 8
    return counts, starts


def _dh_proj(sources, w_all):
    S = sources[0].shape[1]
    D = w_all.shape[1]
    SEG = w_all.shape[2] // 2
    counts, starts = _dz_sources(sources)
    ns = len(sources)
    tm = min(512, S)

    def body(*refs):
        src = refs[:ns]
        w_ref, o_ref = refs[ns:]
        j = pl.program_id(1)

        @pl.when(j == 0)
        def _():
            o_ref[...] = jnp.zeros_like(o_ref)

        for k in range(ns):
            @pl.when((j >= starts[k]) & (j < starts[k] + counts[k]))
            def _(k=k):
                o_ref[...] += _dot_nt(src[k][...], w_ref[...])

    def src_spec(k):
        return pl.BlockSpec((None, tm, SEG),
                            lambda i, j: (jnp.clip(j - starts[k], 0, counts[k] - 1), i, 0))

    return pl.pallas_call(
        body, grid=(S // tm, 8), name="dh_proj",
        in_specs=[src_spec(k) for k in range(ns)] + [pl.BlockSpec((None, D, SEG), lambda i, j: (j // 2, 0, j % 2))],
        out_specs=pl.BlockSpec((tm, D), lambda i, j: (i, 0)),
        out_shape=jax.ShapeDtypeStruct((S, D), F32),
        compiler_params=_cp(("parallel", "arbitrary")),
    )(*sources, w_all)


def _rms_bwd(dh, x2, rinv, gain, dout):
    S, D = x2.shape
    tm = min(256, S)

    def body(dh_ref, x_ref, r_ref, g_ref, dout_ref, gx_ref, dg_ref):
        @pl.when(pl.program_id(0) == 0)
        def _():
            dg_ref[...] = jnp.zeros_like(dg_ref)

        dh = dh_ref[...]
        r = r_ref[...]
        xhat = x_ref[...] * r
        dg_ref[...] += jnp.sum(dh * xhat, axis=0, keepdims=True)
        dxn = dh * g_ref[...]
        gx_ref[...] = dout_ref[...] + r * (dxn - xhat * jnp.mean(dxn * xhat, axis=-1, keepdims=True))

    row = lambda i: (i, 0)
    fix = lambda i: (0, 0)
    return pl.pallas_call(
        body, grid=(S // tm,), name="rms_bwd",
        in_specs=[pl.BlockSpec((tm, D), row), pl.BlockSpec((tm, D), row), pl.BlockSpec((tm, 1), row),
                  pl.BlockSpec((1, D), fix), pl.BlockSpec((tm, D), row)],
        out_specs=[pl.BlockSpec((tm, D), row), pl.BlockSpec((1, D), fix)],
        out_shape=[jax.ShapeDtypeStruct((S, D), F32), jax.ShapeDtypeStruct((1, D), F32)],
        compiler_params=_cp(("arbitrary",)),
    )(dh, x2, rinv, gain, dout)


def _grad_w_in(h, sources):
    S, D = h.shape
    SEG = sources[0].shape[2]
    counts, starts = _dz_sources(sources)
    ns = len(sources)
    tk = min(512, S)

    def body(*refs):
        h_ref = refs[0]
        src = refs[1:1 + ns]
        o_ref = refs[1 + ns]
        j = pl.program_id(0)
        k = pl.program_id(1)

        @pl.when(k == 0)
        def _():
            o_ref[...] = jnp.zeros_like(o_ref)

        for s in range(ns):
            @pl.when((j >= starts[s]) & (j < starts[s] + counts[s]))
            def _(s=s):
                o_ref[...] += _dot_tn(h_ref[...], src[s][...])

    def src_spec(s):
        return pl.BlockSpec((None, tk, SEG),
                            lambda j, k: (jnp.clip(j - starts[s], 0, counts[s] - 1), k, 0))

    return pl.pallas_call(
        body, grid=(8, S // tk), name="grad_w_in",
        in_specs=[pl.BlockSpec((tk, D), lambda j, k: (k, 0))] + [src_spec(s) for s in range(ns)],
        out_specs=pl.BlockSpec((None, D, SEG), lambda j, k: (j // 2, 0, j % 2)),
        out_shape=jax.ShapeDtypeStruct((4, D, 2 * SEG), F32),
        compiler_params=_cp(("parallel", "arbitrary")),
    )(h, *sources)


def _lower_bound(lbl):
    l0 = lbl[0:1, :]
    l1 = lbl[1:2, :]
    m = jnp.maximum(l0, l1)
    e0 = jnp.exp(l0 - m)
    e1 = jnp.exp(l1 - m)
    return e0 / (e0 + e1)


def _chunk_masks():
    row = lax.broadcasted_iota(jnp.int32, (HGRN_CHUNK, HGRN_CHUNK), 0)
    col = lax.broadcasted_iota(jnp.int32, (HGRN_CHUNK, HGRN_CHUNK), 1)
    return row >= col, row <= col


def _hgrn_recompute(zq, zf, zi, lb, tril, tril_bf, st):
    q, dq_dz = _silu_and_grad(zq)
    sf = _sigmoid(zf)
    f = lb + (1.0 - lb) * sf
    k = 1.0 - f
    b = _exact_dot(tril_bf, jnp.log(f))
    b_last = b[HGRN_CHUNK - 1:HGRN_CHUNK, :]
    eb = jnp.exp(b)
    enb = jnp.exp(-b)
    ebl = jnp.exp(b_last)
    qd = q * eb
    kd = k * enb
    ekl = jnp.exp(b_last - b)
    ke = k * ekl
    qd_b = qd.astype(BF16)
    kd_b = kd.astype(BF16)
    ke_b = ke.astype(BF16)
    v_b = zi.astype(BF16)
    a = jnp.where(tril, _dot_nt(qd_b, kd_b), 0.0)
    o = _dot(a.astype(BF16), v_b) + _dot_nt(qd_b, st.astype(BF16))
    return dict(q=q, dq_dz=dq_dz, sf=sf, f=f, k=k, eb=eb, enb=enb, ebl=ebl, qd=qd, kd=kd, ke=ke,
                ekl=ekl,
                qd_b=qd_b, kd_b=kd_b, ke_b=ke_b, v_b=v_b, a=a, o=o)


def _hgrn_fwd(zf32, lb_logits, gnorm):
    _, S, SEG = zf32.shape
    H = SEG // HGRN_HEAD
    T = min(512, S)
    NC = T // HGRN_CHUNK
    NJ = S // T

    def body(zq_ref, zf_ref, zi_ref, zg_ref, lbl_ref, gn_ref, y_ref, st_ref, state):
        j = pl.program_id(1)

        @pl.when(j == 0)
        def _():
            state[...] = jnp.zeros_like(state)

        lb = _lower_bound(lbl_ref[...])
        gn = gn_ref[...]
        tril, _ = _chunk_masks()
        tril_bf = tril.astype(BF16)

        def chunk(c, carry):
            rows = pl.ds(pl.multiple_of(c * HGRN_CHUNK, HGRN_CHUNK), HGRN_CHUNK)
            st = state[...]
            st_ref[c] = st
            r = _hgrn_recompute(zq_ref[rows, :], zf_ref[rows, :], zi_ref[rows, :], lb, tril, tril_bf, st)
            state[...] = st * r["ebl"] + _dot_tn(r["v_b"], r["ke_b"])
            o = r["o"]
            on = o * lax.rsqrt(jnp.mean(o * o, axis=-1, keepdims=True) + NORM_EPS) * gn
            zg = zg_ref[rows, :]
            y_ref[rows, :] = (on * (zg * _sigmoid(zg))).astype(BF16)
            return carry

        lax.fori_loop(0, NC, chunk, 0)

    def zspec(seg):
        return pl.BlockSpec((None, T, HGRN_HEAD), lambda h, j: (seg, j, h))

    return pl.pallas_call(
        body, grid=(H, NJ), name="hgrn_fwd",
        in_specs=[zspec(0), zspec(1), zspec(2), zspec(3),
                  pl.BlockSpec((2, HGRN_HEAD), lambda h, j: (0, h)),
                  pl.BlockSpec((1, HGRN_HEAD), lambda h, j: (0, 0))],
        out_specs=[pl.BlockSpec((T, HGRN_HEAD), lambda h, j: (j, h)),
                   pl.BlockSpec((NC, None, HGRN_HEAD, HGRN_HEAD), lambda h, j: (j, h, 0, 0))],
        out_shape=[jax.ShapeDtypeStruct((S, SEG), BF16),
                   jax.ShapeDtypeStruct((S // HGRN_CHUNK, H, HGRN_HEAD, HGRN_HEAD), F32)],
        scratch_shapes=[pltpu.VMEM((HGRN_HEAD, HGRN_HEAD), F32)],
        compiler_params=_cp(("parallel", "arbitrary")),
    )(zf32, zf32, zf32, zf32, lb_logits, gnorm)


def _hgrn_bwd(zf32, lb_logits, gnorm, states, dy):
    _, S, SEG = zf32.shape
    H = SEG // HGRN_HEAD
    T = min(512, S)
    NC = T // HGRN_CHUNK
    NJ = S // T

    def body(zq_ref, zf_ref, zi_ref, zg_ref, lbl_ref, gn_ref, st_ref, dy_ref, dz_ref, dl_ref, dgn_ref, gstate):
        j = pl.program_id(1)

        @pl.when(j == 0)
        def _():
            gstate[...] = jnp.zeros_like(gstate)
            dl_ref[...] = jnp.zeros_like(dl_ref)
            dgn_ref[...] = jnp.zeros_like(dgn_ref)

        lb = _lower_bound(lbl_ref[...])
        gn = gn_ref[...]
        tril, triu = _chunk_masks()
        tril_bf = tril.astype(BF16)
        triu_bf = triu.astype(BF16)

        def chunk(ci, carry):
            dlb_acc, dgn_acc = carry
            c = NC - 1 - ci
            rows = pl.ds(pl.multiple_of(c * HGRN_CHUNK, HGRN_CHUNK), HGRN_CHUNK)
            st = st_ref[c]
            zf = zf_ref[rows, :]
            r = _hgrn_recompute(zq_ref[rows, :], zf, zi_ref[rows, :], lb, tril, tril_bf, st)
            g = gstate[...]
            g_b = g.astype(BF16)
            o = r["o"]
            rinv = lax.rsqrt(jnp.mean(o * o, axis=-1, keepdims=True) + NORM_EPS)
            ohat = o * rinv
            zg = zg_ref[rows, :]
            sg, dsg = _silu_and_grad(zg)
            dyv = dy_ref[rows, :]
            don = dyv * sg
            dzg = dyv * (ohat * gn) * dsg
            dgn_acc = dgn_acc + jnp.sum(don * ohat, axis=0, keepdims=True)
            dohat = don * gn
            do = rinv * (dohat - ohat * jnp.mean(dohat * ohat, axis=-1, keepdims=True))
            do_b = do.astype(BF16)
            da = jnp.where(tril, _dot_nt(do_b, r["v_b"]), 0.0)
            da_b = da.astype(BF16)
            dv = _dot_tn(r["a"].astype(BF16), do_b) + _dot_nt(r["ke_b"], g_b)
            dqd = _dot(da_b, r["kd_b"]) + _dot(do_b, st.astype(BF16))
            dkd = _dot_tn(da_b, r["qd_b"])
            dke = _dot(r["v_b"], g_b)
            debl = jnp.sum(g * st, axis=0, keepdims=True)
            gstate[...] = g * r["ebl"] + _dot_tn(do_b, r["qd_b"])
            dq = dqd * r["eb"]
            t_ke = dke * r["ke"]
            db = dqd * r["qd"] - dkd * r["kd"] - t_ke
            db_last = jnp.sum(t_ke, axis=0, keepdims=True) + debl * r["ebl"]
            dk = dkd * r["enb"] + dke * r["ekl"]
            dlogf = _exact_dot(triu_bf, db) + db_last
            df = dlogf / r["f"] - dk
            sf = r["sf"]
            dzf = df * (1.0 - lb) * (sf * (1.0 - sf))
            dlb_acc = dlb_acc + jnp.sum(df * (1.0 - sf), axis=0, keepdims=True)
            dz_ref[0, rows, :] = (dq * r["dq_dz"]).astype(BF16)
            dz_ref[1, rows, :] = dzf.astype(BF16)
            dz_ref[2, rows, :] = dv.astype(BF16)
            dz_ref[3, rows, :] = dzg.astype(BF16)
            return dlb_acc, dgn_acc

        zero = jnp.zeros((1, HGRN_HEAD), F32)
        dlb, dgn = lax.fori_loop(0, NC, chunk, (zero, zero))
        dl0 = dlb * lb * (1.0 - lb)
        dl_ref[0:1, :] += dl0
        dl_ref[1:2, :] -= dl0
        dgn_ref[...] += dgn

    def zspec(seg):
        return pl.BlockSpec((None, T, HGRN_HEAD), lambda h, j: (seg, NJ - 1 - j, h))

    return pl.pallas_call(
        body, grid=(H, NJ), name="hgrn_bwd",
        in_specs=[zspec(0), zspec(1), zspec(2), zspec(3),
                  pl.BlockSpec((2, HGRN_HEAD), lambda h, j: (0, h)),
                  pl.BlockSpec((1, HGRN_HEAD), lambda h, j: (0, 0)),
                  pl.BlockSpec((NC, None, HGRN_HEAD, HGRN_HEAD), lambda h, j: (NJ - 1 - j, h, 0, 0)),
                  pl.BlockSpec((T, HGRN_HEAD), lambda h, j: (NJ - 1 - j, h))],
        out_specs=[pl.BlockSpec((4, T, HGRN_HEAD), lambda h, j: (0, NJ - 1 - j, h)),
                   pl.BlockSpec((2, HGRN_HEAD), lambda h, j: (0, h)),
                   pl.BlockSpec((None, 1, HGRN_HEAD), lambda h, j: (h, 0, 0))],
        out_shape=[jax.ShapeDtypeStruct((4, S, SEG), BF16), jax.ShapeDtypeStruct((2, SEG), F32),
                   jax.ShapeDtypeStruct((H, 1, HGRN_HEAD), F32)],
        scratch_shapes=[pltpu.VMEM((HGRN_HEAD, HGRN_HEAD), F32)],
        compiler_params=_cp(("parallel", "arbitrary")),
    )(zf32, zf32, zf32, zf32, lb_logits, gnorm, states, dy)


def _alibi_slope(h, n_heads):
    return float(2.0 ** (-8.0 * (h + 1) / n_heads))


def _band_terms(n, d):
    i = lax.broadcasted_iota(jnp.int32, (BAND, 2 * BAND), 0)
    jj = lax.broadcasted_iota(jnp.int32, (BAND, 2 * BAND), 1)
    delta = BAND + i - jj
    valid = (delta >= 0) & (delta <= BAND) & ((n > 0) | (jj >= BAND))
    return (-d * delta).astype(F32), valid


def _head_halves(x):
    lane = lax.broadcasted_iota(jnp.int32, x.shape, 1)
    zero = jnp.zeros_like(x)
    return jnp.where(lane < ATTN_HEAD, x, zero), jnp.where(lane < ATTN_HEAD, zero, x)


def _per_head(x):
    lane = lax.broadcasted_iota(jnp.int32, x.shape, 1)
    sw = pltpu.roll(x, ATTN_HEAD, 1)
    first = lane < ATTN_HEAD
    return jnp.where(first, x, sw), jnp.where(first, sw, x)


def _attn_fwd(qkv, d):
    _, S, SEG = qkv.shape
    L = S // d
    assert L % BAND == 0
    nb = L // BAND
    NP = SEG // LANES
    n_heads = 2 * NP
    view = qkv.reshape(3, L, d * SEG)

    def body(q_ref, kp_ref, kc_ref, vp_ref, vc_ref, o_ref, l_ref):
        n = pl.program_id(1)
        nd, valid = _band_terms(n, d)
        lane = lax.broadcasted_iota(jnp.int32, (BAND, LANES), 1)
        first = lane < ATTN_HEAD
        for p in range(NP):
            cols = pl.ds(p * LANES, LANES)
            kc = jnp.concatenate([kp_ref[:, cols], kc_ref[:, cols]], axis=0)
            vc = jnp.concatenate([vp_ref[:, cols], vc_ref[:, cols]], axis=0)
            outs, lses = [], []
            for t, qh in enumerate(_head_halves(q_ref[:, cols])):
                s = _dot_nt(qh, kc) * ATTN_SCALE + _alibi_slope(2 * p + t, n_heads) * nd
                s = jnp.where(valid, s, NEG)
                m = jnp.max(s, axis=-1, keepdims=True)
                e = jnp.exp(s - m)
                den = jnp.sum(e, axis=-1, keepdims=True)
                outs.append(_dot(e.astype(BF16), vc) / den)
                lses.append(m + jnp.log(den))
            o_ref[:, cols] = jnp.where(first, outs[0], outs[1])
            l_ref[:, cols] = jnp.where(first, jnp.broadcast_to(lses[0], (BAND, LANES)),
                                       jnp.broadcast_to(lses[1], (BAND, LANES)))

    def spec(seg, prev):
        if prev:
            return pl.BlockSpec((None, BAND, SEG), lambda r, n: (seg, jnp.maximum(n - 1, 0), r))
        return pl.BlockSpec((None, BAND, SEG), lambda r, n: (seg, n, r))

    o, l = pl.pallas_call(
        body, grid=(d, nb), name=f"attn_fwd_d{d}",
        in_specs=[spec(0, False), spec(1, True), spec(1, False), spec(2, True), spec(2, False)],
        out_specs=[pl.BlockSpec((BAND, SEG), lambda r, n: (n, r)), pl.BlockSpec((BAND, SEG), lambda r, n: (n, r))],
        out_shape=[jax.ShapeDtypeStruct((L, d * SEG), F32), jax.ShapeDtypeStruct((L, d * SEG), F32)],
        compiler_params=_cp(("parallel", "parallel")),
    )(view, view, view, view, view)
    return o.reshape(S, SEG), l.reshape(S, SEG)


def _attn_merge(outs, lses, zf32):
    S, SEG = outs[0].shape
    tm = min(256, S)

    def body(o1, o2, o3, l1, l2, l3, zg_ref, o_ref, lse_ref, y_ref):
        a, b, c = l1[...], l2[...], l3[...]
        m = jnp.maximum(jnp.maximum(a, b), c)
        ea, eb, ec = jnp.exp(a - m), jnp.exp(b - m), jnp.exp(c - m)
        tot = ea + eb + ec
        o = (ea / tot) * o1[...] + (eb / tot) * o2[...] + (ec / tot) * o3[...]
        o_ref[...] = o
        lse_ref[...] = m + jnp.log(tot)
        zg = zg_ref[...]
        y_ref[...] = (o * (zg * _sigmoid(zg))).astype(BF16)

    row = pl.BlockSpec((tm, SEG), lambda i: (i, 0))
    return pl.pallas_call(
        body, grid=(S // tm,), name="attn_merge",
        in_specs=[row] * 6 + [pl.BlockSpec((None, tm, SEG), lambda i: (4, i, 0))],
        out_specs=[row, row, row],
        out_shape=[jax.ShapeDtypeStruct((S, SEG), F32), jax.ShapeDtypeStruct((S, SEG), F32),
                   jax.ShapeDtypeStruct((S, SEG), BF16)],
        compiler_params=_cp(("parallel",)),
    )(*outs, *lses, zf32)


def _attn_gate_bwd(dy, o, zf32):
    S, SEG = o.shape
    tm = min(256, S)
    NP = SEG // LANES

    def body(dy_ref, o_ref, zg_ref, do_ref, dl_ref, dzg_ref):
        r = lax.broadcasted_iota(jnp.int32, (LANES, LANES), 0) // ATTN_HEAD
        c = lax.broadcasted_iota(jnp.int32, (LANES, LANES), 1) // ATTN_HEAD
        same_head = (r == c).astype(BF16)
        for p in range(NP):
            cols = pl.ds(p * LANES, LANES)
            sg, dsg = _silu_and_grad(zg_ref[:, cols])
            dyv = dy_ref[:, cols]
            ov = o_ref[:, cols]
            do = dyv * sg
            do_ref[:, cols] = do.astype(BF16)
            dzg_ref[:, cols] = (dyv * ov * dsg).astype(BF16)
            dl_ref[:, cols] = _exact_dot_right(do * ov, same_head)

    return pl.pallas_call(
        body, grid=(S // tm,), name="attn_gate_bwd",
        in_specs=[pl.BlockSpec((tm, SEG), lambda i: (i, 1)), pl.BlockSpec((tm, SEG), lambda i: (i, 0)),
                  pl.BlockSpec((None, tm, SEG), lambda i: (4, i, 0))],
        out_specs=[pl.BlockSpec((tm, SEG), lambda i: (i, 0)), pl.BlockSpec((tm, SEG), lambda i: (i, 0)),
                   pl.BlockSpec((None, tm, SEG), lambda i: (0, i, 0))],
        out_shape=[jax.ShapeDtypeStruct((S, SEG), BF16), jax.ShapeDtypeStruct((S, SEG), F32),
                   jax.ShapeDtypeStruct((1, S, SEG), BF16)],
        compiler_params=_cp(("parallel",)),
    )(dy, o, zf32)


def _exact_dot_right(x, t_bf16):
    hi = x.astype(BF16)
    r1 = x - hi.astype(F32)
    mid = r1.astype(BF16)
    lo = (r1 - mid.astype(F32)).astype(BF16)
    return _dot(hi, t_bf16) + _dot(mid, t_bf16) + _dot(lo, t_bf16)


def _attn_bwd(qkv, do, lse, dl, d, acc, out_dtype):
    _, S, SEG = qkv.shape
    L = S // d
    nb = L // BAND
    NP = SEG // LANES
    n_heads = 2 * NP
    W = d * SEG
    view = qkv.reshape(3, L, W)
    has_acc = acc is not None

    def body(*refs):
        q_ref, kp_ref, kc_ref, vp_ref, vc_ref, do_ref, lse_ref, dl_ref = refs[:8]
        refs = refs[8:]
        if has_acc:
            aq_ref, ak_ref, av_ref = refs[:3]
            refs = refs[3:]
        dq_ref, dk_ref, dv_ref, ck, cv = refs
        n = pl.program_id(1)

        @pl.when(n == 0)
        def _():
            ck[...] = jnp.zeros_like(ck)
            cv[...] = jnp.zeros_like(cv)

        @pl.when(n < nb)
        def _():
            nd, valid = _band_terms(n, d)
            lane = lax.broadcasted_iota(jnp.int32, (BAND, LANES), 1)
            first = lane < ATTN_HEAD
            for p in range(NP):
                cols = pl.ds(p * LANES, LANES)
                kc = jnp.concatenate([kp_ref[:, cols], kc_ref[:, cols]], axis=0)
                vc = jnp.concatenate([vp_ref[:, cols], vc_ref[:, cols]], axis=0)
                q_halves = _head_halves(q_ref[:, cols])
                do_halves = _head_halves(do_ref[:, cols])
                lse_heads = _per_head(lse_ref[:, cols])
                dl_heads = _per_head(dl_ref[:, cols])
                dqs = []
                dk = jnp.zeros((2 * BAND, LANES), F32)
                dv = jnp.zeros((2 * BAND, LANES), F32)
                for t in range(2):
                    s = _dot_nt(q_halves[t], kc) * ATTN_SCALE + _alibi_slope(2 * p + t, n_heads) * nd
                    lse2 = jnp.concatenate([lse_heads[t], lse_heads[t]], axis=1)
                    dl2 = jnp.concatenate([dl_heads[t], dl_heads[t]], axis=1)
                    pr = jnp.where(valid, jnp.exp(s - lse2), 0.0)
                    dp = _dot_nt(do_halves[t], vc)
                    ds = (pr * (dp - dl2) * ATTN_SCALE).astype(BF16)
                    dqs.append(_dot(ds, kc))
                    dk = dk + _dot_tn(ds, q_halves[t])
                    dv = dv + _dot_tn(pr.astype(BF16), do_halves[t])
                dq = jnp.where(first, dqs[0], dqs[1])
                if has_acc:
                    dq = dq + aq_ref[:, cols]
                dq_ref[:, cols] = dq.astype(out_dtype)
                dk_prev = ck[:, cols] + dk[:BAND, :]
                dv_prev = cv[:, cols] + dv[:BAND, :]
                if has_acc:
                    dk_prev = dk_prev + ak_ref[:, cols]
                    dv_prev = dv_prev + av_ref[:, cols]
                dk_ref[:, cols] = dk_prev.astype(out_dtype)
                dv_ref[:, cols] = dv_prev.astype(out_dtype)
                ck[:, cols] = dk[BAND:, :]
                cv[:, cols] = dv[BAND:, :]

        @pl.when(n == nb)
        def _():
            dk_last = ck[...]
            dv_last = cv[...]
            if has_acc:
                dk_last = dk_last + ak_ref[...]
                dv_last = dv_last + av_ref[...]
            dk_ref[...] = dk_last.astype(out_dtype)
            dv_ref[...] = dv_last.astype(out_dtype)

    cur = lambda r, n: (jnp.minimum(n, nb - 1), r)
    lag = lambda r, n: (jnp.maximum(n - 1, 0), r)

    def spec(seg, prev):
        if prev:
            return pl.BlockSpec((None, BAND, SEG), lambda r, n: (seg, jnp.clip(n - 1, 0, nb - 1), r))
        return pl.BlockSpec((None, BAND, SEG), lambda r, n: (seg, jnp.minimum(n, nb - 1), r))

    in_specs = [spec(0, False), spec(1, True), spec(1, False), spec(2, True), spec(2, False),
                pl.BlockSpec((BAND, SEG), cur), pl.BlockSpec((BAND, SEG), cur), pl.BlockSpec((BAND, SEG), cur)]
    args = [view, view, view, view, view, do.reshape(L, W), lse.reshape(L, W), dl.reshape(L, W)]
    aliases = {}
    if has_acc:
        in_specs += [pl.BlockSpec((BAND, SEG), cur), pl.BlockSpec((BAND, SEG), lag), pl.BlockSpec((BAND, SEG), lag)]
        args += [a.reshape(L, W) for a in acc]
        if out_dtype == F32:
            aliases = {8: 0, 9: 1, 10: 2}
    outs = pl.pallas_call(
        body, grid=(d, nb + 1), name=f"attn_bwd_d{d}",
        in_specs=in_specs,
        out_specs=[pl.BlockSpec((BAND, SEG), cur), pl.BlockSpec((BAND, SEG), lag), pl.BlockSpec((BAND, SEG), lag)],
        out_shape=[jax.ShapeDtypeStruct((L, W), out_dtype)] * 3,
        scratch_shapes=[pltpu.VMEM((BAND, SEG), F32), pltpu.VMEM((BAND, SEG), F32)],
        input_output_aliases=aliases,
        compiler_params=_cp(("parallel", "arbitrary")),
    )(*args)
    return [a.reshape(S, SEG) for a in outs]


def _adamw(w, g, m, v, name):
    R, C = w.shape
    tr = R if R <= 256 else 256
    assert R % tr == 0

    def body(w_ref, g_ref, m_ref, v_ref, d_ref, nm_ref, nv_ref):
        g = g_ref[...]
        nm = ADAM_B1 * m_ref[...] + (1.0 - ADAM_B1) * g
        nv = ADAM_B2 * v_ref[...] + (1.0 - ADAM_B2) * (g * g)
        m_hat = nm / (1.0 - ADAM_B1 ** ADAM_STEP)
        v_hat = nv / (1.0 - ADAM_B2 ** ADAM_STEP)
        d_ref[...] = -ADAM_LR * (m_hat / (jnp.sqrt(v_hat) + ADAM_EPS) + ADAM_WD * w_ref[...])
        nm_ref[...] = nm
        nv_ref[...] = nv

    blk = pl.BlockSpec((tr, C), lambda i: (i, 0))
    sds = jax.ShapeDtypeStruct((R, C), F32)
    return pl.pallas_call(
        body, grid=(R // tr,), name=name, in_specs=[blk] * 4, out_specs=[blk] * 3, out_shape=[sds] * 3,
        compiler_params=_cp(("parallel",)),
    )(w, g, m, v)


def _coords():
    return lax.axis_index("x"), lax.axis_index("y"), lax.axis_index("c")


def _other_chips(x, y):
    return [(1 - x, y), (x, 1 - y), (1 - x, 1 - y)]


ANY = pl.BlockSpec(memory_space=pl.ANY)


def _gather_weights(wi, wo):
    nrows = (wi.shape[0] // 2, wo.shape[0] // 2)

    def body(wi_ref, wo_ref, wia_ref, woa_ref, send_sems, recv_sems, local_sems):
        x, y, c = _coords()
        me = 2 * x + y
        srcs = (wi_ref, wo_ref)
        alls = (wia_ref, woa_ref)

        def half(a, chip, hc):
            return alls[a].at[chip, pl.ds(hc * nrows[a], nrows[a]), :]

        def remote(a, k, src, dst, to):
            return pltpu.make_async_remote_copy(src_ref=src, dst_ref=dst, send_sem=send_sems.at[2 * k + a],
                                                recv_sem=recv_sems.at[2 * k + a], device_id=to, device_id_type=MESH)

        local = [pltpu.make_async_copy(srcs[a], alls[a].at[me], local_sems.at[a]) for a in range(2)]
        for cp in local:
            cp.start()
        started = []
        for k, (px, py) in enumerate(_other_chips(x, y)):
            for a in range(2):
                cp = remote(a, k, srcs[a].at[pl.ds(c * nrows[a], nrows[a]), :], half(a, me, c), (px, py, c))
                cp.start()
                started.append(cp)
        for k, (px, py) in enumerate(_other_chips(x, y)):
            for a in range(2):
                got = half(a, 2 * px + py, c)
                remote(a, k, got, got, (px, py, c)).wait_recv()
                cp = remote(a, 3 + k, got, got, (x, y, 1 - c))
                cp.start()
                started.append(cp)
        for k, (px, py) in enumerate(_other_chips(x, y)):
            for a in range(2):
                got = half(a, 2 * px + py, 1 - c)
                remote(a, 3 + k, got, got, (x, y, 1 - c)).wait_recv()
        for cp in started:
            cp.wait_send()
        for cp in local:
            cp.wait()

    return pl.pallas_call(
        body, name="gather_weights", in_specs=[ANY, ANY], out_specs=[ANY, ANY],
        out_shape=[jax.ShapeDtypeStruct((4,) + wi.shape, wi.dtype), jax.ShapeDtypeStruct((4,) + wo.shape, wo.dtype)],
        scratch_shapes=[pltpu.SemaphoreType.DMA((12,)), pltpu.SemaphoreType.DMA((12,)), pltpu.SemaphoreType.DMA((2,))],
    )(wi, wo)


def _swap_halves(gi, go):
    n_i, n_o = gi.shape[1] // 2, go.shape[1] // 2

    def body(gi_ref, go_ref, si_ref, so_ref, send_sems, recv_sems):
        x, y, c = _coords()
        cps = []
        for a, (src, dst, nr) in enumerate(((gi_ref, si_ref, n_i), (go_ref, so_ref, n_o))):
            cp = pltpu.make_async_remote_copy(
                src_ref=src.at[:, pl.ds((1 - c) * nr, nr), :], dst_ref=dst, send_sem=send_sems.at[a],
                recv_sem=recv_sems.at[a], device_id=(x, y, 1 - c), device_id_type=MESH)
            cp.start()
            cps.append(cp)
        for cp in cps:
            cp.wait()

    return pl.pallas_call(
        body, name="swap_halves", in_specs=[ANY, ANY], out_specs=[ANY, ANY],
        out_shape=[jax.ShapeDtypeStruct((4, n_i, gi.shape[2]), F32), jax.ShapeDtypeStruct((4, n_o, go.shape[2]), F32)],
        scratch_shapes=[pltpu.SemaphoreType.DMA((2,)), pltpu.SemaphoreType.DMA((2,))],
    )(gi, go)


def _pair_sum(g, sib, where, name):
    _, n2, C = g.shape
    N = n2 // 2
    tr = min(256, N)
    nt = N // tr

    def body(where_ref, g_ref, s_ref, qb_ref, own_ref):
        q = pl.program_id(1)
        tot = g_ref[...] + s_ref[...]
        qb_ref[...] = tot.astype(BF16)

        @pl.when(q == where_ref[1])
        def _():
            own_ref[...] = tot

    grid_spec = pltpu.PrefetchScalarGridSpec(
        num_scalar_prefetch=1, grid=(nt, 4),
        in_specs=[pl.BlockSpec((None, tr, C), lambda i, q, w: (q, w[0] * nt + i, 0)),
                  pl.BlockSpec((None, tr, C), lambda i, q, w: (q, i, 0))],
        out_specs=[pl.BlockSpec((None, tr, C), lambda i, q, w: (q, i, 0)),
                   pl.BlockSpec((tr, C), lambda i, q, w: (i, 0))])
    return pl.pallas_call(
        body, grid_spec=grid_spec, name=name,
        out_shape=[jax.ShapeDtypeStruct((4, N, C), BF16), jax.ShapeDtypeStruct((N, C), F32)],
        compiler_params=_cp(("parallel", "arbitrary")),
    )(where, g, sib)


def _scatter_to_chips(qi, qo):
    def body(qi_ref, qo_ref, ri_ref, ro_ref, send_sems, recv_sems):
        x, y, c = _coords()
        cps = []
        for k, (px, py) in enumerate(_other_chips(x, y)):
            for a, (src, dst) in enumerate(((qi_ref, ri_ref), (qo_ref, ro_ref))):
                cp = pltpu.make_async_remote_copy(
                    src_ref=src.at[2 * px + py], dst_ref=dst.at[k], send_sem=send_sems.at[2 * k + a],
                    recv_sem=recv_sems.at[2 * k + a], device_id=(px, py, c), device_id_type=MESH)
                cp.start()
                cps.append(cp)
        for cp in cps:
            cp.wait()

    return pl.pallas_call(
        body, name="scatter_to_chips", in_specs=[ANY, ANY], out_specs=[ANY, ANY],
        out_shape=[jax.ShapeDtypeStruct((3,) + qi.shape[1:], BF16), jax.ShapeDtypeStruct((3,) + qo.shape[1:], BF16)],
        scratch_shapes=[pltpu.SemaphoreType.DMA((6,)), pltpu.SemaphoreType.DMA((6,))],
    )(qi, qo)


def _chip_sum(own, got, name):
    N, C = own.shape
    tr = min(256, N)

    def body(own_ref, got_ref, o_ref):
        t = own_ref[...]
        for k in range(3):
            t = t + got_ref[k].astype(F32)
        o_ref[...] = t

    return pl.pallas_call(
        body, grid=(N // tr,), name=name,
        in_specs=[pl.BlockSpec((tr, C), lambda i: (i, 0)), pl.BlockSpec((3, tr, C), lambda i: (0, i, 0))],
        out_specs=pl.BlockSpec((tr, C), lambda i: (i, 0)),
        out_shape=jax.ShapeDtypeStruct((N, C), F32),
        compiler_params=_cp(("parallel",)),
    )(own, got)


def _join_halves(ri, ro):
    def body(ri_ref, ro_ref, gi_ref, go_ref, send_sems, recv_sems, local_sems):
        x, y, c = _coords()
        cps, loc = [], []
        for a, (src, dst) in enumerate(((ri_ref, gi_ref), (ro_ref, go_ref))):
            nr = src.shape[0]
            mine = dst.at[pl.ds(c * nr, nr), :]
            lc = pltpu.make_async_copy(src, mine, local_sems.at[a])
            lc.start()
            loc.append(lc)
            cp = pltpu.make_async_remote_copy(src_ref=src, dst_ref=mine, send_sem=send_sems.at[a],
                                              recv_sem=recv_sems.at[a], device_id=(x, y, 1 - c), device_id_type=MESH)
            cp.start()
            cps.append(cp)
        for a, (src, dst) in enumerate(((ri_ref, gi_ref), (ro_ref, go_ref))):
            nr = src.shape[0]
            theirs = dst.at[pl.ds((1 - c) * nr, nr), :]
            pltpu.make_async_remote_copy(src_ref=src, dst_ref=theirs, send_sem=send_sems.at[a],
                                         recv_sem=recv_sems.at[a], device_id=(x, y, 1 - c),
                                         device_id_type=MESH).wait_recv()
        for cp in cps:
            cp.wait_send()
        for lc in loc:
            lc.wait()

    return pl.pallas_call(
        body, name="join_halves", in_specs=[ANY, ANY], out_specs=[ANY, ANY],
        out_shape=[jax.ShapeDtypeStruct((2 * ri.shape[0], ri.shape[1]), F32),
                   jax.ShapeDtypeStruct((2 * ro.shape[0], ro.shape[1]), F32)],
        scratch_shapes=[pltpu.SemaphoreType.DMA((2,)), pltpu.SemaphoreType.DMA((2,)), pltpu.SemaphoreType.DMA((2,))],
    )(ri, ro)


def _all_reduce_small(part):
    R, C = part.shape

    def body(p_ref, o_ref, slots, send_sems, recv_sems):
        x, y, c = _coords()
        me = 4 * x + 2 * y + c
        slots[me] = p_ref[...]
        cps = []
        for k in range(1, 8):
            fx, fy, fc = (k >> 2) & 1, (k >> 1) & 1, k & 1
            peer = (1 - x if fx else x, 1 - y if fy else y, 1 - c if fc else c)
            cp = pltpu.make_async_remote_copy(src_ref=p_ref, dst_ref=slots.at[me], send_sem=send_sems.at[k - 1],
                                              recv_sem=recv_sems.at[k - 1], device_id=peer, device_id_type=MESH)
            cp.start()
            cps.append(cp)
        for cp in cps:
            cp.wait()
        t = slots[0]
        for k in range(1, 8):
            t = t + slots[k]
        o_ref[...] = t

    vm = pl.BlockSpec(memory_space=pltpu.VMEM)
    return pl.pallas_call(
        body, name="all_reduce_small", in_specs=[vm], out_specs=vm,
        out_shape=jax.ShapeDtypeStruct((R, C), F32),
        scratch_shapes=[pltpu.VMEM((8, R, C), F32), pltpu.SemaphoreType.DMA((7,)), pltpu.SemaphoreType.DMA((7,))],
    )(part)


def _local_step(x2, tgt, norm_gain, w_all, lb_logits, hgrn_gnorm, w_out_all, fgain):
    S, D = x2.shape
    h, rinv = _rms_fwd(x2, norm_gain)
    zf32 = _in_proj(h, w_all, (0, 1, 2, 3, 7), F32, "in_proj_f32")
    qkv = _in_proj(h, w_all, (4, 5, 6), BF16, "in_proj_qkv")
    yh, states = _hgrn_fwd(zf32, lb_logits, hgrn_gnorm)
    outs, lses = [], []
    for d in DILATIONS:
        o, l = _attn_fwd(qkv, d)
        outs.append(o)
        lses.append(l)
    o_attn, lse, ya = _attn_merge(outs, lses, zf32)
    dout, doutb, loss, dfg = _out_proj_loss(yh, ya, w_out_all, x2, tgt, fgain)
    dy = _dy_proj(doutb, w_out_all)
    g_w_out = _grad_w_out(yh, ya, doutb)
    dzh, dlogits, dgn = _hgrn_bwd(zf32, lb_logits, hgrn_gnorm, states, dy)
    do, dl, dzg = _attn_gate_bwd(dy, o_attn, zf32)
    acc = None
    for k, d in enumerate(DILATIONS):
        last = k == len(DILATIONS) - 1
        acc = _attn_bwd(qkv, do, lse, dl, d, acc, BF16 if last else F32)
    sources = [dzh] + [a[None] for a in acc] + [dzg]
    grad_x, dgain = _rms_bwd(_dh_proj(sources, w_all), x2, rinv, norm_gain, dout)
    g_w_in = _grad_w_in(h, sources)
    return loss, grad_x, dgain, g_w_in, dlogits, dgn, g_w_out, dfg


def _pack_small(D, loss, dgain, dlogits, dgn, dfg):
    def row(v):
        v = v.reshape(1, -1)
        return jnp.pad(v, ((0, 0), (0, D - v.shape[1])))
    rows = [row(dgain), row(dfg), row(dlogits[0]), row(dlogits[1]), row(jnp.sum(dgn, axis=0)), row(loss)]
    rows += [jnp.zeros((1, D), F32)] * (8 - len(rows))
    return jnp.concatenate(rows, axis=0)


def kernel(x, norm_gain, w_in, lb_logits, hgrn_gnorm, w_out, final_gain, loss_target, m_norm_gain, m_w_in, m_lb_logits, m_hgrn_gnorm, m_w_out, m_final_gain, v_norm_gain, v_w_in, v_lb_logits, v_hgrn_gnorm, v_w_out, v_final_gain):
    _, S, D = x.shape
    SEG = w_in.shape[2] // 2
    x2 = x[0]
    tgt = loss_target[0]
    fgain = final_gain.reshape(1, D)

    w_all, w_out_all = _gather_weights(w_in[0].astype(BF16), w_out[0].astype(BF16))
    w_out_all = w_out_all.reshape(2 * SEG, D)

    loss, grad_x, dgain, g_w_in, dlogits, dgn, g_w_out, dfg = _local_step(
        x2, tgt, norm_gain, w_all, lb_logits, hgrn_gnorm, w_out_all, fgain)

    where = jnp.stack([lax.axis_index("c"), 2 * lax.axis_index("x") + lax.axis_index("y")]).astype(jnp.int32)
    sib_i, sib_o = _swap_halves(g_w_in, g_w_out)
    qi, own_i = _pair_sum(g_w_in, sib_i, where, "pair_sum_w_in")
    qo, own_o = _pair_sum(g_w_out, sib_o, where, "pair_sum_w_out")
    got_i, got_o = _scatter_to_chips(qi, qo)
    half_i = _chip_sum(own_i, got_i, "chip_sum_w_in")
    half_o = _chip_sum(own_o, got_o, "chip_sum_w_out")
    grad_w_in, grad_w_out = _join_halves(half_i, half_o)

    small = _all_reduce_small(_pack_small(D, loss, dgain, dlogits, dgn, dfg))
    grad_norm_gain = small[0:1, :]
    grad_final_gain = small[1:2, :]
    grad_lb_logits = small[2:4, :SEG]
    grad_hgrn_gnorm = small[4:5, :HGRN_HEAD]
    loss_sum = small[5, 0]

    d_ng, m_ng, v_ng = _adamw(norm_gain, grad_norm_gain, m_norm_gain, v_norm_gain, "adamw_norm_gain")
    d_wi, m_wi, v_wi = _adamw(w_in[0], grad_w_in, m_w_in[0], v_w_in[0], "adamw_w_in")
    d_lb, m_lb, v_lb = _adamw(lb_logits, grad_lb_logits, m_lb_logits, v_lb_logits, "adamw_lb_logits")
    d_gn, m_gn, v_gn = _adamw(hgrn_gnorm, grad_hgrn_gnorm, m_hgrn_gnorm, v_hgrn_gnorm, "adamw_hgrn_gnorm")
    d_wo, m_wo, v_wo = _adamw(w_out[0], grad_w_out, m_w_out[0], v_w_out[0], "adamw_w_out")
    d_fg, m_fg, v_fg = _adamw(fgain, grad_final_gain, m_final_gain.reshape(1, D), v_final_gain.reshape(1, D),
                              "adamw_final_gain")

    return (loss_sum, grad_x[None],
            grad_norm_gain, grad_w_in[None], grad_lb_logits, grad_hgrn_gnorm, grad_w_out[None], grad_final_gain[0],
            d_ng, d_wi[None], d_lb, d_gn, d_wo[None], d_fg[0],
            m_ng, m_wi[None], m_lb, m_gn, m_wo[None], m_fg[0],
            v_ng, v_wi[None], v_lb, v_gn, v_wo[None], v_fg[0])
```

```python
import jax
import jax.numpy as jnp
import numpy as np
from jax import lax
from jax.experimental import pallas as pl
from jax.experimental.pallas import tpu as pltpu

F32 = jnp.float32
BF16 = jnp.bfloat16
MESH = pl.DeviceIdType.MESH

NORM_EPS = 1e-6
HGRN_HEAD = 128
HGRN_CHUNK = 64
HGRN_TILE = 128
HGRN_BLOCK = 512
ATTN_HEAD = 64
LANES = 128
BAND = 128
DILATIONS = (1, 4, 16)
ATTN_SCALE = ATTN_HEAD ** -0.5
assert ATTN_SCALE == 0.125
ATTN_BLOCK_ELEMS = BAND * 2048
NEG = -1e30

ADAM_LR = 0.001
ADAM_B1 = 0.9
ADAM_B2 = 0.999
ADAM_EPS = 1e-08
ADAM_WD = 0.01
ADAM_STEP = 10

MIB = 1024 * 1024


def _cp(semantics=None, vmem_mib=48):
    return pltpu.CompilerParams(dimension_semantics=semantics, vmem_limit_bytes=vmem_mib * MIB)


def _dot(a, b):
    return jnp.dot(a, b, preferred_element_type=F32)


def _dot_nt(a, b):
    return lax.dot_general(a, b, (((1,), (1,)), ((), ())), preferred_element_type=F32)


def _dot_tn(a, b):
    return lax.dot_general(a, b, (((0,), (0,)), ((), ())), preferred_element_type=F32)


def _split3(x):
    hi = x.astype(BF16)
    r1 = x - hi.astype(F32)
    mid = r1.astype(BF16)
    lo = (r1 - mid.astype(F32)).astype(BF16)
    return hi, mid, lo


def _exact_dot(t_bf16, x):
    hi, mid, lo = _split3(x)
    return _dot(t_bf16, hi) + _dot(t_bf16, mid) + _dot(t_bf16, lo)


def _exact_dot_right(x, t_bf16):
    hi, mid, lo = _split3(x)
    return _dot(hi, t_bf16) + _dot(mid, t_bf16) + _dot(lo, t_bf16)


def _sigmoid(z):
    return jax.nn.sigmoid(z)


def _silu_and_grad(z):
    s = _sigmoid(z)
    return z * s, s * (1.0 + z * (1.0 - s))


def _seg_select(j, values):
    out = values[0]
    for t, v in enumerate(values[1:], 1):
        out = jnp.where(j == t, v, out)
    return out


def _rms_fwd(x2, gain):
    S, D = x2.shape
    tm = min(512, S)

    def body(x_ref, g_ref, h_ref, r_ref):
        x = x_ref[...]
        r = lax.rsqrt(jnp.mean(x * x, axis=-1, keepdims=True) + NORM_EPS)
        h_ref[...] = ((x * r) * g_ref[...]).astype(BF16)
        r_ref[...] = r

    return pl.pallas_call(
        body, grid=(S // tm,), name="rms_fwd",
        in_specs=[pl.BlockSpec((tm, D), lambda i: (i, 0)), pl.BlockSpec((1, D), lambda i: (0, 0))],
        out_specs=[pl.BlockSpec((tm, D), lambda i: (i, 0)), pl.BlockSpec((tm, 1), lambda i: (i, 0))],
        out_shape=[jax.ShapeDtypeStruct((S, D), BF16), jax.ShapeDtypeStruct((S, 1), F32)],
        compiler_params=_cp(("parallel",)),
    )(x2, gain)


def _in_proj(h, w_all, segs, name):
    S, D = h.shape
    SEG = w_all.shape[2] // 2
    n = len(segs)
    tm = min(512, S)

    def body(h_ref, w_ref, o_ref):
        o_ref[...] = _dot(h_ref[...], w_ref[...])

    def w_map(j, i):
        seg = _seg_select(j, segs)
        return (seg // 2, 0, seg % 2)

    return pl.pallas_call(
        body, grid=(n, S // tm), name=name,
        in_specs=[pl.BlockSpec((tm, D), lambda j, i: (i, 0)), pl.BlockSpec((None, D, SEG), w_map)],
        out_specs=pl.BlockSpec((None, tm, SEG), lambda j, i: (j, i, 0)),
        out_shape=jax.ShapeDtypeStruct((n, S, SEG), F32),
        compiler_params=_cp(("parallel", "parallel")),
    )(h, w_all)


def _out_proj_loss(yh, ya, w_out, x2, tgt, fgain):
    S, D = x2.shape
    SEG = yh.shape[1]
    tm = min(256, S)

    def body(yh_ref, ya_ref, w_ref, x_ref, t_ref, fg_ref, dout_ref, doutb_ref, loss_ref, dfg_ref):
        i = pl.program_id(0)

        @pl.when(i == 0)
        def _():
            loss_ref[...] = jnp.zeros_like(loss_ref)
            dfg_ref[...] = jnp.zeros_like(dfg_ref)

        out = x_ref[...] + _dot(yh_ref[...], w_ref[pl.ds(0, SEG), :]) + _dot(ya_ref[...], w_ref[pl.ds(SEG, SEG), :])
        r = lax.rsqrt(jnp.mean(out * out, axis=-1, keepdims=True) + NORM_EPS)
        n = out * r
        fg = fg_ref[...]
        err = n * fg - t_ref[...]
        loss_ref[...] += 0.5 * jnp.sum(jnp.mean(err * err, axis=-1, keepdims=True), axis=0, keepdims=True)
        dy = err * (1.0 / D)
        dfg_ref[...] += jnp.sum(dy * n, axis=0, keepdims=True)
        dn = dy * fg
        dout = r * (dn - n * jnp.mean(dn * n, axis=-1, keepdims=True))
        dout_ref[...] = dout
        doutb_ref[...] = dout.astype(BF16)

    row = lambda i: (i, 0)
    fix = lambda i: (0, 0)
    return pl.pallas_call(
        body, grid=(S // tm,), name="out_proj_loss",
        in_specs=[pl.BlockSpec((tm, SEG), row), pl.BlockSpec((tm, SEG), row), pl.BlockSpec((2 * SEG, D), fix),
                  pl.BlockSpec((tm, D), row), pl.BlockSpec((tm, D), row), pl.BlockSpec((1, D), fix)],
        out_specs=[pl.BlockSpec((tm, D), row), pl.BlockSpec((tm, D), row), pl.BlockSpec((1, 1), fix),
                   pl.BlockSpec((1, D), fix)],
        out_shape=[jax.ShapeDtypeStruct((S, D), F32), jax.ShapeDtypeStruct((S, D), BF16),
                   jax.ShapeDtypeStruct((1, 1), F32), jax.ShapeDtypeStruct((1, D), F32)],
        compiler_params=_cp(("arbitrary",)),
    )(yh, ya, w_out, x2, tgt, fgain)


def _dy_proj(doutb, w_out):
    S, D = doutb.shape
    K = w_out.shape[0]
    tm = min(512, S)

    def body(d_ref, w_ref, o_ref):
        o_ref[...] = _dot_nt(d_ref[...], w_ref[...])

    return pl.pallas_call(
        body, grid=(S // tm,), name="dy_proj",
        in_specs=[pl.BlockSpec((tm, D), lambda i: (i, 0)), pl.BlockSpec((K, D), lambda i: (0, 0))],
        out_specs=pl.BlockSpec((tm, K), lambda i: (i, 0)),
        out_shape=jax.ShapeDtypeStruct((S, K), F32),
        compiler_params=_cp(("parallel",)),
    )(doutb, w_out)


def _grad_w_out(yh, ya, doutb):
    S, SEG = yh.shape
    D = doutb.shape[1]
    R = (2 * SEG) // 4
    nb_half = SEG // R
    tk = min(512, S)

    def body(yh_ref, ya_ref, d_ref, o_ref):
        q = pl.program_id(0)
        k = pl.program_id(1)

        @pl.when(k == 0)
        def _():
            o_ref[...] = jnp.zeros_like(o_ref)

        @pl.when(q < nb_half)
        def _():
            o_ref[...] += _dot_tn(yh_ref[...], d_ref[...])

        @pl.when(q >= nb_half)
        def _():
            o_ref[...] += _dot_tn(ya_ref[...], d_ref[...])

    return pl.pallas_call(
        body, grid=(4, S // tk), name="grad_w_out",
        in_specs=[pl.BlockSpec((tk, R), lambda q, k: (k, jnp.minimum(q, nb_half - 1))),
                  pl.BlockSpec((tk, R), lambda q, k: (k, jnp.maximum(q - nb_half, 0))),
                  pl.BlockSpec((tk, D), lambda q, k: (k, 0))],
        out_specs=pl.BlockSpec((None, R, D), lambda q, k: (q, 0, 0)),
        out_shape=jax.ShapeDtypeStruct((4, R, D), F32),
        compiler_params=_cp(("parallel", "arbitrary")),
    )(yh, ya, doutb)


def _dz_sources(sources):
    counts = [s.shape[0] for s in sources]
    starts = [sum(counts[:k]) for k in range(len(counts))]
    assert sum(counts) == 8
    return counts, starts


def _dh_proj(sources, w_all):
    S = sources[0].shape[1]
    D = w_all.shape[1]
    SEG = w_all.shape[2] // 2
    counts, starts = _dz_sources(sources)
    ns = len(sources)
    tm = min(512, S)

    def body(*refs):
        src = refs[:ns]
        w_ref, o_ref = refs[ns:]
        j = pl.program_id(1)

        @pl.when(j == 0)
        def _():
            o_ref[...] = jnp.zeros_like(o_ref)

        for k in range(ns):
            @pl.when((j >= starts[k]) & (j < starts[k] + counts[k]))
            def _(k=k):
                o_ref[...] += _dot_nt(src[k][...], w_ref[...])

    def src_spec(k):
        return pl.BlockSpec((None, tm, SEG),
                            lambda i, j: (jnp.clip(j - starts[k], 0, counts[k] - 1), i, 0))

    return pl.pallas_call(
        body, grid=(S // tm, 8), name="dh_proj",
        in_specs=[src_spec(k) for k in range(ns)] + [pl.BlockSpec((None, D, SEG), lambda i, j: (j // 2, 0, j % 2))],
        out_specs=pl.BlockSpec((tm, D), lambda i, j: (i, 0)),
        out_shape=jax.ShapeDtypeStruct((S, D), F32),
        compiler_params=_cp(("parallel", "arbitrary")),
    )(*sources, w_all)


def _rms_bwd(dh, x2, rinv, gain, dout):
    S, D = x2.shape
    tm = min(256, S)

    def body(dh_ref, x_ref, r_ref, g_ref, dout_ref, gx_ref, dg_ref):
        @pl.when(pl.program_id(0) == 0)
        def _():
            dg_ref[...] = jnp.zeros_like(dg_ref)

        dh = dh_ref[...]
        r = r_ref[...]
        xhat = x_ref[...] * r
        dg_ref[...] += jnp.sum(dh * xhat, axis=0, keepdims=True)
        dxn = dh * g_ref[...]
        gx_ref[...] = dout_ref[...] + r * (dxn - xhat * jnp.mean(dxn * xhat, axis=-1, keepdims=True))

    row = lambda i: (i, 0)
    fix = lambda i: (0, 0)
    return pl.pallas_call(
        body, grid=(S // tm,), name="rms_bwd",
        in_specs=[pl.BlockSpec((tm, D), row), pl.BlockSpec((tm, D), row), pl.BlockSpec((tm, 1), row),
                  pl.BlockSpec((1, D), fix), pl.BlockSpec((tm, D), row)],
        out_specs=[pl.BlockSpec((tm, D), row), pl.BlockSpec((1, D), fix)],
        out_shape=[jax.ShapeDtypeStruct((S, D), F32), jax.ShapeDtypeStruct((1, D), F32)],
        compiler_params=_cp(("arbitrary",)),
    )(dh, x2, rinv, gain, dout)


def _grad_w_in(h, sources):
    S, D = h.shape
    SEG = sources[0].shape[2]
    counts, starts = _dz_sources(sources)
    ns = len(sources)
    tk = min(512, S)

    def body(*refs):
        h_ref = refs[0]
        src = refs[1:1 + ns]
        o_ref = refs[1 + ns]
        j = pl.program_id(0)
        k = pl.program_id(1)

        @pl.when(k == 0)
        def _():
            o_ref[...] = jnp.zeros_like(o_ref)

        for s in range(ns):
            @pl.when((j >= starts[s]) & (j < starts[s] + counts[s]))
            def _(s=s):
                o_ref[...] += _dot_tn(h_ref[...], src[s][...])

    def src_spec(s):
        return pl.BlockSpec((None, tk, SEG),
                            lambda j, k: (jnp.clip(j - starts[s], 0, counts[s] - 1), k, 0))

    return pl.pallas_call(
        body, grid=(8, S // tk), name="grad_w_in",
        in_specs=[pl.BlockSpec((tk, D), lambda j, k: (k, 0))] + [src_spec(s) for s in range(ns)],
        out_specs=pl.BlockSpec((None, D, SEG), lambda j, k: (j // 2, 0, j % 2)),
        out_shape=jax.ShapeDtypeStruct((4, D, 2 * SEG), F32),
        compiler_params=_cp(("parallel", "arbitrary")),
    )(h, *sources)


def _lower_bound(lbl):
    l0 = lbl[0:1, :]
    l1 = lbl[1:2, :]
    m = jnp.maximum(l0, l1)
    e0 = jnp.exp(l0 - m)
    e1 = jnp.exp(l1 - m)
    return e0 / (e0 + e1)


def _tile_masks():
    row = lax.broadcasted_iota(jnp.int32, (HGRN_TILE, HGRN_TILE), 0)
    col = lax.broadcasted_iota(jnp.int32, (HGRN_TILE, HGRN_TILE), 1)
    same = (row // HGRN_CHUNK) == (col // HGRN_CHUNK)
    return same & (row >= col), same & (row <= col)


def _chunk_last(b):
    T = b.shape[0]
    b3 = b.reshape(T // HGRN_CHUNK, HGRN_CHUNK, HGRN_HEAD)
    return jnp.broadcast_to(b3[:, HGRN_CHUNK - 1:HGRN_CHUNK, :], b3.shape).reshape(T, HGRN_HEAD)


def _chunk_sum(x):
    T = x.shape[0]
    x3 = x.reshape(T // HGRN_CHUNK, HGRN_CHUNK, HGRN_HEAD)
    return jnp.broadcast_to(jnp.sum(x3, axis=1, keepdims=True), x3.shape).reshape(T, HGRN_HEAD)


def _hgrn_dims(S, SEG):
    T = min(HGRN_BLOCK, S)
    assert S % T == 0 and T % HGRN_TILE == 0
    tiles = [slice(t * HGRN_TILE, (t + 1) * HGRN_TILE) for t in range(T // HGRN_TILE)]
    chunks = [slice(c * HGRN_CHUNK, (c + 1) * HGRN_CHUNK) for c in range(T // HGRN_CHUNK)]
    return SEG // HGRN_HEAD, T, T // HGRN_CHUNK, S // T, tiles, chunks


def _hgrn_fwd(zf32, lb_logits, gnorm):
    _, S, SEG = zf32.shape
    H, T, NC, NJ, tiles, chunks = _hgrn_dims(S, SEG)

    def body(zq_ref, zf_ref, zi_ref, zg_ref, lbl_ref, gn_ref, y_ref, st_ref, state):
        @pl.when(pl.program_id(1) == 0)
        def _():
            state[...] = jnp.zeros_like(state)

        lb = _lower_bound(lbl_ref[...])
        tril, _ = _tile_masks()
        tril_bf = tril.astype(BF16)
        zq = zq_ref[...]
        q = zq * _sigmoid(zq)
        f = lb + (1.0 - lb) * _sigmoid(zf_ref[...])
        k = 1.0 - f
        logf = jnp.log(f)
        b = jnp.concatenate([_exact_dot(tril_bf, logf[t]) for t in tiles], axis=0)
        bl = _chunk_last(b)
        qd_b = (q * jnp.exp(b)).astype(BF16)
        kd_b = (k * jnp.exp(-b)).astype(BF16)
        ke_b = (k * jnp.exp(bl - b)).astype(BF16)
        v_b = zi_ref[...].astype(BF16)
        o_intra = jnp.concatenate(
            [_dot(jnp.where(tril, _dot_nt(qd_b[t], kd_b[t]), 0.0).astype(BF16), v_b[t]) for t in tiles], axis=0)
        kvs = [_dot_tn(v_b[r], ke_b[r]) for r in chunks]
        ebl = jnp.exp(bl)
        st = state[...]
        sts = []
        for c in range(NC):
            st_ref[c] = st
            sts.append(st.astype(BF16))
            st = st * ebl[c * HGRN_CHUNK:c * HGRN_CHUNK + 1, :] + kvs[c]
        state[...] = st
        o = o_intra + jnp.concatenate([_dot_nt(qd_b[r], sb) for r, sb in zip(chunks, sts)], axis=0)
        on = o * lax.rsqrt(jnp.mean(o * o, axis=-1, keepdims=True) + NORM_EPS) * gn_ref[...]
        zg = zg_ref[...]
        y_ref[...] = (on * (zg * _sigmoid(zg))).astype(BF16)

    def zspec(seg):
        return pl.BlockSpec((None, T, HGRN_HEAD), lambda h, j: (seg, j, h))

    return pl.pallas_call(
        body, grid=(H, NJ), name="hgrn_fwd",
        in_specs=[zspec(0), zspec(1), zspec(2), zspec(3),
                  pl.BlockSpec((2, HGRN_HEAD), lambda h, j: (0, h)),
                  pl.BlockSpec((1, HGRN_HEAD), lambda h, j: (0, 0))],
        out_specs=[pl.BlockSpec((T, HGRN_HEAD), lambda h, j: (j, h)),
                   pl.BlockSpec((NC, None, HGRN_HEAD, HGRN_HEAD), lambda h, j: (j, h, 0, 0))],
        out_shape=[jax.ShapeDtypeStruct((S, SEG), BF16),
                   jax.ShapeDtypeStruct((S // HGRN_CHUNK, H, HGRN_HEAD, HGRN_HEAD), F32)],
        scratch_shapes=[pltpu.VMEM((HGRN_HEAD, HGRN_HEAD), F32)],
        compiler_params=_cp(("parallel", "arbitrary")),
    )(zf32, zf32, zf32, zf32, lb_logits, gnorm)


def _hgrn_bwd(zf32, lb_logits, gnorm, states, dy):
    _, S, SEG = zf32.shape
    H, T, NC, NJ, tiles, chunks = _hgrn_dims(S, SEG)
    C = HGRN_CHUNK

    def body(zq_ref, zf_ref, zi_ref, zg_ref, lbl_ref, gn_ref, st_ref, dy_ref, dz_ref, dl_ref, dgn_ref, gstate):
        @pl.when(pl.program_id(1) == 0)
        def _():
            gstate[...] = jnp.zeros_like(gstate)
            dl_ref[...] = jnp.zeros_like(dl_ref)
            dgn_ref[...] = jnp.zeros_like(dgn_ref)

        lb = _lower_bound(lbl_ref[...])
        gn = gn_ref[...]
        tril, triu = _tile_masks()
        tril_bf = tril.astype(BF16)
        triu_bf = triu.astype(BF16)
        q, dq_dz = _silu_and_grad(zq_ref[...])
        sf = _sigmoid(zf_ref[...])
        f = lb + (1.0 - lb) * sf
        k = 1.0 - f
        logf = jnp.log(f)
        b = jnp.concatenate([_exact_dot(tril_bf, logf[t]) for t in tiles], axis=0)
        bl = _chunk_last(b)
        eb = jnp.exp(b)
        enb = jnp.exp(-b)
        ekl = jnp.exp(bl - b)
        ebl = jnp.exp(bl)
        qd = q * eb
        kd = k * enb
        ke = k * ekl
        qd_b = qd.astype(BF16)
        kd_b = kd.astype(BF16)
        ke_b = ke.astype(BF16)
        v_b = zi_ref[...].astype(BF16)
        sts = [st_ref[c] for c in range(NC)]
        sts_b = [s.astype(BF16) for s in sts]
        a_b = [jnp.where(tril, _dot_nt(qd_b[t], kd_b[t]), 0.0).astype(BF16) for t in tiles]
        o = (jnp.concatenate([_dot(a, v_b[t]) for a, t in zip(a_b, tiles)], axis=0)
             + jnp.concatenate([_dot_nt(qd_b[r], sb) for r, sb in zip(chunks, sts_b)], axis=0))
        rinv = lax.rsqrt(jnp.mean(o * o, axis=-1, keepdims=True) + NORM_EPS)
        ohat = o * rinv
        sg, dsg = _silu_and_grad(zg_ref[...])
        dyv = dy_ref[...]
        don = dyv * sg
        dz_ref[3] = (dyv * (ohat * gn) * dsg).astype(BF16)
        dgn_ref[...] += jnp.sum(don * ohat, axis=0, keepdims=True)
        dohat = don * gn
        do = rinv * (dohat - ohat * jnp.mean(dohat * ohat, axis=-1, keepdims=True))
        do_b = do.astype(BF16)
        da_b = [jnp.where(tril, _dot_nt(do_b[t], v_b[t]), 0.0).astype(BF16) for t in tiles]
        dv_intra = jnp.concatenate([_dot_tn(a, do_b[t]) for a, t in zip(a_b, tiles)], axis=0)
        dqd_intra = jnp.concatenate([_dot(da, kd_b[t]) for da, t in zip(da_b, tiles)], axis=0)
        dkd = jnp.concatenate([_dot_tn(da, qd_b[t]) for da, t in zip(da_b, tiles)], axis=0)
        dqd_inter = jnp.concatenate([_dot(do_b[r], sb) for r, sb in zip(chunks, sts_b)], axis=0)
        gks = [_dot_tn(do_b[r], qd_b[r]) for r in chunks]
        g = gstate[...]
        gs = [None] * NC
        for c in reversed(range(NC)):
            gs[c] = g
            g = g * ebl[c * C:c * C + 1, :] + gks[c]
        gstate[...] = g
        gs_b = [x.astype(BF16) for x in gs]
        dv = dv_intra + jnp.concatenate([_dot_nt(ke_b[r], gb) for r, gb in zip(chunks, gs_b)], axis=0)
        dz_ref[2] = dv.astype(BF16)
        dke = jnp.concatenate([_dot(v_b[r], gb) for r, gb in zip(chunks, gs_b)], axis=0)
        debl = jnp.concatenate(
            [jnp.broadcast_to(jnp.sum(x * s, axis=0, keepdims=True), (C, HGRN_HEAD)) for x, s in zip(gs, sts)], axis=0)
        dqd = dqd_intra + dqd_inter
        dz_ref[0] = ((dqd * eb) * dq_dz).astype(BF16)
        t_ke = dke * ke
        db = dqd * qd - dkd * kd - t_ke
        db_last = _chunk_sum(t_ke) + debl * ebl
        dk = dkd * enb + dke * ekl
        dlogf = jnp.concatenate([_exact_dot(triu_bf, db[t]) for t in tiles], axis=0) + db_last
        df = dlogf / f - dk
        dz_ref[1] = (df * (1.0 - lb) * (sf * (1.0 - sf))).astype(BF16)
        dlb = jnp.sum(df * (1.0 - sf), axis=0, keepdims=True)
        dl0 = dlb * lb * (1.0 - lb)
        dl_ref[0:1, :] += dl0
        dl_ref[1:2, :] -= dl0

    def zspec(seg):
        return pl.BlockSpec((None, T, HGRN_HEAD), lambda h, j: (seg, NJ - 1 - j, h))

    return pl.pallas_call(
        body, grid=(H, NJ), name="hgrn_bwd",
        in_specs=[zspec(0), zspec(1), zspec(2), zspec(3),
                  pl.BlockSpec((2, HGRN_HEAD), lambda h, j: (0, h)),
                  pl.BlockSpec((1, HGRN_HEAD), lambda h, j: (0, 0)),
                  pl.BlockSpec((NC, None, HGRN_HEAD, HGRN_HEAD), lambda h, j: (NJ - 1 - j, h, 0, 0)),
                  pl.BlockSpec((T, HGRN_HEAD), lambda h, j: (NJ - 1 - j, h))],
        out_specs=[pl.BlockSpec((4, T, HGRN_HEAD), lambda h, j: (0, NJ - 1 - j, h)),
                   pl.BlockSpec((2, HGRN_HEAD), lambda h, j: (0, h)),
                   pl.BlockSpec((None, 1, HGRN_HEAD), lambda h, j: (h, 0, 0))],
        out_shape=[jax.ShapeDtypeStruct((4, S, SEG), BF16), jax.ShapeDtypeStruct((2, SEG), F32),
                   jax.ShapeDtypeStruct((H, 1, HGRN_HEAD), F32)],
        scratch_shapes=[pltpu.VMEM((HGRN_HEAD, HGRN_HEAD), F32)],
        compiler_params=_cp(("parallel", "arbitrary")),
    )(zf32, zf32, zf32, zf32, lb_logits, gnorm, states, dy)


def _alibi_slopes(seg):
    n_heads = seg // ATTN_HEAD
    s = 2.0 ** (-8.0 * np.arange(1, n_heads + 1, dtype=np.float64) / n_heads)
    return jnp.asarray(np.repeat(s, ATTN_HEAD)[None, :], F32)


def _attn_dims(S, SEG, d):
    rb = BAND * d
    assert S % rb == 0 and SEG % LANES == 0
    cb = min(SEG, ATTN_BLOCK_ELEMS // rb) if d == 1 else LANES
    assert SEG % cb == 0
    return rb, cb, S // rb, SEG // cb


def _res_rows(r, d):
    return pl.ds(0, BAND) if d == 1 else pl.ds(r, BAND, stride=d)


def _for_residues(d, fn):
    if d == 1:
        fn(0)
    else:
        def step(r, carry):
            fn(r)
            return carry
        lax.fori_loop(0, d, step, 0)


def _band_terms(n, d):
    i = lax.broadcasted_iota(jnp.int32, (BAND, 2 * BAND), 0)
    jj = lax.broadcasted_iota(jnp.int32, (BAND, 2 * BAND), 1)
    delta = BAND + i - jj
    valid = (delta >= 0) & (delta <= BAND) & ((n > 0) | (jj >= BAND))
    return (-d * delta).astype(F32), valid


def _head_biases(slopes, nd, valid):
    out = []
    for s in _per_head(slopes):
        s2 = jnp.concatenate([s, s], axis=1)
        out.append(jnp.where(valid, s2 * nd, NEG))
    return out


def _head_halves(x):
    lane = lax.broadcasted_iota(jnp.int32, x.shape, 1)
    zero = jnp.zeros_like(x)
    return jnp.where(lane < ATTN_HEAD, x, zero), jnp.where(lane < ATTN_HEAD, zero, x)


def _per_head(x):
    lane = lax.broadcasted_iota(jnp.int32, x.shape, 1)
    sw = pltpu.roll(x, ATTN_HEAD, 1)
    first = lane < ATTN_HEAD
    return jnp.where(first, x, sw), jnp.where(first, sw, x)


def _attn_fwd(qkv, slopes, d):
    _, S, SEG = qkv.shape
    rb, cb, nb, ncb = _attn_dims(S, SEG, d)
    NP = cb // LANES

    def body(q_ref, kp_ref, kc_ref, vp_ref, vc_ref, sl_ref, o_ref, l_ref):
        n = pl.program_id(1)
        nd, valid = _band_terms(n, d)
        first = lax.broadcasted_iota(jnp.int32, (BAND, LANES), 1) < ATTN_HEAD
        biases = [_head_biases(sl_ref[:, p * LANES:(p + 1) * LANES], nd, valid) for p in range(NP)]

        def residue(r):
            rows = _res_rows(r, d)
            for p in range(NP):
                cols = slice(p * LANES, (p + 1) * LANES)
                kc = jnp.concatenate([kp_ref[rows, cols], kc_ref[rows, cols]], axis=0).astype(BF16)
                vc = jnp.concatenate([vp_ref[rows, cols], vc_ref[rows, cols]], axis=0).astype(BF16)
                outs, lses = [], []
                for t, qh in enumerate(_head_halves((q_ref[rows, cols] * ATTN_SCALE).astype(BF16))):
                    s = _dot_nt(qh, kc) + biases[p][t]
                    m = jnp.max(s, axis=-1, keepdims=True)
                    e = jnp.exp(s - m)
                    den = jnp.sum(e, axis=-1, keepdims=True)
                    outs.append(_dot(e.astype(BF16), vc) / den)
                    lses.append(m + jnp.log(den))
                o_ref[rows, cols] = jnp.where(first, outs[0], outs[1])
                l_ref[rows, cols] = jnp.where(first, jnp.broadcast_to(lses[0], (BAND, LANES)),
                                              jnp.broadcast_to(lses[1], (BAND, LANES)))

        _for_residues(d, residue)

    def spec(seg, prev):
        if prev:
            return pl.BlockSpec((None, rb, cb), lambda c, n: (seg, jnp.maximum(n - 1, 0), c))
        return pl.BlockSpec((None, rb, cb), lambda c, n: (seg, n, c))

    out = pl.BlockSpec((rb, cb), lambda c, n: (n, c))
    return pl.pallas_call(
        body, grid=(ncb, nb), name=f"attn_fwd_d{d}",
        in_specs=[spec(0, False), spec(1, True), spec(1, False), spec(2, True), spec(2, False),
                  pl.BlockSpec((1, cb), lambda c, n: (0, c))],
        out_specs=[out, out],
        out_shape=[jax.ShapeDtypeStruct((S, SEG), F32), jax.ShapeDtypeStruct((S, SEG), F32)],
        compiler_params=_cp(("parallel", "parallel")),
    )(qkv, qkv, qkv, qkv, qkv, slopes)


def _attn_merge(outs, lses, zf32):
    S, SEG = outs[0].shape
    tm = min(256, S)

    def body(o1, o2, o3, l1, l2, l3, zg_ref, o_ref, lse_ref, y_ref):
        a, b, c = l1[...], l2[...], l3[...]
        m = jnp.maximum(jnp.maximum(a, b), c)
        ea, eb, ec = jnp.exp(a - m), jnp.exp(b - m), jnp.exp(c - m)
        tot = ea + eb + ec
        o = (ea / tot) * o1[...] + (eb / tot) * o2[...] + (ec / tot) * o3[...]
        o_ref[...] = o
        lse_ref[...] = m + jnp.log(tot)
        zg = zg_ref[...]
        y_ref[...] = (o * (zg * _sigmoid(zg))).astype(BF16)

    row = pl.BlockSpec((tm, SEG), lambda i: (i, 0))
    return pl.pallas_call(
        body, grid=(S // tm,), name="attn_merge",
        in_specs=[row] * 6 + [pl.BlockSpec((None, tm, SEG), lambda i: (4, i, 0))],
        out_specs=[row, row, row],
        out_shape=[jax.ShapeDtypeStruct((S, SEG), F32), jax.ShapeDtypeStruct((S, SEG), F32),
                   jax.ShapeDtypeStruct((S, SEG), BF16)],
        compiler_params=_cp(("parallel",)),
    )(*outs, *lses, zf32)


def _attn_gate_bwd(dy, o, zf32):
    S, SEG = o.shape
    tm = min(256, S)
    NP = SEG // LANES

    def body(dy_ref, o_ref, zg_ref, do_ref, dl_ref, dzg_ref):
        r = lax.broadcasted_iota(jnp.int32, (LANES, LANES), 0) // ATTN_HEAD
        c = lax.broadcasted_iota(jnp.int32, (LANES, LANES), 1) // ATTN_HEAD
        same_head = (r == c).astype(BF16)
        for p in range(NP):
            cols = slice(p * LANES, (p + 1) * LANES)
            sg, dsg = _silu_and_grad(zg_ref[:, cols])
            dyv = dy_ref[:, cols]
            ov = o_ref[:, cols]
            do = dyv * sg
            do_ref[:, cols] = do
            dzg_ref[:, cols] = (dyv * ov * dsg).astype(BF16)
            dl_ref[:, cols] = _exact_dot_right(do * ov, same_head)

    return pl.pallas_call(
        body, grid=(S // tm,), name="attn_gate_bwd",
        in_specs=[pl.BlockSpec((tm, SEG), lambda i: (i, 1)), pl.BlockSpec((tm, SEG), lambda i: (i, 0)),
                  pl.BlockSpec((None, tm, SEG), lambda i: (4, i, 0))],
        out_specs=[pl.BlockSpec((tm, SEG), lambda i: (i, 0)), pl.BlockSpec((tm, SEG), lambda i: (i, 0)),
                   pl.BlockSpec((None, tm, SEG), lambda i: (0, i, 0))],
        out_shape=[jax.ShapeDtypeStruct((S, SEG), F32), jax.ShapeDtypeStruct((S, SEG), F32),
                   jax.ShapeDtypeStruct((1, S, SEG), BF16)],
        compiler_params=_cp(("parallel",)),
    )(dy, o, zf32)


def _attn_bwd(qkv, slopes, do, lse, dl, d, acc, out_dtype):
    _, S, SEG = qkv.shape
    rb, cb, nb, ncb = _attn_dims(S, SEG, d)
    NP = cb // LANES
    has_acc = acc is not None

    def body(*refs):
        q_ref, kp_ref, kc_ref, vp_ref, vc_ref, sl_ref, do_ref, lse_ref, dl_ref = refs[:9]
        refs = refs[9:]
        if has_acc:
            aq_ref, ak_ref, av_ref = refs[:3]
            refs = refs[3:]
        dq_ref, dk_ref, dv_ref, ck, cv = refs
        n = pl.program_id(1)

        @pl.when(n == 0)
        def _():
            ck[...] = jnp.zeros_like(ck)
            cv[...] = jnp.zeros_like(cv)

        @pl.when(n < nb)
        def _():
            nd, valid = _band_terms(n, d)
            first = lax.broadcasted_iota(jnp.int32, (BAND, LANES), 1) < ATTN_HEAD
            biases = [_head_biases(sl_ref[:, p * LANES:(p + 1) * LANES], nd, valid) for p in range(NP)]

            def residue(r):
                rows = _res_rows(r, d)
                for p in range(NP):
                    cols = slice(p * LANES, (p + 1) * LANES)
                    kc = jnp.concatenate([kp_ref[rows, cols], kc_ref[rows, cols]], axis=0).astype(BF16)
                    vc = jnp.concatenate([vp_ref[rows, cols], vc_ref[rows, cols]], axis=0).astype(BF16)
                    q_halves = _head_halves((q_ref[rows, cols] * ATTN_SCALE).astype(BF16))
                    do_halves = _head_halves(do_ref[rows, cols].astype(BF16))
                    lse_heads = _per_head(lse_ref[rows, cols])
                    dl_heads = _per_head(dl_ref[rows, cols])
                    dqs = []
                    dk = jnp.zeros((2 * BAND, LANES), F32)
                    dv = jnp.zeros((2 * BAND, LANES), F32)
                    for t in range(2):
                        s = _dot_nt(q_halves[t], kc) + biases[p][t]
                        lse2 = jnp.concatenate([lse_heads[t], lse_heads[t]], axis=1)
                        dl2 = jnp.concatenate([dl_heads[t], dl_heads[t]], axis=1)
                        pr = jnp.exp(s - lse2)
                        dp = _dot_nt(do_halves[t], vc)
                        ds = (pr * (dp - dl2)).astype(BF16)
                        dqs.append(_dot(ds, kc))
                        dk = dk + _dot_tn(ds, q_halves[t])
                        dv = dv + _dot_tn(pr.astype(BF16), do_halves[t])
                    dq = jnp.where(first, dqs[0], dqs[1]) * ATTN_SCALE
                    dk_prev = ck[r, :, cols] + dk[:BAND, :]
                    dv_prev = cv[r, :, cols] + dv[:BAND, :]
                    if has_acc:
                        dq = dq + aq_ref[rows, cols]
                        dk_prev = dk_prev + ak_ref[rows, cols]
                        dv_prev = dv_prev + av_ref[rows, cols]
                    dq_ref[rows, cols] = dq.astype(out_dtype)
                    dk_ref[rows, cols] = dk_prev.astype(out_dtype)
                    dv_ref[rows, cols] = dv_prev.astype(out_dtype)
                    ck[r, :, cols] = dk[BAND:, :]
                    cv[r, :, cols] = dv[BAND:, :]

            _for_residues(d, residue)

        @pl.when(n == nb)
        def _():
            def residue(r):
                rows = _res_rows(r, d)
                dk_last = ck[r]
                dv_last = cv[r]
                if has_acc:
                    dk_last = dk_last + ak_ref[rows, :]
                    dv_last = dv_last + av_ref[rows, :]
                dk_ref[rows, :] = dk_last.astype(out_dtype)
                dv_ref[rows, :] = dv_last.astype(out_dtype)

            _for_residues(d, residue)

    cur2 = lambda c, n: (jnp.minimum(n, nb - 1), c)
    cur3 = lambda c, n: (0, jnp.minimum(n, nb - 1), c)
    lag3 = lambda c, n: (0, jnp.clip(n - 1, 0, nb - 1), c)

    def spec(seg, prev):
        if prev:
            return pl.BlockSpec((None, rb, cb), lambda c, n: (seg, jnp.clip(n - 1, 0, nb - 1), c))
        return pl.BlockSpec((None, rb, cb), lambda c, n: (seg, jnp.minimum(n, nb - 1), c))

    in_specs = [spec(0, False), spec(1, True), spec(1, False), spec(2, True), spec(2, False),
                pl.BlockSpec((1, cb), lambda c, n: (0, c)),
                pl.BlockSpec((rb, cb), cur2), pl.BlockSpec((rb, cb), cur2), pl.BlockSpec((rb, cb), cur2)]
    args = [qkv, qkv, qkv, qkv, qkv, slopes, do, lse, dl]
    aliases = {}
    if has_acc:
        in_specs += [pl.BlockSpec((None, rb, cb), cur3), pl.BlockSpec((None, rb, cb), lag3),
                     pl.BlockSpec((None, rb, cb), lag3)]
        args += list(acc)
        if out_dtype == F32:
            aliases = {9: 0, 10: 1, 11: 2}
    return pl.pallas_call(
        body, grid=(ncb, nb + 1), name=f"attn_bwd_d{d}",
        in_specs=in_specs,
        out_specs=[pl.BlockSpec((None, rb, cb), cur3), pl.BlockSpec((None, rb, cb), lag3),
                   pl.BlockSpec((None, rb, cb), lag3)],
        out_shape=[jax.ShapeDtypeStruct((1, S, SEG), out_dtype)] * 3,
        scratch_shapes=[pltpu.VMEM((d, BAND, cb), F32), pltpu.VMEM((d, BAND, cb), F32)],
        input_output_aliases=aliases,
        compiler_params=_cp(("parallel", "arbitrary")),
    )(*args)


def _adamw(w, g, m, v, name):
    R, C = w.shape
    tr = R if R <= 256 else 256
    assert R % tr == 0

    def body(w_ref, g_ref, m_ref, v_ref, d_ref, nm_ref, nv_ref):
        g = g_ref[...]
        nm = ADAM_B1 * m_ref[...] + (1.0 - ADAM_B1) * g
        nv = ADAM_B2 * v_ref[...] + (1.0 - ADAM_B2) * (g * g)
        m_hat = nm / (1.0 - ADAM_B1 ** ADAM_STEP)
        v_hat = nv / (1.0 - ADAM_B2 ** ADAM_STEP)
        d_ref[...] = -ADAM_LR * (m_hat / (jnp.sqrt(v_hat) + ADAM_EPS) + ADAM_WD * w_ref[...])
        nm_ref[...] = nm
        nv_ref[...] = nv

    blk = pl.BlockSpec((tr, C), lambda i: (i, 0))
    sds = jax.ShapeDtypeStruct((R, C), F32)
    return pl.pallas_call(
        body, grid=(R // tr,), name=name, in_specs=[blk] * 4, out_specs=[blk] * 3, out_shape=[sds] * 3,
        compiler_params=_cp(("parallel",)),
    )(w, g, m, v)


def _coords():
    return lax.axis_index("x"), lax.axis_index("y"), lax.axis_index("c")


def _other_chips(x, y):
    return [(1 - x, y), (x, 1 - y), (1 - x, 1 - y)]


ANY = pl.BlockSpec(memory_space=pl.ANY)


def _cast_into_slot(w, where, name):
    R, C = w.shape
    tr = min(256, R)

    def body(where_ref, w_ref, o_ref):
        o_ref[...] = w_ref[...].astype(BF16)

    grid_spec = pltpu.PrefetchScalarGridSpec(
        num_scalar_prefetch=1, grid=(R // tr,),
        in_specs=[pl.BlockSpec((tr, C), lambda i, w: (i, 0))],
        out_specs=pl.BlockSpec((None, tr, C), lambda i, w: (w[1], i, 0)))
    return pl.pallas_call(
        body, grid_spec=grid_spec, name=name, out_shape=jax.ShapeDtypeStruct((4, R, C), BF16),
        compiler_params=_cp(("parallel",)),
    )(where, w)


def _gather_weights(wia, woa):
    nrows = (wia.shape[1] // 2, woa.shape[1] // 2)

    def body(wi_in, wo_in, wia_ref, woa_ref, send_sems, recv_sems):
        x, y, c = _coords()
        me = 2 * x + y
        alls = (wia_ref, woa_ref)

        def half(a, chip, hc):
            return alls[a].at[chip, pl.ds(hc * nrows[a], nrows[a]), :]

        def remote(a, k, part, to):
            return pltpu.make_async_remote_copy(src_ref=part, dst_ref=part, send_sem=send_sems.at[2 * k + a],
                                                recv_sem=recv_sems.at[2 * k + a], device_id=to, device_id_type=MESH)

        started = []
        for k, (px, py) in enumerate(_other_chips(x, y)):
            for a in range(2):
                cp = remote(a, k, half(a, me, c), (px, py, c))
                cp.start()
                started.append(cp)
        for k, (px, py) in enumerate(_other_chips(x, y)):
            for a in range(2):
                got = half(a, 2 * px + py, c)
                remote(a, k, got, (px, py, c)).wait_recv()
                cp = remote(a, 3 + k, got, (x, y, 1 - c))
                cp.start()
                started.append(cp)
        for k, (px, py) in enumerate(_other_chips(x, y)):
            for a in range(2):
                remote(a, 3 + k, half(a, 2 * px + py, 1 - c), (x, y, 1 - c)).wait_recv()
        for cp in started:
            cp.wait_send()

    return pl.pallas_call(
        body, name="gather_weights", in_specs=[ANY, ANY], out_specs=[ANY, ANY],
        out_shape=[jax.ShapeDtypeStruct(wia.shape, wia.dtype), jax.ShapeDtypeStruct(woa.shape, woa.dtype)],
        scratch_shapes=[pltpu.SemaphoreType.DMA((12,)), pltpu.SemaphoreType.DMA((12,))],
        input_output_aliases={0: 0, 1: 1},
    )(wia, woa)


def _swap_halves(gi, go):
    n_i, n_o = gi.shape[1] // 2, go.shape[1] // 2

    def body(gi_ref, go_ref, si_ref, so_ref, send_sems, recv_sems):
        x, y, c = _coords()
        cps = []
        for a, (src, dst, nr) in enumerate(((gi_ref, si_ref, n_i), (go_ref, so_ref, n_o))):
            cp = pltpu.make_async_remote_copy(
                src_ref=src.at[:, pl.ds((1 - c) * nr, nr), :], dst_ref=dst, send_sem=send_sems.at[a],
                recv_sem=recv_sems.at[a], device_id=(x, y, 1 - c), device_id_type=MESH)
            cp.start()
            cps.append(cp)
        for cp in cps:
            cp.wait()

    return pl.pallas_call(
        body, name="swap_halves", in_specs=[ANY, ANY], out_specs=[ANY, ANY],
        out_shape=[jax.ShapeDtypeStruct((4, n_i, gi.shape[2]), F32), jax.ShapeDtypeStruct((4, n_o, go.shape[2]), F32)],
        scratch_shapes=[pltpu.SemaphoreType.DMA((2,)), pltpu.SemaphoreType.DMA((2,))],
    )(gi, go)


def _pair_sum(g, sib, where, name):
    _, n2, C = g.shape
    N = n2 // 2
    tr = min(256, N)
    nt = N // tr

    def body(where_ref, g_ref, s_ref, qb_ref, own_ref):
        q = pl.program_id(1)
        tot = g_ref[...] + s_ref[...]
        qb_ref[...] = tot.astype(BF16)

        @pl.when(q == where_ref[1])
        def _():
            own_ref[...] = tot

    grid_spec = pltpu.PrefetchScalarGridSpec(
        num_scalar_prefetch=1, grid=(nt, 4),
        in_specs=[pl.BlockSpec((None, tr, C), lambda i, q, w: (q, w[0] * nt + i, 0)),
                  pl.BlockSpec((None, tr, C), lambda i, q, w: (q, i, 0))],
        out_specs=[pl.BlockSpec((None, tr, C), lambda i, q, w: (q, i, 0)),
                   pl.BlockSpec((tr, C), lambda i, q, w: (i, 0))])
    return pl.pallas_call(
        body, grid_spec=grid_spec, name=name,
        out_shape=[jax.ShapeDtypeStruct((4, N, C), BF16), jax.ShapeDtypeStruct((N, C), F32)],
        compiler_params=_cp(("parallel", "arbitrary")),
    )(where, g, sib)


def _scatter_to_chips(qi, qo):
    def body(qi_ref, qo_ref, ri_ref, ro_ref, send_sems, recv_sems):
        x, y, c = _coords()
        cps = []
        for k, (px, py) in enumerate(_other_chips(x, y)):
            for a, (src, dst) in enumerate(((qi_ref, ri_ref), (qo_ref, ro_ref))):
                cp = pltpu.make_async_remote_copy(
                    src_ref=src.at[2 * px + py], dst_ref=dst.at[k], send_sem=send_sems.at[2 * k + a],
                    recv_sem=recv_sems.at[2 * k + a], device_id=(px, py, c), device_id_type=MESH)
                cp.start()
                cps.append(cp)
        for cp in cps:
            cp.wait()

    return pl.pallas_call(
        body, name="scatter_to_chips", in_specs=[ANY, ANY], out_specs=[ANY, ANY],
        out_shape=[jax.ShapeDtypeStruct((3,) + qi.shape[1:], BF16), jax.ShapeDtypeStruct((3,) + qo.shape[1:], BF16)],
        scratch_shapes=[pltpu.SemaphoreType.DMA((6,)), pltpu.SemaphoreType.DMA((6,))],
    )(qi, qo)


def _chip_sum(own, got, where, name):
    N, C = own.shape
    tr = min(256, N)
    nt = N // tr

    def body(where_ref, own_ref, got_ref, o_ref):
        t = own_ref[...]
        for k in range(3):
            t = t + got_ref[k].astype(F32)
        o_ref[...] = t

    grid_spec = pltpu.PrefetchScalarGridSpec(
        num_scalar_prefetch=1, grid=(nt,),
        in_specs=[pl.BlockSpec((tr, C), lambda i, w: (i, 0)), pl.BlockSpec((3, tr, C), lambda i, w: (0, i, 0))],
        out_specs=pl.BlockSpec((tr, C), lambda i, w: (w[0] * nt + i, 0)))
    return pl.pallas_call(
        body, grid_spec=grid_spec, name=name, out_shape=jax.ShapeDtypeStruct((2 * N, C), F32),
        compiler_params=_cp(("parallel",)),
    )(where, own, got)


def _join_halves(gi, go):
    def body(gi_in, go_in, gi_ref, go_ref, send_sems, recv_sems):
        x, y, c = _coords()
        cps = []
        for a, ref in enumerate((gi_ref, go_ref)):
            nr = ref.shape[0] // 2
            mine = ref.at[pl.ds(c * nr, nr), :]
            cp = pltpu.make_async_remote_copy(src_ref=mine, dst_ref=mine, send_sem=send_sems.at[a],
                                              recv_sem=recv_sems.at[a], device_id=(x, y, 1 - c), device_id_type=MESH)
            cp.start()
            cps.append(cp)
        for a, ref in enumerate((gi_ref, go_ref)):
            nr = ref.shape[0] // 2
            theirs = ref.at[pl.ds((1 - c) * nr, nr), :]
            pltpu.make_async_remote_copy(src_ref=theirs, dst_ref=theirs, send_sem=send_sems.at[a],
                                         recv_sem=recv_sems.at[a], device_id=(x, y, 1 - c),
                                         device_id_type=MESH).wait_recv()
        for cp in cps:
            cp.wait_send()

    return pl.pallas_call(
        body, name="join_halves", in_specs=[ANY, ANY], out_specs=[ANY, ANY],
        out_shape=[jax.ShapeDtypeStruct(gi.shape, F32), jax.ShapeDtypeStruct(go.shape, F32)],
        scratch_shapes=[pltpu.SemaphoreType.DMA((2,)), pltpu.SemaphoreType.DMA((2,))],
        input_output_aliases={0: 0, 1: 1},
    )(gi, go)


def _all_reduce_small(part):
    R, C = part.shape

    def body(p_ref, o_ref, slots, send_sems, recv_sems):
        x, y, c = _coords()
        me = 4 * x + 2 * y + c
        slots[me] = p_ref[...]
        cps = []
        for k in range(1, 8):
            fx, fy, fc = (k >> 2) & 1, (k >> 1) & 1, k & 1
            peer = (1 - x if fx else x, 1 - y if fy else y, 1 - c if fc else c)
            cp = pltpu.make_async_remote_copy(src_ref=p_ref, dst_ref=slots.at[me], send_sem=send_sems.at[k - 1],
                                              recv_sem=recv_sems.at[k - 1], device_id=peer, device_id_type=MESH)
            cp.start()
            cps.append(cp)
        for cp in cps:
            cp.wait()
        t = slots[0]
        for k in range(1, 8):
            t = t + slots[k]
        o_ref[...] = t

    vm = pl.BlockSpec(memory_space=pltpu.VMEM)
    return pl.pallas_call(
        body, name="all_reduce_small", in_specs=[vm], out_specs=vm,
        out_shape=jax.ShapeDtypeStruct((R, C), F32),
        scratch_shapes=[pltpu.VMEM((8, R, C), F32), pltpu.SemaphoreType.DMA((7,)), pltpu.SemaphoreType.DMA((7,))],
    )(part)


def _local_step(x2, tgt, norm_gain, w_all, lb_logits, hgrn_gnorm, w_out_all, fgain):
    S, D = x2.shape
    SEG = w_all.shape[2] // 2
    slopes = _alibi_slopes(SEG)
    h, rinv = _rms_fwd(x2, norm_gain)
    zf32 = _in_proj(h, w_all, (0, 1, 2, 3, 7), "in_proj_hgrn_gates")
    qkv = _in_proj(h, w_all, (4, 5, 6), "in_proj_qkv")
    yh, states = _hgrn_fwd(zf32, lb_logits, hgrn_gnorm)
    outs, lses = [], []
    for d in DILATIONS:
        o, l = _attn_fwd(qkv, slopes, d)
        outs.append(o)
        lses.append(l)
    o_attn, lse, ya = _attn_merge(outs, lses, zf32)
    dout, doutb, loss, dfg = _out_proj_loss(yh, ya, w_out_all, x2, tgt, fgain)
    dy = _dy_proj(doutb, w_out_all)
    g_w_out = _grad_w_out(yh, ya, doutb)
    dzh, dlogits, dgn = _hgrn_bwd(zf32, lb_logits, hgrn_gnorm, states, dy)
    do, dl, dzg = _attn_gate_bwd(dy, o_attn, zf32)
    acc = None
    order = sorted(DILATIONS, reverse=True)
    for k, d in enumerate(order):
        acc = _attn_bwd(qkv, slopes, do, lse, dl, d, acc, BF16 if k == len(order) - 1 else F32)
    sources = [dzh] + list(acc) + [dzg]
    grad_x, dgain = _rms_bwd(_dh_proj(sources, w_all), x2, rinv, norm_gain, dout)
    g_w_in = _grad_w_in(h, sources)
    return loss, grad_x, dgain, g_w_in, dlogits, dgn, g_w_out, dfg


def _pack_small(D, loss, dgain, dlogits, dgn, dfg):
    def row(v):
        v = v.reshape(1, -1)
        return jnp.pad(v, ((0, 0), (0, D - v.shape[1])))
    rows = [row(dgain), row(dfg), row(dlogits[0]), row(dlogits[1]), row(jnp.sum(dgn, axis=0)), row(loss)]
    rows += [jnp.zeros((1, D), F32)] * (8 - len(rows))
    return jnp.concatenate(rows, axis=0)


def kernel(x, norm_gain, w_in, lb_logits, hgrn_gnorm, w_out, final_gain, loss_target, m_norm_gain, m_w_in, m_lb_logits, m_hgrn_gnorm, m_w_out, m_final_gain, v_norm_gain, v_w_in, v_lb_logits, v_hgrn_gnorm, v_w_out, v_final_gain):
    _, S, D = x.shape
    SEG = w_in.shape[2] // 2
    x2 = x[0]
    tgt = loss_target[0]
    fgain = final_gain.reshape(1, D)
    where = jnp.stack([lax.axis_index("c"), 2 * lax.axis_index("x") + lax.axis_index("y")]).astype(jnp.int32)

    w_all, w_out_all = _gather_weights(_cast_into_slot(w_in[0], where, "cast_w_in"),
                                       _cast_into_slot(w_out[0], where, "cast_w_out"))
    w_out_all = w_out_all.reshape(2 * SEG, D)

    loss, grad_x, dgain, g_w_in, dlogits, dgn, g_w_out, dfg = _local_step(
        x2, tgt, norm_gain, w_all, lb_logits, hgrn_gnorm, w_out_all, fgain)

    sib_i, sib_o = _swap_halves(g_w_in, g_w_out)
    qi, own_i = _pair_sum(g_w_in, sib_i, where, "pair_sum_w_in")
    qo, own_o = _pair_sum(g_w_out, sib_o, where, "pair_sum_w_out")
    got_i, got_o = _scatter_to_chips(qi, qo)
    grad_w_in, grad_w_out = _join_halves(_chip_sum(own_i, got_i, where, "chip_sum_w_in"),
                                         _chip_sum(own_o, got_o, where, "chip_sum_w_out"))

    small = _all_reduce_small(_pack_small(D, loss, dgain, dlogits, dgn, dfg))
    grad_norm_gain = small[0:1, :]
    grad_final_gain = small[1:2, :]
    grad_lb_logits = small[2:4, :SEG]
    grad_hgrn_gnorm = small[4:5, :HGRN_HEAD]
    loss_sum = small[5, 0]

    d_ng, m_ng, v_ng = _adamw(norm_gain, grad_norm_gain, m_norm_gain, v_norm_gain, "adamw_norm_gain")
    d_wi, m_wi, v_wi = _adamw(w_in[0], grad_w_in, m_w_in[0], v_w_in[0], "adamw_w_in")
    d_lb, m_lb, v_lb = _adamw(lb_logits, grad_lb_logits, m_lb_logits, v_lb_logits, "adamw_lb_logits")
    d_gn, m_gn, v_gn = _adamw(hgrn_gnorm, grad_hgrn_gnorm, m_hgrn_gnorm, v_hgrn_gnorm, "adamw_hgrn_gnorm")
    d_wo, m_wo, v_wo = _adamw(w_out[0], grad_w_out, m_w_out[0], v_w_out[0], "adamw_w_out")
    d_fg, m_fg, v_fg = _adamw(fgain, grad_final_gain, m_final_gain.reshape(1, D), v_final_gain.reshape(1, D),
                              "adamw_final_gain")

    return (loss_sum, grad_x[None],
            grad_norm_gain, grad_w_in[None], grad_lb_logits, grad_hgrn_gnorm, grad_w_out[None], grad_final_gain[0],
            d_ng, d_wi[None], d_lb, d_gn, d_wo[None], d_fg[0],
            m_ng, m_wi[None], m_lb, m_gn, m_wo[None], m_fg[0],
            v_ng, v_wi[None], v_lb, v_gn, v_wo[None], v_fg[0])
```

```python
import jax
import jax.numpy as jnp
import numpy as np
from jax import lax
from jax.experimental import pallas as pl
from jax.experimental.pallas import tpu as pltpu

F32 = jnp.float32
BF16 = jnp.bfloat16
MESH = pl.DeviceIdType.MESH

NORM_EPS = 1e-6
HGRN_HEAD = 128
HGRN_CHUNK = 64
HGRN_TILE = 128
HGRN_BLOCK = 512
ATTN_HEAD = 64
LANES = 128
BAND = 128
DILATIONS = (1, 4, 16)
ATTN_SCALE = ATTN_HEAD ** -0.5
assert ATTN_SCALE == 0.125
ATTN_BLOCK_ELEMS = BAND * 2048
ATTN_UNROLL = 4
NEG = -1e30

ADAM_LR = 0.001
ADAM_B1 = 0.9
ADAM_B2 = 0.999
ADAM_EPS = 1e-08
ADAM_WD = 0.01
ADAM_STEP = 10

MIB = 1024 * 1024


def _cp(semantics=None, vmem_mib=48):
    return pltpu.CompilerParams(dimension_semantics=semantics, vmem_limit_bytes=vmem_mib * MIB)


def _dot(a, b):
    return jnp.dot(a, b, preferred_element_type=F32)


def _dot_nt(a, b):
    return lax.dot_general(a, b, (((1,), (1,)), ((), ())), preferred_element_type=F32)


def _dot_tn(a, b):
    return lax.dot_general(a, b, (((0,), (0,)), ((), ())), preferred_element_type=F32)


def _split3(x):
    hi = x.astype(BF16)
    r1 = x - hi.astype(F32)
    mid = r1.astype(BF16)
    lo = (r1 - mid.astype(F32)).astype(BF16)
    return hi, mid, lo


def _exact_dot(t_bf16, x):
    hi, mid, lo = _split3(x)
    return _dot(t_bf16, hi) + _dot(t_bf16, mid) + _dot(t_bf16, lo)


def _exact_dot_right(x, t_bf16):
    hi, mid, lo = _split3(x)
    return _dot(hi, t_bf16) + _dot(mid, t_bf16) + _dot(lo, t_bf16)


def _sigmoid(z):
    return jax.nn.sigmoid(z)


def _silu_and_grad(z):
    s = _sigmoid(z)
    return z * s, s * (1.0 + z * (1.0 - s))


def _seg_select(j, values):
    out = values[0]
    for t, v in enumerate(values[1:], 1):
        out = jnp.where(j == t, v, out)
    return out


def _rms_fwd(x2, gain):
    S, D = x2.shape
    tm = min(512, S)

    def body(x_ref, g_ref, h_ref, r_ref):
        x = x_ref[...]
        r = lax.rsqrt(jnp.mean(x * x, axis=-1, keepdims=True) + NORM_EPS)
        h_ref[...] = ((x * r) * g_ref[...]).astype(BF16)
        r_ref[...] = r

    return pl.pallas_call(
        body, grid=(S // tm,), name="rms_fwd",
        in_specs=[pl.BlockSpec((tm, D), lambda i: (i, 0)), pl.BlockSpec((1, D), lambda i: (0, 0))],
        out_specs=[pl.BlockSpec((tm, D), lambda i: (i, 0)), pl.BlockSpec((tm, 1), lambda i: (i, 0))],
        out_shape=[jax.ShapeDtypeStruct((S, D), BF16), jax.ShapeDtypeStruct((S, 1), F32)],
        compiler_params=_cp(("parallel",)),
    )(x2, gain)


def _in_proj(h, w_all, segs, name):
    S, D = h.shape
    SEG = w_all.shape[2] // 2
    n = len(segs)
    tm = min(512, S)

    def body(h_ref, w_ref, o_ref):
        o_ref[...] = _dot(h_ref[...], w_ref[...])

    def w_map(j, i):
        seg = _seg_select(j, segs)
        return (seg // 2, 0, seg % 2)

    return pl.pallas_call(
        body, grid=(n, S // tm), name=name,
        in_specs=[pl.BlockSpec((tm, D), lambda j, i: (i, 0)), pl.BlockSpec((None, D, SEG), w_map)],
        out_specs=pl.BlockSpec((None, tm, SEG), lambda j, i: (j, i, 0)),
        out_shape=jax.ShapeDtypeStruct((n, S, SEG), F32),
        compiler_params=_cp(("parallel", "parallel")),
    )(h, w_all)


def _out_proj_loss(yh, ya, w_out, x2, tgt, fgain):
    S, D = x2.shape
    SEG = yh.shape[1]
    tm = min(256, S)

    def body(yh_ref, ya_ref, w_ref, x_ref, t_ref, fg_ref, dout_ref, doutb_ref, loss_ref, dfg_ref):
        i = pl.program_id(0)

        @pl.when(i == 0)
        def _():
            loss_ref[...] = jnp.zeros_like(loss_ref)
            dfg_ref[...] = jnp.zeros_like(dfg_ref)

        out = x_ref[...] + _dot(yh_ref[...], w_ref[pl.ds(0, SEG), :]) + _dot(ya_ref[...], w_ref[pl.ds(SEG, SEG), :])
        r = lax.rsqrt(jnp.mean(out * out, axis=-1, keepdims=True) + NORM_EPS)
        n = out * r
        fg = fg_ref[...]
        err = n * fg - t_ref[...]
        loss_ref[...] += 0.5 * jnp.sum(jnp.mean(err * err, axis=-1, keepdims=True), axis=0, keepdims=True)
        dy = err * (1.0 / D)
        dfg_ref[...] += jnp.sum(dy * n, axis=0, keepdims=True)
        dn = dy * fg
        dout = r * (dn - n * jnp.mean(dn * n, axis=-1, keepdims=True))
        dout_ref[...] = dout
        doutb_ref[...] = dout.astype(BF16)

    row = lambda i: (i, 0)
    fix = lambda i: (0, 0)
    return pl.pallas_call(
        body, grid=(S // tm,), name="out_proj_loss",
        in_specs=[pl.BlockSpec((tm, SEG), row), pl.BlockSpec((tm, SEG), row), pl.BlockSpec((2 * SEG, D), fix),
                  pl.BlockSpec((tm, D), row), pl.BlockSpec((tm, D), row), pl.BlockSpec((1, D), fix)],
        out_specs=[pl.BlockSpec((tm, D), row), pl.BlockSpec((tm, D), row), pl.BlockSpec((1, 1), fix),
                   pl.BlockSpec((1, D), fix)],
        out_shape=[jax.ShapeDtypeStruct((S, D), F32), jax.ShapeDtypeStruct((S, D), BF16),
                   jax.ShapeDtypeStruct((1, 1), F32), jax.ShapeDtypeStruct((1, D), F32)],
        compiler_params=_cp(("arbitrary",)),
    )(yh, ya, w_out, x2, tgt, fgain)


def _dy_proj(doutb, w_out):
    S, D = doutb.shape
    K = w_out.shape[0]
    tm = min(512, S)

    def body(d_ref, w_ref, o_ref):
        o_ref[...] = _dot_nt(d_ref[...], w_ref[...])

    return pl.pallas_call(
        body, grid=(S // tm,), name="dy_proj",
        in_specs=[pl.BlockSpec((tm, D), lambda i: (i, 0)), pl.BlockSpec((K, D), lambda i: (0, 0))],
        out_specs=pl.BlockSpec((tm, K), lambda i: (i, 0)),
        out_shape=jax.ShapeDtypeStruct((S, K), F32),
        compiler_params=_cp(("parallel",)),
    )(doutb, w_out)


def _grad_w_out(yh, ya, doutb):
    S, SEG = yh.shape
    D = doutb.shape[1]
    R = (2 * SEG) // 4
    nb_half = SEG // R
    tk = min(512, S)

    def body(yh_ref, ya_ref, d_ref, o_ref):
        q = pl.program_id(0)
        k = pl.program_id(1)

        @pl.when(k == 0)
        def _():
            o_ref[...] = jnp.zeros_like(o_ref)

        @pl.when(q < nb_half)
        def _():
            o_ref[...] += _dot_tn(yh_ref[...], d_ref[...])

        @pl.when(q >= nb_half)
        def _():
            o_ref[...] += _dot_tn(ya_ref[...], d_ref[...])

    return pl.pallas_call(
        body, grid=(4, S // tk), name="grad_w_out",
        in_specs=[pl.BlockSpec((tk, R), lambda q, k: (k, jnp.minimum(q, nb_half - 1))),
                  pl.BlockSpec((tk, R), lambda q, k: (k, jnp.maximum(q - nb_half, 0))),
                  pl.BlockSpec((tk, D), lambda q, k: (k, 0))],
        out_specs=pl.BlockSpec((None, R, D), lambda q, k: (q, 0, 0)),
        out_shape=jax.ShapeDtypeStruct((4, R, D), F32),
        compiler_params=_cp(("parallel", "arbitrary")),
    )(yh, ya, doutb)


def _dz_sources(sources):
    counts = [s.shape[0] for s in sources]
    starts = [sum(counts[:k]) for k in range(len(counts))]
    assert sum(counts) == 8
    return counts, starts


def _dh_proj(sources, w_all):
    S = sources[0].shape[1]
    D = w_all.shape[1]
    SEG = w_all.shape[2] // 2
    counts, starts = _dz_sources(sources)
    ns = len(sources)
    tm = min(512, S)

    def body(*refs):
        src = refs[:ns]
        w_ref, o_ref = refs[ns:]
        j = pl.program_id(1)

        @pl.when(j == 0)
        def _():
            o_ref[...] = jnp.zeros_like(o_ref)

        for k in range(ns):
            @pl.when((j >= starts[k]) & (j < starts[k] + counts[k]))
            def _(k=k):
                o_ref[...] += _dot_nt(src[k][...], w_ref[...])

    def src_spec(k):
        return pl.BlockSpec((None, tm, SEG),
                            lambda i, j: (jnp.clip(j - starts[k], 0, counts[k] - 1), i, 0))

    return pl.pallas_call(
        body, grid=(S // tm, 8), name="dh_proj",
        in_specs=[src_spec(k) for k in range(ns)] + [pl.BlockSpec((None, D, SEG), lambda i, j: (j // 2, 0, j % 2))],
        out_specs=pl.BlockSpec((tm, D), lambda i, j: (i, 0)),
        out_shape=jax.ShapeDtypeStruct((S, D), F32),
        compiler_params=_cp(("parallel", "arbitrary")),
    )(*sources, w_all)


def _rms_bwd(dh, x2, rinv, gain, dout):
    S, D = x2.shape
    tm = min(256, S)

    def body(dh_ref, x_ref, r_ref, g_ref, dout_ref, gx_ref, dg_ref):
        @pl.when(pl.program_id(0) == 0)
        def _():
            dg_ref[...] = jnp.zeros_like(dg_ref)

        dh = dh_ref[...]
        r = r_ref[...]
        xhat = x_ref[...] * r
        dg_ref[...] += jnp.sum(dh * xhat, axis=0, keepdims=True)
        dxn = dh * g_ref[...]
        gx_ref[...] = dout_ref[...] + r * (dxn - xhat * jnp.mean(dxn * xhat, axis=-1, keepdims=True))

    row = lambda i: (i, 0)
    fix = lambda i: (0, 0)
    return pl.pallas_call(
        body, grid=(S // tm,), name="rms_bwd",
        in_specs=[pl.BlockSpec((tm, D), row), pl.BlockSpec((tm, D), row), pl.BlockSpec((tm, 1), row),
                  pl.BlockSpec((1, D), fix), pl.BlockSpec((tm, D), row)],
        out_specs=[pl.BlockSpec((tm, D), row), pl.BlockSpec((1, D), fix)],
        out_shape=[jax.ShapeDtypeStruct((S, D), F32), jax.ShapeDtypeStruct((1, D), F32)],
        compiler_params=_cp(("arbitrary",)),
    )(dh, x2, rinv, gain, dout)


def _grad_w_in(h, sources):
    S, D = h.shape
    SEG = sources[0].shape[2]
    counts, starts = _dz_sources(sources)
    ns = len(sources)
    tk = min(512, S)

    def body(*refs):
        h_ref = refs[0]
        src = refs[1:1 + ns]
        o_ref = refs[1 + ns]
        j = pl.program_id(0)
        k = pl.program_id(1)

        @pl.when(k == 0)
        def _():
            o_ref[...] = jnp.zeros_like(o_ref)

        for s in range(ns):
            @pl.when((j >= starts[s]) & (j < starts[s] + counts[s]))
            def _(s=s):
                o_ref[...] += _dot_tn(h_ref[...], src[s][...])

    def src_spec(s):
        return pl.BlockSpec((None, tk, SEG),
                            lambda j, k: (jnp.clip(j - starts[s], 0, counts[s] - 1), k, 0))

    return pl.pallas_call(
        body, grid=(8, S // tk), name="grad_w_in",
        in_specs=[pl.BlockSpec((tk, D), lambda j, k: (k, 0))] + [src_spec(s) for s in range(ns)],
        out_specs=pl.BlockSpec((None, D, SEG), lambda j, k: (j // 2, 0, j % 2)),
        out_shape=jax.ShapeDtypeStruct((4, D, 2 * SEG), F32),
        compiler_params=_cp(("parallel", "arbitrary")),
    )(h, *sources)


def _lower_bound(lbl):
    l0 = lbl[0:1, :]
    l1 = lbl[1:2, :]
    m = jnp.maximum(l0, l1)
    e0 = jnp.exp(l0 - m)
    e1 = jnp.exp(l1 - m)
    return e0 / (e0 + e1)


def _tile_masks():
    row = lax.broadcasted_iota(jnp.int32, (HGRN_TILE, HGRN_TILE), 0)
    col = lax.broadcasted_iota(jnp.int32, (HGRN_TILE, HGRN_TILE), 1)
    same = (row // HGRN_CHUNK) == (col // HGRN_CHUNK)
    return same & (row >= col), same & (row <= col)


def _chunk_last(b):
    T = b.shape[0]
    b3 = b.reshape(T // HGRN_CHUNK, HGRN_CHUNK, HGRN_HEAD)
    return jnp.broadcast_to(b3[:, HGRN_CHUNK - 1:HGRN_CHUNK, :], b3.shape).reshape(T, HGRN_HEAD)


def _chunk_sum(x):
    T = x.shape[0]
    x3 = x.reshape(T // HGRN_CHUNK, HGRN_CHUNK, HGRN_HEAD)
    return jnp.broadcast_to(jnp.sum(x3, axis=1, keepdims=True), x3.shape).reshape(T, HGRN_HEAD)


def _hgrn_dims(S, SEG):
    T = min(HGRN_BLOCK, S)
    assert S % T == 0 and T % HGRN_TILE == 0
    tiles = [slice(t * HGRN_TILE, (t + 1) * HGRN_TILE) for t in range(T // HGRN_TILE)]
    chunks = [slice(c * HGRN_CHUNK, (c + 1) * HGRN_CHUNK) for c in range(T // HGRN_CHUNK)]
    return SEG // HGRN_HEAD, T, T // HGRN_CHUNK, S // T, tiles, chunks


def _hgrn_fwd(zf32, lb_logits, gnorm):
    _, S, SEG = zf32.shape
    H, T, NC, NJ, tiles, chunks = _hgrn_dims(S, SEG)

    def body(zq_ref, zf_ref, zi_ref, zg_ref, lbl_ref, gn_ref, y_ref, st_ref, state):
        @pl.when(pl.program_id(1) == 0)
        def _():
            state[...] = jnp.zeros_like(state)

        lb = _lower_bound(lbl_ref[...])
        tril, _ = _tile_masks()
        tril_bf = tril.astype(BF16)
        zq = zq_ref[...]
        q = zq * _sigmoid(zq)
        f = lb + (1.0 - lb) * _sigmoid(zf_ref[...])
        k = 1.0 - f
        logf = jnp.log(f)
        b = jnp.concatenate([_exact_dot(tril_bf, logf[t]) for t in tiles], axis=0)
        bl = _chunk_last(b)
        qd_b = (q * jnp.exp(b)).astype(BF16)
        kd_b = (k * jnp.exp(-b)).astype(BF16)
        ke_b = (k * jnp.exp(bl - b)).astype(BF16)
        v_b = zi_ref[...].astype(BF16)
        o_intra = jnp.concatenate(
            [_dot(jnp.where(tril, _dot_nt(qd_b[t], kd_b[t]), 0.0).astype(BF16), v_b[t]) for t in tiles], axis=0)
        kvs = [_dot_tn(v_b[r], ke_b[r]) for r in chunks]
        ebl = jnp.exp(bl)
        st = state[...]
        sts = []
        for c in range(NC):
            st_ref[c] = st
            sts.append(st.astype(BF16))
            st = st * ebl[c * HGRN_CHUNK:c * HGRN_CHUNK + 1, :] + kvs[c]
        state[...] = st
        o = o_intra + jnp.concatenate([_dot_nt(qd_b[r], sb) for r, sb in zip(chunks, sts)], axis=0)
        on = o * lax.rsqrt(jnp.mean(o * o, axis=-1, keepdims=True) + NORM_EPS) * gn_ref[...]
        zg = zg_ref[...]
        y_ref[...] = (on * (zg * _sigmoid(zg))).astype(BF16)

    def zspec(seg):
        return pl.BlockSpec((None, T, HGRN_HEAD), lambda h, j: (seg, j, h))

    return pl.pallas_call(
        body, grid=(H, NJ), name="hgrn_fwd",
        in_specs=[zspec(0), zspec(1), zspec(2), zspec(3),
                  pl.BlockSpec((2, HGRN_HEAD), lambda h, j: (0, h)),
                  pl.BlockSpec((1, HGRN_HEAD), lambda h, j: (0, 0))],
        out_specs=[pl.BlockSpec((T, HGRN_HEAD), lambda h, j: (j, h)),
                   pl.BlockSpec((NC, None, HGRN_HEAD, HGRN_HEAD), lambda h, j: (j, h, 0, 0))],
        out_shape=[jax.ShapeDtypeStruct((S, SEG), BF16),
                   jax.ShapeDtypeStruct((S // HGRN_CHUNK, H, HGRN_HEAD, HGRN_HEAD), F32)],
        scratch_shapes=[pltpu.VMEM((HGRN_HEAD, HGRN_HEAD), F32)],
        compiler_params=_cp(("parallel", "arbitrary")),
    )(zf32, zf32, zf32, zf32, lb_logits, gnorm)


def _hgrn_bwd(zf32, lb_logits, gnorm, states, dy):
    _, S, SEG = zf32.shape
    H, T, NC, NJ, tiles, chunks = _hgrn_dims(S, SEG)
    C = HGRN_CHUNK

    def body(zq_ref, zf_ref, zi_ref, zg_ref, lbl_ref, gn_ref, st_ref, dy_ref, dz_ref, dl_ref, dgn_ref, gstate):
        @pl.when(pl.program_id(1) == 0)
        def _():
            gstate[...] = jnp.zeros_like(gstate)
            dl_ref[...] = jnp.zeros_like(dl_ref)
            dgn_ref[...] = jnp.zeros_like(dgn_ref)

        lb = _lower_bound(lbl_ref[...])
        gn = gn_ref[...]
        tril, triu = _tile_masks()
        tril_bf = tril.astype(BF16)
        triu_bf = triu.astype(BF16)
        q, dq_dz = _silu_and_grad(zq_ref[...])
        sf = _sigmoid(zf_ref[...])
        f = lb + (1.0 - lb) * sf
        k = 1.0 - f
        logf = jnp.log(f)
        b = jnp.concatenate([_exact_dot(tril_bf, logf[t]) for t in tiles], axis=0)
        bl = _chunk_last(b)
        eb = jnp.exp(b)
        enb = jnp.exp(-b)
        ekl = jnp.exp(bl - b)
        ebl = jnp.exp(bl)
        qd = q * eb
        kd = k * enb
        ke = k * ekl
        qd_b = qd.astype(BF16)
        kd_b = kd.astype(BF16)
        ke_b = ke.astype(BF16)
        v_b = zi_ref[...].astype(BF16)
        sts = [st_ref[c] for c in range(NC)]
        sts_b = [s.astype(BF16) for s in sts]
        a_b = [jnp.where(tril, _dot_nt(qd_b[t], kd_b[t]), 0.0).astype(BF16) for t in tiles]
        o = (jnp.concatenate([_dot(a, v_b[t]) for a, t in zip(a_b, tiles)], axis=0)
             + jnp.concatenate([_dot_nt(qd_b[r], sb) for r, sb in zip(chunks, sts_b)], axis=0))
        rinv = lax.rsqrt(jnp.mean(o * o, axis=-1, keepdims=True) + NORM_EPS)
        ohat = o * rinv
        sg, dsg = _silu_and_grad(zg_ref[...])
        dyv = dy_ref[...]
        don = dyv * sg
        dz_ref[3] = (dyv * (ohat * gn) * dsg).astype(BF16)
        dgn_ref[...] += jnp.sum(don * ohat, axis=0, keepdims=True)
        dohat = don * gn
        do = rinv * (dohat - ohat * jnp.mean(dohat * ohat, axis=-1, keepdims=True))
        do_b = do.astype(BF16)
        da_b = [jnp.where(tril, _dot_nt(do_b[t], v_b[t]), 0.0).astype(BF16) for t in tiles]
        dv_intra = jnp.concatenate([_dot_tn(a, do_b[t]) for a, t in zip(a_b, tiles)], axis=0)
        dqd_intra = jnp.concatenate([_dot(da, kd_b[t]) for da, t in zip(da_b, tiles)], axis=0)
        dkd = jnp.concatenate([_dot_tn(da, qd_b[t]) for da, t in zip(da_b, tiles)], axis=0)
        dqd_inter = jnp.concatenate([_dot(do_b[r], sb) for r, sb in zip(chunks, sts_b)], axis=0)
        gks = [_dot_tn(do_b[r], qd_b[r]) for r in chunks]
        g = gstate[...]
        gs = [None] * NC
        for c in reversed(range(NC)):
            gs[c] = g
            g = g * ebl[c * C:c * C + 1, :] + gks[c]
        gstate[...] = g
        gs_b = [x.astype(BF16) for x in gs]
        dv = dv_intra + jnp.concatenate([_dot_nt(ke_b[r], gb) for r, gb in zip(chunks, gs_b)], axis=0)
        dz_ref[2] = dv.astype(BF16)
        dke = jnp.concatenate([_dot(v_b[r], gb) for r, gb in zip(chunks, gs_b)], axis=0)
        debl = jnp.concatenate(
            [jnp.broadcast_to(jnp.sum(x * s, axis=0, keepdims=True), (C, HGRN_HEAD)) for x, s in zip(gs, sts)], axis=0)
        dqd = dqd_intra + dqd_inter
        dz_ref[0] = ((dqd * eb) * dq_dz).astype(BF16)
        t_ke = dke * ke
        db = dqd * qd - dkd * kd - t_ke
        db_last = _chunk_sum(t_ke) + debl * ebl
        dk = dkd * enb + dke * ekl
        dlogf = jnp.concatenate([_exact_dot(triu_bf, db[t]) for t in tiles], axis=0) + db_last
        df = dlogf / f - dk
        dz_ref[1] = (df * (1.0 - lb) * (sf * (1.0 - sf))).astype(BF16)
        dlb = jnp.sum(df * (1.0 - sf), axis=0, keepdims=True)
        dl0 = dlb * lb * (1.0 - lb)
        dl_ref[0:1, :] += dl0
        dl_ref[1:2, :] -= dl0

    def zspec(seg):
        return pl.BlockSpec((None, T, HGRN_HEAD), lambda h, j: (seg, NJ - 1 - j, h))

    return pl.pallas_call(
        body, grid=(H, NJ), name="hgrn_bwd",
        in_specs=[zspec(0), zspec(1), zspec(2), zspec(3),
                  pl.BlockSpec((2, HGRN_HEAD), lambda h, j: (0, h)),
                  pl.BlockSpec((1, HGRN_HEAD), lambda h, j: (0, 0)),
                  pl.BlockSpec((NC, None, HGRN_HEAD, HGRN_HEAD), lambda h, j: (NJ - 1 - j, h, 0, 0)),
                  pl.BlockSpec((T, HGRN_HEAD), lambda h, j: (NJ - 1 - j, h))],
        out_specs=[pl.BlockSpec((4, T, HGRN_HEAD), lambda h, j: (0, NJ - 1 - j, h)),
                   pl.BlockSpec((2, HGRN_HEAD), lambda h, j: (0, h)),
                   pl.BlockSpec((None, 1, HGRN_HEAD), lambda h, j: (h, 0, 0))],
        out_shape=[jax.ShapeDtypeStruct((4, S, SEG), BF16), jax.ShapeDtypeStruct((2, SEG), F32),
                   jax.ShapeDtypeStruct((H, 1, HGRN_HEAD), F32)],
        scratch_shapes=[pltpu.VMEM((HGRN_HEAD, HGRN_HEAD), F32)],
        compiler_params=_cp(("parallel", "arbitrary")),
    )(zf32, zf32, zf32, zf32, lb_logits, gnorm, states, dy)


def _alibi_slopes(seg):
    n_heads = seg // ATTN_HEAD
    s = 2.0 ** (-8.0 * np.arange(1, n_heads + 1, dtype=np.float64) / n_heads)
    return jnp.asarray(np.repeat(s, ATTN_HEAD)[None, :], F32)


def _attn_dims(S, SEG, d):
    rb = BAND * d
    assert S % rb == 0 and SEG % LANES == 0
    cb = min(SEG, ATTN_BLOCK_ELEMS // rb) if d == 1 else LANES
    assert SEG % cb == 0
    return rb, cb, S // rb, SEG // cb


def _res_rows(r, d):
    return pl.ds(0, BAND) if d == 1 else pl.ds(r, BAND, stride=d)


def _for_residues(d, fn):
    if d == 1:
        fn(0)
    else:
        def step(r, carry):
            fn(r)
            return carry
        lax.fori_loop(0, d, step, 0, unroll=ATTN_UNROLL)


def _band_terms(n, d):
    i = lax.broadcasted_iota(jnp.int32, (BAND, 2 * BAND), 0)
    jj = lax.broadcasted_iota(jnp.int32, (BAND, 2 * BAND), 1)
    delta = BAND + i - jj
    valid = (delta >= 0) & (delta <= BAND) & ((n > 0) | (jj >= BAND))
    return (-d * delta).astype(F32), valid


def _head_biases(slopes, nd, valid):
    out = []
    for s in _per_head(slopes):
        s2 = jnp.concatenate([s, s], axis=1)
        out.append(jnp.where(valid, s2 * nd, NEG))
    return jnp.concatenate(out, axis=0)


def _stack_heads(x):
    lane = lax.broadcasted_iota(jnp.int32, x.shape, 1)
    zero = jnp.zeros_like(x)
    return jnp.concatenate([jnp.where(lane < ATTN_HEAD, x, zero), jnp.where(lane < ATTN_HEAD, zero, x)], axis=0)


def _unstack_heads(x2):
    first = lax.broadcasted_iota(jnp.int32, (BAND, LANES), 1) < ATTN_HEAD
    return jnp.where(first, x2[:BAND], x2[BAND:])


def _stack_per_head(x):
    a, b = _per_head(x)
    col = jnp.concatenate([a, b], axis=0)
    return jnp.concatenate([col, col], axis=1)


def _per_head(x):
    lane = lax.broadcasted_iota(jnp.int32, x.shape, 1)
    sw = pltpu.roll(x, ATTN_HEAD, 1)
    first = lane < ATTN_HEAD
    return jnp.where(first, x, sw), jnp.where(first, sw, x)


def _attn_fwd(qkv, slopes, d):
    _, S, SEG = qkv.shape
    rb, cb, nb, ncb = _attn_dims(S, SEG, d)
    NP = cb // LANES

    def body(q_ref, kp_ref, kc_ref, vp_ref, vc_ref, sl_ref, o_ref, l_ref):
        n = pl.program_id(1)
        nd, valid = _band_terms(n, d)
        biases = [_head_biases(sl_ref[:, p * LANES:(p + 1) * LANES], nd, valid) for p in range(NP)]

        def residue(r):
            rows = _res_rows(r, d)
            for p in range(NP):
                cols = slice(p * LANES, (p + 1) * LANES)
                kc = jnp.concatenate([kp_ref[rows, cols], kc_ref[rows, cols]], axis=0).astype(BF16)
                vc = jnp.concatenate([vp_ref[rows, cols], vc_ref[rows, cols]], axis=0).astype(BF16)
                s = _dot_nt(_stack_heads((q_ref[rows, cols] * ATTN_SCALE).astype(BF16)), kc) + biases[p]
                m = jnp.max(s, axis=-1, keepdims=True)
                e = jnp.exp(s - m)
                den = jnp.sum(e, axis=-1, keepdims=True)
                o_ref[rows, cols] = _unstack_heads(_dot(e.astype(BF16), vc) / den)
                l_ref[rows, cols] = _unstack_heads(jnp.broadcast_to(m + jnp.log(den), (2 * BAND, LANES)))

        _for_residues(d, residue)

    def spec(seg, prev):
        if prev:
            return pl.BlockSpec((None, rb, cb), lambda c, n: (seg, jnp.maximum(n - 1, 0), c))
        return pl.BlockSpec((None, rb, cb), lambda c, n: (seg, n, c))

    out = pl.BlockSpec((rb, cb), lambda c, n: (n, c))
    return pl.pallas_call(
        body, grid=(ncb, nb), name=f"attn_fwd_d{d}",
        in_specs=[spec(0, False), spec(1, True), spec(1, False), spec(2, True), spec(2, False),
                  pl.BlockSpec((1, cb), lambda c, n: (0, c))],
        out_specs=[out, out],
        out_shape=[jax.ShapeDtypeStruct((S, SEG), F32), jax.ShapeDtypeStruct((S, SEG), F32)],
        compiler_params=_cp(("parallel", "parallel")),
    )(qkv, qkv, qkv, qkv, qkv, slopes)


def _attn_merge(outs, lses, zf32):
    S, SEG = outs[0].shape
    tm = min(256, S)

    def body(o1, o2, o3, l1, l2, l3, zg_ref, o_ref, lse_ref, y_ref):
        a, b, c = l1[...], l2[...], l3[...]
        m = jnp.maximum(jnp.maximum(a, b), c)
        ea, eb, ec = jnp.exp(a - m), jnp.exp(b - m), jnp.exp(c - m)
        tot = ea + eb + ec
        o = (ea / tot) * o1[...] + (eb / tot) * o2[...] + (ec / tot) * o3[...]
        o_ref[...] = o
        lse_ref[...] = m + jnp.log(tot)
        zg = zg_ref[...]
        y_ref[...] = (o * (zg * _sigmoid(zg))).astype(BF16)

    row = pl.BlockSpec((tm, SEG), lambda i: (i, 0))
    return pl.pallas_call(
        body, grid=(S // tm,), name="attn_merge",
        in_specs=[row] * 6 + [pl.BlockSpec((None, tm, SEG), lambda i: (4, i, 0))],
        out_specs=[row, row, row],
        out_shape=[jax.ShapeDtypeStruct((S, SEG), F32), jax.ShapeDtypeStruct((S, SEG), F32),
                   jax.ShapeDtypeStruct((S, SEG), BF16)],
        compiler_params=_cp(("parallel",)),
    )(*outs, *lses, zf32)


def _attn_gate_bwd(dy, o, zf32):
    S, SEG = o.shape
    tm = min(256, S)
    NP = SEG // LANES

    def body(dy_ref, o_ref, zg_ref, do_ref, dl_ref, dzg_ref):
        r = lax.broadcasted_iota(jnp.int32, (LANES, LANES), 0) // ATTN_HEAD
        c = lax.broadcasted_iota(jnp.int32, (LANES, LANES), 1) // ATTN_HEAD
        same_head = (r == c).astype(BF16)
        for p in range(NP):
            cols = slice(p * LANES, (p + 1) * LANES)
            sg, dsg = _silu_and_grad(zg_ref[:, cols])
            dyv = dy_ref[:, cols]
            ov = o_ref[:, cols]
            do = dyv * sg
            do_ref[:, cols] = do
            dzg_ref[:, cols] = (dyv * ov * dsg).astype(BF16)
            dl_ref[:, cols] = _exact_dot_right(do * ov, same_head)

    return pl.pallas_call(
        body, grid=(S // tm,), name="attn_gate_bwd",
        in_specs=[pl.BlockSpec((tm, SEG), lambda i: (i, 1)), pl.BlockSpec((tm, SEG), lambda i: (i, 0)),
                  pl.BlockSpec((None, tm, SEG), lambda i: (4, i, 0))],
        out_specs=[pl.BlockSpec((tm, SEG), lambda i: (i, 0)), pl.BlockSpec((tm, SEG), lambda i: (i, 0)),
                   pl.BlockSpec((None, tm, SEG), lambda i: (0, i, 0))],
        out_shape=[jax.ShapeDtypeStruct((S, SEG), F32), jax.ShapeDtypeStruct((S, SEG), F32),
                   jax.ShapeDtypeStruct((1, S, SEG), BF16)],
        compiler_params=_cp(("parallel",)),
    )(dy, o, zf32)


def _attn_bwd(qkv, slopes, do, lse, dl, d, acc, out_dtype):
    _, S, SEG = qkv.shape
    rb, cb, nb, ncb = _attn_dims(S, SEG, d)
    NP = cb // LANES
    has_acc = acc is not None

    def body(*refs):
        q_ref, kp_ref, kc_ref, vp_ref, vc_ref, sl_ref, do_ref, lse_ref, dl_ref = refs[:9]
        refs = refs[9:]
        if has_acc:
            aq_ref, ak_ref, av_ref = refs[:3]
            refs = refs[3:]
        dq_ref, dk_ref, dv_ref, ck, cv = refs
        n = pl.program_id(1)

        @pl.when(n == 0)
        def _():
            ck[...] = jnp.zeros_like(ck)
            cv[...] = jnp.zeros_like(cv)

        @pl.when(n < nb)
        def _():
            nd, valid = _band_terms(n, d)
            biases = [_head_biases(sl_ref[:, p * LANES:(p + 1) * LANES], nd, valid) for p in range(NP)]

            def residue(r):
                rows = _res_rows(r, d)
                for p in range(NP):
                    cols = slice(p * LANES, (p + 1) * LANES)
                    kc = jnp.concatenate([kp_ref[rows, cols], kc_ref[rows, cols]], axis=0).astype(BF16)
                    vc = jnp.concatenate([vp_ref[rows, cols], vc_ref[rows, cols]], axis=0).astype(BF16)
                    qs = _stack_heads((q_ref[rows, cols] * ATTN_SCALE).astype(BF16))
                    dos = _stack_heads(do_ref[rows, cols].astype(BF16))
                    pr = jnp.exp(_dot_nt(qs, kc) + biases[p] - _stack_per_head(lse_ref[rows, cols]))
                    ds = (pr * (_dot_nt(dos, vc) - _stack_per_head(dl_ref[rows, cols]))).astype(BF16)
                    dq = _unstack_heads(_dot(ds, kc)) * ATTN_SCALE
                    dk = _dot_tn(ds, qs)
                    dv = _dot_tn(pr.astype(BF16), dos)
                    dk_prev = ck[r, :, cols] + dk[:BAND, :]
                    dv_prev = cv[r, :, cols] + dv[:BAND, :]
                    if has_acc:
                        dq = dq + aq_ref[rows, cols]
                        dk_prev = dk_prev + ak_ref[rows, cols]
                        dv_prev = dv_prev + av_ref[rows, cols]
                    dq_ref[rows, cols] = dq.astype(out_dtype)
                    dk_ref[rows, cols] = dk_prev.astype(out_dtype)
                    dv_ref[rows, cols] = dv_prev.astype(out_dtype)
                    ck[r, :, cols] = dk[BAND:, :]
                    cv[r, :, cols] = dv[BAND:, :]

            _for_residues(d, residue)

        @pl.when(n == nb)
        def _():
            def residue(r):
                rows = _res_rows(r, d)
                dk_last = ck[r]
                dv_last = cv[r]
                if has_acc:
                    dk_last = dk_last + ak_ref[rows, :]
                    dv_last = dv_last + av_ref[rows, :]
                dk_ref[rows, :] = dk_last.astype(out_dtype)
                dv_ref[rows, :] = dv_last.astype(out_dtype)

            _for_residues(d, residue)

    cur2 = lambda c, n: (jnp.minimum(n, nb - 1), c)
    cur3 = lambda c, n: (0, jnp.minimum(n, nb - 1), c)
    lag3 = lambda c, n: (0, jnp.clip(n - 1, 0, nb - 1), c)

    def spec(seg, prev):
        if prev:
            return pl.BlockSpec((None, rb, cb), lambda c, n: (seg, jnp.clip(n - 1, 0, nb - 1), c))
        return pl.BlockSpec((None, rb, cb), lambda c, n: (seg, jnp.minimum(n, nb - 1), c))

    in_specs = [spec(0, False), spec(1, True), spec(1, False), spec(2, True), spec(2, False),
                pl.BlockSpec((1, cb), lambda c, n: (0, c)),
                pl.BlockSpec((rb, cb), cur2), pl.BlockSpec((rb, cb), cur2), pl.BlockSpec((rb, cb), cur2)]
    args = [qkv, qkv, qkv, qkv, qkv, slopes, do, lse, dl]
    aliases = {}
    if has_acc:
        in_specs += [pl.BlockSpec((None, rb, cb), cur3), pl.BlockSpec((None, rb, cb), lag3),
                     pl.BlockSpec((None, rb, cb), lag3)]
        args += list(acc)
        if out_dtype == F32:
            aliases = {9: 0, 10: 1, 11: 2}
    return pl.pallas_call(
        body, grid=(ncb, nb + 1), name=f"attn_bwd_d{d}",
        in_specs=in_specs,
        out_specs=[pl.BlockSpec((None, rb, cb), cur3), pl.BlockSpec((None, rb, cb), lag3),
                   pl.BlockSpec((None, rb, cb), lag3)],
        out_shape=[jax.ShapeDtypeStruct((1, S, SEG), out_dtype)] * 3,
        scratch_shapes=[pltpu.VMEM((d, BAND, cb), F32), pltpu.VMEM((d, BAND, cb), F32)],
        input_output_aliases=aliases,
        compiler_params=_cp(("parallel", "arbitrary")),
    )(*args)


def _adamw(w, g, m, v, name):
    R, C = w.shape
    tr = R if R <= 256 else 256
    assert R % tr == 0

    def body(w_ref, g_ref, m_ref, v_ref, d_ref, nm_ref, nv_ref):
        g = g_ref[...]
        nm = ADAM_B1 * m_ref[...] + (1.0 - ADAM_B1) * g
        nv = ADAM_B2 * v_ref[...] + (1.0 - ADAM_B2) * (g * g)
        m_hat = nm / (1.0 - ADAM_B1 ** ADAM_STEP)
        v_hat = nv / (1.0 - ADAM_B2 ** ADAM_STEP)
        d_ref[...] = -ADAM_LR * (m_hat / (jnp.sqrt(v_hat) + ADAM_EPS) + ADAM_WD * w_ref[...])
        nm_ref[...] = nm
        nv_ref[...] = nv

    blk = pl.BlockSpec((tr, C), lambda i: (i, 0))
    sds = jax.ShapeDtypeStruct((R, C), F32)
    return pl.pallas_call(
        body, grid=(R // tr,), name=name, in_specs=[blk] * 4, out_specs=[blk] * 3, out_shape=[sds] * 3,
        compiler_params=_cp(("parallel",)),
    )(w, g, m, v)


def _coords():
    return lax.axis_index("x"), lax.axis_index("y"), lax.axis_index("c")


def _other_chips(x, y):
    return [(1 - x, y), (x, 1 - y), (1 - x, 1 - y)]


ANY = pl.BlockSpec(memory_space=pl.ANY)


def _cast_into_slot(w, where, name):
    R, C = w.shape
    tr = min(256, R)

    def body(where_ref, w_ref, o_ref):
        o_ref[...] = w_ref[...].astype(BF16)

    grid_spec = pltpu.PrefetchScalarGridSpec(
        num_scalar_prefetch=1, grid=(R // tr,),
        in_specs=[pl.BlockSpec((tr, C), lambda i, w: (i, 0))],
        out_specs=pl.BlockSpec((None, tr, C), lambda i, w: (w[1], i, 0)))
    return pl.pallas_call(
        body, grid_spec=grid_spec, name=name, out_shape=jax.ShapeDtypeStruct((4, R, C), BF16),
        compiler_params=_cp(("parallel",)),
    )(where, w)


def _gather_weights(wia, woa):
    nrows = (wia.shape[1] // 2, woa.shape[1] // 2)

    def body(wi_in, wo_in, wia_ref, woa_ref, send_sems, recv_sems):
        x, y, c = _coords()
        me = 2 * x + y
        alls = (wia_ref, woa_ref)

        def half(a, chip, hc):
            return alls[a].at[chip, pl.ds(hc * nrows[a], nrows[a]), :]

        def remote(a, k, part, to):
            return pltpu.make_async_remote_copy(src_ref=part, dst_ref=part, send_sem=send_sems.at[2 * k + a],
                                                recv_sem=recv_sems.at[2 * k + a], device_id=to, device_id_type=MESH)

        started = []
        for k, (px, py) in enumerate(_other_chips(x, y)):
            for a in range(2):
                cp = remote(a, k, half(a, me, c), (px, py, c))
                cp.start()
                started.append(cp)
        for k, (px, py) in enumerate(_other_chips(x, y)):
            for a in range(2):
                got = half(a, 2 * px + py, c)
                remote(a, k, got, (px, py, c)).wait_recv()
                cp = remote(a, 3 + k, got, (x, y, 1 - c))
                cp.start()
                started.append(cp)
        for k, (px, py) in enumerate(_other_chips(x, y)):
            for a in range(2):
                remote(a, 3 + k, half(a, 2 * px + py, 1 - c), (x, y, 1 - c)).wait_recv()
        for cp in started:
            cp.wait_send()

    return pl.pallas_call(
        body, name="gather_weights", in_specs=[ANY, ANY], out_specs=[ANY, ANY],
        out_shape=[jax.ShapeDtypeStruct(wia.shape, wia.dtype), jax.ShapeDtypeStruct(woa.shape, woa.dtype)],
        scratch_shapes=[pltpu.SemaphoreType.DMA((12,)), pltpu.SemaphoreType.DMA((12,))],
        input_output_aliases={0: 0, 1: 1},
    )(wia, woa)


def _swap_halves(gi, go):
    n_i, n_o = gi.shape[1] // 2, go.shape[1] // 2

    def body(gi_ref, go_ref, si_ref, so_ref, send_sems, recv_sems):
        x, y, c = _coords()
        cps = []
        for a, (src, dst, nr) in enumerate(((gi_ref, si_ref, n_i), (go_ref, so_ref, n_o))):
            cp = pltpu.make_async_remote_copy(
                src_ref=src.at[:, pl.ds((1 - c) * nr, nr), :], dst_ref=dst, send_sem=send_sems.at[a],
                recv_sem=recv_sems.at[a], device_id=(x, y, 1 - c), device_id_type=MESH)
            cp.start()
            cps.append(cp)
        for cp in cps:
            cp.wait()

    return pl.pallas_call(
        body, name="swap_halves", in_specs=[ANY, ANY], out_specs=[ANY, ANY],
        out_shape=[jax.ShapeDtypeStruct((4, n_i, gi.shape[2]), F32), jax.ShapeDtypeStruct((4, n_o, go.shape[2]), F32)],
        scratch_shapes=[pltpu.SemaphoreType.DMA((2,)), pltpu.SemaphoreType.DMA((2,))],
    )(gi, go)


def _pair_sum(g, sib, where, name):
    _, n2, C = g.shape
    N = n2 // 2
    tr = min(256, N)
    nt = N // tr

    def body(where_ref, g_ref, s_ref, qb_ref, own_ref):
        q = pl.program_id(1)
        tot = g_ref[...] + s_ref[...]
        qb_ref[...] = tot.astype(BF16)

        @pl.when(q == where_ref[1])
        def _():
            own_ref[...] = tot

    grid_spec = pltpu.PrefetchScalarGridSpec(
        num_scalar_prefetch=1, grid=(nt, 4),
        in_specs=[pl.BlockSpec((None, tr, C), lambda i, q, w: (q, w[0] * nt + i, 0)),
                  pl.BlockSpec((None, tr, C), lambda i, q, w: (q, i, 0))],
        out_specs=[pl.BlockSpec((None, tr, C), lambda i, q, w: (q, i, 0)),
                   pl.BlockSpec((tr, C), lambda i, q, w: (i, 0))])
    return pl.pallas_call(
        body, grid_spec=grid_spec, name=name,
        out_shape=[jax.ShapeDtypeStruct((4, N, C), BF16), jax.ShapeDtypeStruct((N, C), F32)],
        compiler_params=_cp(("parallel", "arbitrary")),
    )(where, g, sib)


def _scatter_to_chips(qi, qo):
    def body(qi_ref, qo_ref, ri_ref, ro_ref, send_sems, recv_sems):
        x, y, c = _coords()
        cps = []
        for k, (px, py) in enumerate(_other_chips(x, y)):
            for a, (src, dst) in enumerate(((qi_ref, ri_ref), (qo_ref, ro_ref))):
                cp = pltpu.make_async_remote_copy(
                    src_ref=src.at[2 * px + py], dst_ref=dst.at[k], send_sem=send_sems.at[2 * k + a],
                    recv_sem=recv_sems.at[2 * k + a], device_id=(px, py, c), device_id_type=MESH)
                cp.start()
                cps.append(cp)
        for cp in cps:
            cp.wait()

    return pl.pallas_call(
        body, name="scatter_to_chips", in_specs=[ANY, ANY], out_specs=[ANY, ANY],
        out_shape=[jax.ShapeDtypeStruct((3,) + qi.shape[1:], BF16), jax.ShapeDtypeStruct((3,) + qo.shape[1:], BF16)],
        scratch_shapes=[pltpu.SemaphoreType.DMA((6,)), pltpu.SemaphoreType.DMA((6,))],
    )(qi, qo)


def _chip_sum(own, got, where, name):
    N, C = own.shape
    tr = min(256, N)
    nt = N // tr

    def body(where_ref, own_ref, got_ref, o_ref):
        t = own_ref[...]
        for k in range(3):
            t = t + got_ref[k].astype(F32)
        o_ref[...] = t

    grid_spec = pltpu.PrefetchScalarGridSpec(
        num_scalar_prefetch=1, grid=(nt,),
        in_specs=[pl.BlockSpec((tr, C), lambda i, w: (i, 0)), pl.BlockSpec((3, tr, C), lambda i, w: (0, i, 0))],
        out_specs=pl.BlockSpec((tr, C), lambda i, w: (w[0] * nt + i, 0)))
    return pl.pallas_call(
        body, grid_spec=grid_spec, name=name, out_shape=jax.ShapeDtypeStruct((2 * N, C), F32),
        compiler_params=_cp(("parallel",)),
    )(where, own, got)


def _join_halves(gi, go):
    def body(gi_in, go_in, gi_ref, go_ref, send_sems, recv_sems):
        x, y, c = _coords()
        cps = []
        for a, ref in enumerate((gi_ref, go_ref)):
            nr = ref.shape[0] // 2
            mine = ref.at[pl.ds(c * nr, nr), :]
            cp = pltpu.make_async_remote_copy(src_ref=mine, dst_ref=mine, send_sem=send_sems.at[a],
                                              recv_sem=recv_sems.at[a], device_id=(x, y, 1 - c), device_id_type=MESH)
            cp.start()
            cps.append(cp)
        for a, ref in enumerate((gi_ref, go_ref)):
            nr = ref.shape[0] // 2
            theirs = ref.at[pl.ds((1 - c) * nr, nr), :]
            pltpu.make_async_remote_copy(src_ref=theirs, dst_ref=theirs, send_sem=send_sems.at[a],
                                         recv_sem=recv_sems.at[a], device_id=(x, y, 1 - c),
                                         device_id_type=MESH).wait_recv()
        for cp in cps:
            cp.wait_send()

    return pl.pallas_call(
        body, name="join_halves", in_specs=[ANY, ANY], out_specs=[ANY, ANY],
        out_shape=[jax.ShapeDtypeStruct(gi.shape, F32), jax.ShapeDtypeStruct(go.shape, F32)],
        scratch_shapes=[pltpu.SemaphoreType.DMA((2,)), pltpu.SemaphoreType.DMA((2,))],
        input_output_aliases={0: 0, 1: 1},
    )(gi, go)


def _all_reduce_small(part):
    R, C = part.shape

    def body(p_ref, o_ref, slots, send_sems, recv_sems):
        x, y, c = _coords()
        me = 4 * x + 2 * y + c
        slots[me] = p_ref[...]
        cps = []
        for k in range(1, 8):
            fx, fy, fc = (k >> 2) & 1, (k >> 1) & 1, k & 1
            peer = (1 - x if fx else x, 1 - y if fy else y, 1 - c if fc else c)
            cp = pltpu.make_async_remote_copy(src_ref=p_ref, dst_ref=slots.at[me], send_sem=send_sems.at[k - 1],
                                              recv_sem=recv_sems.at[k - 1], device_id=peer, device_id_type=MESH)
            cp.start()
            cps.append(cp)
        for cp in cps:
            cp.wait()
        t = slots[0]
        for k in range(1, 8):
            t = t + slots[k]
        o_ref[...] = t

    vm = pl.BlockSpec(memory_space=pltpu.VMEM)
    return pl.pallas_call(
        body, name="all_reduce_small", in_specs=[vm], out_specs=vm,
        out_shape=jax.ShapeDtypeStruct((R, C), F32),
        scratch_shapes=[pltpu.VMEM((8, R, C), F32), pltpu.SemaphoreType.DMA((7,)), pltpu.SemaphoreType.DMA((7,))],
    )(part)


def _local_step(x2, tgt, norm_gain, w_all, lb_logits, hgrn_gnorm, w_out_all, fgain):
    S, D = x2.shape
    SEG = w_all.shape[2] // 2
    slopes = _alibi_slopes(SEG)
    h, rinv = _rms_fwd(x2, norm_gain)
    zf32 = _in_proj(h, w_all, (0, 1, 2, 3, 7), "in_proj_hgrn_gates")
    qkv = _in_proj(h, w_all, (4, 5, 6), "in_proj_qkv")
    yh, states = _hgrn_fwd(zf32, lb_logits, hgrn_gnorm)
    outs, lses = [], []
    for d in DILATIONS:
        o, l = _attn_fwd(qkv, slopes, d)
        outs.append(o)
        lses.append(l)
    o_attn, lse, ya = _attn_merge(outs, lses, zf32)
    dout, doutb, loss, dfg = _out_proj_loss(yh, ya, w_out_all, x2, tgt, fgain)
    dy = _dy_proj(doutb, w_out_all)
    g_w_out = _grad_w_out(yh, ya, doutb)
    dzh, dlogits, dgn = _hgrn_bwd(zf32, lb_logits, hgrn_gnorm, states, dy)
    do, dl, dzg = _attn_gate_bwd(dy, o_attn, zf32)
    acc = None
    order = sorted(DILATIONS, reverse=True)
    for k, d in enumerate(order):
        acc = _attn_bwd(qkv, slopes, do, lse, dl, d, acc, BF16 if k == len(order) - 1 else F32)
    sources = [dzh] + list(acc) + [dzg]
    grad_x, dgain = _rms_bwd(_dh_proj(sources, w_all), x2, rinv, norm_gain, dout)
    g_w_in = _grad_w_in(h, sources)
    return loss, grad_x, dgain, g_w_in, dlogits, dgn, g_w_out, dfg


def _pack_small(D, loss, dgain, dlogits, dgn, dfg):
    def row(v):
        v = v.reshape(1, -1)
        return jnp.pad(v, ((0, 0), (0, D - v.shape[1])))
    rows = [row(dgain), row(dfg), row(dlogits[0]), row(dlogits[1]), row(jnp.sum(dgn, axis=0)), row(loss)]
    rows += [jnp.zeros((1, D), F32)] * (8 - len(rows))
    return jnp.concatenate(rows, axis=0)


def kernel(x, norm_gain, w_in, lb_logits, hgrn_gnorm, w_out, final_gain, loss_target, m_norm_gain, m_w_in, m_lb_logits, m_hgrn_gnorm, m_w_out, m_final_gain, v_norm_gain, v_w_in, v_lb_logits, v_hgrn_gnorm, v_w_out, v_final_gain):
    _, S, D = x.shape
    SEG = w_in.shape[2] // 2
    x2 = x[0]
    tgt = loss_target[0]
    fgain = final_gain.reshape(1, D)
    where = jnp.stack([lax.axis_index("c"), 2 * lax.axis_index("x") + lax.axis_index("y")]).astype(jnp.int32)

    w_all, w_out_all = _gather_weights(_cast_into_slot(w_in[0], where, "cast_w_in"),
                                       _cast_into_slot(w_out[0], where, "cast_w_out"))
    w_out_all = w_out_all.reshape(2 * SEG, D)

    loss, grad_x, dgain, g_w_in, dlogits, dgn, g_w_out, dfg = _local_step(
        x2, tgt, norm_gain, w_all, lb_logits, hgrn_gnorm, w_out_all, fgain)

    sib_i, sib_o = _swap_halves(g_w_in, g_w_out)
    qi, own_i = _pair_sum(g_w_in, sib_i, where, "pair_sum_w_in")
    qo, own_o = _pair_sum(g_w_out, sib_o, where, "pair_sum_w_out")
    got_i, got_o = _scatter_to_chips(qi, qo)
    grad_w_in, grad_w_out = _join_halves(_chip_sum(own_i, got_i, where, "chip_sum_w_in"),
                                         _chip_sum(own_o, got_o, where, "chip_sum_w_out"))

    small = _all_reduce_small(_pack_small(D, loss, dgain, dlogits, dgn, dfg))
    grad_norm_gain = small[0:1, :]
    grad_final_gain = small[1:2, :]
    grad_lb_logits = small[2:4, :SEG]
    grad_hgrn_gnorm = small[4:5, :HGRN_HEAD]
    loss_sum = small[5, 0]

    d_ng, m_ng, v_ng = _adamw(norm_gain, grad_norm_gain, m_norm_gain, v_norm_gain, "adamw_norm_gain")
    d_wi, m_wi, v_wi = _adamw(w_in[0], grad_w_in, m_w_in[0], v_w_in[0], "adamw_w_in")
    d_lb, m_lb, v_lb = _adamw(lb_logits, grad_lb_logits, m_lb_logits, v_lb_logits, "adamw_lb_logits")
    d_gn, m_gn, v_gn = _adamw(hgrn_gnorm, grad_hgrn_gnorm, m_hgrn_gnorm, v_hgrn_gnorm, "adamw_hgrn_gnorm")
    d_wo, m_wo, v_wo = _adamw(w_out[0], grad_w_out, m_w_out[0], v_w_out[0], "adamw_w_out")
    d_fg, m_fg, v_fg = _adamw(fgain, grad_final_gain, m_final_gain.reshape(1, D), v_final_gain.reshape(1, D),
                              "adamw_final_gain")

    return (loss_sum, grad_x[None],
            grad_norm_gain, grad_w_in[None], grad_lb_logits, grad_hgrn_gnorm, grad_w_out[None], grad_final_gain[0],
            d_ng, d_wi[None], d_lb, d_gn, d_wo[None], d_fg[0],
            m_ng, m_wi[None], m_lb, m_gn, m_wo[None], m_fg[0],
            v_ng, v_wi[None], v_lb, v_gn, v_wo[None], v_fg[0])
```

```python
import jax
import jax.numpy as jnp
import numpy as np
from jax import lax
from jax.experimental import pallas as pl
from jax.experimental.pallas import tpu as pltpu

F32 = jnp.float32
BF16 = jnp.bfloat16
MESH = pl.DeviceIdType.MESH

NORM_EPS = 1e-6
HGRN_HEAD = 128
HGRN_CHUNK = 64
HGRN_TILE = 128
HGRN_BLOCK = 512
ATTN_HEAD = 64
LANES = 128
BAND = 128
DILATIONS = (1, 4, 16)
ATTN_SCALE = ATTN_HEAD ** -0.5
assert ATTN_SCALE == 0.125
ATTN_BLOCK_ELEMS = BAND * 2048
ATTN_UNROLL = 4
NEG = -1e30

ADAM_LR = 0.001
ADAM_B1 = 0.9
ADAM_B2 = 0.999
ADAM_EPS = 1e-08
ADAM_WD = 0.01
ADAM_STEP = 10

MIB = 1024 * 1024


def _cp(semantics=None, vmem_mib=48):
    return pltpu.CompilerParams(dimension_semantics=semantics, vmem_limit_bytes=vmem_mib * MIB)


def _dot(a, b):
    return jnp.dot(a, b, preferred_element_type=F32)


def _dot_nt(a, b):
    return lax.dot_general(a, b, (((1,), (1,)), ((), ())), preferred_element_type=F32)


def _dot_tn(a, b):
    return lax.dot_general(a, b, (((0,), (0,)), ((), ())), preferred_element_type=F32)


def _split3(x):
    hi = x.astype(BF16)
    r1 = x - hi.astype(F32)
    mid = r1.astype(BF16)
    lo = (r1 - mid.astype(F32)).astype(BF16)
    return hi, mid, lo


def _exact_dot(t_bf16, x):
    hi, mid, lo = _split3(x)
    return _dot(t_bf16, hi) + _dot(t_bf16, mid) + _dot(t_bf16, lo)


def _exact_dot_right(x, t_bf16):
    hi, mid, lo = _split3(x)
    return _dot(hi, t_bf16) + _dot(mid, t_bf16) + _dot(lo, t_bf16)


def _sigmoid(z):
    return jax.nn.sigmoid(z)


def _silu_and_grad(z):
    s = _sigmoid(z)
    return z * s, s * (1.0 + z * (1.0 - s))


def _seg_select(j, values):
    out = values[0]
    for t, v in enumerate(values[1:], 1):
        out = jnp.where(j == t, v, out)
    return out


def _rms_fwd(x2, gain):
    S, D = x2.shape
    tm = min(512, S)

    def body(x_ref, g_ref, h_ref, r_ref):
        x = x_ref[...]
        r = lax.rsqrt(jnp.mean(x * x, axis=-1, keepdims=True) + NORM_EPS)
        h_ref[...] = ((x * r) * g_ref[...]).astype(BF16)
        r_ref[...] = r

    return pl.pallas_call(
        body, grid=(S // tm,), name="rms_fwd",
        in_specs=[pl.BlockSpec((tm, D), lambda i: (i, 0)), pl.BlockSpec((1, D), lambda i: (0, 0))],
        out_specs=[pl.BlockSpec((tm, D), lambda i: (i, 0)), pl.BlockSpec((tm, 1), lambda i: (i, 0))],
        out_shape=[jax.ShapeDtypeStruct((S, D), BF16), jax.ShapeDtypeStruct((S, 1), F32)],
        compiler_params=_cp(("parallel",)),
    )(x2, gain)


def _in_proj(h, w_all, segs, name):
    S, D = h.shape
    SEG = w_all.shape[2] // 2
    n = len(segs)
    tm = min(512, S)

    def body(h_ref, w_ref, o_ref):
        o_ref[...] = _dot(h_ref[...], w_ref[...])

    def w_map(j, i):
        seg = _seg_select(j, segs)
        return (seg // 2, 0, seg % 2)

    return pl.pallas_call(
        body, grid=(n, S // tm), name=name,
        in_specs=[pl.BlockSpec((tm, D), lambda j, i: (i, 0)), pl.BlockSpec((None, D, SEG), w_map)],
        out_specs=pl.BlockSpec((None, tm, SEG), lambda j, i: (j, i, 0)),
        out_shape=jax.ShapeDtypeStruct((n, S, SEG), F32),
        compiler_params=_cp(("parallel", "parallel")),
    )(h, w_all)


def _out_proj_loss(yh, ya, w_out, x2, tgt, fgain):
    S, D = x2.shape
    SEG = yh.shape[1]
    tm = min(256, S)

    def body(yh_ref, ya_ref, w_ref, x_ref, t_ref, fg_ref, dout_ref, doutb_ref, loss_ref, dfg_ref):
        i = pl.program_id(0)

        @pl.when(i == 0)
        def _():
            loss_ref[...] = jnp.zeros_like(loss_ref)
            dfg_ref[...] = jnp.zeros_like(dfg_ref)

        out = x_ref[...] + _dot(yh_ref[...], w_ref[pl.ds(0, SEG), :]) + _dot(ya_ref[...], w_ref[pl.ds(SEG, SEG), :])
        r = lax.rsqrt(jnp.mean(out * out, axis=-1, keepdims=True) + NORM_EPS)
        n = out * r
        fg = fg_ref[...]
        err = n * fg - t_ref[...]
        loss_ref[...] += 0.5 * jnp.sum(jnp.mean(err * err, axis=-1, keepdims=True), axis=0, keepdims=True)
        dy = err * (1.0 / D)
        dfg_ref[...] += jnp.sum(dy * n, axis=0, keepdims=True)
        dn = dy * fg
        dout = r * (dn - n * jnp.mean(dn * n, axis=-1, keepdims=True))
        dout_ref[...] = dout
        doutb_ref[...] = dout.astype(BF16)

    row = lambda i: (i, 0)
    fix = lambda i: (0, 0)
    return pl.pallas_call(
        body, grid=(S // tm,), name="out_proj_loss",
        in_specs=[pl.BlockSpec((tm, SEG), row), pl.BlockSpec((tm, SEG), row), pl.BlockSpec((2 * SEG, D), fix),
                  pl.BlockSpec((tm, D), row), pl.BlockSpec((tm, D), row), pl.BlockSpec((1, D), fix)],
        out_specs=[pl.BlockSpec((tm, D), row), pl.BlockSpec((tm, D), row), pl.BlockSpec((1, 1), fix),
                   pl.BlockSpec((1, D), fix)],
        out_shape=[jax.ShapeDtypeStruct((S, D), F32), jax.ShapeDtypeStruct((S, D), BF16),
                   jax.ShapeDtypeStruct((1, 1), F32), jax.ShapeDtypeStruct((1, D), F32)],
        compiler_params=_cp(("arbitrary",)),
    )(yh, ya, w_out, x2, tgt, fgain)


def _dy_proj(doutb, w_out):
    S, D = doutb.shape
    K = w_out.shape[0]
    tm = min(512, S)

    def body(d_ref, w_ref, o_ref):
        o_ref[...] = _dot_nt(d_ref[...], w_ref[...])

    return pl.pallas_call(
        body, grid=(S // tm,), name="dy_proj",
        in_specs=[pl.BlockSpec((tm, D), lambda i: (i, 0)), pl.BlockSpec((K, D), lambda i: (0, 0))],
        out_specs=pl.BlockSpec((tm, K), lambda i: (i, 0)),
        out_shape=jax.ShapeDtypeStruct((S, K), F32),
        compiler_params=_cp(("parallel",)),
    )(doutb, w_out)


def _grad_w_out(yh, ya, doutb):
    S, SEG = yh.shape
    D = doutb.shape[1]
    R = (2 * SEG) // 4
    nb_half = SEG // R
    tk = min(512, S)

    def body(yh_ref, ya_ref, d_ref, o_ref):
        q = pl.program_id(0)
        k = pl.program_id(1)

        @pl.when(k == 0)
        def _():
            o_ref[...] = jnp.zeros_like(o_ref)

        @pl.when(q < nb_half)
        def _():
            o_ref[...] += _dot_tn(yh_ref[...], d_ref[...])

        @pl.when(q >= nb_half)
        def _():
            o_ref[...] += _dot_tn(ya_ref[...], d_ref[...])

    return pl.pallas_call(
        body, grid=(4, S // tk), name="grad_w_out",
        in_specs=[pl.BlockSpec((tk, R), lambda q, k: (k, jnp.minimum(q, nb_half - 1))),
                  pl.BlockSpec((tk, R), lambda q, k: (k, jnp.maximum(q - nb_half, 0))),
                  pl.BlockSpec((tk, D), lambda q, k: (k, 0))],
        out_specs=pl.BlockSpec((None, R, D), lambda q, k: (q, 0, 0)),
        out_shape=jax.ShapeDtypeStruct((4, R, D), F32),
        compiler_params=_cp(("parallel", "arbitrary")),
    )(yh, ya, doutb)


def _dz_sources(sources):
    counts = [s.shape[0] for s in sources]
    starts = [sum(counts[:k]) for k in range(len(counts))]
    assert sum(counts) == 8
    return counts, starts


def _dh_proj(sources, w_all, token):
    S = sources[0].shape[1]
    D = w_all.shape[1]
    SEG = w_all.shape[2] // 2
    counts, starts = _dz_sources(sources)
    ns = len(sources)
    tm = min(512, S)

    def body(*refs):
        src = refs[:ns]
        w_ref, _, o_ref = refs[ns:]
        j = pl.program_id(1)

        @pl.when(j == 0)
        def _():
            o_ref[...] = jnp.zeros_like(o_ref)

        for k in range(ns):
            @pl.when((j >= starts[k]) & (j < starts[k] + counts[k]))
            def _(k=k):
                o_ref[...] += _dot_nt(src[k][...], w_ref[...])

    def src_spec(k):
        return pl.BlockSpec((None, tm, SEG),
                            lambda i, j: (jnp.clip(j - starts[k], 0, counts[k] - 1), i, 0))

    return pl.pallas_call(
        body, grid=(S // tm, 8), name="dh_proj",
        in_specs=[src_spec(k) for k in range(ns)] + [pl.BlockSpec((None, D, SEG), lambda i, j: (j // 2, 0, j % 2)),
                                                     pl.BlockSpec(token.shape, lambda i, j: (0, 0))],
        out_specs=pl.BlockSpec((tm, D), lambda i, j: (i, 0)),
        out_shape=jax.ShapeDtypeStruct((S, D), F32),
        compiler_params=_cp(("parallel", "arbitrary")),
    )(*sources, w_all, token)


def _rms_bwd(dh, x2, rinv, gain, dout):
    S, D = x2.shape
    tm = min(256, S)

    def body(dh_ref, x_ref, r_ref, g_ref, dout_ref, gx_ref, dg_ref):
        @pl.when(pl.program_id(0) == 0)
        def _():
            dg_ref[...] = jnp.zeros_like(dg_ref)

        dh = dh_ref[...]
        r = r_ref[...]
        xhat = x_ref[...] * r
        dg_ref[...] += jnp.sum(dh * xhat, axis=0, keepdims=True)
        dxn = dh * g_ref[...]
        gx_ref[...] = dout_ref[...] + r * (dxn - xhat * jnp.mean(dxn * xhat, axis=-1, keepdims=True))

    row = lambda i: (i, 0)
    fix = lambda i: (0, 0)
    return pl.pallas_call(
        body, grid=(S // tm,), name="rms_bwd",
        in_specs=[pl.BlockSpec((tm, D), row), pl.BlockSpec((tm, D), row), pl.BlockSpec((tm, 1), row),
                  pl.BlockSpec((1, D), fix), pl.BlockSpec((tm, D), row)],
        out_specs=[pl.BlockSpec((tm, D), row), pl.BlockSpec((1, D), fix)],
        out_shape=[jax.ShapeDtypeStruct((S, D), F32), jax.ShapeDtypeStruct((1, D), F32)],
        compiler_params=_cp(("arbitrary",)),
    )(dh, x2, rinv, gain, dout)


def _grad_w_in(h, sources):
    S, D = h.shape
    SEG = sources[0].shape[2]
    counts, starts = _dz_sources(sources)
    ns = len(sources)
    tk = min(512, S)

    def body(*refs):
        h_ref = refs[0]
        src = refs[1:1 + ns]
        o_ref = refs[1 + ns]
        j = pl.program_id(0)
        k = pl.program_id(1)

        @pl.when(k == 0)
        def _():
            o_ref[...] = jnp.zeros_like(o_ref)

        for s in range(ns):
            @pl.when((j >= starts[s]) & (j < starts[s] + counts[s]))
            def _(s=s):
                o_ref[...] += _dot_tn(h_ref[...], src[s][...])

    def src_spec(s):
        return pl.BlockSpec((None, tk, SEG),
                            lambda j, k: (jnp.clip(j - starts[s], 0, counts[s] - 1), k, 0))

    return pl.pallas_call(
        body, grid=(8, S // tk), name="grad_w_in",
        in_specs=[pl.BlockSpec((tk, D), lambda j, k: (k, 0))] + [src_spec(s) for s in range(ns)],
        out_specs=pl.BlockSpec((None, D, SEG), lambda j, k: (j // 2, 0, j % 2)),
        out_shape=jax.ShapeDtypeStruct((4, D, 2 * SEG), F32),
        compiler_params=_cp(("parallel", "arbitrary")),
    )(h, *sources)


def _lower_bound(lbl):
    l0 = lbl[0:1, :]
    l1 = lbl[1:2, :]
    m = jnp.maximum(l0, l1)
    e0 = jnp.exp(l0 - m)
    e1 = jnp.exp(l1 - m)
    return e0 / (e0 + e1)


def _tile_masks():
    row = lax.broadcasted_iota(jnp.int32, (HGRN_TILE, HGRN_TILE), 0)
    col = lax.broadcasted_iota(jnp.int32, (HGRN_TILE, HGRN_TILE), 1)
    same = (row // HGRN_CHUNK) == (col // HGRN_CHUNK)
    return same & (row >= col), same & (row <= col)


def _chunk_last(b):
    T = b.shape[0]
    b3 = b.reshape(T // HGRN_CHUNK, HGRN_CHUNK, HGRN_HEAD)
    return jnp.broadcast_to(b3[:, HGRN_CHUNK - 1:HGRN_CHUNK, :], b3.shape).reshape(T, HGRN_HEAD)


def _chunk_sum(x):
    T = x.shape[0]
    x3 = x.reshape(T // HGRN_CHUNK, HGRN_CHUNK, HGRN_HEAD)
    return jnp.broadcast_to(jnp.sum(x3, axis=1, keepdims=True), x3.shape).reshape(T, HGRN_HEAD)


def _hgrn_dims(S, SEG):
    T = min(HGRN_BLOCK, S)
    assert S % T == 0 and T % HGRN_TILE == 0
    tiles = [slice(t * HGRN_TILE, (t + 1) * HGRN_TILE) for t in range(T // HGRN_TILE)]
    chunks = [slice(c * HGRN_CHUNK, (c + 1) * HGRN_CHUNK) for c in range(T // HGRN_CHUNK)]
    return SEG // HGRN_HEAD, T, T // HGRN_CHUNK, S // T, tiles, chunks


def _hgrn_fwd(zf32, lb_logits, gnorm):
    _, S, SEG = zf32.shape
    H, T, NC, NJ, tiles, chunks = _hgrn_dims(S, SEG)

    def body(zq_ref, zf_ref, zi_ref, zg_ref, lbl_ref, gn_ref, y_ref, st_ref, state):
        @pl.when(pl.program_id(1) == 0)
        def _():
            state[...] = jnp.zeros_like(state)

        lb = _lower_bound(lbl_ref[...])
        tril, _ = _tile_masks()
        tril_bf = tril.astype(BF16)
        zq = zq_ref[...]
        q = zq * _sigmoid(zq)
        f = lb + (1.0 - lb) * _sigmoid(zf_ref[...])
        k = 1.0 - f
        logf = jnp.log(f)
        b = jnp.concatenate([_exact_dot(tril_bf, logf[t]) for t in tiles], axis=0)
        bl = _chunk_last(b)
        qd_b = (q * jnp.exp(b)).astype(BF16)
        kd_b = (k * jnp.exp(-b)).astype(BF16)
        ke_b = (k * jnp.exp(bl - b)).astype(BF16)
        v_b = zi_ref[...].astype(BF16)
        o_intra = jnp.concatenate(
            [_dot(jnp.where(tril, _dot_nt(qd_b[t], kd_b[t]), 0.0).astype(BF16), v_b[t]) for t in tiles], axis=0)
        kvs = [_dot_tn(v_b[r], ke_b[r]) for r in chunks]
        ebl = jnp.exp(bl)
        st = state[...]
        sts = []
        for c in range(NC):
            st_ref[c] = st
            sts.append(st.astype(BF16))
            st = st * ebl[c * HGRN_CHUNK:c * HGRN_CHUNK + 1, :] + kvs[c]
        state[...] = st
        o = o_intra + jnp.concatenate([_dot_nt(qd_b[r], sb) for r, sb in zip(chunks, sts)], axis=0)
        on = o * lax.rsqrt(jnp.mean(o * o, axis=-1, keepdims=True) + NORM_EPS) * gn_ref[...]
        zg = zg_ref[...]
        y_ref[...] = (on * (zg * _sigmoid(zg))).astype(BF16)

    def zspec(seg):
        return pl.BlockSpec((None, T, HGRN_HEAD), lambda h, j: (seg, j, h))

    return pl.pallas_call(
        body, grid=(H, NJ), name="hgrn_fwd",
        in_specs=[zspec(0), zspec(1), zspec(2), zspec(3),
                  pl.BlockSpec((2, HGRN_HEAD), lambda h, j: (0, h)),
                  pl.BlockSpec((1, HGRN_HEAD), lambda h, j: (0, 0))],
        out_specs=[pl.BlockSpec((T, HGRN_HEAD), lambda h, j: (j, h)),
                   pl.BlockSpec((NC, None, HGRN_HEAD, HGRN_HEAD), lambda h, j: (j, h, 0, 0))],
        out_shape=[jax.ShapeDtypeStruct((S, SEG), BF16),
                   jax.ShapeDtypeStruct((S // HGRN_CHUNK, H, HGRN_HEAD, HGRN_HEAD), F32)],
        scratch_shapes=[pltpu.VMEM((HGRN_HEAD, HGRN_HEAD), F32)],
        compiler_params=_cp(("parallel", "arbitrary")),
    )(zf32, zf32, zf32, zf32, lb_logits, gnorm)


def _hgrn_bwd(zf32, lb_logits, gnorm, states, dy):
    _, S, SEG = zf32.shape
    H, T, NC, NJ, tiles, chunks = _hgrn_dims(S, SEG)
    C = HGRN_CHUNK

    def body(zq_ref, zf_ref, zi_ref, zg_ref, lbl_ref, gn_ref, st_ref, dy_ref, dz_ref, dl_ref, dgn_ref, gstate):
        @pl.when(pl.program_id(1) == 0)
        def _():
            gstate[...] = jnp.zeros_like(gstate)
            dl_ref[...] = jnp.zeros_like(dl_ref)
            dgn_ref[...] = jnp.zeros_like(dgn_ref)

        lb = _lower_bound(lbl_ref[...])
        gn = gn_ref[...]
        tril, triu = _tile_masks()
        tril_bf = tril.astype(BF16)
        triu_bf = triu.astype(BF16)
        q, dq_dz = _silu_and_grad(zq_ref[...])
        sf = _sigmoid(zf_ref[...])
        f = lb + (1.0 - lb) * sf
        k = 1.0 - f
        logf = jnp.log(f)
        b = jnp.concatenate([_exact_dot(tril_bf, logf[t]) for t in tiles], axis=0)
        bl = _chunk_last(b)
        eb = jnp.exp(b)
        enb = jnp.exp(-b)
        ekl = jnp.exp(bl - b)
        ebl = jnp.exp(bl)
        qd = q * eb
        kd = k * enb
        ke = k * ekl
        qd_b = qd.astype(BF16)
        kd_b = kd.astype(BF16)
        ke_b = ke.astype(BF16)
        v_b = zi_ref[...].astype(BF16)
        sts = [st_ref[c] for c in range(NC)]
        sts_b = [s.astype(BF16) for s in sts]
        a_b = [jnp.where(tril, _dot_nt(qd_b[t], kd_b[t]), 0.0).astype(BF16) for t in tiles]
        o = (jnp.concatenate([_dot(a, v_b[t]) for a, t in zip(a_b, tiles)], axis=0)
             + jnp.concatenate([_dot_nt(qd_b[r], sb) for r, sb in zip(chunks, sts_b)], axis=0))
        rinv = lax.rsqrt(jnp.mean(o * o, axis=-1, keepdims=True) + NORM_EPS)
        ohat = o * rinv
        sg, dsg = _silu_and_grad(zg_ref[...])
        dyv = dy_ref[...]
        don = dyv * sg
        dz_ref[3] = (dyv * (ohat * gn) * dsg).astype(BF16)
        dgn_ref[...] += jnp.sum(don * ohat, axis=0, keepdims=True)
        dohat = don * gn
        do = rinv * (dohat - ohat * jnp.mean(dohat * ohat, axis=-1, keepdims=True))
        do_b = do.astype(BF16)
        da_b = [jnp.where(tril, _dot_nt(do_b[t], v_b[t]), 0.0).astype(BF16) for t in tiles]
        dv_intra = jnp.concatenate([_dot_tn(a, do_b[t]) for a, t in zip(a_b, tiles)], axis=0)
        dqd_intra = jnp.concatenate([_dot(da, kd_b[t]) for da, t in zip(da_b, tiles)], axis=0)
        dkd = jnp.concatenate([_dot_tn(da, qd_b[t]) for da, t in zip(da_b, tiles)], axis=0)
        dqd_inter = jnp.concatenate([_dot(do_b[r], sb) for r, sb in zip(chunks, sts_b)], axis=0)
        gks = [_dot_tn(do_b[r], qd_b[r]) for r in chunks]
        g = gstate[...]
        gs = [None] * NC
        for c in reversed(range(NC)):
            gs[c] = g
            g = g * ebl[c * C:c * C + 1, :] + gks[c]
        gstate[...] = g
        gs_b = [x.astype(BF16) for x in gs]
        dv = dv_intra + jnp.concatenate([_dot_nt(ke_b[r], gb) for r, gb in zip(chunks, gs_b)], axis=0)
        dz_ref[2] = dv.astype(BF16)
        dke = jnp.concatenate([_dot(v_b[r], gb) for r, gb in zip(chunks, gs_b)], axis=0)
        debl = jnp.concatenate(
            [jnp.broadcast_to(jnp.sum(x * s, axis=0, keepdims=True), (C, HGRN_HEAD)) for x, s in zip(gs, sts)], axis=0)
        dqd = dqd_intra + dqd_inter
        dz_ref[0] = ((dqd * eb) * dq_dz).astype(BF16)
        t_ke = dke * ke
        db = dqd * qd - dkd * kd - t_ke
        db_last = _chunk_sum(t_ke) + debl * ebl
        dk = dkd * enb + dke * ekl
        dlogf = jnp.concatenate([_exact_dot(triu_bf, db[t]) for t in tiles], axis=0) + db_last
        df = dlogf / f - dk
        dz_ref[1] = (df * (1.0 - lb) * (sf * (1.0 - sf))).astype(BF16)
        dlb = jnp.sum(df * (1.0 - sf), axis=0, keepdims=True)
        dl0 = dlb * lb * (1.0 - lb)
        dl_ref[0:1, :] += dl0
        dl_ref[1:2, :] -= dl0

    def zspec(seg):
        return pl.BlockSpec((None, T, HGRN_HEAD), lambda h, j: (seg, NJ - 1 - j, h))

    return pl.pallas_call(
        body, grid=(H, NJ), name="hgrn_bwd",
        in_specs=[zspec(0), zspec(1), zspec(2), zspec(3),
                  pl.BlockSpec((2, HGRN_HEAD), lambda h, j: (0, h)),
                  pl.BlockSpec((1, HGRN_HEAD), lambda h, j: (0, 0)),
                  pl.BlockSpec((NC, None, HGRN_HEAD, HGRN_HEAD), lambda h, j: (NJ - 1 - j, h, 0, 0)),
                  pl.BlockSpec((T, HGRN_HEAD), lambda h, j: (NJ - 1 - j, h))],
        out_specs=[pl.BlockSpec((4, T, HGRN_HEAD), lambda h, j: (0, NJ - 1 - j, h)),
                   pl.BlockSpec((2, HGRN_HEAD), lambda h, j: (0, h)),
                   pl.BlockSpec((None, 1, HGRN_HEAD), lambda h, j: (h, 0, 0))],
        out_shape=[jax.ShapeDtypeStruct((4, S, SEG), BF16), jax.ShapeDtypeStruct((2, SEG), F32),
                   jax.ShapeDtypeStruct((H, 1, HGRN_HEAD), F32)],
        scratch_shapes=[pltpu.VMEM((HGRN_HEAD, HGRN_HEAD), F32)],
        compiler_params=_cp(("parallel", "arbitrary")),
    )(zf32, zf32, zf32, zf32, lb_logits, gnorm, states, dy)


def _alibi_slopes(seg):
    n_heads = seg // ATTN_HEAD
    s = 2.0 ** (-8.0 * np.arange(1, n_heads + 1, dtype=np.float64) / n_heads)
    return jnp.asarray(np.repeat(s, ATTN_HEAD)[None, :], F32)


def _attn_dims(S, SEG, d):
    rb = BAND * d
    assert S % rb == 0 and SEG % LANES == 0
    cb = min(SEG, ATTN_BLOCK_ELEMS // rb) if d == 1 else LANES
    assert SEG % cb == 0
    return rb, cb, S // rb, SEG // cb


def _res_rows(r, d):
    return pl.ds(0, BAND) if d == 1 else pl.ds(r, BAND, stride=d)


def _for_residues(d, fn):
    if d == 1:
        fn(0)
    else:
        def step(r, carry):
            fn(r)
            return carry
        lax.fori_loop(0, d, step, 0, unroll=ATTN_UNROLL)


def _band_terms(n, d):
    i = lax.broadcasted_iota(jnp.int32, (BAND, 2 * BAND), 0)
    jj = lax.broadcasted_iota(jnp.int32, (BAND, 2 * BAND), 1)
    delta = BAND + i - jj
    valid = (delta >= 0) & (delta <= BAND) & ((n > 0) | (jj >= BAND))
    return (-d * delta).astype(F32), valid


def _head_biases(slopes, nd, valid):
    out = []
    for s in _per_head(slopes):
        s2 = jnp.concatenate([s, s], axis=1)
        out.append(jnp.where(valid, s2 * nd, NEG))
    return jnp.concatenate(out, axis=0)


def _stack_heads(x):
    lane = lax.broadcasted_iota(jnp.int32, x.shape, 1)
    zero = jnp.zeros_like(x)
    return jnp.concatenate([jnp.where(lane < ATTN_HEAD, x, zero), jnp.where(lane < ATTN_HEAD, zero, x)], axis=0)


def _unstack_heads(x2):
    first = lax.broadcasted_iota(jnp.int32, (BAND, LANES), 1) < ATTN_HEAD
    return jnp.where(first, x2[:BAND], x2[BAND:])


def _stack_per_head(x):
    a, b = _per_head(x)
    col = jnp.concatenate([a, b], axis=0)
    return jnp.concatenate([col, col], axis=1)


def _per_head(x):
    lane = lax.broadcasted_iota(jnp.int32, x.shape, 1)
    sw = pltpu.roll(x, ATTN_HEAD, 1)
    first = lane < ATTN_HEAD
    return jnp.where(first, x, sw), jnp.where(first, sw, x)


def _attn_fwd(qkv, slopes, d):
    _, S, SEG = qkv.shape
    rb, cb, nb, ncb = _attn_dims(S, SEG, d)
    NP = cb // LANES

    def body(q_ref, kp_ref, kc_ref, vp_ref, vc_ref, sl_ref, o_ref, l_ref):
        n = pl.program_id(1)
        nd, valid = _band_terms(n, d)
        biases = [_head_biases(sl_ref[:, p * LANES:(p + 1) * LANES], nd, valid) for p in range(NP)]

        def residue(r):
            rows = _res_rows(r, d)
            for p in range(NP):
                cols = slice(p * LANES, (p + 1) * LANES)
                kc = jnp.concatenate([kp_ref[rows, cols], kc_ref[rows, cols]], axis=0).astype(BF16)
                vc = jnp.concatenate([vp_ref[rows, cols], vc_ref[rows, cols]], axis=0).astype(BF16)
                s = _dot_nt(_stack_heads((q_ref[rows, cols] * ATTN_SCALE).astype(BF16)), kc) + biases[p]
                m = jnp.max(s, axis=-1, keepdims=True)
                e = jnp.exp(s - m)
                den = jnp.sum(e, axis=-1, keepdims=True)
                o_ref[rows, cols] = _unstack_heads(_dot(e.astype(BF16), vc) / den)
                l_ref[rows, cols] = _unstack_heads(jnp.broadcast_to(m + jnp.log(den), (2 * BAND, LANES)))

        _for_residues(d, residue)

    def spec(seg, prev):
        if prev:
            return pl.BlockSpec((None, rb, cb), lambda c, n: (seg, jnp.maximum(n - 1, 0), c))
        return pl.BlockSpec((None, rb, cb), lambda c, n: (seg, n, c))

    out = pl.BlockSpec((rb, cb), lambda c, n: (n, c))
    return pl.pallas_call(
        body, grid=(ncb, nb), name=f"attn_fwd_d{d}",
        in_specs=[spec(0, False), spec(1, True), spec(1, False), spec(2, True), spec(2, False),
                  pl.BlockSpec((1, cb), lambda c, n: (0, c))],
        out_specs=[out, out],
        out_shape=[jax.ShapeDtypeStruct((S, SEG), F32), jax.ShapeDtypeStruct((S, SEG), F32)],
        compiler_params=_cp(("parallel", "parallel")),
    )(qkv, qkv, qkv, qkv, qkv, slopes)


def _attn_merge(outs, lses, zf32):
    S, SEG = outs[0].shape
    tm = min(256, S)

    def body(o1, o2, o3, l1, l2, l3, zg_ref, o_ref, lse_ref, y_ref):
        a, b, c = l1[...], l2[...], l3[...]
        m = jnp.maximum(jnp.maximum(a, b), c)
        ea, eb, ec = jnp.exp(a - m), jnp.exp(b - m), jnp.exp(c - m)
        tot = ea + eb + ec
        o = (ea / tot) * o1[...] + (eb / tot) * o2[...] + (ec / tot) * o3[...]
        o_ref[...] = o
        lse_ref[...] = m + jnp.log(tot)
        zg = zg_ref[...]
        y_ref[...] = (o * (zg * _sigmoid(zg))).astype(BF16)

    row = pl.BlockSpec((tm, SEG), lambda i: (i, 0))
    return pl.pallas_call(
        body, grid=(S // tm,), name="attn_merge",
        in_specs=[row] * 6 + [pl.BlockSpec((None, tm, SEG), lambda i: (4, i, 0))],
        out_specs=[row, row, row],
        out_shape=[jax.ShapeDtypeStruct((S, SEG), F32), jax.ShapeDtypeStruct((S, SEG), F32),
                   jax.ShapeDtypeStruct((S, SEG), BF16)],
        compiler_params=_cp(("parallel",)),
    )(*outs, *lses, zf32)


def _attn_gate_bwd(dy, o, zf32):
    S, SEG = o.shape
    tm = min(256, S)
    NP = SEG // LANES

    def body(dy_ref, o_ref, zg_ref, do_ref, dl_ref, dzg_ref):
        r = lax.broadcasted_iota(jnp.int32, (LANES, LANES), 0) // ATTN_HEAD
        c = lax.broadcasted_iota(jnp.int32, (LANES, LANES), 1) // ATTN_HEAD
        same_head = (r == c).astype(BF16)
        for p in range(NP):
            cols = slice(p * LANES, (p + 1) * LANES)
            sg, dsg = _silu_and_grad(zg_ref[:, cols])
            dyv = dy_ref[:, cols]
            ov = o_ref[:, cols]
            do = dyv * sg
            do_ref[:, cols] = do
            dzg_ref[:, cols] = (dyv * ov * dsg).astype(BF16)
            dl_ref[:, cols] = _exact_dot_right(do * ov, same_head)

    return pl.pallas_call(
        body, grid=(S // tm,), name="attn_gate_bwd",
        in_specs=[pl.BlockSpec((tm, SEG), lambda i: (i, 1)), pl.BlockSpec((tm, SEG), lambda i: (i, 0)),
                  pl.BlockSpec((None, tm, SEG), lambda i: (4, i, 0))],
        out_specs=[pl.BlockSpec((tm, SEG), lambda i: (i, 0)), pl.BlockSpec((tm, SEG), lambda i: (i, 0)),
                   pl.BlockSpec((None, tm, SEG), lambda i: (0, i, 0))],
        out_shape=[jax.ShapeDtypeStruct((S, SEG), F32), jax.ShapeDtypeStruct((S, SEG), F32),
                   jax.ShapeDtypeStruct((1, S, SEG), BF16)],
        compiler_params=_cp(("parallel",)),
    )(dy, o, zf32)


def _attn_bwd(qkv, slopes, do, lse, dl, d, acc, out_dtype):
    _, S, SEG = qkv.shape
    rb, cb, nb, ncb = _attn_dims(S, SEG, d)
    NP = cb // LANES
    has_acc = acc is not None

    def body(*refs):
        q_ref, kp_ref, kc_ref, vp_ref, vc_ref, sl_ref, do_ref, lse_ref, dl_ref = refs[:9]
        refs = refs[9:]
        if has_acc:
            aq_ref, ak_ref, av_ref = refs[:3]
            refs = refs[3:]
        dq_ref, dk_ref, dv_ref, ck, cv = refs
        n = pl.program_id(1)

        @pl.when(n == 0)
        def _():
            ck[...] = jnp.zeros_like(ck)
            cv[...] = jnp.zeros_like(cv)

        @pl.when(n < nb)
        def _():
            nd, valid = _band_terms(n, d)
            biases = [_head_biases(sl_ref[:, p * LANES:(p + 1) * LANES], nd, valid) for p in range(NP)]

            def residue(r):
                rows = _res_rows(r, d)
                for p in range(NP):
                    cols = slice(p * LANES, (p + 1) * LANES)
                    kc = jnp.concatenate([kp_ref[rows, cols], kc_ref[rows, cols]], axis=0).astype(BF16)
                    vc = jnp.concatenate([vp_ref[rows, cols], vc_ref[rows, cols]], axis=0).astype(BF16)
                    qs = _stack_heads((q_ref[rows, cols] * ATTN_SCALE).astype(BF16))
                    dos = _stack_heads(do_ref[rows, cols].astype(BF16))
                    pr = jnp.exp(_dot_nt(qs, kc) + biases[p] - _stack_per_head(lse_ref[rows, cols]))
                    ds = (pr * (_dot_nt(dos, vc) - _stack_per_head(dl_ref[rows, cols]))).astype(BF16)
                    dq = _unstack_heads(_dot(ds, kc)) * ATTN_SCALE
                    dk = _dot_tn(ds, qs)
                    dv = _dot_tn(pr.astype(BF16), dos)
                    dk_prev = ck[r, :, cols] + dk[:BAND, :]
                    dv_prev = cv[r, :, cols] + dv[:BAND, :]
                    if has_acc:
                        dq = dq + aq_ref[rows, cols]
                        dk_prev = dk_prev + ak_ref[rows, cols]
                        dv_prev = dv_prev + av_ref[rows, cols]
                    dq_ref[rows, cols] = dq.astype(out_dtype)
                    dk_ref[rows, cols] = dk_prev.astype(out_dtype)
                    dv_ref[rows, cols] = dv_prev.astype(out_dtype)
                    ck[r, :, cols] = dk[BAND:, :]
                    cv[r, :, cols] = dv[BAND:, :]

            _for_residues(d, residue)

        @pl.when(n == nb)
        def _():
            def residue(r):
                rows = _res_rows(r, d)
                dk_last = ck[r]
                dv_last = cv[r]
                if has_acc:
                    dk_last = dk_last + ak_ref[rows, :]
                    dv_last = dv_last + av_ref[rows, :]
                dk_ref[rows, :] = dk_last.astype(out_dtype)
                dv_ref[rows, :] = dv_last.astype(out_dtype)

            _for_residues(d, residue)

    cur2 = lambda c, n: (jnp.minimum(n, nb - 1), c)
    cur3 = lambda c, n: (0, jnp.minimum(n, nb - 1), c)
    lag3 = lambda c, n: (0, jnp.clip(n - 1, 0, nb - 1), c)

    def spec(seg, prev):
        if prev:
            return pl.BlockSpec((None, rb, cb), lambda c, n: (seg, jnp.clip(n - 1, 0, nb - 1), c))
        return pl.BlockSpec((None, rb, cb), lambda c, n: (seg, jnp.minimum(n, nb - 1), c))

    in_specs = [spec(0, False), spec(1, True), spec(1, False), spec(2, True), spec(2, False),
                pl.BlockSpec((1, cb), lambda c, n: (0, c)),
                pl.BlockSpec((rb, cb), cur2), pl.BlockSpec((rb, cb), cur2), pl.BlockSpec((rb, cb), cur2)]
    args = [qkv, qkv, qkv, qkv, qkv, slopes, do, lse, dl]
    aliases = {}
    if has_acc:
        in_specs += [pl.BlockSpec((None, rb, cb), cur3), pl.BlockSpec((None, rb, cb), lag3),
                     pl.BlockSpec((None, rb, cb), lag3)]
        args += list(acc)
        if out_dtype == F32:
            aliases = {9: 0, 10: 1, 11: 2}
    return pl.pallas_call(
        body, grid=(ncb, nb + 1), name=f"attn_bwd_d{d}",
        in_specs=in_specs,
        out_specs=[pl.BlockSpec((None, rb, cb), cur3), pl.BlockSpec((None, rb, cb), lag3),
                   pl.BlockSpec((None, rb, cb), lag3)],
        out_shape=[jax.ShapeDtypeStruct((1, S, SEG), out_dtype)] * 3,
        scratch_shapes=[pltpu.VMEM((d, BAND, cb), F32), pltpu.VMEM((d, BAND, cb), F32)],
        input_output_aliases=aliases,
        compiler_params=_cp(("parallel", "arbitrary")),
    )(*args)


def _adamw(w, g, m, v, name):
    R, C = w.shape
    tr = R if R <= 256 else 256
    assert R % tr == 0

    def body(w_ref, g_ref, m_ref, v_ref, d_ref, nm_ref, nv_ref):
        g = g_ref[...]
        nm = ADAM_B1 * m_ref[...] + (1.0 - ADAM_B1) * g
        nv = ADAM_B2 * v_ref[...] + (1.0 - ADAM_B2) * (g * g)
        m_hat = nm / (1.0 - ADAM_B1 ** ADAM_STEP)
        v_hat = nv / (1.0 - ADAM_B2 ** ADAM_STEP)
        d_ref[...] = -ADAM_LR * (m_hat / (jnp.sqrt(v_hat) + ADAM_EPS) + ADAM_WD * w_ref[...])
        nm_ref[...] = nm
        nv_ref[...] = nv

    blk = pl.BlockSpec((tr, C), lambda i: (i, 0))
    sds = jax.ShapeDtypeStruct((R, C), F32)
    return pl.pallas_call(
        body, grid=(R // tr,), name=name, in_specs=[blk] * 4, out_specs=[blk] * 3, out_shape=[sds] * 3,
        compiler_params=_cp(("parallel",)),
    )(w, g, m, v)


def _coords():
    return lax.axis_index("x"), lax.axis_index("y"), lax.axis_index("c")


def _other_chips(x, y):
    return [(1 - x, y), (x, 1 - y), (1 - x, 1 - y)]


ANY = pl.BlockSpec(memory_space=pl.ANY)


def _cast_into_slot(w, where, name):
    R, C = w.shape
    tr = min(256, R)

    def body(where_ref, w_ref, o_ref):
        o_ref[...] = w_ref[...].astype(BF16)

    grid_spec = pltpu.PrefetchScalarGridSpec(
        num_scalar_prefetch=1, grid=(R // tr,),
        in_specs=[pl.BlockSpec((tr, C), lambda i, w: (i, 0))],
        out_specs=pl.BlockSpec((None, tr, C), lambda i, w: (w[1], i, 0)))
    return pl.pallas_call(
        body, grid_spec=grid_spec, name=name, out_shape=jax.ShapeDtypeStruct((4, R, C), BF16),
        compiler_params=_cp(("parallel",)),
    )(where, w)


def _gather_weights(wia, woa):
    nrows = (wia.shape[1] // 2, woa.shape[1] // 2)

    def body(wi_in, wo_in, wia_ref, woa_ref, send_sems, recv_sems):
        x, y, c = _coords()
        me = 2 * x + y
        alls = (wia_ref, woa_ref)

        def half(a, chip, hc):
            return alls[a].at[chip, pl.ds(hc * nrows[a], nrows[a]), :]

        def remote(a, k, part, to):
            return pltpu.make_async_remote_copy(src_ref=part, dst_ref=part, send_sem=send_sems.at[2 * k + a],
                                                recv_sem=recv_sems.at[2 * k + a], device_id=to, device_id_type=MESH)

        started = []
        for k, (px, py) in enumerate(_other_chips(x, y)):
            for a in range(2):
                cp = remote(a, k, half(a, me, c), (px, py, c))
                cp.start()
                started.append(cp)
        for k, (px, py) in enumerate(_other_chips(x, y)):
            for a in range(2):
                got = half(a, 2 * px + py, c)
                remote(a, k, got, (px, py, c)).wait_recv()
                cp = remote(a, 3 + k, got, (x, y, 1 - c))
                cp.start()
                started.append(cp)
        for k, (px, py) in enumerate(_other_chips(x, y)):
            for a in range(2):
                remote(a, 3 + k, half(a, 2 * px + py, 1 - c), (x, y, 1 - c)).wait_recv()
        for cp in started:
            cp.wait_send()

    return pl.pallas_call(
        body, name="gather_weights", in_specs=[ANY, ANY], out_specs=[ANY, ANY],
        out_shape=[jax.ShapeDtypeStruct(wia.shape, wia.dtype), jax.ShapeDtypeStruct(woa.shape, woa.dtype)],
        scratch_shapes=[pltpu.SemaphoreType.DMA((12,)), pltpu.SemaphoreType.DMA((12,))],
        input_output_aliases={0: 0, 1: 1},
    )(wia, woa)


def _swap_halves(gi, go):
    n_i, n_o = gi.shape[1] // 2, go.shape[1] // 2

    def body(gi_ref, go_ref, si_ref, so_ref, send_sems, recv_sems):
        x, y, c = _coords()
        cps = []
        for a, (src, dst, nr) in enumerate(((gi_ref, si_ref, n_i), (go_ref, so_ref, n_o))):
            cp = pltpu.make_async_remote_copy(
                src_ref=src.at[:, pl.ds((1 - c) * nr, nr), :], dst_ref=dst, send_sem=send_sems.at[a],
                recv_sem=recv_sems.at[a], device_id=(x, y, 1 - c), device_id_type=MESH)
            cp.start()
            cps.append(cp)
        for cp in cps:
            cp.wait()

    return pl.pallas_call(
        body, name="swap_halves", in_specs=[ANY, ANY], out_specs=[ANY, ANY],
        out_shape=[jax.ShapeDtypeStruct((4, n_i, gi.shape[2]), F32), jax.ShapeDtypeStruct((4, n_o, go.shape[2]), F32)],
        scratch_shapes=[pltpu.SemaphoreType.DMA((2,)), pltpu.SemaphoreType.DMA((2,))],
    )(gi, go)


def _pair_sum(g, sib, where, name):
    _, n2, C = g.shape
    N = n2 // 2
    tr = min(256, N)
    nt = N // tr

    def body(where_ref, g_ref, s_ref, qb_ref, own_ref):
        q = pl.program_id(1)
        tot = g_ref[...] + s_ref[...]
        qb_ref[...] = tot.astype(BF16)

        @pl.when(q == where_ref[1])
        def _():
            own_ref[...] = tot

    grid_spec = pltpu.PrefetchScalarGridSpec(
        num_scalar_prefetch=1, grid=(nt, 4),
        in_specs=[pl.BlockSpec((None, tr, C), lambda i, q, w: (q, w[0] * nt + i, 0)),
                  pl.BlockSpec((None, tr, C), lambda i, q, w: (q, i, 0))],
        out_specs=[pl.BlockSpec((None, tr, C), lambda i, q, w: (q, i, 0)),
                   pl.BlockSpec((tr, C), lambda i, q, w: (i, 0))])
    return pl.pallas_call(
        body, grid_spec=grid_spec, name=name,
        out_shape=[jax.ShapeDtypeStruct((4, N, C), BF16), jax.ShapeDtypeStruct((N, C), F32)],
        compiler_params=_cp(("parallel", "arbitrary")),
    )(where, g, sib)


HBM = pl.BlockSpec(memory_space=pltpu.HBM)
SEM = pl.BlockSpec(memory_space=pltpu.SEMAPHORE)


def _in_hbm(a):
    return pltpu.with_memory_space_constraint(a, pltpu.HBM)


def _scatter_copies(qi_ref, qo_ref, ri_ref, ro_ref, send_sems, recv_sems):
    x, y, c = _coords()
    cps = []
    for k, (px, py) in enumerate(_other_chips(x, y)):
        for a, (src, dst) in enumerate(((qi_ref, ri_ref), (qo_ref, ro_ref))):
            cps.append(pltpu.make_async_remote_copy(
                src_ref=src.at[2 * px + py], dst_ref=dst.at[k], send_sem=send_sems.at[2 * k + a],
                recv_sem=recv_sems.at[2 * k + a], device_id=(px, py, c), device_id_type=MESH))
    return cps


def _scatter_start(qi, qo):
    def body(qi_ref, qo_ref, ri_ref, ro_ref, send_sems, recv_sems, qi_thru, qo_thru, ri_thru, ro_thru, token):
        for cp in _scatter_copies(qi_ref, qo_ref, ri_ref, ro_ref, send_sems, recv_sems):
            cp.start()
        token[...] = jnp.zeros_like(token)

    ri = lax.empty((3,) + qi.shape[1:], BF16)
    ro = lax.empty((3,) + qo.shape[1:], BF16)
    return pl.pallas_call(
        body, name="scatter_start",
        out_shape=(pltpu.SemaphoreType.DMA((6,)), pltpu.SemaphoreType.DMA((6,)),
                   pltpu.HBM(qi.shape, BF16), pltpu.HBM(qo.shape, BF16), pltpu.HBM(ri.shape, BF16),
                   pltpu.HBM(ro.shape, BF16), jax.ShapeDtypeStruct((8, LANES), F32)),
        in_specs=(HBM, HBM, HBM, HBM),
        out_specs=(SEM, SEM, HBM, HBM, HBM, HBM, pl.BlockSpec(memory_space=pltpu.VMEM)),
        input_output_aliases={0: 2, 1: 3, 2: 4, 3: 5},
        compiler_params=pltpu.CompilerParams(has_side_effects=pltpu.SideEffectType.DATAFLOW_SIDE_EFFECTING),
    )(_in_hbm(qi), _in_hbm(qo), _in_hbm(ri), _in_hbm(ro))


def _scatter_wait(send_sems, recv_sems, qi, qo, ri, ro, after):
    def body(qi_ref, qo_ref, ri_ref, ro_ref, send_sems, recv_sems, after_ref, qi_dead, qo_dead, ri_got, ro_got):
        for cp in _scatter_copies(qi_ref, qo_ref, ri_ref, ro_ref, send_sems, recv_sems):
            cp.wait_send()
            cp.wait_recv()

    outs = pl.pallas_call(
        body, name="scatter_wait",
        out_shape=(pltpu.HBM(qi.shape, BF16), pltpu.HBM(qo.shape, BF16), pltpu.HBM(ri.shape, BF16),
                   pltpu.HBM(ro.shape, BF16)),
        in_specs=(HBM, HBM, HBM, HBM, SEM, SEM, ANY), out_specs=(HBM, HBM, HBM, HBM),
        input_output_aliases={0: 0, 1: 1, 2: 2, 3: 3},
        compiler_params=pltpu.CompilerParams(has_side_effects=pltpu.SideEffectType.DATAFLOW_SIDE_EFFECTING),
    )(qi, qo, ri, ro, send_sems, recv_sems, after)
    return outs[2], outs[3]


def _chip_sum(own, got, where, name):
    N, C = own.shape
    tr = min(256, N)
    nt = N // tr

    def body(where_ref, own_ref, got_ref, o_ref):
        t = own_ref[...]
        for k in range(3):
            t = t + got_ref[k].astype(F32)
        o_ref[...] = t

    grid_spec = pltpu.PrefetchScalarGridSpec(
        num_scalar_prefetch=1, grid=(nt,),
        in_specs=[pl.BlockSpec((tr, C), lambda i, w: (i, 0)), pl.BlockSpec((3, tr, C), lambda i, w: (0, i, 0))],
        out_specs=pl.BlockSpec((tr, C), lambda i, w: (w[0] * nt + i, 0)))
    return pl.pallas_call(
        body, grid_spec=grid_spec, name=name, out_shape=jax.ShapeDtypeStruct((2 * N, C), F32),
        compiler_params=_cp(("parallel",)),
    )(where, own, got)


def _join_halves(gi, go):
    def body(gi_in, go_in, gi_ref, go_ref, send_sems, recv_sems):
        x, y, c = _coords()
        cps = []
        for a, ref in enumerate((gi_ref, go_ref)):
            nr = ref.shape[0] // 2
            mine = ref.at[pl.ds(c * nr, nr), :]
            cp = pltpu.make_async_remote_copy(src_ref=mine, dst_ref=mine, send_sem=send_sems.at[a],
                                              recv_sem=recv_sems.at[a], device_id=(x, y, 1 - c), device_id_type=MESH)
            cp.start()
            cps.append(cp)
        for a, ref in enumerate((gi_ref, go_ref)):
            nr = ref.shape[0] // 2
            theirs = ref.at[pl.ds((1 - c) * nr, nr), :]
            pltpu.make_async_remote_copy(src_ref=theirs, dst_ref=theirs, send_sem=send_sems.at[a],
                                         recv_sem=recv_sems.at[a], device_id=(x, y, 1 - c),
                                         device_id_type=MESH).wait_recv()
        for cp in cps:
            cp.wait_send()

    return pl.pallas_call(
        body, name="join_halves", in_specs=[ANY, ANY], out_specs=[ANY, ANY],
        out_shape=[jax.ShapeDtypeStruct(gi.shape, F32), jax.ShapeDtypeStruct(go.shape, F32)],
        scratch_shapes=[pltpu.SemaphoreType.DMA((2,)), pltpu.SemaphoreType.DMA((2,))],
        input_output_aliases={0: 0, 1: 1},
    )(gi, go)


def _all_reduce_small(part):
    R, C = part.shape

    def body(p_ref, o_ref, slots, send_sems, recv_sems):
        x, y, c = _coords()
        me = 4 * x + 2 * y + c
        slots[me] = p_ref[...]
        cps = []
        for k in range(1, 8):
            fx, fy, fc = (k >> 2) & 1, (k >> 1) & 1, k & 1
            peer = (1 - x if fx else x, 1 - y if fy else y, 1 - c if fc else c)
            cp = pltpu.make_async_remote_copy(src_ref=p_ref, dst_ref=slots.at[me], send_sem=send_sems.at[k - 1],
                                              recv_sem=recv_sems.at[k - 1], device_id=peer, device_id_type=MESH)
            cp.start()
            cps.append(cp)
        for cp in cps:
            cp.wait()
        t = slots[0]
        for k in range(1, 8):
            t = t + slots[k]
        o_ref[...] = t

    vm = pl.BlockSpec(memory_space=pltpu.VMEM)
    return pl.pallas_call(
        body, name="all_reduce_small", in_specs=[vm], out_specs=vm,
        out_shape=jax.ShapeDtypeStruct((R, C), F32),
        scratch_shapes=[pltpu.VMEM((8, R, C), F32), pltpu.SemaphoreType.DMA((7,)), pltpu.SemaphoreType.DMA((7,))],
    )(part)


def _forward_backward(x2, tgt, norm_gain, w_all, lb_logits, hgrn_gnorm, w_out_all, fgain):
    S, D = x2.shape
    SEG = w_all.shape[2] // 2
    slopes = _alibi_slopes(SEG)
    h, rinv = _rms_fwd(x2, norm_gain)
    zf32 = _in_proj(h, w_all, (0, 1, 2, 3, 7), "in_proj_hgrn_gates")
    qkv = _in_proj(h, w_all, (4, 5, 6), "in_proj_qkv")
    yh, states = _hgrn_fwd(zf32, lb_logits, hgrn_gnorm)
    outs, lses = [], []
    for d in DILATIONS:
        o, l = _attn_fwd(qkv, slopes, d)
        outs.append(o)
        lses.append(l)
    o_attn, lse, ya = _attn_merge(outs, lses, zf32)
    dout, doutb, loss, dfg = _out_proj_loss(yh, ya, w_out_all, x2, tgt, fgain)
    dy = _dy_proj(doutb, w_out_all)
    g_w_out = _grad_w_out(yh, ya, doutb)
    dzh, dlogits, dgn = _hgrn_bwd(zf32, lb_logits, hgrn_gnorm, states, dy)
    do, dl, dzg = _attn_gate_bwd(dy, o_attn, zf32)
    acc = None
    order = sorted(DILATIONS, reverse=True)
    for k, d in enumerate(order):
        acc = _attn_bwd(qkv, slopes, do, lse, dl, d, acc, BF16 if k == len(order) - 1 else F32)
    sources = [dzh] + list(acc) + [dzg]
    g_w_in = _grad_w_in(h, sources)
    return loss, dfg, dlogits, dgn, g_w_out, g_w_in, sources, rinv, dout


def _grad_x(sources, w_all, x2, rinv, norm_gain, dout, token):
    return _rms_bwd(_dh_proj(sources, w_all, token), x2, rinv, norm_gain, dout)


def _local_step(x2, tgt, norm_gain, w_all, lb_logits, hgrn_gnorm, w_out_all, fgain):
    loss, dfg, dlogits, dgn, g_w_out, g_w_in, sources, rinv, dout = _forward_backward(
        x2, tgt, norm_gain, w_all, lb_logits, hgrn_gnorm, w_out_all, fgain)
    grad_x, dgain = _grad_x(sources, w_all, x2, rinv, norm_gain, dout, jnp.zeros((8, LANES), F32))
    return loss, grad_x, dgain, g_w_in, dlogits, dgn, g_w_out, dfg


def _pack_small(D, loss, dgain, dlogits, dgn, dfg):
    def row(v):
        v = v.reshape(1, -1)
        return jnp.pad(v, ((0, 0), (0, D - v.shape[1])))
    rows = [row(dgain), row(dfg), row(dlogits[0]), row(dlogits[1]), row(jnp.sum(dgn, axis=0)), row(loss)]
    rows += [jnp.zeros((1, D), F32)] * (8 - len(rows))
    return jnp.concatenate(rows, axis=0)


def kernel(x, norm_gain, w_in, lb_logits, hgrn_gnorm, w_out, final_gain, loss_target, m_norm_gain, m_w_in, m_lb_logits, m_hgrn_gnorm, m_w_out, m_final_gain, v_norm_gain, v_w_in, v_lb_logits, v_hgrn_gnorm, v_w_out, v_final_gain):
    _, S, D = x.shape
    SEG = w_in.shape[2] // 2
    x2 = x[0]
    tgt = loss_target[0]
    fgain = final_gain.reshape(1, D)
    where = jnp.stack([lax.axis_index("c"), 2 * lax.axis_index("x") + lax.axis_index("y")]).astype(jnp.int32)

    w_all, w_out_all = _gather_weights(_cast_into_slot(w_in[0], where, "cast_w_in"),
                                       _cast_into_slot(w_out[0], where, "cast_w_out"))
    w_out_all = w_out_all.reshape(2 * SEG, D)

    loss, dfg, dlogits, dgn, g_w_out, g_w_in, sources, rinv, dout = _forward_backward(
        x2, tgt, norm_gain, w_all, lb_logits, hgrn_gnorm, w_out_all, fgain)

    sib_i, sib_o = _swap_halves(g_w_in, g_w_out)
    qi, own_i = _pair_sum(g_w_in, sib_i, where, "pair_sum_w_in")
    qo, own_o = _pair_sum(g_w_out, sib_o, where, "pair_sum_w_out")
    send_sems, recv_sems, qi, qo, ri, ro, token = _scatter_start(qi, qo)
    grad_x, dgain = _grad_x(sources, w_all, x2, rinv, norm_gain, dout, token)
    got_i, got_o = _scatter_wait(send_sems, recv_sems, qi, qo, ri, ro, grad_x)
    grad_w_in, grad_w_out = _join_halves(_chip_sum(own_i, got_i, where, "chip_sum_w_in"),
                                         _chip_sum(own_o, got_o, where, "chip_sum_w_out"))

    small = _all_reduce_small(_pack_small(D, loss, dgain, dlogits, dgn, dfg))
    grad_norm_gain = small[0:1, :]
    grad_final_gain = small[1:2, :]
    grad_lb_logits = small[2:4, :SEG]
    grad_hgrn_gnorm = small[4:5, :HGRN_HEAD]
    loss_sum = small[5, 0]

    d_ng, m_ng, v_ng = _adamw(norm_gain, grad_norm_gain, m_norm_gain, v_norm_gain, "adamw_norm_gain")
    d_wi, m_wi, v_wi = _adamw(w_in[0], grad_w_in, m_w_in[0], v_w_in[0], "adamw_w_in")
    d_lb, m_lb, v_lb = _adamw(lb_logits, grad_lb_logits, m_lb_logits, v_lb_logits, "adamw_lb_logits")
    d_gn, m_gn, v_gn = _adamw(hgrn_gnorm, grad_hgrn_gnorm, m_hgrn_gnorm, v_hgrn_gnorm, "adamw_hgrn_gnorm")
    d_wo, m_wo, v_wo = _adamw(w_out[0], grad_w_out, m_w_out[0], v_w_out[0], "adamw_w_out")
    d_fg, m_fg, v_fg = _adamw(fgain, grad_final_gain, m_final_gain.reshape(1, D), v_final_gain.reshape(1, D),
                              "adamw_final_gain")

    return (loss_sum, grad_x[None],
            grad_norm_gain, grad_w_in[None], grad_lb_logits, grad_hgrn_gnorm, grad_w_out[None], grad_final_gain[0],
            d_ng, d_wi[None], d_lb, d_gn, d_wo[None], d_fg[0],
            m_ng, m_wi[None], m_lb, m_gn, m_wo[None], m_fg[0],
            v_ng, v_wi[None], v_lb, v_gn, v_wo[None], v_fg[0])
```

```python
import jax
import jax.numpy as jnp
import numpy as np
from jax import lax
from jax.experimental import pallas as pl
from jax.experimental.pallas import tpu as pltpu

F32 = jnp.float32
BF16 = jnp.bfloat16
MESH = pl.DeviceIdType.MESH

NORM_EPS = 1e-6
HGRN_HEAD = 128
HGRN_CHUNK = 64
HGRN_TILE = 128
HGRN_BLOCK = 512
ATTN_HEAD = 64
LANES = 128
BAND = 128
DILATIONS = (1, 4, 16)
ATTN_SCALE = ATTN_HEAD ** -0.5
assert ATTN_SCALE == 0.125
ATTN_BLOCK_ELEMS = BAND * 2048
ATTN_UNROLL = 4
SEG_QKV = 4
SEG_GATE_A = 7
NEG = -1e30

ADAM_LR = 0.001
ADAM_B1 = 0.9
ADAM_B2 = 0.999
ADAM_EPS = 1e-08
ADAM_WD = 0.01
ADAM_STEP = 10

MIB = 1024 * 1024


def _cp(semantics=None, vmem_mib=48):
    return pltpu.CompilerParams(dimension_semantics=semantics, vmem_limit_bytes=vmem_mib * MIB)


def _dot(a, b):
    return jnp.dot(a, b, preferred_element_type=F32)


def _dot_nt(a, b):
    return lax.dot_general(a, b, (((1,), (1,)), ((), ())), preferred_element_type=F32)


def _dot_tn(a, b):
    return lax.dot_general(a, b, (((0,), (0,)), ((), ())), preferred_element_type=F32)


def _split3(x):
    hi = x.astype(BF16)
    r1 = x - hi.astype(F32)
    mid = r1.astype(BF16)
    lo = (r1 - mid.astype(F32)).astype(BF16)
    return hi, mid, lo


def _exact_dot(t_bf16, x):
    hi, mid, lo = _split3(x)
    return _dot(t_bf16, hi) + _dot(t_bf16, mid) + _dot(t_bf16, lo)


def _exact_dot_right(x, t_bf16):
    hi, mid, lo = _split3(x)
    return _dot(hi, t_bf16) + _dot(mid, t_bf16) + _dot(lo, t_bf16)


def _sigmoid(z):
    return jax.nn.sigmoid(z)


def _silu_and_grad(z):
    s = _sigmoid(z)
    return z * s, s * (1.0 + z * (1.0 - s))


def _seg_select(j, values):
    out = values[0]
    for t, v in enumerate(values[1:], 1):
        out = jnp.where(j == t, v, out)
    return out


def _rms_fwd(x2, gain, token):
    S, D = x2.shape
    tm = min(512, S)

    def body(x_ref, g_ref, _, h_ref, r_ref):
        x = x_ref[...]
        r = lax.rsqrt(jnp.mean(x * x, axis=-1, keepdims=True) + NORM_EPS)
        h_ref[...] = ((x * r) * g_ref[...]).astype(BF16)
        r_ref[...] = r

    return pl.pallas_call(
        body, grid=(S // tm,), name="rms_fwd",
        in_specs=[pl.BlockSpec((tm, D), lambda i: (i, 0)), pl.BlockSpec((1, D), lambda i: (0, 0)),
                  pl.BlockSpec(token.shape, lambda i: (0, 0))],
        out_specs=[pl.BlockSpec((tm, D), lambda i: (i, 0)), pl.BlockSpec((tm, 1), lambda i: (i, 0))],
        out_shape=[jax.ShapeDtypeStruct((S, D), BF16), jax.ShapeDtypeStruct((S, 1), F32)],
        compiler_params=_cp(("parallel",)),
    )(x2, gain, token)


def _in_proj(h, w_all, where, first, count, z_prev, name):
    S, D = h.shape
    SEG = w_all.shape[2] // 2
    tm = min(512, S)

    def body(*refs):
        h_ref, w_ref, o_ref = refs[1], refs[2], refs[-1]
        o_ref[...] = _dot(h_ref[...], w_ref[...])

    def seg_of(j, w):
        return (2 * w[1] + first + j) % 8

    in_specs = [pl.BlockSpec((tm, D), lambda j, i, w: (i, 0)),
                pl.BlockSpec((None, D, SEG), lambda j, i, w: (seg_of(j, w) // 2, 0, seg_of(j, w) % 2))]
    args = [where, h, w_all]
    aliases = {}
    if z_prev is not None:
        in_specs.append(ANY)
        args.append(z_prev)
        aliases = {3: 0}
    grid_spec = pltpu.PrefetchScalarGridSpec(
        num_scalar_prefetch=1, grid=(count, S // tm), in_specs=in_specs,
        out_specs=pl.BlockSpec((None, tm, SEG), lambda j, i, w: (seg_of(j, w), i, 0)))
    return pl.pallas_call(
        body, grid_spec=grid_spec, name=name, out_shape=jax.ShapeDtypeStruct((8, S, SEG), F32),
        input_output_aliases=aliases, compiler_params=_cp(("parallel", "parallel")),
    )(*args)


def _out_proj_loss(yh, ya, w_out, x2, tgt, fgain):
    S, D = x2.shape
    SEG = yh.shape[1]
    tm = min(256, S)

    def body(yh_ref, ya_ref, w_ref, x_ref, t_ref, fg_ref, dout_ref, doutb_ref, loss_ref, dfg_ref):
        i = pl.program_id(0)

        @pl.when(i == 0)
        def _():
            loss_ref[...] = jnp.zeros_like(loss_ref)
            dfg_ref[...] = jnp.zeros_like(dfg_ref)

        out = x_ref[...] + _dot(yh_ref[...], w_ref[pl.ds(0, SEG), :]) + _dot(ya_ref[...], w_ref[pl.ds(SEG, SEG), :])
        r = lax.rsqrt(jnp.mean(out * out, axis=-1, keepdims=True) + NORM_EPS)
        n = out * r
        fg = fg_ref[...]
        err = n * fg - t_ref[...]
        loss_ref[...] += 0.5 * jnp.sum(jnp.mean(err * err, axis=-1, keepdims=True), axis=0, keepdims=True)
        dy = err * (1.0 / D)
        dfg_ref[...] += jnp.sum(dy * n, axis=0, keepdims=True)
        dn = dy * fg
        dout = r * (dn - n * jnp.mean(dn * n, axis=-1, keepdims=True))
        dout_ref[...] = dout
        doutb_ref[...] = dout.astype(BF16)

    row = lambda i: (i, 0)
    fix = lambda i: (0, 0)
    return pl.pallas_call(
        body, grid=(S // tm,), name="out_proj_loss",
        in_specs=[pl.BlockSpec((tm, SEG), row), pl.BlockSpec((tm, SEG), row), pl.BlockSpec((2 * SEG, D), fix),
                  pl.BlockSpec((tm, D), row), pl.BlockSpec((tm, D), row), pl.BlockSpec((1, D), fix)],
        out_specs=[pl.BlockSpec((tm, D), row), pl.BlockSpec((tm, D), row), pl.BlockSpec((1, 1), fix),
                   pl.BlockSpec((1, D), fix)],
        out_shape=[jax.ShapeDtypeStruct((S, D), F32), jax.ShapeDtypeStruct((S, D), BF16),
                   jax.ShapeDtypeStruct((1, 1), F32), jax.ShapeDtypeStruct((1, D), F32)],
        compiler_params=_cp(("arbitrary",)),
    )(yh, ya, w_out, x2, tgt, fgain)


def _dy_proj(doutb, w_out):
    S, D = doutb.shape
    K = w_out.shape[0]
    tm = min(512, S)

    def body(d_ref, w_ref, o_ref):
        o_ref[...] = _dot_nt(d_ref[...], w_ref[...])

    return pl.pallas_call(
        body, grid=(S // tm,), name="dy_proj",
        in_specs=[pl.BlockSpec((tm, D), lambda i: (i, 0)), pl.BlockSpec((K, D), lambda i: (0, 0))],
        out_specs=pl.BlockSpec((tm, K), lambda i: (i, 0)),
        out_shape=jax.ShapeDtypeStruct((S, K), F32),
        compiler_params=_cp(("parallel",)),
    )(doutb, w_out)


def _grad_w_out(yh, ya, doutb):
    S, SEG = yh.shape
    D = doutb.shape[1]
    R = (2 * SEG) // 4
    nb_half = SEG // R
    tk = min(512, S)

    def body(yh_ref, ya_ref, d_ref, o_ref):
        q = pl.program_id(0)
        k = pl.program_id(1)

        @pl.when(k == 0)
        def _():
            o_ref[...] = jnp.zeros_like(o_ref)

        @pl.when(q < nb_half)
        def _():
            o_ref[...] += _dot_tn(yh_ref[...], d_ref[...])

        @pl.when(q >= nb_half)
        def _():
            o_ref[...] += _dot_tn(ya_ref[...], d_ref[...])

    return pl.pallas_call(
        body, grid=(4, S // tk), name="grad_w_out",
        in_specs=[pl.BlockSpec((tk, R), lambda q, k: (k, jnp.minimum(q, nb_half - 1))),
                  pl.BlockSpec((tk, R), lambda q, k: (k, jnp.maximum(q - nb_half, 0))),
                  pl.BlockSpec((tk, D), lambda q, k: (k, 0))],
        out_specs=pl.BlockSpec((None, R, D), lambda q, k: (q, 0, 0)),
        out_shape=jax.ShapeDtypeStruct((4, R, D), F32),
        compiler_params=_cp(("parallel", "arbitrary")),
    )(yh, ya, doutb)


def _dz_sources(sources):
    counts = [s.shape[0] for s in sources]
    starts = [sum(counts[:k]) for k in range(len(counts))]
    assert sum(counts) == 8
    return counts, starts


def _dh_proj(sources, w_all, token, part, name):
    S = sources[0].shape[1]
    D = w_all.shape[1]
    SEG = w_all.shape[2] // 2
    counts, starts = _dz_sources(sources)
    ns = len(sources)
    tm = min(512, S // 2)
    nt = (S // 2) // tm
    t0 = part * nt

    def body(*refs):
        src = refs[:ns]
        w_ref, _, o_ref = refs[ns:]
        j = pl.program_id(1)

        @pl.when(j == 0)
        def _():
            o_ref[...] = jnp.zeros_like(o_ref)

        for k in range(ns):
            @pl.when((j >= starts[k]) & (j < starts[k] + counts[k]))
            def _(k=k):
                o_ref[...] += _dot_nt(src[k][...], w_ref[...])

    def src_spec(k):
        return pl.BlockSpec((None, tm, SEG),
                            lambda i, j: (jnp.clip(j - starts[k], 0, counts[k] - 1), t0 + i, 0))

    return pl.pallas_call(
        body, grid=(nt, 8), name=name,
        in_specs=[src_spec(k) for k in range(ns)] + [pl.BlockSpec((None, D, SEG), lambda i, j: (j // 2, 0, j % 2)),
                                                     pl.BlockSpec(token.shape, lambda i, j: (0, 0))],
        out_specs=pl.BlockSpec((tm, D), lambda i, j: (i, 0)),
        out_shape=jax.ShapeDtypeStruct((S // 2, D), F32),
        compiler_params=_cp(("parallel", "arbitrary")),
    )(*sources, w_all, token)


def _rms_bwd(dh, x2, rinv, gain, dout, part, gx_prev, name):
    S, D = x2.shape
    tm = min(256, S // 2)
    nt = (S // 2) // tm
    t0 = part * nt

    def body(dh_ref, x_ref, r_ref, g_ref, dout_ref, *rest):
        gx_ref, dg_ref = rest[-2:]

        @pl.when(pl.program_id(0) == 0)
        def _():
            dg_ref[...] = jnp.zeros_like(dg_ref)

        dh = dh_ref[...]
        r = r_ref[...]
        xhat = x_ref[...] * r
        dg_ref[...] += jnp.sum(dh * xhat, axis=0, keepdims=True)
        dxn = dh * g_ref[...]
        gx_ref[...] = dout_ref[...] + r * (dxn - xhat * jnp.mean(dxn * xhat, axis=-1, keepdims=True))

    row = lambda i: (t0 + i, 0)
    fix = lambda i: (0, 0)
    in_specs = [pl.BlockSpec((tm, D), lambda i: (i, 0)), pl.BlockSpec((tm, D), row), pl.BlockSpec((tm, 1), row),
                pl.BlockSpec((1, D), fix), pl.BlockSpec((tm, D), row)]
    args = [dh, x2, rinv, gain, dout]
    aliases = {}
    if gx_prev is not None:
        in_specs.append(ANY)
        args.append(gx_prev)
        aliases = {5: 0}
    return pl.pallas_call(
        body, grid=(nt,), name=name, in_specs=in_specs,
        out_specs=[pl.BlockSpec((tm, D), row), pl.BlockSpec((1, D), fix)],
        out_shape=[jax.ShapeDtypeStruct((S, D), F32), jax.ShapeDtypeStruct((1, D), F32)],
        input_output_aliases=aliases, compiler_params=_cp(("arbitrary",)),
    )(*args)


def _grad_w_in(h, sources):
    S, D = h.shape
    SEG = sources[0].shape[2]
    counts, starts = _dz_sources(sources)
    ns = len(sources)
    tk = min(512, S)

    def body(*refs):
        h_ref = refs[0]
        src = refs[1:1 + ns]
        o_ref = refs[1 + ns]
        j = pl.program_id(0)
        k = pl.program_id(1)

        @pl.when(k == 0)
        def _():
            o_ref[...] = jnp.zeros_like(o_ref)

        for s in range(ns):
            @pl.when((j >= starts[s]) & (j < starts[s] + counts[s]))
            def _(s=s):
                o_ref[...] += _dot_tn(h_ref[...], src[s][...])

    def src_spec(s):
        return pl.BlockSpec((None, tk, SEG),
                            lambda j, k: (jnp.clip(j - starts[s], 0, counts[s] - 1), k, 0))

    return pl.pallas_call(
        body, grid=(8, S // tk), name="grad_w_in",
        in_specs=[pl.BlockSpec((tk, D), lambda j, k: (k, 0))] + [src_spec(s) for s in range(ns)],
        out_specs=pl.BlockSpec((None, D, SEG), lambda j, k: (j // 2, 0, j % 2)),
        out_shape=jax.ShapeDtypeStruct((4, D, 2 * SEG), F32),
        compiler_params=_cp(("parallel", "arbitrary")),
    )(h, *sources)


def _lower_bound(lbl):
    l0 = lbl[0:1, :]
    l1 = lbl[1:2, :]
    m = jnp.maximum(l0, l1)
    e0 = jnp.exp(l0 - m)
    e1 = jnp.exp(l1 - m)
    return e0 / (e0 + e1)


def _tile_masks():
    row = lax.broadcasted_iota(jnp.int32, (HGRN_TILE, HGRN_TILE), 0)
    col = lax.broadcasted_iota(jnp.int32, (HGRN_TILE, HGRN_TILE), 1)
    same = (row // HGRN_CHUNK) == (col // HGRN_CHUNK)
    return same & (row >= col), same & (row <= col)


def _chunk_last(b):
    T = b.shape[0]
    b3 = b.reshape(T // HGRN_CHUNK, HGRN_CHUNK, HGRN_HEAD)
    return jnp.broadcast_to(b3[:, HGRN_CHUNK - 1:HGRN_CHUNK, :], b3.shape).reshape(T, HGRN_HEAD)


def _chunk_sum(x):
    T = x.shape[0]
    x3 = x.reshape(T // HGRN_CHUNK, HGRN_CHUNK, HGRN_HEAD)
    return jnp.broadcast_to(jnp.sum(x3, axis=1, keepdims=True), x3.shape).reshape(T, HGRN_HEAD)


def _hgrn_dims(S, SEG):
    T = min(HGRN_BLOCK, S)
    assert S % T == 0 and T % HGRN_TILE == 0
    tiles = [slice(t * HGRN_TILE, (t + 1) * HGRN_TILE) for t in range(T // HGRN_TILE)]
    chunks = [slice(c * HGRN_CHUNK, (c + 1) * HGRN_CHUNK) for c in range(T // HGRN_CHUNK)]
    return SEG // HGRN_HEAD, T, T // HGRN_CHUNK, S // T, tiles, chunks


def _hgrn_fwd(zf32, lb_logits, gnorm):
    _, S, SEG = zf32.shape
    H, T, NC, NJ, tiles, chunks = _hgrn_dims(S, SEG)

    def body(zq_ref, zf_ref, zi_ref, zg_ref, lbl_ref, gn_ref, y_ref, st_ref, state):
        @pl.when(pl.program_id(1) == 0)
        def _():
            state[...] = jnp.zeros_like(state)

        lb = _lower_bound(lbl_ref[...])
        tril, _ = _tile_masks()
        tril_bf = tril.astype(BF16)
        zq = zq_ref[...]
        q = zq * _sigmoid(zq)
        f = lb + (1.0 - lb) * _sigmoid(zf_ref[...])
        k = 1.0 - f
        logf = jnp.log(f)
        b = jnp.concatenate([_exact_dot(tril_bf, logf[t]) for t in tiles], axis=0)
        bl = _chunk_last(b)
        qd_b = (q * jnp.exp(b)).astype(BF16)
        kd_b = (k * jnp.exp(-b)).astype(BF16)
        ke_b = (k * jnp.exp(bl - b)).astype(BF16)
        v_b = zi_ref[...].astype(BF16)
        o_intra = jnp.concatenate(
            [_dot(jnp.where(tril, _dot_nt(qd_b[t], kd_b[t]), 0.0).astype(BF16), v_b[t]) for t in tiles], axis=0)
        kvs = [_dot_tn(v_b[r], ke_b[r]) for r in chunks]
        ebl = jnp.exp(bl)
        st = state[...]
        sts = []
        for c in range(NC):
            st_ref[c] = st
            sts.append(st.astype(BF16))
            st = st * ebl[c * HGRN_CHUNK:c * HGRN_CHUNK + 1, :] + kvs[c]
        state[...] = st
        o = o_intra + jnp.concatenate([_dot_nt(qd_b[r], sb) for r, sb in zip(chunks, sts)], axis=0)
        on = o * lax.rsqrt(jnp.mean(o * o, axis=-1, keepdims=True) + NORM_EPS) * gn_ref[...]
        zg = zg_ref[...]
        y_ref[...] = (on * (zg * _sigmoid(zg))).astype(BF16)

    def zspec(seg):
        return pl.BlockSpec((None, T, HGRN_HEAD), lambda h, j: (seg, j, h))

    return pl.pallas_call(
        body, grid=(H, NJ), name="hgrn_fwd",
        in_specs=[zspec(0), zspec(1), zspec(2), zspec(3),
                  pl.BlockSpec((2, HGRN_HEAD), lambda h, j: (0, h)),
                  pl.BlockSpec((1, HGRN_HEAD), lambda h, j: (0, 0))],
        out_specs=[pl.BlockSpec((T, HGRN_HEAD), lambda h, j: (j, h)),
                   pl.BlockSpec((NC, None, HGRN_HEAD, HGRN_HEAD), lambda h, j: (j, h, 0, 0))],
        out_shape=[jax.ShapeDtypeStruct((S, SEG), BF16),
                   jax.ShapeDtypeStruct((S // HGRN_CHUNK, H, HGRN_HEAD, HGRN_HEAD), F32)],
        scratch_shapes=[pltpu.VMEM((HGRN_HEAD, HGRN_HEAD), F32)],
        compiler_params=_cp(("parallel", "arbitrary")),
    )(zf32, zf32, zf32, zf32, lb_logits, gnorm)


def _hgrn_bwd(zf32, lb_logits, gnorm, states, dy):
    _, S, SEG = zf32.shape
    H, T, NC, NJ, tiles, chunks = _hgrn_dims(S, SEG)
    C = HGRN_CHUNK

    def body(zq_ref, zf_ref, zi_ref, zg_ref, lbl_ref, gn_ref, st_ref, dy_ref, dz_ref, dl_ref, dgn_ref, gstate):
        @pl.when(pl.program_id(1) == 0)
        def _():
            gstate[...] = jnp.zeros_like(gstate)
            dl_ref[...] = jnp.zeros_like(dl_ref)
            dgn_ref[...] = jnp.zeros_like(dgn_ref)

        lb = _lower_bound(lbl_ref[...])
        gn = gn_ref[...]
        tril, triu = _tile_masks()
        tril_bf = tril.astype(BF16)
        triu_bf = triu.astype(BF16)
        q, dq_dz = _silu_and_grad(zq_ref[...])
        sf = _sigmoid(zf_ref[...])
        f = lb + (1.0 - lb) * sf
        k = 1.0 - f
        logf = jnp.log(f)
        b = jnp.concatenate([_exact_dot(tril_bf, logf[t]) for t in tiles], axis=0)
        bl = _chunk_last(b)
        eb = jnp.exp(b)
        enb = jnp.exp(-b)
        ekl = jnp.exp(bl - b)
        ebl = jnp.exp(bl)
        qd = q * eb
        kd = k * enb
        ke = k * ekl
        qd_b = qd.astype(BF16)
        kd_b = kd.astype(BF16)
        ke_b = ke.astype(BF16)
        v_b = zi_ref[...].astype(BF16)
        sts = [st_ref[c] for c in range(NC)]
        sts_b = [s.astype(BF16) for s in sts]
        a_b = [jnp.where(tril, _dot_nt(qd_b[t], kd_b[t]), 0.0).astype(BF16) for t in tiles]
        o = (jnp.concatenate([_dot(a, v_b[t]) for a, t in zip(a_b, tiles)], axis=0)
             + jnp.concatenate([_dot_nt(qd_b[r], sb) for r, sb in zip(chunks, sts_b)], axis=0))
        rinv = lax.rsqrt(jnp.mean(o * o, axis=-1, keepdims=True) + NORM_EPS)
        ohat = o * rinv
        sg, dsg = _silu_and_grad(zg_ref[...])
        dyv = dy_ref[...]
        don = dyv * sg
        dz_ref[3] = (dyv * (ohat * gn) * dsg).astype(BF16)
        dgn_ref[...] += jnp.sum(don * ohat, axis=0, keepdims=True)
        dohat = don * gn
        do = rinv * (dohat - ohat * jnp.mean(dohat * ohat, axis=-1, keepdims=True))
        do_b = do.astype(BF16)
        da_b = [jnp.where(tril, _dot_nt(do_b[t], v_b[t]), 0.0).astype(BF16) for t in tiles]
        dv_intra = jnp.concatenate([_dot_tn(a, do_b[t]) for a, t in zip(a_b, tiles)], axis=0)
        dqd_intra = jnp.concatenate([_dot(da, kd_b[t]) for da, t in zip(da_b, tiles)], axis=0)
        dkd = jnp.concatenate([_dot_tn(da, qd_b[t]) for da, t in zip(da_b, tiles)], axis=0)
        dqd_inter = jnp.concatenate([_dot(do_b[r], sb) for r, sb in zip(chunks, sts_b)], axis=0)
        gks = [_dot_tn(do_b[r], qd_b[r]) for r in chunks]
        g = gstate[...]
        gs = [None] * NC
        for c in reversed(range(NC)):
            gs[c] = g
            g = g * ebl[c * C:c * C + 1, :] + gks[c]
        gstate[...] = g
        gs_b = [x.astype(BF16) for x in gs]
        dv = dv_intra + jnp.concatenate([_dot_nt(ke_b[r], gb) for r, gb in zip(chunks, gs_b)], axis=0)
        dz_ref[2] = dv.astype(BF16)
        dke = jnp.concatenate([_dot(v_b[r], gb) for r, gb in zip(chunks, gs_b)], axis=0)
        debl = jnp.concatenate(
            [jnp.broadcast_to(jnp.sum(x * s, axis=0, keepdims=True), (C, HGRN_HEAD)) for x, s in zip(gs, sts)], axis=0)
        dqd = dqd_intra + dqd_inter
        dz_ref[0] = ((dqd * eb) * dq_dz).astype(BF16)
        t_ke = dke * ke
        db = dqd * qd - dkd * kd - t_ke
        db_last = _chunk_sum(t_ke) + debl * ebl
        dk = dkd * enb + dke * ekl
        dlogf = jnp.concatenate([_exact_dot(triu_bf, db[t]) for t in tiles], axis=0) + db_last
        df = dlogf / f - dk
        dz_ref[1] = (df * (1.0 - lb) * (sf * (1.0 - sf))).astype(BF16)
        dlb = jnp.sum(df * (1.0 - sf), axis=0, keepdims=True)
        dl0 = dlb * lb * (1.0 - lb)
        dl_ref[0:1, :] += dl0
        dl_ref[1:2, :] -= dl0

    def zspec(seg):
        return pl.BlockSpec((None, T, HGRN_HEAD), lambda h, j: (seg, NJ - 1 - j, h))

    return pl.pallas_call(
        body, grid=(H, NJ), name="hgrn_bwd",
        in_specs=[zspec(0), zspec(1), zspec(2), zspec(3),
                  pl.BlockSpec((2, HGRN_HEAD), lambda h, j: (0, h)),
                  pl.BlockSpec((1, HGRN_HEAD), lambda h, j: (0, 0)),
                  pl.BlockSpec((NC, None, HGRN_HEAD, HGRN_HEAD), lambda h, j: (NJ - 1 - j, h, 0, 0)),
                  pl.BlockSpec((T, HGRN_HEAD), lambda h, j: (NJ - 1 - j, h))],
        out_specs=[pl.BlockSpec((4, T, HGRN_HEAD), lambda h, j: (0, NJ - 1 - j, h)),
                   pl.BlockSpec((2, HGRN_HEAD), lambda h, j: (0, h)),
                   pl.BlockSpec((None, 1, HGRN_HEAD), lambda h, j: (h, 0, 0))],
        out_shape=[jax.ShapeDtypeStruct((4, S, SEG), BF16), jax.ShapeDtypeStruct((2, SEG), F32),
                   jax.ShapeDtypeStruct((H, 1, HGRN_HEAD), F32)],
        scratch_shapes=[pltpu.VMEM((HGRN_HEAD, HGRN_HEAD), F32)],
        compiler_params=_cp(("parallel", "arbitrary")),
    )(zf32, zf32, zf32, zf32, lb_logits, gnorm, states, dy)


def _alibi_slopes(seg):
    n_heads = seg // ATTN_HEAD
    s = 2.0 ** (-8.0 * np.arange(1, n_heads + 1, dtype=np.float64) / n_heads)
    return jnp.asarray(np.repeat(s, ATTN_HEAD)[None, :], F32)


def _attn_dims(S, SEG, d):
    rb = BAND * d
    assert S % rb == 0 and SEG % LANES == 0
    cb = min(SEG, ATTN_BLOCK_ELEMS // rb) if d == 1 else LANES
    assert SEG % cb == 0
    return rb, cb, S // rb, SEG // cb


def _res_rows(r, d):
    return pl.ds(0, BAND) if d == 1 else pl.ds(r, BAND, stride=d)


def _for_residues(d, fn):
    if d == 1:
        fn(0)
    else:
        def step(r, carry):
            fn(r)
            return carry
        lax.fori_loop(0, d, step, 0, unroll=ATTN_UNROLL)


def _band_terms(n, d):
    i = lax.broadcasted_iota(jnp.int32, (BAND, 2 * BAND), 0)
    jj = lax.broadcasted_iota(jnp.int32, (BAND, 2 * BAND), 1)
    delta = BAND + i - jj
    valid = (delta >= 0) & (delta <= BAND) & ((n > 0) | (jj >= BAND))
    return (-d * delta).astype(F32), valid


def _head_biases(slopes, nd, valid):
    out = []
    for s in _per_head(slopes):
        s2 = jnp.concatenate([s, s], axis=1)
        out.append(jnp.where(valid, s2 * nd, NEG))
    return jnp.concatenate(out, axis=0)


def _stack_heads(x):
    lane = lax.broadcasted_iota(jnp.int32, x.shape, 1)
    zero = jnp.zeros_like(x)
    return jnp.concatenate([jnp.where(lane < ATTN_HEAD, x, zero), jnp.where(lane < ATTN_HEAD, zero, x)], axis=0)


def _unstack_heads(x2):
    first = lax.broadcasted_iota(jnp.int32, (BAND, LANES), 1) < ATTN_HEAD
    return jnp.where(first, x2[:BAND], x2[BAND:])


def _stack_per_head(x):
    a, b = _per_head(x)
    col = jnp.concatenate([a, b], axis=0)
    return jnp.concatenate([col, col], axis=1)


def _per_head(x):
    lane = lax.broadcasted_iota(jnp.int32, x.shape, 1)
    sw = pltpu.roll(x, ATTN_HEAD, 1)
    first = lane < ATTN_HEAD
    return jnp.where(first, x, sw), jnp.where(first, sw, x)


def _attn_fwd(qkv, slopes, d):
    _, S, SEG = qkv.shape
    rb, cb, nb, ncb = _attn_dims(S, SEG, d)
    NP = cb // LANES

    def body(q_ref, kp_ref, kc_ref, vp_ref, vc_ref, sl_ref, o_ref, l_ref):
        n = pl.program_id(1)
        nd, valid = _band_terms(n, d)
        biases = [_head_biases(sl_ref[:, p * LANES:(p + 1) * LANES], nd, valid) for p in range(NP)]

        def residue(r):
            rows = _res_rows(r, d)
            for p in range(NP):
                cols = slice(p * LANES, (p + 1) * LANES)
                kc = jnp.concatenate([kp_ref[rows, cols], kc_ref[rows, cols]], axis=0).astype(BF16)
                vc = jnp.concatenate([vp_ref[rows, cols], vc_ref[rows, cols]], axis=0).astype(BF16)
                s = _dot_nt(_stack_heads((q_ref[rows, cols] * ATTN_SCALE).astype(BF16)), kc) + biases[p]
                m = jnp.max(s, axis=-1, keepdims=True)
                e = jnp.exp(s - m)
                den = jnp.sum(e, axis=-1, keepdims=True)
                o_ref[rows, cols] = _unstack_heads(_dot(e.astype(BF16), vc) / den)
                l_ref[rows, cols] = _unstack_heads(jnp.broadcast_to(m + jnp.log(den), (2 * BAND, LANES)))

        _for_residues(d, residue)

    def spec(seg, prev):
        if prev:
            return pl.BlockSpec((None, rb, cb), lambda c, n: (SEG_QKV + seg, jnp.maximum(n - 1, 0), c))
        return pl.BlockSpec((None, rb, cb), lambda c, n: (SEG_QKV + seg, n, c))

    out = pl.BlockSpec((rb, cb), lambda c, n: (n, c))
    return pl.pallas_call(
        body, grid=(ncb, nb), name=f"attn_fwd_d{d}",
        in_specs=[spec(0, False), spec(1, True), spec(1, False), spec(2, True), spec(2, False),
                  pl.BlockSpec((1, cb), lambda c, n: (0, c))],
        out_specs=[out, out],
        out_shape=[jax.ShapeDtypeStruct((S, SEG), F32), jax.ShapeDtypeStruct((S, SEG), F32)],
        compiler_params=_cp(("parallel", "parallel")),
    )(qkv, qkv, qkv, qkv, qkv, slopes)


def _attn_merge(outs, lses, zf32):
    S, SEG = outs[0].shape
    tm = min(256, S)

    def body(o1, o2, o3, l1, l2, l3, zg_ref, o_ref, lse_ref, y_ref):
        a, b, c = l1[...], l2[...], l3[...]
        m = jnp.maximum(jnp.maximum(a, b), c)
        ea, eb, ec = jnp.exp(a - m), jnp.exp(b - m), jnp.exp(c - m)
        tot = ea + eb + ec
        o = (ea / tot) * o1[...] + (eb / tot) * o2[...] + (ec / tot) * o3[...]
        o_ref[...] = o
        lse_ref[...] = m + jnp.log(tot)
        zg = zg_ref[...]
        y_ref[...] = (o * (zg * _sigmoid(zg))).astype(BF16)

    row = pl.BlockSpec((tm, SEG), lambda i: (i, 0))
    return pl.pallas_call(
        body, grid=(S // tm,), name="attn_merge",
        in_specs=[row] * 6 + [pl.BlockSpec((None, tm, SEG), lambda i: (SEG_GATE_A, i, 0))],
        out_specs=[row, row, row],
        out_shape=[jax.ShapeDtypeStruct((S, SEG), F32), jax.ShapeDtypeStruct((S, SEG), F32),
                   jax.ShapeDtypeStruct((S, SEG), BF16)],
        compiler_params=_cp(("parallel",)),
    )(*outs, *lses, zf32)


def _attn_gate_bwd(dy, o, zf32):
    S, SEG = o.shape
    tm = min(256, S)
    NP = SEG // LANES

    def body(dy_ref, o_ref, zg_ref, do_ref, dl_ref, dzg_ref):
        r = lax.broadcasted_iota(jnp.int32, (LANES, LANES), 0) // ATTN_HEAD
        c = lax.broadcasted_iota(jnp.int32, (LANES, LANES), 1) // ATTN_HEAD
        same_head = (r == c).astype(BF16)
        for p in range(NP):
            cols = slice(p * LANES, (p + 1) * LANES)
            sg, dsg = _silu_and_grad(zg_ref[:, cols])
            dyv = dy_ref[:, cols]
            ov = o_ref[:, cols]
            do = dyv * sg
            do_ref[:, cols] = do
            dzg_ref[:, cols] = (dyv * ov * dsg).astype(BF16)
            dl_ref[:, cols] = _exact_dot_right(do * ov, same_head)

    return pl.pallas_call(
        body, grid=(S // tm,), name="attn_gate_bwd",
        in_specs=[pl.BlockSpec((tm, SEG), lambda i: (i, 1)), pl.BlockSpec((tm, SEG), lambda i: (i, 0)),
                  pl.BlockSpec((None, tm, SEG), lambda i: (SEG_GATE_A, i, 0))],
        out_specs=[pl.BlockSpec((tm, SEG), lambda i: (i, 0)), pl.BlockSpec((tm, SEG), lambda i: (i, 0)),
                   pl.BlockSpec((None, tm, SEG), lambda i: (0, i, 0))],
        out_shape=[jax.ShapeDtypeStruct((S, SEG), F32), jax.ShapeDtypeStruct((S, SEG), F32),
                   jax.ShapeDtypeStruct((1, S, SEG), BF16)],
        compiler_params=_cp(("parallel",)),
    )(dy, o, zf32)


def _attn_bwd(qkv, slopes, do, lse, dl, d, acc, out_dtype):
    _, S, SEG = qkv.shape
    rb, cb, nb, ncb = _attn_dims(S, SEG, d)
    NP = cb // LANES
    has_acc = acc is not None

    def body(*refs):
        q_ref, kp_ref, kc_ref, vp_ref, vc_ref, sl_ref, do_ref, lse_ref, dl_ref = refs[:9]
        refs = refs[9:]
        if has_acc:
            aq_ref, ak_ref, av_ref = refs[:3]
            refs = refs[3:]
        dq_ref, dk_ref, dv_ref, ck, cv = refs
        n = pl.program_id(1)

        @pl.when(n == 0)
        def _():
            ck[...] = jnp.zeros_like(ck)
            cv[...] = jnp.zeros_like(cv)

        @pl.when(n < nb)
        def _():
            nd, valid = _band_terms(n, d)
            biases = [_head_biases(sl_ref[:, p * LANES:(p + 1) * LANES], nd, valid) for p in range(NP)]

            def residue(r):
                rows = _res_rows(r, d)
                for p in range(NP):
                    cols = slice(p * LANES, (p + 1) * LANES)
                    kc = jnp.concatenate([kp_ref[rows, cols], kc_ref[rows, cols]], axis=0).astype(BF16)
                    vc = jnp.concatenate([vp_ref[rows, cols], vc_ref[rows, cols]], axis=0).astype(BF16)
                    qs = _stack_heads((q_ref[rows, cols] * ATTN_SCALE).astype(BF16))
                    dos = _stack_heads(do_ref[rows, cols].astype(BF16))
                    pr = jnp.exp(_dot_nt(qs, kc) + biases[p] - _stack_per_head(lse_ref[rows, cols]))
                    ds = (pr * (_dot_nt(dos, vc) - _stack_per_head(dl_ref[rows, cols]))).astype(BF16)
                    dq = _unstack_heads(_dot(ds, kc)) * ATTN_SCALE
                    dk = _dot_tn(ds, qs)
                    dv = _dot_tn(pr.astype(BF16), dos)
                    dk_prev = ck[r, :, cols] + dk[:BAND, :]
                    dv_prev = cv[r, :, cols] + dv[:BAND, :]
                    if has_acc:
                        dq = dq + aq_ref[rows, cols]
                        dk_prev = dk_prev + ak_ref[rows, cols]
                        dv_prev = dv_prev + av_ref[rows, cols]
                    dq_ref[rows, cols] = dq.astype(out_dtype)
                    dk_ref[rows, cols] = dk_prev.astype(out_dtype)
                    dv_ref[rows, cols] = dv_prev.astype(out_dtype)
                    ck[r, :, cols] = dk[BAND:, :]
                    cv[r, :, cols] = dv[BAND:, :]

            _for_residues(d, residue)

        @pl.when(n == nb)
        def _():
            def residue(r):
                rows = _res_rows(r, d)
                dk_last = ck[r]
                dv_last = cv[r]
                if has_acc:
                    dk_last = dk_last + ak_ref[rows, :]
                    dv_last = dv_last + av_ref[rows, :]
                dk_ref[rows, :] = dk_last.astype(out_dtype)
                dv_ref[rows, :] = dv_last.astype(out_dtype)

            _for_residues(d, residue)

    cur2 = lambda c, n: (jnp.minimum(n, nb - 1), c)
    cur3 = lambda c, n: (0, jnp.minimum(n, nb - 1), c)
    lag3 = lambda c, n: (0, jnp.clip(n - 1, 0, nb - 1), c)

    def spec(seg, prev):
        if prev:
            return pl.BlockSpec((None, rb, cb), lambda c, n: (SEG_QKV + seg, jnp.clip(n - 1, 0, nb - 1), c))
        return pl.BlockSpec((None, rb, cb), lambda c, n: (SEG_QKV + seg, jnp.minimum(n, nb - 1), c))

    in_specs = [spec(0, False), spec(1, True), spec(1, False), spec(2, True), spec(2, False),
                pl.BlockSpec((1, cb), lambda c, n: (0, c)),
                pl.BlockSpec((rb, cb), cur2), pl.BlockSpec((rb, cb), cur2), pl.BlockSpec((rb, cb), cur2)]
    args = [qkv, qkv, qkv, qkv, qkv, slopes, do, lse, dl]
    aliases = {}
    if has_acc:
        in_specs += [pl.BlockSpec((None, rb, cb), cur3), pl.BlockSpec((None, rb, cb), lag3),
                     pl.BlockSpec((None, rb, cb), lag3)]
        args += list(acc)
        if out_dtype == F32:
            aliases = {9: 0, 10: 1, 11: 2}
    return pl.pallas_call(
        body, grid=(ncb, nb + 1), name=f"attn_bwd_d{d}",
        in_specs=in_specs,
        out_specs=[pl.BlockSpec((None, rb, cb), cur3), pl.BlockSpec((None, rb, cb), lag3),
                   pl.BlockSpec((None, rb, cb), lag3)],
        out_shape=[jax.ShapeDtypeStruct((1, S, SEG), out_dtype)] * 3,
        scratch_shapes=[pltpu.VMEM((d, BAND, cb), F32), pltpu.VMEM((d, BAND, cb), F32)],
        input_output_aliases=aliases,
        compiler_params=_cp(("parallel", "arbitrary")),
    )(*args)


def _adamw(w, g, m, v, name):
    R, C = w.shape
    tr = R if R <= 256 else 256
    assert R % tr == 0

    def body(w_ref, g_ref, m_ref, v_ref, d_ref, nm_ref, nv_ref):
        g = g_ref[...]
        nm = ADAM_B1 * m_ref[...] + (1.0 - ADAM_B1) * g
        nv = ADAM_B2 * v_ref[...] + (1.0 - ADAM_B2) * (g * g)
        m_hat = nm / (1.0 - ADAM_B1 ** ADAM_STEP)
        v_hat = nv / (1.0 - ADAM_B2 ** ADAM_STEP)
        d_ref[...] = -ADAM_LR * (m_hat / (jnp.sqrt(v_hat) + ADAM_EPS) + ADAM_WD * w_ref[...])
        nm_ref[...] = nm
        nv_ref[...] = nv

    blk = pl.BlockSpec((tr, C), lambda i: (i, 0))
    sds = jax.ShapeDtypeStruct((R, C), F32)
    return pl.pallas_call(
        body, grid=(R // tr,), name=name, in_specs=[blk] * 4, out_specs=[blk] * 3, out_shape=[sds] * 3,
        compiler_params=_cp(("parallel",)),
    )(w, g, m, v)


def _coords():
    return lax.axis_index("x"), lax.axis_index("y"), lax.axis_index("c")


def _other_chips(x, y):
    return [(1 - x, y), (x, 1 - y), (1 - x, 1 - y)]


ANY = pl.BlockSpec(memory_space=pl.ANY)


def _cast_into_slot(w, where, name):
    R, C = w.shape
    tr = min(256, R)

    def body(where_ref, w_ref, o_ref):
        o_ref[...] = w_ref[...].astype(BF16)

    grid_spec = pltpu.PrefetchScalarGridSpec(
        num_scalar_prefetch=1, grid=(R // tr,),
        in_specs=[pl.BlockSpec((tr, C), lambda i, w: (i, 0))],
        out_specs=pl.BlockSpec((None, tr, C), lambda i, w: (w[1], i, 0)))
    return pl.pallas_call(
        body, grid_spec=grid_spec, name=name, out_shape=jax.ShapeDtypeStruct((4, R, C), BF16),
        compiler_params=_cp(("parallel",)),
    )(where, w)


def _pair_sum(g, sib, where, name):
    _, n2, C = g.shape
    N = n2 // 2
    tr = min(256, N)
    nt = N // tr

    def body(where_ref, g_ref, s_ref, qb_ref, own_ref):
        q = pl.program_id(1)
        tot = g_ref[...] + s_ref[...]
        qb_ref[...] = tot.astype(BF16)

        @pl.when(q == where_ref[1])
        def _():
            own_ref[...] = tot

    grid_spec = pltpu.PrefetchScalarGridSpec(
        num_scalar_prefetch=1, grid=(nt, 4),
        in_specs=[pl.BlockSpec((None, tr, C), lambda i, q, w: (q, w[0] * nt + i, 0)),
                  pl.BlockSpec((None, tr, C), lambda i, q, w: (q, i, 0))],
        out_specs=[pl.BlockSpec((None, tr, C), lambda i, q, w: (q, i, 0)),
                   pl.BlockSpec((tr, C), lambda i, q, w: (i, 0))])
    return pl.pallas_call(
        body, grid_spec=grid_spec, name=name,
        out_shape=[jax.ShapeDtypeStruct((4, N, C), BF16), jax.ShapeDtypeStruct((N, C), F32)],
        compiler_params=_cp(("parallel", "arbitrary")),
    )(where, g, sib)


HBM = pl.BlockSpec(memory_space=pltpu.HBM)
SEM = pl.BlockSpec(memory_space=pltpu.SEMAPHORE)


def _in_hbm(a):
    return pltpu.with_memory_space_constraint(a, pltpu.HBM)


def _split_start(name, copies, arrays, n_sems, after=None):
    n = len(arrays)

    def body(*refs):
        for cp in copies(refs[:n], refs[-n - 3], refs[-n - 2]):
            cp.start()
        refs[-1][...] = jnp.zeros_like(refs[-1])

    ordered = () if after is None else (after,)
    outs = pl.pallas_call(
        body, name=name,
        out_shape=(pltpu.SemaphoreType.DMA((n_sems,)), pltpu.SemaphoreType.DMA((n_sems,)),
                   *[pltpu.HBM(a.shape, a.dtype) for a in arrays], jax.ShapeDtypeStruct((8, LANES), F32)),
        in_specs=(HBM,) * n + (ANY,) * len(ordered),
        out_specs=(SEM, SEM) + (HBM,) * n + (pl.BlockSpec(memory_space=pltpu.VMEM),),
        input_output_aliases={i: 2 + i for i in range(n)},
        compiler_params=pltpu.CompilerParams(has_side_effects=pltpu.SideEffectType.DATAFLOW_SIDE_EFFECTING),
    )(*[_in_hbm(a) for a in arrays], *ordered)
    return outs[0], outs[1], list(outs[2:2 + n]), outs[-1]


def _split_wait(name, copies, send_sems, recv_sems, arrays, after):
    n = len(arrays)

    def body(*refs):
        for cp in copies(refs[:n], refs[n], refs[n + 1]):
            cp.wait_send()
            cp.wait_recv()

    outs = pl.pallas_call(
        body, name=name,
        out_shape=tuple(pltpu.HBM(a.shape, a.dtype) for a in arrays),
        in_specs=(HBM,) * n + (SEM, SEM, ANY), out_specs=(HBM,) * n,
        input_output_aliases={i: i for i in range(n)},
        compiler_params=pltpu.CompilerParams(has_side_effects=pltpu.SideEffectType.DATAFLOW_SIDE_EFFECTING),
    )(*arrays, send_sems, recv_sems, after)
    return list(outs)


def _remote(src, dst, sems, k, to):
    send_sems, recv_sems = sems
    return pltpu.make_async_remote_copy(src_ref=src, dst_ref=dst, send_sem=send_sems.at[k], recv_sem=recv_sems.at[k],
                                        device_id=to, device_id_type=MESH)


def _gather_in_copies(refs, send_sems, recv_sems):
    (w,) = refs
    x, y, c = _coords()
    seg = w.shape[2] // 2
    mine = w.at[2 * x + y, :, pl.ds(c * seg, seg)]
    return [_remote(mine, mine, (send_sems, recv_sems), k, (px, py, c)) for k, (px, py) in enumerate(_other_chips(x, y))]


def _gather_out_copies(refs, send_sems, recv_sems):
    (w,) = refs
    x, y, c = _coords()
    mine = w.at[2 * x + y]
    return [_remote(mine, mine, (send_sems, recv_sems), k, (px, py, c)) for k, (px, py) in enumerate(_other_chips(x, y))]


def _swap_copies(refs, send_sems, recv_sems):
    gi, go, si, so = refs
    x, y, c = _coords()
    cps = []
    for a, (src, dst) in enumerate(((gi, si), (go, so))):
        nr = dst.shape[1]
        cps.append(_remote(src.at[:, pl.ds((1 - c) * nr, nr), :], dst, (send_sems, recv_sems), a, (x, y, 1 - c)))
    return cps


def _scatter_copies(refs, send_sems, recv_sems):
    qi, qo, ri, ro = refs
    x, y, c = _coords()
    cps = []
    for k, (px, py) in enumerate(_other_chips(x, y)):
        for a, (src, dst) in enumerate(((qi, ri), (qo, ro))):
            cps.append(_remote(src.at[2 * px + py], dst.at[k], (send_sems, recv_sems), 2 * k + a, (px, py, c)))
    return cps


def _forward_segments(wia):
    seg = wia.shape[2] // 2

    def body(w_in, w_ref, send_sems, recv_sems):
        x, y, c = _coords()
        started = []
        for k, (px, py) in enumerate(_other_chips(x, y)):
            got = w_ref.at[2 * px + py, :, pl.ds(c * seg, seg)]
            cp = _remote(got, got, (send_sems, recv_sems), k, (x, y, 1 - c))
            cp.start()
            started.append(cp)
        for k, (px, py) in enumerate(_other_chips(x, y)):
            theirs = w_ref.at[2 * px + py, :, pl.ds((1 - c) * seg, seg)]
            _remote(theirs, theirs, (send_sems, recv_sems), k, (x, y, 1 - c)).wait_recv()
        for cp in started:
            cp.wait_send()

    return pl.pallas_call(
        body, name="forward_segments", in_specs=[ANY], out_specs=ANY,
        out_shape=jax.ShapeDtypeStruct(wia.shape, wia.dtype),
        scratch_shapes=[pltpu.SemaphoreType.DMA((3,)), pltpu.SemaphoreType.DMA((3,))],
        input_output_aliases={0: 0},
    )(wia)


def _chip_sum(own, got, where, name):
    N, C = own.shape
    tr = min(256, N)
    nt = N // tr

    def body(where_ref, own_ref, got_ref, o_ref):
        t = own_ref[...]
        for k in range(3):
            t = t + got_ref[k].astype(F32)
        o_ref[...] = t

    grid_spec = pltpu.PrefetchScalarGridSpec(
        num_scalar_prefetch=1, grid=(nt,),
        in_specs=[pl.BlockSpec((tr, C), lambda i, w: (i, 0)), pl.BlockSpec((3, tr, C), lambda i, w: (0, i, 0))],
        out_specs=pl.BlockSpec((tr, C), lambda i, w: (w[0] * nt + i, 0)))
    return pl.pallas_call(
        body, grid_spec=grid_spec, name=name, out_shape=jax.ShapeDtypeStruct((2 * N, C), F32),
        compiler_params=_cp(("parallel",)),
    )(where, own, got)


def _join_halves(gi, go):
    def body(gi_in, go_in, gi_ref, go_ref, send_sems, recv_sems):
        x, y, c = _coords()
        cps = []
        for a, ref in enumerate((gi_ref, go_ref)):
            nr = ref.shape[0] // 2
            mine = ref.at[pl.ds(c * nr, nr), :]
            cp = pltpu.make_async_remote_copy(src_ref=mine, dst_ref=mine, send_sem=send_sems.at[a],
                                              recv_sem=recv_sems.at[a], device_id=(x, y, 1 - c), device_id_type=MESH)
            cp.start()
            cps.append(cp)
        for a, ref in enumerate((gi_ref, go_ref)):
            nr = ref.shape[0] // 2
            theirs = ref.at[pl.ds((1 - c) * nr, nr), :]
            pltpu.make_async_remote_copy(src_ref=theirs, dst_ref=theirs, send_sem=send_sems.at[a],
                                         recv_sem=recv_sems.at[a], device_id=(x, y, 1 - c),
                                         device_id_type=MESH).wait_recv()
        for cp in cps:
            cp.wait_send()

    return pl.pallas_call(
        body, name="join_halves", in_specs=[ANY, ANY], out_specs=[ANY, ANY],
        out_shape=[jax.ShapeDtypeStruct(gi.shape, F32), jax.ShapeDtypeStruct(go.shape, F32)],
        scratch_shapes=[pltpu.SemaphoreType.DMA((2,)), pltpu.SemaphoreType.DMA((2,))],
        input_output_aliases={0: 0, 1: 1},
    )(gi, go)


def _all_reduce_small(part):
    R, C = part.shape

    def body(p_ref, o_ref, slots, send_sems, recv_sems):
        x, y, c = _coords()
        me = 4 * x + 2 * y + c
        slots[me] = p_ref[...]
        cps = []
        for k in range(1, 8):
            fx, fy, fc = (k >> 2) & 1, (k >> 1) & 1, k & 1
            peer = (1 - x if fx else x, 1 - y if fy else y, 1 - c if fc else c)
            cp = pltpu.make_async_remote_copy(src_ref=p_ref, dst_ref=slots.at[me], send_sem=send_sems.at[k - 1],
                                              recv_sem=recv_sems.at[k - 1], device_id=peer, device_id_type=MESH)
            cp.start()
            cps.append(cp)
        for cp in cps:
            cp.wait()
        t = slots[0]
        for k in range(1, 8):
            t = t + slots[k]
        o_ref[...] = t

    vm = pl.BlockSpec(memory_space=pltpu.VMEM)
    return pl.pallas_call(
        body, name="all_reduce_small", in_specs=[vm], out_specs=vm,
        out_shape=jax.ShapeDtypeStruct((R, C), F32),
        scratch_shapes=[pltpu.VMEM((8, R, C), F32), pltpu.SemaphoreType.DMA((7,)), pltpu.SemaphoreType.DMA((7,))],
    )(part)


def _mixers_forward(z, lb_logits, hgrn_gnorm):
    slopes = _alibi_slopes(z.shape[2])
    yh, states = _hgrn_fwd(z, lb_logits, hgrn_gnorm)
    outs, lses = [], []
    for d in DILATIONS:
        o, l = _attn_fwd(z, slopes, d)
        outs.append(o)
        lses.append(l)
    o_attn, lse, ya = _attn_merge(outs, lses, z)
    return yh, ya, (states, o_attn, lse, slopes)


def _backward_to_dz(z, kept, lb_logits, hgrn_gnorm, yh, ya, w_out_all, x2, tgt, fgain, h):
    states, o_attn, lse, slopes = kept
    dout, doutb, loss, dfg = _out_proj_loss(yh, ya, w_out_all, x2, tgt, fgain)
    dy = _dy_proj(doutb, w_out_all)
    g_w_out = _grad_w_out(yh, ya, doutb)
    dzh, dlogits, dgn = _hgrn_bwd(z, lb_logits, hgrn_gnorm, states, dy)
    do, dl, dzg = _attn_gate_bwd(dy, o_attn, z)
    acc = None
    order = sorted(DILATIONS, reverse=True)
    for k, d in enumerate(order):
        acc = _attn_bwd(z, slopes, do, lse, dl, d, acc, BF16 if k == len(order) - 1 else F32)
    sources = [dzh] + list(acc) + [dzg]
    g_w_in = _grad_w_in(h, sources)
    return loss, dfg, dlogits, dgn, g_w_out, g_w_in, sources, dout


def _grad_x_half(sources, w_all, x2, rinv, norm_gain, dout, token, part, gx_prev):
    dh = _dh_proj(sources, w_all, token, part, f"dh_proj_{part}")
    return _rms_bwd(dh, x2, rinv, norm_gain, dout, part, gx_prev, f"rms_bwd_{part}")


def _local_step(x2, tgt, norm_gain, w_all, lb_logits, hgrn_gnorm, w_out_all, fgain):
    token = jnp.zeros((8, LANES), F32)
    where = jnp.zeros((2,), jnp.int32)
    h, rinv = _rms_fwd(x2, norm_gain, token)
    z = _in_proj(h, w_all, where, 0, 8, None, "in_proj_all")
    yh, ya, kept = _mixers_forward(z, lb_logits, hgrn_gnorm)
    loss, dfg, dlogits, dgn, g_w_out, g_w_in, sources, dout = _backward_to_dz(
        z, kept, lb_logits, hgrn_gnorm, yh, ya, w_out_all, x2, tgt, fgain, h)
    gx, dg0 = _grad_x_half(sources, w_all, x2, rinv, norm_gain, dout, token, 0, None)
    gx, dg1 = _grad_x_half(sources, w_all, x2, rinv, norm_gain, dout, token, 1, gx)
    return loss, gx, dg0 + dg1, g_w_in, dlogits, dgn, g_w_out, dfg


def _pack_small(D, loss, dgain, dlogits, dgn, dfg):
    def row(v):
        v = v.reshape(1, -1)
        return jnp.pad(v, ((0, 0), (0, D - v.shape[1])))
    rows = [row(dgain), row(dfg), row(dlogits[0]), row(dlogits[1]), row(jnp.sum(dgn, axis=0)), row(loss)]
    rows += [jnp.zeros((1, D), F32)] * (8 - len(rows))
    return jnp.concatenate(rows, axis=0)


def kernel(x, norm_gain, w_in, lb_logits, hgrn_gnorm, w_out, final_gain, loss_target, m_norm_gain, m_w_in, m_lb_logits, m_hgrn_gnorm, m_w_out, m_final_gain, v_norm_gain, v_w_in, v_lb_logits, v_hgrn_gnorm, v_w_out, v_final_gain):
    _, S, D = x.shape
    SEG = w_in.shape[2] // 2
    x2 = x[0]
    tgt = loss_target[0]
    fgain = final_gain.reshape(1, D)
    where = jnp.stack([lax.axis_index("c"), 2 * lax.axis_index("x") + lax.axis_index("y")]).astype(jnp.int32)

    wia = _cast_into_slot(w_in[0], where, "cast_w_in")
    woa = _cast_into_slot(w_out[0], where, "cast_w_out")
    sems = _split_start("gather_in_start", _gather_in_copies, [wia], 3)
    (wia,), token = sems[2], sems[3]
    h, rinv = _rms_fwd(x2, norm_gain, token)
    z = _in_proj(h, wia, where, 0, 2, None, "in_proj_own")
    (wia,) = _split_wait("gather_in_wait", _gather_in_copies, sems[0], sems[1], [wia], z)
    wia = _forward_segments(wia)
    sems = _split_start("gather_out_start", _gather_out_copies, [woa], 3, after=wia)
    (woa,) = sems[2]
    z = _in_proj(h, wia, where, 2, 6, z, "in_proj_rest")
    yh, ya, kept = _mixers_forward(z, lb_logits, hgrn_gnorm)
    (woa,) = _split_wait("gather_out_wait", _gather_out_copies, sems[0], sems[1], [woa], ya)
    w_out_all = woa.reshape(2 * SEG, D)

    loss, dfg, dlogits, dgn, g_w_out, g_w_in, sources, dout = _backward_to_dz(
        z, kept, lb_logits, hgrn_gnorm, yh, ya, w_out_all, x2, tgt, fgain, h)

    sib_i = lax.empty((4, g_w_in.shape[1] // 2, g_w_in.shape[2]), F32)
    sib_o = lax.empty((4, g_w_out.shape[1] // 2, g_w_out.shape[2]), F32)
    sems = _split_start("swap_start", _swap_copies, [g_w_in, g_w_out, sib_i, sib_o], 2)
    grad_x, dg0 = _grad_x_half(sources, wia, x2, rinv, norm_gain, dout, sems[3], 0, None)
    g_w_in, g_w_out, sib_i, sib_o = _split_wait("swap_wait", _swap_copies, sems[0], sems[1], sems[2], grad_x)
    qi, own_i = _pair_sum(g_w_in, sib_i, where, "pair_sum_w_in")
    qo, own_o = _pair_sum(g_w_out, sib_o, where, "pair_sum_w_out")
    ri = lax.empty((3,) + qi.shape[1:], BF16)
    ro = lax.empty((3,) + qo.shape[1:], BF16)
    sems = _split_start("scatter_start", _scatter_copies, [qi, qo, ri, ro], 6)
    grad_x, dg1 = _grad_x_half(sources, wia, x2, rinv, norm_gain, dout, sems[3], 1, grad_x)
    _, _, got_i, got_o = _split_wait("scatter_wait", _scatter_copies, sems[0], sems[1], sems[2], grad_x)
    grad_w_in, grad_w_out = _join_halves(_chip_sum(own_i, got_i, where, "chip_sum_w_in"),
                                         _chip_sum(own_o, got_o, where, "chip_sum_w_out"))

    small = _all_reduce_small(_pack_small(D, loss, dg0 + dg1, dlogits, dgn, dfg))
    grad_norm_gain = small[0:1, :]
    grad_final_gain = small[1:2, :]
    grad_lb_logits = small[2:4, :SEG]
    grad_hgrn_gnorm = small[4:5, :HGRN_HEAD]
    loss_sum = small[5, 0]

    d_ng, m_ng, v_ng = _adamw(norm_gain, grad_norm_gain, m_norm_gain, v_norm_gain, "adamw_norm_gain")
    d_wi, m_wi, v_wi = _adamw(w_in[0], grad_w_in, m_w_in[0], v_w_in[0], "adamw_w_in")
    d_lb, m_lb, v_lb = _adamw(lb_logits, grad_lb_logits, m_lb_logits, v_lb_logits, "adamw_lb_logits")
    d_gn, m_gn, v_gn = _adamw(hgrn_gnorm, grad_hgrn_gnorm, m_hgrn_gnorm, v_hgrn_gnorm, "adamw_hgrn_gnorm")
    d_wo, m_wo, v_wo = _adamw(w_out[0], grad_w_out, m_w_out[0], v_w_out[0], "adamw_w_out")
    d_fg, m_fg, v_fg = _adamw(fgain, grad_final_gain, m_final_gain.reshape(1, D), v_final_gain.reshape(1, D),
                              "adamw_final_gain")

    return (loss_sum, grad_x[None],
            grad_norm_gain, grad_w_in[None], grad_lb_logits, grad_hgrn_gnorm, grad_w_out[None], grad_final_gain[0],
            d_ng, d_wi[None], d_lb, d_gn, d_wo[None], d_fg[0],
            m_ng, m_wi[None], m_lb, m_gn, m_wo[None], m_fg[0],
            v_ng, v_wi[None], v_lb, v_gn, v_wo[None], v_fg[0])
```

```python
import jax
import jax.numpy as jnp
import numpy as np
from jax import lax
from jax.experimental import pallas as pl
from jax.experimental.pallas import tpu as pltpu

F32 = jnp.float32
BF16 = jnp.bfloat16
MESH = pl.DeviceIdType.MESH

NORM_EPS = 1e-6
HGRN_HEAD = 128
HGRN_CHUNK = 64
HGRN_TILE = 128
HGRN_BLOCK = 512
ATTN_HEAD = 64
LANES = 128
BAND = 128
DILATIONS = (1, 4, 16)
ATTN_SCALE = ATTN_HEAD ** -0.5
assert ATTN_SCALE == 0.125
ATTN_BLOCK_ELEMS = BAND * 2048
ATTN_UNROLL = 4
SEG_QKV = 4
SEG_GATE_A = 7
NEG = -1e30

ADAM_LR = 0.001
ADAM_B1 = 0.9
ADAM_B2 = 0.999
ADAM_EPS = 1e-08
ADAM_WD = 0.01
ADAM_STEP = 10

MIB = 1024 * 1024


def _cp(semantics=None, vmem_mib=48):
    return pltpu.CompilerParams(dimension_semantics=semantics, vmem_limit_bytes=vmem_mib * MIB)


def _dot(a, b):
    return jnp.dot(a, b, preferred_element_type=F32)


def _dot_nt(a, b):
    return lax.dot_general(a, b, (((1,), (1,)), ((), ())), preferred_element_type=F32)


def _dot_tn(a, b):
    return lax.dot_general(a, b, (((0,), (0,)), ((), ())), preferred_element_type=F32)


def _split3(x):
    hi = x.astype(BF16)
    r1 = x - hi.astype(F32)
    mid = r1.astype(BF16)
    lo = (r1 - mid.astype(F32)).astype(BF16)
    return hi, mid, lo


def _exact_dot(t_bf16, x):
    hi, mid, lo = _split3(x)
    return _dot(t_bf16, hi) + _dot(t_bf16, mid) + _dot(t_bf16, lo)


def _exact_dot_right(x, t_bf16):
    hi, mid, lo = _split3(x)
    return _dot(hi, t_bf16) + _dot(mid, t_bf16) + _dot(lo, t_bf16)


def _sigmoid(z):
    return jax.nn.sigmoid(z)


def _silu_and_grad(z):
    s = _sigmoid(z)
    return z * s, s * (1.0 + z * (1.0 - s))


def _seg_select(j, values):
    out = values[0]
    for t, v in enumerate(values[1:], 1):
        out = jnp.where(j == t, v, out)
    return out


def _rms_fwd(x2, gain, token):
    S, D = x2.shape
    tm = min(512, S)

    def body(x_ref, g_ref, _, h_ref, r_ref):
        x = x_ref[...]
        r = lax.rsqrt(jnp.mean(x * x, axis=-1, keepdims=True) + NORM_EPS)
        h_ref[...] = ((x * r) * g_ref[...]).astype(BF16)
        r_ref[...] = r

    return pl.pallas_call(
        body, grid=(S // tm,), name="rms_fwd",
        in_specs=[pl.BlockSpec((tm, D), lambda i: (i, 0)), pl.BlockSpec((1, D), lambda i: (0, 0)),
                  pl.BlockSpec(token.shape, lambda i: (0, 0))],
        out_specs=[pl.BlockSpec((tm, D), lambda i: (i, 0)), pl.BlockSpec((tm, 1), lambda i: (i, 0))],
        out_shape=[jax.ShapeDtypeStruct((S, D), BF16), jax.ShapeDtypeStruct((S, 1), F32)],
        compiler_params=_cp(("parallel",)),
    )(x2, gain, token)


def _in_proj(h, w_all, where, segs, z_prev, token, name):
    S, D = h.shape
    SEG = w_all.shape[2] // 2
    tm = min(512, S)
    first, step, count, core = segs

    def body(*refs):
        h_ref, w_ref, o_ref = refs[1], refs[2], refs[-1]
        o_ref[...] = _dot(h_ref[...], w_ref[...])

    def seg_of(j, w):
        off = 0 if core is None else (w[0] if core == 0 else 1 - w[0])
        return (2 * w[1] + first + step * j + off) % 8

    in_specs = [pl.BlockSpec((tm, D), lambda j, i, w: (i, 0)),
                pl.BlockSpec((None, D, SEG), lambda j, i, w: (seg_of(j, w) // 2, 0, seg_of(j, w) % 2)),
                pl.BlockSpec(token.shape, lambda j, i, w: (0, 0))]
    args = [where, h, w_all, token]
    aliases = {}
    if z_prev is not None:
        in_specs.append(ANY)
        args.append(z_prev)
        aliases = {4: 0}
    grid_spec = pltpu.PrefetchScalarGridSpec(
        num_scalar_prefetch=1, grid=(count, S // tm), in_specs=in_specs,
        out_specs=pl.BlockSpec((None, tm, SEG), lambda j, i, w: (seg_of(j, w), i, 0)))
    return pl.pallas_call(
        body, grid_spec=grid_spec, name=name, out_shape=jax.ShapeDtypeStruct((8, S, SEG), F32),
        input_output_aliases=aliases, compiler_params=_cp(("parallel", "parallel")),
    )(*args)


def _out_proj_loss(yh, ya, w_out, x2, tgt, fgain):
    S, D = x2.shape
    SEG = yh.shape[1]
    tm = min(256, S)

    def body(yh_ref, ya_ref, w_ref, x_ref, t_ref, fg_ref, dout_ref, doutb_ref, loss_ref, dfg_ref):
        i = pl.program_id(0)

        @pl.when(i == 0)
        def _():
            loss_ref[...] = jnp.zeros_like(loss_ref)
            dfg_ref[...] = jnp.zeros_like(dfg_ref)

        out = x_ref[...] + _dot(yh_ref[...], w_ref[pl.ds(0, SEG), :]) + _dot(ya_ref[...], w_ref[pl.ds(SEG, SEG), :])
        r = lax.rsqrt(jnp.mean(out * out, axis=-1, keepdims=True) + NORM_EPS)
        n = out * r
        fg = fg_ref[...]
        err = n * fg - t_ref[...]
        loss_ref[...] += 0.5 * jnp.sum(jnp.mean(err * err, axis=-1, keepdims=True), axis=0, keepdims=True)
        dy = err * (1.0 / D)
        dfg_ref[...] += jnp.sum(dy * n, axis=0, keepdims=True)
        dn = dy * fg
        dout = r * (dn - n * jnp.mean(dn * n, axis=-1, keepdims=True))
        dout_ref[...] = dout
        doutb_ref[...] = dout.astype(BF16)

    row = lambda i: (i, 0)
    fix = lambda i: (0, 0)
    return pl.pallas_call(
        body, grid=(S // tm,), name="out_proj_loss",
        in_specs=[pl.BlockSpec((tm, SEG), row), pl.BlockSpec((tm, SEG), row), pl.BlockSpec((2 * SEG, D), fix),
                  pl.BlockSpec((tm, D), row), pl.BlockSpec((tm, D), row), pl.BlockSpec((1, D), fix)],
        out_specs=[pl.BlockSpec((tm, D), row), pl.BlockSpec((tm, D), row), pl.BlockSpec((1, 1), fix),
                   pl.BlockSpec((1, D), fix)],
        out_shape=[jax.ShapeDtypeStruct((S, D), F32), jax.ShapeDtypeStruct((S, D), BF16),
                   jax.ShapeDtypeStruct((1, 1), F32), jax.ShapeDtypeStruct((1, D), F32)],
        compiler_params=_cp(("arbitrary",)),
    )(yh, ya, w_out, x2, tgt, fgain)


def _dy_proj(doutb, w_out):
    S, D = doutb.shape
    K = w_out.shape[0]
    tm = min(512, S)

    def body(d_ref, w_ref, o_ref):
        o_ref[...] = _dot_nt(d_ref[...], w_ref[...])

    return pl.pallas_call(
        body, grid=(S // tm,), name="dy_proj",
        in_specs=[pl.BlockSpec((tm, D), lambda i: (i, 0)), pl.BlockSpec((K, D), lambda i: (0, 0))],
        out_specs=pl.BlockSpec((tm, K), lambda i: (i, 0)),
        out_shape=jax.ShapeDtypeStruct((S, K), F32),
        compiler_params=_cp(("parallel",)),
    )(doutb, w_out)


def _grad_w_out(yh, ya, doutb):
    S, SEG = yh.shape
    D = doutb.shape[1]
    R = (2 * SEG) // 4
    nb_half = SEG // R
    tk = min(512, S)

    def body(yh_ref, ya_ref, d_ref, o_ref):
        q = pl.program_id(0)
        k = pl.program_id(1)

        @pl.when(k == 0)
        def _():
            o_ref[...] = jnp.zeros_like(o_ref)

        @pl.when(q < nb_half)
        def _():
            o_ref[...] += _dot_tn(yh_ref[...], d_ref[...])

        @pl.when(q >= nb_half)
        def _():
            o_ref[...] += _dot_tn(ya_ref[...], d_ref[...])

    return pl.pallas_call(
        body, grid=(4, S // tk), name="grad_w_out",
        in_specs=[pl.BlockSpec((tk, R), lambda q, k: (k, jnp.minimum(q, nb_half - 1))),
                  pl.BlockSpec((tk, R), lambda q, k: (k, jnp.maximum(q - nb_half, 0))),
                  pl.BlockSpec((tk, D), lambda q, k: (k, 0))],
        out_specs=pl.BlockSpec((None, R, D), lambda q, k: (q, 0, 0)),
        out_shape=jax.ShapeDtypeStruct((4, R, D), F32),
        compiler_params=_cp(("parallel", "arbitrary")),
    )(yh, ya, doutb)


def _dz_sources(sources):
    counts = [s.shape[0] for s in sources]
    starts = [sum(counts[:k]) for k in range(len(counts))]
    assert sum(counts) == 8
    return counts, starts


def _dh_proj(sources, w_all, token, part, name):
    S = sources[0].shape[1]
    D = w_all.shape[1]
    SEG = w_all.shape[2] // 2
    counts, starts = _dz_sources(sources)
    ns = len(sources)
    tm = min(512, S // 2)
    nt = (S // 2) // tm
    t0 = part * nt

    def body(*refs):
        src = refs[:ns]
        w_ref, _, o_ref = refs[ns:]
        j = pl.program_id(1)

        @pl.when(j == 0)
        def _():
            o_ref[...] = jnp.zeros_like(o_ref)

        for k in range(ns):
            @pl.when((j >= starts[k]) & (j < starts[k] + counts[k]))
            def _(k=k):
                o_ref[...] += _dot_nt(src[k][...], w_ref[...])

    def src_spec(k):
        return pl.BlockSpec((None, tm, SEG),
                            lambda i, j: (jnp.clip(j - starts[k], 0, counts[k] - 1), t0 + i, 0))

    return pl.pallas_call(
        body, grid=(nt, 8), name=name,
        in_specs=[src_spec(k) for k in range(ns)] + [pl.BlockSpec((None, D, SEG), lambda i, j: (j // 2, 0, j % 2)),
                                                     pl.BlockSpec(token.shape, lambda i, j: (0, 0))],
        out_specs=pl.BlockSpec((tm, D), lambda i, j: (i, 0)),
        out_shape=jax.ShapeDtypeStruct((S // 2, D), F32),
        compiler_params=_cp(("parallel", "arbitrary")),
    )(*sources, w_all, token)


def _rms_bwd(dh, x2, rinv, gain, dout, part, gx_prev, name):
    S, D = x2.shape
    tm = min(256, S // 2)
    nt = (S // 2) // tm
    t0 = part * nt

    def body(dh_ref, x_ref, r_ref, g_ref, dout_ref, *rest):
        gx_ref, dg_ref = rest[-2:]

        @pl.when(pl.program_id(0) == 0)
        def _():
            dg_ref[...] = jnp.zeros_like(dg_ref)

        dh = dh_ref[...]
        r = r_ref[...]
        xhat = x_ref[...] * r
        dg_ref[...] += jnp.sum(dh * xhat, axis=0, keepdims=True)
        dxn = dh * g_ref[...]
        gx_ref[...] = dout_ref[...] + r * (dxn - xhat * jnp.mean(dxn * xhat, axis=-1, keepdims=True))

    row = lambda i: (t0 + i, 0)
    fix = lambda i: (0, 0)
    in_specs = [pl.BlockSpec((tm, D), lambda i: (i, 0)), pl.BlockSpec((tm, D), row), pl.BlockSpec((tm, 1), row),
                pl.BlockSpec((1, D), fix), pl.BlockSpec((tm, D), row)]
    args = [dh, x2, rinv, gain, dout]
    aliases = {}
    if gx_prev is not None:
        in_specs.append(ANY)
        args.append(gx_prev)
        aliases = {5: 0}
    return pl.pallas_call(
        body, grid=(nt,), name=name, in_specs=in_specs,
        out_specs=[pl.BlockSpec((tm, D), row), pl.BlockSpec((1, D), fix)],
        out_shape=[jax.ShapeDtypeStruct((S, D), F32), jax.ShapeDtypeStruct((1, D), F32)],
        input_output_aliases=aliases, compiler_params=_cp(("arbitrary",)),
    )(*args)


def _grad_w_in(h, sources):
    S, D = h.shape
    SEG = sources[0].shape[2]
    counts, starts = _dz_sources(sources)
    ns = len(sources)
    tk = min(512, S)

    def body(*refs):
        h_ref = refs[0]
        src = refs[1:1 + ns]
        o_ref = refs[1 + ns]
        j = pl.program_id(0)
        k = pl.program_id(1)

        @pl.when(k == 0)
        def _():
            o_ref[...] = jnp.zeros_like(o_ref)

        for s in range(ns):
            @pl.when((j >= starts[s]) & (j < starts[s] + counts[s]))
            def _(s=s):
                o_ref[...] += _dot_tn(h_ref[...], src[s][...])

    def src_spec(s):
        return pl.BlockSpec((None, tk, SEG),
                            lambda j, k: (jnp.clip(j - starts[s], 0, counts[s] - 1), k, 0))

    return pl.pallas_call(
        body, grid=(8, S // tk), name="grad_w_in",
        in_specs=[pl.BlockSpec((tk, D), lambda j, k: (k, 0))] + [src_spec(s) for s in range(ns)],
        out_specs=pl.BlockSpec((None, D, SEG), lambda j, k: (j // 2, 0, j % 2)),
        out_shape=jax.ShapeDtypeStruct((4, D, 2 * SEG), F32),
        compiler_params=_cp(("parallel", "arbitrary")),
    )(h, *sources)


def _lower_bound(lbl):
    l0 = lbl[0:1, :]
    l1 = lbl[1:2, :]
    m = jnp.maximum(l0, l1)
    e0 = jnp.exp(l0 - m)
    e1 = jnp.exp(l1 - m)
    return e0 / (e0 + e1)


def _tile_masks():
    row = lax.broadcasted_iota(jnp.int32, (HGRN_TILE, HGRN_TILE), 0)
    col = lax.broadcasted_iota(jnp.int32, (HGRN_TILE, HGRN_TILE), 1)
    same = (row // HGRN_CHUNK) == (col // HGRN_CHUNK)
    return same & (row >= col), same & (row <= col)


def _chunk_last(b):
    T = b.shape[0]
    b3 = b.reshape(T // HGRN_CHUNK, HGRN_CHUNK, HGRN_HEAD)
    return jnp.broadcast_to(b3[:, HGRN_CHUNK - 1:HGRN_CHUNK, :], b3.shape).reshape(T, HGRN_HEAD)


def _chunk_sum(x):
    T = x.shape[0]
    x3 = x.reshape(T // HGRN_CHUNK, HGRN_CHUNK, HGRN_HEAD)
    return jnp.broadcast_to(jnp.sum(x3, axis=1, keepdims=True), x3.shape).reshape(T, HGRN_HEAD)


def _hgrn_dims(S, SEG):
    T = min(HGRN_BLOCK, S)
    assert S % T == 0 and T % HGRN_TILE == 0
    tiles = [slice(t * HGRN_TILE, (t + 1) * HGRN_TILE) for t in range(T // HGRN_TILE)]
    chunks = [slice(c * HGRN_CHUNK, (c + 1) * HGRN_CHUNK) for c in range(T // HGRN_CHUNK)]
    return SEG // HGRN_HEAD, T, T // HGRN_CHUNK, S // T, tiles, chunks


def _hgrn_fwd(zf32, lb_logits, gnorm):
    _, S, SEG = zf32.shape
    H, T, NC, NJ, tiles, chunks = _hgrn_dims(S, SEG)

    def body(zq_ref, zf_ref, zi_ref, zg_ref, lbl_ref, gn_ref, y_ref, st_ref, state):
        @pl.when(pl.program_id(1) == 0)
        def _():
            state[...] = jnp.zeros_like(state)

        lb = _lower_bound(lbl_ref[...])
        tril, _ = _tile_masks()
        tril_bf = tril.astype(BF16)
        zq = zq_ref[...]
        q = zq * _sigmoid(zq)
        f = lb + (1.0 - lb) * _sigmoid(zf_ref[...])
        k = 1.0 - f
        logf = jnp.log(f)
        b = jnp.concatenate([_exact_dot(tril_bf, logf[t]) for t in tiles], axis=0)
        bl = _chunk_last(b)
        qd_b = (q * jnp.exp(b)).astype(BF16)
        kd_b = (k * jnp.exp(-b)).astype(BF16)
        ke_b = (k * jnp.exp(bl - b)).astype(BF16)
        v_b = zi_ref[...].astype(BF16)
        o_intra = jnp.concatenate(
            [_dot(jnp.where(tril, _dot_nt(qd_b[t], kd_b[t]), 0.0).astype(BF16), v_b[t]) for t in tiles], axis=0)
        kvs = [_dot_tn(v_b[r], ke_b[r]) for r in chunks]
        ebl = jnp.exp(bl)
        st = state[...]
        sts = []
        for c in range(NC):
            st_ref[c] = st
            sts.append(st.astype(BF16))
            st = st * ebl[c * HGRN_CHUNK:c * HGRN_CHUNK + 1, :] + kvs[c]
        state[...] = st
        o = o_intra + jnp.concatenate([_dot_nt(qd_b[r], sb) for r, sb in zip(chunks, sts)], axis=0)
        on = o * lax.rsqrt(jnp.mean(o * o, axis=-1, keepdims=True) + NORM_EPS) * gn_ref[...]
        zg = zg_ref[...]
        y_ref[...] = (on * (zg * _sigmoid(zg))).astype(BF16)

    def zspec(seg):
        return pl.BlockSpec((None, T, HGRN_HEAD), lambda h, j: (seg, j, h))

    return pl.pallas_call(
        body, grid=(H, NJ), name="hgrn_fwd",
        in_specs=[zspec(0), zspec(1), zspec(2), zspec(3),
                  pl.BlockSpec((2, HGRN_HEAD), lambda h, j: (0, h)),
                  pl.BlockSpec((1, HGRN_HEAD), lambda h, j: (0, 0))],
        out_specs=[pl.BlockSpec((T, HGRN_HEAD), lambda h, j: (j, h)),
                   pl.BlockSpec((NC, None, HGRN_HEAD, HGRN_HEAD), lambda h, j: (j, h, 0, 0))],
        out_shape=[jax.ShapeDtypeStruct((S, SEG), BF16),
                   jax.ShapeDtypeStruct((S // HGRN_CHUNK, H, HGRN_HEAD, HGRN_HEAD), F32)],
        scratch_shapes=[pltpu.VMEM((HGRN_HEAD, HGRN_HEAD), F32)],
        compiler_params=_cp(("parallel", "arbitrary")),
    )(zf32, zf32, zf32, zf32, lb_logits, gnorm)


def _hgrn_bwd(zf32, lb_logits, gnorm, states, dy):
    _, S, SEG = zf32.shape
    H, T, NC, NJ, tiles, chunks = _hgrn_dims(S, SEG)
    C = HGRN_CHUNK

    def body(zq_ref, zf_ref, zi_ref, zg_ref, lbl_ref, gn_ref, st_ref, dy_ref, dz_ref, dl_ref, dgn_ref, gstate):
        @pl.when(pl.program_id(1) == 0)
        def _():
            gstate[...] = jnp.zeros_like(gstate)
            dl_ref[...] = jnp.zeros_like(dl_ref)
            dgn_ref[...] = jnp.zeros_like(dgn_ref)

        lb = _lower_bound(lbl_ref[...])
        gn = gn_ref[...]
        tril, triu = _tile_masks()
        tril_bf = tril.astype(BF16)
        triu_bf = triu.astype(BF16)
        q, dq_dz = _silu_and_grad(zq_ref[...])
        sf = _sigmoid(zf_ref[...])
        f = lb + (1.0 - lb) * sf
        k = 1.0 - f
        logf = jnp.log(f)
        b = jnp.concatenate([_exact_dot(tril_bf, logf[t]) for t in tiles], axis=0)
        bl = _chunk_last(b)
        eb = jnp.exp(b)
        enb = jnp.exp(-b)
        ekl = jnp.exp(bl - b)
        ebl = jnp.exp(bl)
        qd = q * eb
        kd = k * enb
        ke = k * ekl
        qd_b = qd.astype(BF16)
        kd_b = kd.astype(BF16)
        ke_b = ke.astype(BF16)
        v_b = zi_ref[...].astype(BF16)
        sts = [st_ref[c] for c in range(NC)]
        sts_b = [s.astype(BF16) for s in sts]
        a_b = [jnp.where(tril, _dot_nt(qd_b[t], kd_b[t]), 0.0).astype(BF16) for t in tiles]
        o = (jnp.concatenate([_dot(a, v_b[t]) for a, t in zip(a_b, tiles)], axis=0)
             + jnp.concatenate([_dot_nt(qd_b[r], sb) for r, sb in zip(chunks, sts_b)], axis=0))
        rinv = lax.rsqrt(jnp.mean(o * o, axis=-1, keepdims=True) + NORM_EPS)
        ohat = o * rinv
        sg, dsg = _silu_and_grad(zg_ref[...])
        dyv = dy_ref[...]
        don = dyv * sg
        dz_ref[3] = (dyv * (ohat * gn) * dsg).astype(BF16)
        dgn_ref[...] += jnp.sum(don * ohat, axis=0, keepdims=True)
        dohat = don * gn
        do = rinv * (dohat - ohat * jnp.mean(dohat * ohat, axis=-1, keepdims=True))
        do_b = do.astype(BF16)
        da_b = [jnp.where(tril, _dot_nt(do_b[t], v_b[t]), 0.0).astype(BF16) for t in tiles]
        dv_intra = jnp.concatenate([_dot_tn(a, do_b[t]) for a, t in zip(a_b, tiles)], axis=0)
        dqd_intra = jnp.concatenate([_dot(da, kd_b[t]) for da, t in zip(da_b, tiles)], axis=0)
        dkd = jnp.concatenate([_dot_tn(da, qd_b[t]) for da, t in zip(da_b, tiles)], axis=0)
        dqd_inter = jnp.concatenate([_dot(do_b[r], sb) for r, sb in zip(chunks, sts_b)], axis=0)
        gks = [_dot_tn(do_b[r], qd_b[r]) for r in chunks]
        g = gstate[...]
        gs = [None] * NC
        for c in reversed(range(NC)):
            gs[c] = g
            g = g * ebl[c * C:c * C + 1, :] + gks[c]
        gstate[...] = g
        gs_b = [x.astype(BF16) for x in gs]
        dv = dv_intra + jnp.concatenate([_dot_nt(ke_b[r], gb) for r, gb in zip(chunks, gs_b)], axis=0)
        dz_ref[2] = dv.astype(BF16)
        dke = jnp.concatenate([_dot(v_b[r], gb) for r, gb in zip(chunks, gs_b)], axis=0)
        debl = jnp.concatenate(
            [jnp.broadcast_to(jnp.sum(x * s, axis=0, keepdims=True), (C, HGRN_HEAD)) for x, s in zip(gs, sts)], axis=0)
        dqd = dqd_intra + dqd_inter
        dz_ref[0] = ((dqd * eb) * dq_dz).astype(BF16)
        t_ke = dke * ke
        db = dqd * qd - dkd * kd - t_ke
        db_last = _chunk_sum(t_ke) + debl * ebl
        dk = dkd * enb + dke * ekl
        dlogf = jnp.concatenate([_exact_dot(triu_bf, db[t]) for t in tiles], axis=0) + db_last
        df = dlogf / f - dk
        dz_ref[1] = (df * (1.0 - lb) * (sf * (1.0 - sf))).astype(BF16)
        dlb = jnp.sum(df * (1.0 - sf), axis=0, keepdims=True)
        dl0 = dlb * lb * (1.0 - lb)
        dl_ref[0:1, :] += dl0
        dl_ref[1:2, :] -= dl0

    def zspec(seg):
        return pl.BlockSpec((None, T, HGRN_HEAD), lambda h, j: (seg, NJ - 1 - j, h))

    return pl.pallas_call(
        body, grid=(H, NJ), name="hgrn_bwd",
        in_specs=[zspec(0), zspec(1), zspec(2), zspec(3),
                  pl.BlockSpec((2, HGRN_HEAD), lambda h, j: (0, h)),
                  pl.BlockSpec((1, HGRN_HEAD), lambda h, j: (0, 0)),
                  pl.BlockSpec((NC, None, HGRN_HEAD, HGRN_HEAD), lambda h, j: (NJ - 1 - j, h, 0, 0)),
                  pl.BlockSpec((T, HGRN_HEAD), lambda h, j: (NJ - 1 - j, h))],
        out_specs=[pl.BlockSpec((4, T, HGRN_HEAD), lambda h, j: (0, NJ - 1 - j, h)),
                   pl.BlockSpec((2, HGRN_HEAD), lambda h, j: (0, h)),
                   pl.BlockSpec((None, 1, HGRN_HEAD), lambda h, j: (h, 0, 0))],
        out_shape=[jax.ShapeDtypeStruct((4, S, SEG), BF16), jax.ShapeDtypeStruct((2, SEG), F32),
                   jax.ShapeDtypeStruct((H, 1, HGRN_HEAD), F32)],
        scratch_shapes=[pltpu.VMEM((HGRN_HEAD, HGRN_HEAD), F32)],
        compiler_params=_cp(("parallel", "arbitrary")),
    )(zf32, zf32, zf32, zf32, lb_logits, gnorm, states, dy)


def _alibi_slopes(seg):
    n_heads = seg // ATTN_HEAD
    s = 2.0 ** (-8.0 * np.arange(1, n_heads + 1, dtype=np.float64) / n_heads)
    return jnp.asarray(np.repeat(s, ATTN_HEAD)[None, :], F32)


def _attn_dims(S, SEG, d):
    rb = BAND * d
    assert S % rb == 0 and SEG % LANES == 0
    cb = min(SEG, ATTN_BLOCK_ELEMS // rb) if d == 1 else LANES
    assert SEG % cb == 0
    return rb, cb, S // rb, SEG // cb


def _res_rows(r, d):
    return pl.ds(0, BAND) if d == 1 else pl.ds(r, BAND, stride=d)


def _for_residues(d, fn):
    if d == 1:
        fn(0)
    else:
        def step(r, carry):
            fn(r)
            return carry
        lax.fori_loop(0, d, step, 0, unroll=ATTN_UNROLL)


def _band_terms(n, d):
    i = lax.broadcasted_iota(jnp.int32, (BAND, 2 * BAND), 0)
    jj = lax.broadcasted_iota(jnp.int32, (BAND, 2 * BAND), 1)
    delta = BAND + i - jj
    valid = (delta >= 0) & (delta <= BAND) & ((n > 0) | (jj >= BAND))
    return (-d * delta).astype(F32), valid


def _head_biases(slopes, nd, valid):
    out = []
    for s in _per_head(slopes):
        s2 = jnp.concatenate([s, s], axis=1)
        out.append(jnp.where(valid, s2 * nd, NEG))
    return jnp.concatenate(out, axis=0)


def _stack_heads(x):
    lane = lax.broadcasted_iota(jnp.int32, x.shape, 1)
    zero = jnp.zeros_like(x)
    return jnp.concatenate([jnp.where(lane < ATTN_HEAD, x, zero), jnp.where(lane < ATTN_HEAD, zero, x)], axis=0)


def _unstack_heads(x2):
    first = lax.broadcasted_iota(jnp.int32, (BAND, LANES), 1) < ATTN_HEAD
    return jnp.where(first, x2[:BAND], x2[BAND:])


def _stack_per_head(x):
    a, b = _per_head(x)
    col = jnp.concatenate([a, b], axis=0)
    return jnp.concatenate([col, col], axis=1)


def _per_head(x):
    lane = lax.broadcasted_iota(jnp.int32, x.shape, 1)
    sw = pltpu.roll(x, ATTN_HEAD, 1)
    first = lane < ATTN_HEAD
    return jnp.where(first, x, sw), jnp.where(first, sw, x)


def _attn_fwd(qkv, slopes, d):
    _, S, SEG = qkv.shape
    rb, cb, nb, ncb = _attn_dims(S, SEG, d)
    NP = cb // LANES

    def body(q_ref, kp_ref, kc_ref, vp_ref, vc_ref, sl_ref, o_ref, l_ref):
        n = pl.program_id(1)
        nd, valid = _band_terms(n, d)
        biases = [_head_biases(sl_ref[:, p * LANES:(p + 1) * LANES], nd, valid) for p in range(NP)]

        def residue(r):
            rows = _res_rows(r, d)
            for p in range(NP):
                cols = slice(p * LANES, (p + 1) * LANES)
                kc = jnp.concatenate([kp_ref[rows, cols], kc_ref[rows, cols]], axis=0).astype(BF16)
                vc = jnp.concatenate([vp_ref[rows, cols], vc_ref[rows, cols]], axis=0).astype(BF16)
                s = _dot_nt(_stack_heads((q_ref[rows, cols] * ATTN_SCALE).astype(BF16)), kc) + biases[p]
                m = jnp.max(s, axis=-1, keepdims=True)
                e = jnp.exp(s - m)
                den = jnp.sum(e, axis=-1, keepdims=True)
                o_ref[rows, cols] = _unstack_heads(_dot(e.astype(BF16), vc) / den)
                l_ref[rows, cols] = _unstack_heads(jnp.broadcast_to(m + jnp.log(den), (2 * BAND, LANES)))

        _for_residues(d, residue)

    def spec(seg, prev):
        if prev:
            return pl.BlockSpec((None, rb, cb), lambda c, n: (SEG_QKV + seg, jnp.maximum(n - 1, 0), c))
        return pl.BlockSpec((None, rb, cb), lambda c, n: (SEG_QKV + seg, n, c))

    out = pl.BlockSpec((rb, cb), lambda c, n: (n, c))
    return pl.pallas_call(
        body, grid=(ncb, nb), name=f"attn_fwd_d{d}",
        in_specs=[spec(0, False), spec(1, True), spec(1, False), spec(2, True), spec(2, False),
                  pl.BlockSpec((1, cb), lambda c, n: (0, c))],
        out_specs=[out, out],
        out_shape=[jax.ShapeDtypeStruct((S, SEG), F32), jax.ShapeDtypeStruct((S, SEG), F32)],
        compiler_params=_cp(("parallel", "parallel")),
    )(qkv, qkv, qkv, qkv, qkv, slopes)


def _attn_merge(outs, lses, zf32):
    S, SEG = outs[0].shape
    tm = min(256, S)

    def body(o1, o2, o3, l1, l2, l3, zg_ref, o_ref, lse_ref, y_ref):
        a, b, c = l1[...], l2[...], l3[...]
        m = jnp.maximum(jnp.maximum(a, b), c)
        ea, eb, ec = jnp.exp(a - m), jnp.exp(b - m), jnp.exp(c - m)
        tot = ea + eb + ec
        o = (ea / tot) * o1[...] + (eb / tot) * o2[...] + (ec / tot) * o3[...]
        o_ref[...] = o
        lse_ref[...] = m + jnp.log(tot)
        zg = zg_ref[...]
        y_ref[...] = (o * (zg * _sigmoid(zg))).astype(BF16)

    row = pl.BlockSpec((tm, SEG), lambda i: (i, 0))
    return pl.pallas_call(
        body, grid=(S // tm,), name="attn_merge",
        in_specs=[row] * 6 + [pl.BlockSpec((None, tm, SEG), lambda i: (SEG_GATE_A, i, 0))],
        out_specs=[row, row, row],
        out_shape=[jax.ShapeDtypeStruct((S, SEG), F32), jax.ShapeDtypeStruct((S, SEG), F32),
                   jax.ShapeDtypeStruct((S, SEG), BF16)],
        compiler_params=_cp(("parallel",)),
    )(*outs, *lses, zf32)


def _attn_gate_bwd(dy, o, zf32):
    S, SEG = o.shape
    tm = min(256, S)
    NP = SEG // LANES

    def body(dy_ref, o_ref, zg_ref, do_ref, dl_ref, dzg_ref):
        r = lax.broadcasted_iota(jnp.int32, (LANES, LANES), 0) // ATTN_HEAD
        c = lax.broadcasted_iota(jnp.int32, (LANES, LANES), 1) // ATTN_HEAD
        same_head = (r == c).astype(BF16)
        for p in range(NP):
            cols = slice(p * LANES, (p + 1) * LANES)
            sg, dsg = _silu_and_grad(zg_ref[:, cols])
            dyv = dy_ref[:, cols]
            ov = o_ref[:, cols]
            do = dyv * sg
            do_ref[:, cols] = do
            dzg_ref[:, cols] = (dyv * ov * dsg).astype(BF16)
            dl_ref[:, cols] = _exact_dot_right(do * ov, same_head)

    return pl.pallas_call(
        body, grid=(S // tm,), name="attn_gate_bwd",
        in_specs=[pl.BlockSpec((tm, SEG), lambda i: (i, 1)), pl.BlockSpec((tm, SEG), lambda i: (i, 0)),
                  pl.BlockSpec((None, tm, SEG), lambda i: (SEG_GATE_A, i, 0))],
        out_specs=[pl.BlockSpec((tm, SEG), lambda i: (i, 0)), pl.BlockSpec((tm, SEG), lambda i: (i, 0)),
                   pl.BlockSpec((None, tm, SEG), lambda i: (0, i, 0))],
        out_shape=[jax.ShapeDtypeStruct((S, SEG), F32), jax.ShapeDtypeStruct((S, SEG), F32),
                   jax.ShapeDtypeStruct((1, S, SEG), BF16)],
        compiler_params=_cp(("parallel",)),
    )(dy, o, zf32)


def _attn_bwd(qkv, slopes, do, lse, dl, d, acc, out_dtype):
    _, S, SEG = qkv.shape
    rb, cb, nb, ncb = _attn_dims(S, SEG, d)
    NP = cb // LANES
    has_acc = acc is not None

    def body(*refs):
        q_ref, kp_ref, kc_ref, vp_ref, vc_ref, sl_ref, do_ref, lse_ref, dl_ref = refs[:9]
        refs = refs[9:]
        if has_acc:
            aq_ref, ak_ref, av_ref = refs[:3]
            refs = refs[3:]
        dq_ref, dk_ref, dv_ref, ck, cv = refs
        n = pl.program_id(1)

        @pl.when(n == 0)
        def _():
            ck[...] = jnp.zeros_like(ck)
            cv[...] = jnp.zeros_like(cv)

        @pl.when(n < nb)
        def _():
            nd, valid = _band_terms(n, d)
            biases = [_head_biases(sl_ref[:, p * LANES:(p + 1) * LANES], nd, valid) for p in range(NP)]

            def residue(r):
                rows = _res_rows(r, d)
                for p in range(NP):
                    cols = slice(p * LANES, (p + 1) * LANES)
                    kc = jnp.concatenate([kp_ref[rows, cols], kc_ref[rows, cols]], axis=0).astype(BF16)
                    vc = jnp.concatenate([vp_ref[rows, cols], vc_ref[rows, cols]], axis=0).astype(BF16)
                    qs = _stack_heads((q_ref[rows, cols] * ATTN_SCALE).astype(BF16))
                    dos = _stack_heads(do_ref[rows, cols].astype(BF16))
                    pr = jnp.exp(_dot_nt(qs, kc) + biases[p] - _stack_per_head(lse_ref[rows, cols]))
                    ds = (pr * (_dot_nt(dos, vc) - _stack_per_head(dl_ref[rows, cols]))).astype(BF16)
                    dq = _unstack_heads(_dot(ds, kc)) * ATTN_SCALE
                    dk = _dot_tn(ds, qs)
                    dv = _dot_tn(pr.astype(BF16), dos)
                    dk_prev = ck[r, :, cols] + dk[:BAND, :]
                    dv_prev = cv[r, :, cols] + dv[:BAND, :]
                    if has_acc:
                        dq = dq + aq_ref[rows, cols]
                        dk_prev = dk_prev + ak_ref[rows, cols]
                        dv_prev = dv_prev + av_ref[rows, cols]
                    dq_ref[rows, cols] = dq.astype(out_dtype)
                    dk_ref[rows, cols] = dk_prev.astype(out_dtype)
                    dv_ref[rows, cols] = dv_prev.astype(out_dtype)
                    ck[r, :, cols] = dk[BAND:, :]
                    cv[r, :, cols] = dv[BAND:, :]

            _for_residues(d, residue)

        @pl.when(n == nb)
        def _():
            def residue(r):
                rows = _res_rows(r, d)
                dk_last = ck[r]
                dv_last = cv[r]
                if has_acc:
                    dk_last = dk_last + ak_ref[rows, :]
                    dv_last = dv_last + av_ref[rows, :]
                dk_ref[rows, :] = dk_last.astype(out_dtype)
                dv_ref[rows, :] = dv_last.astype(out_dtype)

            _for_residues(d, residue)

    cur2 = lambda c, n: (jnp.minimum(n, nb - 1), c)
    cur3 = lambda c, n: (0, jnp.minimum(n, nb - 1), c)
    lag3 = lambda c, n: (0, jnp.clip(n - 1, 0, nb - 1), c)

    def spec(seg, prev):
        if prev:
            return pl.BlockSpec((None, rb, cb), lambda c, n: (SEG_QKV + seg, jnp.clip(n - 1, 0, nb - 1), c))
        return pl.BlockSpec((None, rb, cb), lambda c, n: (SEG_QKV + seg, jnp.minimum(n, nb - 1), c))

    in_specs = [spec(0, False), spec(1, True), spec(1, False), spec(2, True), spec(2, False),
                pl.BlockSpec((1, cb), lambda c, n: (0, c)),
                pl.BlockSpec((rb, cb), cur2), pl.BlockSpec((rb, cb), cur2), pl.BlockSpec((rb, cb), cur2)]
    args = [qkv, qkv, qkv, qkv, qkv, slopes, do, lse, dl]
    aliases = {}
    if has_acc:
        in_specs += [pl.BlockSpec((None, rb, cb), cur3), pl.BlockSpec((None, rb, cb), lag3),
                     pl.BlockSpec((None, rb, cb), lag3)]
        args += list(acc)
        if out_dtype == F32:
            aliases = {9: 0, 10: 1, 11: 2}
    return pl.pallas_call(
        body, grid=(ncb, nb + 1), name=f"attn_bwd_d{d}",
        in_specs=in_specs,
        out_specs=[pl.BlockSpec((None, rb, cb), cur3), pl.BlockSpec((None, rb, cb), lag3),
                   pl.BlockSpec((None, rb, cb), lag3)],
        out_shape=[jax.ShapeDtypeStruct((1, S, SEG), out_dtype)] * 3,
        scratch_shapes=[pltpu.VMEM((d, BAND, cb), F32), pltpu.VMEM((d, BAND, cb), F32)],
        input_output_aliases=aliases,
        compiler_params=_cp(("parallel", "arbitrary")),
    )(*args)


def _adamw(w, g, m, v, name):
    R, C = w.shape
    tr = R if R <= 256 else 256
    assert R % tr == 0

    def body(w_ref, g_ref, m_ref, v_ref, d_ref, nm_ref, nv_ref):
        g = g_ref[...]
        nm = ADAM_B1 * m_ref[...] + (1.0 - ADAM_B1) * g
        nv = ADAM_B2 * v_ref[...] + (1.0 - ADAM_B2) * (g * g)
        m_hat = nm / (1.0 - ADAM_B1 ** ADAM_STEP)
        v_hat = nv / (1.0 - ADAM_B2 ** ADAM_STEP)
        d_ref[...] = -ADAM_LR * (m_hat / (jnp.sqrt(v_hat) + ADAM_EPS) + ADAM_WD * w_ref[...])
        nm_ref[...] = nm
        nv_ref[...] = nv

    blk = pl.BlockSpec((tr, C), lambda i: (i, 0))
    sds = jax.ShapeDtypeStruct((R, C), F32)
    return pl.pallas_call(
        body, grid=(R // tr,), name=name, in_specs=[blk] * 4, out_specs=[blk] * 3, out_shape=[sds] * 3,
        compiler_params=_cp(("parallel",)),
    )(w, g, m, v)


def _coords():
    return lax.axis_index("x"), lax.axis_index("y"), lax.axis_index("c")


def _other_chips(x, y):
    return [(1 - x, y), (x, 1 - y), (1 - x, 1 - y)]


ANY = pl.BlockSpec(memory_space=pl.ANY)


def _cast_into_slot(w, where, name):
    R, C = w.shape
    tr = min(256, R)

    def body(where_ref, w_ref, o_ref):
        o_ref[...] = w_ref[...].astype(BF16)

    grid_spec = pltpu.PrefetchScalarGridSpec(
        num_scalar_prefetch=1, grid=(R // tr,),
        in_specs=[pl.BlockSpec((tr, C), lambda i, w: (i, 0))],
        out_specs=pl.BlockSpec((None, tr, C), lambda i, w: (w[1], i, 0)))
    return pl.pallas_call(
        body, grid_spec=grid_spec, name=name, out_shape=jax.ShapeDtypeStruct((4, R, C), BF16),
        compiler_params=_cp(("parallel",)),
    )(where, w)


def _pair_sum(g, sib, where, name):
    _, n2, C = g.shape
    N = n2 // 2
    tr = min(256, N)
    nt = N // tr

    def body(where_ref, g_ref, s_ref, qb_ref, own_ref):
        q = pl.program_id(1)
        tot = g_ref[...] + s_ref[...]
        qb_ref[...] = tot.astype(BF16)

        @pl.when(q == where_ref[1])
        def _():
            own_ref[...] = tot

    grid_spec = pltpu.PrefetchScalarGridSpec(
        num_scalar_prefetch=1, grid=(nt, 4),
        in_specs=[pl.BlockSpec((None, tr, C), lambda i, q, w: (q, w[0] * nt + i, 0)),
                  pl.BlockSpec((None, tr, C), lambda i, q, w: (q, i, 0))],
        out_specs=[pl.BlockSpec((None, tr, C), lambda i, q, w: (q, i, 0)),
                   pl.BlockSpec((tr, C), lambda i, q, w: (i, 0))])
    return pl.pallas_call(
        body, grid_spec=grid_spec, name=name,
        out_shape=[jax.ShapeDtypeStruct((4, N, C), BF16), jax.ShapeDtypeStruct((N, C), F32)],
        compiler_params=_cp(("parallel", "arbitrary")),
    )(where, g, sib)


HBM = pl.BlockSpec(memory_space=pltpu.HBM)
SEM = pl.BlockSpec(memory_space=pltpu.SEMAPHORE)


def _in_hbm(a):
    return pltpu.with_memory_space_constraint(a, pltpu.HBM)


def _split_start(name, copies, arrays, n_sems, after=None):
    n = len(arrays)

    def body(*refs):
        for cp in copies(refs[:n], refs[-n - 3], refs[-n - 2]):
            cp.start()
        refs[-1][...] = jnp.zeros_like(refs[-1])

    ordered = () if after is None else (after,)
    outs = pl.pallas_call(
        body, name=name,
        out_shape=(pltpu.SemaphoreType.DMA((n_sems,)), pltpu.SemaphoreType.DMA((n_sems,)),
                   *[pltpu.HBM(a.shape, a.dtype) for a in arrays], jax.ShapeDtypeStruct((8, LANES), F32)),
        in_specs=(HBM,) * n + (ANY,) * len(ordered),
        out_specs=(SEM, SEM) + (HBM,) * n + (pl.BlockSpec(memory_space=pltpu.VMEM),),
        input_output_aliases={i: 2 + i for i in range(n)},
        compiler_params=pltpu.CompilerParams(has_side_effects=pltpu.SideEffectType.DATAFLOW_SIDE_EFFECTING),
    )(*[_in_hbm(a) for a in arrays], *ordered)
    return outs[0], outs[1], list(outs[2:2 + n]), outs[-1]


def _split_wait(name, copies, send_sems, recv_sems, arrays, after):
    n = len(arrays)

    def body(*refs):
        for cp in copies(refs[:n], refs[n], refs[n + 1]):
            cp.wait_send()
            cp.wait_recv()

    outs = pl.pallas_call(
        body, name=name,
        out_shape=tuple(pltpu.HBM(a.shape, a.dtype) for a in arrays),
        in_specs=(HBM,) * n + (SEM, SEM, ANY), out_specs=(HBM,) * n,
        input_output_aliases={i: i for i in range(n)},
        compiler_params=pltpu.CompilerParams(has_side_effects=pltpu.SideEffectType.DATAFLOW_SIDE_EFFECTING),
    )(*arrays, send_sems, recv_sems, after)
    return list(outs)


def _remote(src, dst, sems, k, to):
    send_sems, recv_sems = sems
    return pltpu.make_async_remote_copy(src_ref=src, dst_ref=dst, send_sem=send_sems.at[k], recv_sem=recv_sems.at[k],
                                        device_id=to, device_id_type=MESH)


def _gather_in_copies(refs, send_sems, recv_sems):
    (w,) = refs
    x, y, c = _coords()
    seg = w.shape[2] // 2
    mine = w.at[2 * x + y, :, pl.ds(c * seg, seg)]
    return [_remote(mine, mine, (send_sems, recv_sems), k, (px, py, c)) for k, (px, py) in enumerate(_other_chips(x, y))]


def _gather_out_copies(refs, send_sems, recv_sems):
    (w,) = refs
    x, y, c = _coords()
    mine = w.at[2 * x + y]
    return [_remote(mine, mine, (send_sems, recv_sems), k, (px, py, c)) for k, (px, py) in enumerate(_other_chips(x, y))]


def _swap_copies(refs, send_sems, recv_sems):
    gi, go, si, so = refs
    x, y, c = _coords()
    cps = []
    for a, (src, dst) in enumerate(((gi, si), (go, so))):
        nr = dst.shape[1]
        cps.append(_remote(src.at[:, pl.ds((1 - c) * nr, nr), :], dst, (send_sems, recv_sems), a, (x, y, 1 - c)))
    return cps


def _scatter_copies(refs, send_sems, recv_sems):
    qi, qo, ri, ro = refs
    x, y, c = _coords()
    cps = []
    for k, (px, py) in enumerate(_other_chips(x, y)):
        for a, (src, dst) in enumerate(((qi, ri), (qo, ro))):
            cps.append(_remote(src.at[2 * px + py], dst.at[k], (send_sems, recv_sems), 2 * k + a, (px, py, c)))
    return cps


def _forward_copies(refs, send_sems, recv_sems):
    (w,) = refs
    x, y, c = _coords()
    seg = w.shape[2] // 2
    cps = []
    for k, (px, py) in enumerate(_other_chips(x, y)):
        got = w.at[2 * px + py, :, pl.ds(c * seg, seg)]
        cps.append(_remote(got, got, (send_sems, recv_sems), k, (x, y, 1 - c)))
    return cps


def _chip_sum(own, got, where, name):
    N, C = own.shape
    tr = min(256, N)
    nt = N // tr

    def body(where_ref, own_ref, got_ref, o_ref):
        t = own_ref[...]
        for k in range(3):
            t = t + got_ref[k].astype(F32)
        o_ref[...] = t

    grid_spec = pltpu.PrefetchScalarGridSpec(
        num_scalar_prefetch=1, grid=(nt,),
        in_specs=[pl.BlockSpec((tr, C), lambda i, w: (i, 0)), pl.BlockSpec((3, tr, C), lambda i, w: (0, i, 0))],
        out_specs=pl.BlockSpec((tr, C), lambda i, w: (w[0] * nt + i, 0)))
    return pl.pallas_call(
        body, grid_spec=grid_spec, name=name, out_shape=jax.ShapeDtypeStruct((2 * N, C), F32),
        compiler_params=_cp(("parallel",)),
    )(where, own, got)


def _join_halves(gi, go):
    def body(gi_in, go_in, gi_ref, go_ref, send_sems, recv_sems):
        x, y, c = _coords()
        cps = []
        for a, ref in enumerate((gi_ref, go_ref)):
            nr = ref.shape[0] // 2
            mine = ref.at[pl.ds(c * nr, nr), :]
            cp = pltpu.make_async_remote_copy(src_ref=mine, dst_ref=mine, send_sem=send_sems.at[a],
                                              recv_sem=recv_sems.at[a], device_id=(x, y, 1 - c), device_id_type=MESH)
            cp.start()
            cps.append(cp)
        for a, ref in enumerate((gi_ref, go_ref)):
            nr = ref.shape[0] // 2
            theirs = ref.at[pl.ds((1 - c) * nr, nr), :]
            pltpu.make_async_remote_copy(src_ref=theirs, dst_ref=theirs, send_sem=send_sems.at[a],
                                         recv_sem=recv_sems.at[a], device_id=(x, y, 1 - c),
                                         device_id_type=MESH).wait_recv()
        for cp in cps:
            cp.wait_send()

    return pl.pallas_call(
        body, name="join_halves", in_specs=[ANY, ANY], out_specs=[ANY, ANY],
        out_shape=[jax.ShapeDtypeStruct(gi.shape, F32), jax.ShapeDtypeStruct(go.shape, F32)],
        scratch_shapes=[pltpu.SemaphoreType.DMA((2,)), pltpu.SemaphoreType.DMA((2,))],
        input_output_aliases={0: 0, 1: 1},
    )(gi, go)


def _all_reduce_small(part):
    R, C = part.shape

    def body(p_ref, o_ref, slots, send_sems, recv_sems):
        x, y, c = _coords()
        me = 4 * x + 2 * y + c
        slots[me] = p_ref[...]
        cps = []
        for k in range(1, 8):
            fx, fy, fc = (k >> 2) & 1, (k >> 1) & 1, k & 1
            peer = (1 - x if fx else x, 1 - y if fy else y, 1 - c if fc else c)
            cp = pltpu.make_async_remote_copy(src_ref=p_ref, dst_ref=slots.at[me], send_sem=send_sems.at[k - 1],
                                              recv_sem=recv_sems.at[k - 1], device_id=peer, device_id_type=MESH)
            cp.start()
            cps.append(cp)
        for cp in cps:
            cp.wait()
        t = slots[0]
        for k in range(1, 8):
            t = t + slots[k]
        o_ref[...] = t

    vm = pl.BlockSpec(memory_space=pltpu.VMEM)
    return pl.pallas_call(
        body, name="all_reduce_small", in_specs=[vm], out_specs=vm,
        out_shape=jax.ShapeDtypeStruct((R, C), F32),
        scratch_shapes=[pltpu.VMEM((8, R, C), F32), pltpu.SemaphoreType.DMA((7,)), pltpu.SemaphoreType.DMA((7,))],
    )(part)


def _mixers_forward(z, lb_logits, hgrn_gnorm):
    slopes = _alibi_slopes(z.shape[2])
    yh, states = _hgrn_fwd(z, lb_logits, hgrn_gnorm)
    outs, lses = [], []
    for d in DILATIONS:
        o, l = _attn_fwd(z, slopes, d)
        outs.append(o)
        lses.append(l)
    o_attn, lse, ya = _attn_merge(outs, lses, z)
    return yh, ya, (states, o_attn, lse, slopes)


def _backward_to_dz(z, kept, lb_logits, hgrn_gnorm, yh, ya, w_out_all, x2, tgt, fgain, h):
    states, o_attn, lse, slopes = kept
    dout, doutb, loss, dfg = _out_proj_loss(yh, ya, w_out_all, x2, tgt, fgain)
    dy = _dy_proj(doutb, w_out_all)
    g_w_out = _grad_w_out(yh, ya, doutb)
    dzh, dlogits, dgn = _hgrn_bwd(z, lb_logits, hgrn_gnorm, states, dy)
    do, dl, dzg = _attn_gate_bwd(dy, o_attn, z)
    acc = None
    order = sorted(DILATIONS, reverse=True)
    for k, d in enumerate(order):
        acc = _attn_bwd(z, slopes, do, lse, dl, d, acc, BF16 if k == len(order) - 1 else F32)
    sources = [dzh] + list(acc) + [dzg]
    g_w_in = _grad_w_in(h, sources)
    return loss, dfg, dlogits, dgn, g_w_out, g_w_in, sources, dout


def _grad_x_half(sources, w_all, x2, rinv, norm_gain, dout, token, part, gx_prev):
    dh = _dh_proj(sources, w_all, token, part, f"dh_proj_{part}")
    return _rms_bwd(dh, x2, rinv, norm_gain, dout, part, gx_prev, f"rms_bwd_{part}")


def _local_step(x2, tgt, norm_gain, w_all, lb_logits, hgrn_gnorm, w_out_all, fgain):
    token = jnp.zeros((8, LANES), F32)
    where = jnp.zeros((2,), jnp.int32)
    h, rinv = _rms_fwd(x2, norm_gain, token)
    z = _in_proj(h, w_all, where, (0, 1, 8, None), None, token, "in_proj_all")
    yh, ya, kept = _mixers_forward(z, lb_logits, hgrn_gnorm)
    loss, dfg, dlogits, dgn, g_w_out, g_w_in, sources, dout = _backward_to_dz(
        z, kept, lb_logits, hgrn_gnorm, yh, ya, w_out_all, x2, tgt, fgain, h)
    gx, dg0 = _grad_x_half(sources, w_all, x2, rinv, norm_gain, dout, token, 0, None)
    gx, dg1 = _grad_x_half(sources, w_all, x2, rinv, norm_gain, dout, token, 1, gx)
    return loss, gx, dg0 + dg1, g_w_in, dlogits, dgn, g_w_out, dfg


def _pack_small(D, loss, dgain, dlogits, dgn, dfg):
    def row(v):
        v = v.reshape(1, -1)
        return jnp.pad(v, ((0, 0), (0, D - v.shape[1])))
    rows = [row(dgain), row(dfg), row(dlogits[0]), row(dlogits[1]), row(jnp.sum(dgn, axis=0)), row(loss)]
    rows += [jnp.zeros((1, D), F32)] * (8 - len(rows))
    return jnp.concatenate(rows, axis=0)


def kernel(x, norm_gain, w_in, lb_logits, hgrn_gnorm, w_out, final_gain, loss_target, m_norm_gain, m_w_in, m_lb_logits, m_hgrn_gnorm, m_w_out, m_final_gain, v_norm_gain, v_w_in, v_lb_logits, v_hgrn_gnorm, v_w_out, v_final_gain):
    _, S, D = x.shape
    SEG = w_in.shape[2] // 2
    x2 = x[0]
    tgt = loss_target[0]
    fgain = final_gain.reshape(1, D)
    where = jnp.stack([lax.axis_index("c"), 2 * lax.axis_index("x") + lax.axis_index("y")]).astype(jnp.int32)

    wia = _cast_into_slot(w_in[0], where, "cast_w_in")
    woa = _cast_into_slot(w_out[0], where, "cast_w_out")
    sems = _split_start("gather_in_start", _gather_in_copies, [wia], 3)
    (wia,), token = sems[2], sems[3]
    h, rinv = _rms_fwd(x2, norm_gain, token)
    z = _in_proj(h, wia, where, (0, 1, 2, None), None, token, "in_proj_own")
    (wia,) = _split_wait("gather_in_wait", _gather_in_copies, sems[0], sems[1], [wia], z)
    out_sems = _split_start("gather_out_start", _gather_out_copies, [woa], 3, after=wia)
    sems = _split_start("forward_start", _forward_copies, [wia], 3, after=out_sems[3])
    z = _in_proj(h, sems[2][0], where, (2, 2, 3, 0), z, sems[3], "in_proj_received")
    (wia,) = _split_wait("forward_wait", _forward_copies, sems[0], sems[1], sems[2], z)
    z = _in_proj(h, wia, where, (2, 2, 3, 1), z, token, "in_proj_forwarded")
    yh, ya, kept = _mixers_forward(z, lb_logits, hgrn_gnorm)
    (woa,) = _split_wait("gather_out_wait", _gather_out_copies, out_sems[0], out_sems[1], out_sems[2], ya)
    w_out_all = woa.reshape(2 * SEG, D)

    loss, dfg, dlogits, dgn, g_w_out, g_w_in, sources, dout = _backward_to_dz(
        z, kept, lb_logits, hgrn_gnorm, yh, ya, w_out_all, x2, tgt, fgain, h)

    sib_i = lax.empty((4, g_w_in.shape[1] // 2, g_w_in.shape[2]), F32)
    sib_o = lax.empty((4, g_w_out.shape[1] // 2, g_w_out.shape[2]), F32)
    sems = _split_start("swap_start", _swap_copies, [g_w_in, g_w_out, sib_i, sib_o], 2)
    grad_x, dg0 = _grad_x_half(sources, wia, x2, rinv, norm_gain, dout, sems[3], 0, None)
    g_w_in, g_w_out, sib_i, sib_o = _split_wait("swap_wait", _swap_copies, sems[0], sems[1], sems[2], grad_x)
    qi, own_i = _pair_sum(g_w_in, sib_i, where, "pair_sum_w_in")
    qo, own_o = _pair_sum(g_w_out, sib_o, where, "pair_sum_w_out")
    ri = lax.empty((3,) + qi.shape[1:], BF16)
    ro = lax.empty((3,) + qo.shape[1:], BF16)
    sems = _split_start("scatter_start", _scatter_copies, [qi, qo, ri, ro], 6)
    grad_x, dg1 = _grad_x_half(sources, wia, x2, rinv, norm_gain, dout, sems[3], 1, grad_x)
    _, _, got_i, got_o = _split_wait("scatter_wait", _scatter_copies, sems[0], sems[1], sems[2], grad_x)
    grad_w_in, grad_w_out = _join_halves(_chip_sum(own_i, got_i, where, "chip_sum_w_in"),
                                         _chip_sum(own_o, got_o, where, "chip_sum_w_out"))

    small = _all_reduce_small(_pack_small(D, loss, dg0 + dg1, dlogits, dgn, dfg))
    grad_norm_gain = small[0:1, :]
    grad_final_gain = small[1:2, :]
    grad_lb_logits = small[2:4, :SEG]
    grad_hgrn_gnorm = small[4:5, :HGRN_HEAD]
    loss_sum = small[5, 0]

    d_ng, m_ng, v_ng = _adamw(norm_gain, grad_norm_gain, m_norm_gain, v_norm_gain, "adamw_norm_gain")
    d_wi, m_wi, v_wi = _adamw(w_in[0], grad_w_in, m_w_in[0], v_w_in[0], "adamw_w_in")
    d_lb, m_lb, v_lb = _adamw(lb_logits, grad_lb_logits, m_lb_logits, v_lb_logits, "adamw_lb_logits")
    d_gn, m_gn, v_gn = _adamw(hgrn_gnorm, grad_hgrn_gnorm, m_hgrn_gnorm, v_hgrn_gnorm, "adamw_hgrn_gnorm")
    d_wo, m_wo, v_wo = _adamw(w_out[0], grad_w_out, m_w_out[0], v_w_out[0], "adamw_w_out")
    d_fg, m_fg, v_fg = _adamw(fgain, grad_final_gain, m_final_gain.reshape(1, D), v_final_gain.reshape(1, D),
                              "adamw_final_gain")

    return (loss_sum, grad_x[None],
            grad_norm_gain, grad_w_in[None], grad_lb_logits, grad_hgrn_gnorm, grad_w_out[None], grad_final_gain[0],
            d_ng, d_wi[None], d_lb, d_gn, d_wo[None], d_fg[0],
            m_ng, m_wi[None], m_lb, m_gn, m_wo[None], m_fg[0],
            v_ng, v_wi[None], v_lb, v_gn, v_wo[None], v_fg[0])
```

```python
import jax
import jax.numpy as jnp
import numpy as np
from jax import lax
from jax.experimental import pallas as pl
from jax.experimental.pallas import tpu as pltpu

F32 = jnp.float32
BF16 = jnp.bfloat16
MESH = pl.DeviceIdType.MESH

NORM_EPS = 1e-6
HGRN_HEAD = 128
HGRN_CHUNK = 64
HGRN_TILE = 128
HGRN_BLOCK = 512
ATTN_HEAD = 64
LANES = 128
BAND = 128
DILATIONS = (1, 4, 16)
ATTN_SCALE = ATTN_HEAD ** -0.5
assert ATTN_SCALE == 0.125
ATTN_BLOCK_ELEMS = BAND * 2048
ATTN_UNROLL = 4
SEG_QKV = 4
SEG_GATE_A = 7
NEG = -1e30

ADAM_LR = 0.001
ADAM_B1 = 0.9
ADAM_B2 = 0.999
ADAM_EPS = 1e-08
ADAM_WD = 0.01
ADAM_STEP = 10

MIB = 1024 * 1024


def _cp(semantics=None, vmem_mib=48):
    return pltpu.CompilerParams(dimension_semantics=semantics, vmem_limit_bytes=vmem_mib * MIB)


def _dot(a, b):
    return jnp.dot(a, b, preferred_element_type=F32)


def _dot_nt(a, b):
    return lax.dot_general(a, b, (((1,), (1,)), ((), ())), preferred_element_type=F32)


def _dot_tn(a, b):
    return lax.dot_general(a, b, (((0,), (0,)), ((), ())), preferred_element_type=F32)


def _split3(x):
    hi = x.astype(BF16)
    r1 = x - hi.astype(F32)
    mid = r1.astype(BF16)
    lo = (r1 - mid.astype(F32)).astype(BF16)
    return hi, mid, lo


def _exact_dot(t_bf16, x):
    hi, mid, lo = _split3(x)
    return _dot(t_bf16, hi) + _dot(t_bf16, mid) + _dot(t_bf16, lo)


def _exact_dot_right(x, t_bf16):
    hi, mid, lo = _split3(x)
    return _dot(hi, t_bf16) + _dot(mid, t_bf16) + _dot(lo, t_bf16)


def _sigmoid(z):
    return jax.nn.sigmoid(z)


def _silu_and_grad(z):
    s = _sigmoid(z)
    return z * s, s * (1.0 + z * (1.0 - s))


def _seg_select(j, values):
    out = values[0]
    for t, v in enumerate(values[1:], 1):
        out = jnp.where(j == t, v, out)
    return out


def _rms_fwd(x2, gain, token):
    S, D = x2.shape
    tm = min(512, S)

    def body(x_ref, g_ref, _, h_ref, r_ref):
        x = x_ref[...]
        r = lax.rsqrt(jnp.mean(x * x, axis=-1, keepdims=True) + NORM_EPS)
        h_ref[...] = ((x * r) * g_ref[...]).astype(BF16)
        r_ref[...] = r

    return pl.pallas_call(
        body, grid=(S // tm,), name="rms_fwd",
        in_specs=[pl.BlockSpec((tm, D), lambda i: (i, 0)), pl.BlockSpec((1, D), lambda i: (0, 0)),
                  pl.BlockSpec(token.shape, lambda i: (0, 0))],
        out_specs=[pl.BlockSpec((tm, D), lambda i: (i, 0)), pl.BlockSpec((tm, 1), lambda i: (i, 0))],
        out_shape=[jax.ShapeDtypeStruct((S, D), BF16), jax.ShapeDtypeStruct((S, 1), F32)],
        compiler_params=_cp(("parallel",)),
    )(x2, gain, token)


def _in_proj(h, w_all, where, segs, z_prev, token, name):
    S, D = h.shape
    SEG = w_all.shape[2] // 2
    tm = min(512, S)
    first, step, count, core = segs

    def body(*refs):
        h_ref, w_ref, o_ref = refs[1], refs[2], refs[-1]
        o_ref[...] = _dot(h_ref[...], w_ref[...])

    def seg_of(j, w):
        off = 0 if core is None else (w[0] if core == 0 else 1 - w[0])
        return (2 * w[1] + first + step * j + off) % 8

    in_specs = [pl.BlockSpec((tm, D), lambda j, i, w: (i, 0)),
                pl.BlockSpec((None, D, SEG), lambda j, i, w: (seg_of(j, w) // 2, 0, seg_of(j, w) % 2)),
                pl.BlockSpec(token.shape, lambda j, i, w: (0, 0))]
    args = [where, h, w_all, token]
    aliases = {}
    if z_prev is not None:
        in_specs.append(ANY)
        args.append(z_prev)
        aliases = {4: 0}
    grid_spec = pltpu.PrefetchScalarGridSpec(
        num_scalar_prefetch=1, grid=(count, S // tm), in_specs=in_specs,
        out_specs=pl.BlockSpec((None, tm, SEG), lambda j, i, w: (seg_of(j, w), i, 0)))
    return pl.pallas_call(
        body, grid_spec=grid_spec, name=name, out_shape=jax.ShapeDtypeStruct((8, S, SEG), F32),
        input_output_aliases=aliases, compiler_params=_cp(("parallel", "parallel")),
    )(*args)


def _out_proj_loss(yh, ya, w_out, x2, tgt, fgain):
    S, D = x2.shape
    SEG = yh.shape[1]
    tm = min(256, S)
    parts = 2

    def body(yh_ref, ya_ref, w_ref, x_ref, t_ref, fg_ref, dout_ref, doutb_ref, loss_ref, dfg_ref):
        i = pl.program_id(0)

        @pl.when(i == 0)
        def _():
            loss_ref[...] = jnp.zeros_like(loss_ref)
            dfg_ref[...] = jnp.zeros_like(dfg_ref)

        fg = fg_ref[...]
        loss = jnp.zeros((1, 1), F32)
        dfg = jnp.zeros((1, D), F32)
        for rows in [pl.ds(p * (tm // parts), tm // parts) for p in range(parts)]:
            out = (x_ref[rows, :] + _dot(yh_ref[rows, :], w_ref[pl.ds(0, SEG), :])
                   + _dot(ya_ref[rows, :], w_ref[pl.ds(SEG, SEG), :]))
            r = lax.rsqrt(jnp.mean(out * out, axis=-1, keepdims=True) + NORM_EPS)
            n = out * r
            err = n * fg - t_ref[rows, :]
            loss = loss + 0.5 * jnp.sum(jnp.mean(err * err, axis=-1, keepdims=True), axis=0, keepdims=True)
            dy = err * (1.0 / D)
            dfg = dfg + jnp.sum(dy * n, axis=0, keepdims=True)
            dn = dy * fg
            dout = r * (dn - n * jnp.mean(dn * n, axis=-1, keepdims=True))
            dout_ref[rows, :] = dout
            doutb_ref[rows, :] = dout.astype(BF16)
        loss_ref[...] += loss
        dfg_ref[...] += dfg

    row = lambda i: (i, 0)
    fix = lambda i: (0, 0)
    return pl.pallas_call(
        body, grid=(S // tm,), name="out_proj_loss",
        in_specs=[pl.BlockSpec((tm, SEG), row), pl.BlockSpec((tm, SEG), row), pl.BlockSpec((2 * SEG, D), fix),
                  pl.BlockSpec((tm, D), row), pl.BlockSpec((tm, D), row), pl.BlockSpec((1, D), fix)],
        out_specs=[pl.BlockSpec((tm, D), row), pl.BlockSpec((tm, D), row), pl.BlockSpec((1, 1), fix),
                   pl.BlockSpec((1, D), fix)],
        out_shape=[jax.ShapeDtypeStruct((S, D), F32), jax.ShapeDtypeStruct((S, D), BF16),
                   jax.ShapeDtypeStruct((1, 1), F32), jax.ShapeDtypeStruct((1, D), F32)],
        compiler_params=_cp(("arbitrary",)),
    )(yh, ya, w_out, x2, tgt, fgain)


def _dy_proj(doutb, w_out):
    S, D = doutb.shape
    K = w_out.shape[0]
    tm = min(512, S)

    def body(d_ref, w_ref, o_ref):
        o_ref[...] = _dot_nt(d_ref[...], w_ref[...])

    return pl.pallas_call(
        body, grid=(S // tm,), name="dy_proj",
        in_specs=[pl.BlockSpec((tm, D), lambda i: (i, 0)), pl.BlockSpec((K, D), lambda i: (0, 0))],
        out_specs=pl.BlockSpec((tm, K), lambda i: (i, 0)),
        out_shape=jax.ShapeDtypeStruct((S, K), F32),
        compiler_params=_cp(("parallel",)),
    )(doutb, w_out)


def _grad_w_out(yh, ya, doutb):
    S, SEG = yh.shape
    D = doutb.shape[1]
    R = (2 * SEG) // 4
    nb_half = SEG // R
    tk = min(512, S)

    def body(yh_ref, ya_ref, d_ref, o_ref):
        q = pl.program_id(0)
        k = pl.program_id(1)

        @pl.when(k == 0)
        def _():
            o_ref[...] = jnp.zeros_like(o_ref)

        @pl.when(q < nb_half)
        def _():
            o_ref[...] += _dot_tn(yh_ref[...], d_ref[...])

        @pl.when(q >= nb_half)
        def _():
            o_ref[...] += _dot_tn(ya_ref[...], d_ref[...])

    return pl.pallas_call(
        body, grid=(4, S // tk), name="grad_w_out",
        in_specs=[pl.BlockSpec((tk, R), lambda q, k: (k, jnp.minimum(q, nb_half - 1))),
                  pl.BlockSpec((tk, R), lambda q, k: (k, jnp.maximum(q - nb_half, 0))),
                  pl.BlockSpec((tk, D), lambda q, k: (k, 0))],
        out_specs=pl.BlockSpec((None, R, D), lambda q, k: (q, 0, 0)),
        out_shape=jax.ShapeDtypeStruct((4, R, D), F32),
        compiler_params=_cp(("parallel", "arbitrary")),
    )(yh, ya, doutb)


def _dz_sources(sources):
    counts = [s.shape[0] for s in sources]
    starts = [sum(counts[:k]) for k in range(len(counts))]
    assert sum(counts) == 8
    return counts, starts


def _dh_proj(sources, w_all, token, part, name):
    S = sources[0].shape[1]
    D = w_all.shape[1]
    SEG = w_all.shape[2] // 2
    counts, starts = _dz_sources(sources)
    assert all(c % 2 == 0 for c in counts)
    ns = len(sources)
    tm = min(512, S // 2)
    nt = (S // 2) // tm
    t0 = part * nt

    def body(*refs):
        src = refs[:ns]
        w_ref, _, o_ref = refs[ns:]
        j = pl.program_id(1)

        @pl.when(j == 0)
        def _():
            o_ref[...] = jnp.zeros_like(o_ref)

        for k in range(ns):
            @pl.when((2 * j >= starts[k]) & (2 * j < starts[k] + counts[k]))
            def _(k=k):
                o_ref[...] += (_dot_nt(src[k][0], w_ref[:, pl.ds(0, SEG)])
                               + _dot_nt(src[k][1], w_ref[:, pl.ds(SEG, SEG)]))

    def src_spec(k):
        return pl.BlockSpec((2, tm, SEG),
                            lambda i, j: (jnp.clip(j - starts[k] // 2, 0, counts[k] // 2 - 1), t0 + i, 0))

    return pl.pallas_call(
        body, grid=(nt, 4), name=name,
        in_specs=[src_spec(k) for k in range(ns)] + [pl.BlockSpec((None, D, 2 * SEG), lambda i, j: (j, 0, 0)),
                                                     pl.BlockSpec(token.shape, lambda i, j: (0, 0))],
        out_specs=pl.BlockSpec((tm, D), lambda i, j: (i, 0)),
        out_shape=jax.ShapeDtypeStruct((S // 2, D), F32),
        compiler_params=_cp(("parallel", "arbitrary")),
    )(*sources, w_all, token)


def _rms_bwd(dh, x2, rinv, gain, dout, part, gx_prev, name):
    S, D = x2.shape
    tm = min(256, S // 2)
    nt = (S // 2) // tm
    t0 = part * nt

    def body(dh_ref, x_ref, r_ref, g_ref, dout_ref, *rest):
        gx_ref, dg_ref = rest[-2:]

        @pl.when(pl.program_id(0) == 0)
        def _():
            dg_ref[...] = jnp.zeros_like(dg_ref)

        dh = dh_ref[...]
        r = r_ref[...]
        xhat = x_ref[...] * r
        dg_ref[...] += jnp.sum(dh * xhat, axis=0, keepdims=True)
        dxn = dh * g_ref[...]
        gx_ref[...] = dout_ref[...] + r * (dxn - xhat * jnp.mean(dxn * xhat, axis=-1, keepdims=True))

    row = lambda i: (t0 + i, 0)
    fix = lambda i: (0, 0)
    in_specs = [pl.BlockSpec((tm, D), lambda i: (i, 0)), pl.BlockSpec((tm, D), row), pl.BlockSpec((tm, 1), row),
                pl.BlockSpec((1, D), fix), pl.BlockSpec((tm, D), row)]
    args = [dh, x2, rinv, gain, dout]
    aliases = {}
    if gx_prev is not None:
        in_specs.append(ANY)
        args.append(gx_prev)
        aliases = {5: 0}
    return pl.pallas_call(
        body, grid=(nt,), name=name, in_specs=in_specs,
        out_specs=[pl.BlockSpec((tm, D), row), pl.BlockSpec((1, D), fix)],
        out_shape=[jax.ShapeDtypeStruct((S, D), F32), jax.ShapeDtypeStruct((1, D), F32)],
        input_output_aliases=aliases, compiler_params=_cp(("arbitrary",)),
    )(*args)


def _grad_w_in(h, sources):
    S, D = h.shape
    SEG = sources[0].shape[2]
    counts, starts = _dz_sources(sources)
    ns = len(sources)
    tk = min(1024, S)

    def body(*refs):
        h_ref = refs[0]
        src = refs[1:1 + ns]
        o_ref = refs[1 + ns]
        j = pl.program_id(0)
        k = pl.program_id(1)

        @pl.when(k == 0)
        def _():
            o_ref[...] = jnp.zeros_like(o_ref)

        for s in range(ns):
            @pl.when((j >= starts[s]) & (j < starts[s] + counts[s]))
            def _(s=s):
                o_ref[...] += _dot_tn(h_ref[...], src[s][...])

    def src_spec(s):
        return pl.BlockSpec((None, tk, SEG),
                            lambda j, k: (jnp.clip(j - starts[s], 0, counts[s] - 1), k, 0))

    return pl.pallas_call(
        body, grid=(8, S // tk), name="grad_w_in",
        in_specs=[pl.BlockSpec((tk, D), lambda j, k: (k, 0))] + [src_spec(s) for s in range(ns)],
        out_specs=pl.BlockSpec((None, D, SEG), lambda j, k: (j // 2, 0, j % 2)),
        out_shape=jax.ShapeDtypeStruct((4, D, 2 * SEG), F32),
        compiler_params=_cp(("parallel", "arbitrary")),
    )(h, *sources)


def _lower_bound(lbl):
    l0 = lbl[0:1, :]
    l1 = lbl[1:2, :]
    m = jnp.maximum(l0, l1)
    e0 = jnp.exp(l0 - m)
    e1 = jnp.exp(l1 - m)
    return e0 / (e0 + e1)


def _tile_masks():
    row = lax.broadcasted_iota(jnp.int32, (HGRN_TILE, HGRN_TILE), 0)
    col = lax.broadcasted_iota(jnp.int32, (HGRN_TILE, HGRN_TILE), 1)
    same = (row // HGRN_CHUNK) == (col // HGRN_CHUNK)
    return same & (row >= col), same & (row <= col)


def _chunk_last(b):
    T = b.shape[0]
    b3 = b.reshape(T // HGRN_CHUNK, HGRN_CHUNK, HGRN_HEAD)
    return jnp.broadcast_to(b3[:, HGRN_CHUNK - 1:HGRN_CHUNK, :], b3.shape).reshape(T, HGRN_HEAD)


def _chunk_sum(x):
    T = x.shape[0]
    x3 = x.reshape(T // HGRN_CHUNK, HGRN_CHUNK, HGRN_HEAD)
    return jnp.broadcast_to(jnp.sum(x3, axis=1, keepdims=True), x3.shape).reshape(T, HGRN_HEAD)


def _hgrn_dims(S, SEG):
    T = min(HGRN_BLOCK, S)
    assert S % T == 0 and T % HGRN_TILE == 0
    tiles = [slice(t * HGRN_TILE, (t + 1) * HGRN_TILE) for t in range(T // HGRN_TILE)]
    chunks = [slice(c * HGRN_CHUNK, (c + 1) * HGRN_CHUNK) for c in range(T // HGRN_CHUNK)]
    return SEG // HGRN_HEAD, T, T // HGRN_CHUNK, S // T, tiles, chunks


def _hgrn_fwd(zf32, lb_logits, gnorm):
    _, S, SEG = zf32.shape
    H, T, NC, NJ, tiles, chunks = _hgrn_dims(S, SEG)

    def body(zq_ref, zf_ref, zi_ref, zg_ref, lbl_ref, gn_ref, y_ref, st_ref, state):
        @pl.when(pl.program_id(1) == 0)
        def _():
            state[...] = jnp.zeros_like(state)

        lb = _lower_bound(lbl_ref[...])
        tril, _ = _tile_masks()
        tril_bf = tril.astype(BF16)
        zq = zq_ref[...]
        q = zq * _sigmoid(zq)
        f = lb + (1.0 - lb) * _sigmoid(zf_ref[...])
        k = 1.0 - f
        logf = jnp.log(f)
        b = jnp.concatenate([_exact_dot(tril_bf, logf[t]) for t in tiles], axis=0)
        bl = _chunk_last(b)
        qd_b = (q * jnp.exp(b)).astype(BF16)
        kd_b = (k * jnp.exp(-b)).astype(BF16)
        ke_b = (k * jnp.exp(bl - b)).astype(BF16)
        v_b = zi_ref[...].astype(BF16)
        o_intra = jnp.concatenate(
            [_dot(jnp.where(tril, _dot_nt(qd_b[t], kd_b[t]), 0.0).astype(BF16), v_b[t]) for t in tiles], axis=0)
        kvs = [_dot_tn(v_b[r], ke_b[r]) for r in chunks]
        ebl = jnp.exp(bl)
        st = state[...]
        sts = []
        for c in range(NC):
            st_ref[c] = st
            sts.append(st.astype(BF16))
            st = st * ebl[c * HGRN_CHUNK:c * HGRN_CHUNK + 1, :] + kvs[c]
        state[...] = st
        o = o_intra + jnp.concatenate([_dot_nt(qd_b[r], sb) for r, sb in zip(chunks, sts)], axis=0)
        on = o * lax.rsqrt(jnp.mean(o * o, axis=-1, keepdims=True) + NORM_EPS) * gn_ref[...]
        zg = zg_ref[...]
        y_ref[...] = (on * (zg * _sigmoid(zg))).astype(BF16)

    def zspec(seg):
        return pl.BlockSpec((None, T, HGRN_HEAD), lambda h, j: (seg, j, h))

    return pl.pallas_call(
        body, grid=(H, NJ), name="hgrn_fwd",
        in_specs=[zspec(0), zspec(1), zspec(2), zspec(3),
                  pl.BlockSpec((2, HGRN_HEAD), lambda h, j: (0, h)),
                  pl.BlockSpec((1, HGRN_HEAD), lambda h, j: (0, 0))],
        out_specs=[pl.BlockSpec((T, HGRN_HEAD), lambda h, j: (j, h)),
                   pl.BlockSpec((NC, None, HGRN_HEAD, HGRN_HEAD), lambda h, j: (j, h, 0, 0))],
        out_shape=[jax.ShapeDtypeStruct((S, SEG), BF16),
                   jax.ShapeDtypeStruct((S // HGRN_CHUNK, H, HGRN_HEAD, HGRN_HEAD), F32)],
        scratch_shapes=[pltpu.VMEM((HGRN_HEAD, HGRN_HEAD), F32)],
        compiler_params=_cp(("parallel", "arbitrary")),
    )(zf32, zf32, zf32, zf32, lb_logits, gnorm)


def _hgrn_bwd(zf32, lb_logits, gnorm, states, dy):
    _, S, SEG = zf32.shape
    H, T, NC, NJ, tiles, chunks = _hgrn_dims(S, SEG)
    C = HGRN_CHUNK

    def body(zq_ref, zf_ref, zi_ref, zg_ref, lbl_ref, gn_ref, st_ref, dy_ref, dz_ref, dl_ref, dgn_ref, gstate):
        @pl.when(pl.program_id(1) == 0)
        def _():
            gstate[...] = jnp.zeros_like(gstate)
            dl_ref[...] = jnp.zeros_like(dl_ref)
            dgn_ref[...] = jnp.zeros_like(dgn_ref)

        lb = _lower_bound(lbl_ref[...])
        gn = gn_ref[...]
        tril, triu = _tile_masks()
        tril_bf = tril.astype(BF16)
        triu_bf = triu.astype(BF16)
        q, dq_dz = _silu_and_grad(zq_ref[...])
        sf = _sigmoid(zf_ref[...])
        f = lb + (1.0 - lb) * sf
        k = 1.0 - f
        logf = jnp.log(f)
        b = jnp.concatenate([_exact_dot(tril_bf, logf[t]) for t in tiles], axis=0)
        bl = _chunk_last(b)
        eb = jnp.exp(b)
        enb = jnp.exp(-b)
        ekl = jnp.exp(bl - b)
        ebl = jnp.exp(bl)
        qd = q * eb
        kd = k * enb
        ke = k * ekl
        qd_b = qd.astype(BF16)
        kd_b = kd.astype(BF16)
        ke_b = ke.astype(BF16)
        v_b = zi_ref[...].astype(BF16)
        sts = [st_ref[c] for c in range(NC)]
        sts_b = [s.astype(BF16) for s in sts]
        a_b = [jnp.where(tril, _dot_nt(qd_b[t], kd_b[t]), 0.0).astype(BF16) for t in tiles]
        o = (jnp.concatenate([_dot(a, v_b[t]) for a, t in zip(a_b, tiles)], axis=0)
             + jnp.concatenate([_dot_nt(qd_b[r], sb) for r, sb in zip(chunks, sts_b)], axis=0))
        rinv = lax.rsqrt(jnp.mean(o * o, axis=-1, keepdims=True) + NORM_EPS)
        ohat = o * rinv
        sg, dsg = _silu_and_grad(zg_ref[...])
        dyv = dy_ref[...]
        don = dyv * sg
        dz_ref[3] = (dyv * (ohat * gn) * dsg).astype(BF16)
        dgn_ref[...] += jnp.sum(don * ohat, axis=0, keepdims=True)
        dohat = don * gn
        do = rinv * (dohat - ohat * jnp.mean(dohat * ohat, axis=-1, keepdims=True))
        do_b = do.astype(BF16)
        da_b = [jnp.where(tril, _dot_nt(do_b[t], v_b[t]), 0.0).astype(BF16) for t in tiles]
        dv_intra = jnp.concatenate([_dot_tn(a, do_b[t]) for a, t in zip(a_b, tiles)], axis=0)
        dqd_intra = jnp.concatenate([_dot(da, kd_b[t]) for da, t in zip(da_b, tiles)], axis=0)
        dkd = jnp.concatenate([_dot_tn(da, qd_b[t]) for da, t in zip(da_b, tiles)], axis=0)
        dqd_inter = jnp.concatenate([_dot(do_b[r], sb) for r, sb in zip(chunks, sts_b)], axis=0)
        gks = [_dot_tn(do_b[r], qd_b[r]) for r in chunks]
        g = gstate[...]
        gs = [None] * NC
        for c in reversed(range(NC)):
            gs[c] = g
            g = g * ebl[c * C:c * C + 1, :] + gks[c]
        gstate[...] = g
        gs_b = [x.astype(BF16) for x in gs]
        dv = dv_intra + jnp.concatenate([_dot_nt(ke_b[r], gb) for r, gb in zip(chunks, gs_b)], axis=0)
        dz_ref[2] = dv.astype(BF16)
        dke = jnp.concatenate([_dot(v_b[r], gb) for r, gb in zip(chunks, gs_b)], axis=0)
        debl = jnp.concatenate(
            [jnp.broadcast_to(jnp.sum(x * s, axis=0, keepdims=True), (C, HGRN_HEAD)) for x, s in zip(gs, sts)], axis=0)
        dqd = dqd_intra + dqd_inter
        dz_ref[0] = ((dqd * eb) * dq_dz).astype(BF16)
        t_ke = dke * ke
        db = dqd * qd - dkd * kd - t_ke
        db_last = _chunk_sum(t_ke) + debl * ebl
        dk = dkd * enb + dke * ekl
        dlogf = jnp.concatenate([_exact_dot(triu_bf, db[t]) for t in tiles], axis=0) + db_last
        df = dlogf / f - dk
        dz_ref[1] = (df * (1.0 - lb) * (sf * (1.0 - sf))).astype(BF16)
        dlb = jnp.sum(df * (1.0 - sf), axis=0, keepdims=True)
        dl0 = dlb * lb * (1.0 - lb)
        dl_ref[0:1, :] += dl0
        dl_ref[1:2, :] -= dl0

    def zspec(seg):
        return pl.BlockSpec((None, T, HGRN_HEAD), lambda h, j: (seg, NJ - 1 - j, h))

    return pl.pallas_call(
        body, grid=(H, NJ), name="hgrn_bwd",
        in_specs=[zspec(0), zspec(1), zspec(2), zspec(3),
                  pl.BlockSpec((2, HGRN_HEAD), lambda h, j: (0, h)),
                  pl.BlockSpec((1, HGRN_HEAD), lambda h, j: (0, 0)),
                  pl.BlockSpec((NC, None, HGRN_HEAD, HGRN_HEAD), lambda h, j: (NJ - 1 - j, h, 0, 0)),
                  pl.BlockSpec((T, HGRN_HEAD), lambda h, j: (NJ - 1 - j, h))],
        out_specs=[pl.BlockSpec((4, T, HGRN_HEAD), lambda h, j: (0, NJ - 1 - j, h)),
                   pl.BlockSpec((2, HGRN_HEAD), lambda h, j: (0, h)),
                   pl.BlockSpec((None, 1, HGRN_HEAD), lambda h, j: (h, 0, 0))],
        out_shape=[jax.ShapeDtypeStruct((4, S, SEG), BF16), jax.ShapeDtypeStruct((2, SEG), F32),
                   jax.ShapeDtypeStruct((H, 1, HGRN_HEAD), F32)],
        scratch_shapes=[pltpu.VMEM((HGRN_HEAD, HGRN_HEAD), F32)],
        compiler_params=_cp(("parallel", "arbitrary")),
    )(zf32, zf32, zf32, zf32, lb_logits, gnorm, states, dy)


def _alibi_slopes(seg):
    n_heads = seg // ATTN_HEAD
    s = 2.0 ** (-8.0 * np.arange(1, n_heads + 1, dtype=np.float64) / n_heads)
    return jnp.asarray(np.repeat(s, ATTN_HEAD)[None, :], F32)


def _attn_dims(S, SEG, d):
    rb = BAND * d
    assert S % rb == 0 and SEG % LANES == 0
    cb = min(SEG, ATTN_BLOCK_ELEMS // rb) if d == 1 else LANES
    assert SEG % cb == 0
    return rb, cb, S // rb, SEG // cb


def _res_rows(r, d):
    return pl.ds(0, BAND) if d == 1 else pl.ds(r, BAND, stride=d)


def _for_residues(d, fn):
    if d == 1:
        fn(0)
    else:
        def step(r, carry):
            fn(r)
            return carry
        lax.fori_loop(0, d, step, 0, unroll=ATTN_UNROLL)


def _band_terms(n, d):
    i = lax.broadcasted_iota(jnp.int32, (BAND, 2 * BAND), 0)
    jj = lax.broadcasted_iota(jnp.int32, (BAND, 2 * BAND), 1)
    delta = BAND + i - jj
    valid = (delta >= 0) & (delta <= BAND) & ((n > 0) | (jj >= BAND))
    return (-d * delta).astype(F32), valid


def _head_biases(slopes, nd, valid):
    out = []
    for s in _per_head(slopes):
        s2 = jnp.concatenate([s, s], axis=1)
        out.append(jnp.where(valid, s2 * nd, NEG))
    return jnp.concatenate(out, axis=0)


def _stack_heads(x):
    lane = lax.broadcasted_iota(jnp.int32, x.shape, 1)
    zero = jnp.zeros_like(x)
    return jnp.concatenate([jnp.where(lane < ATTN_HEAD, x, zero), jnp.where(lane < ATTN_HEAD, zero, x)], axis=0)


def _unstack_heads(x2):
    first = lax.broadcasted_iota(jnp.int32, (BAND, LANES), 1) < ATTN_HEAD
    return jnp.where(first, x2[:BAND], x2[BAND:])


def _stack_per_head(x):
    a, b = _per_head(x)
    col = jnp.concatenate([a, b], axis=0)
    return jnp.concatenate([col, col], axis=1)


def _per_head(x):
    lane = lax.broadcasted_iota(jnp.int32, x.shape, 1)
    sw = pltpu.roll(x, ATTN_HEAD, 1)
    first = lane < ATTN_HEAD
    return jnp.where(first, x, sw), jnp.where(first, sw, x)


def _attn_fwd(qkv, slopes, d):
    _, S, SEG = qkv.shape
    rb, cb, nb, ncb = _attn_dims(S, SEG, d)
    NP = cb // LANES

    def body(q_ref, kp_ref, kc_ref, vp_ref, vc_ref, sl_ref, o_ref, l_ref):
        n = pl.program_id(1)
        nd, valid = _band_terms(n, d)
        biases = [_head_biases(sl_ref[:, p * LANES:(p + 1) * LANES], nd, valid) for p in range(NP)]

        def residue(r):
            rows = _res_rows(r, d)
            for p in range(NP):
                cols = slice(p * LANES, (p + 1) * LANES)
                kc = jnp.concatenate([kp_ref[rows, cols], kc_ref[rows, cols]], axis=0).astype(BF16)
                vc = jnp.concatenate([vp_ref[rows, cols], vc_ref[rows, cols]], axis=0).astype(BF16)
                s = _dot_nt(_stack_heads((q_ref[rows, cols] * ATTN_SCALE).astype(BF16)), kc) + biases[p]
                m = jnp.max(s, axis=-1, keepdims=True)
                e = jnp.exp(s - m)
                den = jnp.sum(e, axis=-1, keepdims=True)
                o_ref[rows, cols] = _unstack_heads(_dot(e.astype(BF16), vc) / den)
                l_ref[rows, cols] = _unstack_heads(jnp.broadcast_to(m + jnp.log(den), (2 * BAND, LANES)))

        _for_residues(d, residue)

    def spec(seg, prev):
        if prev:
            return pl.BlockSpec((None, rb, cb), lambda c, n: (SEG_QKV + seg, jnp.maximum(n - 1, 0), c))
        return pl.BlockSpec((None, rb, cb), lambda c, n: (SEG_QKV + seg, n, c))

    out = pl.BlockSpec((rb, cb), lambda c, n: (n, c))
    return pl.pallas_call(
        body, grid=(ncb, nb), name=f"attn_fwd_d{d}",
        in_specs=[spec(0, False), spec(1, True), spec(1, False), spec(2, True), spec(2, False),
                  pl.BlockSpec((1, cb), lambda c, n: (0, c))],
        out_specs=[out, out],
        out_shape=[jax.ShapeDtypeStruct((S, SEG), F32), jax.ShapeDtypeStruct((S, SEG), F32)],
        compiler_params=_cp(("parallel", "parallel")),
    )(qkv, qkv, qkv, qkv, qkv, slopes)


def _attn_merge(outs, lses, zf32):
    S, SEG = outs[0].shape
    tm = min(256, S)

    def body(o1, o2, o3, l1, l2, l3, zg_ref, o_ref, lse_ref, y_ref):
        a, b, c = l1[...], l2[...], l3[...]
        m = jnp.maximum(jnp.maximum(a, b), c)
        ea, eb, ec = jnp.exp(a - m), jnp.exp(b - m), jnp.exp(c - m)
        tot = ea + eb + ec
        o = (ea / tot) * o1[...] + (eb / tot) * o2[...] + (ec / tot) * o3[...]
        o_ref[...] = o
        lse_ref[...] = m + jnp.log(tot)
        zg = zg_ref[...]
        y_ref[...] = (o * (zg * _sigmoid(zg))).astype(BF16)

    row = pl.BlockSpec((tm, SEG), lambda i: (i, 0))
    return pl.pallas_call(
        body, grid=(S // tm,), name="attn_merge",
        in_specs=[row] * 6 + [pl.BlockSpec((None, tm, SEG), lambda i: (SEG_GATE_A, i, 0))],
        out_specs=[row, row, row],
        out_shape=[jax.ShapeDtypeStruct((S, SEG), F32), jax.ShapeDtypeStruct((S, SEG), F32),
                   jax.ShapeDtypeStruct((S, SEG), BF16)],
        compiler_params=_cp(("parallel",)),
    )(*outs, *lses, zf32)


def _attn_gate_bwd(dy, o, zf32):
    S, SEG = o.shape
    tm = min(256, S)
    NP = SEG // LANES

    def body(dy_ref, o_ref, zg_ref, do_ref, dl_ref, dzg_ref):
        r = lax.broadcasted_iota(jnp.int32, (LANES, LANES), 0) // ATTN_HEAD
        c = lax.broadcasted_iota(jnp.int32, (LANES, LANES), 1) // ATTN_HEAD
        same_head = (r == c).astype(BF16)
        for p in range(NP):
            cols = slice(p * LANES, (p + 1) * LANES)
            sg, dsg = _silu_and_grad(zg_ref[:, cols])
            dyv = dy_ref[:, cols]
            ov = o_ref[:, cols]
            do = dyv * sg
            do_ref[:, cols] = do
            dzg_ref[:, cols] = (dyv * ov * dsg).astype(BF16)
            dl_ref[:, cols] = _exact_dot_right(do * ov, same_head)

    return pl.pallas_call(
        body, grid=(S // tm,), name="attn_gate_bwd",
        in_specs=[pl.BlockSpec((tm, SEG), lambda i: (i, 1)), pl.BlockSpec((tm, SEG), lambda i: (i, 0)),
                  pl.BlockSpec((None, tm, SEG), lambda i: (SEG_GATE_A, i, 0))],
        out_specs=[pl.BlockSpec((tm, SEG), lambda i: (i, 0)), pl.BlockSpec((tm, SEG), lambda i: (i, 0)),
                   pl.BlockSpec((None, tm, SEG), lambda i: (3, i, 0))],
        out_shape=[jax.ShapeDtypeStruct((S, SEG), F32), jax.ShapeDtypeStruct((S, SEG), F32),
                   jax.ShapeDtypeStruct((4, S, SEG), BF16)],
        compiler_params=_cp(("parallel",)),
    )(dy, o, zf32)


def _attn_bwd(qkv, slopes, do, lse, dl, d, acc, into):
    _, S, SEG = qkv.shape
    rb, cb, nb, ncb = _attn_dims(S, SEG, d)
    NP = cb // LANES
    has_acc = acc is not None
    out_dtype = F32 if into is None else into.dtype

    def body(*refs):
        q_ref, kp_ref, kc_ref, vp_ref, vc_ref, sl_ref, do_ref, lse_ref, dl_ref = refs[:9]
        acc_ref = refs[9] if has_acc else None
        out_ref, cq, ck, cv = refs[-4:]
        n = pl.program_id(1)

        def emit(r, rows, cols, dq, dk, dv):
            for t, val in enumerate((dq, dk, dv)):
                if has_acc:
                    val = val + acc_ref.at[t][rows, cols]
                out_ref.at[t][rows, cols] = val.astype(out_dtype)

        @pl.when(n == 0)
        def _():
            cq[...] = jnp.zeros_like(cq)
            ck[...] = jnp.zeros_like(ck)
            cv[...] = jnp.zeros_like(cv)

        @pl.when(n < nb)
        def _():
            nd, valid = _band_terms(n, d)
            biases = [_head_biases(sl_ref[:, p * LANES:(p + 1) * LANES], nd, valid) for p in range(NP)]

            def residue(r):
                rows = _res_rows(r, d)
                for p in range(NP):
                    cols = slice(p * LANES, (p + 1) * LANES)
                    kc = jnp.concatenate([kp_ref[rows, cols], kc_ref[rows, cols]], axis=0).astype(BF16)
                    vc = jnp.concatenate([vp_ref[rows, cols], vc_ref[rows, cols]], axis=0).astype(BF16)
                    qs = _stack_heads((q_ref[rows, cols] * ATTN_SCALE).astype(BF16))
                    dos = _stack_heads(do_ref[rows, cols].astype(BF16))
                    pr = jnp.exp(_dot_nt(qs, kc) + biases[p] - _stack_per_head(lse_ref[rows, cols]))
                    ds = (pr * (_dot_nt(dos, vc) - _stack_per_head(dl_ref[rows, cols]))).astype(BF16)
                    dq = _unstack_heads(_dot(ds, kc)) * ATTN_SCALE
                    dk = _dot_tn(ds, qs)
                    dv = _dot_tn(pr.astype(BF16), dos)
                    emit(r, rows, cols, cq[r, :, cols], ck[r, :, cols] + dk[:BAND, :], cv[r, :, cols] + dv[:BAND, :])
                    cq[r, :, cols] = dq
                    ck[r, :, cols] = dk[BAND:, :]
                    cv[r, :, cols] = dv[BAND:, :]

            _for_residues(d, residue)

        @pl.when(n == nb)
        def _():
            _for_residues(d, lambda r: emit(r, _res_rows(r, d), slice(None), cq[r], ck[r], cv[r]))

    cur2 = lambda c, n: (jnp.minimum(n, nb - 1), c)
    lag3 = lambda c, n: (0, jnp.clip(n - 1, 0, nb - 1), c)

    def spec(seg, prev):
        if prev:
            return pl.BlockSpec((None, rb, cb), lambda c, n: (SEG_QKV + seg, jnp.clip(n - 1, 0, nb - 1), c))
        return pl.BlockSpec((None, rb, cb), lambda c, n: (SEG_QKV + seg, jnp.minimum(n, nb - 1), c))

    in_specs = [spec(0, False), spec(1, True), spec(1, False), spec(2, True), spec(2, False),
                pl.BlockSpec((1, cb), lambda c, n: (0, c)),
                pl.BlockSpec((rb, cb), cur2), pl.BlockSpec((rb, cb), cur2), pl.BlockSpec((rb, cb), cur2)]
    args = [qkv, qkv, qkv, qkv, qkv, slopes, do, lse, dl]
    aliases = {}
    if has_acc:
        in_specs.append(pl.BlockSpec((3, rb, cb), lag3))
        args.append(acc)
        if into is None:
            aliases = {9: 0}
    if into is not None:
        in_specs.append(ANY)
        args.append(into)
        aliases = {len(args) - 1: 0}
    out_sds = jax.ShapeDtypeStruct((3, S, SEG), F32) if into is None else jax.ShapeDtypeStruct(into.shape, into.dtype)
    return pl.pallas_call(
        body, grid=(ncb, nb + 1), name=f"attn_bwd_d{d}",
        in_specs=in_specs, out_specs=pl.BlockSpec((3, rb, cb), lag3), out_shape=out_sds,
        scratch_shapes=[pltpu.VMEM((d, BAND, cb), F32)] * 3,
        input_output_aliases=aliases,
        compiler_params=_cp(("parallel", "arbitrary")),
    )(*args)


def _adamw(w, g, m, v, name):
    R, C = w.shape
    tr = R if R <= 256 else 256
    assert R % tr == 0

    def body(w_ref, g_ref, m_ref, v_ref, d_ref, nm_ref, nv_ref):
        g = g_ref[...]
        nm = ADAM_B1 * m_ref[...] + (1.0 - ADAM_B1) * g
        nv = ADAM_B2 * v_ref[...] + (1.0 - ADAM_B2) * (g * g)
        m_hat = nm / (1.0 - ADAM_B1 ** ADAM_STEP)
        v_hat = nv / (1.0 - ADAM_B2 ** ADAM_STEP)
        d_ref[...] = -ADAM_LR * (m_hat / (jnp.sqrt(v_hat) + ADAM_EPS) + ADAM_WD * w_ref[...])
        nm_ref[...] = nm
        nv_ref[...] = nv

    blk = pl.BlockSpec((tr, C), lambda i: (i, 0))
    sds = jax.ShapeDtypeStruct((R, C), F32)
    return pl.pallas_call(
        body, grid=(R // tr,), name=name, in_specs=[blk] * 4, out_specs=[blk] * 3, out_shape=[sds] * 3,
        compiler_params=_cp(("parallel",)),
    )(w, g, m, v)


def _coords():
    return lax.axis_index("x"), lax.axis_index("y"), lax.axis_index("c")


def _other_chips(x, y):
    return [(1 - x, y), (x, 1 - y), (1 - x, 1 - y)]


ANY = pl.BlockSpec(memory_space=pl.ANY)


def _cast_into_slot(w, where, name):
    R, C = w.shape
    tr = min(256, R)

    def body(where_ref, w_ref, o_ref):
        o_ref[...] = w_ref[...].astype(BF16)

    grid_spec = pltpu.PrefetchScalarGridSpec(
        num_scalar_prefetch=1, grid=(R // tr,),
        in_specs=[pl.BlockSpec((tr, C), lambda i, w: (i, 0))],
        out_specs=pl.BlockSpec((None, tr, C), lambda i, w: (w[1], i, 0)))
    return pl.pallas_call(
        body, grid_spec=grid_spec, name=name, out_shape=jax.ShapeDtypeStruct((4, R, C), BF16),
        compiler_params=_cp(("parallel",)),
    )(where, w)


def _pair_sum(g, sib, where, name):
    _, n2, C = g.shape
    N = n2 // 2
    tr = min(256, N)
    nt = N // tr

    def body(where_ref, g_ref, s_ref, qb_ref, own_ref):
        q = pl.program_id(1)
        tot = g_ref[...] + s_ref[...]
        qb_ref[...] = tot.astype(BF16)

        @pl.when(q == where_ref[1])
        def _():
            own_ref[...] = tot

    grid_spec = pltpu.PrefetchScalarGridSpec(
        num_scalar_prefetch=1, grid=(nt, 4),
        in_specs=[pl.BlockSpec((None, tr, C), lambda i, q, w: (q, w[0] * nt + i, 0)),
                  pl.BlockSpec((None, tr, C), lambda i, q, w: (q, i, 0))],
        out_specs=[pl.BlockSpec((None, tr, C), lambda i, q, w: (q, i, 0)),
                   pl.BlockSpec((tr, C), lambda i, q, w: (i, 0))])
    return pl.pallas_call(
        body, grid_spec=grid_spec, name=name,
        out_shape=[jax.ShapeDtypeStruct((4, N, C), BF16), jax.ShapeDtypeStruct((N, C), F32)],
        compiler_params=_cp(("parallel", "arbitrary")),
    )(where, g, sib)


HBM = pl.BlockSpec(memory_space=pltpu.HBM)
SEM = pl.BlockSpec(memory_space=pltpu.SEMAPHORE)


def _in_hbm(a):
    return pltpu.with_memory_space_constraint(a, pltpu.HBM)


def _split_start(name, copies, arrays, n_sems, after=None):
    n = len(arrays)

    def body(*refs):
        for cp in copies(refs[:n], refs[-n - 3], refs[-n - 2]):
            cp.start()
        refs[-1][...] = jnp.zeros_like(refs[-1])

    ordered = () if after is None else (after,)
    outs = pl.pallas_call(
        body, name=name,
        out_shape=(pltpu.SemaphoreType.DMA((n_sems,)), pltpu.SemaphoreType.DMA((n_sems,)),
                   *[pltpu.HBM(a.shape, a.dtype) for a in arrays], jax.ShapeDtypeStruct((8, LANES), F32)),
        in_specs=(HBM,) * n + (ANY,) * len(ordered),
        out_specs=(SEM, SEM) + (HBM,) * n + (pl.BlockSpec(memory_space=pltpu.VMEM),),
        input_output_aliases={i: 2 + i for i in range(n)},
        compiler_params=pltpu.CompilerParams(has_side_effects=pltpu.SideEffectType.DATAFLOW_SIDE_EFFECTING),
    )(*[_in_hbm(a) for a in arrays], *ordered)
    return outs[0], outs[1], list(outs[2:2 + n]), outs[-1]


def _split_wait(name, copies, send_sems, recv_sems, arrays, after):
    n = len(arrays)

    def body(*refs):
        for cp in copies(refs[:n], refs[n], refs[n + 1]):
            cp.wait_send()
            cp.wait_recv()

    outs = pl.pallas_call(
        body, name=name,
        out_shape=tuple(pltpu.HBM(a.shape, a.dtype) for a in arrays),
        in_specs=(HBM,) * n + (SEM, SEM, ANY), out_specs=(HBM,) * n,
        input_output_aliases={i: i for i in range(n)},
        compiler_params=pltpu.CompilerParams(has_side_effects=pltpu.SideEffectType.DATAFLOW_SIDE_EFFECTING),
    )(*arrays, send_sems, recv_sems, after)
    return list(outs)


def _remote(src, dst, sems, k, to):
    send_sems, recv_sems = sems
    return pltpu.make_async_remote_copy(src_ref=src, dst_ref=dst, send_sem=send_sems.at[k], recv_sem=recv_sems.at[k],
                                        device_id=to, device_id_type=MESH)


def _gather_in_copies(refs, send_sems, recv_sems):
    (w,) = refs
    x, y, c = _coords()
    seg = w.shape[2] // 2
    mine = w.at[2 * x + y, :, pl.ds(c * seg, seg)]
    return [_remote(mine, mine, (send_sems, recv_sems), k, (px, py, c)) for k, (px, py) in enumerate(_other_chips(x, y))]


def _gather_out_copies(refs, send_sems, recv_sems):
    (w,) = refs
    x, y, c = _coords()
    mine = w.at[2 * x + y]
    return [_remote(mine, mine, (send_sems, recv_sems), k, (px, py, c)) for k, (px, py) in enumerate(_other_chips(x, y))]


def _swap_copies(refs, send_sems, recv_sems):
    gi, go, si, so = refs
    x, y, c = _coords()
    cps = []
    for a, (src, dst) in enumerate(((gi, si), (go, so))):
        nr = dst.shape[1]
        cps.append(_remote(src.at[:, pl.ds((1 - c) * nr, nr), :], dst, (send_sems, recv_sems), a, (x, y, 1 - c)))
    return cps


def _scatter_copies(refs, send_sems, recv_sems):
    qi, qo, ri, ro = refs
    x, y, c = _coords()
    cps = []
    for k, (px, py) in enumerate(_other_chips(x, y)):
        for a, (src, dst) in enumerate(((qi, ri), (qo, ro))):
            cps.append(_remote(src.at[2 * px + py], dst.at[k], (send_sems, recv_sems), 2 * k + a, (px, py, c)))
    return cps


def _forward_copies(refs, send_sems, recv_sems):
    (w,) = refs
    x, y, c = _coords()
    seg = w.shape[2] // 2
    cps = []
    for k, (px, py) in enumerate(_other_chips(x, y)):
        got = w.at[2 * px + py, :, pl.ds(c * seg, seg)]
        cps.append(_remote(got, got, (send_sems, recv_sems), k, (x, y, 1 - c)))
    return cps


def _chip_sum(own, got, where, name):
    N, C = own.shape
    tr = min(256, N)
    nt = N // tr

    def body(where_ref, own_ref, got_ref, o_ref):
        t = own_ref[...]
        for k in range(3):
            t = t + got_ref[k].astype(F32)
        o_ref[...] = t

    grid_spec = pltpu.PrefetchScalarGridSpec(
        num_scalar_prefetch=1, grid=(nt,),
        in_specs=[pl.BlockSpec((tr, C), lambda i, w: (i, 0)), pl.BlockSpec((3, tr, C), lambda i, w: (0, i, 0))],
        out_specs=pl.BlockSpec((tr, C), lambda i, w: (w[0] * nt + i, 0)))
    return pl.pallas_call(
        body, grid_spec=grid_spec, name=name, out_shape=jax.ShapeDtypeStruct((2 * N, C), F32),
        compiler_params=_cp(("parallel",)),
    )(where, own, got)


def _join_halves(gi, go):
    def body(gi_in, go_in, gi_ref, go_ref, send_sems, recv_sems):
        x, y, c = _coords()
        cps = []
        for a, ref in enumerate((gi_ref, go_ref)):
            nr = ref.shape[0] // 2
            mine = ref.at[pl.ds(c * nr, nr), :]
            cp = pltpu.make_async_remote_copy(src_ref=mine, dst_ref=mine, send_sem=send_sems.at[a],
                                              recv_sem=recv_sems.at[a], device_id=(x, y, 1 - c), device_id_type=MESH)
            cp.start()
            cps.append(cp)
        for a, ref in enumerate((gi_ref, go_ref)):
            nr = ref.shape[0] // 2
            theirs = ref.at[pl.ds((1 - c) * nr, nr), :]
            pltpu.make_async_remote_copy(src_ref=theirs, dst_ref=theirs, send_sem=send_sems.at[a],
                                         recv_sem=recv_sems.at[a], device_id=(x, y, 1 - c),
                                         device_id_type=MESH).wait_recv()
        for cp in cps:
            cp.wait_send()

    return pl.pallas_call(
        body, name="join_halves", in_specs=[ANY, ANY], out_specs=[ANY, ANY],
        out_shape=[jax.ShapeDtypeStruct(gi.shape, F32), jax.ShapeDtypeStruct(go.shape, F32)],
        scratch_shapes=[pltpu.SemaphoreType.DMA((2,)), pltpu.SemaphoreType.DMA((2,))],
        input_output_aliases={0: 0, 1: 1},
    )(gi, go)


def _all_reduce_small(part):
    R, C = part.shape

    def body(p_ref, o_ref, slots, send_sems, recv_sems):
        x, y, c = _coords()
        me = 4 * x + 2 * y + c
        slots[me] = p_ref[...]
        cps = []
        for k in range(1, 8):
            fx, fy, fc = (k >> 2) & 1, (k >> 1) & 1, k & 1
            peer = (1 - x if fx else x, 1 - y if fy else y, 1 - c if fc else c)
            cp = pltpu.make_async_remote_copy(src_ref=p_ref, dst_ref=slots.at[me], send_sem=send_sems.at[k - 1],
                                              recv_sem=recv_sems.at[k - 1], device_id=peer, device_id_type=MESH)
            cp.start()
            cps.append(cp)
        for cp in cps:
            cp.wait()
        t = slots[0]
        for k in range(1, 8):
            t = t + slots[k]
        o_ref[...] = t

    vm = pl.BlockSpec(memory_space=pltpu.VMEM)
    return pl.pallas_call(
        body, name="all_reduce_small", in_specs=[vm], out_specs=vm,
        out_shape=jax.ShapeDtypeStruct((R, C), F32),
        scratch_shapes=[pltpu.VMEM((8, R, C), F32), pltpu.SemaphoreType.DMA((7,)), pltpu.SemaphoreType.DMA((7,))],
    )(part)


def _mixers_forward(z, lb_logits, hgrn_gnorm):
    slopes = _alibi_slopes(z.shape[2])
    yh, states = _hgrn_fwd(z, lb_logits, hgrn_gnorm)
    outs, lses = [], []
    for d in DILATIONS:
        o, l = _attn_fwd(z, slopes, d)
        outs.append(o)
        lses.append(l)
    o_attn, lse, ya = _attn_merge(outs, lses, z)
    return yh, ya, (states, o_attn, lse, slopes)


def _backward_to_dz(z, kept, lb_logits, hgrn_gnorm, yh, ya, w_out_all, x2, tgt, fgain, h):
    states, o_attn, lse, slopes = kept
    dout, doutb, loss, dfg = _out_proj_loss(yh, ya, w_out_all, x2, tgt, fgain)
    dy = _dy_proj(doutb, w_out_all)
    g_w_out = _grad_w_out(yh, ya, doutb)
    dzh, dlogits, dgn = _hgrn_bwd(z, lb_logits, hgrn_gnorm, states, dy)
    do, dl, dza = _attn_gate_bwd(dy, o_attn, z)
    acc = None
    order = sorted(DILATIONS, reverse=True)
    for d in order[:-1]:
        acc = _attn_bwd(z, slopes, do, lse, dl, d, acc, None)
    dza = _attn_bwd(z, slopes, do, lse, dl, order[-1], acc, dza)
    sources = [dzh, dza]
    g_w_in = _grad_w_in(h, sources)
    return loss, dfg, dlogits, dgn, g_w_out, g_w_in, sources, dout


def _grad_x_half(sources, w_all, x2, rinv, norm_gain, dout, token, part, gx_prev):
    dh = _dh_proj(sources, w_all, token, part, f"dh_proj_{part}")
    return _rms_bwd(dh, x2, rinv, norm_gain, dout, part, gx_prev, f"rms_bwd_{part}")


def _local_step(x2, tgt, norm_gain, w_all, lb_logits, hgrn_gnorm, w_out_all, fgain):
    token = jnp.zeros((8, LANES), F32)
    where = jnp.zeros((2,), jnp.int32)
    h, rinv = _rms_fwd(x2, norm_gain, token)
    z = _in_proj(h, w_all, where, (0, 1, 8, None), None, token, "in_proj_all")
    yh, ya, kept = _mixers_forward(z, lb_logits, hgrn_gnorm)
    loss, dfg, dlogits, dgn, g_w_out, g_w_in, sources, dout = _backward_to_dz(
        z, kept, lb_logits, hgrn_gnorm, yh, ya, w_out_all, x2, tgt, fgain, h)
    gx, dg0 = _grad_x_half(sources, w_all, x2, rinv, norm_gain, dout, token, 0, None)
    gx, dg1 = _grad_x_half(sources, w_all, x2, rinv, norm_gain, dout, token, 1, gx)
    return loss, gx, dg0 + dg1, g_w_in, dlogits, dgn, g_w_out, dfg


def _pack_small(D, loss, dgain, dlogits, dgn, dfg):
    def row(v):
        v = v.reshape(1, -1)
        return jnp.pad(v, ((0, 0), (0, D - v.shape[1])))
    rows = [row(dgain), row(dfg), row(dlogits[0]), row(dlogits[1]), row(jnp.sum(dgn, axis=0)), row(loss)]
    rows += [jnp.zeros((1, D), F32)] * (8 - len(rows))
    return jnp.concatenate(rows, axis=0)


def kernel(x, norm_gain, w_in, lb_logits, hgrn_gnorm, w_out, final_gain, loss_target, m_norm_gain, m_w_in, m_lb_logits, m_hgrn_gnorm, m_w_out, m_final_gain, v_norm_gain, v_w_in, v_lb_logits, v_hgrn_gnorm, v_w_out, v_final_gain):
    _, S, D = x.shape
    SEG = w_in.shape[2] // 2
    x2 = x[0]
    tgt = loss_target[0]
    fgain = final_gain.reshape(1, D)
    where = jnp.stack([lax.axis_index("c"), 2 * lax.axis_index("x") + lax.axis_index("y")]).astype(jnp.int32)

    wia = _cast_into_slot(w_in[0], where, "cast_w_in")
    woa = _cast_into_slot(w_out[0], where, "cast_w_out")
    sems = _split_start("gather_in_start", _gather_in_copies, [wia], 3)
    (wia,), token = sems[2], sems[3]
    h, rinv = _rms_fwd(x2, norm_gain, token)
    z = _in_proj(h, wia, where, (0, 1, 2, None), None, token, "in_proj_own")
    (wia,) = _split_wait("gather_in_wait", _gather_in_copies, sems[0], sems[1], [wia], z)
    out_sems = _split_start("gather_out_start", _gather_out_copies, [woa], 3, after=wia)
    sems = _split_start("forward_start", _forward_copies, [wia], 3, after=out_sems[3])
    z = _in_proj(h, sems[2][0], where, (2, 2, 3, 0), z, sems[3], "in_proj_received")
    (wia,) = _split_wait("forward_wait", _forward_copies, sems[0], sems[1], sems[2], z)
    z = _in_proj(h, wia, where, (2, 2, 3, 1), z, token, "in_proj_forwarded")
    yh, ya, kept = _mixers_forward(z, lb_logits, hgrn_gnorm)
    (woa,) = _split_wait("gather_out_wait", _gather_out_copies, out_sems[0], out_sems[1], out_sems[2], ya)
    w_out_all = woa.reshape(2 * SEG, D)

    loss, dfg, dlogits, dgn, g_w_out, g_w_in, sources, dout = _backward_to_dz(
        z, kept, lb_logits, hgrn_gnorm, yh, ya, w_out_all, x2, tgt, fgain, h)

    sib_i = lax.empty((4, g_w_in.shape[1] // 2, g_w_in.shape[2]), F32)
    sib_o = lax.empty((4, g_w_out.shape[1] // 2, g_w_out.shape[2]), F32)
    sems = _split_start("swap_start", _swap_copies, [g_w_in, g_w_out, sib_i, sib_o], 2)
    grad_x, dg0 = _grad_x_half(sources, wia, x2, rinv, norm_gain, dout, sems[3], 0, None)
    g_w_in, g_w_out, sib_i, sib_o = _split_wait("swap_wait", _swap_copies, sems[0], sems[1], sems[2], grad_x)
    qi, own_i = _pair_sum(g_w_in, sib_i, where, "pair_sum_w_in")
    qo, own_o = _pair_sum(g_w_out, sib_o, where, "pair_sum_w_out")
    ri = lax.empty((3,) + qi.shape[1:], BF16)
    ro = lax.empty((3,) + qo.shape[1:], BF16)
    sems = _split_start("scatter_start", _scatter_copies, [qi, qo, ri, ro], 6)
    grad_x, dg1 = _grad_x_half(sources, wia, x2, rinv, norm_gain, dout, sems[3], 1, grad_x)
    _, _, got_i, got_o = _split_wait("scatter_wait", _scatter_copies, sems[0], sems[1], sems[2], grad_x)
    grad_w_in, grad_w_out = _join_halves(_chip_sum(own_i, got_i, where, "chip_sum_w_in"),
                                         _chip_sum(own_o, got_o, where, "chip_sum_w_out"))

    small = _all_reduce_small(_pack_small(D, loss, dg0 + dg1, dlogits, dgn, dfg))
    grad_norm_gain = small[0:1, :]
    grad_final_gain = small[1:2, :]
    grad_lb_logits = small[2:4, :SEG]
    grad_hgrn_gnorm = small[4:5, :HGRN_HEAD]
    loss_sum = small[5, 0]

    d_ng, m_ng, v_ng = _adamw(norm_gain, grad_norm_gain, m_norm_gain, v_norm_gain, "adamw_norm_gain")
    d_wi, m_wi, v_wi = _adamw(w_in[0], grad_w_in, m_w_in[0], v_w_in[0], "adamw_w_in")
    d_lb, m_lb, v_lb = _adamw(lb_logits, grad_lb_logits, m_lb_logits, v_lb_logits, "adamw_lb_logits")
    d_gn, m_gn, v_gn = _adamw(hgrn_gnorm, grad_hgrn_gnorm, m_hgrn_gnorm, v_hgrn_gnorm, "adamw_hgrn_gnorm")
    d_wo, m_wo, v_wo = _adamw(w_out[0], grad_w_out, m_w_out[0], v_w_out[0], "adamw_w_out")
    d_fg, m_fg, v_fg = _adamw(fgain, grad_final_gain, m_final_gain.reshape(1, D), v_final_gain.reshape(1, D),
                              "adamw_final_gain")

    return (loss_sum, grad_x[None],
            grad_norm_gain, grad_w_in[None], grad_lb_logits, grad_hgrn_gnorm, grad_w_out[None], grad_final_gain[0],
            d_ng, d_wi[None], d_lb, d_gn, d_wo[None], d_fg[0],
            m_ng, m_wi[None], m_lb, m_gn, m_wo[None], m_fg[0],
            v_ng, v_wi[None], v_lb, v_gn, v_wo[None], v_fg[0])
```

```python
import jax
import jax.numpy as jnp
import numpy as np
from jax import lax
from jax.experimental import pallas as pl
from jax.experimental.pallas import tpu as pltpu

F32 = jnp.float32
BF16 = jnp.bfloat16
MESH = pl.DeviceIdType.MESH

NORM_EPS = 1e-6
HGRN_HEAD = 128
HGRN_CHUNK = 64
HGRN_TILE = 128
HGRN_BLOCK = 512
ATTN_HEAD = 64
LANES = 128
BAND = 128
DILATIONS = (1, 4, 16)
ATTN_SCALE = ATTN_HEAD ** -0.5
assert ATTN_SCALE == 0.125
ATTN_BLOCK_ELEMS = BAND * 2048
ATTN_UNROLL = 4
SEG_QKV = 4
SEG_GATE_A = 7
NEG = -1e30

ADAM_LR = 0.001
ADAM_B1 = 0.9
ADAM_B2 = 0.999
ADAM_EPS = 1e-08
ADAM_WD = 0.01
ADAM_STEP = 10

MIB = 1024 * 1024


def _cp(semantics=None, vmem_mib=48):
    return pltpu.CompilerParams(dimension_semantics=semantics, vmem_limit_bytes=vmem_mib * MIB)


def _dot(a, b):
    return jnp.dot(a, b, preferred_element_type=F32)


def _dot_nt(a, b):
    return lax.dot_general(a, b, (((1,), (1,)), ((), ())), preferred_element_type=F32)


def _dot_tn(a, b):
    return lax.dot_general(a, b, (((0,), (0,)), ((), ())), preferred_element_type=F32)


def _split3(x):
    hi = x.astype(BF16)
    r1 = x - hi.astype(F32)
    mid = r1.astype(BF16)
    lo = (r1 - mid.astype(F32)).astype(BF16)
    return hi, mid, lo


def _exact_dot(t_bf16, x):
    hi, mid, lo = _split3(x)
    return _dot(t_bf16, hi) + _dot(t_bf16, mid) + _dot(t_bf16, lo)


def _exact_dot_right(x, t_bf16):
    hi, mid, lo = _split3(x)
    return _dot(hi, t_bf16) + _dot(mid, t_bf16) + _dot(lo, t_bf16)


def _sigmoid(z):
    return jax.nn.sigmoid(z)


def _silu_and_grad(z):
    s = _sigmoid(z)
    return z * s, s * (1.0 + z * (1.0 - s))


def _seg_select(j, values):
    out = values[0]
    for t, v in enumerate(values[1:], 1):
        out = jnp.where(j == t, v, out)
    return out


def _rms_fwd(x2, gain, token):
    S, D = x2.shape
    tm = min(512, S)

    def body(x_ref, g_ref, _, h_ref, r_ref):
        x = x_ref[...]
        r = lax.rsqrt(jnp.mean(x * x, axis=-1, keepdims=True) + NORM_EPS)
        h_ref[...] = ((x * r) * g_ref[...]).astype(BF16)
        r_ref[...] = r

    return pl.pallas_call(
        body, grid=(S // tm,), name="rms_fwd",
        in_specs=[pl.BlockSpec((tm, D), lambda i: (i, 0)), pl.BlockSpec((1, D), lambda i: (0, 0)),
                  pl.BlockSpec(token.shape, lambda i: (0, 0))],
        out_specs=[pl.BlockSpec((tm, D), lambda i: (i, 0)), pl.BlockSpec((tm, 1), lambda i: (i, 0))],
        out_shape=[jax.ShapeDtypeStruct((S, D), BF16), jax.ShapeDtypeStruct((S, 1), F32)],
        compiler_params=_cp(("parallel",)),
    )(x2, gain, token)


def _in_proj(h, w_all, where, segs, z_prev, token, name):
    S, D = h.shape
    SEG = w_all.shape[2] // 2
    tm = min(512, S)
    first, step, count, core = segs

    def body(*refs):
        h_ref, w_ref, o_ref = refs[1], refs[2], refs[-1]
        o_ref[...] = _dot(h_ref[...], w_ref[...])

    def seg_of(j, w):
        off = 0 if core is None else (w[0] if core == 0 else 1 - w[0])
        return (2 * w[1] + first + step * j + off) % 8

    in_specs = [pl.BlockSpec((tm, D), lambda j, i, w: (i, 0)),
                pl.BlockSpec((None, D, SEG), lambda j, i, w: (seg_of(j, w) // 2, 0, seg_of(j, w) % 2)),
                pl.BlockSpec(token.shape, lambda j, i, w: (0, 0))]
    args = [where, h, w_all, token]
    aliases = {}
    if z_prev is not None:
        in_specs.append(ANY)
        args.append(z_prev)
        aliases = {4: 0}
    grid_spec = pltpu.PrefetchScalarGridSpec(
        num_scalar_prefetch=1, grid=(count, S // tm), in_specs=in_specs,
        out_specs=pl.BlockSpec((None, tm, SEG), lambda j, i, w: (seg_of(j, w), i, 0)))
    return pl.pallas_call(
        body, grid_spec=grid_spec, name=name, out_shape=jax.ShapeDtypeStruct((8, S, SEG), F32),
        input_output_aliases=aliases, compiler_params=_cp(("parallel", "parallel")),
    )(*args)


def _out_proj_loss(yh, ya, w_out, x2, tgt, fgain):
    S, D = x2.shape
    SEG = yh.shape[1]
    tm = min(256, S)
    parts = 2

    def body(yh_ref, ya_ref, w_ref, x_ref, t_ref, fg_ref, dout_ref, doutb_ref, loss_ref, dfg_ref):
        i = pl.program_id(0)

        @pl.when(i == 0)
        def _():
            loss_ref[...] = jnp.zeros_like(loss_ref)
            dfg_ref[...] = jnp.zeros_like(dfg_ref)

        fg = fg_ref[...]
        loss = jnp.zeros((1, 1), F32)
        dfg = jnp.zeros((1, D), F32)
        for rows in [pl.ds(p * (tm // parts), tm // parts) for p in range(parts)]:
            out = (x_ref[rows, :] + _dot(yh_ref[rows, :], w_ref[pl.ds(0, SEG), :])
                   + _dot(ya_ref[rows, :], w_ref[pl.ds(SEG, SEG), :]))
            r = lax.rsqrt(jnp.mean(out * out, axis=-1, keepdims=True) + NORM_EPS)
            n = out * r
            err = n * fg - t_ref[rows, :]
            loss = loss + 0.5 * jnp.sum(jnp.mean(err * err, axis=-1, keepdims=True), axis=0, keepdims=True)
            dy = err * (1.0 / D)
            dfg = dfg + jnp.sum(dy * n, axis=0, keepdims=True)
            dn = dy * fg
            dout = r * (dn - n * jnp.mean(dn * n, axis=-1, keepdims=True))
            dout_ref[rows, :] = dout
            doutb_ref[rows, :] = dout.astype(BF16)
        loss_ref[...] += loss
        dfg_ref[...] += dfg

    row = lambda i: (i, 0)
    fix = lambda i: (0, 0)
    return pl.pallas_call(
        body, grid=(S // tm,), name="out_proj_loss",
        in_specs=[pl.BlockSpec((tm, SEG), row), pl.BlockSpec((tm, SEG), row), pl.BlockSpec((2 * SEG, D), fix),
                  pl.BlockSpec((tm, D), row), pl.BlockSpec((tm, D), row), pl.BlockSpec((1, D), fix)],
        out_specs=[pl.BlockSpec((tm, D), row), pl.BlockSpec((tm, D), row), pl.BlockSpec((1, 1), fix),
                   pl.BlockSpec((1, D), fix)],
        out_shape=[jax.ShapeDtypeStruct((S, D), F32), jax.ShapeDtypeStruct((S, D), BF16),
                   jax.ShapeDtypeStruct((1, 1), F32), jax.ShapeDtypeStruct((1, D), F32)],
        compiler_params=_cp(("arbitrary",)),
    )(yh, ya, w_out, x2, tgt, fgain)


def _dy_proj(doutb, w_out):
    S, D = doutb.shape
    K = w_out.shape[0]
    tm = min(512, S)

    def body(d_ref, w_ref, o_ref):
        o_ref[...] = _dot_nt(d_ref[...], w_ref[...])

    return pl.pallas_call(
        body, grid=(S // tm,), name="dy_proj",
        in_specs=[pl.BlockSpec((tm, D), lambda i: (i, 0)), pl.BlockSpec((K, D), lambda i: (0, 0))],
        out_specs=pl.BlockSpec((tm, K), lambda i: (i, 0)),
        out_shape=jax.ShapeDtypeStruct((S, K), F32),
        compiler_params=_cp(("parallel",)),
    )(doutb, w_out)


def _grad_w_out(yh, ya, doutb):
    S, SEG = yh.shape
    D = doutb.shape[1]
    R = (2 * SEG) // 4
    nb_half = SEG // R
    tk = min(512, S)

    def body(yh_ref, ya_ref, d_ref, o_ref):
        q = pl.program_id(0)
        k = pl.program_id(1)

        @pl.when(k == 0)
        def _():
            o_ref[...] = jnp.zeros_like(o_ref)

        @pl.when(q < nb_half)
        def _():
            o_ref[...] += _dot_tn(yh_ref[...], d_ref[...])

        @pl.when(q >= nb_half)
        def _():
            o_ref[...] += _dot_tn(ya_ref[...], d_ref[...])

    return pl.pallas_call(
        body, grid=(4, S // tk), name="grad_w_out",
        in_specs=[pl.BlockSpec((tk, R), lambda q, k: (k, jnp.minimum(q, nb_half - 1))),
                  pl.BlockSpec((tk, R), lambda q, k: (k, jnp.maximum(q - nb_half, 0))),
                  pl.BlockSpec((tk, D), lambda q, k: (k, 0))],
        out_specs=pl.BlockSpec((None, R, D), lambda q, k: (q, 0, 0)),
        out_shape=jax.ShapeDtypeStruct((4, R, D), F32),
        compiler_params=_cp(("parallel", "arbitrary")),
    )(yh, ya, doutb)


def _dz_sources(sources):
    counts = [s.shape[0] for s in sources]
    starts = [sum(counts[:k]) for k in range(len(counts))]
    assert sum(counts) == 8
    return counts, starts


def _dh_proj(sources, w_all, token, part, name):
    S = sources[0].shape[1]
    D = w_all.shape[1]
    SEG = w_all.shape[2] // 2
    counts, starts = _dz_sources(sources)
    assert all(c % 2 == 0 for c in counts)
    ns = len(sources)
    tm = min(512, S // 2)
    nt = (S // 2) // tm
    t0 = part * nt

    def body(*refs):
        src = refs[:ns]
        w_ref, _, o_ref = refs[ns:]
        j = pl.program_id(1)

        @pl.when(j == 0)
        def _():
            o_ref[...] = jnp.zeros_like(o_ref)

        for k in range(ns):
            @pl.when((2 * j >= starts[k]) & (2 * j < starts[k] + counts[k]))
            def _(k=k):
                o_ref[...] += (_dot_nt(src[k][0], w_ref[:, pl.ds(0, SEG)])
                               + _dot_nt(src[k][1], w_ref[:, pl.ds(SEG, SEG)]))

    def src_spec(k):
        return pl.BlockSpec((2, tm, SEG),
                            lambda i, j: (jnp.clip(j - starts[k] // 2, 0, counts[k] // 2 - 1), t0 + i, 0))

    return pl.pallas_call(
        body, grid=(nt, 4), name=name,
        in_specs=[src_spec(k) for k in range(ns)] + [pl.BlockSpec((None, D, 2 * SEG), lambda i, j: (j, 0, 0)),
                                                     pl.BlockSpec(token.shape, lambda i, j: (0, 0))],
        out_specs=pl.BlockSpec((tm, D), lambda i, j: (i, 0)),
        out_shape=jax.ShapeDtypeStruct((S // 2, D), F32),
        compiler_params=_cp(("parallel", "arbitrary")),
    )(*sources, w_all, token)


def _rms_bwd(dh, x2, rinv, gain, dout, part, gx_prev, name):
    S, D = x2.shape
    tm = min(256, S // 2)
    nt = (S // 2) // tm
    t0 = part * nt

    def body(dh_ref, x_ref, r_ref, g_ref, dout_ref, *rest):
        gx_ref, dg_ref = rest[-2:]

        @pl.when(pl.program_id(0) == 0)
        def _():
            dg_ref[...] = jnp.zeros_like(dg_ref)

        dh = dh_ref[...]
        r = r_ref[...]
        xhat = x_ref[...] * r
        dg_ref[...] += jnp.sum(dh * xhat, axis=0, keepdims=True)
        dxn = dh * g_ref[...]
        gx_ref[...] = dout_ref[...] + r * (dxn - xhat * jnp.mean(dxn * xhat, axis=-1, keepdims=True))

    row = lambda i: (t0 + i, 0)
    fix = lambda i: (0, 0)
    in_specs = [pl.BlockSpec((tm, D), lambda i: (i, 0)), pl.BlockSpec((tm, D), row), pl.BlockSpec((tm, 1), row),
                pl.BlockSpec((1, D), fix), pl.BlockSpec((tm, D), row)]
    args = [dh, x2, rinv, gain, dout]
    aliases = {}
    if gx_prev is not None:
        in_specs.append(ANY)
        args.append(gx_prev)
        aliases = {5: 0}
    return pl.pallas_call(
        body, grid=(nt,), name=name, in_specs=in_specs,
        out_specs=[pl.BlockSpec((tm, D), row), pl.BlockSpec((1, D), fix)],
        out_shape=[jax.ShapeDtypeStruct((S, D), F32), jax.ShapeDtypeStruct((1, D), F32)],
        input_output_aliases=aliases, compiler_params=_cp(("arbitrary",)),
    )(*args)


def _grad_w_in(h, sources):
    S, D = h.shape
    SEG = sources[0].shape[2]
    counts, starts = _dz_sources(sources)
    ns = len(sources)
    tk = min(1024, S)

    def body(*refs):
        h_ref = refs[0]
        src = refs[1:1 + ns]
        o_ref = refs[1 + ns]
        j = pl.program_id(0)
        k = pl.program_id(1)

        @pl.when(k == 0)
        def _():
            o_ref[...] = jnp.zeros_like(o_ref)

        for s in range(ns):
            @pl.when((j >= starts[s]) & (j < starts[s] + counts[s]))
            def _(s=s):
                o_ref[...] += _dot_tn(h_ref[...], src[s][...])

    def src_spec(s):
        return pl.BlockSpec((None, tk, SEG),
                            lambda j, k: (jnp.clip(j - starts[s], 0, counts[s] - 1), k, 0))

    return pl.pallas_call(
        body, grid=(8, S // tk), name="grad_w_in",
        in_specs=[pl.BlockSpec((tk, D), lambda j, k: (k, 0))] + [src_spec(s) for s in range(ns)],
        out_specs=pl.BlockSpec((None, D, SEG), lambda j, k: (j // 2, 0, j % 2)),
        out_shape=jax.ShapeDtypeStruct((4, D, 2 * SEG), F32),
        compiler_params=_cp(("parallel", "arbitrary")),
    )(h, *sources)


def _lower_bound(lbl):
    l0 = lbl[0:1, :]
    l1 = lbl[1:2, :]
    m = jnp.maximum(l0, l1)
    e0 = jnp.exp(l0 - m)
    e1 = jnp.exp(l1 - m)
    return e0 / (e0 + e1)


def _tile_masks():
    row = lax.broadcasted_iota(jnp.int32, (HGRN_TILE, HGRN_TILE), 0)
    col = lax.broadcasted_iota(jnp.int32, (HGRN_TILE, HGRN_TILE), 1)
    same = (row // HGRN_CHUNK) == (col // HGRN_CHUNK)
    return same & (row >= col), same & (row <= col)


def _chunk_last(b):
    T = b.shape[0]
    b3 = b.reshape(T // HGRN_CHUNK, HGRN_CHUNK, HGRN_HEAD)
    return jnp.broadcast_to(b3[:, HGRN_CHUNK - 1:HGRN_CHUNK, :], b3.shape).reshape(T, HGRN_HEAD)


def _chunk_sum(x):
    T = x.shape[0]
    x3 = x.reshape(T // HGRN_CHUNK, HGRN_CHUNK, HGRN_HEAD)
    return jnp.broadcast_to(jnp.sum(x3, axis=1, keepdims=True), x3.shape).reshape(T, HGRN_HEAD)


def _hgrn_dims(S, SEG):
    T = min(HGRN_BLOCK, S)
    assert S % T == 0 and T % HGRN_TILE == 0
    tiles = [slice(t * HGRN_TILE, (t + 1) * HGRN_TILE) for t in range(T // HGRN_TILE)]
    chunks = [slice(c * HGRN_CHUNK, (c + 1) * HGRN_CHUNK) for c in range(T // HGRN_CHUNK)]
    return SEG // HGRN_HEAD, T, T // HGRN_CHUNK, S // T, tiles, chunks


def _hgrn_fwd(zf32, lb_logits, gnorm):
    _, S, SEG = zf32.shape
    H, T, NC, NJ, tiles, chunks = _hgrn_dims(S, SEG)

    def body(zq_ref, zf_ref, zi_ref, zg_ref, lbl_ref, gn_ref, y_ref, st_ref, state):
        @pl.when(pl.program_id(1) == 0)
        def _():
            state[...] = jnp.zeros_like(state)

        lb = _lower_bound(lbl_ref[...])
        tril, _ = _tile_masks()
        tril_bf = tril.astype(BF16)
        zq = zq_ref[...]
        q = zq * _sigmoid(zq)
        f = lb + (1.0 - lb) * _sigmoid(zf_ref[...])
        k = 1.0 - f
        logf = jnp.log(f)
        b = jnp.concatenate([_exact_dot(tril_bf, logf[t]) for t in tiles], axis=0)
        bl = _chunk_last(b)
        qd_b = (q * jnp.exp(b)).astype(BF16)
        kd_b = (k * jnp.exp(-b)).astype(BF16)
        ke_b = (k * jnp.exp(bl - b)).astype(BF16)
        v_b = zi_ref[...].astype(BF16)
        o_intra = jnp.concatenate(
            [_dot(jnp.where(tril, _dot_nt(qd_b[t], kd_b[t]), 0.0).astype(BF16), v_b[t]) for t in tiles], axis=0)
        kvs = [_dot_tn(v_b[r], ke_b[r]) for r in chunks]
        ebl = jnp.exp(bl)
        st = state[...]
        sts = []
        for c in range(NC):
            st_ref[c] = st
            sts.append(st.astype(BF16))
            st = st * ebl[c * HGRN_CHUNK:c * HGRN_CHUNK + 1, :] + kvs[c]
        state[...] = st
        o = o_intra + jnp.concatenate([_dot_nt(qd_b[r], sb) for r, sb in zip(chunks, sts)], axis=0)
        on = o * lax.rsqrt(jnp.mean(o * o, axis=-1, keepdims=True) + NORM_EPS) * gn_ref[...]
        zg = zg_ref[...]
        y_ref[...] = (on * (zg * _sigmoid(zg))).astype(BF16)

    def zspec(seg):
        return pl.BlockSpec((None, T, HGRN_HEAD), lambda h, j: (seg, j, h))

    return pl.pallas_call(
        body, grid=(H, NJ), name="hgrn_fwd",
        in_specs=[zspec(0), zspec(1), zspec(2), zspec(3),
                  pl.BlockSpec((2, HGRN_HEAD), lambda h, j: (0, h)),
                  pl.BlockSpec((1, HGRN_HEAD), lambda h, j: (0, 0))],
        out_specs=[pl.BlockSpec((T, HGRN_HEAD), lambda h, j: (j, h)),
                   pl.BlockSpec((NC, None, HGRN_HEAD, HGRN_HEAD), lambda h, j: (j, h, 0, 0))],
        out_shape=[jax.ShapeDtypeStruct((S, SEG), BF16),
                   jax.ShapeDtypeStruct((S // HGRN_CHUNK, H, HGRN_HEAD, HGRN_HEAD), F32)],
        scratch_shapes=[pltpu.VMEM((HGRN_HEAD, HGRN_HEAD), F32)],
        compiler_params=_cp(("parallel", "arbitrary")),
    )(zf32, zf32, zf32, zf32, lb_logits, gnorm)


def _hgrn_bwd(zf32, lb_logits, gnorm, states, dy):
    _, S, SEG = zf32.shape
    H, T, NC, NJ, tiles, chunks = _hgrn_dims(S, SEG)
    C = HGRN_CHUNK

    def body(zq_ref, zf_ref, zi_ref, zg_ref, lbl_ref, gn_ref, st_ref, dy_ref, dz_ref, dl_ref, dgn_ref, gstate):
        @pl.when(pl.program_id(1) == 0)
        def _():
            gstate[...] = jnp.zeros_like(gstate)
            dl_ref[...] = jnp.zeros_like(dl_ref)
            dgn_ref[...] = jnp.zeros_like(dgn_ref)

        lb = _lower_bound(lbl_ref[...])
        gn = gn_ref[...]
        tril, triu = _tile_masks()
        tril_bf = tril.astype(BF16)
        triu_bf = triu.astype(BF16)
        q, dq_dz = _silu_and_grad(zq_ref[...])
        sf = _sigmoid(zf_ref[...])
        f = lb + (1.0 - lb) * sf
        k = 1.0 - f
        logf = jnp.log(f)
        b = jnp.concatenate([_exact_dot(tril_bf, logf[t]) for t in tiles], axis=0)
        bl = _chunk_last(b)
        eb = jnp.exp(b)
        enb = jnp.exp(-b)
        ekl = jnp.exp(bl - b)
        ebl = jnp.exp(bl)
        qd = q * eb
        kd = k * enb
        ke = k * ekl
        qd_b = qd.astype(BF16)
        kd_b = kd.astype(BF16)
        ke_b = ke.astype(BF16)
        v_b = zi_ref[...].astype(BF16)
        sts = [st_ref[c] for c in range(NC)]
        sts_b = [s.astype(BF16) for s in sts]
        a_b = [jnp.where(tril, _dot_nt(qd_b[t], kd_b[t]), 0.0).astype(BF16) for t in tiles]
        o = (jnp.concatenate([_dot(a, v_b[t]) for a, t in zip(a_b, tiles)], axis=0)
             + jnp.concatenate([_dot_nt(qd_b[r], sb) for r, sb in zip(chunks, sts_b)], axis=0))
        rinv = lax.rsqrt(jnp.mean(o * o, axis=-1, keepdims=True) + NORM_EPS)
        ohat = o * rinv
        sg, dsg = _silu_and_grad(zg_ref[...])
        dyv = dy_ref[...]
        don = dyv * sg
        dz_ref[3] = (dyv * (ohat * gn) * dsg).astype(BF16)
        dgn_ref[...] += jnp.sum(don * ohat, axis=0, keepdims=True)
        dohat = don * gn
        do = rinv * (dohat - ohat * jnp.mean(dohat * ohat, axis=-1, keepdims=True))
        do_b = do.astype(BF16)
        da_b = [jnp.where(tril, _dot_nt(do_b[t], v_b[t]), 0.0).astype(BF16) for t in tiles]
        dv_intra = jnp.concatenate([_dot_tn(a, do_b[t]) for a, t in zip(a_b, tiles)], axis=0)
        dqd_intra = jnp.concatenate([_dot(da, kd_b[t]) for da, t in zip(da_b, tiles)], axis=0)
        dkd = jnp.concatenate([_dot_tn(da, qd_b[t]) for da, t in zip(da_b, tiles)], axis=0)
        dqd_inter = jnp.concatenate([_dot(do_b[r], sb) for r, sb in zip(chunks, sts_b)], axis=0)
        gks = [_dot_tn(do_b[r], qd_b[r]) for r in chunks]
        g = gstate[...]
        gs = [None] * NC
        for c in reversed(range(NC)):
            gs[c] = g
            g = g * ebl[c * C:c * C + 1, :] + gks[c]
        gstate[...] = g
        gs_b = [x.astype(BF16) for x in gs]
        dv = dv_intra + jnp.concatenate([_dot_nt(ke_b[r], gb) for r, gb in zip(chunks, gs_b)], axis=0)
        dz_ref[2] = dv.astype(BF16)
        dke = jnp.concatenate([_dot(v_b[r], gb) for r, gb in zip(chunks, gs_b)], axis=0)
        debl = jnp.concatenate(
            [jnp.broadcast_to(jnp.sum(x * s, axis=0, keepdims=True), (C, HGRN_HEAD)) for x, s in zip(gs, sts)], axis=0)
        dqd = dqd_intra + dqd_inter
        dz_ref[0] = ((dqd * eb) * dq_dz).astype(BF16)
        t_ke = dke * ke
        db = dqd * qd - dkd * kd - t_ke
        db_last = _chunk_sum(t_ke) + debl * ebl
        dk = dkd * enb + dke * ekl
        dlogf = jnp.concatenate([_exact_dot(triu_bf, db[t]) for t in tiles], axis=0) + db_last
        df = dlogf / f - dk
        dz_ref[1] = (df * (1.0 - lb) * (sf * (1.0 - sf))).astype(BF16)
        dlb = jnp.sum(df * (1.0 - sf), axis=0, keepdims=True)
        dl0 = dlb * lb * (1.0 - lb)
        dl_ref[0:1, :] += dl0
        dl_ref[1:2, :] -= dl0

    def zspec(seg):
        return pl.BlockSpec((None, T, HGRN_HEAD), lambda h, j: (seg, NJ - 1 - j, h))

    return pl.pallas_call(
        body, grid=(H, NJ), name="hgrn_bwd",
        in_specs=[zspec(0), zspec(1), zspec(2), zspec(3),
                  pl.BlockSpec((2, HGRN_HEAD), lambda h, j: (0, h)),
                  pl.BlockSpec((1, HGRN_HEAD), lambda h, j: (0, 0)),
                  pl.BlockSpec((NC, None, HGRN_HEAD, HGRN_HEAD), lambda h, j: (NJ - 1 - j, h, 0, 0)),
                  pl.BlockSpec((T, HGRN_HEAD), lambda h, j: (NJ - 1 - j, h))],
        out_specs=[pl.BlockSpec((4, T, HGRN_HEAD), lambda h, j: (0, NJ - 1 - j, h)),
                   pl.BlockSpec((2, HGRN_HEAD), lambda h, j: (0, h)),
                   pl.BlockSpec((None, 1, HGRN_HEAD), lambda h, j: (h, 0, 0))],
        out_shape=[jax.ShapeDtypeStruct((4, S, SEG), BF16), jax.ShapeDtypeStruct((2, SEG), F32),
                   jax.ShapeDtypeStruct((H, 1, HGRN_HEAD), F32)],
        scratch_shapes=[pltpu.VMEM((HGRN_HEAD, HGRN_HEAD), F32)],
        compiler_params=_cp(("parallel", "arbitrary")),
    )(zf32, zf32, zf32, zf32, lb_logits, gnorm, states, dy)


def _alibi_slopes(seg):
    n_heads = seg // ATTN_HEAD
    s = 2.0 ** (-8.0 * np.arange(1, n_heads + 1, dtype=np.float64) / n_heads)
    return jnp.asarray(np.repeat(s, ATTN_HEAD)[None, :], F32)


def _attn_dims(S, SEG, d):
    rb = BAND * d
    assert S % rb == 0 and SEG % LANES == 0
    cb = min(SEG, ATTN_BLOCK_ELEMS // rb) if d == 1 else LANES
    assert SEG % cb == 0
    return rb, cb, S // rb, SEG // cb


def _res_rows(r, d):
    return pl.ds(0, BAND) if d == 1 else pl.ds(r, BAND, stride=d)


def _for_residues(d, fn):
    if d == 1:
        fn(0)
    else:
        def step(r, carry):
            fn(r)
            return carry
        lax.fori_loop(0, d, step, 0, unroll=ATTN_UNROLL)


def _for_groups(d, n_pairs, fn):
    if d == 1:
        for g0 in range(0, n_pairs, ATTN_UNROLL):
            fn([(0, p) for p in range(g0, min(n_pairs, g0 + ATTN_UNROLL))])
    else:
        assert n_pairs == 1 and d % ATTN_UNROLL == 0

        def step(g, carry):
            fn([(g * ATTN_UNROLL + i, 0) for i in range(ATTN_UNROLL)])
            return carry
        lax.fori_loop(0, d // ATTN_UNROLL, step, 0)


def _band_terms(n, d):
    i = lax.broadcasted_iota(jnp.int32, (BAND, 2 * BAND), 0)
    jj = lax.broadcasted_iota(jnp.int32, (BAND, 2 * BAND), 1)
    delta = BAND + i - jj
    valid = (delta >= 0) & (delta <= BAND) & ((n > 0) | (jj >= BAND))
    return (-d * delta).astype(F32), valid


def _head_biases(slopes, nd, valid):
    out = []
    for s in _per_head(slopes):
        s2 = jnp.concatenate([s, s], axis=1)
        out.append(jnp.where(valid, s2 * nd, NEG))
    return jnp.concatenate(out, axis=0)


def _stack_heads(x):
    lane = lax.broadcasted_iota(jnp.int32, x.shape, 1)
    zero = jnp.zeros_like(x)
    return jnp.concatenate([jnp.where(lane < ATTN_HEAD, x, zero), jnp.where(lane < ATTN_HEAD, zero, x)], axis=0)


def _unstack_heads(x2):
    first = lax.broadcasted_iota(jnp.int32, (BAND, LANES), 1) < ATTN_HEAD
    return jnp.where(first, x2[:BAND], x2[BAND:])


def _stack_per_head(x):
    a, b = _per_head(x)
    col = jnp.concatenate([a, b], axis=0)
    return jnp.concatenate([col, col], axis=1)


def _per_head(x):
    lane = lax.broadcasted_iota(jnp.int32, x.shape, 1)
    sw = pltpu.roll(x, ATTN_HEAD, 1)
    first = lane < ATTN_HEAD
    return jnp.where(first, x, sw), jnp.where(first, sw, x)


def _attn_fwd(qkv, slopes, d):
    _, S, SEG = qkv.shape
    rb, cb, nb, ncb = _attn_dims(S, SEG, d)
    NP = cb // LANES

    def body(q_ref, kp_ref, kc_ref, vp_ref, vc_ref, sl_ref, o_ref, l_ref):
        n = pl.program_id(1)
        nd, valid = _band_terms(n, d)
        biases = [_head_biases(sl_ref[:, p * LANES:(p + 1) * LANES], nd, valid) for p in range(NP)]

        def group(items):
            at = [(_res_rows(r, d), slice(p * LANES, (p + 1) * LANES)) for r, p in items]
            scores, values = [], []
            for rows, cols in at:
                kc = jnp.concatenate([kp_ref[rows, cols], kc_ref[rows, cols]], axis=0).astype(BF16)
                values.append(jnp.concatenate([vp_ref[rows, cols], vc_ref[rows, cols]], axis=0).astype(BF16))
                scores.append(_dot_nt(_stack_heads((q_ref[rows, cols] * ATTN_SCALE).astype(BF16)), kc))
            probs = []
            for (r, p), s in zip(items, scores):
                s = s + biases[p]
                m = jnp.max(s, axis=-1, keepdims=True)
                e = jnp.exp(s - m)
                den = jnp.sum(e, axis=-1, keepdims=True)
                probs.append((e.astype(BF16), den, m + jnp.log(den)))
            for (rows, cols), vc, (e, den, lse) in zip(at, values, probs):
                o_ref[rows, cols] = _unstack_heads(_dot(e, vc) / den)
                l_ref[rows, cols] = _unstack_heads(jnp.broadcast_to(lse, (2 * BAND, LANES)))

        _for_groups(d, NP, group)

    def spec(seg, prev):
        if prev:
            return pl.BlockSpec((None, rb, cb), lambda c, n: (SEG_QKV + seg, jnp.maximum(n - 1, 0), c))
        return pl.BlockSpec((None, rb, cb), lambda c, n: (SEG_QKV + seg, n, c))

    out = pl.BlockSpec((rb, cb), lambda c, n: (n, c))
    return pl.pallas_call(
        body, grid=(ncb, nb), name=f"attn_fwd_d{d}",
        in_specs=[spec(0, False), spec(1, True), spec(1, False), spec(2, True), spec(2, False),
                  pl.BlockSpec((1, cb), lambda c, n: (0, c))],
        out_specs=[out, out],
        out_shape=[jax.ShapeDtypeStruct((S, SEG), F32), jax.ShapeDtypeStruct((S, SEG), F32)],
        compiler_params=_cp(("parallel", "parallel")),
    )(qkv, qkv, qkv, qkv, qkv, slopes)


def _attn_merge(outs, lses, zf32):
    S, SEG = outs[0].shape
    tm = min(256, S)

    def body(o1, o2, o3, l1, l2, l3, zg_ref, o_ref, lse_ref, y_ref):
        a, b, c = l1[...], l2[...], l3[...]
        m = jnp.maximum(jnp.maximum(a, b), c)
        ea, eb, ec = jnp.exp(a - m), jnp.exp(b - m), jnp.exp(c - m)
        tot = ea + eb + ec
        o = (ea / tot) * o1[...] + (eb / tot) * o2[...] + (ec / tot) * o3[...]
        o_ref[...] = o
        lse_ref[...] = m + jnp.log(tot)
        zg = zg_ref[...]
        y_ref[...] = (o * (zg * _sigmoid(zg))).astype(BF16)

    row = pl.BlockSpec((tm, SEG), lambda i: (i, 0))
    return pl.pallas_call(
        body, grid=(S // tm,), name="attn_merge",
        in_specs=[row] * 6 + [pl.BlockSpec((None, tm, SEG), lambda i: (SEG_GATE_A, i, 0))],
        out_specs=[row, row, row],
        out_shape=[jax.ShapeDtypeStruct((S, SEG), F32), jax.ShapeDtypeStruct((S, SEG), F32),
                   jax.ShapeDtypeStruct((S, SEG), BF16)],
        compiler_params=_cp(("parallel",)),
    )(*outs, *lses, zf32)


def _attn_gate_bwd(dy, o, zf32):
    S, SEG = o.shape
    tm = min(256, S)
    NP = SEG // LANES

    def body(dy_ref, o_ref, zg_ref, do_ref, dl_ref, dzg_ref):
        r = lax.broadcasted_iota(jnp.int32, (LANES, LANES), 0) // ATTN_HEAD
        c = lax.broadcasted_iota(jnp.int32, (LANES, LANES), 1) // ATTN_HEAD
        same_head = (r == c).astype(BF16)
        for p in range(NP):
            cols = slice(p * LANES, (p + 1) * LANES)
            sg, dsg = _silu_and_grad(zg_ref[:, cols])
            dyv = dy_ref[:, cols]
            ov = o_ref[:, cols]
            do = dyv * sg
            do_ref[:, cols] = do
            dzg_ref[:, cols] = (dyv * ov * dsg).astype(BF16)
            dl_ref[:, cols] = _exact_dot_right(do * ov, same_head)

    return pl.pallas_call(
        body, grid=(S // tm,), name="attn_gate_bwd",
        in_specs=[pl.BlockSpec((tm, SEG), lambda i: (i, 1)), pl.BlockSpec((tm, SEG), lambda i: (i, 0)),
                  pl.BlockSpec((None, tm, SEG), lambda i: (SEG_GATE_A, i, 0))],
        out_specs=[pl.BlockSpec((tm, SEG), lambda i: (i, 0)), pl.BlockSpec((tm, SEG), lambda i: (i, 0)),
                   pl.BlockSpec((None, tm, SEG), lambda i: (3, i, 0))],
        out_shape=[jax.ShapeDtypeStruct((S, SEG), F32), jax.ShapeDtypeStruct((S, SEG), F32),
                   jax.ShapeDtypeStruct((4, S, SEG), BF16)],
        compiler_params=_cp(("parallel",)),
    )(dy, o, zf32)


def _attn_bwd(qkv, slopes, do, lse, dl, d, acc, into):
    _, S, SEG = qkv.shape
    rb, cb, nb, ncb = _attn_dims(S, SEG, d)
    NP = cb // LANES
    has_acc = acc is not None
    out_dtype = F32 if into is None else into.dtype

    def body(*refs):
        q_ref, kp_ref, kc_ref, vp_ref, vc_ref, sl_ref, do_ref, lse_ref, dl_ref = refs[:9]
        acc_ref = refs[9] if has_acc else None
        out_ref, cq, ck, cv = refs[-4:]
        n = pl.program_id(1)

        def emit(r, rows, cols, dq, dk, dv):
            for t, val in enumerate((dq, dk, dv)):
                if has_acc:
                    val = val + acc_ref.at[t][rows, cols]
                out_ref.at[t][rows, cols] = val.astype(out_dtype)

        @pl.when(n == 0)
        def _():
            cq[...] = jnp.zeros_like(cq)
            ck[...] = jnp.zeros_like(ck)
            cv[...] = jnp.zeros_like(cv)

        @pl.when(n < nb)
        def _():
            nd, valid = _band_terms(n, d)
            biases = [_head_biases(sl_ref[:, p * LANES:(p + 1) * LANES], nd, valid) for p in range(NP)]

            def group(items):
                at = [(_res_rows(r, d), slice(p * LANES, (p + 1) * LANES)) for r, p in items]
                first = []
                for rows, cols in at:
                    kc = jnp.concatenate([kp_ref[rows, cols], kc_ref[rows, cols]], axis=0).astype(BF16)
                    vc = jnp.concatenate([vp_ref[rows, cols], vc_ref[rows, cols]], axis=0).astype(BF16)
                    qs = _stack_heads((q_ref[rows, cols] * ATTN_SCALE).astype(BF16))
                    dos = _stack_heads(do_ref[rows, cols].astype(BF16))
                    first.append((kc, qs, dos, _dot_nt(qs, kc), _dot_nt(dos, vc)))
                second = []
                for (r, p), (rows, cols), (kc, qs, dos, s, dp) in zip(items, at, first):
                    pr = jnp.exp(s + biases[p] - _stack_per_head(lse_ref[rows, cols]))
                    ds = (pr * (dp - _stack_per_head(dl_ref[rows, cols]))).astype(BF16)
                    second.append((kc, qs, dos, pr.astype(BF16), ds))
                for (r, p), (rows, cols), (kc, qs, dos, pr, ds) in zip(items, at, second):
                    dq = _unstack_heads(_dot(ds, kc)) * ATTN_SCALE
                    dk = _dot_tn(ds, qs)
                    dv = _dot_tn(pr, dos)
                    emit(r, rows, cols, cq[r, :, cols], ck[r, :, cols] + dk[:BAND, :], cv[r, :, cols] + dv[:BAND, :])
                    cq[r, :, cols] = dq
                    ck[r, :, cols] = dk[BAND:, :]
                    cv[r, :, cols] = dv[BAND:, :]

            _for_groups(d, NP, group)

        @pl.when(n == nb)
        def _():
            _for_residues(d, lambda r: emit(r, _res_rows(r, d), slice(None), cq[r], ck[r], cv[r]))

    cur2 = lambda c, n: (jnp.minimum(n, nb - 1), c)
    lag3 = lambda c, n: (0, jnp.clip(n - 1, 0, nb - 1), c)

    def spec(seg, prev):
        if prev:
            return pl.BlockSpec((None, rb, cb), lambda c, n: (SEG_QKV + seg, jnp.clip(n - 1, 0, nb - 1), c))
        return pl.BlockSpec((None, rb, cb), lambda c, n: (SEG_QKV + seg, jnp.minimum(n, nb - 1), c))

    in_specs = [spec(0, False), spec(1, True), spec(1, False), spec(2, True), spec(2, False),
                pl.BlockSpec((1, cb), lambda c, n: (0, c)),
                pl.BlockSpec((rb, cb), cur2), pl.BlockSpec((rb, cb), cur2), pl.BlockSpec((rb, cb), cur2)]
    args = [qkv, qkv, qkv, qkv, qkv, slopes, do, lse, dl]
    aliases = {}
    if has_acc:
        in_specs.append(pl.BlockSpec((3, rb, cb), lag3))
        args.append(acc)
        if into is None:
            aliases = {9: 0}
    if into is not None:
        in_specs.append(ANY)
        args.append(into)
        aliases = {len(args) - 1: 0}
    out_sds = jax.ShapeDtypeStruct((3, S, SEG), F32) if into is None else jax.ShapeDtypeStruct(into.shape, into.dtype)
    return pl.pallas_call(
        body, grid=(ncb, nb + 1), name=f"attn_bwd_d{d}",
        in_specs=in_specs, out_specs=pl.BlockSpec((3, rb, cb), lag3), out_shape=out_sds,
        scratch_shapes=[pltpu.VMEM((d, BAND, cb), F32)] * 3,
        input_output_aliases=aliases,
        compiler_params=_cp(("parallel", "arbitrary")),
    )(*args)


def _adamw(w, g, m, v, name):
    R, C = w.shape
    tr = R if R <= 256 else 256
    assert R % tr == 0

    def body(w_ref, g_ref, m_ref, v_ref, d_ref, nm_ref, nv_ref):
        g = g_ref[...]
        nm = ADAM_B1 * m_ref[...] + (1.0 - ADAM_B1) * g
        nv = ADAM_B2 * v_ref[...] + (1.0 - ADAM_B2) * (g * g)
        m_hat = nm / (1.0 - ADAM_B1 ** ADAM_STEP)
        v_hat = nv / (1.0 - ADAM_B2 ** ADAM_STEP)
        d_ref[...] = -ADAM_LR * (m_hat / (jnp.sqrt(v_hat) + ADAM_EPS) + ADAM_WD * w_ref[...])
        nm_ref[...] = nm
        nv_ref[...] = nv

    blk = pl.BlockSpec((tr, C), lambda i: (i, 0))
    sds = jax.ShapeDtypeStruct((R, C), F32)
    return pl.pallas_call(
        body, grid=(R // tr,), name=name, in_specs=[blk] * 4, out_specs=[blk] * 3, out_shape=[sds] * 3,
        compiler_params=_cp(("parallel",)),
    )(w, g, m, v)


def _coords():
    return lax.axis_index("x"), lax.axis_index("y"), lax.axis_index("c")


def _other_chips(x, y):
    return [(1 - x, y), (x, 1 - y), (1 - x, 1 - y)]


ANY = pl.BlockSpec(memory_space=pl.ANY)


def _cast_into_slot(w, where, name):
    R, C = w.shape
    tr = min(256, R)

    def body(where_ref, w_ref, o_ref):
        o_ref[...] = w_ref[...].astype(BF16)

    grid_spec = pltpu.PrefetchScalarGridSpec(
        num_scalar_prefetch=1, grid=(R // tr,),
        in_specs=[pl.BlockSpec((tr, C), lambda i, w: (i, 0))],
        out_specs=pl.BlockSpec((None, tr, C), lambda i, w: (w[1], i, 0)))
    return pl.pallas_call(
        body, grid_spec=grid_spec, name=name, out_shape=jax.ShapeDtypeStruct((4, R, C), BF16),
        compiler_params=_cp(("parallel",)),
    )(where, w)


def _pair_sum(g, sib, where, name):
    _, n2, C = g.shape
    N = n2 // 2
    tr = min(256, N)
    nt = N // tr

    def body(where_ref, g_ref, s_ref, qb_ref, own_ref):
        q = pl.program_id(1)
        tot = g_ref[...] + s_ref[...]
        qb_ref[...] = tot.astype(BF16)

        @pl.when(q == where_ref[1])
        def _():
            own_ref[...] = tot

    grid_spec = pltpu.PrefetchScalarGridSpec(
        num_scalar_prefetch=1, grid=(nt, 4),
        in_specs=[pl.BlockSpec((None, tr, C), lambda i, q, w: (q, w[0] * nt + i, 0)),
                  pl.BlockSpec((None, tr, C), lambda i, q, w: (q, i, 0))],
        out_specs=[pl.BlockSpec((None, tr, C), lambda i, q, w: (q, i, 0)),
                   pl.BlockSpec((tr, C), lambda i, q, w: (i, 0))])
    return pl.pallas_call(
        body, grid_spec=grid_spec, name=name,
        out_shape=[jax.ShapeDtypeStruct((4, N, C), BF16), jax.ShapeDtypeStruct((N, C), F32)],
        compiler_params=_cp(("parallel", "arbitrary")),
    )(where, g, sib)


HBM = pl.BlockSpec(memory_space=pltpu.HBM)
SEM = pl.BlockSpec(memory_space=pltpu.SEMAPHORE)


def _in_hbm(a):
    return pltpu.with_memory_space_constraint(a, pltpu.HBM)


def _split_start(name, copies, arrays, n_sems, after=None):
    n = len(arrays)

    def body(*refs):
        for cp in copies(refs[:n], refs[-n - 3], refs[-n - 2]):
            cp.start()
        refs[-1][...] = jnp.zeros_like(refs[-1])

    ordered = () if after is None else (after,)
    outs = pl.pallas_call(
        body, name=name,
        out_shape=(pltpu.SemaphoreType.DMA((n_sems,)), pltpu.SemaphoreType.DMA((n_sems,)),
                   *[pltpu.HBM(a.shape, a.dtype) for a in arrays], jax.ShapeDtypeStruct((8, LANES), F32)),
        in_specs=(HBM,) * n + (ANY,) * len(ordered),
        out_specs=(SEM, SEM) + (HBM,) * n + (pl.BlockSpec(memory_space=pltpu.VMEM),),
        input_output_aliases={i: 2 + i for i in range(n)},
        compiler_params=pltpu.CompilerParams(has_side_effects=pltpu.SideEffectType.DATAFLOW_SIDE_EFFECTING),
    )(*[_in_hbm(a) for a in arrays], *ordered)
    return outs[0], outs[1], list(outs[2:2 + n]), outs[-1]


def _split_wait(name, copies, send_sems, recv_sems, arrays, after):
    n = len(arrays)

    def body(*refs):
        for cp in copies(refs[:n], refs[n], refs[n + 1]):
            cp.wait_send()
            cp.wait_recv()

    outs = pl.pallas_call(
        body, name=name,
        out_shape=tuple(pltpu.HBM(a.shape, a.dtype) for a in arrays),
        in_specs=(HBM,) * n + (SEM, SEM, ANY), out_specs=(HBM,) * n,
        input_output_aliases={i: i for i in range(n)},
        compiler_params=pltpu.CompilerParams(has_side_effects=pltpu.SideEffectType.DATAFLOW_SIDE_EFFECTING),
    )(*arrays, send_sems, recv_sems, after)
    return list(outs)


def _remote(src, dst, sems, k, to):
    send_sems, recv_sems = sems
    return pltpu.make_async_remote_copy(src_ref=src, dst_ref=dst, send_sem=send_sems.at[k], recv_sem=recv_sems.at[k],
                                        device_id=to, device_id_type=MESH)


def _gather_in_copies(refs, send_sems, recv_sems):
    (w,) = refs
    x, y, c = _coords()
    seg = w.shape[2] // 2
    mine = w.at[2 * x + y, :, pl.ds(c * seg, seg)]
    return [_remote(mine, mine, (send_sems, recv_sems), k, (px, py, c)) for k, (px, py) in enumerate(_other_chips(x, y))]


def _gather_out_copies(refs, send_sems, recv_sems):
    (w,) = refs
    x, y, c = _coords()
    mine = w.at[2 * x + y]
    return [_remote(mine, mine, (send_sems, recv_sems), k, (px, py, c)) for k, (px, py) in enumerate(_other_chips(x, y))]


def _swap_copies(refs, send_sems, recv_sems):
    gi, go, si, so = refs
    x, y, c = _coords()
    cps = []
    for a, (src, dst) in enumerate(((gi, si), (go, so))):
        nr = dst.shape[1]
        cps.append(_remote(src.at[:, pl.ds((1 - c) * nr, nr), :], dst, (send_sems, recv_sems), a, (x, y, 1 - c)))
    return cps


def _scatter_copies(refs, send_sems, recv_sems):
    qi, qo, ri, ro = refs
    x, y, c = _coords()
    cps = []
    for k, (px, py) in enumerate(_other_chips(x, y)):
        for a, (src, dst) in enumerate(((qi, ri), (qo, ro))):
            cps.append(_remote(src.at[2 * px + py], dst.at[k], (send_sems, recv_sems), 2 * k + a, (px, py, c)))
    return cps


def _forward_copies(refs, send_sems, recv_sems):
    (w,) = refs
    x, y, c = _coords()
    seg = w.shape[2] // 2
    cps = []
    for k, (px, py) in enumerate(_other_chips(x, y)):
        got = w.at[2 * px + py, :, pl.ds(c * seg, seg)]
        cps.append(_remote(got, got, (send_sems, recv_sems), k, (x, y, 1 - c)))
    return cps


def _chip_sum(own, got, where, name):
    N, C = own.shape
    tr = min(256, N)
    nt = N // tr

    def body(where_ref, own_ref, got_ref, o_ref):
        t = own_ref[...]
        for k in range(3):
            t = t + got_ref[k].astype(F32)
        o_ref[...] = t

    grid_spec = pltpu.PrefetchScalarGridSpec(
        num_scalar_prefetch=1, grid=(nt,),
        in_specs=[pl.BlockSpec((tr, C), lambda i, w: (i, 0)), pl.BlockSpec((3, tr, C), lambda i, w: (0, i, 0))],
        out_specs=pl.BlockSpec((tr, C), lambda i, w: (w[0] * nt + i, 0)))
    return pl.pallas_call(
        body, grid_spec=grid_spec, name=name, out_shape=jax.ShapeDtypeStruct((2 * N, C), F32),
        compiler_params=_cp(("parallel",)),
    )(where, own, got)


def _join_halves(gi, go):
    def body(gi_in, go_in, gi_ref, go_ref, send_sems, recv_sems):
        x, y, c = _coords()
        cps = []
        for a, ref in enumerate((gi_ref, go_ref)):
            nr = ref.shape[0] // 2
            mine = ref.at[pl.ds(c * nr, nr), :]
            cp = pltpu.make_async_remote_copy(src_ref=mine, dst_ref=mine, send_sem=send_sems.at[a],
                                              recv_sem=recv_sems.at[a], device_id=(x, y, 1 - c), device_id_type=MESH)
            cp.start()
            cps.append(cp)
        for a, ref in enumerate((gi_ref, go_ref)):
            nr = ref.shape[0] // 2
            theirs = ref.at[pl.ds((1 - c) * nr, nr), :]
            pltpu.make_async_remote_copy(src_ref=theirs, dst_ref=theirs, send_sem=send_sems.at[a],
                                         recv_sem=recv_sems.at[a], device_id=(x, y, 1 - c),
                                         device_id_type=MESH).wait_recv()
        for cp in cps:
            cp.wait_send()

    return pl.pallas_call(
        body, name="join_halves", in_specs=[ANY, ANY], out_specs=[ANY, ANY],
        out_shape=[jax.ShapeDtypeStruct(gi.shape, F32), jax.ShapeDtypeStruct(go.shape, F32)],
        scratch_shapes=[pltpu.SemaphoreType.DMA((2,)), pltpu.SemaphoreType.DMA((2,))],
        input_output_aliases={0: 0, 1: 1},
    )(gi, go)


def _all_reduce_small(part):
    R, C = part.shape

    def body(p_ref, o_ref, slots, send_sems, recv_sems):
        x, y, c = _coords()
        me = 4 * x + 2 * y + c
        slots[me] = p_ref[...]
        cps = []
        for k in range(1, 8):
            fx, fy, fc = (k >> 2) & 1, (k >> 1) & 1, k & 1
            peer = (1 - x if fx else x, 1 - y if fy else y, 1 - c if fc else c)
            cp = pltpu.make_async_remote_copy(src_ref=p_ref, dst_ref=slots.at[me], send_sem=send_sems.at[k - 1],
                                              recv_sem=recv_sems.at[k - 1], device_id=peer, device_id_type=MESH)
            cp.start()
            cps.append(cp)
        for cp in cps:
            cp.wait()
        t = slots[0]
        for k in range(1, 8):
            t = t + slots[k]
        o_ref[...] = t

    vm = pl.BlockSpec(memory_space=pltpu.VMEM)
    return pl.pallas_call(
        body, name="all_reduce_small", in_specs=[vm], out_specs=vm,
        out_shape=jax.ShapeDtypeStruct((R, C), F32),
        scratch_shapes=[pltpu.VMEM((8, R, C), F32), pltpu.SemaphoreType.DMA((7,)), pltpu.SemaphoreType.DMA((7,))],
    )(part)


def _mixers_forward(z, lb_logits, hgrn_gnorm):
    slopes = _alibi_slopes(z.shape[2])
    yh, states = _hgrn_fwd(z, lb_logits, hgrn_gnorm)
    outs, lses = [], []
    for d in DILATIONS:
        o, l = _attn_fwd(z, slopes, d)
        outs.append(o)
        lses.append(l)
    o_attn, lse, ya = _attn_merge(outs, lses, z)
    return yh, ya, (states, o_attn, lse, slopes)


def _backward_to_dz(z, kept, lb_logits, hgrn_gnorm, yh, ya, w_out_all, x2, tgt, fgain, h):
    states, o_attn, lse, slopes = kept
    dout, doutb, loss, dfg = _out_proj_loss(yh, ya, w_out_all, x2, tgt, fgain)
    dy = _dy_proj(doutb, w_out_all)
    g_w_out = _grad_w_out(yh, ya, doutb)
    dzh, dlogits, dgn = _hgrn_bwd(z, lb_logits, hgrn_gnorm, states, dy)
    do, dl, dza = _attn_gate_bwd(dy, o_attn, z)
    acc = None
    order = sorted(DILATIONS, reverse=True)
    for d in order[:-1]:
        acc = _attn_bwd(z, slopes, do, lse, dl, d, acc, None)
    dza = _attn_bwd(z, slopes, do, lse, dl, order[-1], acc, dza)
    sources = [dzh, dza]
    g_w_in = _grad_w_in(h, sources)
    return loss, dfg, dlogits, dgn, g_w_out, g_w_in, sources, dout


def _grad_x_half(sources, w_all, x2, rinv, norm_gain, dout, token, part, gx_prev):
    dh = _dh_proj(sources, w_all, token, part, f"dh_proj_{part}")
    return _rms_bwd(dh, x2, rinv, norm_gain, dout, part, gx_prev, f"rms_bwd_{part}")


def _local_step(x2, tgt, norm_gain, w_all, lb_logits, hgrn_gnorm, w_out_all, fgain):
    token = jnp.zeros((8, LANES), F32)
    where = jnp.zeros((2,), jnp.int32)
    h, rinv = _rms_fwd(x2, norm_gain, token)
    z = _in_proj(h, w_all, where, (0, 1, 8, None), None, token, "in_proj_all")
    yh, ya, kept = _mixers_forward(z, lb_logits, hgrn_gnorm)
    loss, dfg, dlogits, dgn, g_w_out, g_w_in, sources, dout = _backward_to_dz(
        z, kept, lb_logits, hgrn_gnorm, yh, ya, w_out_all, x2, tgt, fgain, h)
    gx, dg0 = _grad_x_half(sources, w_all, x2, rinv, norm_gain, dout, token, 0, None)
    gx, dg1 = _grad_x_half(sources, w_all, x2, rinv, norm_gain, dout, token, 1, gx)
    return loss, gx, dg0 + dg1, g_w_in, dlogits, dgn, g_w_out, dfg


def _pack_small(D, loss, dgain, dlogits, dgn, dfg):
    def row(v):
        v = v.reshape(1, -1)
        return jnp.pad(v, ((0, 0), (0, D - v.shape[1])))
    rows = [row(dgain), row(dfg), row(dlogits[0]), row(dlogits[1]), row(jnp.sum(dgn, axis=0)), row(loss)]
    rows += [jnp.zeros((1, D), F32)] * (8 - len(rows))
    return jnp.concatenate(rows, axis=0)


def kernel(x, norm_gain, w_in, lb_logits, hgrn_gnorm, w_out, final_gain, loss_target, m_norm_gain, m_w_in, m_lb_logits, m_hgrn_gnorm, m_w_out, m_final_gain, v_norm_gain, v_w_in, v_lb_logits, v_hgrn_gnorm, v_w_out, v_final_gain):
    _, S, D = x.shape
    SEG = w_in.shape[2] // 2
    x2 = x[0]
    tgt = loss_target[0]
    fgain = final_gain.reshape(1, D)
    where = jnp.stack([lax.axis_index("c"), 2 * lax.axis_index("x") + lax.axis_index("y")]).astype(jnp.int32)

    wia = _cast_into_slot(w_in[0], where, "cast_w_in")
    woa = _cast_into_slot(w_out[0], where, "cast_w_out")
    sems = _split_start("gather_in_start", _gather_in_copies, [wia], 3)
    (wia,), token = sems[2], sems[3]
    h, rinv = _rms_fwd(x2, norm_gain, token)
    z = _in_proj(h, wia, where, (0, 1, 2, None), None, token, "in_proj_own")
    (wia,) = _split_wait("gather_in_wait", _gather_in_copies, sems[0], sems[1], [wia], z)
    out_sems = _split_start("gather_out_start", _gather_out_copies, [woa], 3, after=wia)
    sems = _split_start("forward_start", _forward_copies, [wia], 3, after=out_sems[3])
    z = _in_proj(h, sems[2][0], where, (2, 2, 3, 0), z, sems[3], "in_proj_received")
    (wia,) = _split_wait("forward_wait", _forward_copies, sems[0], sems[1], sems[2], z)
    z = _in_proj(h, wia, where, (2, 2, 3, 1), z, token, "in_proj_forwarded")
    yh, ya, kept = _mixers_forward(z, lb_logits, hgrn_gnorm)
    (woa,) = _split_wait("gather_out_wait", _gather_out_copies, out_sems[0], out_sems[1], out_sems[2], ya)
    w_out_all = woa.reshape(2 * SEG, D)

    loss, dfg, dlogits, dgn, g_w_out, g_w_in, sources, dout = _backward_to_dz(
        z, kept, lb_logits, hgrn_gnorm, yh, ya, w_out_all, x2, tgt, fgain, h)

    sib_i = lax.empty((4, g_w_in.shape[1] // 2, g_w_in.shape[2]), F32)
    sib_o = lax.empty((4, g_w_out.shape[1] // 2, g_w_out.shape[2]), F32)
    sems = _split_start("swap_start", _swap_copies, [g_w_in, g_w_out, sib_i, sib_o], 2)
    grad_x, dg0 = _grad_x_half(sources, wia, x2, rinv, norm_gain, dout, sems[3], 0, None)
    g_w_in, g_w_out, sib_i, sib_o = _split_wait("swap_wait", _swap_copies, sems[0], sems[1], sems[2], grad_x)
    qi, own_i = _pair_sum(g_w_in, sib_i, where, "pair_sum_w_in")
    qo, own_o = _pair_sum(g_w_out, sib_o, where, "pair_sum_w_out")
    ri = lax.empty((3,) + qi.shape[1:], BF16)
    ro = lax.empty((3,) + qo.shape[1:], BF16)
    sems = _split_start("scatter_start", _scatter_copies, [qi, qo, ri, ro], 6)
    grad_x, dg1 = _grad_x_half(sources, wia, x2, rinv, norm_gain, dout, sems[3], 1, grad_x)
    _, _, got_i, got_o = _split_wait("scatter_wait", _scatter_copies, sems[0], sems[1], sems[2], grad_x)
    grad_w_in, grad_w_out = _join_halves(_chip_sum(own_i, got_i, where, "chip_sum_w_in"),
                                         _chip_sum(own_o, got_o, where, "chip_sum_w_out"))

    small = _all_reduce_small(_pack_small(D, loss, dg0 + dg1, dlogits, dgn, dfg))
    grad_norm_gain = small[0:1, :]
    grad_final_gain = small[1:2, :]
    grad_lb_logits = small[2:4, :SEG]
    grad_hgrn_gnorm = small[4:5, :HGRN_HEAD]
    loss_sum = small[5, 0]

    d_ng, m_ng, v_ng = _adamw(norm_gain, grad_norm_gain, m_norm_gain, v_norm_gain, "adamw_norm_gain")
    d_wi, m_wi, v_wi = _adamw(w_in[0], grad_w_in, m_w_in[0], v_w_in[0], "adamw_w_in")
    d_lb, m_lb, v_lb = _adamw(lb_logits, grad_lb_logits, m_lb_logits, v_lb_logits, "adamw_lb_logits")
    d_gn, m_gn, v_gn = _adamw(hgrn_gnorm, grad_hgrn_gnorm, m_hgrn_gnorm, v_hgrn_gnorm, "adamw_hgrn_gnorm")
    d_wo, m_wo, v_wo = _adamw(w_out[0], grad_w_out, m_w_out[0], v_w_out[0], "adamw_w_out")
    d_fg, m_fg, v_fg = _adamw(fgain, grad_final_gain, m_final_gain.reshape(1, D), v_final_gain.reshape(1, D),
                              "adamw_final_gain")

    return (loss_sum, grad_x[None],
            grad_norm_gain, grad_w_in[None], grad_lb_logits, grad_hgrn_gnorm, grad_w_out[None], grad_final_gain[0],
            d_ng, d_wi[None], d_lb, d_gn, d_wo[None], d_fg[0],
            m_ng, m_wi[None], m_lb, m_gn, m_wo[None], m_fg[0],
            v_ng, v_wi[None], v_lb, v_gn, v_wo[None], v_fg[0])
```

```python
import jax
import jax.numpy as jnp
import numpy as np
from jax import lax
from jax.experimental import pallas as pl
from jax.experimental.pallas import tpu as pltpu

F32 = jnp.float32
BF16 = jnp.bfloat16
MESH = pl.DeviceIdType.MESH

NORM_EPS = 1e-6
HGRN_HEAD = 128
HGRN_CHUNK = 64
HGRN_TILE = 128
HGRN_BLOCK = 512
ATTN_HEAD = 64
LANES = 128
BAND = 128
DILATIONS = (1, 4, 16)
ATTN_SCALE = ATTN_HEAD ** -0.5
assert ATTN_SCALE == 0.125
ATTN_BLOCK_ELEMS = BAND * 2048
ATTN_UNROLL = 4
SEG_QKV = 4
SEG_GATE_A = 7
NEG = -1e30

ADAM_LR = 0.001
ADAM_B1 = 0.9
ADAM_B2 = 0.999
ADAM_EPS = 1e-08
ADAM_WD = 0.01
ADAM_STEP = 10

MIB = 1024 * 1024


def _cp(semantics=None, vmem_mib=48):
    return pltpu.CompilerParams(dimension_semantics=semantics, vmem_limit_bytes=vmem_mib * MIB)


def _dot(a, b):
    return jnp.dot(a, b, preferred_element_type=F32)


def _dot_nt(a, b):
    return lax.dot_general(a, b, (((1,), (1,)), ((), ())), preferred_element_type=F32)


def _dot_tn(a, b):
    return lax.dot_general(a, b, (((0,), (0,)), ((), ())), preferred_element_type=F32)


def _split3(x):
    hi = x.astype(BF16)
    r1 = x - hi.astype(F32)
    mid = r1.astype(BF16)
    lo = (r1 - mid.astype(F32)).astype(BF16)
    return hi, mid, lo


def _exact_dot(t_bf16, x):
    hi, mid, lo = _split3(x)
    return _dot(t_bf16, hi) + _dot(t_bf16, mid) + _dot(t_bf16, lo)


def _exact_dot_right(x, t_bf16):
    hi, mid, lo = _split3(x)
    return _dot(hi, t_bf16) + _dot(mid, t_bf16) + _dot(lo, t_bf16)


def _sigmoid(z):
    return jax.nn.sigmoid(z)


def _silu_and_grad(z):
    s = _sigmoid(z)
    return z * s, s * (1.0 + z * (1.0 - s))


def _seg_select(j, values):
    out = values[0]
    for t, v in enumerate(values[1:], 1):
        out = jnp.where(j == t, v, out)
    return out


def _rms_fwd(x2, gain, token):
    S, D = x2.shape
    tm = min(512, S)

    def body(x_ref, g_ref, _, h_ref, r_ref):
        x = x_ref[...]
        r = lax.rsqrt(jnp.mean(x * x, axis=-1, keepdims=True) + NORM_EPS)
        h_ref[...] = ((x * r) * g_ref[...]).astype(BF16)
        r_ref[...] = r

    return pl.pallas_call(
        body, grid=(S // tm,), name="rms_fwd",
        in_specs=[pl.BlockSpec((tm, D), lambda i: (i, 0)), pl.BlockSpec((1, D), lambda i: (0, 0)),
                  pl.BlockSpec(token.shape, lambda i: (0, 0))],
        out_specs=[pl.BlockSpec((tm, D), lambda i: (i, 0)), pl.BlockSpec((tm, 1), lambda i: (i, 0))],
        out_shape=[jax.ShapeDtypeStruct((S, D), BF16), jax.ShapeDtypeStruct((S, 1), F32)],
        compiler_params=_cp(("parallel",)),
    )(x2, gain, token)


def _in_proj(h, w_all, where, segs, z_prev, token, name):
    S, D = h.shape
    SEG = w_all.shape[2] // 2
    NLB = SEG // LANES
    tm = min(512, S)
    first, step, count, core = segs

    def body(*refs):
        h_ref, w_ref, o_ref = refs[1], refs[2], refs[-1]
        res = _dot(h_ref[...], w_ref[...])
        for p in range(NLB):
            o_ref[p] = res[:, p * LANES:(p + 1) * LANES]

    def seg_of(j, w):
        off = 0 if core is None else (w[0] if core == 0 else 1 - w[0])
        return (2 * w[1] + first + step * j + off) % 8

    in_specs = [pl.BlockSpec((tm, D), lambda j, i, w: (i, 0)),
                pl.BlockSpec((None, D, SEG), lambda j, i, w: (seg_of(j, w) // 2, 0, seg_of(j, w) % 2)),
                pl.BlockSpec(token.shape, lambda j, i, w: (0, 0))]
    args = [where, h, w_all, token]
    aliases = {}
    if z_prev is not None:
        in_specs.append(ANY)
        args.append(z_prev)
        aliases = {4: 0}
    grid_spec = pltpu.PrefetchScalarGridSpec(
        num_scalar_prefetch=1, grid=(count, S // tm), in_specs=in_specs,
        out_specs=pl.BlockSpec((None, NLB, tm, LANES), lambda j, i, w: (seg_of(j, w), 0, i, 0)))
    return pl.pallas_call(
        body, grid_spec=grid_spec, name=name, out_shape=jax.ShapeDtypeStruct((8, NLB, S, LANES), F32),
        input_output_aliases=aliases, compiler_params=_cp(("parallel", "parallel")),
    )(*args)


def _out_proj_loss(yh, ya, w_out, x2, tgt, fgain):
    S, D = x2.shape
    SEG = yh.shape[1]
    tm = min(256, S)
    parts = 2

    def body(yh_ref, ya_ref, w_ref, x_ref, t_ref, fg_ref, dout_ref, doutb_ref, loss_ref, dfg_ref):
        i = pl.program_id(0)

        @pl.when(i == 0)
        def _():
            loss_ref[...] = jnp.zeros_like(loss_ref)
            dfg_ref[...] = jnp.zeros_like(dfg_ref)

        fg = fg_ref[...]
        loss = jnp.zeros((1, 1), F32)
        dfg = jnp.zeros((1, D), F32)
        for rows in [pl.ds(p * (tm // parts), tm // parts) for p in range(parts)]:
            out = (x_ref[rows, :] + _dot(yh_ref[rows, :], w_ref[pl.ds(0, SEG), :])
                   + _dot(ya_ref[rows, :], w_ref[pl.ds(SEG, SEG), :]))
            r = lax.rsqrt(jnp.mean(out * out, axis=-1, keepdims=True) + NORM_EPS)
            n = out * r
            err = n * fg - t_ref[rows, :]
            loss = loss + 0.5 * jnp.sum(jnp.mean(err * err, axis=-1, keepdims=True), axis=0, keepdims=True)
            dy = err * (1.0 / D)
            dfg = dfg + jnp.sum(dy * n, axis=0, keepdims=True)
            dn = dy * fg
            dout = r * (dn - n * jnp.mean(dn * n, axis=-1, keepdims=True))
            dout_ref[rows, :] = dout
            doutb_ref[rows, :] = dout.astype(BF16)
        loss_ref[...] += loss
        dfg_ref[...] += dfg

    row = lambda i: (i, 0)
    fix = lambda i: (0, 0)
    return pl.pallas_call(
        body, grid=(S // tm,), name="out_proj_loss",
        in_specs=[pl.BlockSpec((tm, SEG), row), pl.BlockSpec((tm, SEG), row), pl.BlockSpec((2 * SEG, D), fix),
                  pl.BlockSpec((tm, D), row), pl.BlockSpec((tm, D), row), pl.BlockSpec((1, D), fix)],
        out_specs=[pl.BlockSpec((tm, D), row), pl.BlockSpec((tm, D), row), pl.BlockSpec((1, 1), fix),
                   pl.BlockSpec((1, D), fix)],
        out_shape=[jax.ShapeDtypeStruct((S, D), F32), jax.ShapeDtypeStruct((S, D), BF16),
                   jax.ShapeDtypeStruct((1, 1), F32), jax.ShapeDtypeStruct((1, D), F32)],
        compiler_params=_cp(("arbitrary",)),
    )(yh, ya, w_out, x2, tgt, fgain)


def _dy_proj(doutb, w_out):
    S, D = doutb.shape
    K = w_out.shape[0]
    tm = min(512, S)

    def body(d_ref, w_ref, o_ref):
        o_ref[...] = _dot_nt(d_ref[...], w_ref[...])

    return pl.pallas_call(
        body, grid=(S // tm,), name="dy_proj",
        in_specs=[pl.BlockSpec((tm, D), lambda i: (i, 0)), pl.BlockSpec((K, D), lambda i: (0, 0))],
        out_specs=pl.BlockSpec((tm, K), lambda i: (i, 0)),
        out_shape=jax.ShapeDtypeStruct((S, K), F32),
        compiler_params=_cp(("parallel",)),
    )(doutb, w_out)


def _grad_w_out(yh, ya, doutb):
    S, SEG = yh.shape
    D = doutb.shape[1]
    R = (2 * SEG) // 4
    nb_half = SEG // R
    tk = min(512, S)

    def body(yh_ref, ya_ref, d_ref, o_ref):
        q = pl.program_id(0)
        k = pl.program_id(1)

        @pl.when(k == 0)
        def _():
            o_ref[...] = jnp.zeros_like(o_ref)

        @pl.when(q < nb_half)
        def _():
            o_ref[...] += _dot_tn(yh_ref[...], d_ref[...])

        @pl.when(q >= nb_half)
        def _():
            o_ref[...] += _dot_tn(ya_ref[...], d_ref[...])

    return pl.pallas_call(
        body, grid=(4, S // tk), name="grad_w_out",
        in_specs=[pl.BlockSpec((tk, R), lambda q, k: (k, jnp.minimum(q, nb_half - 1))),
                  pl.BlockSpec((tk, R), lambda q, k: (k, jnp.maximum(q - nb_half, 0))),
                  pl.BlockSpec((tk, D), lambda q, k: (k, 0))],
        out_specs=pl.BlockSpec((None, R, D), lambda q, k: (q, 0, 0)),
        out_shape=jax.ShapeDtypeStruct((4, R, D), F32),
        compiler_params=_cp(("parallel", "arbitrary")),
    )(yh, ya, doutb)


def _dz_sources(sources):
    counts = [s.shape[0] for s in sources]
    starts = [sum(counts[:k]) for k in range(len(counts))]
    assert sum(counts) == 8
    return counts, starts


def _dh_proj(sources, w_all, token, part, name):
    S = sources[0].shape[1]
    D = w_all.shape[1]
    SEG = w_all.shape[2] // 2
    counts, starts = _dz_sources(sources)
    assert all(c % 2 == 0 for c in counts)
    ns = len(sources)
    tm = min(512, S // 2)
    nt = (S // 2) // tm
    t0 = part * nt

    def body(*refs):
        src = refs[:ns]
        w_ref, _, o_ref = refs[ns:]
        j = pl.program_id(1)

        @pl.when(j == 0)
        def _():
            o_ref[...] = jnp.zeros_like(o_ref)

        for k in range(ns):
            @pl.when((2 * j >= starts[k]) & (2 * j < starts[k] + counts[k]))
            def _(k=k):
                o_ref[...] += (_dot_nt(src[k][0], w_ref[:, pl.ds(0, SEG)])
                               + _dot_nt(src[k][1], w_ref[:, pl.ds(SEG, SEG)]))

    def src_spec(k):
        return pl.BlockSpec((2, tm, SEG),
                            lambda i, j: (jnp.clip(j - starts[k] // 2, 0, counts[k] // 2 - 1), t0 + i, 0))

    return pl.pallas_call(
        body, grid=(nt, 4), name=name,
        in_specs=[src_spec(k) for k in range(ns)] + [pl.BlockSpec((None, D, 2 * SEG), lambda i, j: (j, 0, 0)),
                                                     pl.BlockSpec(token.shape, lambda i, j: (0, 0))],
        out_specs=pl.BlockSpec((tm, D), lambda i, j: (i, 0)),
        out_shape=jax.ShapeDtypeStruct((S // 2, D), F32),
        compiler_params=_cp(("parallel", "arbitrary")),
    )(*sources, w_all, token)


def _rms_bwd(dh, x2, rinv, gain, dout, part, gx_prev, name):
    S, D = x2.shape
    tm = min(256, S // 2)
    nt = (S // 2) // tm
    t0 = part * nt

    def body(dh_ref, x_ref, r_ref, g_ref, dout_ref, *rest):
        gx_ref, dg_ref = rest[-2:]

        @pl.when(pl.program_id(0) == 0)
        def _():
            dg_ref[...] = jnp.zeros_like(dg_ref)

        dh = dh_ref[...]
        r = r_ref[...]
        xhat = x_ref[...] * r
        dg_ref[...] += jnp.sum(dh * xhat, axis=0, keepdims=True)
        dxn = dh * g_ref[...]
        gx_ref[...] = dout_ref[...] + r * (dxn - xhat * jnp.mean(dxn * xhat, axis=-1, keepdims=True))

    row = lambda i: (t0 + i, 0)
    fix = lambda i: (0, 0)
    in_specs = [pl.BlockSpec((tm, D), lambda i: (i, 0)), pl.BlockSpec((tm, D), row), pl.BlockSpec((tm, 1), row),
                pl.BlockSpec((1, D), fix), pl.BlockSpec((tm, D), row)]
    args = [dh, x2, rinv, gain, dout]
    aliases = {}
    if gx_prev is not None:
        in_specs.append(ANY)
        args.append(gx_prev)
        aliases = {5: 0}
    return pl.pallas_call(
        body, grid=(nt,), name=name, in_specs=in_specs,
        out_specs=[pl.BlockSpec((tm, D), row), pl.BlockSpec((1, D), fix)],
        out_shape=[jax.ShapeDtypeStruct((S, D), F32), jax.ShapeDtypeStruct((1, D), F32)],
        input_output_aliases=aliases, compiler_params=_cp(("arbitrary",)),
    )(*args)


def _grad_w_in(h, sources):
    S, D = h.shape
    SEG = sources[0].shape[2]
    counts, starts = _dz_sources(sources)
    ns = len(sources)
    tk = min(1024, S)

    def body(*refs):
        h_ref = refs[0]
        src = refs[1:1 + ns]
        o_ref = refs[1 + ns]
        j = pl.program_id(0)
        k = pl.program_id(1)

        @pl.when(k == 0)
        def _():
            o_ref[...] = jnp.zeros_like(o_ref)

        for s in range(ns):
            @pl.when((j >= starts[s]) & (j < starts[s] + counts[s]))
            def _(s=s):
                o_ref[...] += _dot_tn(h_ref[...], src[s][...])

    def src_spec(s):
        return pl.BlockSpec((None, tk, SEG),
                            lambda j, k: (jnp.clip(j - starts[s], 0, counts[s] - 1), k, 0))

    return pl.pallas_call(
        body, grid=(8, S // tk), name="grad_w_in",
        in_specs=[pl.BlockSpec((tk, D), lambda j, k: (k, 0))] + [src_spec(s) for s in range(ns)],
        out_specs=pl.BlockSpec((None, D, SEG), lambda j, k: (j // 2, 0, j % 2)),
        out_shape=jax.ShapeDtypeStruct((4, D, 2 * SEG), F32),
        compiler_params=_cp(("parallel", "arbitrary")),
    )(h, *sources)


def _lower_bound(lbl):
    l0 = lbl[0:1, :]
    l1 = lbl[1:2, :]
    m = jnp.maximum(l0, l1)
    e0 = jnp.exp(l0 - m)
    e1 = jnp.exp(l1 - m)
    return e0 / (e0 + e1)


def _tile_masks():
    row = lax.broadcasted_iota(jnp.int32, (HGRN_TILE, HGRN_TILE), 0)
    col = lax.broadcasted_iota(jnp.int32, (HGRN_TILE, HGRN_TILE), 1)
    same = (row // HGRN_CHUNK) == (col // HGRN_CHUNK)
    return same & (row >= col), same & (row <= col)


def _chunk_last(b):
    T = b.shape[0]
    b3 = b.reshape(T // HGRN_CHUNK, HGRN_CHUNK, HGRN_HEAD)
    return jnp.broadcast_to(b3[:, HGRN_CHUNK - 1:HGRN_CHUNK, :], b3.shape).reshape(T, HGRN_HEAD)


def _chunk_sum(x):
    T = x.shape[0]
    x3 = x.reshape(T // HGRN_CHUNK, HGRN_CHUNK, HGRN_HEAD)
    return jnp.broadcast_to(jnp.sum(x3, axis=1, keepdims=True), x3.shape).reshape(T, HGRN_HEAD)


def _hgrn_dims(S, SEG):
    T = min(HGRN_BLOCK, S)
    assert S % T == 0 and T % HGRN_TILE == 0
    tiles = [slice(t * HGRN_TILE, (t + 1) * HGRN_TILE) for t in range(T // HGRN_TILE)]
    chunks = [slice(c * HGRN_CHUNK, (c + 1) * HGRN_CHUNK) for c in range(T // HGRN_CHUNK)]
    return SEG // HGRN_HEAD, T, T // HGRN_CHUNK, S // T, tiles, chunks


def _hgrn_fwd(zf32, lb_logits, gnorm):
    _, NLB, S, _ = zf32.shape
    SEG = NLB * LANES
    H, T, NC, NJ, tiles, chunks = _hgrn_dims(S, SEG)

    def body(zq_ref, zf_ref, zi_ref, zg_ref, lbl_ref, gn_ref, y_ref, st_ref, state):
        @pl.when(pl.program_id(1) == 0)
        def _():
            state[...] = jnp.zeros_like(state)

        lb = _lower_bound(lbl_ref[...])
        tril, _ = _tile_masks()
        tril_bf = tril.astype(BF16)
        zq = zq_ref[...]
        q = zq * _sigmoid(zq)
        f = lb + (1.0 - lb) * _sigmoid(zf_ref[...])
        k = 1.0 - f
        logf = jnp.log(f)
        b = jnp.concatenate([_exact_dot(tril_bf, logf[t]) for t in tiles], axis=0)
        bl = _chunk_last(b)
        qd_b = (q * jnp.exp(b)).astype(BF16)
        kd_b = (k * jnp.exp(-b)).astype(BF16)
        ke_b = (k * jnp.exp(bl - b)).astype(BF16)
        v_b = zi_ref[...].astype(BF16)
        o_intra = jnp.concatenate(
            [_dot(jnp.where(tril, _dot_nt(qd_b[t], kd_b[t]), 0.0).astype(BF16), v_b[t]) for t in tiles], axis=0)
        kvs = [_dot_tn(v_b[r], ke_b[r]) for r in chunks]
        ebl = jnp.exp(bl)
        st = state[...]
        sts = []
        for c in range(NC):
            st_ref[c] = st
            sts.append(st.astype(BF16))
            st = st * ebl[c * HGRN_CHUNK:c * HGRN_CHUNK + 1, :] + kvs[c]
        state[...] = st
        o = o_intra + jnp.concatenate([_dot_nt(qd_b[r], sb) for r, sb in zip(chunks, sts)], axis=0)
        on = o * lax.rsqrt(jnp.mean(o * o, axis=-1, keepdims=True) + NORM_EPS) * gn_ref[...]
        zg = zg_ref[...]
        y_ref[...] = (on * (zg * _sigmoid(zg))).astype(BF16)

    def zspec(seg):
        return pl.BlockSpec((None, None, T, HGRN_HEAD), lambda h, j: (seg, h, j, 0))

    return pl.pallas_call(
        body, grid=(H, NJ), name="hgrn_fwd",
        in_specs=[zspec(0), zspec(1), zspec(2), zspec(3),
                  pl.BlockSpec((2, HGRN_HEAD), lambda h, j: (0, h)),
                  pl.BlockSpec((1, HGRN_HEAD), lambda h, j: (0, 0))],
        out_specs=[pl.BlockSpec((T, HGRN_HEAD), lambda h, j: (j, h)),
                   pl.BlockSpec((NC, None, HGRN_HEAD, HGRN_HEAD), lambda h, j: (j, h, 0, 0))],
        out_shape=[jax.ShapeDtypeStruct((S, SEG), BF16),
                   jax.ShapeDtypeStruct((S // HGRN_CHUNK, H, HGRN_HEAD, HGRN_HEAD), F32)],
        scratch_shapes=[pltpu.VMEM((HGRN_HEAD, HGRN_HEAD), F32)],
        compiler_params=_cp(("parallel", "arbitrary")),
    )(zf32, zf32, zf32, zf32, lb_logits, gnorm)


def _hgrn_bwd(zf32, lb_logits, gnorm, states, dy):
    _, NLB, S, _ = zf32.shape
    SEG = NLB * LANES
    H, T, NC, NJ, tiles, chunks = _hgrn_dims(S, SEG)
    C = HGRN_CHUNK

    def body(zq_ref, zf_ref, zi_ref, zg_ref, lbl_ref, gn_ref, st_ref, dy_ref, dz_ref, dl_ref, dgn_ref, gstate):
        @pl.when(pl.program_id(1) == 0)
        def _():
            gstate[...] = jnp.zeros_like(gstate)
            dl_ref[...] = jnp.zeros_like(dl_ref)
            dgn_ref[...] = jnp.zeros_like(dgn_ref)

        lb = _lower_bound(lbl_ref[...])
        gn = gn_ref[...]
        tril, triu = _tile_masks()
        tril_bf = tril.astype(BF16)
        triu_bf = triu.astype(BF16)
        q, dq_dz = _silu_and_grad(zq_ref[...])
        sf = _sigmoid(zf_ref[...])
        f = lb + (1.0 - lb) * sf
        k = 1.0 - f
        logf = jnp.log(f)
        b = jnp.concatenate([_exact_dot(tril_bf, logf[t]) for t in tiles], axis=0)
        bl = _chunk_last(b)
        eb = jnp.exp(b)
        enb = jnp.exp(-b)
        ekl = jnp.exp(bl - b)
        ebl = jnp.exp(bl)
        qd = q * eb
        kd = k * enb
        ke = k * ekl
        qd_b = qd.astype(BF16)
        kd_b = kd.astype(BF16)
        ke_b = ke.astype(BF16)
        v_b = zi_ref[...].astype(BF16)
        sts = [st_ref[c] for c in range(NC)]
        sts_b = [s.astype(BF16) for s in sts]
        a_b = [jnp.where(tril, _dot_nt(qd_b[t], kd_b[t]), 0.0).astype(BF16) for t in tiles]
        o = (jnp.concatenate([_dot(a, v_b[t]) for a, t in zip(a_b, tiles)], axis=0)
             + jnp.concatenate([_dot_nt(qd_b[r], sb) for r, sb in zip(chunks, sts_b)], axis=0))
        rinv = lax.rsqrt(jnp.mean(o * o, axis=-1, keepdims=True) + NORM_EPS)
        ohat = o * rinv
        sg, dsg = _silu_and_grad(zg_ref[...])
        dyv = dy_ref[...]
        don = dyv * sg
        dz_ref[3] = (dyv * (ohat * gn) * dsg).astype(BF16)
        dgn_ref[...] += jnp.sum(don * ohat, axis=0, keepdims=True)
        dohat = don * gn
        do = rinv * (dohat - ohat * jnp.mean(dohat * ohat, axis=-1, keepdims=True))
        do_b = do.astype(BF16)
        da_b = [jnp.where(tril, _dot_nt(do_b[t], v_b[t]), 0.0).astype(BF16) for t in tiles]
        dv_intra = jnp.concatenate([_dot_tn(a, do_b[t]) for a, t in zip(a_b, tiles)], axis=0)
        dqd_intra = jnp.concatenate([_dot(da, kd_b[t]) for da, t in zip(da_b, tiles)], axis=0)
        dkd = jnp.concatenate([_dot_tn(da, qd_b[t]) for da, t in zip(da_b, tiles)], axis=0)
        dqd_inter = jnp.concatenate([_dot(do_b[r], sb) for r, sb in zip(chunks, sts_b)], axis=0)
        gks = [_dot_tn(do_b[r], qd_b[r]) for r in chunks]
        g = gstate[...]
        gs = [None] * NC
        for c in reversed(range(NC)):
            gs[c] = g
            g = g * ebl[c * C:c * C + 1, :] + gks[c]
        gstate[...] = g
        gs_b = [x.astype(BF16) for x in gs]
        dv = dv_intra + jnp.concatenate([_dot_nt(ke_b[r], gb) for r, gb in zip(chunks, gs_b)], axis=0)
        dz_ref[2] = dv.astype(BF16)
        dke = jnp.concatenate([_dot(v_b[r], gb) for r, gb in zip(chunks, gs_b)], axis=0)
        debl = jnp.concatenate(
            [jnp.broadcast_to(jnp.sum(x * s, axis=0, keepdims=True), (C, HGRN_HEAD)) for x, s in zip(gs, sts)], axis=0)
        dqd = dqd_intra + dqd_inter
        dz_ref[0] = ((dqd * eb) * dq_dz).astype(BF16)
        t_ke = dke * ke
        db = dqd * qd - dkd * kd - t_ke
        db_last = _chunk_sum(t_ke) + debl * ebl
        dk = dkd * enb + dke * ekl
        dlogf = jnp.concatenate([_exact_dot(triu_bf, db[t]) for t in tiles], axis=0) + db_last
        df = dlogf / f - dk
        dz_ref[1] = (df * (1.0 - lb) * (sf * (1.0 - sf))).astype(BF16)
        dlb = jnp.sum(df * (1.0 - sf), axis=0, keepdims=True)
        dl0 = dlb * lb * (1.0 - lb)
        dl_ref[0:1, :] += dl0
        dl_ref[1:2, :] -= dl0

    def zspec(seg):
        return pl.BlockSpec((None, None, T, HGRN_HEAD), lambda h, j: (seg, h, NJ - 1 - j, 0))

    return pl.pallas_call(
        body, grid=(H, NJ), name="hgrn_bwd",
        in_specs=[zspec(0), zspec(1), zspec(2), zspec(3),
                  pl.BlockSpec((2, HGRN_HEAD), lambda h, j: (0, h)),
                  pl.BlockSpec((1, HGRN_HEAD), lambda h, j: (0, 0)),
                  pl.BlockSpec((NC, None, HGRN_HEAD, HGRN_HEAD), lambda h, j: (NJ - 1 - j, h, 0, 0)),
                  pl.BlockSpec((T, HGRN_HEAD), lambda h, j: (NJ - 1 - j, h))],
        out_specs=[pl.BlockSpec((4, T, HGRN_HEAD), lambda h, j: (0, NJ - 1 - j, h)),
                   pl.BlockSpec((2, HGRN_HEAD), lambda h, j: (0, h)),
                   pl.BlockSpec((None, 1, HGRN_HEAD), lambda h, j: (h, 0, 0))],
        out_shape=[jax.ShapeDtypeStruct((4, S, SEG), BF16), jax.ShapeDtypeStruct((2, SEG), F32),
                   jax.ShapeDtypeStruct((H, 1, HGRN_HEAD), F32)],
        scratch_shapes=[pltpu.VMEM((HGRN_HEAD, HGRN_HEAD), F32)],
        compiler_params=_cp(("parallel", "arbitrary")),
    )(zf32, zf32, zf32, zf32, lb_logits, gnorm, states, dy)


def _alibi_slopes(seg):
    n_heads = seg // ATTN_HEAD
    s = 2.0 ** (-8.0 * np.arange(1, n_heads + 1, dtype=np.float64) / n_heads)
    return jnp.asarray(np.repeat(s, ATTN_HEAD)[None, :], F32)


def _attn_dims(S, SEG, d):
    rb = BAND * d
    assert S % rb == 0 and SEG % LANES == 0
    npb = max(1, min(SEG // LANES, ATTN_BLOCK_ELEMS // (rb * LANES)))
    assert (SEG // LANES) % npb == 0
    return rb, npb, S // rb, (SEG // LANES) // npb


def _res_rows(r, d):
    return pl.ds(0, BAND) if d == 1 else pl.ds(r, BAND, stride=d)


def _for_residues(d, fn):
    if d == 1:
        fn(0)
    else:
        def step(r, carry):
            fn(r)
            return carry
        lax.fori_loop(0, d, step, 0, unroll=ATTN_UNROLL)


def _for_groups(d, n_pairs, fn):
    def over_pairs(r):
        for g0 in range(0, n_pairs, ATTN_UNROLL):
            fn([(r, p) for p in range(g0, min(n_pairs, g0 + ATTN_UNROLL))])

    if d == 1:
        over_pairs(0)
    elif n_pairs >= ATTN_UNROLL:
        def step(r, carry):
            over_pairs(r)
            return carry
        lax.fori_loop(0, d, step, 0)
    else:
        per_group = ATTN_UNROLL // n_pairs
        assert d % per_group == 0

        def step(g, carry):
            fn([(g * per_group + i, p) for i in range(per_group) for p in range(n_pairs)])
            return carry
        lax.fori_loop(0, d // per_group, step, 0)


def _band_terms(n, d):
    i = lax.broadcasted_iota(jnp.int32, (BAND, 2 * BAND), 0)
    jj = lax.broadcasted_iota(jnp.int32, (BAND, 2 * BAND), 1)
    delta = BAND + i - jj
    valid = (delta >= 0) & (delta <= BAND) & ((n > 0) | (jj >= BAND))
    return (-d * delta).astype(F32), valid


def _head_biases(slopes, nd, valid):
    out = []
    for s in _per_head(slopes):
        s2 = jnp.concatenate([s, s], axis=1)
        out.append(jnp.where(valid, s2 * nd, NEG))
    return jnp.concatenate(out, axis=0)


def _stack_heads(x):
    lane = lax.broadcasted_iota(jnp.int32, x.shape, 1)
    zero = jnp.zeros_like(x)
    return jnp.concatenate([jnp.where(lane < ATTN_HEAD, x, zero), jnp.where(lane < ATTN_HEAD, zero, x)], axis=0)


def _unstack_heads(x2):
    first = lax.broadcasted_iota(jnp.int32, (BAND, LANES), 1) < ATTN_HEAD
    return jnp.where(first, x2[:BAND], x2[BAND:])


def _stack_per_head(x):
    a, b = _per_head(x)
    col = jnp.concatenate([a, b], axis=0)
    return jnp.concatenate([col, col], axis=1)


def _per_head(x):
    lane = lax.broadcasted_iota(jnp.int32, x.shape, 1)
    sw = pltpu.roll(x, ATTN_HEAD, 1)
    first = lane < ATTN_HEAD
    return jnp.where(first, x, sw), jnp.where(first, sw, x)


def _attn_fwd(qkv, slopes, d):
    _, NLB, S, _ = qkv.shape
    rb, NP, nb, ncb = _attn_dims(S, NLB * LANES, d)

    def body(q_ref, kp_ref, kc_ref, vp_ref, vc_ref, sl_ref, o_ref, l_ref):
        n = pl.program_id(1)
        nd, valid = _band_terms(n, d)
        biases = [_head_biases(sl_ref[:, p * LANES:(p + 1) * LANES], nd, valid) for p in range(NP)]

        def group(items):
            scores, values = [], []
            for r, p in items:
                rows = _res_rows(r, d)
                kc = jnp.concatenate([kp_ref.at[p][rows, :], kc_ref.at[p][rows, :]], axis=0).astype(BF16)
                values.append(jnp.concatenate([vp_ref.at[p][rows, :], vc_ref.at[p][rows, :]], axis=0).astype(BF16))
                scores.append(_dot_nt(_stack_heads((q_ref.at[p][rows, :] * ATTN_SCALE).astype(BF16)), kc))
            probs = []
            for (r, p), s in zip(items, scores):
                s = s + biases[p]
                m = jnp.max(s, axis=-1, keepdims=True)
                e = jnp.exp(s - m)
                den = jnp.sum(e, axis=-1, keepdims=True)
                probs.append((e.astype(BF16), den, m + jnp.log(den)))
            for (r, p), vc, (e, den, lse) in zip(items, values, probs):
                rows = _res_rows(r, d)
                o_ref.at[p][rows, :] = _unstack_heads(_dot(e, vc) / den)
                l_ref.at[p][rows, :] = _unstack_heads(jnp.broadcast_to(lse, (2 * BAND, LANES)))

        _for_groups(d, NP, group)

    def spec(seg, prev):
        if prev:
            return pl.BlockSpec((None, NP, rb, LANES), lambda c, n: (SEG_QKV + seg, c, jnp.maximum(n - 1, 0), 0))
        return pl.BlockSpec((None, NP, rb, LANES), lambda c, n: (SEG_QKV + seg, c, n, 0))

    out = pl.BlockSpec((NP, rb, LANES), lambda c, n: (c, n, 0))
    return pl.pallas_call(
        body, grid=(ncb, nb), name=f"attn_fwd_d{d}",
        in_specs=[spec(0, False), spec(1, True), spec(1, False), spec(2, True), spec(2, False),
                  pl.BlockSpec((1, NP * LANES), lambda c, n: (0, c))],
        out_specs=[out, out],
        out_shape=[jax.ShapeDtypeStruct((NLB, S, LANES), F32)] * 2,
        compiler_params=_cp(("parallel", "parallel")),
    )(qkv, qkv, qkv, qkv, qkv, slopes)


def _attn_merge(outs, lses, zf32):
    NLB, S, _ = outs[0].shape
    SEG = NLB * LANES
    tm = min(256, S)

    def body(o1, o2, o3, l1, l2, l3, zg_ref, o_ref, lse_ref, y_ref):
        a, b, c = l1[...], l2[...], l3[...]
        m = jnp.maximum(jnp.maximum(a, b), c)
        ea, eb, ec = jnp.exp(a - m), jnp.exp(b - m), jnp.exp(c - m)
        tot = ea + eb + ec
        o = (ea / tot) * o1[...] + (eb / tot) * o2[...] + (ec / tot) * o3[...]
        o_ref[...] = o
        lse_ref[...] = m + jnp.log(tot)
        zg = zg_ref[...]
        y = (o * (zg * _sigmoid(zg))).astype(BF16)
        for p in range(NLB):
            y_ref[:, p * LANES:(p + 1) * LANES] = y[p]

    blk = pl.BlockSpec((NLB, tm, LANES), lambda i: (0, i, 0))
    return pl.pallas_call(
        body, grid=(S // tm,), name="attn_merge",
        in_specs=[blk] * 6 + [pl.BlockSpec((None, NLB, tm, LANES), lambda i: (SEG_GATE_A, 0, i, 0))],
        out_specs=[blk, blk, pl.BlockSpec((tm, SEG), lambda i: (i, 0))],
        out_shape=[jax.ShapeDtypeStruct((NLB, S, LANES), F32), jax.ShapeDtypeStruct((NLB, S, LANES), F32),
                   jax.ShapeDtypeStruct((S, SEG), BF16)],
        compiler_params=_cp(("parallel",)),
    )(*outs, *lses, zf32)


def _attn_gate_bwd(dy, o, zf32):
    NP, S, _ = o.shape
    SEG = NP * LANES
    tm = min(256, S)

    def body(dy_ref, o_ref, zg_ref, do_ref, dl_ref, dzg_ref):
        r = lax.broadcasted_iota(jnp.int32, (LANES, LANES), 0) // ATTN_HEAD
        c = lax.broadcasted_iota(jnp.int32, (LANES, LANES), 1) // ATTN_HEAD
        same_head = (r == c).astype(BF16)
        for p in range(NP):
            cols = slice(p * LANES, (p + 1) * LANES)
            sg, dsg = _silu_and_grad(zg_ref[p])
            dyv = dy_ref[:, cols]
            ov = o_ref[p]
            do = dyv * sg
            do_ref[p] = do
            dzg_ref[:, cols] = (dyv * ov * dsg).astype(BF16)
            dl_ref[p] = _exact_dot_right(do * ov, same_head)

    blk = pl.BlockSpec((NP, tm, LANES), lambda i: (0, i, 0))
    return pl.pallas_call(
        body, grid=(S // tm,), name="attn_gate_bwd",
        in_specs=[pl.BlockSpec((tm, SEG), lambda i: (i, 1)), blk,
                  pl.BlockSpec((None, NP, tm, LANES), lambda i: (SEG_GATE_A, 0, i, 0))],
        out_specs=[blk, blk, pl.BlockSpec((None, tm, SEG), lambda i: (3, i, 0))],
        out_shape=[jax.ShapeDtypeStruct((NP, S, LANES), F32), jax.ShapeDtypeStruct((NP, S, LANES), F32),
                   jax.ShapeDtypeStruct((4, S, SEG), BF16)],
        compiler_params=_cp(("parallel",)),
    )(dy, o, zf32)


def _attn_bwd(qkv, slopes, do, lse, dl, d, acc, into):
    _, NLB, S, _ = qkv.shape
    SEG = NLB * LANES
    rb, NP, nb, ncb = _attn_dims(S, SEG, d)
    has_acc = acc is not None
    out_dtype = F32 if into is None else into.dtype
    assert into is None or d == 1

    def body(*refs):
        q_ref, kp_ref, kc_ref, vp_ref, vc_ref, sl_ref, do_ref, lse_ref, dl_ref = refs[:9]
        acc_ref = refs[9] if has_acc else None
        out_ref, cq, ck, cv = refs[-4:]
        n = pl.program_id(1)

        def emit(r, p, dq, dk, dv):
            rows = _res_rows(r, d)
            for t, val in enumerate((dq, dk, dv)):
                if has_acc:
                    val = val + acc_ref.at[t].at[p][rows, :]
                if into is None:
                    out_ref.at[t].at[p][rows, :] = val.astype(out_dtype)
                else:
                    out_ref.at[t][rows, p * LANES:(p + 1) * LANES] = val.astype(out_dtype)

        @pl.when(n == 0)
        def _():
            cq[...] = jnp.zeros_like(cq)
            ck[...] = jnp.zeros_like(ck)
            cv[...] = jnp.zeros_like(cv)

        @pl.when(n < nb)
        def _():
            nd, valid = _band_terms(n, d)
            biases = [_head_biases(sl_ref[:, p * LANES:(p + 1) * LANES], nd, valid) for p in range(NP)]

            def group(items):
                first = []
                for r, p in items:
                    rows = _res_rows(r, d)
                    kc = jnp.concatenate([kp_ref.at[p][rows, :], kc_ref.at[p][rows, :]], axis=0).astype(BF16)
                    vc = jnp.concatenate([vp_ref.at[p][rows, :], vc_ref.at[p][rows, :]], axis=0).astype(BF16)
                    qs = _stack_heads((q_ref.at[p][rows, :] * ATTN_SCALE).astype(BF16))
                    dos = _stack_heads(do_ref.at[p][rows, :].astype(BF16))
                    first.append((kc, qs, dos, _dot_nt(qs, kc), _dot_nt(dos, vc)))
                second = []
                for (r, p), (kc, qs, dos, s, dp) in zip(items, first):
                    rows = _res_rows(r, d)
                    pr = jnp.exp(s + biases[p] - _stack_per_head(lse_ref.at[p][rows, :]))
                    ds = (pr * (dp - _stack_per_head(dl_ref.at[p][rows, :]))).astype(BF16)
                    second.append((kc, qs, dos, pr.astype(BF16), ds))
                for (r, p), (kc, qs, dos, pr, ds) in zip(items, second):
                    dq = _unstack_heads(_dot(ds, kc)) * ATTN_SCALE
                    dk = _dot_tn(ds, qs)
                    dv = _dot_tn(pr, dos)
                    emit(r, p, cq[r, p], ck[r, p] + dk[:BAND, :], cv[r, p] + dv[:BAND, :])
                    cq[r, p] = dq
                    ck[r, p] = dk[BAND:, :]
                    cv[r, p] = dv[BAND:, :]

            _for_groups(d, NP, group)

        @pl.when(n == nb)
        def _():
            def last(r):
                for p in range(NP):
                    emit(r, p, cq[r, p], ck[r, p], cv[r, p])
            _for_residues(d, last)

    cur = lambda c, n: (c, jnp.minimum(n, nb - 1), 0)
    lag = lambda c, n: (0, c, jnp.clip(n - 1, 0, nb - 1), 0)

    def spec(seg, prev):
        if prev:
            return pl.BlockSpec((None, NP, rb, LANES), lambda c, n: (SEG_QKV + seg, c, jnp.clip(n - 1, 0, nb - 1), 0))
        return pl.BlockSpec((None, NP, rb, LANES), lambda c, n: (SEG_QKV + seg, c, jnp.minimum(n, nb - 1), 0))

    in_specs = [spec(0, False), spec(1, True), spec(1, False), spec(2, True), spec(2, False),
                pl.BlockSpec((1, NP * LANES), lambda c, n: (0, c))] + [pl.BlockSpec((NP, rb, LANES), cur)] * 3
    args = [qkv, qkv, qkv, qkv, qkv, slopes, do, lse, dl]
    aliases = {}
    if has_acc:
        in_specs.append(pl.BlockSpec((3, NP, rb, LANES), lag))
        args.append(acc)
        if into is None:
            aliases = {9: 0}
    if into is None:
        out_sds = jax.ShapeDtypeStruct((3, NLB, S, LANES), F32)
        out_spec = pl.BlockSpec((3, NP, rb, LANES), lag)
    else:
        in_specs.append(ANY)
        args.append(into)
        aliases = {len(args) - 1: 0}
        out_sds = jax.ShapeDtypeStruct(into.shape, into.dtype)
        out_spec = pl.BlockSpec((3, rb, NP * LANES), lambda c, n: (0, jnp.clip(n - 1, 0, nb - 1), c))
    return pl.pallas_call(
        body, grid=(ncb, nb + 1), name=f"attn_bwd_d{d}",
        in_specs=in_specs, out_specs=out_spec, out_shape=out_sds,
        scratch_shapes=[pltpu.VMEM((d, NP, BAND, LANES), F32)] * 3,
        input_output_aliases=aliases,
        compiler_params=_cp(("parallel", "arbitrary")),
    )(*args)


def _adamw(w, g, m, v, name):
    R, C = w.shape
    tr = R if R <= 256 else 256
    assert R % tr == 0

    def body(w_ref, g_ref, m_ref, v_ref, d_ref, nm_ref, nv_ref):
        g = g_ref[...]
        nm = ADAM_B1 * m_ref[...] + (1.0 - ADAM_B1) * g
        nv = ADAM_B2 * v_ref[...] + (1.0 - ADAM_B2) * (g * g)
        m_hat = nm / (1.0 - ADAM_B1 ** ADAM_STEP)
        v_hat = nv / (1.0 - ADAM_B2 ** ADAM_STEP)
        d_ref[...] = -ADAM_LR * (m_hat / (jnp.sqrt(v_hat) + ADAM_EPS) + ADAM_WD * w_ref[...])
        nm_ref[...] = nm
        nv_ref[...] = nv

    blk = pl.BlockSpec((tr, C), lambda i: (i, 0))
    sds = jax.ShapeDtypeStruct((R, C), F32)
    return pl.pallas_call(
        body, grid=(R // tr,), name=name, in_specs=[blk] * 4, out_specs=[blk] * 3, out_shape=[sds] * 3,
        compiler_params=_cp(("parallel",)),
    )(w, g, m, v)


def _coords():
    return lax.axis_index("x"), lax.axis_index("y"), lax.axis_index("c")


def _other_chips(x, y):
    return [(1 - x, y), (x, 1 - y), (1 - x, 1 - y)]


ANY = pl.BlockSpec(memory_space=pl.ANY)


def _cast_into_slot(w, where, name):
    R, C = w.shape
    tr = min(256, R)

    def body(where_ref, w_ref, o_ref):
        o_ref[...] = w_ref[...].astype(BF16)

    grid_spec = pltpu.PrefetchScalarGridSpec(
        num_scalar_prefetch=1, grid=(R // tr,),
        in_specs=[pl.BlockSpec((tr, C), lambda i, w: (i, 0))],
        out_specs=pl.BlockSpec((None, tr, C), lambda i, w: (w[1], i, 0)))
    return pl.pallas_call(
        body, grid_spec=grid_spec, name=name, out_shape=jax.ShapeDtypeStruct((4, R, C), BF16),
        compiler_params=_cp(("parallel",)),
    )(where, w)


def _pair_sum(g, sib, where, name):
    _, n2, C = g.shape
    N = n2 // 2
    tr = min(256, N)
    nt = N // tr

    def body(where_ref, g_ref, s_ref, qb_ref, own_ref):
        q = pl.program_id(1)
        tot = g_ref[...] + s_ref[...]
        qb_ref[...] = tot.astype(BF16)

        @pl.when(q == where_ref[1])
        def _():
            own_ref[...] = tot

    grid_spec = pltpu.PrefetchScalarGridSpec(
        num_scalar_prefetch=1, grid=(nt, 4),
        in_specs=[pl.BlockSpec((None, tr, C), lambda i, q, w: (q, w[0] * nt + i, 0)),
                  pl.BlockSpec((None, tr, C), lambda i, q, w: (q, i, 0))],
        out_specs=[pl.BlockSpec((None, tr, C), lambda i, q, w: (q, i, 0)),
                   pl.BlockSpec((tr, C), lambda i, q, w: (i, 0))])
    return pl.pallas_call(
        body, grid_spec=grid_spec, name=name,
        out_shape=[jax.ShapeDtypeStruct((4, N, C), BF16), jax.ShapeDtypeStruct((N, C), F32)],
        compiler_params=_cp(("parallel", "arbitrary")),
    )(where, g, sib)


HBM = pl.BlockSpec(memory_space=pltpu.HBM)
SEM = pl.BlockSpec(memory_space=pltpu.SEMAPHORE)


def _in_hbm(a):
    return pltpu.with_memory_space_constraint(a, pltpu.HBM)


def _split_start(name, copies, arrays, n_sems, after=None):
    n = len(arrays)

    def body(*refs):
        for cp in copies(refs[:n], refs[-n - 3], refs[-n - 2]):
            cp.start()
        refs[-1][...] = jnp.zeros_like(refs[-1])

    ordered = () if after is None else (after,)
    outs = pl.pallas_call(
        body, name=name,
        out_shape=(pltpu.SemaphoreType.DMA((n_sems,)), pltpu.SemaphoreType.DMA((n_sems,)),
                   *[pltpu.HBM(a.shape, a.dtype) for a in arrays], jax.ShapeDtypeStruct((8, LANES), F32)),
        in_specs=(HBM,) * n + (ANY,) * len(ordered),
        out_specs=(SEM, SEM) + (HBM,) * n + (pl.BlockSpec(memory_space=pltpu.VMEM),),
        input_output_aliases={i: 2 + i for i in range(n)},
        compiler_params=pltpu.CompilerParams(has_side_effects=pltpu.SideEffectType.DATAFLOW_SIDE_EFFECTING),
    )(*[_in_hbm(a) for a in arrays], *ordered)
    return outs[0], outs[1], list(outs[2:2 + n]), outs[-1]


def _split_wait(name, copies, send_sems, recv_sems, arrays, after):
    n = len(arrays)

    def body(*refs):
        for cp in copies(refs[:n], refs[n], refs[n + 1]):
            cp.wait_send()
            cp.wait_recv()

    outs = pl.pallas_call(
        body, name=name,
        out_shape=tuple(pltpu.HBM(a.shape, a.dtype) for a in arrays),
        in_specs=(HBM,) * n + (SEM, SEM, ANY), out_specs=(HBM,) * n,
        input_output_aliases={i: i for i in range(n)},
        compiler_params=pltpu.CompilerParams(has_side_effects=pltpu.SideEffectType.DATAFLOW_SIDE_EFFECTING),
    )(*arrays, send_sems, recv_sems, after)
    return list(outs)


def _remote(src, dst, sems, k, to):
    send_sems, recv_sems = sems
    return pltpu.make_async_remote_copy(src_ref=src, dst_ref=dst, send_sem=send_sems.at[k], recv_sem=recv_sems.at[k],
                                        device_id=to, device_id_type=MESH)


def _gather_in_copies(refs, send_sems, recv_sems):
    (w,) = refs
    x, y, c = _coords()
    seg = w.shape[2] // 2
    mine = w.at[2 * x + y, :, pl.ds(c * seg, seg)]
    return [_remote(mine, mine, (send_sems, recv_sems), k, (px, py, c)) for k, (px, py) in enumerate(_other_chips(x, y))]


def _gather_out_copies(refs, send_sems, recv_sems):
    (w,) = refs
    x, y, c = _coords()
    mine = w.at[2 * x + y]
    return [_remote(mine, mine, (send_sems, recv_sems), k, (px, py, c)) for k, (px, py) in enumerate(_other_chips(x, y))]


def _swap_copies(refs, send_sems, recv_sems):
    gi, go, si, so = refs
    x, y, c = _coords()
    cps = []
    for a, (src, dst) in enumerate(((gi, si), (go, so))):
        nr = dst.shape[1]
        cps.append(_remote(src.at[:, pl.ds((1 - c) * nr, nr), :], dst, (send_sems, recv_sems), a, (x, y, 1 - c)))
    return cps


def _scatter_copies(refs, send_sems, recv_sems):
    qi, qo, ri, ro = refs
    x, y, c = _coords()
    cps = []
    for k, (px, py) in enumerate(_other_chips(x, y)):
        for a, (src, dst) in enumerate(((qi, ri), (qo, ro))):
            cps.append(_remote(src.at[2 * px + py], dst.at[k], (send_sems, recv_sems), 2 * k + a, (px, py, c)))
    return cps


def _forward_copies(refs, send_sems, recv_sems):
    (w,) = refs
    x, y, c = _coords()
    seg = w.shape[2] // 2
    cps = []
    for k, (px, py) in enumerate(_other_chips(x, y)):
        got = w.at[2 * px + py, :, pl.ds(c * seg, seg)]
        cps.append(_remote(got, got, (send_sems, recv_sems), k, (x, y, 1 - c)))
    return cps


def _chip_sum(own, got, where, name):
    N, C = own.shape
    tr = min(256, N)
    nt = N // tr

    def body(where_ref, own_ref, got_ref, o_ref):
        t = own_ref[...]
        for k in range(3):
            t = t + got_ref[k].astype(F32)
        o_ref[...] = t

    grid_spec = pltpu.PrefetchScalarGridSpec(
        num_scalar_prefetch=1, grid=(nt,),
        in_specs=[pl.BlockSpec((tr, C), lambda i, w: (i, 0)), pl.BlockSpec((3, tr, C), lambda i, w: (0, i, 0))],
        out_specs=pl.BlockSpec((tr, C), lambda i, w: (w[0] * nt + i, 0)))
    return pl.pallas_call(
        body, grid_spec=grid_spec, name=name, out_shape=jax.ShapeDtypeStruct((2 * N, C), F32),
        compiler_params=_cp(("parallel",)),
    )(where, own, got)


def _join_halves(gi, go):
    def body(gi_in, go_in, gi_ref, go_ref, send_sems, recv_sems):
        x, y, c = _coords()
        cps = []
        for a, ref in enumerate((gi_ref, go_ref)):
            nr = ref.shape[0] // 2
            mine = ref.at[pl.ds(c * nr, nr), :]
            cp = pltpu.make_async_remote_copy(src_ref=mine, dst_ref=mine, send_sem=send_sems.at[a],
                                              recv_sem=recv_sems.at[a], device_id=(x, y, 1 - c), device_id_type=MESH)
            cp.start()
            cps.append(cp)
        for a, ref in enumerate((gi_ref, go_ref)):
            nr = ref.shape[0] // 2
            theirs = ref.at[pl.ds((1 - c) * nr, nr), :]
            pltpu.make_async_remote_copy(src_ref=theirs, dst_ref=theirs, send_sem=send_sems.at[a],
                                         recv_sem=recv_sems.at[a], device_id=(x, y, 1 - c),
                                         device_id_type=MESH).wait_recv()
        for cp in cps:
            cp.wait_send()

    return pl.pallas_call(
        body, name="join_halves", in_specs=[ANY, ANY], out_specs=[ANY, ANY],
        out_shape=[jax.ShapeDtypeStruct(gi.shape, F32), jax.ShapeDtypeStruct(go.shape, F32)],
        scratch_shapes=[pltpu.SemaphoreType.DMA((2,)), pltpu.SemaphoreType.DMA((2,))],
        input_output_aliases={0: 0, 1: 1},
    )(gi, go)


def _all_reduce_small(part):
    R, C = part.shape

    def body(p_ref, o_ref, slots, send_sems, recv_sems):
        x, y, c = _coords()
        me = 4 * x + 2 * y + c
        slots[me] = p_ref[...]
        cps = []
        for k in range(1, 8):
            fx, fy, fc = (k >> 2) & 1, (k >> 1) & 1, k & 1
            peer = (1 - x if fx else x, 1 - y if fy else y, 1 - c if fc else c)
            cp = pltpu.make_async_remote_copy(src_ref=p_ref, dst_ref=slots.at[me], send_sem=send_sems.at[k - 1],
                                              recv_sem=recv_sems.at[k - 1], device_id=peer, device_id_type=MESH)
            cp.start()
            cps.append(cp)
        for cp in cps:
            cp.wait()
        t = slots[0]
        for k in range(1, 8):
            t = t + slots[k]
        o_ref[...] = t

    vm = pl.BlockSpec(memory_space=pltpu.VMEM)
    return pl.pallas_call(
        body, name="all_reduce_small", in_specs=[vm], out_specs=vm,
        out_shape=jax.ShapeDtypeStruct((R, C), F32),
        scratch_shapes=[pltpu.VMEM((8, R, C), F32), pltpu.SemaphoreType.DMA((7,)), pltpu.SemaphoreType.DMA((7,))],
    )(part)


def _mixers_forward(z, lb_logits, hgrn_gnorm):
    slopes = _alibi_slopes(z.shape[1] * LANES)
    yh, states = _hgrn_fwd(z, lb_logits, hgrn_gnorm)
    outs, lses = [], []
    for d in DILATIONS:
        o, l = _attn_fwd(z, slopes, d)
        outs.append(o)
        lses.append(l)
    o_attn, lse, ya = _attn_merge(outs, lses, z)
    return yh, ya, (states, o_attn, lse, slopes)


def _backward_to_dz(z, kept, lb_logits, hgrn_gnorm, yh, ya, w_out_all, x2, tgt, fgain, h):
    states, o_attn, lse, slopes = kept
    dout, doutb, loss, dfg = _out_proj_loss(yh, ya, w_out_all, x2, tgt, fgain)
    dy = _dy_proj(doutb, w_out_all)
    g_w_out = _grad_w_out(yh, ya, doutb)
    dzh, dlogits, dgn = _hgrn_bwd(z, lb_logits, hgrn_gnorm, states, dy)
    do, dl, dza = _attn_gate_bwd(dy, o_attn, z)
    acc = None
    order = sorted(DILATIONS, reverse=True)
    for d in order[:-1]:
        acc = _attn_bwd(z, slopes, do, lse, dl, d, acc, None)
    dza = _attn_bwd(z, slopes, do, lse, dl, order[-1], acc, dza)
    sources = [dzh, dza]
    g_w_in = _grad_w_in(h, sources)
    return loss, dfg, dlogits, dgn, g_w_out, g_w_in, sources, dout


def _grad_x_half(sources, w_all, x2, rinv, norm_gain, dout, token, part, gx_prev):
    dh = _dh_proj(sources, w_all, token, part, f"dh_proj_{part}")
    return _rms_bwd(dh, x2, rinv, norm_gain, dout, part, gx_prev, f"rms_bwd_{part}")


def _local_step(x2, tgt, norm_gain, w_all, lb_logits, hgrn_gnorm, w_out_all, fgain):
    token = jnp.zeros((8, LANES), F32)
    where = jnp.zeros((2,), jnp.int32)
    h, rinv = _rms_fwd(x2, norm_gain, token)
    z = _in_proj(h, w_all, where, (0, 1, 8, None), None, token, "in_proj_all")
    yh, ya, kept = _mixers_forward(z, lb_logits, hgrn_gnorm)
    loss, dfg, dlogits, dgn, g_w_out, g_w_in, sources, dout = _backward_to_dz(
        z, kept, lb_logits, hgrn_gnorm, yh, ya, w_out_all, x2, tgt, fgain, h)
    gx, dg0 = _grad_x_half(sources, w_all, x2, rinv, norm_gain, dout, token, 0, None)
    gx, dg1 = _grad_x_half(sources, w_all, x2, rinv, norm_gain, dout, token, 1, gx)
    return loss, gx, dg0 + dg1, g_w_in, dlogits, dgn, g_w_out, dfg


def _pack_small(D, loss, dgain, dlogits, dgn, dfg):
    def row(v):
        v = v.reshape(1, -1)
        return jnp.pad(v, ((0, 0), (0, D - v.shape[1])))
    rows = [row(dgain), row(dfg), row(dlogits[0]), row(dlogits[1]), row(jnp.sum(dgn, axis=0)), row(loss)]
    rows += [jnp.zeros((1, D), F32)] * (8 - len(rows))
    return jnp.concatenate(rows, axis=0)


def kernel(x, norm_gain, w_in, lb_logits, hgrn_gnorm, w_out, final_gain, loss_target, m_norm_gain, m_w_in, m_lb_logits, m_hgrn_gnorm, m_w_out, m_final_gain, v_norm_gain, v_w_in, v_lb_logits, v_hgrn_gnorm, v_w_out, v_final_gain):
    _, S, D = x.shape
    SEG = w_in.shape[2] // 2
    x2 = x[0]
    tgt = loss_target[0]
    fgain = final_gain.reshape(1, D)
    where = jnp.stack([lax.axis_index("c"), 2 * lax.axis_index("x") + lax.axis_index("y")]).astype(jnp.int32)

    wia = _cast_into_slot(w_in[0], where, "cast_w_in")
    woa = _cast_into_slot(w_out[0], where, "cast_w_out")
    sems = _split_start("gather_in_start", _gather_in_copies, [wia], 3)
    (wia,), token = sems[2], sems[3]
    h, rinv = _rms_fwd(x2, norm_gain, token)
    z = _in_proj(h, wia, where, (0, 1, 2, None), None, token, "in_proj_own")
    (wia,) = _split_wait("gather_in_wait", _gather_in_copies, sems[0], sems[1], [wia], z)
    out_sems = _split_start("gather_out_start", _gather_out_copies, [woa], 3, after=wia)
    sems = _split_start("forward_start", _forward_copies, [wia], 3, after=out_sems[3])
    z = _in_proj(h, sems[2][0], where, (2, 2, 3, 0), z, sems[3], "in_proj_received")
    (wia,) = _split_wait("forward_wait", _forward_copies, sems[0], sems[1], sems[2], z)
    z = _in_proj(h, wia, where, (2, 2, 3, 1), z, token, "in_proj_forwarded")
    yh, ya, kept = _mixers_forward(z, lb_logits, hgrn_gnorm)
    (woa,) = _split_wait("gather_out_wait", _gather_out_copies, out_sems[0], out_sems[1], out_sems[2], ya)
    w_out_all = woa.reshape(2 * SEG, D)

    loss, dfg, dlogits, dgn, g_w_out, g_w_in, sources, dout = _backward_to_dz(
        z, kept, lb_logits, hgrn_gnorm, yh, ya, w_out_all, x2, tgt, fgain, h)

    sib_i = lax.empty((4, g_w_in.shape[1] // 2, g_w_in.shape[2]), F32)
    sib_o = lax.empty((4, g_w_out.shape[1] // 2, g_w_out.shape[2]), F32)
    sems = _split_start("swap_start", _swap_copies, [g_w_in, g_w_out, sib_i, sib_o], 2)
    grad_x, dg0 = _grad_x_half(sources, wia, x2, rinv, norm_gain, dout, sems[3], 0, None)
    g_w_in, g_w_out, sib_i, sib_o = _split_wait("swap_wait", _swap_copies, sems[0], sems[1], sems[2], grad_x)
    qi, own_i = _pair_sum(g_w_in, sib_i, where, "pair_sum_w_in")
    qo, own_o = _pair_sum(g_w_out, sib_o, where, "pair_sum_w_out")
    ri = lax.empty((3,) + qi.shape[1:], BF16)
    ro = lax.empty((3,) + qo.shape[1:], BF16)
    sems = _split_start("scatter_start", _scatter_copies, [qi, qo, ri, ro], 6)
    grad_x, dg1 = _grad_x_half(sources, wia, x2, rinv, norm_gain, dout, sems[3], 1, grad_x)
    _, _, got_i, got_o = _split_wait("scatter_wait", _scatter_copies, sems[0], sems[1], sems[2], grad_x)
    grad_w_in, grad_w_out = _join_halves(_chip_sum(own_i, got_i, where, "chip_sum_w_in"),
                                         _chip_sum(own_o, got_o, where, "chip_sum_w_out"))

    small = _all_reduce_small(_pack_small(D, loss, dg0 + dg1, dlogits, dgn, dfg))
    grad_norm_gain = small[0:1, :]
    grad_final_gain = small[1:2, :]
    grad_lb_logits = small[2:4, :SEG]
    grad_hgrn_gnorm = small[4:5, :HGRN_HEAD]
    loss_sum = small[5, 0]

    d_ng, m_ng, v_ng = _adamw(norm_gain, grad_norm_gain, m_norm_gain, v_norm_gain, "adamw_norm_gain")
    d_wi, m_wi, v_wi = _adamw(w_in[0], grad_w_in, m_w_in[0], v_w_in[0], "adamw_w_in")
    d_lb, m_lb, v_lb = _adamw(lb_logits, grad_lb_logits, m_lb_logits, v_lb_logits, "adamw_lb_logits")
    d_gn, m_gn, v_gn = _adamw(hgrn_gnorm, grad_hgrn_gnorm, m_hgrn_gnorm, v_hgrn_gnorm, "adamw_hgrn_gnorm")
    d_wo, m_wo, v_wo = _adamw(w_out[0], grad_w_out, m_w_out[0], v_w_out[0], "adamw_w_out")
    d_fg, m_fg, v_fg = _adamw(fgain, grad_final_gain, m_final_gain.reshape(1, D), v_final_gain.reshape(1, D),
                              "adamw_final_gain")

    return (loss_sum, grad_x[None],
            grad_norm_gain, grad_w_in[None], grad_lb_logits, grad_hgrn_gnorm, grad_w_out[None], grad_final_gain[0],
            d_ng, d_wi[None], d_lb, d_gn, d_wo[None], d_fg[0],
            m_ng, m_wi[None], m_lb, m_gn, m_wo[None], m_fg[0],
            v_ng, v_wi[None], v_lb, v_gn, v_wo[None], v_fg[0])
```

```python
import jax
import jax.numpy as jnp
import numpy as np
from jax import lax
from jax.experimental import pallas as pl
from jax.experimental.pallas import tpu as pltpu

F32 = jnp.float32
BF16 = jnp.bfloat16
MESH = pl.DeviceIdType.MESH

NORM_EPS = 1e-6
HGRN_HEAD = 128
HGRN_CHUNK = 64
HGRN_TILE = 128
HGRN_BLOCK = 512
HGRN_HEADS_PER_STEP = 2
ATTN_HEAD = 64
LANES = 128
BAND = 128
DILATIONS = (1, 4, 16)
ATTN_SCALE = ATTN_HEAD ** -0.5
assert ATTN_SCALE == 0.125
ATTN_BLOCK_ELEMS = BAND * 2048
ATTN_UNROLL = 4
SEG_QKV = 4
SEG_GATE_A = 7
NEG = -1e30

ADAM_LR = 0.001
ADAM_B1 = 0.9
ADAM_B2 = 0.999
ADAM_EPS = 1e-08
ADAM_WD = 0.01
ADAM_STEP = 10

MIB = 1024 * 1024


def _cp(semantics=None, vmem_mib=48):
    return pltpu.CompilerParams(dimension_semantics=semantics, vmem_limit_bytes=vmem_mib * MIB)


def _dot(a, b):
    return jnp.dot(a, b, preferred_element_type=F32)


def _dot_nt(a, b):
    return lax.dot_general(a, b, (((1,), (1,)), ((), ())), preferred_element_type=F32)


def _dot_tn(a, b):
    return lax.dot_general(a, b, (((0,), (0,)), ((), ())), preferred_element_type=F32)


def _split3(x):
    hi = x.astype(BF16)
    r1 = x - hi.astype(F32)
    mid = r1.astype(BF16)
    lo = (r1 - mid.astype(F32)).astype(BF16)
    return hi, mid, lo


def _exact_dot(t_bf16, x):
    hi, mid, lo = _split3(x)
    return _dot(t_bf16, hi) + _dot(t_bf16, mid) + _dot(t_bf16, lo)


def _exact_dot_right(x, t_bf16):
    hi, mid, lo = _split3(x)
    return _dot(hi, t_bf16) + _dot(mid, t_bf16) + _dot(lo, t_bf16)


def _sigmoid(z):
    return jax.nn.sigmoid(z)


def _silu_and_grad(z):
    s = _sigmoid(z)
    return z * s, s * (1.0 + z * (1.0 - s))


def _seg_select(j, values):
    out = values[0]
    for t, v in enumerate(values[1:], 1):
        out = jnp.where(j == t, v, out)
    return out


def _rms_fwd(x2, gain, token):
    S, D = x2.shape
    tm = min(512, S)

    def body(x_ref, g_ref, _, h_ref, r_ref):
        x = x_ref[...]
        r = lax.rsqrt(jnp.mean(x * x, axis=-1, keepdims=True) + NORM_EPS)
        h_ref[...] = ((x * r) * g_ref[...]).astype(BF16)
        r_ref[...] = r

    return pl.pallas_call(
        body, grid=(S // tm,), name="rms_fwd",
        in_specs=[pl.BlockSpec((tm, D), lambda i: (i, 0)), pl.BlockSpec((1, D), lambda i: (0, 0)),
                  pl.BlockSpec(token.shape, lambda i: (0, 0))],
        out_specs=[pl.BlockSpec((tm, D), lambda i: (i, 0)), pl.BlockSpec((tm, 1), lambda i: (i, 0))],
        out_shape=[jax.ShapeDtypeStruct((S, D), BF16), jax.ShapeDtypeStruct((S, 1), F32)],
        compiler_params=_cp(("parallel",)),
    )(x2, gain, token)


def _in_proj(h, w_all, where, segs, z_prev, token, name):
    S, D = h.shape
    SEG = w_all.shape[2] // 2
    NLB = SEG // LANES
    tm = min(512, S)
    count = len(segs)

    def body(*refs):
        h_ref, w_ref, o_ref = refs[1], refs[2], refs[-1]
        res = _dot(h_ref[...], w_ref[...])
        for p in range(NLB):
            o_ref[p] = res[:, p * LANES:(p + 1) * LANES]

    def seg_of(j, w):
        halves = {0: 0, 1: 1, "mine": w[0], "sibling": 1 - w[0]}
        return _seg_select(j, [2 * jnp.bitwise_xor(w[1], rel) + halves[half] for rel, half in segs])

    in_specs = [pl.BlockSpec((tm, D), lambda j, i, w: (i, 0)),
                pl.BlockSpec((None, D, SEG), lambda j, i, w: (seg_of(j, w) // 2, 0, seg_of(j, w) % 2)),
                pl.BlockSpec(token.shape, lambda j, i, w: (0, 0))]
    args = [where, h, w_all, token]
    aliases = {}
    if z_prev is not None:
        in_specs.append(ANY)
        args.append(z_prev)
        aliases = {4: 0}
    grid_spec = pltpu.PrefetchScalarGridSpec(
        num_scalar_prefetch=1, grid=(count, S // tm), in_specs=in_specs,
        out_specs=pl.BlockSpec((None, NLB, tm, LANES), lambda j, i, w: (seg_of(j, w), 0, i, 0)))
    return pl.pallas_call(
        body, grid_spec=grid_spec, name=name, out_shape=jax.ShapeDtypeStruct((8, NLB, S, LANES), F32),
        input_output_aliases=aliases, compiler_params=_cp(("parallel", "parallel")),
    )(*args)


def _out_proj_loss(yh, ya, w_out, x2, tgt, fgain):
    S, D = x2.shape
    SEG = yh.shape[1]
    tm = min(256, S)
    parts = 2

    def body(yh_ref, ya_ref, w_ref, x_ref, t_ref, fg_ref, dout_ref, doutb_ref, loss_ref, dfg_ref):
        i = pl.program_id(0)

        @pl.when(i == 0)
        def _():
            loss_ref[...] = jnp.zeros_like(loss_ref)
            dfg_ref[...] = jnp.zeros_like(dfg_ref)

        fg = fg_ref[...]
        loss = jnp.zeros((1, 1), F32)
        dfg = jnp.zeros((1, D), F32)
        for rows in [pl.ds(p * (tm // parts), tm // parts) for p in range(parts)]:
            out = (x_ref[rows, :] + _dot(yh_ref[rows, :], w_ref[pl.ds(0, SEG), :])
                   + _dot(ya_ref[rows, :], w_ref[pl.ds(SEG, SEG), :]))
            r = lax.rsqrt(jnp.mean(out * out, axis=-1, keepdims=True) + NORM_EPS)
            n = out * r
            err = n * fg - t_ref[rows, :]
            loss = loss + 0.5 * jnp.sum(jnp.mean(err * err, axis=-1, keepdims=True), axis=0, keepdims=True)
            dy = err * (1.0 / D)
            dfg = dfg + jnp.sum(dy * n, axis=0, keepdims=True)
            dn = dy * fg
            dout = r * (dn - n * jnp.mean(dn * n, axis=-1, keepdims=True))
            dout_ref[rows, :] = dout
            doutb_ref[rows, :] = dout.astype(BF16)
        loss_ref[...] += loss
        dfg_ref[...] += dfg

    row = lambda i: (i, 0)
    fix = lambda i: (0, 0)
    return pl.pallas_call(
        body, grid=(S // tm,), name="out_proj_loss",
        in_specs=[pl.BlockSpec((tm, SEG), row), pl.BlockSpec((tm, SEG), row), pl.BlockSpec((2 * SEG, D), fix),
                  pl.BlockSpec((tm, D), row), pl.BlockSpec((tm, D), row), pl.BlockSpec((1, D), fix)],
        out_specs=[pl.BlockSpec((tm, D), row), pl.BlockSpec((tm, D), row), pl.BlockSpec((1, 1), fix),
                   pl.BlockSpec((1, D), fix)],
        out_shape=[jax.ShapeDtypeStruct((S, D), F32), jax.ShapeDtypeStruct((S, D), BF16),
                   jax.ShapeDtypeStruct((1, 1), F32), jax.ShapeDtypeStruct((1, D), F32)],
        compiler_params=_cp(("arbitrary",)),
    )(yh, ya, w_out, x2, tgt, fgain)


def _dy_proj(doutb, w_out):
    S, D = doutb.shape
    K = w_out.shape[0]
    tm = min(512, S)

    def body(d_ref, w_ref, o_ref):
        o_ref[...] = _dot_nt(d_ref[...], w_ref[...])

    return pl.pallas_call(
        body, grid=(S // tm,), name="dy_proj",
        in_specs=[pl.BlockSpec((tm, D), lambda i: (i, 0)), pl.BlockSpec((K, D), lambda i: (0, 0))],
        out_specs=pl.BlockSpec((tm, K), lambda i: (i, 0)),
        out_shape=jax.ShapeDtypeStruct((S, K), F32),
        compiler_params=_cp(("parallel",)),
    )(doutb, w_out)


def _grad_w_out(yh, ya, doutb):
    S, SEG = yh.shape
    D = doutb.shape[1]
    R = (2 * SEG) // 4
    nb_half = SEG // R
    tk = min(512, S)

    def body(yh_ref, ya_ref, d_ref, o_ref):
        q = pl.program_id(0)
        k = pl.program_id(1)

        @pl.when(k == 0)
        def _():
            o_ref[...] = jnp.zeros_like(o_ref)

        @pl.when(q < nb_half)
        def _():
            o_ref[...] += _dot_tn(yh_ref[...], d_ref[...])

        @pl.when(q >= nb_half)
        def _():
            o_ref[...] += _dot_tn(ya_ref[...], d_ref[...])

    return pl.pallas_call(
        body, grid=(4, S // tk), name="grad_w_out",
        in_specs=[pl.BlockSpec((tk, R), lambda q, k: (k, jnp.minimum(q, nb_half - 1))),
                  pl.BlockSpec((tk, R), lambda q, k: (k, jnp.maximum(q - nb_half, 0))),
                  pl.BlockSpec((tk, D), lambda q, k: (k, 0))],
        out_specs=pl.BlockSpec((None, R, D), lambda q, k: (q, 0, 0)),
        out_shape=jax.ShapeDtypeStruct((4, R, D), F32),
        compiler_params=_cp(("parallel", "arbitrary")),
    )(yh, ya, doutb)


def _dz_sources(sources):
    counts = [s.shape[0] for s in sources]
    starts = [sum(counts[:k]) for k in range(len(counts))]
    assert sum(counts) == 8
    return counts, starts


def _dh_proj(sources, w_all, token, part, name):
    S = sources[0].shape[1]
    D = w_all.shape[1]
    SEG = w_all.shape[2] // 2
    counts, starts = _dz_sources(sources)
    assert all(c % 2 == 0 for c in counts)
    ns = len(sources)
    tm = min(512, S // 2)
    nt = (S // 2) // tm
    t0 = part * nt

    def body(*refs):
        src = refs[:ns]
        w_ref, _, o_ref = refs[ns:]
        j = pl.program_id(1)

        @pl.when(j == 0)
        def _():
            o_ref[...] = jnp.zeros_like(o_ref)

        for k in range(ns):
            @pl.when((2 * j >= starts[k]) & (2 * j < starts[k] + counts[k]))
            def _(k=k):
                o_ref[...] += (_dot_nt(src[k][0], w_ref[:, pl.ds(0, SEG)])
                               + _dot_nt(src[k][1], w_ref[:, pl.ds(SEG, SEG)]))

    def src_spec(k):
        return pl.BlockSpec((2, tm, SEG),
                            lambda i, j: (jnp.clip(j - starts[k] // 2, 0, counts[k] // 2 - 1), t0 + i, 0))

    return pl.pallas_call(
        body, grid=(nt, 4), name=name,
        in_specs=[src_spec(k) for k in range(ns)] + [pl.BlockSpec((None, D, 2 * SEG), lambda i, j: (j, 0, 0)),
                                                     pl.BlockSpec(token.shape, lambda i, j: (0, 0))],
        out_specs=pl.BlockSpec((tm, D), lambda i, j: (i, 0)),
        out_shape=jax.ShapeDtypeStruct((S // 2, D), F32),
        compiler_params=_cp(("parallel", "arbitrary")),
    )(*sources, w_all, token)


def _rms_bwd(dh, x2, rinv, gain, dout, part, gx_prev, name):
    S, D = x2.shape
    tm = min(256, S // 2)
    nt = (S // 2) // tm
    t0 = part * nt

    def body(dh_ref, x_ref, r_ref, g_ref, dout_ref, *rest):
        gx_ref, dg_ref = rest[-2:]

        @pl.when(pl.program_id(0) == 0)
        def _():
            dg_ref[...] = jnp.zeros_like(dg_ref)

        dh = dh_ref[...]
        r = r_ref[...]
        xhat = x_ref[...] * r
        dg_ref[...] += jnp.sum(dh * xhat, axis=0, keepdims=True)
        dxn = dh * g_ref[...]
        gx_ref[...] = dout_ref[...] + r * (dxn - xhat * jnp.mean(dxn * xhat, axis=-1, keepdims=True))

    row = lambda i: (t0 + i, 0)
    fix = lambda i: (0, 0)
    in_specs = [pl.BlockSpec((tm, D), lambda i: (i, 0)), pl.BlockSpec((tm, D), row), pl.BlockSpec((tm, 1), row),
                pl.BlockSpec((1, D), fix), pl.BlockSpec((tm, D), row)]
    args = [dh, x2, rinv, gain, dout]
    aliases = {}
    if gx_prev is not None:
        in_specs.append(ANY)
        args.append(gx_prev)
        aliases = {5: 0}
    return pl.pallas_call(
        body, grid=(nt,), name=name, in_specs=in_specs,
        out_specs=[pl.BlockSpec((tm, D), row), pl.BlockSpec((1, D), fix)],
        out_shape=[jax.ShapeDtypeStruct((S, D), F32), jax.ShapeDtypeStruct((1, D), F32)],
        input_output_aliases=aliases, compiler_params=_cp(("arbitrary",)),
    )(*args)


def _grad_w_in(h, sources):
    S, D = h.shape
    SEG = sources[0].shape[2]
    counts, starts = _dz_sources(sources)
    ns = len(sources)
    tk = min(1024, S)

    def body(*refs):
        h_ref = refs[0]
        src = refs[1:1 + ns]
        o_ref = refs[1 + ns]
        j = pl.program_id(0)
        k = pl.program_id(1)

        @pl.when(k == 0)
        def _():
            o_ref[...] = jnp.zeros_like(o_ref)

        for s in range(ns):
            @pl.when((j >= starts[s]) & (j < starts[s] + counts[s]))
            def _(s=s):
                o_ref[...] += _dot_tn(h_ref[...], src[s][...])

    def src_spec(s):
        return pl.BlockSpec((None, tk, SEG),
                            lambda j, k: (jnp.clip(j - starts[s], 0, counts[s] - 1), k, 0))

    return pl.pallas_call(
        body, grid=(8, S // tk), name="grad_w_in",
        in_specs=[pl.BlockSpec((tk, D), lambda j, k: (k, 0))] + [src_spec(s) for s in range(ns)],
        out_specs=pl.BlockSpec((None, D, SEG), lambda j, k: (j // 2, 0, j % 2)),
        out_shape=jax.ShapeDtypeStruct((4, D, 2 * SEG), F32),
        compiler_params=_cp(("parallel", "arbitrary")),
    )(h, *sources)


def _lower_bound(lbl):
    l0 = lbl[0:1, :]
    l1 = lbl[1:2, :]
    m = jnp.maximum(l0, l1)
    e0 = jnp.exp(l0 - m)
    e1 = jnp.exp(l1 - m)
    return e0 / (e0 + e1)


def _tile_masks():
    row = lax.broadcasted_iota(jnp.int32, (HGRN_TILE, HGRN_TILE), 0)
    col = lax.broadcasted_iota(jnp.int32, (HGRN_TILE, HGRN_TILE), 1)
    same = (row // HGRN_CHUNK) == (col // HGRN_CHUNK)
    return same & (row >= col), same & (row <= col)


def _chunk_last(b):
    T = b.shape[0]
    b3 = b.reshape(T // HGRN_CHUNK, HGRN_CHUNK, HGRN_HEAD)
    return jnp.broadcast_to(b3[:, HGRN_CHUNK - 1:HGRN_CHUNK, :], b3.shape).reshape(T, HGRN_HEAD)


def _chunk_sum(x):
    T = x.shape[0]
    x3 = x.reshape(T // HGRN_CHUNK, HGRN_CHUNK, HGRN_HEAD)
    return jnp.broadcast_to(jnp.sum(x3, axis=1, keepdims=True), x3.shape).reshape(T, HGRN_HEAD)


def _hgrn_dims(S, SEG):
    T = min(HGRN_BLOCK, S)
    assert S % T == 0 and T % HGRN_TILE == 0
    tiles = [slice(t * HGRN_TILE, (t + 1) * HGRN_TILE) for t in range(T // HGRN_TILE)]
    chunks = [slice(c * HGRN_CHUNK, (c + 1) * HGRN_CHUNK) for c in range(T // HGRN_CHUNK)]
    return SEG // HGRN_HEAD, T, T // HGRN_CHUNK, S // T, tiles, chunks


def _hgrn_fwd(zf32, lb_logits, gnorm):
    _, NLB, S, _ = zf32.shape
    SEG = NLB * LANES
    H, T, NC, NJ, tiles, chunks = _hgrn_dims(S, SEG)
    HP = HGRN_HEADS_PER_STEP
    assert H % HP == 0

    def body(zq_ref, zf_ref, zi_ref, zg_ref, lbl_ref, gn_ref, y_ref, st_ref, state):
        @pl.when(pl.program_id(1) == 0)
        def _():
            state[...] = jnp.zeros_like(state)

        tril, _ = _tile_masks()
        tril_bf = tril.astype(BF16)
        for hh in range(HP):
            cols = slice(hh * HGRN_HEAD, (hh + 1) * HGRN_HEAD)
            lb = _lower_bound(lbl_ref[:, cols])
            zq = zq_ref[hh]
            q = zq * _sigmoid(zq)
            f = lb + (1.0 - lb) * _sigmoid(zf_ref[hh])
            k = 1.0 - f
            logf = jnp.log(f)
            b = jnp.concatenate([_exact_dot(tril_bf, logf[t]) for t in tiles], axis=0)
            bl = _chunk_last(b)
            qd_b = (q * jnp.exp(b)).astype(BF16)
            kd_b = (k * jnp.exp(-b)).astype(BF16)
            ke_b = (k * jnp.exp(bl - b)).astype(BF16)
            v_b = zi_ref[hh].astype(BF16)
            o_intra = jnp.concatenate(
                [_dot(jnp.where(tril, _dot_nt(qd_b[t], kd_b[t]), 0.0).astype(BF16), v_b[t]) for t in tiles], axis=0)
            kvs = [_dot_tn(v_b[r], ke_b[r]) for r in chunks]
            ebl = jnp.exp(bl)
            st = state[hh]
            sts = []
            for c in range(NC):
                st_ref[c, hh] = st
                sts.append(st.astype(BF16))
                st = st * ebl[c * HGRN_CHUNK:c * HGRN_CHUNK + 1, :] + kvs[c]
            state[hh] = st
            o = o_intra + jnp.concatenate([_dot_nt(qd_b[r], sb) for r, sb in zip(chunks, sts)], axis=0)
            on = o * lax.rsqrt(jnp.mean(o * o, axis=-1, keepdims=True) + NORM_EPS) * gn_ref[...]
            zg = zg_ref[hh]
            y_ref[:, cols] = (on * (zg * _sigmoid(zg))).astype(BF16)

    def zspec(seg):
        return pl.BlockSpec((None, HP, T, HGRN_HEAD), lambda h, j: (seg, h, j, 0))

    return pl.pallas_call(
        body, grid=(H // HP, NJ), name="hgrn_fwd",
        in_specs=[zspec(0), zspec(1), zspec(2), zspec(3),
                  pl.BlockSpec((2, HP * HGRN_HEAD), lambda h, j: (0, h)),
                  pl.BlockSpec((1, HGRN_HEAD), lambda h, j: (0, 0))],
        out_specs=[pl.BlockSpec((T, HP * HGRN_HEAD), lambda h, j: (j, h)),
                   pl.BlockSpec((NC, HP, HGRN_HEAD, HGRN_HEAD), lambda h, j: (j, h, 0, 0))],
        out_shape=[jax.ShapeDtypeStruct((S, SEG), BF16),
                   jax.ShapeDtypeStruct((S // HGRN_CHUNK, H, HGRN_HEAD, HGRN_HEAD), F32)],
        scratch_shapes=[pltpu.VMEM((HP, HGRN_HEAD, HGRN_HEAD), F32)],
        compiler_params=_cp(("parallel", "arbitrary")),
    )(zf32, zf32, zf32, zf32, lb_logits, gnorm)


def _hgrn_bwd(zf32, lb_logits, gnorm, states, dy):
    _, NLB, S, _ = zf32.shape
    SEG = NLB * LANES
    H, T, NC, NJ, tiles, chunks = _hgrn_dims(S, SEG)
    C = HGRN_CHUNK
    HP = HGRN_HEADS_PER_STEP
    assert H % HP == 0

    def body(zq_ref, zf_ref, zi_ref, zg_ref, lbl_ref, gn_ref, st_ref, dy_ref, dz_ref, dl_ref, dgn_ref, gstate):
        @pl.when(pl.program_id(1) == 0)
        def _():
            gstate[...] = jnp.zeros_like(gstate)
            dl_ref[...] = jnp.zeros_like(dl_ref)
            dgn_ref[...] = jnp.zeros_like(dgn_ref)

        gn = gn_ref[...]
        tril, triu = _tile_masks()
        tril_bf = tril.astype(BF16)
        triu_bf = triu.astype(BF16)
        for hh in range(HP):
            cols = slice(hh * HGRN_HEAD, (hh + 1) * HGRN_HEAD)
            lb = _lower_bound(lbl_ref[:, cols])
            q, dq_dz = _silu_and_grad(zq_ref[hh])
            sf = _sigmoid(zf_ref[hh])
            f = lb + (1.0 - lb) * sf
            k = 1.0 - f
            logf = jnp.log(f)
            b = jnp.concatenate([_exact_dot(tril_bf, logf[t]) for t in tiles], axis=0)
            bl = _chunk_last(b)
            eb = jnp.exp(b)
            enb = jnp.exp(-b)
            ekl = jnp.exp(bl - b)
            ebl = jnp.exp(bl)
            qd = q * eb
            kd = k * enb
            ke = k * ekl
            qd_b = qd.astype(BF16)
            kd_b = kd.astype(BF16)
            ke_b = ke.astype(BF16)
            v_b = zi_ref[hh].astype(BF16)
            sts = [st_ref[c, hh] for c in range(NC)]
            sts_b = [s.astype(BF16) for s in sts]
            a_b = [jnp.where(tril, _dot_nt(qd_b[t], kd_b[t]), 0.0).astype(BF16) for t in tiles]
            o = (jnp.concatenate([_dot(a, v_b[t]) for a, t in zip(a_b, tiles)], axis=0)
                 + jnp.concatenate([_dot_nt(qd_b[r], sb) for r, sb in zip(chunks, sts_b)], axis=0))
            rinv = lax.rsqrt(jnp.mean(o * o, axis=-1, keepdims=True) + NORM_EPS)
            ohat = o * rinv
            sg, dsg = _silu_and_grad(zg_ref[hh])
            dyv = dy_ref[:, cols]
            don = dyv * sg
            dz_ref[3, :, cols] = (dyv * (ohat * gn) * dsg).astype(BF16)
            dgn_ref[hh] += jnp.sum(don * ohat, axis=0, keepdims=True)
            dohat = don * gn
            do = rinv * (dohat - ohat * jnp.mean(dohat * ohat, axis=-1, keepdims=True))
            do_b = do.astype(BF16)
            da_b = [jnp.where(tril, _dot_nt(do_b[t], v_b[t]), 0.0).astype(BF16) for t in tiles]
            dv_intra = jnp.concatenate([_dot_tn(a, do_b[t]) for a, t in zip(a_b, tiles)], axis=0)
            dqd_intra = jnp.concatenate([_dot(da, kd_b[t]) for da, t in zip(da_b, tiles)], axis=0)
            dkd = jnp.concatenate([_dot_tn(da, qd_b[t]) for da, t in zip(da_b, tiles)], axis=0)
            dqd_inter = jnp.concatenate([_dot(do_b[r], sb) for r, sb in zip(chunks, sts_b)], axis=0)
            gks = [_dot_tn(do_b[r], qd_b[r]) for r in chunks]
            g = gstate[hh]
            gs = [None] * NC
            for c in reversed(range(NC)):
                gs[c] = g
                g = g * ebl[c * C:c * C + 1, :] + gks[c]
            gstate[hh] = g
            gs_b = [x.astype(BF16) for x in gs]
            dv = dv_intra + jnp.concatenate([_dot_nt(ke_b[r], gb) for r, gb in zip(chunks, gs_b)], axis=0)
            dz_ref[2, :, cols] = dv.astype(BF16)
            dke = jnp.concatenate([_dot(v_b[r], gb) for r, gb in zip(chunks, gs_b)], axis=0)
            debl = jnp.concatenate(
                [jnp.broadcast_to(jnp.sum(x * s, axis=0, keepdims=True), (C, HGRN_HEAD)) for x, s in zip(gs, sts)], axis=0)
            dqd = dqd_intra + dqd_inter
            dz_ref[0, :, cols] = ((dqd * eb) * dq_dz).astype(BF16)
            t_ke = dke * ke
            db = dqd * qd - dkd * kd - t_ke
            db_last = _chunk_sum(t_ke) + debl * ebl
            dk = dkd * enb + dke * ekl
            dlogf = jnp.concatenate([_exact_dot(triu_bf, db[t]) for t in tiles], axis=0) + db_last
            df = dlogf / f - dk
            dz_ref[1, :, cols] = (df * (1.0 - lb) * (sf * (1.0 - sf))).astype(BF16)
            dlb = jnp.sum(df * (1.0 - sf), axis=0, keepdims=True)
            dl0 = dlb * lb * (1.0 - lb)
            dl_ref[0:1, cols] += dl0
            dl_ref[1:2, cols] -= dl0

    def zspec(seg):
        return pl.BlockSpec((None, HP, T, HGRN_HEAD), lambda h, j: (seg, h, NJ - 1 - j, 0))

    return pl.pallas_call(
        body, grid=(H // HP, NJ), name="hgrn_bwd",
        in_specs=[zspec(0), zspec(1), zspec(2), zspec(3),
                  pl.BlockSpec((2, HP * HGRN_HEAD), lambda h, j: (0, h)),
                  pl.BlockSpec((1, HGRN_HEAD), lambda h, j: (0, 0)),
                  pl.BlockSpec((NC, HP, HGRN_HEAD, HGRN_HEAD), lambda h, j: (NJ - 1 - j, h, 0, 0)),
                  pl.BlockSpec((T, HP * HGRN_HEAD), lambda h, j: (NJ - 1 - j, h))],
        out_specs=[pl.BlockSpec((4, T, HP * HGRN_HEAD), lambda h, j: (0, NJ - 1 - j, h)),
                   pl.BlockSpec((2, HP * HGRN_HEAD), lambda h, j: (0, h)),
                   pl.BlockSpec((HP, 1, HGRN_HEAD), lambda h, j: (h, 0, 0))],
        out_shape=[jax.ShapeDtypeStruct((4, S, SEG), BF16), jax.ShapeDtypeStruct((2, SEG), F32),
                   jax.ShapeDtypeStruct((H, 1, HGRN_HEAD), F32)],
        scratch_shapes=[pltpu.VMEM((HP, HGRN_HEAD, HGRN_HEAD), F32)],
        compiler_params=_cp(("parallel", "arbitrary")),
    )(zf32, zf32, zf32, zf32, lb_logits, gnorm, states, dy)


def _alibi_slopes(seg):
    n_heads = seg // ATTN_HEAD
    s = 2.0 ** (-8.0 * np.arange(1, n_heads + 1, dtype=np.float64) / n_heads)
    return jnp.asarray(np.repeat(s, ATTN_HEAD)[None, :], F32)


def _attn_dims(S, SEG, d):
    rb = BAND * d
    assert S % rb == 0 and SEG % LANES == 0
    npb = max(1, min(SEG // LANES, ATTN_BLOCK_ELEMS // (rb * LANES)))
    assert (SEG // LANES) % npb == 0
    return rb, npb, S // rb, (SEG // LANES) // npb


def _res_rows(r, d):
    return pl.ds(0, BAND) if d == 1 else pl.ds(r, BAND, stride=d)


def _for_residues(d, fn):
    if d == 1:
        fn(0)
    else:
        def step(r, carry):
            fn(r)
            return carry
        lax.fori_loop(0, d, step, 0, unroll=ATTN_UNROLL)


def _for_groups(d, n_pairs, fn):
    def over_pairs(r):
        for g0 in range(0, n_pairs, ATTN_UNROLL):
            fn([(r, p) for p in range(g0, min(n_pairs, g0 + ATTN_UNROLL))])

    if d == 1:
        over_pairs(0)
    elif n_pairs >= ATTN_UNROLL:
        def step(r, carry):
            over_pairs(r)
            return carry
        lax.fori_loop(0, d, step, 0)
    else:
        per_group = ATTN_UNROLL // n_pairs
        assert d % per_group == 0

        def step(g, carry):
            fn([(g * per_group + i, p) for i in range(per_group) for p in range(n_pairs)])
            return carry
        lax.fori_loop(0, d // per_group, step, 0)


def _band_terms(n, d):
    i = lax.broadcasted_iota(jnp.int32, (BAND, 2 * BAND), 0)
    jj = lax.broadcasted_iota(jnp.int32, (BAND, 2 * BAND), 1)
    delta = BAND + i - jj
    valid = (delta >= 0) & (delta <= BAND) & ((n > 0) | (jj >= BAND))
    return (-d * delta).astype(F32), valid


def _head_biases(slopes, nd, valid):
    out = []
    for s in _per_head(slopes):
        s2 = jnp.concatenate([s, s], axis=1)
        out.append(jnp.where(valid, s2 * nd, NEG))
    return jnp.concatenate(out, axis=0)


def _stack_heads(x):
    lane = lax.broadcasted_iota(jnp.int32, x.shape, 1)
    zero = jnp.zeros_like(x)
    return jnp.concatenate([jnp.where(lane < ATTN_HEAD, x, zero), jnp.where(lane < ATTN_HEAD, zero, x)], axis=0)


def _unstack_heads(x2):
    first = lax.broadcasted_iota(jnp.int32, (BAND, LANES), 1) < ATTN_HEAD
    return jnp.where(first, x2[:BAND], x2[BAND:])


def _stack_per_head(x):
    a, b = _per_head(x)
    col = jnp.concatenate([a, b], axis=0)
    return jnp.concatenate([col, col], axis=1)


def _per_head(x):
    lane = lax.broadcasted_iota(jnp.int32, x.shape, 1)
    sw = pltpu.roll(x, ATTN_HEAD, 1)
    first = lane < ATTN_HEAD
    return jnp.where(first, x, sw), jnp.where(first, sw, x)


def _attn_fwd(qkv, slopes, d):
    _, NLB, S, _ = qkv.shape
    rb, NP, nb, ncb = _attn_dims(S, NLB * LANES, d)

    def body(q_ref, kp_ref, kc_ref, vp_ref, vc_ref, sl_ref, o_ref, l_ref):
        n = pl.program_id(1)
        nd, valid = _band_terms(n, d)
        biases = [_head_biases(sl_ref[:, p * LANES:(p + 1) * LANES], nd, valid) for p in range(NP)]

        def group(items):
            scores, values = [], []
            for r, p in items:
                rows = _res_rows(r, d)
                kc = jnp.concatenate([kp_ref.at[p][rows, :], kc_ref.at[p][rows, :]], axis=0).astype(BF16)
                values.append(jnp.concatenate([vp_ref.at[p][rows, :], vc_ref.at[p][rows, :]], axis=0).astype(BF16))
                scores.append(_dot_nt(_stack_heads((q_ref.at[p][rows, :] * ATTN_SCALE).astype(BF16)), kc))
            probs = []
            for (r, p), s in zip(items, scores):
                s = s + biases[p]
                m = jnp.max(s, axis=-1, keepdims=True)
                e = jnp.exp(s - m)
                den = jnp.sum(e, axis=-1, keepdims=True)
                probs.append((e.astype(BF16), den, m + jnp.log(den)))
            for (r, p), vc, (e, den, lse) in zip(items, values, probs):
                rows = _res_rows(r, d)
                o_ref.at[p][rows, :] = _unstack_heads(_dot(e, vc) / den)
                l_ref.at[p][rows, :] = _unstack_heads(jnp.broadcast_to(lse, (2 * BAND, LANES)))

        _for_groups(d, NP, group)

    def spec(seg, prev):
        if prev:
            return pl.BlockSpec((None, NP, rb, LANES), lambda c, n: (SEG_QKV + seg, c, jnp.maximum(n - 1, 0), 0))
        return pl.BlockSpec((None, NP, rb, LANES), lambda c, n: (SEG_QKV + seg, c, n, 0))

    out = pl.BlockSpec((NP, rb, LANES), lambda c, n: (c, n, 0))
    return pl.pallas_call(
        body, grid=(ncb, nb), name=f"attn_fwd_d{d}",
        in_specs=[spec(0, False), spec(1, True), spec(1, False), spec(2, True), spec(2, False),
                  pl.BlockSpec((1, NP * LANES), lambda c, n: (0, c))],
        out_specs=[out, out],
        out_shape=[jax.ShapeDtypeStruct((NLB, S, LANES), F32)] * 2,
        compiler_params=_cp(("parallel", "parallel")),
    )(qkv, qkv, qkv, qkv, qkv, slopes)


def _attn_merge(outs, lses, zf32):
    NLB, S, _ = outs[0].shape
    SEG = NLB * LANES
    tm = min(256, S)

    def body(o1, o2, o3, l1, l2, l3, zg_ref, o_ref, lse_ref, y_ref):
        a, b, c = l1[...], l2[...], l3[...]
        m = jnp.maximum(jnp.maximum(a, b), c)
        ea, eb, ec = jnp.exp(a - m), jnp.exp(b - m), jnp.exp(c - m)
        tot = ea + eb + ec
        o = (ea / tot) * o1[...] + (eb / tot) * o2[...] + (ec / tot) * o3[...]
        o_ref[...] = o
        lse_ref[...] = m + jnp.log(tot)
        zg = zg_ref[...]
        y = (o * (zg * _sigmoid(zg))).astype(BF16)
        for p in range(NLB):
            y_ref[:, p * LANES:(p + 1) * LANES] = y[p]

    blk = pl.BlockSpec((NLB, tm, LANES), lambda i: (0, i, 0))
    return pl.pallas_call(
        body, grid=(S // tm,), name="attn_merge",
        in_specs=[blk] * 6 + [pl.BlockSpec((None, NLB, tm, LANES), lambda i: (SEG_GATE_A, 0, i, 0))],
        out_specs=[blk, blk, pl.BlockSpec((tm, SEG), lambda i: (i, 0))],
        out_shape=[jax.ShapeDtypeStruct((NLB, S, LANES), F32), jax.ShapeDtypeStruct((NLB, S, LANES), F32),
                   jax.ShapeDtypeStruct((S, SEG), BF16)],
        compiler_params=_cp(("parallel",)),
    )(*outs, *lses, zf32)


def _attn_gate_bwd(dy, o, zf32):
    NP, S, _ = o.shape
    SEG = NP * LANES
    tm = min(256, S)

    def body(dy_ref, o_ref, zg_ref, do_ref, dl_ref, dzg_ref):
        r = lax.broadcasted_iota(jnp.int32, (LANES, LANES), 0) // ATTN_HEAD
        c = lax.broadcasted_iota(jnp.int32, (LANES, LANES), 1) // ATTN_HEAD
        same_head = (r == c).astype(BF16)
        for p in range(NP):
            cols = slice(p * LANES, (p + 1) * LANES)
            sg, dsg = _silu_and_grad(zg_ref[p])
            dyv = dy_ref[:, cols]
            ov = o_ref[p]
            do = dyv * sg
            do_ref[p] = do
            dzg_ref[:, cols] = (dyv * ov * dsg).astype(BF16)
            dl_ref[p] = _exact_dot_right(do * ov, same_head)

    blk = pl.BlockSpec((NP, tm, LANES), lambda i: (0, i, 0))
    return pl.pallas_call(
        body, grid=(S // tm,), name="attn_gate_bwd",
        in_specs=[pl.BlockSpec((tm, SEG), lambda i: (i, 1)), blk,
                  pl.BlockSpec((None, NP, tm, LANES), lambda i: (SEG_GATE_A, 0, i, 0))],
        out_specs=[blk, blk, pl.BlockSpec((None, tm, SEG), lambda i: (3, i, 0))],
        out_shape=[jax.ShapeDtypeStruct((NP, S, LANES), F32), jax.ShapeDtypeStruct((NP, S, LANES), F32),
                   jax.ShapeDtypeStruct((4, S, SEG), BF16)],
        compiler_params=_cp(("parallel",)),
    )(dy, o, zf32)


def _attn_bwd(qkv, slopes, do, lse, dl, d, acc, into):
    _, NLB, S, _ = qkv.shape
    SEG = NLB * LANES
    rb, NP, nb, ncb = _attn_dims(S, SEG, d)
    has_acc = acc is not None
    out_dtype = F32 if into is None else into.dtype
    assert into is None or d == 1

    def body(*refs):
        q_ref, kp_ref, kc_ref, vp_ref, vc_ref, sl_ref, do_ref, lse_ref, dl_ref = refs[:9]
        acc_ref = refs[9] if has_acc else None
        out_ref, cq, ck, cv = refs[-4:]
        n = pl.program_id(1)

        def emit(r, p, dq, dk, dv):
            rows = _res_rows(r, d)
            for t, val in enumerate((dq, dk, dv)):
                if has_acc:
                    val = val + acc_ref.at[t].at[p][rows, :]
                if into is None:
                    out_ref.at[t].at[p][rows, :] = val.astype(out_dtype)
                else:
                    out_ref.at[t][rows, p * LANES:(p + 1) * LANES] = val.astype(out_dtype)

        @pl.when(n == 0)
        def _():
            cq[...] = jnp.zeros_like(cq)
            ck[...] = jnp.zeros_like(ck)
            cv[...] = jnp.zeros_like(cv)

        @pl.when(n < nb)
        def _():
            nd, valid = _band_terms(n, d)
            biases = [_head_biases(sl_ref[:, p * LANES:(p + 1) * LANES], nd, valid) for p in range(NP)]

            def group(items):
                first = []
                for r, p in items:
                    rows = _res_rows(r, d)
                    kc = jnp.concatenate([kp_ref.at[p][rows, :], kc_ref.at[p][rows, :]], axis=0).astype(BF16)
                    vc = jnp.concatenate([vp_ref.at[p][rows, :], vc_ref.at[p][rows, :]], axis=0).astype(BF16)
                    qs = _stack_heads((q_ref.at[p][rows, :] * ATTN_SCALE).astype(BF16))
                    dos = _stack_heads(do_ref.at[p][rows, :].astype(BF16))
                    first.append((kc, qs, dos, _dot_nt(qs, kc), _dot_nt(dos, vc)))
                second = []
                for (r, p), (kc, qs, dos, s, dp) in zip(items, first):
                    rows = _res_rows(r, d)
                    pr = jnp.exp(s + biases[p] - _stack_per_head(lse_ref.at[p][rows, :]))
                    ds = (pr * (dp - _stack_per_head(dl_ref.at[p][rows, :]))).astype(BF16)
                    second.append((kc, qs, dos, pr.astype(BF16), ds))
                for (r, p), (kc, qs, dos, pr, ds) in zip(items, second):
                    dq = _unstack_heads(_dot(ds, kc)) * ATTN_SCALE
                    dk = _dot_tn(ds, qs)
                    dv = _dot_tn(pr, dos)
                    emit(r, p, cq[r, p], ck[r, p] + dk[:BAND, :], cv[r, p] + dv[:BAND, :])
                    cq[r, p] = dq
                    ck[r, p] = dk[BAND:, :]
                    cv[r, p] = dv[BAND:, :]

            _for_groups(d, NP, group)

        @pl.when(n == nb)
        def _():
            def last(r):
                for p in range(NP):
                    emit(r, p, cq[r, p], ck[r, p], cv[r, p])
            _for_residues(d, last)

    cur = lambda c, n: (c, jnp.minimum(n, nb - 1), 0)
    lag = lambda c, n: (0, c, jnp.clip(n - 1, 0, nb - 1), 0)

    def spec(seg, prev):
        if prev:
            return pl.BlockSpec((None, NP, rb, LANES), lambda c, n: (SEG_QKV + seg, c, jnp.clip(n - 1, 0, nb - 1), 0))
        return pl.BlockSpec((None, NP, rb, LANES), lambda c, n: (SEG_QKV + seg, c, jnp.minimum(n, nb - 1), 0))

    in_specs = [spec(0, False), spec(1, True), spec(1, False), spec(2, True), spec(2, False),
                pl.BlockSpec((1, NP * LANES), lambda c, n: (0, c))] + [pl.BlockSpec((NP, rb, LANES), cur)] * 3
    args = [qkv, qkv, qkv, qkv, qkv, slopes, do, lse, dl]
    aliases = {}
    if has_acc:
        in_specs.append(pl.BlockSpec((3, NP, rb, LANES), lag))
        args.append(acc)
        if into is None:
            aliases = {9: 0}
    if into is None:
        out_sds = jax.ShapeDtypeStruct((3, NLB, S, LANES), F32)
        out_spec = pl.BlockSpec((3, NP, rb, LANES), lag)
    else:
        in_specs.append(ANY)
        args.append(into)
        aliases = {len(args) - 1: 0}
        out_sds = jax.ShapeDtypeStruct(into.shape, into.dtype)
        out_spec = pl.BlockSpec((3, rb, NP * LANES), lambda c, n: (0, jnp.clip(n - 1, 0, nb - 1), c))
    return pl.pallas_call(
        body, grid=(ncb, nb + 1), name=f"attn_bwd_d{d}",
        in_specs=in_specs, out_specs=out_spec, out_shape=out_sds,
        scratch_shapes=[pltpu.VMEM((d, NP, BAND, LANES), F32)] * 3,
        input_output_aliases=aliases,
        compiler_params=_cp(("parallel", "arbitrary")),
    )(*args)


def _adamw(w, g, m, v, name):
    R, C = w.shape
    tr = R if R <= 256 else 256
    assert R % tr == 0

    def body(w_ref, g_ref, m_ref, v_ref, d_ref, nm_ref, nv_ref):
        g = g_ref[...]
        nm = ADAM_B1 * m_ref[...] + (1.0 - ADAM_B1) * g
        nv = ADAM_B2 * v_ref[...] + (1.0 - ADAM_B2) * (g * g)
        m_hat = nm / (1.0 - ADAM_B1 ** ADAM_STEP)
        v_hat = nv / (1.0 - ADAM_B2 ** ADAM_STEP)
        d_ref[...] = -ADAM_LR * (m_hat / (jnp.sqrt(v_hat) + ADAM_EPS) + ADAM_WD * w_ref[...])
        nm_ref[...] = nm
        nv_ref[...] = nv

    blk = pl.BlockSpec((tr, C), lambda i: (i, 0))
    sds = jax.ShapeDtypeStruct((R, C), F32)
    return pl.pallas_call(
        body, grid=(R // tr,), name=name, in_specs=[blk] * 4, out_specs=[blk] * 3, out_shape=[sds] * 3,
        compiler_params=_cp(("parallel",)),
    )(w, g, m, v)


def _coords():
    return lax.axis_index("x"), lax.axis_index("y"), lax.axis_index("c")


def _other_chips(x, y):
    return [(1 - x, y), (x, 1 - y), (1 - x, 1 - y)]


ANY = pl.BlockSpec(memory_space=pl.ANY)


def _cast_into_slot(w, where, name):
    R, C = w.shape
    tr = min(256, R)

    def body(where_ref, w_ref, o_ref):
        o_ref[...] = w_ref[...].astype(BF16)

    grid_spec = pltpu.PrefetchScalarGridSpec(
        num_scalar_prefetch=1, grid=(R // tr,),
        in_specs=[pl.BlockSpec((tr, C), lambda i, w: (i, 0))],
        out_specs=pl.BlockSpec((None, tr, C), lambda i, w: (w[1], i, 0)))
    return pl.pallas_call(
        body, grid_spec=grid_spec, name=name, out_shape=jax.ShapeDtypeStruct((4, R, C), BF16),
        compiler_params=_cp(("parallel",)),
    )(where, w)


def _pair_sum(g, sib, where, name):
    _, n2, C = g.shape
    N = n2 // 2
    tr = min(256, N)
    nt = N // tr

    def body(where_ref, g_ref, s_ref, qb_ref, own_ref):
        q = pl.program_id(1)
        tot = g_ref[...] + s_ref[...]
        qb_ref[...] = tot.astype(BF16)

        @pl.when(q == where_ref[1])
        def _():
            own_ref[...] = tot

    grid_spec = pltpu.PrefetchScalarGridSpec(
        num_scalar_prefetch=1, grid=(nt, 4),
        in_specs=[pl.BlockSpec((None, tr, C), lambda i, q, w: (q, w[0] * nt + i, 0)),
                  pl.BlockSpec((None, tr, C), lambda i, q, w: (q, i, 0))],
        out_specs=[pl.BlockSpec((None, tr, C), lambda i, q, w: (q, i, 0)),
                   pl.BlockSpec((tr, C), lambda i, q, w: (i, 0))])
    return pl.pallas_call(
        body, grid_spec=grid_spec, name=name,
        out_shape=[jax.ShapeDtypeStruct((4, N, C), BF16), jax.ShapeDtypeStruct((N, C), F32)],
        compiler_params=_cp(("parallel", "arbitrary")),
    )(where, g, sib)


HBM = pl.BlockSpec(memory_space=pltpu.HBM)
SEM = pl.BlockSpec(memory_space=pltpu.SEMAPHORE)


def _in_hbm(a):
    return pltpu.with_memory_space_constraint(a, pltpu.HBM)


def _split_start(name, copies, arrays, n_sems, after=None):
    n = len(arrays)

    def body(*refs):
        for cp in copies(refs[:n], refs[-n - 3], refs[-n - 2]):
            cp.start()
        refs[-1][...] = jnp.zeros_like(refs[-1])

    ordered = () if after is None else (after,)
    outs = pl.pallas_call(
        body, name=name,
        out_shape=(pltpu.SemaphoreType.DMA((n_sems,)), pltpu.SemaphoreType.DMA((n_sems,)),
                   *[pltpu.HBM(a.shape, a.dtype) for a in arrays], jax.ShapeDtypeStruct((8, LANES), F32)),
        in_specs=(HBM,) * n + (ANY,) * len(ordered),
        out_specs=(SEM, SEM) + (HBM,) * n + (pl.BlockSpec(memory_space=pltpu.VMEM),),
        input_output_aliases={i: 2 + i for i in range(n)},
        compiler_params=pltpu.CompilerParams(has_side_effects=pltpu.SideEffectType.DATAFLOW_SIDE_EFFECTING),
    )(*[_in_hbm(a) for a in arrays], *ordered)
    return outs[0], outs[1], list(outs[2:2 + n]), outs[-1]


def _split_wait(name, copies, send_sems, recv_sems, arrays, after):
    n = len(arrays)

    def body(*refs):
        for cp in copies(refs[:n], refs[n], refs[n + 1]):
            cp.wait_send()
            cp.wait_recv()

    outs = pl.pallas_call(
        body, name=name,
        out_shape=tuple(pltpu.HBM(a.shape, a.dtype) for a in arrays),
        in_specs=(HBM,) * n + (SEM, SEM, ANY), out_specs=(HBM,) * n,
        input_output_aliases={i: i for i in range(n)},
        compiler_params=pltpu.CompilerParams(has_side_effects=pltpu.SideEffectType.DATAFLOW_SIDE_EFFECTING),
    )(*arrays, send_sems, recv_sems, after)
    return list(outs)


def _remote(src, dst, sems, k, to):
    send_sems, recv_sems = sems
    return pltpu.make_async_remote_copy(src_ref=src, dst_ref=dst, send_sem=send_sems.at[k], recv_sem=recv_sems.at[k],
                                        device_id=to, device_id_type=MESH)


def _chip_at(x, y, rel):
    px = 1 - x if rel & 2 else x
    py = 1 - y if rel & 1 else y
    return px, py, 2 * px + py


def _gather_in_copies(rels):
    def copies(refs, send_sems, recv_sems):
        (w,) = refs
        x, y, c = _coords()
        seg = w.shape[2] // 2
        mine = w.at[2 * x + y, :, pl.ds(c * seg, seg)]
        return [_remote(mine, mine, (send_sems, recv_sems), k, _chip_at(x, y, rel)[:2] + (c,))
                for k, rel in enumerate(rels)]
    return copies


def _gather_out_copies(refs, send_sems, recv_sems):
    (w,) = refs
    x, y, c = _coords()
    mine = w.at[2 * x + y]
    return [_remote(mine, mine, (send_sems, recv_sems), k, (px, py, c)) for k, (px, py) in enumerate(_other_chips(x, y))]


def _swap_copies(refs, send_sems, recv_sems):
    gi, go, si, so = refs
    x, y, c = _coords()
    cps = []
    for a, (src, dst) in enumerate(((gi, si), (go, so))):
        nr = dst.shape[1]
        cps.append(_remote(src.at[:, pl.ds((1 - c) * nr, nr), :], dst, (send_sems, recv_sems), a, (x, y, 1 - c)))
    return cps


def _scatter_copies(refs, send_sems, recv_sems):
    qi, qo, ri, ro = refs
    x, y, c = _coords()
    cps = []
    for k, (px, py) in enumerate(_other_chips(x, y)):
        for a, (src, dst) in enumerate(((qi, ri), (qo, ro))):
            cps.append(_remote(src.at[2 * px + py], dst.at[k], (send_sems, recv_sems), 2 * k + a, (px, py, c)))
    return cps


def _forward_copies(rels):
    def copies(refs, send_sems, recv_sems):
        (w,) = refs
        x, y, c = _coords()
        seg = w.shape[2] // 2
        cps = []
        for k, rel in enumerate(rels):
            got = w.at[_chip_at(x, y, rel)[2], :, pl.ds(c * seg, seg)]
            cps.append(_remote(got, got, (send_sems, recv_sems), k, (x, y, 1 - c)))
        return cps
    return copies


def _chip_sum(own, got, where, name):
    N, C = own.shape
    tr = min(256, N)
    nt = N // tr

    def body(where_ref, own_ref, got_ref, o_ref):
        t = own_ref[...]
        for k in range(3):
            t = t + got_ref[k].astype(F32)
        o_ref[...] = t

    grid_spec = pltpu.PrefetchScalarGridSpec(
        num_scalar_prefetch=1, grid=(nt,),
        in_specs=[pl.BlockSpec((tr, C), lambda i, w: (i, 0)), pl.BlockSpec((3, tr, C), lambda i, w: (0, i, 0))],
        out_specs=pl.BlockSpec((tr, C), lambda i, w: (w[0] * nt + i, 0)))
    return pl.pallas_call(
        body, grid_spec=grid_spec, name=name, out_shape=jax.ShapeDtypeStruct((2 * N, C), F32),
        compiler_params=_cp(("parallel",)),
    )(where, own, got)


def _join_halves(gi, go):
    def body(gi_in, go_in, gi_ref, go_ref, send_sems, recv_sems):
        x, y, c = _coords()
        cps = []
        for a, ref in enumerate((gi_ref, go_ref)):
            nr = ref.shape[0] // 2
            mine = ref.at[pl.ds(c * nr, nr), :]
            cp = pltpu.make_async_remote_copy(src_ref=mine, dst_ref=mine, send_sem=send_sems.at[a],
                                              recv_sem=recv_sems.at[a], device_id=(x, y, 1 - c), device_id_type=MESH)
            cp.start()
            cps.append(cp)
        for a, ref in enumerate((gi_ref, go_ref)):
            nr = ref.shape[0] // 2
            theirs = ref.at[pl.ds((1 - c) * nr, nr), :]
            pltpu.make_async_remote_copy(src_ref=theirs, dst_ref=theirs, send_sem=send_sems.at[a],
                                         recv_sem=recv_sems.at[a], device_id=(x, y, 1 - c),
                                         device_id_type=MESH).wait_recv()
        for cp in cps:
            cp.wait_send()

    return pl.pallas_call(
        body, name="join_halves", in_specs=[ANY, ANY], out_specs=[ANY, ANY],
        out_shape=[jax.ShapeDtypeStruct(gi.shape, F32), jax.ShapeDtypeStruct(go.shape, F32)],
        scratch_shapes=[pltpu.SemaphoreType.DMA((2,)), pltpu.SemaphoreType.DMA((2,))],
        input_output_aliases={0: 0, 1: 1},
    )(gi, go)


def _all_reduce_small(part):
    R, C = part.shape

    def body(p_ref, o_ref, slots, send_sems, recv_sems):
        x, y, c = _coords()
        me = 4 * x + 2 * y + c
        slots[me] = p_ref[...]
        cps = []
        for k in range(1, 8):
            fx, fy, fc = (k >> 2) & 1, (k >> 1) & 1, k & 1
            peer = (1 - x if fx else x, 1 - y if fy else y, 1 - c if fc else c)
            cp = pltpu.make_async_remote_copy(src_ref=p_ref, dst_ref=slots.at[me], send_sem=send_sems.at[k - 1],
                                              recv_sem=recv_sems.at[k - 1], device_id=peer, device_id_type=MESH)
            cp.start()
            cps.append(cp)
        for cp in cps:
            cp.wait()
        t = slots[0]
        for k in range(1, 8):
            t = t + slots[k]
        o_ref[...] = t

    vm = pl.BlockSpec(memory_space=pltpu.VMEM)
    return pl.pallas_call(
        body, name="all_reduce_small", in_specs=[vm], out_specs=vm,
        out_shape=jax.ShapeDtypeStruct((R, C), F32),
        scratch_shapes=[pltpu.VMEM((8, R, C), F32), pltpu.SemaphoreType.DMA((7,)), pltpu.SemaphoreType.DMA((7,))],
    )(part)


def _mixers_forward(z, lb_logits, hgrn_gnorm):
    slopes = _alibi_slopes(z.shape[1] * LANES)
    yh, states = _hgrn_fwd(z, lb_logits, hgrn_gnorm)
    outs, lses = [], []
    for d in DILATIONS:
        o, l = _attn_fwd(z, slopes, d)
        outs.append(o)
        lses.append(l)
    o_attn, lse, ya = _attn_merge(outs, lses, z)
    return yh, ya, (states, o_attn, lse, slopes)


def _backward_to_dz(z, kept, lb_logits, hgrn_gnorm, yh, ya, w_out_all, x2, tgt, fgain, h):
    states, o_attn, lse, slopes = kept
    dout, doutb, loss, dfg = _out_proj_loss(yh, ya, w_out_all, x2, tgt, fgain)
    dy = _dy_proj(doutb, w_out_all)
    g_w_out = _grad_w_out(yh, ya, doutb)
    dzh, dlogits, dgn = _hgrn_bwd(z, lb_logits, hgrn_gnorm, states, dy)
    do, dl, dza = _attn_gate_bwd(dy, o_attn, z)
    acc = None
    order = sorted(DILATIONS, reverse=True)
    for d in order[:-1]:
        acc = _attn_bwd(z, slopes, do, lse, dl, d, acc, None)
    dza = _attn_bwd(z, slopes, do, lse, dl, order[-1], acc, dza)
    sources = [dzh, dza]
    g_w_in = _grad_w_in(h, sources)
    return loss, dfg, dlogits, dgn, g_w_out, g_w_in, sources, dout


def _grad_x_half(sources, w_all, x2, rinv, norm_gain, dout, token, part, gx_prev):
    dh = _dh_proj(sources, w_all, token, part, f"dh_proj_{part}")
    return _rms_bwd(dh, x2, rinv, norm_gain, dout, part, gx_prev, f"rms_bwd_{part}")


def _local_step(x2, tgt, norm_gain, w_all, lb_logits, hgrn_gnorm, w_out_all, fgain):
    token = jnp.zeros((8, LANES), F32)
    where = jnp.zeros((2,), jnp.int32)
    h, rinv = _rms_fwd(x2, norm_gain, token)
    z = _in_proj(h, w_all, where, [(rel, half) for rel in range(4) for half in range(2)], None, token, "in_proj_all")
    yh, ya, kept = _mixers_forward(z, lb_logits, hgrn_gnorm)
    loss, dfg, dlogits, dgn, g_w_out, g_w_in, sources, dout = _backward_to_dz(
        z, kept, lb_logits, hgrn_gnorm, yh, ya, w_out_all, x2, tgt, fgain, h)
    gx, dg0 = _grad_x_half(sources, w_all, x2, rinv, norm_gain, dout, token, 0, None)
    gx, dg1 = _grad_x_half(sources, w_all, x2, rinv, norm_gain, dout, token, 1, gx)
    return loss, gx, dg0 + dg1, g_w_in, dlogits, dgn, g_w_out, dfg


def _pack_small(D, loss, dgain, dlogits, dgn, dfg):
    def row(v):
        v = v.reshape(1, -1)
        return jnp.pad(v, ((0, 0), (0, D - v.shape[1])))
    rows = [row(dgain), row(dfg), row(dlogits[0]), row(dlogits[1]), row(jnp.sum(dgn, axis=0)), row(loss)]
    rows += [jnp.zeros((1, D), F32)] * (8 - len(rows))
    return jnp.concatenate(rows, axis=0)


def kernel(x, norm_gain, w_in, lb_logits, hgrn_gnorm, w_out, final_gain, loss_target, m_norm_gain, m_w_in, m_lb_logits, m_hgrn_gnorm, m_w_out, m_final_gain, v_norm_gain, v_w_in, v_lb_logits, v_hgrn_gnorm, v_w_out, v_final_gain):
    _, S, D = x.shape
    SEG = w_in.shape[2] // 2
    x2 = x[0]
    tgt = loss_target[0]
    fgain = final_gain.reshape(1, D)
    where = jnp.stack([lax.axis_index("c"), 2 * lax.axis_index("x") + lax.axis_index("y")]).astype(jnp.int32)

    wia = _cast_into_slot(w_in[0], where, "cast_w_in")
    woa = _cast_into_slot(w_out[0], where, "cast_w_out")
    near, far = (2, 1), (3,)
    ga = _split_start("gather_near_start", _gather_in_copies(near), [wia], 2)
    h, rinv = _rms_fwd(x2, norm_gain, ga[3])
    z = _in_proj(h, ga[2][0], where, [(0, 0), (0, 1)], None, ga[3], "in_proj_own")
    (wia,) = _split_wait("gather_near_wait", _gather_in_copies(near), ga[0], ga[1], ga[2], z)
    gb = _split_start("gather_far_start", _gather_in_copies(far), [wia], 1)
    fa = _split_start("forward_near_start", _forward_copies(near), gb[2], 2, after=gb[3])
    z = _in_proj(h, fa[2][0], where, [(2, "mine"), (1, "mine")], z, fa[3], "in_proj_near")
    (wia,) = _split_wait("forward_near_wait", _forward_copies(near), fa[0], fa[1], fa[2], z)
    (wia,) = _split_wait("gather_far_wait", _gather_in_copies(far), gb[0], gb[1], [wia], z)
    out_sems = _split_start("gather_out_start", _gather_out_copies, [woa], 3, after=wia)
    fb = _split_start("forward_far_start", _forward_copies(far), [wia], 1, after=out_sems[3])
    z = _in_proj(h, fb[2][0], where, [(3, "mine"), (2, "sibling"), (1, "sibling")], z, fb[3], "in_proj_far")
    (wia,) = _split_wait("forward_far_wait", _forward_copies(far), fb[0], fb[1], fb[2], z)
    z = _in_proj(h, wia, where, [(3, "sibling")], z, fb[3], "in_proj_last")
    yh, ya, kept = _mixers_forward(z, lb_logits, hgrn_gnorm)
    (woa,) = _split_wait("gather_out_wait", _gather_out_copies, out_sems[0], out_sems[1], out_sems[2], ya)
    w_out_all = woa.reshape(2 * SEG, D)

    loss, dfg, dlogits, dgn, g_w_out, g_w_in, sources, dout = _backward_to_dz(
        z, kept, lb_logits, hgrn_gnorm, yh, ya, w_out_all, x2, tgt, fgain, h)

    sib_i = lax.empty((4, g_w_in.shape[1] // 2, g_w_in.shape[2]), F32)
    sib_o = lax.empty((4, g_w_out.shape[1] // 2, g_w_out.shape[2]), F32)
    sems = _split_start("swap_start", _swap_copies, [g_w_in, g_w_out, sib_i, sib_o], 2)
    grad_x, dg0 = _grad_x_half(sources, wia, x2, rinv, norm_gain, dout, sems[3], 0, None)
    g_w_in, g_w_out, sib_i, sib_o = _split_wait("swap_wait", _swap_copies, sems[0], sems[1], sems[2], grad_x)
    qi, own_i = _pair_sum(g_w_in, sib_i, where, "pair_sum_w_in")
    qo, own_o = _pair_sum(g_w_out, sib_o, where, "pair_sum_w_out")
    ri = lax.empty((3,) + qi.shape[1:], BF16)
    ro = lax.empty((3,) + qo.shape[1:], BF16)
    sems = _split_start("scatter_start", _scatter_copies, [qi, qo, ri, ro], 6)
    grad_x, dg1 = _grad_x_half(sources, wia, x2, rinv, norm_gain, dout, sems[3], 1, grad_x)
    _, _, got_i, got_o = _split_wait("scatter_wait", _scatter_copies, sems[0], sems[1], sems[2], grad_x)
    grad_w_in, grad_w_out = _join_halves(_chip_sum(own_i, got_i, where, "chip_sum_w_in"),
                                         _chip_sum(own_o, got_o, where, "chip_sum_w_out"))

    small = _all_reduce_small(_pack_small(D, loss, dg0 + dg1, dlogits, dgn, dfg))
    grad_norm_gain = small[0:1, :]
    grad_final_gain = small[1:2, :]
    grad_lb_logits = small[2:4, :SEG]
    grad_hgrn_gnorm = small[4:5, :HGRN_HEAD]
    loss_sum = small[5, 0]

    d_ng, m_ng, v_ng = _adamw(norm_gain, grad_norm_gain, m_norm_gain, v_norm_gain, "adamw_norm_gain")
    d_wi, m_wi, v_wi = _adamw(w_in[0], grad_w_in, m_w_in[0], v_w_in[0], "adamw_w_in")
    d_lb, m_lb, v_lb = _adamw(lb_logits, grad_lb_logits, m_lb_logits, v_lb_logits, "adamw_lb_logits")
    d_gn, m_gn, v_gn = _adamw(hgrn_gnorm, grad_hgrn_gnorm, m_hgrn_gnorm, v_hgrn_gnorm, "adamw_hgrn_gnorm")
    d_wo, m_wo, v_wo = _adamw(w_out[0], grad_w_out, m_w_out[0], v_w_out[0], "adamw_w_out")
    d_fg, m_fg, v_fg = _adamw(fgain, grad_final_gain, m_final_gain.reshape(1, D), v_final_gain.reshape(1, D),
                              "adamw_final_gain")

    return (loss_sum, grad_x[None],
            grad_norm_gain, grad_w_in[None], grad_lb_logits, grad_hgrn_gnorm, grad_w_out[None], grad_final_gain[0],
            d_ng, d_wi[None], d_lb, d_gn, d_wo[None], d_fg[0],
            m_ng, m_wi[None], m_lb, m_gn, m_wo[None], m_fg[0],
            v_ng, v_wi[None], v_lb, v_gn, v_wo[None], v_fg[0])
```

```python
import jax
import jax.numpy as jnp
import numpy as np
from jax import lax
from jax.experimental import pallas as pl
from jax.experimental.pallas import tpu as pltpu

F32 = jnp.float32
BF16 = jnp.bfloat16
MESH = pl.DeviceIdType.MESH

NORM_EPS = 1e-6
HGRN_HEAD = 128
HGRN_CHUNK = 64
HGRN_TILE = 128
HGRN_BLOCK = 512
HGRN_HEADS_PER_STEP = 4
ATTN_HEAD = 64
LANES = 128
BAND = 128
DILATIONS = (1, 4, 16)
DEINTERLEAVE = 16
ATTN_SCALE = ATTN_HEAD ** -0.5
assert ATTN_SCALE == 0.125
ATTN_BLOCK_ELEMS = BAND * 2048
ATTN_UNROLL = 4
SEG_QKV = 4
SEG_GATE_A = 7
NEG = -1e30

ADAM_LR = 0.001
ADAM_B1 = 0.9
ADAM_B2 = 0.999
ADAM_EPS = 1e-08
ADAM_WD = 0.01
ADAM_STEP = 10

MIB = 1024 * 1024


def _cp(semantics=None, vmem_mib=48):
    return pltpu.CompilerParams(dimension_semantics=semantics, vmem_limit_bytes=vmem_mib * MIB)


def _dot(a, b):
    return jnp.dot(a, b, preferred_element_type=F32)


def _dot_nt(a, b):
    return lax.dot_general(a, b, (((1,), (1,)), ((), ())), preferred_element_type=F32)


def _dot_tn(a, b):
    return lax.dot_general(a, b, (((0,), (0,)), ((), ())), preferred_element_type=F32)


def _split3(x):
    hi = x.astype(BF16)
    r1 = x - hi.astype(F32)
    mid = r1.astype(BF16)
    lo = (r1 - mid.astype(F32)).astype(BF16)
    return hi, mid, lo


def _exact_dot(t_bf16, x):
    hi, mid, lo = _split3(x)
    return _dot(t_bf16, hi) + _dot(t_bf16, mid) + _dot(t_bf16, lo)


def _exact_dot_right(x, t_bf16):
    hi, mid, lo = _split3(x)
    return _dot(hi, t_bf16) + _dot(mid, t_bf16) + _dot(lo, t_bf16)


def _sigmoid(z):
    return jax.nn.sigmoid(z)


def _silu_and_grad(z):
    s = _sigmoid(z)
    return z * s, s * (1.0 + z * (1.0 - s))


def _seg_select(j, values):
    out = values[0]
    for t, v in enumerate(values[1:], 1):
        out = jnp.where(j == t, v, out)
    return out


def _rms_fwd(x2, gain, token):
    S, D = x2.shape
    tm = min(512, S)

    def body(x_ref, g_ref, _, h_ref, r_ref):
        x = x_ref[...]
        r = lax.rsqrt(jnp.mean(x * x, axis=-1, keepdims=True) + NORM_EPS)
        h_ref[...] = ((x * r) * g_ref[...]).astype(BF16)
        r_ref[...] = r

    return pl.pallas_call(
        body, grid=(S // tm,), name="rms_fwd",
        in_specs=[pl.BlockSpec((tm, D), lambda i: (i, 0)), pl.BlockSpec((1, D), lambda i: (0, 0)),
                  pl.BlockSpec(token.shape, lambda i: (0, 0))],
        out_specs=[pl.BlockSpec((tm, D), lambda i: (i, 0)), pl.BlockSpec((tm, 1), lambda i: (i, 0))],
        out_shape=[jax.ShapeDtypeStruct((S, D), BF16), jax.ShapeDtypeStruct((S, 1), F32)],
        compiler_params=_cp(("parallel",)),
    )(x2, gain, token)


def _in_proj(h, w_all, where, segs, z_prev, token, name):
    S, D = h.shape
    SEG = w_all.shape[2] // 2
    NLB = SEG // LANES
    tm = min(512, S)
    count = len(segs)
    DI = DEINTERLEAVE
    tu = tm // DI

    def is_qkv(seg):
        return (seg >= SEG_QKV) & (seg < SEG_QKV + 3)

    def seg_of(j, w):
        halves = {0: 0, 1: 1, "mine": w[0], "sibling": 1 - w[0]}
        cands = [2 * jnp.bitwise_xor(w[1], rel) + halves[half] for rel, half in segs]
        keys = [is_qkv(s).astype(jnp.int32) for s in cands]
        out = cands[0]
        for k in range(count):
            pos = (sum(jnp.where(keys[t] < keys[k], 1, 0) for t in range(count))
                   + sum(jnp.where(keys[t] == keys[k], 1, 0) for t in range(k)))
            out = jnp.where(pos == j, cands[k], out)
        return out

    def body(*refs):
        where_ref, h_ref, w_ref = refs[:3]
        o_ref, o16_ref = refs[-2:]
        res = _dot(h_ref[...], w_ref[...])
        for p in range(NLB):
            o_ref[p] = res[:, p * LANES:(p + 1) * LANES]

        @pl.when(is_qkv(seg_of(pl.program_id(0), where_ref)))
        def _():
            for p in range(NLB):
                for r in range(DI):
                    o16_ref[p, r] = o_ref.at[p][pl.ds(r, tu, stride=DI), :]

    def z16_map(j, i, w):
        seg = seg_of(j, w)
        return (jnp.where(is_qkv(seg), seg - SEG_QKV, 3), 0, 0, jnp.where(is_qkv(seg), i, 0), 0)

    in_specs = [pl.BlockSpec((tm, D), lambda j, i, w: (i, 0)),
                pl.BlockSpec((None, D, SEG), lambda j, i, w: (seg_of(j, w) // 2, 0, seg_of(j, w) % 2)),
                pl.BlockSpec(token.shape, lambda j, i, w: (0, 0))]
    args = [where, h, w_all, token]
    aliases = {}
    if z_prev is not None:
        in_specs += [ANY, ANY]
        args += list(z_prev)
        aliases = {4: 0, 5: 1}
    grid_spec = pltpu.PrefetchScalarGridSpec(
        num_scalar_prefetch=1, grid=(count, S // tm), in_specs=in_specs,
        out_specs=[pl.BlockSpec((None, NLB, tm, LANES), lambda j, i, w: (seg_of(j, w), 0, i, 0)),
                   pl.BlockSpec((None, NLB, DI, tu, LANES), z16_map)])
    return pl.pallas_call(
        body, grid_spec=grid_spec, name=name,
        out_shape=[jax.ShapeDtypeStruct((8, NLB, S, LANES), F32),
                   jax.ShapeDtypeStruct((4, NLB, DI, S // DI, LANES), F32)],
        input_output_aliases=aliases, compiler_params=_cp(("parallel", "parallel")),
    )(*args)


def _out_proj_loss(yh, ya, w_out, x2, tgt, fgain):
    S, D = x2.shape
    SEG = yh.shape[1]
    tm = min(256, S)
    parts = 2

    def body(yh_ref, ya_ref, w_ref, x_ref, t_ref, fg_ref, dout_ref, doutb_ref, loss_ref, dfg_ref):
        i = pl.program_id(0)

        @pl.when(i == 0)
        def _():
            loss_ref[...] = jnp.zeros_like(loss_ref)
            dfg_ref[...] = jnp.zeros_like(dfg_ref)

        fg = fg_ref[...]
        loss = jnp.zeros((1, 1), F32)
        dfg = jnp.zeros((1, D), F32)
        for rows in [pl.ds(p * (tm // parts), tm // parts) for p in range(parts)]:
            out = (x_ref[rows, :] + _dot(yh_ref[rows, :], w_ref[pl.ds(0, SEG), :])
                   + _dot(ya_ref[rows, :], w_ref[pl.ds(SEG, SEG), :]))
            r = lax.rsqrt(jnp.mean(out * out, axis=-1, keepdims=True) + NORM_EPS)
            n = out * r
            err = n * fg - t_ref[rows, :]
            loss = loss + 0.5 * jnp.sum(jnp.mean(err * err, axis=-1, keepdims=True), axis=0, keepdims=True)
            dy = err * (1.0 / D)
            dfg = dfg + jnp.sum(dy * n, axis=0, keepdims=True)
            dn = dy * fg
            dout = r * (dn - n * jnp.mean(dn * n, axis=-1, keepdims=True))
            dout_ref[rows, :] = dout
            doutb_ref[rows, :] = dout.astype(BF16)
        loss_ref[...] += loss
        dfg_ref[...] += dfg

    row = lambda i: (i, 0)
    fix = lambda i: (0, 0)
    return pl.pallas_call(
        body, grid=(S // tm,), name="out_proj_loss",
        in_specs=[pl.BlockSpec((tm, SEG), row), pl.BlockSpec((tm, SEG), row), pl.BlockSpec((2 * SEG, D), fix),
                  pl.BlockSpec((tm, D), row), pl.BlockSpec((tm, D), row), pl.BlockSpec((1, D), fix)],
        out_specs=[pl.BlockSpec((tm, D), row), pl.BlockSpec((tm, D), row), pl.BlockSpec((1, 1), fix),
                   pl.BlockSpec((1, D), fix)],
        out_shape=[jax.ShapeDtypeStruct((S, D), F32), jax.ShapeDtypeStruct((S, D), BF16),
                   jax.ShapeDtypeStruct((1, 1), F32), jax.ShapeDtypeStruct((1, D), F32)],
        compiler_params=_cp(("arbitrary",)),
    )(yh, ya, w_out, x2, tgt, fgain)


def _dy_proj(doutb, w_out):
    S, D = doutb.shape
    K = w_out.shape[0]
    tm = min(512, S)

    def body(d_ref, w_ref, o_ref):
        o_ref[...] = _dot_nt(d_ref[...], w_ref[...])

    return pl.pallas_call(
        body, grid=(S // tm,), name="dy_proj",
        in_specs=[pl.BlockSpec((tm, D), lambda i: (i, 0)), pl.BlockSpec((K, D), lambda i: (0, 0))],
        out_specs=pl.BlockSpec((tm, K), lambda i: (i, 0)),
        out_shape=jax.ShapeDtypeStruct((S, K), F32),
        compiler_params=_cp(("parallel",)),
    )(doutb, w_out)


def _grad_w_out(yh, ya, doutb):
    S, SEG = yh.shape
    D = doutb.shape[1]
    R = (2 * SEG) // 4
    nb_half = SEG // R
    tk = min(512, S)

    def body(yh_ref, ya_ref, d_ref, o_ref):
        q = pl.program_id(0)
        k = pl.program_id(1)

        @pl.when(k == 0)
        def _():
            o_ref[...] = jnp.zeros_like(o_ref)

        @pl.when(q < nb_half)
        def _():
            o_ref[...] += _dot_tn(yh_ref[...], d_ref[...])

        @pl.when(q >= nb_half)
        def _():
            o_ref[...] += _dot_tn(ya_ref[...], d_ref[...])

    return pl.pallas_call(
        body, grid=(4, S // tk), name="grad_w_out",
        in_specs=[pl.BlockSpec((tk, R), lambda q, k: (k, jnp.minimum(q, nb_half - 1))),
                  pl.BlockSpec((tk, R), lambda q, k: (k, jnp.maximum(q - nb_half, 0))),
                  pl.BlockSpec((tk, D), lambda q, k: (k, 0))],
        out_specs=pl.BlockSpec((None, R, D), lambda q, k: (q, 0, 0)),
        out_shape=jax.ShapeDtypeStruct((4, R, D), F32),
        compiler_params=_cp(("parallel", "arbitrary")),
    )(yh, ya, doutb)


def _dz_sources(sources):
    counts = [s.shape[0] for s in sources]
    starts = [sum(counts[:k]) for k in range(len(counts))]
    assert sum(counts) == 8
    return counts, starts


def _dh_proj(sources, w_all, token, part, name):
    S = sources[0].shape[1]
    D = w_all.shape[1]
    SEG = w_all.shape[2] // 2
    counts, starts = _dz_sources(sources)
    assert all(c % 2 == 0 for c in counts)
    ns = len(sources)
    tm = min(512, S // 2)
    nt = (S // 2) // tm
    t0 = part * nt

    def body(*refs):
        src = refs[:ns]
        w_ref, _, o_ref = refs[ns:]
        j = pl.program_id(1)

        @pl.when(j == 0)
        def _():
            o_ref[...] = jnp.zeros_like(o_ref)

        for k in range(ns):
            @pl.when((2 * j >= starts[k]) & (2 * j < starts[k] + counts[k]))
            def _(k=k):
                o_ref[...] += (_dot_nt(src[k][0], w_ref[:, pl.ds(0, SEG)])
                               + _dot_nt(src[k][1], w_ref[:, pl.ds(SEG, SEG)]))

    def src_spec(k):
        return pl.BlockSpec((2, tm, SEG),
                            lambda i, j: (jnp.clip(j - starts[k] // 2, 0, counts[k] // 2 - 1), t0 + i, 0))

    return pl.pallas_call(
        body, grid=(nt, 4), name=name,
        in_specs=[src_spec(k) for k in range(ns)] + [pl.BlockSpec((None, D, 2 * SEG), lambda i, j: (j, 0, 0)),
                                                     pl.BlockSpec(token.shape, lambda i, j: (0, 0))],
        out_specs=pl.BlockSpec((tm, D), lambda i, j: (i, 0)),
        out_shape=jax.ShapeDtypeStruct((S // 2, D), F32),
        compiler_params=_cp(("parallel", "arbitrary")),
    )(*sources, w_all, token)


def _rms_bwd(dh, x2, rinv, gain, dout, part, gx_prev, name):
    S, D = x2.shape
    tm = min(256, S // 2)
    nt = (S // 2) // tm
    t0 = part * nt

    def body(dh_ref, x_ref, r_ref, g_ref, dout_ref, *rest):
        gx_ref, dg_ref = rest[-2:]

        @pl.when(pl.program_id(0) == 0)
        def _():
            dg_ref[...] = jnp.zeros_like(dg_ref)

        dh = dh_ref[...]
        r = r_ref[...]
        xhat = x_ref[...] * r
        dg_ref[...] += jnp.sum(dh * xhat, axis=0, keepdims=True)
        dxn = dh * g_ref[...]
        gx_ref[...] = dout_ref[...] + r * (dxn - xhat * jnp.mean(dxn * xhat, axis=-1, keepdims=True))

    row = lambda i: (t0 + i, 0)
    fix = lambda i: (0, 0)
    in_specs = [pl.BlockSpec((tm, D), lambda i: (i, 0)), pl.BlockSpec((tm, D), row), pl.BlockSpec((tm, 1), row),
                pl.BlockSpec((1, D), fix), pl.BlockSpec((tm, D), row)]
    args = [dh, x2, rinv, gain, dout]
    aliases = {}
    if gx_prev is not None:
        in_specs.append(ANY)
        args.append(gx_prev)
        aliases = {5: 0}
    return pl.pallas_call(
        body, grid=(nt,), name=name, in_specs=in_specs,
        out_specs=[pl.BlockSpec((tm, D), row), pl.BlockSpec((1, D), fix)],
        out_shape=[jax.ShapeDtypeStruct((S, D), F32), jax.ShapeDtypeStruct((1, D), F32)],
        input_output_aliases=aliases, compiler_params=_cp(("arbitrary",)),
    )(*args)


def _grad_w_in(h, sources):
    S, D = h.shape
    SEG = sources[0].shape[2]
    counts, starts = _dz_sources(sources)
    ns = len(sources)
    tk = min(1024, S)

    def body(*refs):
        h_ref = refs[0]
        src = refs[1:1 + ns]
        o_ref = refs[1 + ns]
        j = pl.program_id(0)
        k = pl.program_id(1)

        @pl.when(k == 0)
        def _():
            o_ref[...] = jnp.zeros_like(o_ref)

        for s in range(ns):
            @pl.when((j >= starts[s]) & (j < starts[s] + counts[s]))
            def _(s=s):
                o_ref[...] += _dot_tn(h_ref[...], src[s][...])

    def src_spec(s):
        return pl.BlockSpec((None, tk, SEG),
                            lambda j, k: (jnp.clip(j - starts[s], 0, counts[s] - 1), k, 0))

    return pl.pallas_call(
        body, grid=(8, S // tk), name="grad_w_in",
        in_specs=[pl.BlockSpec((tk, D), lambda j, k: (k, 0))] + [src_spec(s) for s in range(ns)],
        out_specs=pl.BlockSpec((None, D, SEG), lambda j, k: (j // 2, 0, j % 2)),
        out_shape=jax.ShapeDtypeStruct((4, D, 2 * SEG), F32),
        compiler_params=_cp(("parallel", "arbitrary")),
    )(h, *sources)


def _lower_bound(lbl):
    l0 = lbl[0:1, :]
    l1 = lbl[1:2, :]
    m = jnp.maximum(l0, l1)
    e0 = jnp.exp(l0 - m)
    e1 = jnp.exp(l1 - m)
    return e0 / (e0 + e1)


def _tile_masks():
    row = lax.broadcasted_iota(jnp.int32, (HGRN_TILE, HGRN_TILE), 0)
    col = lax.broadcasted_iota(jnp.int32, (HGRN_TILE, HGRN_TILE), 1)
    same = (row // HGRN_CHUNK) == (col // HGRN_CHUNK)
    return same & (row >= col), same & (row <= col)


def _chunk_last(b):
    T = b.shape[0]
    b3 = b.reshape(T // HGRN_CHUNK, HGRN_CHUNK, HGRN_HEAD)
    return jnp.broadcast_to(b3[:, HGRN_CHUNK - 1:HGRN_CHUNK, :], b3.shape).reshape(T, HGRN_HEAD)


def _chunk_sum(x):
    T = x.shape[0]
    x3 = x.reshape(T // HGRN_CHUNK, HGRN_CHUNK, HGRN_HEAD)
    return jnp.broadcast_to(jnp.sum(x3, axis=1, keepdims=True), x3.shape).reshape(T, HGRN_HEAD)


def _hgrn_dims(S, SEG):
    T = min(HGRN_BLOCK, S)
    assert S % T == 0 and T % HGRN_TILE == 0
    tiles = [slice(t * HGRN_TILE, (t + 1) * HGRN_TILE) for t in range(T // HGRN_TILE)]
    chunks = [slice(c * HGRN_CHUNK, (c + 1) * HGRN_CHUNK) for c in range(T // HGRN_CHUNK)]
    return SEG // HGRN_HEAD, T, T // HGRN_CHUNK, S // T, tiles, chunks


def _hgrn_fwd(zf32, lb_logits, gnorm):
    _, NLB, S, _ = zf32.shape
    SEG = NLB * LANES
    H, T, NC, NJ, tiles, chunks = _hgrn_dims(S, SEG)
    HP = min(HGRN_HEADS_PER_STEP, H)
    assert H % HP == 0

    def body(zq_ref, zf_ref, zi_ref, zg_ref, lbl_ref, gn_ref, y_ref, st_ref, state):
        @pl.when(pl.program_id(1) == 0)
        def _():
            state[...] = jnp.zeros_like(state)

        tril, _ = _tile_masks()
        tril_bf = tril.astype(BF16)
        for hh in range(HP):
            cols = slice(hh * HGRN_HEAD, (hh + 1) * HGRN_HEAD)
            lb = _lower_bound(lbl_ref[:, cols])
            zq = zq_ref[hh]
            q = zq * _sigmoid(zq)
            f = lb + (1.0 - lb) * _sigmoid(zf_ref[hh])
            k = 1.0 - f
            logf = jnp.log(f)
            b = jnp.concatenate([_exact_dot(tril_bf, logf[t]) for t in tiles], axis=0)
            bl = _chunk_last(b)
            qd_b = (q * jnp.exp(b)).astype(BF16)
            kd_b = (k * jnp.exp(-b)).astype(BF16)
            ke_b = (k * jnp.exp(bl - b)).astype(BF16)
            v_b = zi_ref[hh].astype(BF16)
            o_intra = jnp.concatenate(
                [_dot(jnp.where(tril, _dot_nt(qd_b[t], kd_b[t]), 0.0).astype(BF16), v_b[t]) for t in tiles], axis=0)
            kvs = [_dot_tn(v_b[r], ke_b[r]) for r in chunks]
            ebl = jnp.exp(bl)
            st = state[hh]
            sts = []
            for c in range(NC):
                st_ref[c, hh] = st
                sts.append(st.astype(BF16))
                st = st * ebl[c * HGRN_CHUNK:c * HGRN_CHUNK + 1, :] + kvs[c]
            state[hh] = st
            o = o_intra + jnp.concatenate([_dot_nt(qd_b[r], sb) for r, sb in zip(chunks, sts)], axis=0)
            on = o * lax.rsqrt(jnp.mean(o * o, axis=-1, keepdims=True) + NORM_EPS) * gn_ref[...]
            zg = zg_ref[hh]
            y_ref[:, cols] = (on * (zg * _sigmoid(zg))).astype(BF16)

    def zspec(seg):
        return pl.BlockSpec((None, HP, T, HGRN_HEAD), lambda h, j: (seg, h, j, 0))

    return pl.pallas_call(
        body, grid=(H // HP, NJ), name="hgrn_fwd",
        in_specs=[zspec(0), zspec(1), zspec(2), zspec(3),
                  pl.BlockSpec((2, HP * HGRN_HEAD), lambda h, j: (0, h)),
                  pl.BlockSpec((1, HGRN_HEAD), lambda h, j: (0, 0))],
        out_specs=[pl.BlockSpec((T, HP * HGRN_HEAD), lambda h, j: (j, h)),
                   pl.BlockSpec((NC, HP, HGRN_HEAD, HGRN_HEAD), lambda h, j: (j, h, 0, 0))],
        out_shape=[jax.ShapeDtypeStruct((S, SEG), BF16),
                   jax.ShapeDtypeStruct((S // HGRN_CHUNK, H, HGRN_HEAD, HGRN_HEAD), F32)],
        scratch_shapes=[pltpu.VMEM((HP, HGRN_HEAD, HGRN_HEAD), F32)],
        compiler_params=_cp(("parallel", "arbitrary")),
    )(zf32, zf32, zf32, zf32, lb_logits, gnorm)


def _hgrn_bwd(zf32, lb_logits, gnorm, states, dy):
    _, NLB, S, _ = zf32.shape
    SEG = NLB * LANES
    H, T, NC, NJ, tiles, chunks = _hgrn_dims(S, SEG)
    C = HGRN_CHUNK
    HP = min(HGRN_HEADS_PER_STEP, H)
    assert H % HP == 0

    def body(zq_ref, zf_ref, zi_ref, zg_ref, lbl_ref, gn_ref, st_ref, dy_ref, dz_ref, dl_ref, dgn_ref, gstate):
        @pl.when(pl.program_id(1) == 0)
        def _():
            gstate[...] = jnp.zeros_like(gstate)
            dl_ref[...] = jnp.zeros_like(dl_ref)
            dgn_ref[...] = jnp.zeros_like(dgn_ref)

        gn = gn_ref[...]
        tril, triu = _tile_masks()
        tril_bf = tril.astype(BF16)
        triu_bf = triu.astype(BF16)
        for hh in range(HP):
            cols = slice(hh * HGRN_HEAD, (hh + 1) * HGRN_HEAD)
            lb = _lower_bound(lbl_ref[:, cols])
            q, dq_dz = _silu_and_grad(zq_ref[hh])
            sf = _sigmoid(zf_ref[hh])
            f = lb + (1.0 - lb) * sf
            k = 1.0 - f
            logf = jnp.log(f)
            b = jnp.concatenate([_exact_dot(tril_bf, logf[t]) for t in tiles], axis=0)
            bl = _chunk_last(b)
            eb = jnp.exp(b)
            enb = jnp.exp(-b)
            ekl = jnp.exp(bl - b)
            ebl = jnp.exp(bl)
            qd = q * eb
            kd = k * enb
            ke = k * ekl
            qd_b = qd.astype(BF16)
            kd_b = kd.astype(BF16)
            ke_b = ke.astype(BF16)
            v_b = zi_ref[hh].astype(BF16)
            sts = [st_ref[c, hh] for c in range(NC)]
            sts_b = [s.astype(BF16) for s in sts]
            a_b = [jnp.where(tril, _dot_nt(qd_b[t], kd_b[t]), 0.0).astype(BF16) for t in tiles]
            o = (jnp.concatenate([_dot(a, v_b[t]) for a, t in zip(a_b, tiles)], axis=0)
                 + jnp.concatenate([_dot_nt(qd_b[r], sb) for r, sb in zip(chunks, sts_b)], axis=0))
            rinv = lax.rsqrt(jnp.mean(o * o, axis=-1, keepdims=True) + NORM_EPS)
            ohat = o * rinv
            sg, dsg = _silu_and_grad(zg_ref[hh])
            dyv = dy_ref[:, cols]
            don = dyv * sg
            dz_ref[3, :, cols] = (dyv * (ohat * gn) * dsg).astype(BF16)
            dgn_ref[hh] += jnp.sum(don * ohat, axis=0, keepdims=True)
            dohat = don * gn
            do = rinv * (dohat - ohat * jnp.mean(dohat * ohat, axis=-1, keepdims=True))
            do_b = do.astype(BF16)
            da_b = [jnp.where(tril, _dot_nt(do_b[t], v_b[t]), 0.0).astype(BF16) for t in tiles]
            dv_intra = jnp.concatenate([_dot_tn(a, do_b[t]) for a, t in zip(a_b, tiles)], axis=0)
            dqd_intra = jnp.concatenate([_dot(da, kd_b[t]) for da, t in zip(da_b, tiles)], axis=0)
            dkd = jnp.concatenate([_dot_tn(da, qd_b[t]) for da, t in zip(da_b, tiles)], axis=0)
            dqd_inter = jnp.concatenate([_dot(do_b[r], sb) for r, sb in zip(chunks, sts_b)], axis=0)
            gks = [_dot_tn(do_b[r], qd_b[r]) for r in chunks]
            g = gstate[hh]
            gs = [None] * NC
            for c in reversed(range(NC)):
                gs[c] = g
                g = g * ebl[c * C:c * C + 1, :] + gks[c]
            gstate[hh] = g
            gs_b = [x.astype(BF16) for x in gs]
            dv = dv_intra + jnp.concatenate([_dot_nt(ke_b[r], gb) for r, gb in zip(chunks, gs_b)], axis=0)
            dz_ref[2, :, cols] = dv.astype(BF16)
            dke = jnp.concatenate([_dot(v_b[r], gb) for r, gb in zip(chunks, gs_b)], axis=0)
            debl = jnp.concatenate(
                [jnp.broadcast_to(jnp.sum(x * s, axis=0, keepdims=True), (C, HGRN_HEAD)) for x, s in zip(gs, sts)], axis=0)
            dqd = dqd_intra + dqd_inter
            dz_ref[0, :, cols] = ((dqd * eb) * dq_dz).astype(BF16)
            t_ke = dke * ke
            db = dqd * qd - dkd * kd - t_ke
            db_last = _chunk_sum(t_ke) + debl * ebl
            dk = dkd * enb + dke * ekl
            dlogf = jnp.concatenate([_exact_dot(triu_bf, db[t]) for t in tiles], axis=0) + db_last
            df = dlogf / f - dk
            dz_ref[1, :, cols] = (df * (1.0 - lb) * (sf * (1.0 - sf))).astype(BF16)
            dlb = jnp.sum(df * (1.0 - sf), axis=0, keepdims=True)
            dl0 = dlb * lb * (1.0 - lb)
            dl_ref[0:1, cols] += dl0
            dl_ref[1:2, cols] -= dl0

    def zspec(seg):
        return pl.BlockSpec((None, HP, T, HGRN_HEAD), lambda h, j: (seg, h, NJ - 1 - j, 0))

    return pl.pallas_call(
        body, grid=(H // HP, NJ), name="hgrn_bwd",
        in_specs=[zspec(0), zspec(1), zspec(2), zspec(3),
                  pl.BlockSpec((2, HP * HGRN_HEAD), lambda h, j: (0, h)),
                  pl.BlockSpec((1, HGRN_HEAD), lambda h, j: (0, 0)),
                  pl.BlockSpec((NC, HP, HGRN_HEAD, HGRN_HEAD), lambda h, j: (NJ - 1 - j, h, 0, 0)),
                  pl.BlockSpec((T, HP * HGRN_HEAD), lambda h, j: (NJ - 1 - j, h))],
        out_specs=[pl.BlockSpec((4, T, HP * HGRN_HEAD), lambda h, j: (0, NJ - 1 - j, h)),
                   pl.BlockSpec((2, HP * HGRN_HEAD), lambda h, j: (0, h)),
                   pl.BlockSpec((HP, 1, HGRN_HEAD), lambda h, j: (h, 0, 0))],
        out_shape=[jax.ShapeDtypeStruct((4, S, SEG), BF16), jax.ShapeDtypeStruct((2, SEG), F32),
                   jax.ShapeDtypeStruct((H, 1, HGRN_HEAD), F32)],
        scratch_shapes=[pltpu.VMEM((HP, HGRN_HEAD, HGRN_HEAD), F32)],
        compiler_params=_cp(("parallel", "arbitrary")),
    )(zf32, zf32, zf32, zf32, lb_logits, gnorm, states, dy)


def _alibi_slopes(seg):
    n_heads = seg // ATTN_HEAD
    s = 2.0 ** (-8.0 * np.arange(1, n_heads + 1, dtype=np.float64) / n_heads)
    return jnp.asarray(np.repeat(s, ATTN_HEAD)[None, :], F32)


def _attn_dims(S, SEG, d):
    rb = BAND * d
    assert S % rb == 0 and SEG % LANES == 0
    npb = max(1, min(SEG // LANES, ATTN_BLOCK_ELEMS // (rb * LANES)))
    assert (SEG // LANES) % npb == 0
    return rb, npb, S // rb, (SEG // LANES) // npb


def _res_rows(r, d):
    return pl.ds(0, BAND) if d == 1 else pl.ds(r, BAND, stride=d)


def _for_residues(d, fn):
    if d == 1:
        fn(0)
    else:
        def step(r, carry):
            fn(r)
            return carry
        lax.fori_loop(0, d, step, 0, unroll=ATTN_UNROLL)


def _for_groups(d, n_pairs, fn):
    def over_pairs(r):
        for g0 in range(0, n_pairs, ATTN_UNROLL):
            fn([(r, p) for p in range(g0, min(n_pairs, g0 + ATTN_UNROLL))])

    if d == 1:
        over_pairs(0)
    elif n_pairs >= ATTN_UNROLL:
        def step(r, carry):
            over_pairs(r)
            return carry
        lax.fori_loop(0, d, step, 0)
    else:
        per_group = ATTN_UNROLL // n_pairs
        assert d % per_group == 0

        def step(g, carry):
            fn([(g * per_group + i, p) for i in range(per_group) for p in range(n_pairs)])
            return carry
        lax.fori_loop(0, d // per_group, step, 0)


def _band_terms(n, d):
    i = lax.broadcasted_iota(jnp.int32, (BAND, 2 * BAND), 0)
    jj = lax.broadcasted_iota(jnp.int32, (BAND, 2 * BAND), 1)
    delta = BAND + i - jj
    valid = (delta >= 0) & (delta <= BAND) & ((n > 0) | (jj >= BAND))
    return (-d * delta).astype(F32), valid


def _head_biases(slopes, nd, valid):
    out = []
    for s in _per_head(slopes):
        s2 = jnp.concatenate([s, s], axis=1)
        out.append(jnp.where(valid, s2 * nd, NEG))
    return jnp.concatenate(out, axis=0)


def _stack_heads(x):
    lane = lax.broadcasted_iota(jnp.int32, x.shape, 1)
    zero = jnp.zeros_like(x)
    return jnp.concatenate([jnp.where(lane < ATTN_HEAD, x, zero), jnp.where(lane < ATTN_HEAD, zero, x)], axis=0)


def _unstack_heads(x2):
    first = lax.broadcasted_iota(jnp.int32, (BAND, LANES), 1) < ATTN_HEAD
    return jnp.where(first, x2[:BAND], x2[BAND:])


def _stack_per_head(x):
    a, b = _per_head(x)
    col = jnp.concatenate([a, b], axis=0)
    return jnp.concatenate([col, col], axis=1)


def _per_head(x):
    lane = lax.broadcasted_iota(jnp.int32, x.shape, 1)
    sw = pltpu.roll(x, ATTN_HEAD, 1)
    first = lane < ATTN_HEAD
    return jnp.where(first, x, sw), jnp.where(first, sw, x)


def _qkv_source(zz, d):
    z, z16 = zz
    if d == DEINTERLEAVE:
        def take(ref, p, r):
            return ref.at[p][r]

        def spec(seg, np_, row_block):
            return pl.BlockSpec((None, np_, d, BAND, LANES), lambda c, n: (seg, c, 0, row_block(c, n), 0))
        return z, z16, take, spec

    def take(ref, p, r):
        return ref.at[p][_res_rows(r, d), :]

    def spec(seg, np_, row_block):
        return pl.BlockSpec((None, np_, BAND * d, LANES), lambda c, n: (SEG_QKV + seg, c, row_block(c, n), 0))
    return z, z, take, spec


def _attn_fwd(qkv, slopes, d):
    qkv, src, take, spec = _qkv_source(qkv, d)
    _, NLB, S, _ = qkv.shape
    rb, NP, nb, ncb = _attn_dims(S, NLB * LANES, d)

    def body(q_ref, kp_ref, kc_ref, vp_ref, vc_ref, sl_ref, o_ref, l_ref):
        n = pl.program_id(1)
        nd, valid = _band_terms(n, d)
        biases = [_head_biases(sl_ref[:, p * LANES:(p + 1) * LANES], nd, valid) for p in range(NP)]

        def group(items):
            scores, values = [], []
            for r, p in items:
                kc = jnp.concatenate([take(kp_ref, p, r), take(kc_ref, p, r)], axis=0).astype(BF16)
                values.append(jnp.concatenate([take(vp_ref, p, r), take(vc_ref, p, r)], axis=0).astype(BF16))
                scores.append(_dot_nt(_stack_heads((take(q_ref, p, r) * ATTN_SCALE).astype(BF16)), kc))
            probs = []
            for (r, p), s in zip(items, scores):
                s = s + biases[p]
                m = jnp.max(s, axis=-1, keepdims=True)
                e = jnp.exp(s - m)
                den = jnp.sum(e, axis=-1, keepdims=True)
                probs.append((e.astype(BF16), den, m + jnp.log(den)))
            for (r, p), vc, (e, den, lse) in zip(items, values, probs):
                rows = _res_rows(r, d)
                o_ref.at[p][rows, :] = _unstack_heads(_dot(e, vc) / den)
                l_ref.at[p][rows, :] = _unstack_heads(jnp.broadcast_to(lse, (2 * BAND, LANES)))

        _for_groups(d, NP, group)

    cur = lambda c, n: n
    prev = lambda c, n: jnp.maximum(n - 1, 0)
    out = pl.BlockSpec((NP, rb, LANES), lambda c, n: (c, n, 0))
    return pl.pallas_call(
        body, grid=(ncb, nb), name=f"attn_fwd_d{d}",
        in_specs=[spec(0, NP, cur), spec(1, NP, prev), spec(1, NP, cur), spec(2, NP, prev), spec(2, NP, cur),
                  pl.BlockSpec((1, NP * LANES), lambda c, n: (0, c))],
        out_specs=[out, out],
        out_shape=[jax.ShapeDtypeStruct((NLB, S, LANES), F32)] * 2,
        compiler_params=_cp(("parallel", "parallel")),
    )(src, src, src, src, src, slopes)


def _attn_merge(outs, lses, zf32):
    NLB, S, _ = outs[0].shape
    SEG = NLB * LANES
    tm = min(256, S)

    def body(o1, o2, o3, l1, l2, l3, zg_ref, o_ref, lse_ref, y_ref):
        a, b, c = l1[...], l2[...], l3[...]
        m = jnp.maximum(jnp.maximum(a, b), c)
        ea, eb, ec = jnp.exp(a - m), jnp.exp(b - m), jnp.exp(c - m)
        tot = ea + eb + ec
        o = (ea / tot) * o1[...] + (eb / tot) * o2[...] + (ec / tot) * o3[...]
        o_ref[...] = o
        lse_ref[...] = m + jnp.log(tot)
        zg = zg_ref[...]
        y = (o * (zg * _sigmoid(zg))).astype(BF16)
        for p in range(NLB):
            y_ref[:, p * LANES:(p + 1) * LANES] = y[p]

    blk = pl.BlockSpec((NLB, tm, LANES), lambda i: (0, i, 0))
    return pl.pallas_call(
        body, grid=(S // tm,), name="attn_merge",
        in_specs=[blk] * 6 + [pl.BlockSpec((None, NLB, tm, LANES), lambda i: (SEG_GATE_A, 0, i, 0))],
        out_specs=[blk, blk, pl.BlockSpec((tm, SEG), lambda i: (i, 0))],
        out_shape=[jax.ShapeDtypeStruct((NLB, S, LANES), F32), jax.ShapeDtypeStruct((NLB, S, LANES), F32),
                   jax.ShapeDtypeStruct((S, SEG), BF16)],
        compiler_params=_cp(("parallel",)),
    )(*outs, *lses, zf32)


def _attn_gate_bwd(dy, o, zf32):
    NP, S, _ = o.shape
    SEG = NP * LANES
    tm = min(256, S)

    def body(dy_ref, o_ref, zg_ref, do_ref, dl_ref, dzg_ref):
        r = lax.broadcasted_iota(jnp.int32, (LANES, LANES), 0) // ATTN_HEAD
        c = lax.broadcasted_iota(jnp.int32, (LANES, LANES), 1) // ATTN_HEAD
        same_head = (r == c).astype(BF16)
        for p in range(NP):
            cols = slice(p * LANES, (p + 1) * LANES)
            sg, dsg = _silu_and_grad(zg_ref[p])
            dyv = dy_ref[:, cols]
            ov = o_ref[p]
            do = dyv * sg
            do_ref[p] = do
            dzg_ref[:, cols] = (dyv * ov * dsg).astype(BF16)
            dl_ref[p] = _exact_dot_right(do * ov, same_head)

    blk = pl.BlockSpec((NP, tm, LANES), lambda i: (0, i, 0))
    return pl.pallas_call(
        body, grid=(S // tm,), name="attn_gate_bwd",
        in_specs=[pl.BlockSpec((tm, SEG), lambda i: (i, 1)), blk,
                  pl.BlockSpec((None, NP, tm, LANES), lambda i: (SEG_GATE_A, 0, i, 0))],
        out_specs=[blk, blk, pl.BlockSpec((None, tm, SEG), lambda i: (3, i, 0))],
        out_shape=[jax.ShapeDtypeStruct((NP, S, LANES), F32), jax.ShapeDtypeStruct((NP, S, LANES), F32),
                   jax.ShapeDtypeStruct((4, S, SEG), BF16)],
        compiler_params=_cp(("parallel",)),
    )(dy, o, zf32)


def _attn_bwd(qkv, slopes, do, lse, dl, d, acc, into):
    qkv, src, take, spec = _qkv_source(qkv, d)
    _, NLB, S, _ = qkv.shape
    SEG = NLB * LANES
    rb, NP, nb, ncb = _attn_dims(S, SEG, d)
    has_acc = acc is not None
    out_dtype = F32 if into is None else into.dtype
    assert into is None or d == 1

    def body(*refs):
        q_ref, kp_ref, kc_ref, vp_ref, vc_ref, sl_ref, do_ref, lse_ref, dl_ref = refs[:9]
        acc_ref = refs[9] if has_acc else None
        out_ref, cq, ck, cv = refs[-4:]
        n = pl.program_id(1)

        def emit(r, p, dq, dk, dv):
            rows = _res_rows(r, d)
            for t, val in enumerate((dq, dk, dv)):
                if has_acc:
                    val = val + acc_ref.at[t].at[p][rows, :]
                if into is None:
                    out_ref.at[t].at[p][rows, :] = val.astype(out_dtype)
                else:
                    out_ref.at[t][rows, p * LANES:(p + 1) * LANES] = val.astype(out_dtype)

        @pl.when(n == 0)
        def _():
            cq[...] = jnp.zeros_like(cq)
            ck[...] = jnp.zeros_like(ck)
            cv[...] = jnp.zeros_like(cv)

        @pl.when(n < nb)
        def _():
            nd, valid = _band_terms(n, d)
            biases = [_head_biases(sl_ref[:, p * LANES:(p + 1) * LANES], nd, valid) for p in range(NP)]

            def group(items):
                first = []
                for r, p in items:
                    rows = _res_rows(r, d)
                    kc = jnp.concatenate([take(kp_ref, p, r), take(kc_ref, p, r)], axis=0).astype(BF16)
                    vc = jnp.concatenate([take(vp_ref, p, r), take(vc_ref, p, r)], axis=0).astype(BF16)
                    qs = _stack_heads((take(q_ref, p, r) * ATTN_SCALE).astype(BF16))
                    dos = _stack_heads(do_ref.at[p][rows, :].astype(BF16))
                    first.append((kc, qs, dos, _dot_nt(qs, kc), _dot_nt(dos, vc)))
                second = []
                for (r, p), (kc, qs, dos, s, dp) in zip(items, first):
                    rows = _res_rows(r, d)
                    pr = jnp.exp(s + biases[p] - _stack_per_head(lse_ref.at[p][rows, :]))
                    ds = (pr * (dp - _stack_per_head(dl_ref.at[p][rows, :]))).astype(BF16)
                    second.append((kc, qs, dos, pr.astype(BF16), ds))
                for (r, p), (kc, qs, dos, pr, ds) in zip(items, second):
                    dq = _unstack_heads(_dot(ds, kc)) * ATTN_SCALE
                    dk = _dot_tn(ds, qs)
                    dv = _dot_tn(pr, dos)
                    emit(r, p, cq[r, p], ck[r, p] + dk[:BAND, :], cv[r, p] + dv[:BAND, :])
                    cq[r, p] = dq
                    ck[r, p] = dk[BAND:, :]
                    cv[r, p] = dv[BAND:, :]

            _for_groups(d, NP, group)

        @pl.when(n == nb)
        def _():
            def last(r):
                for p in range(NP):
                    emit(r, p, cq[r, p], ck[r, p], cv[r, p])
            _for_residues(d, last)

    cur = lambda c, n: (c, jnp.minimum(n, nb - 1), 0)
    lag = lambda c, n: (0, c, jnp.clip(n - 1, 0, nb - 1), 0)

    at = lambda c, n: jnp.minimum(n, nb - 1)
    before = lambda c, n: jnp.clip(n - 1, 0, nb - 1)
    in_specs = [spec(0, NP, at), spec(1, NP, before), spec(1, NP, at), spec(2, NP, before), spec(2, NP, at),
                pl.BlockSpec((1, NP * LANES), lambda c, n: (0, c))] + [pl.BlockSpec((NP, rb, LANES), cur)] * 3
    args = [src, src, src, src, src, slopes, do, lse, dl]
    aliases = {}
    if has_acc:
        in_specs.append(pl.BlockSpec((3, NP, rb, LANES), lag))
        args.append(acc)
        if into is None:
            aliases = {9: 0}
    if into is None:
        out_sds = jax.ShapeDtypeStruct((3, NLB, S, LANES), F32)
        out_spec = pl.BlockSpec((3, NP, rb, LANES), lag)
    else:
        in_specs.append(ANY)
        args.append(into)
        aliases = {len(args) - 1: 0}
        out_sds = jax.ShapeDtypeStruct(into.shape, into.dtype)
        out_spec = pl.BlockSpec((3, rb, NP * LANES), lambda c, n: (0, jnp.clip(n - 1, 0, nb - 1), c))
    return pl.pallas_call(
        body, grid=(ncb, nb + 1), name=f"attn_bwd_d{d}",
        in_specs=in_specs, out_specs=out_spec, out_shape=out_sds,
        scratch_shapes=[pltpu.VMEM((d, NP, BAND, LANES), F32)] * 3,
        input_output_aliases=aliases,
        compiler_params=_cp(("parallel", "arbitrary")),
    )(*args)


def _adamw(w, g, m, v, name):
    R, C = w.shape
    tr = R if R <= 256 else 256
    assert R % tr == 0

    def body(w_ref, g_ref, m_ref, v_ref, d_ref, nm_ref, nv_ref):
        g = g_ref[...]
        nm = ADAM_B1 * m_ref[...] + (1.0 - ADAM_B1) * g
        nv = ADAM_B2 * v_ref[...] + (1.0 - ADAM_B2) * (g * g)
        m_hat = nm / (1.0 - ADAM_B1 ** ADAM_STEP)
        v_hat = nv / (1.0 - ADAM_B2 ** ADAM_STEP)
        d_ref[...] = -ADAM_LR * (m_hat / (jnp.sqrt(v_hat) + ADAM_EPS) + ADAM_WD * w_ref[...])
        nm_ref[...] = nm
        nv_ref[...] = nv

    blk = pl.BlockSpec((tr, C), lambda i: (i, 0))
    sds = jax.ShapeDtypeStruct((R, C), F32)
    return pl.pallas_call(
        body, grid=(R // tr,), name=name, in_specs=[blk] * 4, out_specs=[blk] * 3, out_shape=[sds] * 3,
        compiler_params=_cp(("parallel",)),
    )(w, g, m, v)


def _coords():
    return lax.axis_index("x"), lax.axis_index("y"), lax.axis_index("c")


def _other_chips(x, y):
    return [(1 - x, y), (x, 1 - y), (1 - x, 1 - y)]


ANY = pl.BlockSpec(memory_space=pl.ANY)


def _cast_into_slot(w, where, name):
    R, C = w.shape
    tr = min(256, R)

    def body(where_ref, w_ref, o_ref):
        o_ref[...] = w_ref[...].astype(BF16)

    grid_spec = pltpu.PrefetchScalarGridSpec(
        num_scalar_prefetch=1, grid=(R // tr,),
        in_specs=[pl.BlockSpec((tr, C), lambda i, w: (i, 0))],
        out_specs=pl.BlockSpec((None, tr, C), lambda i, w: (w[1], i, 0)))
    return pl.pallas_call(
        body, grid_spec=grid_spec, name=name, out_shape=jax.ShapeDtypeStruct((4, R, C), BF16),
        compiler_params=_cp(("parallel",)),
    )(where, w)


def _pair_sum(g, sib, where, name):
    _, n2, C = g.shape
    N = n2 // 2
    tr = min(256, N)
    nt = N // tr

    def body(where_ref, g_ref, s_ref, qb_ref, own_ref):
        q = pl.program_id(1)
        tot = g_ref[...] + s_ref[...]
        qb_ref[...] = tot.astype(BF16)

        @pl.when(q == where_ref[1])
        def _():
            own_ref[...] = tot

    grid_spec = pltpu.PrefetchScalarGridSpec(
        num_scalar_prefetch=1, grid=(nt, 4),
        in_specs=[pl.BlockSpec((None, tr, C), lambda i, q, w: (q, w[0] * nt + i, 0)),
                  pl.BlockSpec((None, tr, C), lambda i, q, w: (q, i, 0))],
        out_specs=[pl.BlockSpec((None, tr, C), lambda i, q, w: (q, i, 0)),
                   pl.BlockSpec((tr, C), lambda i, q, w: (i, 0))])
    return pl.pallas_call(
        body, grid_spec=grid_spec, name=name,
        out_shape=[jax.ShapeDtypeStruct((4, N, C), BF16), jax.ShapeDtypeStruct((N, C), F32)],
        compiler_params=_cp(("parallel", "arbitrary")),
    )(where, g, sib)


HBM = pl.BlockSpec(memory_space=pltpu.HBM)
SEM = pl.BlockSpec(memory_space=pltpu.SEMAPHORE)


def _in_hbm(a):
    return pltpu.with_memory_space_constraint(a, pltpu.HBM)


def _split_start(name, copies, arrays, n_sems, after=None):
    n = len(arrays)

    def body(*refs):
        for cp in copies(refs[:n], refs[-n - 3], refs[-n - 2]):
            cp.start()
        refs[-1][...] = jnp.zeros_like(refs[-1])

    ordered = () if after is None else (after,)
    outs = pl.pallas_call(
        body, name=name,
        out_shape=(pltpu.SemaphoreType.DMA((n_sems,)), pltpu.SemaphoreType.DMA((n_sems,)),
                   *[pltpu.HBM(a.shape, a.dtype) for a in arrays], jax.ShapeDtypeStruct((8, LANES), F32)),
        in_specs=(HBM,) * n + (ANY,) * len(ordered),
        out_specs=(SEM, SEM) + (HBM,) * n + (pl.BlockSpec(memory_space=pltpu.VMEM),),
        input_output_aliases={i: 2 + i for i in range(n)},
        compiler_params=pltpu.CompilerParams(has_side_effects=pltpu.SideEffectType.DATAFLOW_SIDE_EFFECTING),
    )(*[_in_hbm(a) for a in arrays], *ordered)
    return outs[0], outs[1], list(outs[2:2 + n]), outs[-1]


def _split_wait(name, copies, send_sems, recv_sems, arrays, after):
    n = len(arrays)

    def body(*refs):
        for cp in copies(refs[:n], refs[n], refs[n + 1]):
            cp.wait_send()
            cp.wait_recv()

    outs = pl.pallas_call(
        body, name=name,
        out_shape=tuple(pltpu.HBM(a.shape, a.dtype) for a in arrays),
        in_specs=(HBM,) * n + (SEM, SEM, ANY), out_specs=(HBM,) * n,
        input_output_aliases={i: i for i in range(n)},
        compiler_params=pltpu.CompilerParams(has_side_effects=pltpu.SideEffectType.DATAFLOW_SIDE_EFFECTING),
    )(*arrays, send_sems, recv_sems, after)
    return list(outs)


def _remote(src, dst, sems, k, to):
    send_sems, recv_sems = sems
    return pltpu.make_async_remote_copy(src_ref=src, dst_ref=dst, send_sem=send_sems.at[k], recv_sem=recv_sems.at[k],
                                        device_id=to, device_id_type=MESH)


def _chip_at(x, y, rel):
    px = 1 - x if rel & 2 else x
    py = 1 - y if rel & 1 else y
    return px, py, 2 * px + py


def _gather_in_copies(rels):
    def copies(refs, send_sems, recv_sems):
        (w,) = refs
        x, y, c = _coords()
        seg = w.shape[2] // 2
        mine = w.at[2 * x + y, :, pl.ds(c * seg, seg)]
        return [_remote(mine, mine, (send_sems, recv_sems), k, _chip_at(x, y, rel)[:2] + (c,))
                for k, rel in enumerate(rels)]
    return copies


def _gather_out_copies(refs, send_sems, recv_sems):
    (w,) = refs
    x, y, c = _coords()
    mine = w.at[2 * x + y]
    return [_remote(mine, mine, (send_sems, recv_sems), k, (px, py, c)) for k, (px, py) in enumerate(_other_chips(x, y))]


def _swap_copies(refs, send_sems, recv_sems):
    gi, go, si, so = refs
    x, y, c = _coords()
    cps = []
    for a, (src, dst) in enumerate(((gi, si), (go, so))):
        nr = dst.shape[1]
        cps.append(_remote(src.at[:, pl.ds((1 - c) * nr, nr), :], dst, (send_sems, recv_sems), a, (x, y, 1 - c)))
    return cps


def _scatter_copies(refs, send_sems, recv_sems):
    qi, qo, ri, ro = refs
    x, y, c = _coords()
    cps = []
    for k, (px, py) in enumerate(_other_chips(x, y)):
        for a, (src, dst) in enumerate(((qi, ri), (qo, ro))):
            cps.append(_remote(src.at[2 * px + py], dst.at[k], (send_sems, recv_sems), 2 * k + a, (px, py, c)))
    return cps


def _forward_copies(rels):
    def copies(refs, send_sems, recv_sems):
        (w,) = refs
        x, y, c = _coords()
        seg = w.shape[2] // 2
        cps = []
        for k, rel in enumerate(rels):
            got = w.at[_chip_at(x, y, rel)[2], :, pl.ds(c * seg, seg)]
            cps.append(_remote(got, got, (send_sems, recv_sems), k, (x, y, 1 - c)))
        return cps
    return copies


def _chip_sum(own, got, where, name):
    N, C = own.shape
    tr = min(256, N)
    nt = N // tr

    def body(where_ref, own_ref, got_ref, o_ref):
        t = own_ref[...]
        for k in range(3):
            t = t + got_ref[k].astype(F32)
        o_ref[...] = t

    grid_spec = pltpu.PrefetchScalarGridSpec(
        num_scalar_prefetch=1, grid=(nt,),
        in_specs=[pl.BlockSpec((tr, C), lambda i, w: (i, 0)), pl.BlockSpec((3, tr, C), lambda i, w: (0, i, 0))],
        out_specs=pl.BlockSpec((tr, C), lambda i, w: (w[0] * nt + i, 0)))
    return pl.pallas_call(
        body, grid_spec=grid_spec, name=name, out_shape=jax.ShapeDtypeStruct((2 * N, C), F32),
        compiler_params=_cp(("parallel",)),
    )(where, own, got)


def _join_halves(gi, go):
    def body(gi_in, go_in, gi_ref, go_ref, send_sems, recv_sems):
        x, y, c = _coords()
        cps = []
        for a, ref in enumerate((gi_ref, go_ref)):
            nr = ref.shape[0] // 2
            mine = ref.at[pl.ds(c * nr, nr), :]
            cp = pltpu.make_async_remote_copy(src_ref=mine, dst_ref=mine, send_sem=send_sems.at[a],
                                              recv_sem=recv_sems.at[a], device_id=(x, y, 1 - c), device_id_type=MESH)
            cp.start()
            cps.append(cp)
        for a, ref in enumerate((gi_ref, go_ref)):
            nr = ref.shape[0] // 2
            theirs = ref.at[pl.ds((1 - c) * nr, nr), :]
            pltpu.make_async_remote_copy(src_ref=theirs, dst_ref=theirs, send_sem=send_sems.at[a],
                                         recv_sem=recv_sems.at[a], device_id=(x, y, 1 - c),
                                         device_id_type=MESH).wait_recv()
        for cp in cps:
            cp.wait_send()

    return pl.pallas_call(
        body, name="join_halves", in_specs=[ANY, ANY], out_specs=[ANY, ANY],
        out_shape=[jax.ShapeDtypeStruct(gi.shape, F32), jax.ShapeDtypeStruct(go.shape, F32)],
        scratch_shapes=[pltpu.SemaphoreType.DMA((2,)), pltpu.SemaphoreType.DMA((2,))],
        input_output_aliases={0: 0, 1: 1},
    )(gi, go)


def _all_reduce_small(part):
    R, C = part.shape

    def body(p_ref, o_ref, slots, send_sems, recv_sems):
        x, y, c = _coords()
        me = 4 * x + 2 * y + c
        slots[me] = p_ref[...]
        cps = []
        for k in range(1, 8):
            fx, fy, fc = (k >> 2) & 1, (k >> 1) & 1, k & 1
            peer = (1 - x if fx else x, 1 - y if fy else y, 1 - c if fc else c)
            cp = pltpu.make_async_remote_copy(src_ref=p_ref, dst_ref=slots.at[me], send_sem=send_sems.at[k - 1],
                                              recv_sem=recv_sems.at[k - 1], device_id=peer, device_id_type=MESH)
            cp.start()
            cps.append(cp)
        for cp in cps:
            cp.wait()
        t = slots[0]
        for k in range(1, 8):
            t = t + slots[k]
        o_ref[...] = t

    vm = pl.BlockSpec(memory_space=pltpu.VMEM)
    return pl.pallas_call(
        body, name="all_reduce_small", in_specs=[vm], out_specs=vm,
        out_shape=jax.ShapeDtypeStruct((R, C), F32),
        scratch_shapes=[pltpu.VMEM((8, R, C), F32), pltpu.SemaphoreType.DMA((7,)), pltpu.SemaphoreType.DMA((7,))],
    )(part)


def _mixers_forward(z, lb_logits, hgrn_gnorm):
    slopes = _alibi_slopes(z[0].shape[1] * LANES)
    yh, states = _hgrn_fwd(z[0], lb_logits, hgrn_gnorm)
    outs, lses = [], []
    for d in DILATIONS:
        o, l = _attn_fwd(z, slopes, d)
        outs.append(o)
        lses.append(l)
    o_attn, lse, ya = _attn_merge(outs, lses, z[0])
    return yh, ya, (states, o_attn, lse, slopes)


def _backward_to_dz(z, kept, lb_logits, hgrn_gnorm, yh, ya, w_out_all, x2, tgt, fgain, h):
    states, o_attn, lse, slopes = kept
    dout, doutb, loss, dfg = _out_proj_loss(yh, ya, w_out_all, x2, tgt, fgain)
    dy = _dy_proj(doutb, w_out_all)
    g_w_out = _grad_w_out(yh, ya, doutb)
    dzh, dlogits, dgn = _hgrn_bwd(z[0], lb_logits, hgrn_gnorm, states, dy)
    do, dl, dza = _attn_gate_bwd(dy, o_attn, z[0])
    acc = None
    order = sorted(DILATIONS, reverse=True)
    for d in order[:-1]:
        acc = _attn_bwd(z, slopes, do, lse, dl, d, acc, None)
    dza = _attn_bwd(z, slopes, do, lse, dl, order[-1], acc, dza)
    sources = [dzh, dza]
    g_w_in = _grad_w_in(h, sources)
    return loss, dfg, dlogits, dgn, g_w_out, g_w_in, sources, dout


def _grad_x_half(sources, w_all, x2, rinv, norm_gain, dout, token, part, gx_prev):
    dh = _dh_proj(sources, w_all, token, part, f"dh_proj_{part}")
    return _rms_bwd(dh, x2, rinv, norm_gain, dout, part, gx_prev, f"rms_bwd_{part}")


def _local_step(x2, tgt, norm_gain, w_all, lb_logits, hgrn_gnorm, w_out_all, fgain):
    token = jnp.zeros((8, LANES), F32)
    where = jnp.zeros((2,), jnp.int32)
    h, rinv = _rms_fwd(x2, norm_gain, token)
    z = _in_proj(h, w_all, where, [(rel, half) for rel in range(4) for half in range(2)], None, token, "in_proj_all")
    yh, ya, kept = _mixers_forward(z, lb_logits, hgrn_gnorm)
    loss, dfg, dlogits, dgn, g_w_out, g_w_in, sources, dout = _backward_to_dz(
        z, kept, lb_logits, hgrn_gnorm, yh, ya, w_out_all, x2, tgt, fgain, h)
    gx, dg0 = _grad_x_half(sources, w_all, x2, rinv, norm_gain, dout, token, 0, None)
    gx, dg1 = _grad_x_half(sources, w_all, x2, rinv, norm_gain, dout, token, 1, gx)
    return loss, gx, dg0 + dg1, g_w_in, dlogits, dgn, g_w_out, dfg


def _pack_small(D, loss, dgain, dlogits, dgn, dfg):
    def row(v):
        v = v.reshape(1, -1)
        return jnp.pad(v, ((0, 0), (0, D - v.shape[1])))
    rows = [row(dgain), row(dfg), row(dlogits[0]), row(dlogits[1]), row(jnp.sum(dgn, axis=0)), row(loss)]
    rows += [jnp.zeros((1, D), F32)] * (8 - len(rows))
    return jnp.concatenate(rows, axis=0)


def kernel(x, norm_gain, w_in, lb_logits, hgrn_gnorm, w_out, final_gain, loss_target, m_norm_gain, m_w_in, m_lb_logits, m_hgrn_gnorm, m_w_out, m_final_gain, v_norm_gain, v_w_in, v_lb_logits, v_hgrn_gnorm, v_w_out, v_final_gain):
    _, S, D = x.shape
    SEG = w_in.shape[2] // 2
    x2 = x[0]
    tgt = loss_target[0]
    fgain = final_gain.reshape(1, D)
    where = jnp.stack([lax.axis_index("c"), 2 * lax.axis_index("x") + lax.axis_index("y")]).astype(jnp.int32)

    wia = _cast_into_slot(w_in[0], where, "cast_w_in")
    woa = _cast_into_slot(w_out[0], where, "cast_w_out")
    near, far = (2, 1), (3,)
    ga = _split_start("gather_near_start", _gather_in_copies(near), [wia], 2)
    h, rinv = _rms_fwd(x2, norm_gain, ga[3])
    z = _in_proj(h, ga[2][0], where, [(0, 0), (0, 1)], None, ga[3], "in_proj_own")
    (wia,) = _split_wait("gather_near_wait", _gather_in_copies(near), ga[0], ga[1], ga[2], z[0])
    gb = _split_start("gather_far_start", _gather_in_copies(far), [wia], 1)
    fa = _split_start("forward_near_start", _forward_copies(near), gb[2], 2, after=gb[3])
    z = _in_proj(h, fa[2][0], where, [(2, "mine"), (1, "mine")], z, fa[3], "in_proj_near")
    (wia,) = _split_wait("forward_near_wait", _forward_copies(near), fa[0], fa[1], fa[2], z[0])
    (wia,) = _split_wait("gather_far_wait", _gather_in_copies(far), gb[0], gb[1], [wia], z[0])
    out_sems = _split_start("gather_out_start", _gather_out_copies, [woa], 3, after=wia)
    fb = _split_start("forward_far_start", _forward_copies(far), [wia], 1, after=out_sems[3])
    z = _in_proj(h, fb[2][0], where, [(3, "mine"), (2, "sibling"), (1, "sibling")], z, fb[3], "in_proj_far")
    (wia,) = _split_wait("forward_far_wait", _forward_copies(far), fb[0], fb[1], fb[2], z[0])
    z = _in_proj(h, wia, where, [(3, "sibling")], z, fb[3], "in_proj_last")
    yh, ya, kept = _mixers_forward(z, lb_logits, hgrn_gnorm)
    (woa,) = _split_wait("gather_out_wait", _gather_out_copies, out_sems[0], out_sems[1], out_sems[2], ya)
    w_out_all = woa.reshape(2 * SEG, D)

    loss, dfg, dlogits, dgn, g_w_out, g_w_in, sources, dout = _backward_to_dz(
        z, kept, lb_logits, hgrn_gnorm, yh, ya, w_out_all, x2, tgt, fgain, h)

    sib_i = lax.empty((4, g_w_in.shape[1] // 2, g_w_in.shape[2]), F32)
    sib_o = lax.empty((4, g_w_out.shape[1] // 2, g_w_out.shape[2]), F32)
    sems = _split_start("swap_start", _swap_copies, [g_w_in, g_w_out, sib_i, sib_o], 2)
    grad_x, dg0 = _grad_x_half(sources, wia, x2, rinv, norm_gain, dout, sems[3], 0, None)
    g_w_in, g_w_out, sib_i, sib_o = _split_wait("swap_wait", _swap_copies, sems[0], sems[1], sems[2], grad_x)
    qi, own_i = _pair_sum(g_w_in, sib_i, where, "pair_sum_w_in")
    qo, own_o = _pair_sum(g_w_out, sib_o, where, "pair_sum_w_out")
    ri = lax.empty((3,) + qi.shape[1:], BF16)
    ro = lax.empty((3,) + qo.shape[1:], BF16)
    sems = _split_start("scatter_start", _scatter_copies, [qi, qo, ri, ro], 6)
    grad_x, dg1 = _grad_x_half(sources, wia, x2, rinv, norm_gain, dout, sems[3], 1, grad_x)
    _, _, got_i, got_o = _split_wait("scatter_wait", _scatter_copies, sems[0], sems[1], sems[2], grad_x)
    grad_w_in, grad_w_out = _join_halves(_chip_sum(own_i, got_i, where, "chip_sum_w_in"),
                                         _chip_sum(own_o, got_o, where, "chip_sum_w_out"))

    small = _all_reduce_small(_pack_small(D, loss, dg0 + dg1, dlogits, dgn, dfg))
    grad_norm_gain = small[0:1, :]
    grad_final_gain = small[1:2, :]
    grad_lb_logits = small[2:4, :SEG]
    grad_hgrn_gnorm = small[4:5, :HGRN_HEAD]
    loss_sum = small[5, 0]

    d_ng, m_ng, v_ng = _adamw(norm_gain, grad_norm_gain, m_norm_gain, v_norm_gain, "adamw_norm_gain")
    d_wi, m_wi, v_wi = _adamw(w_in[0], grad_w_in, m_w_in[0], v_w_in[0], "adamw_w_in")
    d_lb, m_lb, v_lb = _adamw(lb_logits, grad_lb_logits, m_lb_logits, v_lb_logits, "adamw_lb_logits")
    d_gn, m_gn, v_gn = _adamw(hgrn_gnorm, grad_hgrn_gnorm, m_hgrn_gnorm, v_hgrn_gnorm, "adamw_hgrn_gnorm")
    d_wo, m_wo, v_wo = _adamw(w_out[0], grad_w_out, m_w_out[0], v_w_out[0], "adamw_w_out")
    d_fg, m_fg, v_fg = _adamw(fgain, grad_final_gain, m_final_gain.reshape(1, D), v_final_gain.reshape(1, D),
                              "adamw_final_gain")

    return (loss_sum, grad_x[None],
            grad_norm_gain, grad_w_in[None], grad_lb_logits, grad_hgrn_gnorm, grad_w_out[None], grad_final_gain[0],
            d_ng, d_wi[None], d_lb, d_gn, d_wo[None], d_fg[0],
            m_ng, m_wi[None], m_lb, m_gn, m_wo[None], m_fg[0],
            v_ng, v_wi[None], v_lb, v_gn, v_wo[None], v_fg[0])
```

```python
import jax
import jax.numpy as jnp
import numpy as np
from jax import lax
from jax.experimental import pallas as pl
from jax.experimental.pallas import tpu as pltpu

F32 = jnp.float32
BF16 = jnp.bfloat16
MESH = pl.DeviceIdType.MESH

NORM_EPS = 1e-6
HGRN_HEAD = 128
HGRN_CHUNK = 64
HGRN_TILE = 128
HGRN_BLOCK = 512
HGRN_HEADS_PER_STEP = 4
ATTN_HEAD = 64
LANES = 128
BAND = 128
DILATIONS = (1, 4, 16)
DEINTERLEAVE = 16
ATTN_SCALE = ATTN_HEAD ** -0.5
assert ATTN_SCALE == 0.125
ATTN_BLOCK_ELEMS = BAND * 2048
ATTN_UNROLL = 4
SEG_QKV = 4
SEG_GATE_A = 7
NEG = -1e30

ADAM_LR = 0.001
ADAM_B1 = 0.9
ADAM_B2 = 0.999
ADAM_EPS = 1e-08
ADAM_WD = 0.01
ADAM_STEP = 10

MIB = 1024 * 1024


def _cp(semantics=None, vmem_mib=48):
    return pltpu.CompilerParams(dimension_semantics=semantics, vmem_limit_bytes=vmem_mib * MIB)


def _dot(a, b):
    return jnp.dot(a, b, preferred_element_type=F32)


def _dot_nt(a, b):
    return lax.dot_general(a, b, (((1,), (1,)), ((), ())), preferred_element_type=F32)


def _dot_tn(a, b):
    return lax.dot_general(a, b, (((0,), (0,)), ((), ())), preferred_element_type=F32)


def _split3(x):
    hi = x.astype(BF16)
    r1 = x - hi.astype(F32)
    mid = r1.astype(BF16)
    lo = (r1 - mid.astype(F32)).astype(BF16)
    return hi, mid, lo


def _exact_dot(t_bf16, x):
    hi, mid, lo = _split3(x)
    return _dot(t_bf16, hi) + _dot(t_bf16, mid) + _dot(t_bf16, lo)


def _exact_dot_right(x, t_bf16):
    hi, mid, lo = _split3(x)
    return _dot(hi, t_bf16) + _dot(mid, t_bf16) + _dot(lo, t_bf16)


def _sigmoid(z):
    return jax.nn.sigmoid(z)


def _silu_and_grad(z):
    s = _sigmoid(z)
    return z * s, s * (1.0 + z * (1.0 - s))


def _seg_select(j, values):
    out = values[0]
    for t, v in enumerate(values[1:], 1):
        out = jnp.where(j == t, v, out)
    return out


def _rms_fwd(x2, gain, token):
    S, D = x2.shape
    tm = min(512, S)

    def body(x_ref, g_ref, _, h_ref, r_ref):
        x = x_ref[...]
        r = lax.rsqrt(jnp.mean(x * x, axis=-1, keepdims=True) + NORM_EPS)
        h_ref[...] = ((x * r) * g_ref[...]).astype(BF16)
        r_ref[...] = r

    return pl.pallas_call(
        body, grid=(S // tm,), name="rms_fwd",
        in_specs=[pl.BlockSpec((tm, D), lambda i: (i, 0)), pl.BlockSpec((1, D), lambda i: (0, 0)),
                  pl.BlockSpec(token.shape, lambda i: (0, 0))],
        out_specs=[pl.BlockSpec((tm, D), lambda i: (i, 0)), pl.BlockSpec((tm, 1), lambda i: (i, 0))],
        out_shape=[jax.ShapeDtypeStruct((S, D), BF16), jax.ShapeDtypeStruct((S, 1), F32)],
        compiler_params=_cp(("parallel",)),
    )(x2, gain, token)


def _in_proj(h, w_all, where, segs, z_prev, token, name):
    S, D = h.shape
    SEG = w_all.shape[2] // 2
    NLB = SEG // LANES
    tm = min(512, S)
    count = len(segs)
    DI = DEINTERLEAVE
    tu = tm // DI

    def is_qkv(seg):
        return (seg >= SEG_QKV) & (seg < SEG_QKV + 3)

    def seg_of(j, w):
        halves = {0: 0, 1: 1, "mine": w[0], "sibling": 1 - w[0]}
        cands = [2 * jnp.bitwise_xor(w[1], rel) + halves[half] for rel, half in segs]
        keys = [is_qkv(s).astype(jnp.int32) for s in cands]
        out = cands[0]
        for k in range(count):
            pos = (sum(jnp.where(keys[t] < keys[k], 1, 0) for t in range(count))
                   + sum(jnp.where(keys[t] == keys[k], 1, 0) for t in range(k)))
            out = jnp.where(pos == j, cands[k], out)
        return out

    def body(*refs):
        where_ref, h_ref, w_ref = refs[:3]
        o_ref, o16_ref = refs[-2:]
        res = _dot(h_ref[...], w_ref[...])
        for p in range(NLB):
            o_ref[p] = res[:, p * LANES:(p + 1) * LANES]

        @pl.when(is_qkv(seg_of(pl.program_id(0), where_ref)))
        def _():
            for p in range(NLB):
                for r in range(DI):
                    o16_ref[p, r] = o_ref.at[p][pl.ds(r, tu, stride=DI), :]

    def z16_map(j, i, w):
        seg = seg_of(j, w)
        return (jnp.where(is_qkv(seg), seg - SEG_QKV, 3), 0, 0, jnp.where(is_qkv(seg), i, 0), 0)

    in_specs = [pl.BlockSpec((tm, D), lambda j, i, w: (i, 0)),
                pl.BlockSpec((None, D, SEG), lambda j, i, w: (seg_of(j, w) // 2, 0, seg_of(j, w) % 2)),
                pl.BlockSpec(token.shape, lambda j, i, w: (0, 0))]
    args = [where, h, w_all, token]
    aliases = {}
    if z_prev is not None:
        in_specs += [ANY, ANY]
        args += list(z_prev)
        aliases = {4: 0, 5: 1}
    grid_spec = pltpu.PrefetchScalarGridSpec(
        num_scalar_prefetch=1, grid=(count, S // tm), in_specs=in_specs,
        out_specs=[pl.BlockSpec((None, NLB, tm, LANES), lambda j, i, w: (seg_of(j, w), 0, i, 0)),
                   pl.BlockSpec((None, NLB, DI, tu, LANES), z16_map)])
    return pl.pallas_call(
        body, grid_spec=grid_spec, name=name,
        out_shape=[jax.ShapeDtypeStruct((8, NLB, S, LANES), F32),
                   jax.ShapeDtypeStruct((4, NLB, DI, S // DI, LANES), F32)],
        input_output_aliases=aliases, compiler_params=_cp(("parallel", "parallel")),
    )(*args)


def _out_proj_loss(yh, ya, w_out, x2, tgt, fgain):
    S, D = x2.shape
    SEG = yh.shape[1]
    tm = min(256, S)
    parts = 2

    def body(yh_ref, ya_ref, w_ref, x_ref, t_ref, fg_ref, dout_ref, doutb_ref, loss_ref, dfg_ref):
        i = pl.program_id(0)

        @pl.when(i == 0)
        def _():
            loss_ref[...] = jnp.zeros_like(loss_ref)
            dfg_ref[...] = jnp.zeros_like(dfg_ref)

        fg = fg_ref[...]
        loss = jnp.zeros((1, 1), F32)
        dfg = jnp.zeros((1, D), F32)
        for rows in [pl.ds(p * (tm // parts), tm // parts) for p in range(parts)]:
            out = (x_ref[rows, :] + _dot(yh_ref[rows, :], w_ref[pl.ds(0, SEG), :])
                   + _dot(ya_ref[rows, :], w_ref[pl.ds(SEG, SEG), :]))
            r = lax.rsqrt(jnp.mean(out * out, axis=-1, keepdims=True) + NORM_EPS)
            n = out * r
            err = n * fg - t_ref[rows, :]
            loss = loss + 0.5 * jnp.sum(jnp.mean(err * err, axis=-1, keepdims=True), axis=0, keepdims=True)
            dy = err * (1.0 / D)
            dfg = dfg + jnp.sum(dy * n, axis=0, keepdims=True)
            dn = dy * fg
            dout = r * (dn - n * jnp.mean(dn * n, axis=-1, keepdims=True))
            dout_ref[rows, :] = dout
            doutb_ref[rows, :] = dout.astype(BF16)
        loss_ref[...] += loss
        dfg_ref[...] += dfg

    row = lambda i: (i, 0)
    fix = lambda i: (0, 0)
    return pl.pallas_call(
        body, grid=(S // tm,), name="out_proj_loss",
        in_specs=[pl.BlockSpec((tm, SEG), row), pl.BlockSpec((tm, SEG), row), pl.BlockSpec((2 * SEG, D), fix),
                  pl.BlockSpec((tm, D), row), pl.BlockSpec((tm, D), row), pl.BlockSpec((1, D), fix)],
        out_specs=[pl.BlockSpec((tm, D), row), pl.BlockSpec((tm, D), row), pl.BlockSpec((1, 1), fix),
                   pl.BlockSpec((1, D), fix)],
        out_shape=[jax.ShapeDtypeStruct((S, D), F32), jax.ShapeDtypeStruct((S, D), BF16),
                   jax.ShapeDtypeStruct((1, 1), F32), jax.ShapeDtypeStruct((1, D), F32)],
        compiler_params=_cp(("arbitrary",)),
    )(yh, ya, w_out, x2, tgt, fgain)


def _dy_proj(doutb, w_out):
    S, D = doutb.shape
    K = w_out.shape[0]
    tm = min(512, S)

    def body(d_ref, w_ref, o_ref):
        o_ref[...] = _dot_nt(d_ref[...], w_ref[...])

    return pl.pallas_call(
        body, grid=(S // tm,), name="dy_proj",
        in_specs=[pl.BlockSpec((tm, D), lambda i: (i, 0)), pl.BlockSpec((K, D), lambda i: (0, 0))],
        out_specs=pl.BlockSpec((tm, K), lambda i: (i, 0)),
        out_shape=jax.ShapeDtypeStruct((S, K), F32),
        compiler_params=_cp(("parallel",)),
    )(doutb, w_out)


def _grad_w_out(yh, ya, doutb):
    S, SEG = yh.shape
    D = doutb.shape[1]
    R = (2 * SEG) // 4
    nb_half = SEG // R
    tk = min(1024, S)

    def body(yh_ref, ya_ref, d_ref, o_ref):
        q = pl.program_id(0)
        k = pl.program_id(1)

        @pl.when(k == 0)
        def _():
            o_ref[...] = jnp.zeros_like(o_ref)

        @pl.when(q < nb_half)
        def _():
            o_ref[...] += _dot_tn(yh_ref[...], d_ref[...])

        @pl.when(q >= nb_half)
        def _():
            o_ref[...] += _dot_tn(ya_ref[...], d_ref[...])

    return pl.pallas_call(
        body, grid=(4, S // tk), name="grad_w_out",
        in_specs=[pl.BlockSpec((tk, R), lambda q, k: (k, jnp.minimum(q, nb_half - 1))),
                  pl.BlockSpec((tk, R), lambda q, k: (k, jnp.maximum(q - nb_half, 0))),
                  pl.BlockSpec((tk, D), lambda q, k: (k, 0))],
        out_specs=pl.BlockSpec((None, R, D), lambda q, k: (q, 0, 0)),
        out_shape=jax.ShapeDtypeStruct((4, R, D), F32),
        compiler_params=_cp(("parallel", "arbitrary")),
    )(yh, ya, doutb)


def _dz_sources(sources):
    counts = [s.shape[0] for s in sources]
    starts = [sum(counts[:k]) for k in range(len(counts))]
    assert sum(counts) == 8
    return counts, starts


def _dh_proj(sources, w_all, token, part, name):
    S = sources[0].shape[1]
    D = w_all.shape[1]
    SEG = w_all.shape[2] // 2
    counts, starts = _dz_sources(sources)
    assert all(c % 2 == 0 for c in counts)
    ns = len(sources)
    tm = min(512, S // 2)
    nt = (S // 2) // tm
    t0 = part * nt

    def body(*refs):
        src = refs[:ns]
        w_ref, _, o_ref = refs[ns:]
        j = pl.program_id(1)

        @pl.when(j == 0)
        def _():
            o_ref[...] = jnp.zeros_like(o_ref)

        for k in range(ns):
            @pl.when((2 * j >= starts[k]) & (2 * j < starts[k] + counts[k]))
            def _(k=k):
                o_ref[...] += (_dot_nt(src[k][0], w_ref[:, pl.ds(0, SEG)])
                               + _dot_nt(src[k][1], w_ref[:, pl.ds(SEG, SEG)]))

    def src_spec(k):
        return pl.BlockSpec((2, tm, SEG),
                            lambda i, j: (jnp.clip(j - starts[k] // 2, 0, counts[k] // 2 - 1), t0 + i, 0))

    return pl.pallas_call(
        body, grid=(nt, 4), name=name,
        in_specs=[src_spec(k) for k in range(ns)] + [pl.BlockSpec((None, D, 2 * SEG), lambda i, j: (j, 0, 0)),
                                                     pl.BlockSpec(token.shape, lambda i, j: (0, 0))],
        out_specs=pl.BlockSpec((tm, D), lambda i, j: (i, 0)),
        out_shape=jax.ShapeDtypeStruct((S // 2, D), F32),
        compiler_params=_cp(("parallel", "arbitrary")),
    )(*sources, w_all, token)


def _rms_bwd(dh, x2, rinv, gain, dout, part, gx_prev, name):
    S, D = x2.shape
    tm = min(256, S // 2)
    nt = (S // 2) // tm
    t0 = part * nt

    def body(dh_ref, x_ref, r_ref, g_ref, dout_ref, *rest):
        gx_ref, dg_ref = rest[-2:]

        @pl.when(pl.program_id(0) == 0)
        def _():
            dg_ref[...] = jnp.zeros_like(dg_ref)

        dh = dh_ref[...]
        r = r_ref[...]
        xhat = x_ref[...] * r
        dg_ref[...] += jnp.sum(dh * xhat, axis=0, keepdims=True)
        dxn = dh * g_ref[...]
        gx_ref[...] = dout_ref[...] + r * (dxn - xhat * jnp.mean(dxn * xhat, axis=-1, keepdims=True))

    row = lambda i: (t0 + i, 0)
    fix = lambda i: (0, 0)
    in_specs = [pl.BlockSpec((tm, D), lambda i: (i, 0)), pl.BlockSpec((tm, D), row), pl.BlockSpec((tm, 1), row),
                pl.BlockSpec((1, D), fix), pl.BlockSpec((tm, D), row)]
    args = [dh, x2, rinv, gain, dout]
    aliases = {}
    if gx_prev is not None:
        in_specs.append(ANY)
        args.append(gx_prev)
        aliases = {5: 0}
    return pl.pallas_call(
        body, grid=(nt,), name=name, in_specs=in_specs,
        out_specs=[pl.BlockSpec((tm, D), row), pl.BlockSpec((1, D), fix)],
        out_shape=[jax.ShapeDtypeStruct((S, D), F32), jax.ShapeDtypeStruct((1, D), F32)],
        input_output_aliases=aliases, compiler_params=_cp(("arbitrary",)),
    )(*args)


def _grad_w_in(h, sources):
    S, D = h.shape
    SEG = sources[0].shape[2]
    counts, starts = _dz_sources(sources)
    ns = len(sources)
    tk = min(1024, S)

    def body(*refs):
        h_ref = refs[0]
        src = refs[1:1 + ns]
        o_ref = refs[1 + ns]
        j = pl.program_id(0)
        k = pl.program_id(1)

        @pl.when(k == 0)
        def _():
            o_ref[...] = jnp.zeros_like(o_ref)

        for s in range(ns):
            @pl.when((j >= starts[s]) & (j < starts[s] + counts[s]))
            def _(s=s):
                o_ref[...] += _dot_tn(h_ref[...], src[s][...])

    def src_spec(s):
        return pl.BlockSpec((None, tk, SEG),
                            lambda j, k: (jnp.clip(j - starts[s], 0, counts[s] - 1), k, 0))

    return pl.pallas_call(
        body, grid=(8, S // tk), name="grad_w_in",
        in_specs=[pl.BlockSpec((tk, D), lambda j, k: (k, 0))] + [src_spec(s) for s in range(ns)],
        out_specs=pl.BlockSpec((None, D, SEG), lambda j, k: (j // 2, 0, j % 2)),
        out_shape=jax.ShapeDtypeStruct((4, D, 2 * SEG), F32),
        compiler_params=_cp(("parallel", "arbitrary")),
    )(h, *sources)


def _lower_bound(lbl):
    l0 = lbl[0:1, :]
    l1 = lbl[1:2, :]
    m = jnp.maximum(l0, l1)
    e0 = jnp.exp(l0 - m)
    e1 = jnp.exp(l1 - m)
    return e0 / (e0 + e1)


def _tile_masks():
    row = lax.broadcasted_iota(jnp.int32, (HGRN_TILE, HGRN_TILE), 0)
    col = lax.broadcasted_iota(jnp.int32, (HGRN_TILE, HGRN_TILE), 1)
    same = (row // HGRN_CHUNK) == (col // HGRN_CHUNK)
    return same & (row >= col), same & (row <= col)


def _chunk_last(b):
    T = b.shape[0]
    b3 = b.reshape(T // HGRN_CHUNK, HGRN_CHUNK, HGRN_HEAD)
    return jnp.broadcast_to(b3[:, HGRN_CHUNK - 1:HGRN_CHUNK, :], b3.shape).reshape(T, HGRN_HEAD)


def _chunk_sum(x):
    T = x.shape[0]
    x3 = x.reshape(T // HGRN_CHUNK, HGRN_CHUNK, HGRN_HEAD)
    return jnp.broadcast_to(jnp.sum(x3, axis=1, keepdims=True), x3.shape).reshape(T, HGRN_HEAD)


def _hgrn_dims(S, SEG):
    T = min(HGRN_BLOCK, S)
    assert S % T == 0 and T % HGRN_TILE == 0
    tiles = [slice(t * HGRN_TILE, (t + 1) * HGRN_TILE) for t in range(T // HGRN_TILE)]
    chunks = [slice(c * HGRN_CHUNK, (c + 1) * HGRN_CHUNK) for c in range(T // HGRN_CHUNK)]
    return SEG // HGRN_HEAD, T, T // HGRN_CHUNK, S // T, tiles, chunks


def _hgrn_fwd(zf32, lb_logits, gnorm):
    _, NLB, S, _ = zf32.shape
    SEG = NLB * LANES
    H, T, NC, NJ, tiles, chunks = _hgrn_dims(S, SEG)
    HP = min(HGRN_HEADS_PER_STEP, H)
    assert H % HP == 0

    def body(zq_ref, zf_ref, zi_ref, zg_ref, lbl_ref, gn_ref, y_ref, st_ref, state):
        @pl.when(pl.program_id(1) == 0)
        def _():
            state[...] = jnp.zeros_like(state)

        tril, _ = _tile_masks()
        tril_bf = tril.astype(BF16)
        for hh in range(HP):
            cols = slice(hh * HGRN_HEAD, (hh + 1) * HGRN_HEAD)
            lb = _lower_bound(lbl_ref[:, cols])
            zq = zq_ref[hh]
            q = zq * _sigmoid(zq)
            f = lb + (1.0 - lb) * _sigmoid(zf_ref[hh])
            k = 1.0 - f
            logf = jnp.log(f)
            b = jnp.concatenate([_exact_dot(tril_bf, logf[t]) for t in tiles], axis=0)
            bl = _chunk_last(b)
            qd_b = (q * jnp.exp(b)).astype(BF16)
            kd_b = (k * jnp.exp(-b)).astype(BF16)
            ke_b = (k * jnp.exp(bl - b)).astype(BF16)
            v_b = zi_ref[hh].astype(BF16)
            o_intra = jnp.concatenate(
                [_dot(jnp.where(tril, _dot_nt(qd_b[t], kd_b[t]), 0.0).astype(BF16), v_b[t]) for t in tiles], axis=0)
            kvs = [_dot_tn(v_b[r], ke_b[r]) for r in chunks]
            ebl = jnp.exp(bl)
            st = state[hh]
            sts = []
            for c in range(NC):
                st_ref[c, hh] = st
                sts.append(st.astype(BF16))
                st = st * ebl[c * HGRN_CHUNK:c * HGRN_CHUNK + 1, :] + kvs[c]
            state[hh] = st
            o = o_intra + jnp.concatenate([_dot_nt(qd_b[r], sb) for r, sb in zip(chunks, sts)], axis=0)
            on = o * lax.rsqrt(jnp.mean(o * o, axis=-1, keepdims=True) + NORM_EPS) * gn_ref[...]
            zg = zg_ref[hh]
            y_ref[:, cols] = (on * (zg * _sigmoid(zg))).astype(BF16)

    def zspec(seg):
        return pl.BlockSpec((None, HP, T, HGRN_HEAD), lambda h, j: (seg, h, j, 0))

    return pl.pallas_call(
        body, grid=(H // HP, NJ), name="hgrn_fwd",
        in_specs=[zspec(0), zspec(1), zspec(2), zspec(3),
                  pl.BlockSpec((2, HP * HGRN_HEAD), lambda h, j: (0, h)),
                  pl.BlockSpec((1, HGRN_HEAD), lambda h, j: (0, 0))],
        out_specs=[pl.BlockSpec((T, HP * HGRN_HEAD), lambda h, j: (j, h)),
                   pl.BlockSpec((NC, HP, HGRN_HEAD, HGRN_HEAD), lambda h, j: (j, h, 0, 0))],
        out_shape=[jax.ShapeDtypeStruct((S, SEG), BF16),
                   jax.ShapeDtypeStruct((S // HGRN_CHUNK, H, HGRN_HEAD, HGRN_HEAD), F32)],
        scratch_shapes=[pltpu.VMEM((HP, HGRN_HEAD, HGRN_HEAD), F32)],
        compiler_params=_cp(("parallel", "arbitrary")),
    )(zf32, zf32, zf32, zf32, lb_logits, gnorm)


def _hgrn_bwd(zf32, lb_logits, gnorm, states, dy):
    _, NLB, S, _ = zf32.shape
    SEG = NLB * LANES
    H, T, NC, NJ, tiles, chunks = _hgrn_dims(S, SEG)
    C = HGRN_CHUNK
    HP = min(HGRN_HEADS_PER_STEP, H)
    assert H % HP == 0

    def body(zq_ref, zf_ref, zi_ref, zg_ref, lbl_ref, gn_ref, st_ref, dy_ref, dz_ref, dl_ref, dgn_ref, gstate):
        @pl.when(pl.program_id(1) == 0)
        def _():
            gstate[...] = jnp.zeros_like(gstate)
            dl_ref[...] = jnp.zeros_like(dl_ref)
            dgn_ref[...] = jnp.zeros_like(dgn_ref)

        gn = gn_ref[...]
        tril, triu = _tile_masks()
        tril_bf = tril.astype(BF16)
        triu_bf = triu.astype(BF16)
        for hh in range(HP):
            cols = slice(hh * HGRN_HEAD, (hh + 1) * HGRN_HEAD)
            lb = _lower_bound(lbl_ref[:, cols])
            q, dq_dz = _silu_and_grad(zq_ref[hh])
            sf = _sigmoid(zf_ref[hh])
            f = lb + (1.0 - lb) * sf
            k = 1.0 - f
            logf = jnp.log(f)
            b = jnp.concatenate([_exact_dot(tril_bf, logf[t]) for t in tiles], axis=0)
            bl = _chunk_last(b)
            eb = jnp.exp(b)
            enb = jnp.exp(-b)
            ekl = jnp.exp(bl - b)
            ebl = jnp.exp(bl)
            qd = q * eb
            kd = k * enb
            ke = k * ekl
            qd_b = qd.astype(BF16)
            kd_b = kd.astype(BF16)
            ke_b = ke.astype(BF16)
            v_b = zi_ref[hh].astype(BF16)
            sts = [st_ref[c, hh] for c in range(NC)]
            sts_b = [s.astype(BF16) for s in sts]
            a_b = [jnp.where(tril, _dot_nt(qd_b[t], kd_b[t]), 0.0).astype(BF16) for t in tiles]
            o = (jnp.concatenate([_dot(a, v_b[t]) for a, t in zip(a_b, tiles)], axis=0)
                 + jnp.concatenate([_dot_nt(qd_b[r], sb) for r, sb in zip(chunks, sts_b)], axis=0))
            rinv = lax.rsqrt(jnp.mean(o * o, axis=-1, keepdims=True) + NORM_EPS)
            ohat = o * rinv
            sg, dsg = _silu_and_grad(zg_ref[hh])
            dyv = dy_ref[:, cols]
            don = dyv * sg
            dz_ref[3, :, cols] = (dyv * (ohat * gn) * dsg).astype(BF16)
            dgn_ref[hh] += jnp.sum(don * ohat, axis=0, keepdims=True)
            dohat = don * gn
            do = rinv * (dohat - ohat * jnp.mean(dohat * ohat, axis=-1, keepdims=True))
            do_b = do.astype(BF16)
            da_b = [jnp.where(tril, _dot_nt(do_b[t], v_b[t]), 0.0).astype(BF16) for t in tiles]
            dv_intra = jnp.concatenate([_dot_tn(a, do_b[t]) for a, t in zip(a_b, tiles)], axis=0)
            dqd_intra = jnp.concatenate([_dot(da, kd_b[t]) for da, t in zip(da_b, tiles)], axis=0)
            dkd = jnp.concatenate([_dot_tn(da, qd_b[t]) for da, t in zip(da_b, tiles)], axis=0)
            dqd_inter = jnp.concatenate([_dot(do_b[r], sb) for r, sb in zip(chunks, sts_b)], axis=0)
            gks = [_dot_tn(do_b[r], qd_b[r]) for r in chunks]
            g = gstate[hh]
            gs = [None] * NC
            for c in reversed(range(NC)):
                gs[c] = g
                g = g * ebl[c * C:c * C + 1, :] + gks[c]
            gstate[hh] = g
            gs_b = [x.astype(BF16) for x in gs]
            dv = dv_intra + jnp.concatenate([_dot_nt(ke_b[r], gb) for r, gb in zip(chunks, gs_b)], axis=0)
            dz_ref[2, :, cols] = dv.astype(BF16)
            dke = jnp.concatenate([_dot(v_b[r], gb) for r, gb in zip(chunks, gs_b)], axis=0)
            debl = jnp.concatenate(
                [jnp.broadcast_to(jnp.sum(x * s, axis=0, keepdims=True), (C, HGRN_HEAD)) for x, s in zip(gs, sts)], axis=0)
            dqd = dqd_intra + dqd_inter
            dz_ref[0, :, cols] = ((dqd * eb) * dq_dz).astype(BF16)
            t_ke = dke * ke
            db = dqd * qd - dkd * kd - t_ke
            db_last = _chunk_sum(t_ke) + debl * ebl
            dk = dkd * enb + dke * ekl
            dlogf = jnp.concatenate([_exact_dot(triu_bf, db[t]) for t in tiles], axis=0) + db_last
            df = dlogf / f - dk
            dz_ref[1, :, cols] = (df * (1.0 - lb) * (sf * (1.0 - sf))).astype(BF16)
            dlb = jnp.sum(df * (1.0 - sf), axis=0, keepdims=True)
            dl0 = dlb * lb * (1.0 - lb)
            dl_ref[0:1, cols] += dl0
            dl_ref[1:2, cols] -= dl0

    def zspec(seg):
        return pl.BlockSpec((None, HP, T, HGRN_HEAD), lambda h, j: (seg, h, NJ - 1 - j, 0))

    return pl.pallas_call(
        body, grid=(H // HP, NJ), name="hgrn_bwd",
        in_specs=[zspec(0), zspec(1), zspec(2), zspec(3),
                  pl.BlockSpec((2, HP * HGRN_HEAD), lambda h, j: (0, h)),
                  pl.BlockSpec((1, HGRN_HEAD), lambda h, j: (0, 0)),
                  pl.BlockSpec((NC, HP, HGRN_HEAD, HGRN_HEAD), lambda h, j: (NJ - 1 - j, h, 0, 0)),
                  pl.BlockSpec((T, HP * HGRN_HEAD), lambda h, j: (NJ - 1 - j, h))],
        out_specs=[pl.BlockSpec((4, T, HP * HGRN_HEAD), lambda h, j: (0, NJ - 1 - j, h)),
                   pl.BlockSpec((2, HP * HGRN_HEAD), lambda h, j: (0, h)),
                   pl.BlockSpec((HP, 1, HGRN_HEAD), lambda h, j: (h, 0, 0))],
        out_shape=[jax.ShapeDtypeStruct((4, S, SEG), BF16), jax.ShapeDtypeStruct((2, SEG), F32),
                   jax.ShapeDtypeStruct((H, 1, HGRN_HEAD), F32)],
        scratch_shapes=[pltpu.VMEM((HP, HGRN_HEAD, HGRN_HEAD), F32)],
        compiler_params=_cp(("parallel", "arbitrary")),
    )(zf32, zf32, zf32, zf32, lb_logits, gnorm, states, dy)


def _alibi_slopes(seg):
    n_heads = seg // ATTN_HEAD
    s = 2.0 ** (-8.0 * np.arange(1, n_heads + 1, dtype=np.float64) / n_heads)
    return jnp.asarray(np.repeat(s, ATTN_HEAD)[None, :], F32)


def _attn_dims(S, SEG, d):
    rb = BAND * d
    assert S % rb == 0 and SEG % LANES == 0
    npb = max(1, min(SEG // LANES, ATTN_BLOCK_ELEMS // (rb * LANES)))
    assert (SEG // LANES) % npb == 0
    return rb, npb, S // rb, (SEG // LANES) // npb


def _res_rows(r, d):
    return pl.ds(0, BAND) if d == 1 else pl.ds(r, BAND, stride=d)


def _for_residues(d, fn):
    if d == 1:
        fn(0)
    else:
        def step(r, carry):
            fn(r)
            return carry
        lax.fori_loop(0, d, step, 0, unroll=ATTN_UNROLL)


def _for_groups(d, n_pairs, fn):
    def over_pairs(r):
        for g0 in range(0, n_pairs, ATTN_UNROLL):
            fn([(r, p) for p in range(g0, min(n_pairs, g0 + ATTN_UNROLL))])

    if d == 1:
        over_pairs(0)
    elif n_pairs >= ATTN_UNROLL:
        def step(r, carry):
            over_pairs(r)
            return carry
        lax.fori_loop(0, d, step, 0)
    else:
        per_group = ATTN_UNROLL // n_pairs
        assert d % per_group == 0

        def step(g, carry):
            fn([(g * per_group + i, p) for i in range(per_group) for p in range(n_pairs)])
            return carry
        lax.fori_loop(0, d // per_group, step, 0)


def _band_terms(n, d):
    i = lax.broadcasted_iota(jnp.int32, (BAND, 2 * BAND), 0)
    jj = lax.broadcasted_iota(jnp.int32, (BAND, 2 * BAND), 1)
    delta = BAND + i - jj
    valid = (delta >= 0) & (delta <= BAND) & ((n > 0) | (jj >= BAND))
    return (-d * delta).astype(F32), valid


def _head_biases(slopes, nd, valid):
    out = []
    for s in _per_head(slopes):
        s2 = jnp.concatenate([s, s], axis=1)
        out.append(jnp.where(valid, s2 * nd, NEG))
    return jnp.concatenate(out, axis=0)


def _stack_heads(x):
    lane = lax.broadcasted_iota(jnp.int32, x.shape, 1)
    zero = jnp.zeros_like(x)
    return jnp.concatenate([jnp.where(lane < ATTN_HEAD, x, zero), jnp.where(lane < ATTN_HEAD, zero, x)], axis=0)


def _unstack_heads(x2):
    first = lax.broadcasted_iota(jnp.int32, (BAND, LANES), 1) < ATTN_HEAD
    return jnp.where(first, x2[:BAND], x2[BAND:])


def _stack_per_head(x):
    a, b = _per_head(x)
    col = jnp.concatenate([a, b], axis=0)
    return jnp.concatenate([col, col], axis=1)


def _per_head(x):
    lane = lax.broadcasted_iota(jnp.int32, x.shape, 1)
    sw = pltpu.roll(x, ATTN_HEAD, 1)
    first = lane < ATTN_HEAD
    return jnp.where(first, x, sw), jnp.where(first, sw, x)


def _qkv_source(zz, d):
    z, z16 = zz
    if d == DEINTERLEAVE:
        def take(ref, p, r):
            return ref.at[p][r]

        def spec(seg, np_, row_block):
            return pl.BlockSpec((None, np_, d, BAND, LANES), lambda c, n: (seg, c, 0, row_block(c, n), 0))
        return z, z16, take, spec

    def take(ref, p, r):
        return ref.at[p][_res_rows(r, d), :]

    def spec(seg, np_, row_block):
        return pl.BlockSpec((None, np_, BAND * d, LANES), lambda c, n: (SEG_QKV + seg, c, row_block(c, n), 0))
    return z, z, take, spec


def _attn_fwd(qkv, slopes, d):
    qkv, src, take, spec = _qkv_source(qkv, d)
    _, NLB, S, _ = qkv.shape
    rb, NP, nb, ncb = _attn_dims(S, NLB * LANES, d)

    def body(q_ref, kp_ref, kc_ref, vp_ref, vc_ref, sl_ref, o_ref, l_ref):
        n = pl.program_id(1)
        nd, valid = _band_terms(n, d)
        biases = [_head_biases(sl_ref[:, p * LANES:(p + 1) * LANES], nd, valid) for p in range(NP)]

        def group(items):
            scores, values = [], []
            for r, p in items:
                kc = jnp.concatenate([take(kp_ref, p, r), take(kc_ref, p, r)], axis=0).astype(BF16)
                values.append(jnp.concatenate([take(vp_ref, p, r), take(vc_ref, p, r)], axis=0).astype(BF16))
                scores.append(_dot_nt(_stack_heads((take(q_ref, p, r) * ATTN_SCALE).astype(BF16)), kc))
            probs = []
            for (r, p), s in zip(items, scores):
                s = s + biases[p]
                m = jnp.max(s, axis=-1, keepdims=True)
                e = jnp.exp(s - m)
                den = jnp.sum(e, axis=-1, keepdims=True)
                probs.append((e.astype(BF16), den, m + jnp.log(den)))
            for (r, p), vc, (e, den, lse) in zip(items, values, probs):
                rows = _res_rows(r, d)
                o_ref.at[p][rows, :] = _unstack_heads(_dot(e, vc) / den)
                l_ref.at[p][rows, :] = _unstack_heads(jnp.broadcast_to(lse, (2 * BAND, LANES)))

        _for_groups(d, NP, group)

    cur = lambda c, n: n
    prev = lambda c, n: jnp.maximum(n - 1, 0)
    out = pl.BlockSpec((NP, rb, LANES), lambda c, n: (c, n, 0))
    return pl.pallas_call(
        body, grid=(ncb, nb), name=f"attn_fwd_d{d}",
        in_specs=[spec(0, NP, cur), spec(1, NP, prev), spec(1, NP, cur), spec(2, NP, prev), spec(2, NP, cur),
                  pl.BlockSpec((1, NP * LANES), lambda c, n: (0, c))],
        out_specs=[out, out],
        out_shape=[jax.ShapeDtypeStruct((NLB, S, LANES), F32)] * 2,
        compiler_params=_cp(("parallel", "parallel")),
    )(src, src, src, src, src, slopes)


def _attn_merge(outs, lses, zf32):
    NLB, S, _ = outs[0].shape
    SEG = NLB * LANES
    tm = min(256, S)

    def body(o1, o2, o3, l1, l2, l3, zg_ref, o_ref, lse_ref, y_ref):
        a, b, c = l1[...], l2[...], l3[...]
        m = jnp.maximum(jnp.maximum(a, b), c)
        ea, eb, ec = jnp.exp(a - m), jnp.exp(b - m), jnp.exp(c - m)
        tot = ea + eb + ec
        o = (ea / tot) * o1[...] + (eb / tot) * o2[...] + (ec / tot) * o3[...]
        o_ref[...] = o
        lse_ref[...] = m + jnp.log(tot)
        zg = zg_ref[...]
        y = (o * (zg * _sigmoid(zg))).astype(BF16)
        for p in range(NLB):
            y_ref[:, p * LANES:(p + 1) * LANES] = y[p]

    blk = pl.BlockSpec((NLB, tm, LANES), lambda i: (0, i, 0))
    return pl.pallas_call(
        body, grid=(S // tm,), name="attn_merge",
        in_specs=[blk] * 6 + [pl.BlockSpec((None, NLB, tm, LANES), lambda i: (SEG_GATE_A, 0, i, 0))],
        out_specs=[blk, blk, pl.BlockSpec((tm, SEG), lambda i: (i, 0))],
        out_shape=[jax.ShapeDtypeStruct((NLB, S, LANES), F32), jax.ShapeDtypeStruct((NLB, S, LANES), F32),
                   jax.ShapeDtypeStruct((S, SEG), BF16)],
        compiler_params=_cp(("parallel",)),
    )(*outs, *lses, zf32)


def _attn_gate_bwd(dy, o, zf32):
    NP, S, _ = o.shape
    SEG = NP * LANES
    tm = min(256, S)

    def body(dy_ref, o_ref, zg_ref, do_ref, dl_ref, dzg_ref):
        r = lax.broadcasted_iota(jnp.int32, (LANES, LANES), 0) // ATTN_HEAD
        c = lax.broadcasted_iota(jnp.int32, (LANES, LANES), 1) // ATTN_HEAD
        same_head = (r == c).astype(BF16)
        for p in range(NP):
            cols = slice(p * LANES, (p + 1) * LANES)
            sg, dsg = _silu_and_grad(zg_ref[p])
            dyv = dy_ref[:, cols]
            ov = o_ref[p]
            do = dyv * sg
            do_ref[p] = do
            dzg_ref[:, cols] = (dyv * ov * dsg).astype(BF16)
            dl_ref[p] = _exact_dot_right(do * ov, same_head)

    blk = pl.BlockSpec((NP, tm, LANES), lambda i: (0, i, 0))
    return pl.pallas_call(
        body, grid=(S // tm,), name="attn_gate_bwd",
        in_specs=[pl.BlockSpec((tm, SEG), lambda i: (i, 1)), blk,
                  pl.BlockSpec((None, NP, tm, LANES), lambda i: (SEG_GATE_A, 0, i, 0))],
        out_specs=[blk, blk, pl.BlockSpec((None, tm, SEG), lambda i: (3, i, 0))],
        out_shape=[jax.ShapeDtypeStruct((NP, S, LANES), F32), jax.ShapeDtypeStruct((NP, S, LANES), F32),
                   jax.ShapeDtypeStruct((4, S, SEG), BF16)],
        compiler_params=_cp(("parallel",)),
    )(dy, o, zf32)


def _attn_bwd(qkv, slopes, do, lse, dl, d, acc, into):
    qkv, src, take, spec = _qkv_source(qkv, d)
    _, NLB, S, _ = qkv.shape
    SEG = NLB * LANES
    rb, NP, nb, ncb = _attn_dims(S, SEG, d)
    has_acc = acc is not None
    out_dtype = F32 if into is None else into.dtype
    assert into is None or d == 1

    def body(*refs):
        q_ref, kp_ref, kc_ref, vp_ref, vc_ref, sl_ref, do_ref, lse_ref, dl_ref = refs[:9]
        acc_ref = refs[9] if has_acc else None
        out_ref, cq, ck, cv = refs[-4:]
        n = pl.program_id(1)

        def emit(r, p, dq, dk, dv):
            rows = _res_rows(r, d)
            for t, val in enumerate((dq, dk, dv)):
                if has_acc:
                    val = val + acc_ref.at[t].at[p][rows, :]
                if into is None:
                    out_ref.at[t].at[p][rows, :] = val.astype(out_dtype)
                else:
                    out_ref.at[t][rows, p * LANES:(p + 1) * LANES] = val.astype(out_dtype)

        @pl.when(n == 0)
        def _():
            cq[...] = jnp.zeros_like(cq)
            ck[...] = jnp.zeros_like(ck)
            cv[...] = jnp.zeros_like(cv)

        @pl.when(n < nb)
        def _():
            nd, valid = _band_terms(n, d)
            biases = [_head_biases(sl_ref[:, p * LANES:(p + 1) * LANES], nd, valid) for p in range(NP)]

            def group(items):
                first = []
                for r, p in items:
                    rows = _res_rows(r, d)
                    kc = jnp.concatenate([take(kp_ref, p, r), take(kc_ref, p, r)], axis=0).astype(BF16)
                    vc = jnp.concatenate([take(vp_ref, p, r), take(vc_ref, p, r)], axis=0).astype(BF16)
                    qs = _stack_heads((take(q_ref, p, r) * ATTN_SCALE).astype(BF16))
                    dos = _stack_heads(do_ref.at[p][rows, :].astype(BF16))
                    first.append((kc, qs, dos, _dot_nt(qs, kc), _dot_nt(dos, vc)))
                second = []
                for (r, p), (kc, qs, dos, s, dp) in zip(items, first):
                    rows = _res_rows(r, d)
                    pr = jnp.exp(s + biases[p] - _stack_per_head(lse_ref.at[p][rows, :]))
                    ds = (pr * (dp - _stack_per_head(dl_ref.at[p][rows, :]))).astype(BF16)
                    second.append((kc, qs, dos, pr.astype(BF16), ds))
                for (r, p), (kc, qs, dos, pr, ds) in zip(items, second):
                    dq = _unstack_heads(_dot(ds, kc)) * ATTN_SCALE
                    dk = _dot_tn(ds, qs)
                    dv = _dot_tn(pr, dos)
                    emit(r, p, cq[r, p], ck[r, p] + dk[:BAND, :], cv[r, p] + dv[:BAND, :])
                    cq[r, p] = dq
                    ck[r, p] = dk[BAND:, :]
                    cv[r, p] = dv[BAND:, :]

            _for_groups(d, NP, group)

        @pl.when(n == nb)
        def _():
            def last(r):
                for p in range(NP):
                    emit(r, p, cq[r, p], ck[r, p], cv[r, p])
            _for_residues(d, last)

    cur = lambda c, n: (c, jnp.minimum(n, nb - 1), 0)
    lag = lambda c, n: (0, c, jnp.clip(n - 1, 0, nb - 1), 0)

    at = lambda c, n: jnp.minimum(n, nb - 1)
    before = lambda c, n: jnp.clip(n - 1, 0, nb - 1)
    in_specs = [spec(0, NP, at), spec(1, NP, before), spec(1, NP, at), spec(2, NP, before), spec(2, NP, at),
                pl.BlockSpec((1, NP * LANES), lambda c, n: (0, c))] + [pl.BlockSpec((NP, rb, LANES), cur)] * 3
    args = [src, src, src, src, src, slopes, do, lse, dl]
    aliases = {}
    if has_acc:
        in_specs.append(pl.BlockSpec((3, NP, rb, LANES), lag))
        args.append(acc)
        if into is None:
            aliases = {9: 0}
    if into is None:
        out_sds = jax.ShapeDtypeStruct((3, NLB, S, LANES), F32)
        out_spec = pl.BlockSpec((3, NP, rb, LANES), lag)
    else:
        in_specs.append(ANY)
        args.append(into)
        aliases = {len(args) - 1: 0}
        out_sds = jax.ShapeDtypeStruct(into.shape, into.dtype)
        out_spec = pl.BlockSpec((3, rb, NP * LANES), lambda c, n: (0, jnp.clip(n - 1, 0, nb - 1), c))
    return pl.pallas_call(
        body, grid=(ncb, nb + 1), name=f"attn_bwd_d{d}",
        in_specs=in_specs, out_specs=out_spec, out_shape=out_sds,
        scratch_shapes=[pltpu.VMEM((d, NP, BAND, LANES), F32)] * 3,
        input_output_aliases=aliases,
        compiler_params=_cp(("parallel", "arbitrary")),
    )(*args)


def _adamw(w, g, m, v, name):
    R, C = w.shape
    tr = R if R <= 256 else 256
    assert R % tr == 0

    def body(w_ref, g_ref, m_ref, v_ref, d_ref, nm_ref, nv_ref, go_ref):
        g = g_ref[...]
        nm = ADAM_B1 * m_ref[...] + (1.0 - ADAM_B1) * g
        nv = ADAM_B2 * v_ref[...] + (1.0 - ADAM_B2) * (g * g)
        m_hat = nm / (1.0 - ADAM_B1 ** ADAM_STEP)
        v_hat = nv / (1.0 - ADAM_B2 ** ADAM_STEP)
        d_ref[...] = -ADAM_LR * (m_hat / (jnp.sqrt(v_hat) + ADAM_EPS) + ADAM_WD * w_ref[...])
        nm_ref[...] = nm
        nv_ref[...] = nv
        go_ref[...] = g

    blk = pl.BlockSpec((tr, C), lambda i: (i, 0))
    sds = jax.ShapeDtypeStruct((R, C), F32)
    return pl.pallas_call(
        body, grid=(R // tr,), name=name, in_specs=[blk] * 4, out_specs=[blk] * 4, out_shape=[sds] * 4,
        compiler_params=_cp(("parallel",)),
    )(w, g, m, v)


def _coords():
    return lax.axis_index("x"), lax.axis_index("y"), lax.axis_index("c")


def _other_chips(x, y):
    return [(1 - x, y), (x, 1 - y), (1 - x, 1 - y)]


ANY = pl.BlockSpec(memory_space=pl.ANY)


def _cast_into_slot(w, where, name):
    R, C = w.shape
    tr = min(256, R)

    def body(where_ref, w_ref, o_ref):
        o_ref[...] = w_ref[...].astype(BF16)

    grid_spec = pltpu.PrefetchScalarGridSpec(
        num_scalar_prefetch=1, grid=(R // tr,),
        in_specs=[pl.BlockSpec((tr, C), lambda i, w: (i, 0))],
        out_specs=pl.BlockSpec((None, tr, C), lambda i, w: (w[1], i, 0)))
    return pl.pallas_call(
        body, grid_spec=grid_spec, name=name, out_shape=jax.ShapeDtypeStruct((4, R, C), BF16),
        compiler_params=_cp(("parallel",)),
    )(where, w)


def _pair_sum(g, sib, where, name):
    _, n2, C = g.shape
    N = n2 // 2
    tr = min(256, N)
    nt = N // tr

    def body(where_ref, g_ref, s_ref, qb_ref, own_ref):
        q = pl.program_id(1)
        tot = g_ref[...] + s_ref[...]
        qb_ref[...] = tot.astype(BF16)

        @pl.when(q == where_ref[1])
        def _():
            own_ref[...] = tot

    grid_spec = pltpu.PrefetchScalarGridSpec(
        num_scalar_prefetch=1, grid=(nt, 4),
        in_specs=[pl.BlockSpec((None, tr, C), lambda i, q, w: (q, w[0] * nt + i, 0)),
                  pl.BlockSpec((None, tr, C), lambda i, q, w: (q, i, 0))],
        out_specs=[pl.BlockSpec((None, tr, C), lambda i, q, w: (q, i, 0)),
                   pl.BlockSpec((tr, C), lambda i, q, w: (i, 0))])
    return pl.pallas_call(
        body, grid_spec=grid_spec, name=name,
        out_shape=[jax.ShapeDtypeStruct((4, N, C), BF16), jax.ShapeDtypeStruct((N, C), F32)],
        compiler_params=_cp(("parallel", "arbitrary")),
    )(where, g, sib)


HBM = pl.BlockSpec(memory_space=pltpu.HBM)
SEM = pl.BlockSpec(memory_space=pltpu.SEMAPHORE)


def _in_hbm(a):
    return pltpu.with_memory_space_constraint(a, pltpu.HBM)


def _split_start(name, copies, arrays, n_sems, after=None):
    n = len(arrays)

    def body(*refs):
        for cp in copies(refs[:n], refs[-n - 3], refs[-n - 2]):
            cp.start()
        refs[-1][...] = jnp.zeros_like(refs[-1])

    ordered = () if after is None else (after,)
    outs = pl.pallas_call(
        body, name=name,
        out_shape=(pltpu.SemaphoreType.DMA((n_sems,)), pltpu.SemaphoreType.DMA((n_sems,)),
                   *[pltpu.HBM(a.shape, a.dtype) for a in arrays], jax.ShapeDtypeStruct((8, LANES), F32)),
        in_specs=(HBM,) * n + (ANY,) * len(ordered),
        out_specs=(SEM, SEM) + (HBM,) * n + (pl.BlockSpec(memory_space=pltpu.VMEM),),
        input_output_aliases={i: 2 + i for i in range(n)},
        compiler_params=pltpu.CompilerParams(has_side_effects=pltpu.SideEffectType.DATAFLOW_SIDE_EFFECTING),
    )(*[_in_hbm(a) for a in arrays], *ordered)
    return outs[0], outs[1], list(outs[2:2 + n]), outs[-1]


def _split_wait(name, copies, send_sems, recv_sems, arrays, after):
    n = len(arrays)

    def body(*refs):
        for cp in copies(refs[:n], refs[n], refs[n + 1]):
            cp.wait_send()
            cp.wait_recv()

    outs = pl.pallas_call(
        body, name=name,
        out_shape=tuple(pltpu.HBM(a.shape, a.dtype) for a in arrays),
        in_specs=(HBM,) * n + (SEM, SEM, ANY), out_specs=(HBM,) * n,
        input_output_aliases={i: i for i in range(n)},
        compiler_params=pltpu.CompilerParams(has_side_effects=pltpu.SideEffectType.DATAFLOW_SIDE_EFFECTING),
    )(*arrays, send_sems, recv_sems, after)
    return list(outs)


def _remote(src, dst, sems, k, to):
    send_sems, recv_sems = sems
    return pltpu.make_async_remote_copy(src_ref=src, dst_ref=dst, send_sem=send_sems.at[k], recv_sem=recv_sems.at[k],
                                        device_id=to, device_id_type=MESH)


def _chip_at(x, y, rel):
    px = 1 - x if rel & 2 else x
    py = 1 - y if rel & 1 else y
    return px, py, 2 * px + py


def _gather_in_copies(rels):
    def copies(refs, send_sems, recv_sems):
        (w,) = refs
        x, y, c = _coords()
        seg = w.shape[2] // 2
        mine = w.at[2 * x + y, :, pl.ds(c * seg, seg)]
        return [_remote(mine, mine, (send_sems, recv_sems), k, _chip_at(x, y, rel)[:2] + (c,))
                for k, rel in enumerate(rels)]
    return copies


def _gather_out_copies(refs, send_sems, recv_sems):
    (w,) = refs
    x, y, c = _coords()
    mine = w.at[2 * x + y]
    return [_remote(mine, mine, (send_sems, recv_sems), k, (px, py, c)) for k, (px, py) in enumerate(_other_chips(x, y))]


def _swap_copies(refs, send_sems, recv_sems):
    gi, go, si, so = refs
    x, y, c = _coords()
    cps = []
    for a, (src, dst) in enumerate(((gi, si), (go, so))):
        nr = dst.shape[1]
        cps.append(_remote(src.at[:, pl.ds((1 - c) * nr, nr), :], dst, (send_sems, recv_sems), a, (x, y, 1 - c)))
    return cps


def _scatter_copies(refs, send_sems, recv_sems):
    qi, qo, ri, ro = refs
    x, y, c = _coords()
    cps = []
    for k, (px, py) in enumerate(_other_chips(x, y)):
        for a, (src, dst) in enumerate(((qi, ri), (qo, ro))):
            cps.append(_remote(src.at[2 * px + py], dst.at[k], (send_sems, recv_sems), 2 * k + a, (px, py, c)))
    return cps


def _forward_copies(rels):
    def copies(refs, send_sems, recv_sems):
        (w,) = refs
        x, y, c = _coords()
        seg = w.shape[2] // 2
        cps = []
        for k, rel in enumerate(rels):
            got = w.at[_chip_at(x, y, rel)[2], :, pl.ds(c * seg, seg)]
            cps.append(_remote(got, got, (send_sems, recv_sems), k, (x, y, 1 - c)))
        return cps
    return copies


def _chip_sum(own, got, where, name):
    N, C = own.shape
    tr = min(256, N)
    nt = N // tr

    def body(where_ref, own_ref, got_ref, o_ref):
        t = own_ref[...]
        for k in range(3):
            t = t + got_ref[k].astype(F32)
        o_ref[...] = t

    grid_spec = pltpu.PrefetchScalarGridSpec(
        num_scalar_prefetch=1, grid=(nt,),
        in_specs=[pl.BlockSpec((tr, C), lambda i, w: (i, 0)), pl.BlockSpec((3, tr, C), lambda i, w: (0, i, 0))],
        out_specs=pl.BlockSpec((tr, C), lambda i, w: (w[0] * nt + i, 0)))
    return pl.pallas_call(
        body, grid_spec=grid_spec, name=name, out_shape=jax.ShapeDtypeStruct((2 * N, C), F32),
        compiler_params=_cp(("parallel",)),
    )(where, own, got)


def _join_copies(refs, send_sems, recv_sems):
    x, y, c = _coords()
    cps = []
    for a, ref in enumerate(refs):
        nr = ref.shape[0] // 2
        mine = ref.at[pl.ds(c * nr, nr), :]
        cps.append(_remote(mine, mine, (send_sems, recv_sems), a, (x, y, 1 - c)))
    return cps


def _all_reduce_small(part, token):
    R, C = part.shape

    def body(p_ref, _, o_ref, slots, send_sems, recv_sems):
        x, y, c = _coords()
        me = 4 * x + 2 * y + c
        slots[me] = p_ref[...]
        cps = []
        for k in range(1, 8):
            fx, fy, fc = (k >> 2) & 1, (k >> 1) & 1, k & 1
            peer = (1 - x if fx else x, 1 - y if fy else y, 1 - c if fc else c)
            cp = pltpu.make_async_remote_copy(src_ref=p_ref, dst_ref=slots.at[me], send_sem=send_sems.at[k - 1],
                                              recv_sem=recv_sems.at[k - 1], device_id=peer, device_id_type=MESH)
            cp.start()
            cps.append(cp)
        for cp in cps:
            cp.wait()
        t = slots[0]
        for k in range(1, 8):
            t = t + slots[k]
        o_ref[...] = t

    vm = pl.BlockSpec(memory_space=pltpu.VMEM)
    return pl.pallas_call(
        body, name="all_reduce_small", in_specs=[vm, vm], out_specs=vm,
        out_shape=jax.ShapeDtypeStruct((R, C), F32),
        scratch_shapes=[pltpu.VMEM((8, R, C), F32), pltpu.SemaphoreType.DMA((7,)), pltpu.SemaphoreType.DMA((7,))],
    )(part, token)


def _mixers_forward(z, lb_logits, hgrn_gnorm):
    slopes = _alibi_slopes(z[0].shape[1] * LANES)
    yh, states = _hgrn_fwd(z[0], lb_logits, hgrn_gnorm)
    outs, lses = [], []
    for d in DILATIONS:
        o, l = _attn_fwd(z, slopes, d)
        outs.append(o)
        lses.append(l)
    o_attn, lse, ya = _attn_merge(outs, lses, z[0])
    return yh, ya, (states, o_attn, lse, slopes)


def _backward_to_dz(z, kept, lb_logits, hgrn_gnorm, yh, ya, w_out_all, x2, tgt, fgain, h):
    states, o_attn, lse, slopes = kept
    dout, doutb, loss, dfg = _out_proj_loss(yh, ya, w_out_all, x2, tgt, fgain)
    dy = _dy_proj(doutb, w_out_all)
    g_w_out = _grad_w_out(yh, ya, doutb)
    dzh, dlogits, dgn = _hgrn_bwd(z[0], lb_logits, hgrn_gnorm, states, dy)
    do, dl, dza = _attn_gate_bwd(dy, o_attn, z[0])
    acc = None
    order = sorted(DILATIONS, reverse=True)
    for d in order[:-1]:
        acc = _attn_bwd(z, slopes, do, lse, dl, d, acc, None)
    dza = _attn_bwd(z, slopes, do, lse, dl, order[-1], acc, dza)
    sources = [dzh, dza]
    g_w_in = _grad_w_in(h, sources)
    return loss, dfg, dlogits, dgn, g_w_out, g_w_in, sources, dout


def _grad_x_half(sources, w_all, x2, rinv, norm_gain, dout, token, part, gx_prev):
    dh = _dh_proj(sources, w_all, token, part, f"dh_proj_{part}")
    return _rms_bwd(dh, x2, rinv, norm_gain, dout, part, gx_prev, f"rms_bwd_{part}")


def _local_step(x2, tgt, norm_gain, w_all, lb_logits, hgrn_gnorm, w_out_all, fgain):
    token = jnp.zeros((8, LANES), F32)
    where = jnp.zeros((2,), jnp.int32)
    h, rinv = _rms_fwd(x2, norm_gain, token)
    z = _in_proj(h, w_all, where, [(rel, half) for rel in range(4) for half in range(2)], None, token, "in_proj_all")
    yh, ya, kept = _mixers_forward(z, lb_logits, hgrn_gnorm)
    loss, dfg, dlogits, dgn, g_w_out, g_w_in, sources, dout = _backward_to_dz(
        z, kept, lb_logits, hgrn_gnorm, yh, ya, w_out_all, x2, tgt, fgain, h)
    gx, dg0 = _grad_x_half(sources, w_all, x2, rinv, norm_gain, dout, token, 0, None)
    gx, dg1 = _grad_x_half(sources, w_all, x2, rinv, norm_gain, dout, token, 1, gx)
    return loss, gx, dg0 + dg1, g_w_in, dlogits, dgn, g_w_out, dfg


def _pack_small(D, loss, dgain, dlogits, dgn, dfg):
    def row(v):
        v = v.reshape(1, -1)
        return jnp.pad(v, ((0, 0), (0, D - v.shape[1])))
    rows = [row(dgain), row(dfg), row(dlogits[0]), row(dlogits[1]), row(jnp.sum(dgn, axis=0)), row(loss)]
    rows += [jnp.zeros((1, D), F32)] * (8 - len(rows))
    return jnp.concatenate(rows, axis=0)


def kernel(x, norm_gain, w_in, lb_logits, hgrn_gnorm, w_out, final_gain, loss_target, m_norm_gain, m_w_in, m_lb_logits, m_hgrn_gnorm, m_w_out, m_final_gain, v_norm_gain, v_w_in, v_lb_logits, v_hgrn_gnorm, v_w_out, v_final_gain):
    _, S, D = x.shape
    SEG = w_in.shape[2] // 2
    x2 = x[0]
    tgt = loss_target[0]
    fgain = final_gain.reshape(1, D)
    where = jnp.stack([lax.axis_index("c"), 2 * lax.axis_index("x") + lax.axis_index("y")]).astype(jnp.int32)

    wia = _cast_into_slot(w_in[0], where, "cast_w_in")
    woa = _cast_into_slot(w_out[0], where, "cast_w_out")
    near, far = (2, 1), (3,)
    ga = _split_start("gather_near_start", _gather_in_copies(near), [wia], 2)
    h, rinv = _rms_fwd(x2, norm_gain, ga[3])
    z = _in_proj(h, ga[2][0], where, [(0, 0), (0, 1)], None, ga[3], "in_proj_own")
    (wia,) = _split_wait("gather_near_wait", _gather_in_copies(near), ga[0], ga[1], ga[2], z[0])
    gb = _split_start("gather_far_start", _gather_in_copies(far), [wia], 1)
    fa = _split_start("forward_near_start", _forward_copies(near), gb[2], 2, after=gb[3])
    z = _in_proj(h, fa[2][0], where, [(2, "mine"), (1, "mine")], z, fa[3], "in_proj_near")
    (wia,) = _split_wait("forward_near_wait", _forward_copies(near), fa[0], fa[1], fa[2], z[0])
    (wia,) = _split_wait("gather_far_wait", _gather_in_copies(far), gb[0], gb[1], [wia], z[0])
    out_sems = _split_start("gather_out_start", _gather_out_copies, [woa], 3, after=wia)
    fb = _split_start("forward_far_start", _forward_copies(far), [wia], 1, after=out_sems[3])
    z = _in_proj(h, fb[2][0], where, [(3, "mine"), (2, "sibling"), (1, "sibling")], z, fb[3], "in_proj_far")
    (wia,) = _split_wait("forward_far_wait", _forward_copies(far), fb[0], fb[1], fb[2], z[0])
    z = _in_proj(h, wia, where, [(3, "sibling")], z, fb[3], "in_proj_last")
    yh, ya, kept = _mixers_forward(z, lb_logits, hgrn_gnorm)
    (woa,) = _split_wait("gather_out_wait", _gather_out_copies, out_sems[0], out_sems[1], out_sems[2], ya)
    w_out_all = woa.reshape(2 * SEG, D)

    loss, dfg, dlogits, dgn, g_w_out, g_w_in, sources, dout = _backward_to_dz(
        z, kept, lb_logits, hgrn_gnorm, yh, ya, w_out_all, x2, tgt, fgain, h)

    sib_i = lax.empty((4, g_w_in.shape[1] // 2, g_w_in.shape[2]), F32)
    sib_o = lax.empty((4, g_w_out.shape[1] // 2, g_w_out.shape[2]), F32)
    sems = _split_start("swap_start", _swap_copies, [g_w_in, g_w_out, sib_i, sib_o], 2)
    grad_x, dg0 = _grad_x_half(sources, wia, x2, rinv, norm_gain, dout, sems[3], 0, None)
    g_w_in, g_w_out, sib_i, sib_o = _split_wait("swap_wait", _swap_copies, sems[0], sems[1], sems[2], grad_x)
    qi, own_i = _pair_sum(g_w_in, sib_i, where, "pair_sum_w_in")
    qo, own_o = _pair_sum(g_w_out, sib_o, where, "pair_sum_w_out")
    ri = lax.empty((3,) + qi.shape[1:], BF16)
    ro = lax.empty((3,) + qo.shape[1:], BF16)
    sems = _split_start("scatter_start", _scatter_copies, [qi, qo, ri, ro], 6)
    grad_x, dg1 = _grad_x_half(sources, wia, x2, rinv, norm_gain, dout, sems[3], 1, grad_x)
    _, _, got_i, got_o = _split_wait("scatter_wait", _scatter_copies, sems[0], sems[1], sems[2], grad_x)
    jn = _split_start("join_start", _join_copies, [_chip_sum(own_i, got_i, where, "chip_sum_w_in"),
                                                   _chip_sum(own_o, got_o, where, "chip_sum_w_out")], 2)
    small = _all_reduce_small(_pack_small(D, loss, dg0 + dg1, dlogits, dgn, dfg), jn[3])
    loss_sum = small[5, 0]
    d_ng, m_ng, v_ng, grad_norm_gain = _adamw(norm_gain, small[0:1, :], m_norm_gain, v_norm_gain, "adamw_norm_gain")
    d_lb, m_lb, v_lb, grad_lb_logits = _adamw(lb_logits, small[2:4, :SEG], m_lb_logits, v_lb_logits, "adamw_lb_logits")
    d_gn, m_gn, v_gn, grad_hgrn_gnorm = _adamw(hgrn_gnorm, small[4:5, :HGRN_HEAD], m_hgrn_gnorm, v_hgrn_gnorm,
                                               "adamw_hgrn_gnorm")
    d_fg, m_fg, v_fg, grad_final_gain = _adamw(fgain, small[1:2, :], m_final_gain.reshape(1, D),
                                               v_final_gain.reshape(1, D), "adamw_final_gain")
    g_w_in, g_w_out = _split_wait("join_wait", _join_copies, jn[0], jn[1], jn[2], d_fg)
    d_wi, m_wi, v_wi, grad_w_in = _adamw(w_in[0], g_w_in, m_w_in[0], v_w_in[0], "adamw_w_in")
    d_wo, m_wo, v_wo, grad_w_out = _adamw(w_out[0], g_w_out, m_w_out[0], v_w_out[0], "adamw_w_out")

    return (loss_sum, grad_x[None],
            grad_norm_gain, grad_w_in[None], grad_lb_logits, grad_hgrn_gnorm, grad_w_out[None], grad_final_gain[0],
            d_ng, d_wi[None], d_lb, d_gn, d_wo[None], d_fg[0],
            m_ng, m_wi[None], m_lb, m_gn, m_wo[None], m_fg[0],
            v_ng, v_wi[None], v_lb, v_gn, v_wo[None], v_fg[0])
```

```python
import jax
import jax.numpy as jnp
import numpy as np
from jax import lax
from jax.experimental import pallas as pl
from jax.experimental.pallas import tpu as pltpu

F32 = jnp.float32
BF16 = jnp.bfloat16
MESH = pl.DeviceIdType.MESH

NORM_EPS = 1e-6
HGRN_HEAD = 128
HGRN_CHUNK = 64
HGRN_TILE = 128
HGRN_BLOCK = 512
HGRN_HEADS_PER_STEP = 4
ATTN_HEAD = 64
LANES = 128
BAND = 128
DILATIONS = (1, 4, 16)
DEINTERLEAVE = 16
ATTN_SCALE = ATTN_HEAD ** -0.5
assert ATTN_SCALE == 0.125
ATTN_BLOCK_ELEMS = BAND * 2048
ATTN_UNROLL = 4
SEG_QKV = 4
SEG_GATE_A = 7
NEG = -1e30

ADAM_LR = 0.001
ADAM_B1 = 0.9
ADAM_B2 = 0.999
ADAM_EPS = 1e-08
ADAM_WD = 0.01
ADAM_STEP = 10

MIB = 1024 * 1024


def _cp(semantics=None, vmem_mib=48):
    return pltpu.CompilerParams(dimension_semantics=semantics, vmem_limit_bytes=vmem_mib * MIB)


def _dot(a, b):
    return jnp.dot(a, b, preferred_element_type=F32)


def _dot_nt(a, b):
    return lax.dot_general(a, b, (((1,), (1,)), ((), ())), preferred_element_type=F32)


def _dot_tn(a, b):
    return lax.dot_general(a, b, (((0,), (0,)), ((), ())), preferred_element_type=F32)


def _split3(x):
    hi = x.astype(BF16)
    r1 = x - hi.astype(F32)
    mid = r1.astype(BF16)
    lo = (r1 - mid.astype(F32)).astype(BF16)
    return hi, mid, lo


def _exact_dot(t_bf16, x):
    hi, mid, lo = _split3(x)
    return _dot(t_bf16, hi) + _dot(t_bf16, mid) + _dot(t_bf16, lo)


def _exact_dot_right(x, t_bf16):
    hi, mid, lo = _split3(x)
    return _dot(hi, t_bf16) + _dot(mid, t_bf16) + _dot(lo, t_bf16)


def _sigmoid(z):
    return jax.nn.sigmoid(z)


def _silu_and_grad(z):
    s = _sigmoid(z)
    return z * s, s * (1.0 + z * (1.0 - s))


def _seg_select(j, values):
    out = values[0]
    for t, v in enumerate(values[1:], 1):
        out = jnp.where(j == t, v, out)
    return out


def _rms_fwd(x2, gain, token):
    S, D = x2.shape
    tm = min(512, S)

    def body(x_ref, g_ref, _, h_ref, r_ref):
        x = x_ref[...]
        r = lax.rsqrt(jnp.mean(x * x, axis=-1, keepdims=True) + NORM_EPS)
        h_ref[...] = ((x * r) * g_ref[...]).astype(BF16)
        r_ref[...] = r

    return pl.pallas_call(
        body, grid=(S // tm,), name="rms_fwd",
        in_specs=[pl.BlockSpec((tm, D), lambda i: (i, 0)), pl.BlockSpec((1, D), lambda i: (0, 0)),
                  pl.BlockSpec(token.shape, lambda i: (0, 0))],
        out_specs=[pl.BlockSpec((tm, D), lambda i: (i, 0)), pl.BlockSpec((tm, 1), lambda i: (i, 0))],
        out_shape=[jax.ShapeDtypeStruct((S, D), BF16), jax.ShapeDtypeStruct((S, 1), F32)],
        compiler_params=_cp(("parallel",)),
    )(x2, gain, token)


def _in_proj(h, w_all, where, segs, z_prev, token, name):
    S, D = h.shape
    SEG = w_all.shape[2] // 2
    NLB = SEG // LANES
    tm = min(512, S)
    count = len(segs)
    DI = DEINTERLEAVE
    tu = tm // DI

    def is_qkv(seg):
        return (seg >= SEG_QKV) & (seg < SEG_QKV + 3)

    def seg_of(j, w):
        halves = {0: 0, 1: 1, "mine": w[0], "sibling": 1 - w[0]}
        cands = [2 * jnp.bitwise_xor(w[1], rel) + halves[half] for rel, half in segs]
        keys = [is_qkv(s).astype(jnp.int32) for s in cands]
        out = cands[0]
        for k in range(count):
            pos = (sum(jnp.where(keys[t] < keys[k], 1, 0) for t in range(count))
                   + sum(jnp.where(keys[t] == keys[k], 1, 0) for t in range(k)))
            out = jnp.where(pos == j, cands[k], out)
        return out

    def body(*refs):
        where_ref, h_ref, w_ref = refs[:3]
        o_ref, o16_ref = refs[-2:]
        res = _dot(h_ref[...], w_ref[...])
        for p in range(NLB):
            o_ref[p] = res[:, p * LANES:(p + 1) * LANES]

        @pl.when(is_qkv(seg_of(pl.program_id(0), where_ref)))
        def _():
            for p in range(NLB):
                for r in range(DI):
                    o16_ref[p, r] = o_ref.at[p][pl.ds(r, tu, stride=DI), :]

    def z16_map(j, i, w):
        seg = seg_of(j, w)
        return (jnp.where(is_qkv(seg), seg - SEG_QKV, 3), 0, 0, jnp.where(is_qkv(seg), i, 0), 0)

    in_specs = [pl.BlockSpec((tm, D), lambda j, i, w: (i, 0)),
                pl.BlockSpec((None, D, SEG), lambda j, i, w: (seg_of(j, w) // 2, 0, seg_of(j, w) % 2)),
                pl.BlockSpec(token.shape, lambda j, i, w: (0, 0))]
    args = [where, h, w_all, token]
    aliases = {}
    if z_prev is not None:
        in_specs += [ANY, ANY]
        args += list(z_prev)
        aliases = {4: 0, 5: 1}
    grid_spec = pltpu.PrefetchScalarGridSpec(
        num_scalar_prefetch=1, grid=(count, S // tm), in_specs=in_specs,
        out_specs=[pl.BlockSpec((None, NLB, tm, LANES), lambda j, i, w: (seg_of(j, w), 0, i, 0)),
                   pl.BlockSpec((None, NLB, DI, tu, LANES), z16_map)])
    return pl.pallas_call(
        body, grid_spec=grid_spec, name=name,
        out_shape=[jax.ShapeDtypeStruct((8, NLB, S, LANES), F32),
                   jax.ShapeDtypeStruct((4, NLB, DI, S // DI, LANES), F32)],
        input_output_aliases=aliases, compiler_params=_cp(("parallel", "parallel")),
    )(*args)


def _out_proj_loss(yh, ya, w_out, x2, tgt, fgain):
    S, D = x2.shape
    SEG = yh.shape[1]
    tm = min(256, S)
    parts = 2

    def body(yh_ref, ya_ref, w_ref, x_ref, t_ref, fg_ref, dout_ref, doutb_ref, loss_ref, dfg_ref):
        i = pl.program_id(0)

        @pl.when(i == 0)
        def _():
            loss_ref[...] = jnp.zeros_like(loss_ref)
            dfg_ref[...] = jnp.zeros_like(dfg_ref)

        fg = fg_ref[...]
        loss = jnp.zeros((1, 1), F32)
        dfg = jnp.zeros((1, D), F32)
        for rows in [pl.ds(p * (tm // parts), tm // parts) for p in range(parts)]:
            out = (x_ref[rows, :] + _dot(yh_ref[rows, :], w_ref[pl.ds(0, SEG), :])
                   + _dot(ya_ref[rows, :], w_ref[pl.ds(SEG, SEG), :]))
            r = lax.rsqrt(jnp.mean(out * out, axis=-1, keepdims=True) + NORM_EPS)
            n = out * r
            err = n * fg - t_ref[rows, :]
            loss = loss + 0.5 * jnp.sum(jnp.mean(err * err, axis=-1, keepdims=True), axis=0, keepdims=True)
            dy = err * (1.0 / D)
            dfg = dfg + jnp.sum(dy * n, axis=0, keepdims=True)
            dn = dy * fg
            dout = r * (dn - n * jnp.mean(dn * n, axis=-1, keepdims=True))
            dout_ref[rows, :] = dout
            doutb_ref[rows, :] = dout.astype(BF16)
        loss_ref[...] += loss
        dfg_ref[...] += dfg

    row = lambda i: (i, 0)
    fix = lambda i: (0, 0)
    return pl.pallas_call(
        body, grid=(S // tm,), name="out_proj_loss",
        in_specs=[pl.BlockSpec((tm, SEG), row), pl.BlockSpec((tm, SEG), row), pl.BlockSpec((2 * SEG, D), fix),
                  pl.BlockSpec((tm, D), row), pl.BlockSpec((tm, D), row), pl.BlockSpec((1, D), fix)],
        out_specs=[pl.BlockSpec((tm, D), row), pl.BlockSpec((tm, D), row), pl.BlockSpec((1, 1), fix),
                   pl.BlockSpec((1, D), fix)],
        out_shape=[jax.ShapeDtypeStruct((S, D), F32), jax.ShapeDtypeStruct((S, D), BF16),
                   jax.ShapeDtypeStruct((1, 1), F32), jax.ShapeDtypeStruct((1, D), F32)],
        compiler_params=_cp(("arbitrary",)),
    )(yh, ya, w_out, x2, tgt, fgain)


def _dy_proj(doutb, w_out):
    S, D = doutb.shape
    K = w_out.shape[0]
    tm = min(512, S)

    def body(d_ref, w_ref, o_ref):
        o_ref[...] = _dot_nt(d_ref[...], w_ref[...])

    return pl.pallas_call(
        body, grid=(S // tm,), name="dy_proj",
        in_specs=[pl.BlockSpec((tm, D), lambda i: (i, 0)), pl.BlockSpec((K, D), lambda i: (0, 0))],
        out_specs=pl.BlockSpec((tm, K), lambda i: (i, 0)),
        out_shape=jax.ShapeDtypeStruct((S, K), F32),
        compiler_params=_cp(("parallel",)),
    )(doutb, w_out)


def _grad_w_out(yh, ya, doutb):
    S, SEG = yh.shape
    D = doutb.shape[1]
    R = (2 * SEG) // 4
    nb_half = SEG // R
    tk = min(1024, S)

    def body(yh_ref, ya_ref, d_ref, o_ref):
        q = pl.program_id(0)
        k = pl.program_id(1)

        @pl.when(k == 0)
        def _():
            o_ref[...] = jnp.zeros_like(o_ref)

        @pl.when(q < nb_half)
        def _():
            o_ref[...] += _dot_tn(yh_ref[...], d_ref[...])

        @pl.when(q >= nb_half)
        def _():
            o_ref[...] += _dot_tn(ya_ref[...], d_ref[...])

    return pl.pallas_call(
        body, grid=(4, S // tk), name="grad_w_out",
        in_specs=[pl.BlockSpec((tk, R), lambda q, k: (k, jnp.minimum(q, nb_half - 1))),
                  pl.BlockSpec((tk, R), lambda q, k: (k, jnp.maximum(q - nb_half, 0))),
                  pl.BlockSpec((tk, D), lambda q, k: (k, 0))],
        out_specs=pl.BlockSpec((None, R, D), lambda q, k: (q, 0, 0)),
        out_shape=jax.ShapeDtypeStruct((4, R, D), F32),
        compiler_params=_cp(("parallel", "arbitrary")),
    )(yh, ya, doutb)


def _dz_sources(sources):
    counts = [s.shape[0] for s in sources]
    starts = [sum(counts[:k]) for k in range(len(counts))]
    assert sum(counts) == 8
    return counts, starts


def _row_part(S, part, tile):
    first = max(512, (S * 3 // 8) // 512 * 512)
    rows = first if part == 0 else S - first
    assert rows % tile == 0 and first % tile == 0
    return (0 if part == 0 else first // tile), rows // tile, rows


def _dh_proj(sources, w_all, token, part, name):
    S = sources[0].shape[1]
    D = w_all.shape[1]
    SEG = w_all.shape[2] // 2
    counts, starts = _dz_sources(sources)
    assert all(c % 2 == 0 for c in counts)
    ns = len(sources)
    tm = 512
    t0, nt, nrows = _row_part(S, part, tm)

    def body(*refs):
        src = refs[:ns]
        w_ref, _, o_ref = refs[ns:]
        j = pl.program_id(1)

        @pl.when(j == 0)
        def _():
            o_ref[...] = jnp.zeros_like(o_ref)

        for k in range(ns):
            @pl.when((2 * j >= starts[k]) & (2 * j < starts[k] + counts[k]))
            def _(k=k):
                o_ref[...] += (_dot_nt(src[k][0], w_ref[:, pl.ds(0, SEG)])
                               + _dot_nt(src[k][1], w_ref[:, pl.ds(SEG, SEG)]))

    def src_spec(k):
        return pl.BlockSpec((2, tm, SEG),
                            lambda i, j: (jnp.clip(j - starts[k] // 2, 0, counts[k] // 2 - 1), t0 + i, 0))

    return pl.pallas_call(
        body, grid=(nt, 4), name=name,
        in_specs=[src_spec(k) for k in range(ns)] + [pl.BlockSpec((None, D, 2 * SEG), lambda i, j: (j, 0, 0)),
                                                     pl.BlockSpec(token.shape, lambda i, j: (0, 0))],
        out_specs=pl.BlockSpec((tm, D), lambda i, j: (i, 0)),
        out_shape=jax.ShapeDtypeStruct((nrows, D), F32),
        compiler_params=_cp(("parallel", "arbitrary")),
    )(*sources, w_all, token)


def _rms_bwd(dh, x2, rinv, gain, dout, part, gx_prev, name):
    S, D = x2.shape
    tm = 256
    t0, nt, _ = _row_part(S, part, tm)

    def body(dh_ref, x_ref, r_ref, g_ref, dout_ref, *rest):
        gx_ref, dg_ref = rest[-2:]

        @pl.when(pl.program_id(0) == 0)
        def _():
            dg_ref[...] = jnp.zeros_like(dg_ref)

        dh = dh_ref[...]
        r = r_ref[...]
        xhat = x_ref[...] * r
        dg_ref[...] += jnp.sum(dh * xhat, axis=0, keepdims=True)
        dxn = dh * g_ref[...]
        gx_ref[...] = dout_ref[...] + r * (dxn - xhat * jnp.mean(dxn * xhat, axis=-1, keepdims=True))

    row = lambda i: (t0 + i, 0)
    fix = lambda i: (0, 0)
    in_specs = [pl.BlockSpec((tm, D), lambda i: (i, 0)), pl.BlockSpec((tm, D), row), pl.BlockSpec((tm, 1), row),
                pl.BlockSpec((1, D), fix), pl.BlockSpec((tm, D), row)]
    args = [dh, x2, rinv, gain, dout]
    aliases = {}
    if gx_prev is not None:
        in_specs.append(ANY)
        args.append(gx_prev)
        aliases = {5: 0}
    return pl.pallas_call(
        body, grid=(nt,), name=name, in_specs=in_specs,
        out_specs=[pl.BlockSpec((tm, D), row), pl.BlockSpec((1, D), fix)],
        out_shape=[jax.ShapeDtypeStruct((S, D), F32), jax.ShapeDtypeStruct((1, D), F32)],
        input_output_aliases=aliases, compiler_params=_cp(("arbitrary",)),
    )(*args)


def _grad_w_in(h, sources):
    S, D = h.shape
    SEG = sources[0].shape[2]
    counts, starts = _dz_sources(sources)
    ns = len(sources)
    tk = min(1024, S)

    def body(*refs):
        h_ref = refs[0]
        src = refs[1:1 + ns]
        o_ref = refs[1 + ns]
        j = pl.program_id(0)
        k = pl.program_id(1)

        @pl.when(k == 0)
        def _():
            o_ref[...] = jnp.zeros_like(o_ref)

        for s in range(ns):
            @pl.when((j >= starts[s]) & (j < starts[s] + counts[s]))
            def _(s=s):
                o_ref[...] += _dot_tn(h_ref[...], src[s][...])

    def src_spec(s):
        return pl.BlockSpec((None, tk, SEG),
                            lambda j, k: (jnp.clip(j - starts[s], 0, counts[s] - 1), k, 0))

    return pl.pallas_call(
        body, grid=(8, S // tk), name="grad_w_in",
        in_specs=[pl.BlockSpec((tk, D), lambda j, k: (k, 0))] + [src_spec(s) for s in range(ns)],
        out_specs=pl.BlockSpec((None, D, SEG), lambda j, k: (j // 2, 0, j % 2)),
        out_shape=jax.ShapeDtypeStruct((4, D, 2 * SEG), F32),
        compiler_params=_cp(("parallel", "arbitrary")),
    )(h, *sources)


def _lower_bound(lbl):
    l0 = lbl[0:1, :]
    l1 = lbl[1:2, :]
    m = jnp.maximum(l0, l1)
    e0 = jnp.exp(l0 - m)
    e1 = jnp.exp(l1 - m)
    return e0 / (e0 + e1)


def _tile_masks():
    row = lax.broadcasted_iota(jnp.int32, (HGRN_TILE, HGRN_TILE), 0)
    col = lax.broadcasted_iota(jnp.int32, (HGRN_TILE, HGRN_TILE), 1)
    same = (row // HGRN_CHUNK) == (col // HGRN_CHUNK)
    return same & (row >= col), same & (row <= col)


def _chunk_last(b):
    T = b.shape[0]
    b3 = b.reshape(T // HGRN_CHUNK, HGRN_CHUNK, HGRN_HEAD)
    return jnp.broadcast_to(b3[:, HGRN_CHUNK - 1:HGRN_CHUNK, :], b3.shape).reshape(T, HGRN_HEAD)


def _chunk_sum(x):
    T = x.shape[0]
    x3 = x.reshape(T // HGRN_CHUNK, HGRN_CHUNK, HGRN_HEAD)
    return jnp.broadcast_to(jnp.sum(x3, axis=1, keepdims=True), x3.shape).reshape(T, HGRN_HEAD)


def _hgrn_dims(S, SEG):
    T = min(HGRN_BLOCK, S)
    assert S % T == 0 and T % HGRN_TILE == 0
    tiles = [slice(t * HGRN_TILE, (t + 1) * HGRN_TILE) for t in range(T // HGRN_TILE)]
    chunks = [slice(c * HGRN_CHUNK, (c + 1) * HGRN_CHUNK) for c in range(T // HGRN_CHUNK)]
    return SEG // HGRN_HEAD, T, T // HGRN_CHUNK, S // T, tiles, chunks


def _hgrn_fwd(zf32, lb_logits, gnorm):
    _, NLB, S, _ = zf32.shape
    SEG = NLB * LANES
    H, T, NC, NJ, tiles, chunks = _hgrn_dims(S, SEG)
    HP = min(HGRN_HEADS_PER_STEP, H)
    assert H % HP == 0

    def body(zq_ref, zf_ref, zi_ref, zg_ref, lbl_ref, gn_ref, y_ref, st_ref, state):
        @pl.when(pl.program_id(1) == 0)
        def _():
            state[...] = jnp.zeros_like(state)

        tril, _ = _tile_masks()
        tril_bf = tril.astype(BF16)
        for hh in range(HP):
            cols = slice(hh * HGRN_HEAD, (hh + 1) * HGRN_HEAD)
            lb = _lower_bound(lbl_ref[:, cols])
            zq = zq_ref[hh]
            q = zq * _sigmoid(zq)
            f = lb + (1.0 - lb) * _sigmoid(zf_ref[hh])
            k = 1.0 - f
            logf = jnp.log(f)
            b = jnp.concatenate([_exact_dot(tril_bf, logf[t]) for t in tiles], axis=0)
            bl = _chunk_last(b)
            qd_b = (q * jnp.exp(b)).astype(BF16)
            kd_b = (k * jnp.exp(-b)).astype(BF16)
            ke_b = (k * jnp.exp(bl - b)).astype(BF16)
            v_b = zi_ref[hh].astype(BF16)
            o_intra = jnp.concatenate(
                [_dot(jnp.where(tril, _dot_nt(qd_b[t], kd_b[t]), 0.0).astype(BF16), v_b[t]) for t in tiles], axis=0)
            kvs = [_dot_tn(v_b[r], ke_b[r]) for r in chunks]
            ebl = jnp.exp(bl)
            st = state[hh]
            sts = []
            for c in range(NC):
                st_ref[c, hh] = st
                sts.append(st.astype(BF16))
                st = st * ebl[c * HGRN_CHUNK:c * HGRN_CHUNK + 1, :] + kvs[c]
            state[hh] = st
            o = o_intra + jnp.concatenate([_dot_nt(qd_b[r], sb) for r, sb in zip(chunks, sts)], axis=0)
            on = o * lax.rsqrt(jnp.mean(o * o, axis=-1, keepdims=True) + NORM_EPS) * gn_ref[...]
            zg = zg_ref[hh]
            y_ref[:, cols] = (on * (zg * _sigmoid(zg))).astype(BF16)

    def zspec(seg):
        return pl.BlockSpec((None, HP, T, HGRN_HEAD), lambda h, j: (seg, h, j, 0))

    return pl.pallas_call(
        body, grid=(H // HP, NJ), name="hgrn_fwd",
        in_specs=[zspec(0), zspec(1), zspec(2), zspec(3),
                  pl.BlockSpec((2, HP * HGRN_HEAD), lambda h, j: (0, h)),
                  pl.BlockSpec((1, HGRN_HEAD), lambda h, j: (0, 0))],
        out_specs=[pl.BlockSpec((T, HP * HGRN_HEAD), lambda h, j: (j, h)),
                   pl.BlockSpec((NC, HP, HGRN_HEAD, HGRN_HEAD), lambda h, j: (j, h, 0, 0))],
        out_shape=[jax.ShapeDtypeStruct((S, SEG), BF16),
                   jax.ShapeDtypeStruct((S // HGRN_CHUNK, H, HGRN_HEAD, HGRN_HEAD), F32)],
        scratch_shapes=[pltpu.VMEM((HP, HGRN_HEAD, HGRN_HEAD), F32)],
        compiler_params=_cp(("parallel", "arbitrary")),
    )(zf32, zf32, zf32, zf32, lb_logits, gnorm)


def _hgrn_bwd(zf32, lb_logits, gnorm, states, dy):
    _, NLB, S, _ = zf32.shape
    SEG = NLB * LANES
    H, T, NC, NJ, tiles, chunks = _hgrn_dims(S, SEG)
    C = HGRN_CHUNK
    HP = min(HGRN_HEADS_PER_STEP, H)
    assert H % HP == 0

    def body(zq_ref, zf_ref, zi_ref, zg_ref, lbl_ref, gn_ref, st_ref, dy_ref, dz_ref, dl_ref, dgn_ref, gstate):
        @pl.when(pl.program_id(1) == 0)
        def _():
            gstate[...] = jnp.zeros_like(gstate)
            dl_ref[...] = jnp.zeros_like(dl_ref)
            dgn_ref[...] = jnp.zeros_like(dgn_ref)

        gn = gn_ref[...]
        tril, triu = _tile_masks()
        tril_bf = tril.astype(BF16)
        triu_bf = triu.astype(BF16)
        for hh in range(HP):
            cols = slice(hh * HGRN_HEAD, (hh + 1) * HGRN_HEAD)
            lb = _lower_bound(lbl_ref[:, cols])
            q, dq_dz = _silu_and_grad(zq_ref[hh])
            sf = _sigmoid(zf_ref[hh])
            f = lb + (1.0 - lb) * sf
            k = 1.0 - f
            logf = jnp.log(f)
            b = jnp.concatenate([_exact_dot(tril_bf, logf[t]) for t in tiles], axis=0)
            bl = _chunk_last(b)
            eb = jnp.exp(b)
            enb = jnp.exp(-b)
            ekl = jnp.exp(bl - b)
            ebl = jnp.exp(bl)
            qd = q * eb
            kd = k * enb
            ke = k * ekl
            qd_b = qd.astype(BF16)
            kd_b = kd.astype(BF16)
            ke_b = ke.astype(BF16)
            v_b = zi_ref[hh].astype(BF16)
            sts = [st_ref[c, hh] for c in range(NC)]
            sts_b = [s.astype(BF16) for s in sts]
            a_b = [jnp.where(tril, _dot_nt(qd_b[t], kd_b[t]), 0.0).astype(BF16) for t in tiles]
            o = (jnp.concatenate([_dot(a, v_b[t]) for a, t in zip(a_b, tiles)], axis=0)
                 + jnp.concatenate([_dot_nt(qd_b[r], sb) for r, sb in zip(chunks, sts_b)], axis=0))
            rinv = lax.rsqrt(jnp.mean(o * o, axis=-1, keepdims=True) + NORM_EPS)
            ohat = o * rinv
            sg, dsg = _silu_and_grad(zg_ref[hh])
            dyv = dy_ref[:, cols]
            don = dyv * sg
            dz_ref[3, :, cols] = (dyv * (ohat * gn) * dsg).astype(BF16)
            dgn_ref[hh] += jnp.sum(don * ohat, axis=0, keepdims=True)
            dohat = don * gn
            do = rinv * (dohat - ohat * jnp.mean(dohat * ohat, axis=-1, keepdims=True))
            do_b = do.astype(BF16)
            da_b = [jnp.where(tril, _dot_nt(do_b[t], v_b[t]), 0.0).astype(BF16) for t in tiles]
            dv_intra = jnp.concatenate([_dot_tn(a, do_b[t]) for a, t in zip(a_b, tiles)], axis=0)
            dqd_intra = jnp.concatenate([_dot(da, kd_b[t]) for da, t in zip(da_b, tiles)], axis=0)
            dkd = jnp.concatenate([_dot_tn(da, qd_b[t]) for da, t in zip(da_b, tiles)], axis=0)
            dqd_inter = jnp.concatenate([_dot(do_b[r], sb) for r, sb in zip(chunks, sts_b)], axis=0)
            gks = [_dot_tn(do_b[r], qd_b[r]) for r in chunks]
            g = gstate[hh]
            gs = [None] * NC
            for c in reversed(range(NC)):
                gs[c] = g
                g = g * ebl[c * C:c * C + 1, :] + gks[c]
            gstate[hh] = g
            gs_b = [x.astype(BF16) for x in gs]
            dv = dv_intra + jnp.concatenate([_dot_nt(ke_b[r], gb) for r, gb in zip(chunks, gs_b)], axis=0)
            dz_ref[2, :, cols] = dv.astype(BF16)
            dke = jnp.concatenate([_dot(v_b[r], gb) for r, gb in zip(chunks, gs_b)], axis=0)
            debl = jnp.concatenate(
                [jnp.broadcast_to(jnp.sum(x * s, axis=0, keepdims=True), (C, HGRN_HEAD)) for x, s in zip(gs, sts)], axis=0)
            dqd = dqd_intra + dqd_inter
            dz_ref[0, :, cols] = ((dqd * eb) * dq_dz).astype(BF16)
            t_ke = dke * ke
            db = dqd * qd - dkd * kd - t_ke
            db_last = _chunk_sum(t_ke) + debl * ebl
            dk = dkd * enb + dke * ekl
            dlogf = jnp.concatenate([_exact_dot(triu_bf, db[t]) for t in tiles], axis=0) + db_last
            df = dlogf / f - dk
            dz_ref[1, :, cols] = (df * (1.0 - lb) * (sf * (1.0 - sf))).astype(BF16)
            dlb = jnp.sum(df * (1.0 - sf), axis=0, keepdims=True)
            dl0 = dlb * lb * (1.0 - lb)
            dl_ref[0:1, cols] += dl0
            dl_ref[1:2, cols] -= dl0

    def zspec(seg):
        return pl.BlockSpec((None, HP, T, HGRN_HEAD), lambda h, j: (seg, h, NJ - 1 - j, 0))

    return pl.pallas_call(
        body, grid=(H // HP, NJ), name="hgrn_bwd",
        in_specs=[zspec(0), zspec(1), zspec(2), zspec(3),
                  pl.BlockSpec((2, HP * HGRN_HEAD), lambda h, j: (0, h)),
                  pl.BlockSpec((1, HGRN_HEAD), lambda h, j: (0, 0)),
                  pl.BlockSpec((NC, HP, HGRN_HEAD, HGRN_HEAD), lambda h, j: (NJ - 1 - j, h, 0, 0)),
                  pl.BlockSpec((T, HP * HGRN_HEAD), lambda h, j: (NJ - 1 - j, h))],
        out_specs=[pl.BlockSpec((4, T, HP * HGRN_HEAD), lambda h, j: (0, NJ - 1 - j, h)),
                   pl.BlockSpec((2, HP * HGRN_HEAD), lambda h, j: (0, h)),
                   pl.BlockSpec((HP, 1, HGRN_HEAD), lambda h, j: (h, 0, 0))],
        out_shape=[jax.ShapeDtypeStruct((4, S, SEG), BF16), jax.ShapeDtypeStruct((2, SEG), F32),
                   jax.ShapeDtypeStruct((H, 1, HGRN_HEAD), F32)],
        scratch_shapes=[pltpu.VMEM((HP, HGRN_HEAD, HGRN_HEAD), F32)],
        compiler_params=_cp(("parallel", "arbitrary")),
    )(zf32, zf32, zf32, zf32, lb_logits, gnorm, states, dy)


def _alibi_slopes(seg):
    n_heads = seg // ATTN_HEAD
    s = 2.0 ** (-8.0 * np.arange(1, n_heads + 1, dtype=np.float64) / n_heads)
    return jnp.asarray(np.repeat(s, ATTN_HEAD)[None, :], F32)


def _attn_dims(S, SEG, d):
    rb = BAND * d
    assert S % rb == 0 and SEG % LANES == 0
    npb = max(1, min(SEG // LANES, ATTN_BLOCK_ELEMS // (rb * LANES)))
    assert (SEG // LANES) % npb == 0
    return rb, npb, S // rb, (SEG // LANES) // npb


def _res_rows(r, d):
    return pl.ds(0, BAND) if d == 1 else pl.ds(r, BAND, stride=d)


def _for_residues(d, fn):
    if d == 1:
        fn(0)
    else:
        def step(r, carry):
            fn(r)
            return carry
        lax.fori_loop(0, d, step, 0, unroll=ATTN_UNROLL)


def _for_groups(d, n_pairs, fn):
    def over_pairs(r):
        for g0 in range(0, n_pairs, ATTN_UNROLL):
            fn([(r, p) for p in range(g0, min(n_pairs, g0 + ATTN_UNROLL))])

    if d == 1:
        over_pairs(0)
    elif n_pairs >= ATTN_UNROLL:
        def step(r, carry):
            over_pairs(r)
            return carry
        lax.fori_loop(0, d, step, 0)
    else:
        per_group = ATTN_UNROLL // n_pairs
        assert d % per_group == 0

        def step(g, carry):
            fn([(g * per_group + i, p) for i in range(per_group) for p in range(n_pairs)])
            return carry
        lax.fori_loop(0, d // per_group, step, 0)


def _band_terms(n, d):
    i = lax.broadcasted_iota(jnp.int32, (BAND, 2 * BAND), 0)
    jj = lax.broadcasted_iota(jnp.int32, (BAND, 2 * BAND), 1)
    delta = BAND + i - jj
    valid = (delta >= 0) & (delta <= BAND) & ((n > 0) | (jj >= BAND))
    return (-d * delta).astype(F32), valid


def _head_biases(slopes, nd, valid):
    out = []
    for s in _per_head(slopes):
        s2 = jnp.concatenate([s, s], axis=1)
        out.append(jnp.where(valid, s2 * nd, NEG))
    return jnp.concatenate(out, axis=0)


def _stack_heads(x):
    lane = lax.broadcasted_iota(jnp.int32, x.shape, 1)
    zero = jnp.zeros_like(x)
    return jnp.concatenate([jnp.where(lane < ATTN_HEAD, x, zero), jnp.where(lane < ATTN_HEAD, zero, x)], axis=0)


def _unstack_heads(x2):
    first = lax.broadcasted_iota(jnp.int32, (BAND, LANES), 1) < ATTN_HEAD
    return jnp.where(first, x2[:BAND], x2[BAND:])


def _stack_per_head(x):
    a, b = _per_head(x)
    col = jnp.concatenate([a, b], axis=0)
    return jnp.concatenate([col, col], axis=1)


def _per_head(x):
    lane = lax.broadcasted_iota(jnp.int32, x.shape, 1)
    sw = pltpu.roll(x, ATTN_HEAD, 1)
    first = lane < ATTN_HEAD
    return jnp.where(first, x, sw), jnp.where(first, sw, x)


def _qkv_source(zz, d):
    z, z16 = zz
    if d == DEINTERLEAVE:
        def take(ref, p, r):
            return ref.at[p][r]

        def spec(seg, np_, row_block):
            return pl.BlockSpec((None, np_, d, BAND, LANES), lambda c, n: (seg, c, 0, row_block(c, n), 0))
        return z, z16, take, spec

    def take(ref, p, r):
        return ref.at[p][_res_rows(r, d), :]

    def spec(seg, np_, row_block):
        return pl.BlockSpec((None, np_, BAND * d, LANES), lambda c, n: (SEG_QKV + seg, c, row_block(c, n), 0))
    return z, z, take, spec


def _attn_fwd(qkv, slopes, d):
    qkv, src, take, spec = _qkv_source(qkv, d)
    _, NLB, S, _ = qkv.shape
    rb, NP, nb, ncb = _attn_dims(S, NLB * LANES, d)

    def body(q_ref, kc_ref, vc_ref, sl_ref, o_ref, l_ref, kp_ref, vp_ref):
        n = pl.program_id(1)

        @pl.when(n == 0)
        def _():
            kp_ref[...] = jnp.zeros_like(kp_ref)
            vp_ref[...] = jnp.zeros_like(vp_ref)

        nd, valid = _band_terms(n, d)
        biases = [_head_biases(sl_ref[:, p * LANES:(p + 1) * LANES], nd, valid) for p in range(NP)]

        def group(items):
            scores, values = [], []
            for r, p in items:
                kc = jnp.concatenate([take(kp_ref, p, r), take(kc_ref, p, r)], axis=0).astype(BF16)
                values.append(jnp.concatenate([take(vp_ref, p, r), take(vc_ref, p, r)], axis=0).astype(BF16))
                scores.append(_dot_nt(_stack_heads((take(q_ref, p, r) * ATTN_SCALE).astype(BF16)), kc))
            probs = []
            for (r, p), s in zip(items, scores):
                s = s + biases[p]
                m = jnp.max(s, axis=-1, keepdims=True)
                e = jnp.exp(s - m)
                den = jnp.sum(e, axis=-1, keepdims=True)
                probs.append((e.astype(BF16), den, m + jnp.log(den)))
            for (r, p), vc, (e, den, lse) in zip(items, values, probs):
                rows = _res_rows(r, d)
                o_ref.at[p][rows, :] = _unstack_heads(_dot(e, vc) / den)
                l_ref.at[p][rows, :] = _unstack_heads(jnp.broadcast_to(lse, (2 * BAND, LANES)))

        _for_groups(d, NP, group)
        kp_ref[...] = kc_ref[...]
        vp_ref[...] = vc_ref[...]

    cur = lambda c, n: n
    out = pl.BlockSpec((NP, rb, LANES), lambda c, n: (c, n, 0))
    kv_block = spec(1, NP, cur).block_shape[1:]
    return pl.pallas_call(
        body, grid=(ncb, nb), name=f"attn_fwd_d{d}",
        in_specs=[spec(0, NP, cur), spec(1, NP, cur), spec(2, NP, cur),
                  pl.BlockSpec((1, NP * LANES), lambda c, n: (0, c))],
        out_specs=[out, out],
        out_shape=[jax.ShapeDtypeStruct((NLB, S, LANES), F32)] * 2,
        scratch_shapes=[pltpu.VMEM(kv_block, F32), pltpu.VMEM(kv_block, F32)],
        compiler_params=_cp(("parallel", "arbitrary")),
    )(src, src, src, slopes)


def _attn_merge(outs, lses, zf32):
    NLB, S, _ = outs[0].shape
    SEG = NLB * LANES
    tm = min(256, S)

    def body(o1, o2, o3, l1, l2, l3, zg_ref, o_ref, lse_ref, y_ref):
        a, b, c = l1[...], l2[...], l3[...]
        m = jnp.maximum(jnp.maximum(a, b), c)
        ea, eb, ec = jnp.exp(a - m), jnp.exp(b - m), jnp.exp(c - m)
        tot = ea + eb + ec
        o = (ea / tot) * o1[...] + (eb / tot) * o2[...] + (ec / tot) * o3[...]
        o_ref[...] = o
        lse_ref[...] = m + jnp.log(tot)
        zg = zg_ref[...]
        y = (o * (zg * _sigmoid(zg))).astype(BF16)
        for p in range(NLB):
            y_ref[:, p * LANES:(p + 1) * LANES] = y[p]

    blk = pl.BlockSpec((NLB, tm, LANES), lambda i: (0, i, 0))
    return pl.pallas_call(
        body, grid=(S // tm,), name="attn_merge",
        in_specs=[blk] * 6 + [pl.BlockSpec((None, NLB, tm, LANES), lambda i: (SEG_GATE_A, 0, i, 0))],
        out_specs=[blk, blk, pl.BlockSpec((tm, SEG), lambda i: (i, 0))],
        out_shape=[jax.ShapeDtypeStruct((NLB, S, LANES), F32), jax.ShapeDtypeStruct((NLB, S, LANES), F32),
                   jax.ShapeDtypeStruct((S, SEG), BF16)],
        compiler_params=_cp(("parallel",)),
    )(*outs, *lses, zf32)


def _attn_gate_bwd(dy, o, zf32):
    NP, S, _ = o.shape
    SEG = NP * LANES
    tm = min(256, S)

    def body(dy_ref, o_ref, zg_ref, do_ref, dl_ref, dzg_ref):
        r = lax.broadcasted_iota(jnp.int32, (LANES, LANES), 0) // ATTN_HEAD
        c = lax.broadcasted_iota(jnp.int32, (LANES, LANES), 1) // ATTN_HEAD
        same_head = (r == c).astype(BF16)
        for p in range(NP):
            cols = slice(p * LANES, (p + 1) * LANES)
            sg, dsg = _silu_and_grad(zg_ref[p])
            dyv = dy_ref[:, cols]
            ov = o_ref[p]
            do = dyv * sg
            do_ref[p] = do
            dzg_ref[:, cols] = (dyv * ov * dsg).astype(BF16)
            dl_ref[p] = _exact_dot_right(do * ov, same_head)

    blk = pl.BlockSpec((NP, tm, LANES), lambda i: (0, i, 0))
    return pl.pallas_call(
        body, grid=(S // tm,), name="attn_gate_bwd",
        in_specs=[pl.BlockSpec((tm, SEG), lambda i: (i, 1)), blk,
                  pl.BlockSpec((None, NP, tm, LANES), lambda i: (SEG_GATE_A, 0, i, 0))],
        out_specs=[blk, blk, pl.BlockSpec((None, tm, SEG), lambda i: (3, i, 0))],
        out_shape=[jax.ShapeDtypeStruct((NP, S, LANES), F32), jax.ShapeDtypeStruct((NP, S, LANES), F32),
                   jax.ShapeDtypeStruct((4, S, SEG), BF16)],
        compiler_params=_cp(("parallel",)),
    )(dy, o, zf32)


def _attn_bwd(qkv, slopes, do, lse, dl, d, acc, into):
    qkv, src, take, spec = _qkv_source(qkv, d)
    _, NLB, S, _ = qkv.shape
    SEG = NLB * LANES
    rb, NP, nb, ncb = _attn_dims(S, SEG, d)
    has_acc = acc is not None
    out_dtype = F32 if into is None else into.dtype
    assert into is None or d == 1

    def body(*refs):
        q_ref, kp_ref, kc_ref, vp_ref, vc_ref, sl_ref, do_ref, lse_ref, dl_ref = refs[:9]
        acc_ref = refs[9] if has_acc else None
        out_ref, cq, ck, cv = refs[-4:]
        n = pl.program_id(1)

        def emit(r, p, dq, dk, dv):
            rows = _res_rows(r, d)
            for t, val in enumerate((dq, dk, dv)):
                if has_acc:
                    val = val + acc_ref.at[t].at[p][rows, :]
                if into is None:
                    out_ref.at[t].at[p][rows, :] = val.astype(out_dtype)
                else:
                    out_ref.at[t][rows, p * LANES:(p + 1) * LANES] = val.astype(out_dtype)

        @pl.when(n == 0)
        def _():
            cq[...] = jnp.zeros_like(cq)
            ck[...] = jnp.zeros_like(ck)
            cv[...] = jnp.zeros_like(cv)

        @pl.when(n < nb)
        def _():
            nd, valid = _band_terms(n, d)
            biases = [_head_biases(sl_ref[:, p * LANES:(p + 1) * LANES], nd, valid) for p in range(NP)]

            def group(items):
                first = []
                for r, p in items:
                    rows = _res_rows(r, d)
                    kc = jnp.concatenate([take(kp_ref, p, r), take(kc_ref, p, r)], axis=0).astype(BF16)
                    vc = jnp.concatenate([take(vp_ref, p, r), take(vc_ref, p, r)], axis=0).astype(BF16)
                    qs = _stack_heads((take(q_ref, p, r) * ATTN_SCALE).astype(BF16))
                    dos = _stack_heads(do_ref.at[p][rows, :].astype(BF16))
                    first.append((kc, qs, dos, _dot_nt(qs, kc), _dot_nt(dos, vc)))
                second = []
                for (r, p), (kc, qs, dos, s, dp) in zip(items, first):
                    rows = _res_rows(r, d)
                    pr = jnp.exp(s + biases[p] - _stack_per_head(lse_ref.at[p][rows, :]))
                    ds = (pr * (dp - _stack_per_head(dl_ref.at[p][rows, :]))).astype(BF16)
                    second.append((kc, qs, dos, pr.astype(BF16), ds))
                for (r, p), (kc, qs, dos, pr, ds) in zip(items, second):
                    dq = _unstack_heads(_dot(ds, kc)) * ATTN_SCALE
                    dk = _dot_tn(ds, qs)
                    dv = _dot_tn(pr, dos)
                    emit(r, p, cq[r, p], ck[r, p] + dk[:BAND, :], cv[r, p] + dv[:BAND, :])
                    cq[r, p] = dq
                    ck[r, p] = dk[BAND:, :]
                    cv[r, p] = dv[BAND:, :]

            _for_groups(d, NP, group)

        @pl.when(n == nb)
        def _():
            def last(r):
                for p in range(NP):
                    emit(r, p, cq[r, p], ck[r, p], cv[r, p])
            _for_residues(d, last)

    cur = lambda c, n: (c, jnp.minimum(n, nb - 1), 0)
    lag = lambda c, n: (0, c, jnp.clip(n - 1, 0, nb - 1), 0)

    at = lambda c, n: jnp.minimum(n, nb - 1)
    before = lambda c, n: jnp.clip(n - 1, 0, nb - 1)
    in_specs = [spec(0, NP, at), spec(1, NP, before), spec(1, NP, at), spec(2, NP, before), spec(2, NP, at),
                pl.BlockSpec((1, NP * LANES), lambda c, n: (0, c))] + [pl.BlockSpec((NP, rb, LANES), cur)] * 3
    args = [src, src, src, src, src, slopes, do, lse, dl]
    aliases = {}
    if has_acc:
        in_specs.append(pl.BlockSpec((3, NP, rb, LANES), lag))
        args.append(acc)
        if into is None:
            aliases = {9: 0}
    if into is None:
        out_sds = jax.ShapeDtypeStruct((3, NLB, S, LANES), F32)
        out_spec = pl.BlockSpec((3, NP, rb, LANES), lag)
    else:
        in_specs.append(ANY)
        args.append(into)
        aliases = {len(args) - 1: 0}
        out_sds = jax.ShapeDtypeStruct(into.shape, into.dtype)
        out_spec = pl.BlockSpec((3, rb, NP * LANES), lambda c, n: (0, jnp.clip(n - 1, 0, nb - 1), c))
    return pl.pallas_call(
        body, grid=(ncb, nb + 1), name=f"attn_bwd_d{d}",
        in_specs=in_specs, out_specs=out_spec, out_shape=out_sds,
        scratch_shapes=[pltpu.VMEM((d, NP, BAND, LANES), F32)] * 3,
        input_output_aliases=aliases,
        compiler_params=_cp(("parallel", "arbitrary")),
    )(*args)


def _adamw(w, g, m, v, name):
    R, C = w.shape
    tr = R if R <= 256 else 256
    assert R % tr == 0

    def body(w_ref, g_ref, m_ref, v_ref, d_ref, nm_ref, nv_ref, go_ref):
        g = g_ref[...]
        nm = ADAM_B1 * m_ref[...] + (1.0 - ADAM_B1) * g
        nv = ADAM_B2 * v_ref[...] + (1.0 - ADAM_B2) * (g * g)
        m_hat = nm / (1.0 - ADAM_B1 ** ADAM_STEP)
        v_hat = nv / (1.0 - ADAM_B2 ** ADAM_STEP)
        d_ref[...] = -ADAM_LR * (m_hat / (jnp.sqrt(v_hat) + ADAM_EPS) + ADAM_WD * w_ref[...])
        nm_ref[...] = nm
        nv_ref[...] = nv
        go_ref[...] = g

    blk = pl.BlockSpec((tr, C), lambda i: (i, 0))
    sds = jax.ShapeDtypeStruct((R, C), F32)
    return pl.pallas_call(
        body, grid=(R // tr,), name=name, in_specs=[blk] * 4, out_specs=[blk] * 4, out_shape=[sds] * 4,
        compiler_params=_cp(("parallel",)),
    )(w, g, m, v)


def _coords():
    return lax.axis_index("x"), lax.axis_index("y"), lax.axis_index("c")


def _other_chips(x, y):
    return [(1 - x, y), (x, 1 - y), (1 - x, 1 - y)]


ANY = pl.BlockSpec(memory_space=pl.ANY)


def _cast_into_slot(w, where, name):
    R, C = w.shape
    tr = min(256, R)

    def body(where_ref, w_ref, o_ref):
        o_ref[...] = w_ref[...].astype(BF16)

    grid_spec = pltpu.PrefetchScalarGridSpec(
        num_scalar_prefetch=1, grid=(R // tr,),
        in_specs=[pl.BlockSpec((tr, C), lambda i, w: (i, 0))],
        out_specs=pl.BlockSpec((None, tr, C), lambda i, w: (w[1], i, 0)))
    return pl.pallas_call(
        body, grid_spec=grid_spec, name=name, out_shape=jax.ShapeDtypeStruct((4, R, C), BF16),
        compiler_params=_cp(("parallel",)),
    )(where, w)


def _pair_sum(g, sib, where, name):
    _, n2, C = g.shape
    N = n2 // 2
    tr = min(256, N)
    nt = N // tr

    def body(where_ref, g_ref, s_ref, qb_ref, own_ref):
        q = pl.program_id(1)
        tot = g_ref[...] + s_ref[...]
        qb_ref[...] = tot.astype(BF16)

        @pl.when(q == where_ref[1])
        def _():
            own_ref[...] = tot

    grid_spec = pltpu.PrefetchScalarGridSpec(
        num_scalar_prefetch=1, grid=(nt, 4),
        in_specs=[pl.BlockSpec((None, tr, C), lambda i, q, w: (q, w[0] * nt + i, 0)),
                  pl.BlockSpec((None, tr, C), lambda i, q, w: (q, i, 0))],
        out_specs=[pl.BlockSpec((None, tr, C), lambda i, q, w: (q, i, 0)),
                   pl.BlockSpec((tr, C), lambda i, q, w: (i, 0))])
    return pl.pallas_call(
        body, grid_spec=grid_spec, name=name,
        out_shape=[jax.ShapeDtypeStruct((4, N, C), BF16), jax.ShapeDtypeStruct((N, C), F32)],
        compiler_params=_cp(("parallel", "arbitrary")),
    )(where, g, sib)


HBM = pl.BlockSpec(memory_space=pltpu.HBM)
SEM = pl.BlockSpec(memory_space=pltpu.SEMAPHORE)


def _in_hbm(a):
    return pltpu.with_memory_space_constraint(a, pltpu.HBM)


def _split_start(name, copies, arrays, n_sems, after=None):
    n = len(arrays)

    def body(*refs):
        for cp in copies(refs[:n], refs[-n - 3], refs[-n - 2]):
            cp.start()
        refs[-1][...] = jnp.zeros_like(refs[-1])

    ordered = () if after is None else (after,)
    outs = pl.pallas_call(
        body, name=name,
        out_shape=(pltpu.SemaphoreType.DMA((n_sems,)), pltpu.SemaphoreType.DMA((n_sems,)),
                   *[pltpu.HBM(a.shape, a.dtype) for a in arrays], jax.ShapeDtypeStruct((8, LANES), F32)),
        in_specs=(HBM,) * n + (ANY,) * len(ordered),
        out_specs=(SEM, SEM) + (HBM,) * n + (pl.BlockSpec(memory_space=pltpu.VMEM),),
        input_output_aliases={i: 2 + i for i in range(n)},
        compiler_params=pltpu.CompilerParams(has_side_effects=pltpu.SideEffectType.DATAFLOW_SIDE_EFFECTING),
    )(*[_in_hbm(a) for a in arrays], *ordered)
    return outs[0], outs[1], list(outs[2:2 + n]), outs[-1]


def _split_wait(name, copies, send_sems, recv_sems, arrays, after):
    n = len(arrays)

    def body(*refs):
        for cp in copies(refs[:n], refs[n], refs[n + 1]):
            cp.wait_send()
            cp.wait_recv()

    outs = pl.pallas_call(
        body, name=name,
        out_shape=tuple(pltpu.HBM(a.shape, a.dtype) for a in arrays),
        in_specs=(HBM,) * n + (SEM, SEM, ANY), out_specs=(HBM,) * n,
        input_output_aliases={i: i for i in range(n)},
        compiler_params=pltpu.CompilerParams(has_side_effects=pltpu.SideEffectType.DATAFLOW_SIDE_EFFECTING),
    )(*arrays, send_sems, recv_sems, after)
    return list(outs)


def _remote(src, dst, sems, k, to):
    send_sems, recv_sems = sems
    return pltpu.make_async_remote_copy(src_ref=src, dst_ref=dst, send_sem=send_sems.at[k], recv_sem=recv_sems.at[k],
                                        device_id=to, device_id_type=MESH)


def _chip_at(x, y, rel):
    px = 1 - x if rel & 2 else x
    py = 1 - y if rel & 1 else y
    return px, py, 2 * px + py


def _gather_in_copies(rels):
    def copies(refs, send_sems, recv_sems):
        (w,) = refs
        x, y, c = _coords()
        seg = w.shape[2] // 2
        mine = w.at[2 * x + y, :, pl.ds(c * seg, seg)]
        return [_remote(mine, mine, (send_sems, recv_sems), k, _chip_at(x, y, rel)[:2] + (c,))
                for k, rel in enumerate(rels)]
    return copies


def _gather_out_copies(refs, send_sems, recv_sems):
    (w,) = refs
    x, y, c = _coords()
    mine = w.at[2 * x + y]
    return [_remote(mine, mine, (send_sems, recv_sems), k, (px, py, c)) for k, (px, py) in enumerate(_other_chips(x, y))]


def _swap_copies(refs, send_sems, recv_sems):
    gi, go, si, so = refs
    x, y, c = _coords()
    cps = []
    for a, (src, dst) in enumerate(((gi, si), (go, so))):
        nr = dst.shape[1]
        cps.append(_remote(src.at[:, pl.ds((1 - c) * nr, nr), :], dst, (send_sems, recv_sems), a, (x, y, 1 - c)))
    return cps


def _scatter_copies(refs, send_sems, recv_sems):
    qi, qo, ri, ro = refs
    x, y, c = _coords()
    cps = []
    for k, (px, py) in enumerate(_other_chips(x, y)):
        for a, (src, dst) in enumerate(((qi, ri), (qo, ro))):
            cps.append(_remote(src.at[2 * px + py], dst.at[k], (send_sems, recv_sems), 2 * k + a, (px, py, c)))
    return cps


def _forward_copies(rels):
    def copies(refs, send_sems, recv_sems):
        (w,) = refs
        x, y, c = _coords()
        seg = w.shape[2] // 2
        cps = []
        for k, rel in enumerate(rels):
            got = w.at[_chip_at(x, y, rel)[2], :, pl.ds(c * seg, seg)]
            cps.append(_remote(got, got, (send_sems, recv_sems), k, (x, y, 1 - c)))
        return cps
    return copies


def _chip_sum(own, got, where, name):
    N, C = own.shape
    tr = min(256, N)
    nt = N // tr

    def body(where_ref, own_ref, got_ref, o_ref):
        t = own_ref[...]
        for k in range(3):
            t = t + got_ref[k].astype(F32)
        o_ref[...] = t

    grid_spec = pltpu.PrefetchScalarGridSpec(
        num_scalar_prefetch=1, grid=(nt,),
        in_specs=[pl.BlockSpec((tr, C), lambda i, w: (i, 0)), pl.BlockSpec((3, tr, C), lambda i, w: (0, i, 0))],
        out_specs=pl.BlockSpec((tr, C), lambda i, w: (w[0] * nt + i, 0)))
    return pl.pallas_call(
        body, grid_spec=grid_spec, name=name, out_shape=jax.ShapeDtypeStruct((2 * N, C), F32),
        compiler_params=_cp(("parallel",)),
    )(where, own, got)


def _join_copies(refs, send_sems, recv_sems):
    x, y, c = _coords()
    cps = []
    for a, ref in enumerate(refs):
        nr = ref.shape[0] // 2
        mine = ref.at[pl.ds(c * nr, nr), :]
        cps.append(_remote(mine, mine, (send_sems, recv_sems), a, (x, y, 1 - c)))
    return cps


def _all_reduce_small(part, token):
    R, C = part.shape

    def body(p_ref, _, o_ref, slots, send_sems, recv_sems):
        x, y, c = _coords()
        me = 4 * x + 2 * y + c
        slots[me] = p_ref[...]
        cps = []
        for k in range(1, 8):
            fx, fy, fc = (k >> 2) & 1, (k >> 1) & 1, k & 1
            peer = (1 - x if fx else x, 1 - y if fy else y, 1 - c if fc else c)
            cp = pltpu.make_async_remote_copy(src_ref=p_ref, dst_ref=slots.at[me], send_sem=send_sems.at[k - 1],
                                              recv_sem=recv_sems.at[k - 1], device_id=peer, device_id_type=MESH)
            cp.start()
            cps.append(cp)
        for cp in cps:
            cp.wait()
        t = slots[0]
        for k in range(1, 8):
            t = t + slots[k]
        o_ref[...] = t

    vm = pl.BlockSpec(memory_space=pltpu.VMEM)
    return pl.pallas_call(
        body, name="all_reduce_small", in_specs=[vm, vm], out_specs=vm,
        out_shape=jax.ShapeDtypeStruct((R, C), F32),
        scratch_shapes=[pltpu.VMEM((8, R, C), F32), pltpu.SemaphoreType.DMA((7,)), pltpu.SemaphoreType.DMA((7,))],
    )(part, token)


def _mixers_forward(z, lb_logits, hgrn_gnorm):
    slopes = _alibi_slopes(z[0].shape[1] * LANES)
    yh, states = _hgrn_fwd(z[0], lb_logits, hgrn_gnorm)
    outs, lses = [], []
    for d in DILATIONS:
        o, l = _attn_fwd(z, slopes, d)
        outs.append(o)
        lses.append(l)
    o_attn, lse, ya = _attn_merge(outs, lses, z[0])
    return yh, ya, (states, o_attn, lse, slopes)


def _backward_to_dz(z, kept, lb_logits, hgrn_gnorm, yh, ya, w_out_all, x2, tgt, fgain, h):
    states, o_attn, lse, slopes = kept
    dout, doutb, loss, dfg = _out_proj_loss(yh, ya, w_out_all, x2, tgt, fgain)
    dy = _dy_proj(doutb, w_out_all)
    g_w_out = _grad_w_out(yh, ya, doutb)
    dzh, dlogits, dgn = _hgrn_bwd(z[0], lb_logits, hgrn_gnorm, states, dy)
    do, dl, dza = _attn_gate_bwd(dy, o_attn, z[0])
    acc = None
    order = sorted(DILATIONS, reverse=True)
    for d in order[:-1]:
        acc = _attn_bwd(z, slopes, do, lse, dl, d, acc, None)
    dza = _attn_bwd(z, slopes, do, lse, dl, order[-1], acc, dza)
    sources = [dzh, dza]
    g_w_in = _grad_w_in(h, sources)
    return loss, dfg, dlogits, dgn, g_w_out, g_w_in, sources, dout


def _grad_x_half(sources, w_all, x2, rinv, norm_gain, dout, token, part, gx_prev):
    dh = _dh_proj(sources, w_all, token, part, f"dh_proj_{part}")
    return _rms_bwd(dh, x2, rinv, norm_gain, dout, part, gx_prev, f"rms_bwd_{part}")


def _local_step(x2, tgt, norm_gain, w_all, lb_logits, hgrn_gnorm, w_out_all, fgain):
    token = jnp.zeros((8, LANES), F32)
    where = jnp.zeros((2,), jnp.int32)
    h, rinv = _rms_fwd(x2, norm_gain, token)
    z = _in_proj(h, w_all, where, [(rel, half) for rel in range(4) for half in range(2)], None, token, "in_proj_all")
    yh, ya, kept = _mixers_forward(z, lb_logits, hgrn_gnorm)
    loss, dfg, dlogits, dgn, g_w_out, g_w_in, sources, dout = _backward_to_dz(
        z, kept, lb_logits, hgrn_gnorm, yh, ya, w_out_all, x2, tgt, fgain, h)
    gx, dg0 = _grad_x_half(sources, w_all, x2, rinv, norm_gain, dout, token, 0, None)
    gx, dg1 = _grad_x_half(sources, w_all, x2, rinv, norm_gain, dout, token, 1, gx)
    return loss, gx, dg0 + dg1, g_w_in, dlogits, dgn, g_w_out, dfg


def _pack_small(D, loss, dgain, dlogits, dgn, dfg):
    def row(v):
        v = v.reshape(1, -1)
        return jnp.pad(v, ((0, 0), (0, D - v.shape[1])))
    rows = [row(dgain), row(dfg), row(dlogits[0]), row(dlogits[1]), row(jnp.sum(dgn, axis=0)), row(loss)]
    rows += [jnp.zeros((1, D), F32)] * (8 - len(rows))
    return jnp.concatenate(rows, axis=0)


def kernel(x, norm_gain, w_in, lb_logits, hgrn_gnorm, w_out, final_gain, loss_target, m_norm_gain, m_w_in, m_lb_logits, m_hgrn_gnorm, m_w_out, m_final_gain, v_norm_gain, v_w_in, v_lb_logits, v_hgrn_gnorm, v_w_out, v_final_gain):
    _, S, D = x.shape
    SEG = w_in.shape[2] // 2
    x2 = x[0]
    tgt = loss_target[0]
    fgain = final_gain.reshape(1, D)
    where = jnp.stack([lax.axis_index("c"), 2 * lax.axis_index("x") + lax.axis_index("y")]).astype(jnp.int32)

    wia = _cast_into_slot(w_in[0], where, "cast_w_in")
    woa = _cast_into_slot(w_out[0], where, "cast_w_out")
    near, far = (2, 1), (3,)
    ga = _split_start("gather_near_start", _gather_in_copies(near), [wia], 2)
    h, rinv = _rms_fwd(x2, norm_gain, ga[3])
    z = _in_proj(h, ga[2][0], where, [(0, 0), (0, 1)], None, ga[3], "in_proj_own")
    (wia,) = _split_wait("gather_near_wait", _gather_in_copies(near), ga[0], ga[1], ga[2], z[0])
    gb = _split_start("gather_far_start", _gather_in_copies(far), [wia], 1)
    fa = _split_start("forward_near_start", _forward_copies(near), gb[2], 2, after=gb[3])
    z = _in_proj(h, fa[2][0], where, [(2, "mine"), (1, "mine")], z, fa[3], "in_proj_near")
    (wia,) = _split_wait("forward_near_wait", _forward_copies(near), fa[0], fa[1], fa[2], z[0])
    (wia,) = _split_wait("gather_far_wait", _gather_in_copies(far), gb[0], gb[1], [wia], z[0])
    out_sems = _split_start("gather_out_start", _gather_out_copies, [woa], 3, after=wia)
    fb = _split_start("forward_far_start", _forward_copies(far), [wia], 1, after=out_sems[3])
    z = _in_proj(h, fb[2][0], where, [(3, "mine"), (2, "sibling"), (1, "sibling")], z, fb[3], "in_proj_far")
    (wia,) = _split_wait("forward_far_wait", _forward_copies(far), fb[0], fb[1], fb[2], z[0])
    z = _in_proj(h, wia, where, [(3, "sibling")], z, fb[3], "in_proj_last")
    yh, ya, kept = _mixers_forward(z, lb_logits, hgrn_gnorm)
    (woa,) = _split_wait("gather_out_wait", _gather_out_copies, out_sems[0], out_sems[1], out_sems[2], ya)
    w_out_all = woa.reshape(2 * SEG, D)

    loss, dfg, dlogits, dgn, g_w_out, g_w_in, sources, dout = _backward_to_dz(
        z, kept, lb_logits, hgrn_gnorm, yh, ya, w_out_all, x2, tgt, fgain, h)

    sib_i = lax.empty((4, g_w_in.shape[1] // 2, g_w_in.shape[2]), F32)
    sib_o = lax.empty((4, g_w_out.shape[1] // 2, g_w_out.shape[2]), F32)
    sems = _split_start("swap_start", _swap_copies, [g_w_in, g_w_out, sib_i, sib_o], 2)
    grad_x, dg0 = _grad_x_half(sources, wia, x2, rinv, norm_gain, dout, sems[3], 0, None)
    g_w_in, g_w_out, sib_i, sib_o = _split_wait("swap_wait", _swap_copies, sems[0], sems[1], sems[2], grad_x)
    qi, own_i = _pair_sum(g_w_in, sib_i, where, "pair_sum_w_in")
    qo, own_o = _pair_sum(g_w_out, sib_o, where, "pair_sum_w_out")
    ri = lax.empty((3,) + qi.shape[1:], BF16)
    ro = lax.empty((3,) + qo.shape[1:], BF16)
    sems = _split_start("scatter_start", _scatter_copies, [qi, qo, ri, ro], 6)
    grad_x, dg1 = _grad_x_half(sources, wia, x2, rinv, norm_gain, dout, sems[3], 1, grad_x)
    _, _, got_i, got_o = _split_wait("scatter_wait", _scatter_copies, sems[0], sems[1], sems[2], grad_x)
    jn = _split_start("join_start", _join_copies, [_chip_sum(own_i, got_i, where, "chip_sum_w_in"),
                                                   _chip_sum(own_o, got_o, where, "chip_sum_w_out")], 2)
    small = _all_reduce_small(_pack_small(D, loss, dg0 + dg1, dlogits, dgn, dfg), jn[3])
    loss_sum = small[5, 0]
    d_ng, m_ng, v_ng, grad_norm_gain = _adamw(norm_gain, small[0:1, :], m_norm_gain, v_norm_gain, "adamw_norm_gain")
    d_lb, m_lb, v_lb, grad_lb_logits = _adamw(lb_logits, small[2:4, :SEG], m_lb_logits, v_lb_logits, "adamw_lb_logits")
    d_gn, m_gn, v_gn, grad_hgrn_gnorm = _adamw(hgrn_gnorm, small[4:5, :HGRN_HEAD], m_hgrn_gnorm, v_hgrn_gnorm,
                                               "adamw_hgrn_gnorm")
    d_fg, m_fg, v_fg, grad_final_gain = _adamw(fgain, small[1:2, :], m_final_gain.reshape(1, D),
                                               v_final_gain.reshape(1, D), "adamw_final_gain")
    g_w_in, g_w_out = _split_wait("join_wait", _join_copies, jn[0], jn[1], jn[2], d_fg)
    d_wi, m_wi, v_wi, grad_w_in = _adamw(w_in[0], g_w_in, m_w_in[0], v_w_in[0], "adamw_w_in")
    d_wo, m_wo, v_wo, grad_w_out = _adamw(w_out[0], g_w_out, m_w_out[0], v_w_out[0], "adamw_w_out")

    return (loss_sum, grad_x[None],
            grad_norm_gain, grad_w_in[None], grad_lb_logits, grad_hgrn_gnorm, grad_w_out[None], grad_final_gain[0],
            d_ng, d_wi[None], d_lb, d_gn, d_wo[None], d_fg[0],
            m_ng, m_wi[None], m_lb, m_gn, m_wo[None], m_fg[0],
            v_ng, v_wi[None], v_lb, v_gn, v_wo[None], v_fg[0])
```

```python
import jax
import jax.numpy as jnp
import numpy as np
from jax import lax
from jax.experimental import pallas as pl
from jax.experimental.pallas import tpu as pltpu

F32 = jnp.float32
BF16 = jnp.bfloat16
MESH = pl.DeviceIdType.MESH

NORM_EPS = 1e-6
HGRN_HEAD = 128
HGRN_CHUNK = 64
HGRN_TILE = 128
HGRN_BLOCK = 512
HGRN_HEADS_PER_STEP = 4
ATTN_HEAD = 64
LANES = 128
BAND = 128
DILATIONS = (1, 4, 16)
DEINTERLEAVE = 16
ATTN_SCALE = ATTN_HEAD ** -0.5
assert ATTN_SCALE == 0.125
ATTN_BLOCK_ELEMS = BAND * 2048
ATTN_UNROLL = 4
SEG_QKV = 4
SEG_GATE_A = 7
NEG = -1e30

ADAM_LR = 0.001
ADAM_B1 = 0.9
ADAM_B2 = 0.999
ADAM_EPS = 1e-08
ADAM_WD = 0.01
ADAM_STEP = 10

MIB = 1024 * 1024


def _cp(semantics=None, vmem_mib=48):
    return pltpu.CompilerParams(dimension_semantics=semantics, vmem_limit_bytes=vmem_mib * MIB)


def _dot(a, b):
    return jnp.dot(a, b, preferred_element_type=F32)


def _dot_nt(a, b):
    return lax.dot_general(a, b, (((1,), (1,)), ((), ())), preferred_element_type=F32)


def _dot_tn(a, b):
    return lax.dot_general(a, b, (((0,), (0,)), ((), ())), preferred_element_type=F32)


def _split3(x):
    hi = x.astype(BF16)
    r1 = x - hi.astype(F32)
    mid = r1.astype(BF16)
    lo = (r1 - mid.astype(F32)).astype(BF16)
    return hi, mid, lo


def _exact_dot(t_bf16, x):
    hi, mid, lo = _split3(x)
    return _dot(t_bf16, hi) + _dot(t_bf16, mid) + _dot(t_bf16, lo)


def _exact_dot_right(x, t_bf16):
    hi, mid, lo = _split3(x)
    return _dot(hi, t_bf16) + _dot(mid, t_bf16) + _dot(lo, t_bf16)


def _sigmoid(z):
    return jax.nn.sigmoid(z)


def _silu_and_grad(z):
    s = _sigmoid(z)
    return z * s, s * (1.0 + z * (1.0 - s))


def _seg_select(j, values):
    out = values[0]
    for t, v in enumerate(values[1:], 1):
        out = jnp.where(j == t, v, out)
    return out


def _rms_fwd(x2, gain, token):
    S, D = x2.shape
    tm = min(512, S)

    def body(x_ref, g_ref, _, h_ref, r_ref):
        x = x_ref[...]
        r = lax.rsqrt(jnp.mean(x * x, axis=-1, keepdims=True) + NORM_EPS)
        h_ref[...] = ((x * r) * g_ref[...]).astype(BF16)
        r_ref[...] = r

    return pl.pallas_call(
        body, grid=(S // tm,), name="rms_fwd",
        in_specs=[pl.BlockSpec((tm, D), lambda i: (i, 0)), pl.BlockSpec((1, D), lambda i: (0, 0)),
                  pl.BlockSpec(token.shape, lambda i: (0, 0))],
        out_specs=[pl.BlockSpec((tm, D), lambda i: (i, 0)), pl.BlockSpec((tm, 1), lambda i: (i, 0))],
        out_shape=[jax.ShapeDtypeStruct((S, D), BF16), jax.ShapeDtypeStruct((S, 1), F32)],
        compiler_params=_cp(("parallel",)),
    )(x2, gain, token)


def _in_proj(h, w_all, where, segs, z_prev, token, name):
    S, D = h.shape
    SEG = w_all.shape[2] // 2
    NLB = SEG // LANES
    tm = min(512, S)
    count = len(segs)
    DI = DEINTERLEAVE
    tu = tm // DI

    def is_qkv(seg):
        return (seg >= SEG_QKV) & (seg < SEG_QKV + 3)

    def seg_of(j, w):
        halves = {0: 0, 1: 1, "mine": w[0], "sibling": 1 - w[0]}
        cands = [2 * jnp.bitwise_xor(w[1], rel) + halves[half] for rel, half in segs]
        keys = [is_qkv(s).astype(jnp.int32) for s in cands]
        out = cands[0]
        for k in range(count):
            pos = (sum(jnp.where(keys[t] < keys[k], 1, 0) for t in range(count))
                   + sum(jnp.where(keys[t] == keys[k], 1, 0) for t in range(k)))
            out = jnp.where(pos == j, cands[k], out)
        return out

    def body(*refs):
        where_ref, h_ref, w_ref = refs[:3]
        o_ref, o16_ref = refs[-2:]
        res = _dot(h_ref[...], w_ref[...])
        for p in range(NLB):
            o_ref[p] = res[:, p * LANES:(p + 1) * LANES]

        @pl.when(is_qkv(seg_of(pl.program_id(0), where_ref)))
        def _():
            for p in range(NLB):
                for r in range(DI):
                    o16_ref[p, r] = o_ref.at[p][pl.ds(r, tu, stride=DI), :]

    def z16_map(j, i, w):
        seg = seg_of(j, w)
        return (jnp.where(is_qkv(seg), seg - SEG_QKV, 3), 0, 0, jnp.where(is_qkv(seg), i, 0), 0)

    in_specs = [pl.BlockSpec((tm, D), lambda j, i, w: (i, 0)),
                pl.BlockSpec((None, D, SEG), lambda j, i, w: (seg_of(j, w) // 2, 0, seg_of(j, w) % 2)),
                pl.BlockSpec(token.shape, lambda j, i, w: (0, 0))]
    args = [where, h, w_all, token]
    aliases = {}
    if z_prev is not None:
        in_specs += [ANY, ANY]
        args += list(z_prev)
        aliases = {4: 0, 5: 1}
    grid_spec = pltpu.PrefetchScalarGridSpec(
        num_scalar_prefetch=1, grid=(count, S // tm), in_specs=in_specs,
        out_specs=[pl.BlockSpec((None, NLB, tm, LANES), lambda j, i, w: (seg_of(j, w), 0, i, 0)),
                   pl.BlockSpec((None, NLB, DI, tu, LANES), z16_map)])
    return pl.pallas_call(
        body, grid_spec=grid_spec, name=name,
        out_shape=[jax.ShapeDtypeStruct((8, NLB, S, LANES), F32),
                   jax.ShapeDtypeStruct((4, NLB, DI, S // DI, LANES), F32)],
        input_output_aliases=aliases, compiler_params=_cp(("parallel", "parallel")),
    )(*args)


def _out_proj_loss(yh, ya, w_out, x2, tgt, fgain):
    S, D = x2.shape
    SEG = yh.shape[1]
    tm = min(256, S)
    parts = 2

    def body(yh_ref, ya_ref, w_ref, x_ref, t_ref, fg_ref, dout_ref, doutb_ref, loss_ref, dfg_ref):
        i = pl.program_id(0)

        @pl.when(i == 0)
        def _():
            loss_ref[...] = jnp.zeros_like(loss_ref)
            dfg_ref[...] = jnp.zeros_like(dfg_ref)

        fg = fg_ref[...]
        loss = jnp.zeros((1, 1), F32)
        dfg = jnp.zeros((1, D), F32)
        for rows in [pl.ds(p * (tm // parts), tm // parts) for p in range(parts)]:
            out = (x_ref[rows, :] + _dot(yh_ref[rows, :], w_ref[pl.ds(0, SEG), :])
                   + _dot(ya_ref[rows, :], w_ref[pl.ds(SEG, SEG), :]))
            r = lax.rsqrt(jnp.mean(out * out, axis=-1, keepdims=True) + NORM_EPS)
            n = out * r
            err = n * fg - t_ref[rows, :]
            loss = loss + 0.5 * jnp.sum(jnp.mean(err * err, axis=-1, keepdims=True), axis=0, keepdims=True)
            dy = err * (1.0 / D)
            dfg = dfg + jnp.sum(dy * n, axis=0, keepdims=True)
            dn = dy * fg
            dout = r * (dn - n * jnp.mean(dn * n, axis=-1, keepdims=True))
            dout_ref[rows, :] = dout
            doutb_ref[rows, :] = dout.astype(BF16)
        loss_ref[...] += loss
        dfg_ref[...] += dfg

    row = lambda i: (i, 0)
    fix = lambda i: (0, 0)
    return pl.pallas_call(
        body, grid=(S // tm,), name="out_proj_loss",
        in_specs=[pl.BlockSpec((tm, SEG), row), pl.BlockSpec((tm, SEG), row), pl.BlockSpec((2 * SEG, D), fix),
                  pl.BlockSpec((tm, D), row), pl.BlockSpec((tm, D), row), pl.BlockSpec((1, D), fix)],
        out_specs=[pl.BlockSpec((tm, D), row), pl.BlockSpec((tm, D), row), pl.BlockSpec((1, 1), fix),
                   pl.BlockSpec((1, D), fix)],
        out_shape=[jax.ShapeDtypeStruct((S, D), F32), jax.ShapeDtypeStruct((S, D), BF16),
                   jax.ShapeDtypeStruct((1, 1), F32), jax.ShapeDtypeStruct((1, D), F32)],
        compiler_params=_cp(("arbitrary",)),
    )(yh, ya, w_out, x2, tgt, fgain)


def _dy_proj(doutb, w_out):
    S, D = doutb.shape
    K = w_out.shape[0]
    tm = min(512, S)

    def body(d_ref, w_ref, o_ref):
        o_ref[...] = _dot_nt(d_ref[...], w_ref[...])

    return pl.pallas_call(
        body, grid=(S // tm,), name="dy_proj",
        in_specs=[pl.BlockSpec((tm, D), lambda i: (i, 0)), pl.BlockSpec((K, D), lambda i: (0, 0))],
        out_specs=pl.BlockSpec((tm, K), lambda i: (i, 0)),
        out_shape=jax.ShapeDtypeStruct((S, K), F32),
        compiler_params=_cp(("parallel",)),
    )(doutb, w_out)


def _grad_w_out(yh, ya, doutb):
    S, SEG = yh.shape
    D = doutb.shape[1]
    R = (2 * SEG) // 4
    nb_half = SEG // R
    tk = min(1024, S)

    def body(yh_ref, ya_ref, d_ref, o_ref):
        q = pl.program_id(0)
        k = pl.program_id(1)

        @pl.when(k == 0)
        def _():
            o_ref[...] = jnp.zeros_like(o_ref)

        @pl.when(q < nb_half)
        def _():
            o_ref[...] += _dot_tn(yh_ref[...], d_ref[...])

        @pl.when(q >= nb_half)
        def _():
            o_ref[...] += _dot_tn(ya_ref[...], d_ref[...])

    return pl.pallas_call(
        body, grid=(4, S // tk), name="grad_w_out",
        in_specs=[pl.BlockSpec((tk, R), lambda q, k: (k, jnp.minimum(q, nb_half - 1))),
                  pl.BlockSpec((tk, R), lambda q, k: (k, jnp.maximum(q - nb_half, 0))),
                  pl.BlockSpec((tk, D), lambda q, k: (k, 0))],
        out_specs=pl.BlockSpec((None, R, D), lambda q, k: (q, 0, 0)),
        out_shape=jax.ShapeDtypeStruct((4, R, D), F32),
        compiler_params=_cp(("parallel", "arbitrary")),
    )(yh, ya, doutb)


def _dz_sources(sources):
    counts = [s.shape[0] for s in sources]
    starts = [sum(counts[:k]) for k in range(len(counts))]
    assert sum(counts) == 8
    return counts, starts


def _row_part(S, part, tile):
    first = max(512, (S * 3 // 8) // 512 * 512)
    rows = first if part == 0 else S - first
    assert rows % tile == 0 and first % tile == 0
    return (0 if part == 0 else first // tile), rows // tile, rows


def _dh_proj(sources, w_all, token, part, name):
    S = sources[0].shape[1]
    D = w_all.shape[1]
    SEG = w_all.shape[2] // 2
    counts, starts = _dz_sources(sources)
    assert all(c % 2 == 0 for c in counts)
    ns = len(sources)
    tm = 1024 if all(_row_part(S, p, 1)[2] % 1024 == 0 for p in (0, 1)) else 512
    t0, nt, nrows = _row_part(S, part, tm)

    def body(*refs):
        src = refs[:ns]
        w_ref, _, o_ref = refs[ns:]
        j = pl.program_id(1)

        @pl.when(j == 0)
        def _():
            o_ref[...] = jnp.zeros_like(o_ref)

        for k in range(ns):
            @pl.when((2 * j >= starts[k]) & (2 * j < starts[k] + counts[k]))
            def _(k=k):
                o_ref[...] += (_dot_nt(src[k][0], w_ref[:, pl.ds(0, SEG)])
                               + _dot_nt(src[k][1], w_ref[:, pl.ds(SEG, SEG)]))

    def src_spec(k):
        return pl.BlockSpec((2, tm, SEG),
                            lambda i, j: (jnp.clip(j - starts[k] // 2, 0, counts[k] // 2 - 1), t0 + i, 0))

    return pl.pallas_call(
        body, grid=(nt, 4), name=name,
        in_specs=[src_spec(k) for k in range(ns)] + [pl.BlockSpec((None, D, 2 * SEG), lambda i, j: (j, 0, 0)),
                                                     pl.BlockSpec(token.shape, lambda i, j: (0, 0))],
        out_specs=pl.BlockSpec((tm, D), lambda i, j: (i, 0)),
        out_shape=jax.ShapeDtypeStruct((nrows, D), F32),
        compiler_params=_cp(("parallel", "arbitrary"), 48 if tm == 512 else 60),
    )(*sources, w_all, token)


def _rms_bwd(dh, x2, rinv, gain, dout, part, gx_prev, name):
    S, D = x2.shape
    tm = 256
    t0, nt, _ = _row_part(S, part, tm)

    def body(dh_ref, x_ref, r_ref, g_ref, dout_ref, *rest):
        gx_ref, dg_ref = rest[-2:]

        @pl.when(pl.program_id(0) == 0)
        def _():
            dg_ref[...] = jnp.zeros_like(dg_ref)

        dh = dh_ref[...]
        r = r_ref[...]
        xhat = x_ref[...] * r
        dg_ref[...] += jnp.sum(dh * xhat, axis=0, keepdims=True)
        dxn = dh * g_ref[...]
        gx_ref[...] = dout_ref[...] + r * (dxn - xhat * jnp.mean(dxn * xhat, axis=-1, keepdims=True))

    row = lambda i: (t0 + i, 0)
    fix = lambda i: (0, 0)
    in_specs = [pl.BlockSpec((tm, D), lambda i: (i, 0)), pl.BlockSpec((tm, D), row), pl.BlockSpec((tm, 1), row),
                pl.BlockSpec((1, D), fix), pl.BlockSpec((tm, D), row)]
    args = [dh, x2, rinv, gain, dout]
    aliases = {}
    if gx_prev is not None:
        in_specs.append(ANY)
        args.append(gx_prev)
        aliases = {5: 0}
    return pl.pallas_call(
        body, grid=(nt,), name=name, in_specs=in_specs,
        out_specs=[pl.BlockSpec((tm, D), row), pl.BlockSpec((1, D), fix)],
        out_shape=[jax.ShapeDtypeStruct((S, D), F32), jax.ShapeDtypeStruct((1, D), F32)],
        input_output_aliases=aliases, compiler_params=_cp(("arbitrary",)),
    )(*args)


def _grad_w_in(h, sources):
    S, D = h.shape
    SEG = sources[0].shape[2]
    counts, starts = _dz_sources(sources)
    ns = len(sources)
    tk = min(1024, S)

    def body(*refs):
        h_ref = refs[0]
        src = refs[1:1 + ns]
        o_ref = refs[1 + ns]
        j = pl.program_id(0)
        k = pl.program_id(1)

        @pl.when(k == 0)
        def _():
            o_ref[...] = jnp.zeros_like(o_ref)

        for s in range(ns):
            @pl.when((j >= starts[s]) & (j < starts[s] + counts[s]))
            def _(s=s):
                o_ref[...] += _dot_tn(h_ref[...], src[s][...])

    def src_spec(s):
        return pl.BlockSpec((None, tk, SEG),
                            lambda j, k: (jnp.clip(j - starts[s], 0, counts[s] - 1), k, 0))

    return pl.pallas_call(
        body, grid=(8, S // tk), name="grad_w_in",
        in_specs=[pl.BlockSpec((tk, D), lambda j, k: (k, 0))] + [src_spec(s) for s in range(ns)],
        out_specs=pl.BlockSpec((None, D, SEG), lambda j, k: (j // 2, 0, j % 2)),
        out_shape=jax.ShapeDtypeStruct((4, D, 2 * SEG), F32),
        compiler_params=_cp(("parallel", "arbitrary")),
    )(h, *sources)


def _lower_bound(lbl):
    l0 = lbl[0:1, :]
    l1 = lbl[1:2, :]
    m = jnp.maximum(l0, l1)
    e0 = jnp.exp(l0 - m)
    e1 = jnp.exp(l1 - m)
    return e0 / (e0 + e1)


def _tile_masks():
    row = lax.broadcasted_iota(jnp.int32, (HGRN_TILE, HGRN_TILE), 0)
    col = lax.broadcasted_iota(jnp.int32, (HGRN_TILE, HGRN_TILE), 1)
    same = (row // HGRN_CHUNK) == (col // HGRN_CHUNK)
    return same & (row >= col), same & (row <= col)


def _chunk_last(b):
    T = b.shape[0]
    b3 = b.reshape(T // HGRN_CHUNK, HGRN_CHUNK, HGRN_HEAD)
    return jnp.broadcast_to(b3[:, HGRN_CHUNK - 1:HGRN_CHUNK, :], b3.shape).reshape(T, HGRN_HEAD)


def _chunk_sum(x):
    T = x.shape[0]
    x3 = x.reshape(T // HGRN_CHUNK, HGRN_CHUNK, HGRN_HEAD)
    return jnp.broadcast_to(jnp.sum(x3, axis=1, keepdims=True), x3.shape).reshape(T, HGRN_HEAD)


def _hgrn_dims(S, SEG):
    T = min(HGRN_BLOCK, S)
    assert S % T == 0 and T % HGRN_TILE == 0
    tiles = [slice(t * HGRN_TILE, (t + 1) * HGRN_TILE) for t in range(T // HGRN_TILE)]
    chunks = [slice(c * HGRN_CHUNK, (c + 1) * HGRN_CHUNK) for c in range(T // HGRN_CHUNK)]
    return SEG // HGRN_HEAD, T, T // HGRN_CHUNK, S // T, tiles, chunks


def _hgrn_fwd(zf32, lb_logits, gnorm):
    _, NLB, S, _ = zf32.shape
    SEG = NLB * LANES
    H, T, NC, NJ, tiles, chunks = _hgrn_dims(S, SEG)
    HP = min(HGRN_HEADS_PER_STEP, H)
    assert H % HP == 0

    def body(zq_ref, zf_ref, zi_ref, zg_ref, lbl_ref, gn_ref, y_ref, st_ref, state):
        @pl.when(pl.program_id(1) == 0)
        def _():
            state[...] = jnp.zeros_like(state)

        tril, _ = _tile_masks()
        tril_bf = tril.astype(BF16)
        for hh in range(HP):
            cols = slice(hh * HGRN_HEAD, (hh + 1) * HGRN_HEAD)
            lb = _lower_bound(lbl_ref[:, cols])
            zq = zq_ref[hh]
            q = zq * _sigmoid(zq)
            f = lb + (1.0 - lb) * _sigmoid(zf_ref[hh])
            k = 1.0 - f
            logf = jnp.log(f)
            b = jnp.concatenate([_exact_dot(tril_bf, logf[t]) for t in tiles], axis=0)
            bl = _chunk_last(b)
            qd_b = (q * jnp.exp(b)).astype(BF16)
            kd_b = (k * jnp.exp(-b)).astype(BF16)
            ke_b = (k * jnp.exp(bl - b)).astype(BF16)
            v_b = zi_ref[hh].astype(BF16)
            o_intra = jnp.concatenate(
                [_dot(jnp.where(tril, _dot_nt(qd_b[t], kd_b[t]), 0.0).astype(BF16), v_b[t]) for t in tiles], axis=0)
            kvs = [_dot_tn(v_b[r], ke_b[r]) for r in chunks]
            ebl = jnp.exp(bl)
            st = state[hh]
            sts = []
            for c in range(NC):
                st_ref[c, hh] = st
                sts.append(st.astype(BF16))
                st = st * ebl[c * HGRN_CHUNK:c * HGRN_CHUNK + 1, :] + kvs[c]
            state[hh] = st
            o = o_intra + jnp.concatenate([_dot_nt(qd_b[r], sb) for r, sb in zip(chunks, sts)], axis=0)
            on = o * lax.rsqrt(jnp.mean(o * o, axis=-1, keepdims=True) + NORM_EPS) * gn_ref[...]
            zg = zg_ref[hh]
            y_ref[:, cols] = (on * (zg * _sigmoid(zg))).astype(BF16)

    def zspec(seg):
        return pl.BlockSpec((None, HP, T, HGRN_HEAD), lambda h, j: (seg, h, j, 0))

    return pl.pallas_call(
        body, grid=(H // HP, NJ), name="hgrn_fwd",
        in_specs=[zspec(0), zspec(1), zspec(2), zspec(3),
                  pl.BlockSpec((2, HP * HGRN_HEAD), lambda h, j: (0, h)),
                  pl.BlockSpec((1, HGRN_HEAD), lambda h, j: (0, 0))],
        out_specs=[pl.BlockSpec((T, HP * HGRN_HEAD), lambda h, j: (j, h)),
                   pl.BlockSpec((NC, HP, HGRN_HEAD, HGRN_HEAD), lambda h, j: (j, h, 0, 0))],
        out_shape=[jax.ShapeDtypeStruct((S, SEG), BF16),
                   jax.ShapeDtypeStruct((S // HGRN_CHUNK, H, HGRN_HEAD, HGRN_HEAD), F32)],
        scratch_shapes=[pltpu.VMEM((HP, HGRN_HEAD, HGRN_HEAD), F32)],
        compiler_params=_cp(("parallel", "arbitrary")),
    )(zf32, zf32, zf32, zf32, lb_logits, gnorm)


def _hgrn_bwd(zf32, lb_logits, gnorm, states, dy):
    _, NLB, S, _ = zf32.shape
    SEG = NLB * LANES
    H, T, NC, NJ, tiles, chunks = _hgrn_dims(S, SEG)
    C = HGRN_CHUNK
    HP = min(HGRN_HEADS_PER_STEP, H)
    assert H % HP == 0

    def body(zq_ref, zf_ref, zi_ref, zg_ref, lbl_ref, gn_ref, st_ref, dy_ref, dz_ref, dl_ref, dgn_ref, gstate):
        @pl.when(pl.program_id(1) == 0)
        def _():
            gstate[...] = jnp.zeros_like(gstate)
            dl_ref[...] = jnp.zeros_like(dl_ref)
            dgn_ref[...] = jnp.zeros_like(dgn_ref)

        gn = gn_ref[...]
        tril, triu = _tile_masks()
        tril_bf = tril.astype(BF16)
        triu_bf = triu.astype(BF16)
        for hh in range(HP):
            cols = slice(hh * HGRN_HEAD, (hh + 1) * HGRN_HEAD)
            lb = _lower_bound(lbl_ref[:, cols])
            q, dq_dz = _silu_and_grad(zq_ref[hh])
            sf = _sigmoid(zf_ref[hh])
            f = lb + (1.0 - lb) * sf
            k = 1.0 - f
            logf = jnp.log(f)
            b = jnp.concatenate([_exact_dot(tril_bf, logf[t]) for t in tiles], axis=0)
            bl = _chunk_last(b)
            eb = jnp.exp(b)
            enb = jnp.exp(-b)
            ekl = jnp.exp(bl - b)
            ebl = jnp.exp(bl)
            qd = q * eb
            kd = k * enb
            ke = k * ekl
            qd_b = qd.astype(BF16)
            kd_b = kd.astype(BF16)
            ke_b = ke.astype(BF16)
            v_b = zi_ref[hh].astype(BF16)
            sts = [st_ref[c, hh] for c in range(NC)]
            sts_b = [s.astype(BF16) for s in sts]
            a_b = [jnp.where(tril, _dot_nt(qd_b[t], kd_b[t]), 0.0).astype(BF16) for t in tiles]
            o = (jnp.concatenate([_dot(a, v_b[t]) for a, t in zip(a_b, tiles)], axis=0)
                 + jnp.concatenate([_dot_nt(qd_b[r], sb) for r, sb in zip(chunks, sts_b)], axis=0))
            rinv = lax.rsqrt(jnp.mean(o * o, axis=-1, keepdims=True) + NORM_EPS)
            ohat = o * rinv
            sg, dsg = _silu_and_grad(zg_ref[hh])
            dyv = dy_ref[:, cols]
            don = dyv * sg
            dz_ref[3, :, cols] = (dyv * (ohat * gn) * dsg).astype(BF16)
            dgn_ref[hh] += jnp.sum(don * ohat, axis=0, keepdims=True)
            dohat = don * gn
            do = rinv * (dohat - ohat * jnp.mean(dohat * ohat, axis=-1, keepdims=True))
            do_b = do.astype(BF16)
            da_b = [jnp.where(tril, _dot_nt(do_b[t], v_b[t]), 0.0).astype(BF16) for t in tiles]
            dv_intra = jnp.concatenate([_dot_tn(a, do_b[t]) for a, t in zip(a_b, tiles)], axis=0)
            dqd_intra = jnp.concatenate([_dot(da, kd_b[t]) for da, t in zip(da_b, tiles)], axis=0)
            dkd = jnp.concatenate([_dot_tn(da, qd_b[t]) for da, t in zip(da_b, tiles)], axis=0)
            dqd_inter = jnp.concatenate([_dot(do_b[r], sb) for r, sb in zip(chunks, sts_b)], axis=0)
            gks = [_dot_tn(do_b[r], qd_b[r]) for r in chunks]
            g = gstate[hh]
            gs = [None] * NC
            for c in reversed(range(NC)):
                gs[c] = g
                g = g * ebl[c * C:c * C + 1, :] + gks[c]
            gstate[hh] = g
            gs_b = [x.astype(BF16) for x in gs]
            dv = dv_intra + jnp.concatenate([_dot_nt(ke_b[r], gb) for r, gb in zip(chunks, gs_b)], axis=0)
            dz_ref[2, :, cols] = dv.astype(BF16)
            dke = jnp.concatenate([_dot(v_b[r], gb) for r, gb in zip(chunks, gs_b)], axis=0)
            debl = jnp.concatenate(
                [jnp.broadcast_to(jnp.sum(x * s, axis=0, keepdims=True), (C, HGRN_HEAD)) for x, s in zip(gs, sts)], axis=0)
            dqd = dqd_intra + dqd_inter
            dz_ref[0, :, cols] = ((dqd * eb) * dq_dz).astype(BF16)
            t_ke = dke * ke
            db = dqd * qd - dkd * kd - t_ke
            db_last = _chunk_sum(t_ke) + debl * ebl
            dk = dkd * enb + dke * ekl
            dlogf = jnp.concatenate([_exact_dot(triu_bf, db[t]) for t in tiles], axis=0) + db_last
            df = dlogf / f - dk
            dz_ref[1, :, cols] = (df * (1.0 - lb) * (sf * (1.0 - sf))).astype(BF16)
            dlb = jnp.sum(df * (1.0 - sf), axis=0, keepdims=True)
            dl0 = dlb * lb * (1.0 - lb)
            dl_ref[0:1, cols] += dl0
            dl_ref[1:2, cols] -= dl0

    def zspec(seg):
        return pl.BlockSpec((None, HP, T, HGRN_HEAD), lambda h, j: (seg, h, NJ - 1 - j, 0))

    return pl.pallas_call(
        body, grid=(H // HP, NJ), name="hgrn_bwd",
        in_specs=[zspec(0), zspec(1), zspec(2), zspec(3),
                  pl.BlockSpec((2, HP * HGRN_HEAD), lambda h, j: (0, h)),
                  pl.BlockSpec((1, HGRN_HEAD), lambda h, j: (0, 0)),
                  pl.BlockSpec((NC, HP, HGRN_HEAD, HGRN_HEAD), lambda h, j: (NJ - 1 - j, h, 0, 0)),
                  pl.BlockSpec((T, HP * HGRN_HEAD), lambda h, j: (NJ - 1 - j, h))],
        out_specs=[pl.BlockSpec((4, T, HP * HGRN_HEAD), lambda h, j: (0, NJ - 1 - j, h)),
                   pl.BlockSpec((2, HP * HGRN_HEAD), lambda h, j: (0, h)),
                   pl.BlockSpec((HP, 1, HGRN_HEAD), lambda h, j: (h, 0, 0))],
        out_shape=[jax.ShapeDtypeStruct((4, S, SEG), BF16), jax.ShapeDtypeStruct((2, SEG), F32),
                   jax.ShapeDtypeStruct((H, 1, HGRN_HEAD), F32)],
        scratch_shapes=[pltpu.VMEM((HP, HGRN_HEAD, HGRN_HEAD), F32)],
        compiler_params=_cp(("parallel", "arbitrary")),
    )(zf32, zf32, zf32, zf32, lb_logits, gnorm, states, dy)


def _alibi_slopes(seg):
    n_heads = seg // ATTN_HEAD
    s = 2.0 ** (-8.0 * np.arange(1, n_heads + 1, dtype=np.float64) / n_heads)
    return jnp.asarray(np.repeat(s, ATTN_HEAD)[None, :], F32)


def _attn_dims(S, SEG, d):
    rb = BAND * d
    assert S % rb == 0 and SEG % LANES == 0
    npb = max(1, min(SEG // LANES, ATTN_BLOCK_ELEMS // (rb * LANES)))
    assert (SEG // LANES) % npb == 0
    return rb, npb, S // rb, (SEG // LANES) // npb


def _res_rows(r, d):
    return pl.ds(0, BAND) if d == 1 else pl.ds(r, BAND, stride=d)


def _for_residues(d, fn):
    if d == 1:
        fn(0)
    else:
        def step(r, carry):
            fn(r)
            return carry
        lax.fori_loop(0, d, step, 0, unroll=ATTN_UNROLL)


def _for_groups(d, n_pairs, fn):
    def over_pairs(r):
        for g0 in range(0, n_pairs, ATTN_UNROLL):
            fn([(r, p) for p in range(g0, min(n_pairs, g0 + ATTN_UNROLL))])

    if d == 1:
        over_pairs(0)
    elif n_pairs >= ATTN_UNROLL:
        def step(r, carry):
            over_pairs(r)
            return carry
        lax.fori_loop(0, d, step, 0)
    else:
        per_group = ATTN_UNROLL // n_pairs
        assert d % per_group == 0

        def step(g, carry):
            fn([(g * per_group + i, p) for i in range(per_group) for p in range(n_pairs)])
            return carry
        lax.fori_loop(0, d // per_group, step, 0)


def _band_terms(n, d):
    i = lax.broadcasted_iota(jnp.int32, (BAND, 2 * BAND), 0)
    jj = lax.broadcasted_iota(jnp.int32, (BAND, 2 * BAND), 1)
    delta = BAND + i - jj
    valid = (delta >= 0) & (delta <= BAND) & ((n > 0) | (jj >= BAND))
    return (-d * delta).astype(F32), valid


def _head_biases(slopes, nd, valid):
    out = []
    for s in _per_head(slopes):
        s2 = jnp.concatenate([s, s], axis=1)
        out.append(jnp.where(valid, s2 * nd, NEG))
    return jnp.concatenate(out, axis=0)


def _stack_heads(x):
    lane = lax.broadcasted_iota(jnp.int32, x.shape, 1)
    zero = jnp.zeros_like(x)
    return jnp.concatenate([jnp.where(lane < ATTN_HEAD, x, zero), jnp.where(lane < ATTN_HEAD, zero, x)], axis=0)


def _unstack_heads(x2):
    first = lax.broadcasted_iota(jnp.int32, (BAND, LANES), 1) < ATTN_HEAD
    return jnp.where(first, x2[:BAND], x2[BAND:])


def _stack_per_head(x):
    a, b = _per_head(x)
    col = jnp.concatenate([a, b], axis=0)
    return jnp.concatenate([col, col], axis=1)


def _per_head(x):
    lane = lax.broadcasted_iota(jnp.int32, x.shape, 1)
    sw = pltpu.roll(x, ATTN_HEAD, 1)
    first = lane < ATTN_HEAD
    return jnp.where(first, x, sw), jnp.where(first, sw, x)


def _qkv_source(zz, d):
    z, z16 = zz
    if d == DEINTERLEAVE:
        def take(ref, p, r):
            return ref.at[p][r]

        def spec(seg, np_, row_block):
            return pl.BlockSpec((None, np_, d, BAND, LANES), lambda c, n: (seg, c, 0, row_block(c, n), 0))
        return z, z16, take, spec

    def take(ref, p, r):
        return ref.at[p][_res_rows(r, d), :]

    def spec(seg, np_, row_block):
        return pl.BlockSpec((None, np_, BAND * d, LANES), lambda c, n: (SEG_QKV + seg, c, row_block(c, n), 0))
    return z, z, take, spec


def _attn_fwd(qkv, slopes, d):
    qkv, src, take, spec = _qkv_source(qkv, d)
    _, NLB, S, _ = qkv.shape
    rb, NP, nb, ncb = _attn_dims(S, NLB * LANES, d)

    def body(q_ref, kc_ref, vc_ref, sl_ref, o_ref, l_ref, kp_ref, vp_ref):
        n = pl.program_id(1)

        @pl.when(n == 0)
        def _():
            kp_ref[...] = jnp.zeros_like(kp_ref)
            vp_ref[...] = jnp.zeros_like(vp_ref)

        nd, valid = _band_terms(n, d)
        biases = [_head_biases(sl_ref[:, p * LANES:(p + 1) * LANES], nd, valid) for p in range(NP)]

        def group(items):
            scores, values = [], []
            for r, p in items:
                kc = jnp.concatenate([take(kp_ref, p, r), take(kc_ref, p, r)], axis=0).astype(BF16)
                values.append(jnp.concatenate([take(vp_ref, p, r), take(vc_ref, p, r)], axis=0).astype(BF16))
                scores.append(_dot_nt(_stack_heads((take(q_ref, p, r) * ATTN_SCALE).astype(BF16)), kc))
            probs = []
            for (r, p), s in zip(items, scores):
                s = s + biases[p]
                m = jnp.max(s, axis=-1, keepdims=True)
                e = jnp.exp(s - m)
                den = jnp.sum(e, axis=-1, keepdims=True)
                probs.append((e.astype(BF16), den, m + jnp.log(den)))
            for (r, p), vc, (e, den, lse) in zip(items, values, probs):
                rows = _res_rows(r, d)
                o_ref.at[p][rows, :] = _unstack_heads(_dot(e, vc) / den)
                l_ref.at[p][rows, :] = _unstack_heads(jnp.broadcast_to(lse, (2 * BAND, LANES)))

        _for_groups(d, NP, group)
        kp_ref[...] = kc_ref[...]
        vp_ref[...] = vc_ref[...]

    cur = lambda c, n: n
    out = pl.BlockSpec((NP, rb, LANES), lambda c, n: (c, n, 0))
    kv_block = spec(1, NP, cur).block_shape[1:]
    return pl.pallas_call(
        body, grid=(ncb, nb), name=f"attn_fwd_d{d}",
        in_specs=[spec(0, NP, cur), spec(1, NP, cur), spec(2, NP, cur),
                  pl.BlockSpec((1, NP * LANES), lambda c, n: (0, c))],
        out_specs=[out, out],
        out_shape=[jax.ShapeDtypeStruct((NLB, S, LANES), F32)] * 2,
        scratch_shapes=[pltpu.VMEM(kv_block, F32), pltpu.VMEM(kv_block, F32)],
        compiler_params=_cp(("parallel", "arbitrary")),
    )(src, src, src, slopes)


def _attn_merge(outs, lses, zf32):
    NLB, S, _ = outs[0].shape
    SEG = NLB * LANES
    tm = min(256, S)

    def body(o1, o2, o3, l1, l2, l3, zg_ref, o_ref, lse_ref, y_ref):
        a, b, c = l1[...], l2[...], l3[...]
        m = jnp.maximum(jnp.maximum(a, b), c)
        ea, eb, ec = jnp.exp(a - m), jnp.exp(b - m), jnp.exp(c - m)
        tot = ea + eb + ec
        o = (ea / tot) * o1[...] + (eb / tot) * o2[...] + (ec / tot) * o3[...]
        o_ref[...] = o
        lse_ref[...] = m + jnp.log(tot)
        zg = zg_ref[...]
        y = (o * (zg * _sigmoid(zg))).astype(BF16)
        for p in range(NLB):
            y_ref[:, p * LANES:(p + 1) * LANES] = y[p]

    blk = pl.BlockSpec((NLB, tm, LANES), lambda i: (0, i, 0))
    return pl.pallas_call(
        body, grid=(S // tm,), name="attn_merge",
        in_specs=[blk] * 6 + [pl.BlockSpec((None, NLB, tm, LANES), lambda i: (SEG_GATE_A, 0, i, 0))],
        out_specs=[blk, blk, pl.BlockSpec((tm, SEG), lambda i: (i, 0))],
        out_shape=[jax.ShapeDtypeStruct((NLB, S, LANES), F32), jax.ShapeDtypeStruct((NLB, S, LANES), F32),
                   jax.ShapeDtypeStruct((S, SEG), BF16)],
        compiler_params=_cp(("parallel",)),
    )(*outs, *lses, zf32)


def _attn_gate_bwd(dy, o, zf32):
    NP, S, _ = o.shape
    SEG = NP * LANES
    tm = min(256, S)

    def body(dy_ref, o_ref, zg_ref, do_ref, dl_ref, dzg_ref):
        r = lax.broadcasted_iota(jnp.int32, (LANES, LANES), 0) // ATTN_HEAD
        c = lax.broadcasted_iota(jnp.int32, (LANES, LANES), 1) // ATTN_HEAD
        same_head = (r == c).astype(BF16)
        for p in range(NP):
            cols = slice(p * LANES, (p + 1) * LANES)
            sg, dsg = _silu_and_grad(zg_ref[p])
            dyv = dy_ref[:, cols]
            ov = o_ref[p]
            do = dyv * sg
            do_ref[p] = do
            dzg_ref[:, cols] = (dyv * ov * dsg).astype(BF16)
            dl_ref[p] = _exact_dot_right(do * ov, same_head)

    blk = pl.BlockSpec((NP, tm, LANES), lambda i: (0, i, 0))
    return pl.pallas_call(
        body, grid=(S // tm,), name="attn_gate_bwd",
        in_specs=[pl.BlockSpec((tm, SEG), lambda i: (i, 1)), blk,
                  pl.BlockSpec((None, NP, tm, LANES), lambda i: (SEG_GATE_A, 0, i, 0))],
        out_specs=[blk, blk, pl.BlockSpec((None, tm, SEG), lambda i: (3, i, 0))],
        out_shape=[jax.ShapeDtypeStruct((NP, S, LANES), F32), jax.ShapeDtypeStruct((NP, S, LANES), F32),
                   jax.ShapeDtypeStruct((4, S, SEG), BF16)],
        compiler_params=_cp(("parallel",)),
    )(dy, o, zf32)


def _attn_bwd(qkv, slopes, do, lse, dl, d, acc, into):
    qkv, src, take, spec = _qkv_source(qkv, d)
    _, NLB, S, _ = qkv.shape
    SEG = NLB * LANES
    rb, NP, nb, ncb = _attn_dims(S, SEG, d)
    has_acc = acc is not None
    out_dtype = F32 if into is None else into.dtype
    assert into is None or d == 1

    def body(*refs):
        q_ref, kc_ref, vc_ref, sl_ref, do_ref, lse_ref, dl_ref = refs[:7]
        acc_ref = refs[7] if has_acc else None
        out_ref, cq, ck, cv, kp_ref, vp_ref = refs[-6:]
        n = pl.program_id(1)

        def emit(r, p, dq, dk, dv):
            rows = _res_rows(r, d)
            for t, val in enumerate((dq, dk, dv)):
                if has_acc:
                    val = val + acc_ref.at[t].at[p][rows, :]
                if into is None:
                    out_ref.at[t].at[p][rows, :] = val.astype(out_dtype)
                else:
                    out_ref.at[t][rows, p * LANES:(p + 1) * LANES] = val.astype(out_dtype)

        @pl.when(n == 0)
        def _():
            cq[...] = jnp.zeros_like(cq)
            ck[...] = jnp.zeros_like(ck)
            cv[...] = jnp.zeros_like(cv)
            kp_ref[...] = jnp.zeros_like(kp_ref)
            vp_ref[...] = jnp.zeros_like(vp_ref)

        @pl.when(n < nb)
        def _():
            nd, valid = _band_terms(n, d)
            biases = [_head_biases(sl_ref[:, p * LANES:(p + 1) * LANES], nd, valid) for p in range(NP)]

            def group(items):
                first = []
                for r, p in items:
                    rows = _res_rows(r, d)
                    kc = jnp.concatenate([take(kp_ref, p, r), take(kc_ref, p, r)], axis=0).astype(BF16)
                    vc = jnp.concatenate([take(vp_ref, p, r), take(vc_ref, p, r)], axis=0).astype(BF16)
                    qs = _stack_heads((take(q_ref, p, r) * ATTN_SCALE).astype(BF16))
                    dos = _stack_heads(do_ref.at[p][rows, :].astype(BF16))
                    first.append((kc, qs, dos, _dot_nt(qs, kc), _dot_nt(dos, vc)))
                second = []
                for (r, p), (kc, qs, dos, s, dp) in zip(items, first):
                    rows = _res_rows(r, d)
                    pr = jnp.exp(s + biases[p] - _stack_per_head(lse_ref.at[p][rows, :]))
                    ds = (pr * (dp - _stack_per_head(dl_ref.at[p][rows, :]))).astype(BF16)
                    second.append((kc, qs, dos, pr.astype(BF16), ds))
                for (r, p), (kc, qs, dos, pr, ds) in zip(items, second):
                    dq = _unstack_heads(_dot(ds, kc)) * ATTN_SCALE
                    dk = _dot_tn(ds, qs)
                    dv = _dot_tn(pr, dos)
                    emit(r, p, cq[r, p], ck[r, p] + dk[:BAND, :], cv[r, p] + dv[:BAND, :])
                    cq[r, p] = dq
                    ck[r, p] = dk[BAND:, :]
                    cv[r, p] = dv[BAND:, :]

            _for_groups(d, NP, group)
            kp_ref[...] = kc_ref[...]
            vp_ref[...] = vc_ref[...]

        @pl.when(n == nb)
        def _():
            def last(r):
                for p in range(NP):
                    emit(r, p, cq[r, p], ck[r, p], cv[r, p])
            _for_residues(d, last)

    cur = lambda c, n: (c, jnp.minimum(n, nb - 1), 0)
    lag = lambda c, n: (0, c, jnp.clip(n - 1, 0, nb - 1), 0)

    at = lambda c, n: jnp.minimum(n, nb - 1)
    in_specs = [spec(0, NP, at), spec(1, NP, at), spec(2, NP, at),
                pl.BlockSpec((1, NP * LANES), lambda c, n: (0, c))] + [pl.BlockSpec((NP, rb, LANES), cur)] * 3
    kv_block = in_specs[1].block_shape[1:]
    args = [src, src, src, slopes, do, lse, dl]
    aliases = {}
    if has_acc:
        in_specs.append(pl.BlockSpec((3, NP, rb, LANES), lag))
        args.append(acc)
        if into is None:
            aliases = {7: 0}
    if into is None:
        out_sds = jax.ShapeDtypeStruct((3, NLB, S, LANES), F32)
        out_spec = pl.BlockSpec((3, NP, rb, LANES), lag)
    else:
        in_specs.append(ANY)
        args.append(into)
        aliases = {len(args) - 1: 0}
        out_sds = jax.ShapeDtypeStruct(into.shape, into.dtype)
        out_spec = pl.BlockSpec((3, rb, NP * LANES), lambda c, n: (0, jnp.clip(n - 1, 0, nb - 1), c))
    return pl.pallas_call(
        body, grid=(ncb, nb + 1), name=f"attn_bwd_d{d}",
        in_specs=in_specs, out_specs=out_spec, out_shape=out_sds,
        scratch_shapes=[pltpu.VMEM((d, NP, BAND, LANES), F32)] * 3 + [pltpu.VMEM(kv_block, F32)] * 2,
        input_output_aliases=aliases,
        compiler_params=_cp(("parallel", "arbitrary")),
    )(*args)


def _adamw(w, g, m, v, name):
    R, C = w.shape
    tr = R if R <= 256 else 256
    assert R % tr == 0

    def body(w_ref, g_ref, m_ref, v_ref, d_ref, nm_ref, nv_ref, go_ref):
        g = g_ref[...]
        nm = ADAM_B1 * m_ref[...] + (1.0 - ADAM_B1) * g
        nv = ADAM_B2 * v_ref[...] + (1.0 - ADAM_B2) * (g * g)
        m_hat = nm / (1.0 - ADAM_B1 ** ADAM_STEP)
        v_hat = nv / (1.0 - ADAM_B2 ** ADAM_STEP)
        d_ref[...] = -ADAM_LR * (m_hat / (jnp.sqrt(v_hat) + ADAM_EPS) + ADAM_WD * w_ref[...])
        nm_ref[...] = nm
        nv_ref[...] = nv
        go_ref[...] = g

    blk = pl.BlockSpec((tr, C), lambda i: (i, 0))
    sds = jax.ShapeDtypeStruct((R, C), F32)
    return pl.pallas_call(
        body, grid=(R // tr,), name=name, in_specs=[blk] * 4, out_specs=[blk] * 4, out_shape=[sds] * 4,
        compiler_params=_cp(("parallel",)),
    )(w, g, m, v)


def _coords():
    return lax.axis_index("x"), lax.axis_index("y"), lax.axis_index("c")


def _other_chips(x, y):
    return [(1 - x, y), (x, 1 - y), (1 - x, 1 - y)]


ANY = pl.BlockSpec(memory_space=pl.ANY)


def _cast_into_slot(w, where, name):
    R, C = w.shape
    tr = min(256, R)

    def body(where_ref, w_ref, o_ref):
        o_ref[...] = w_ref[...].astype(BF16)

    grid_spec = pltpu.PrefetchScalarGridSpec(
        num_scalar_prefetch=1, grid=(R // tr,),
        in_specs=[pl.BlockSpec((tr, C), lambda i, w: (i, 0))],
        out_specs=pl.BlockSpec((None, tr, C), lambda i, w: (w[1], i, 0)))
    return pl.pallas_call(
        body, grid_spec=grid_spec, name=name, out_shape=jax.ShapeDtypeStruct((4, R, C), BF16),
        compiler_params=_cp(("parallel",)),
    )(where, w)


def _pair_sum(g, sib, where, name):
    _, n2, C = g.shape
    N = n2 // 2
    tr = min(256, N)
    nt = N // tr

    def body(where_ref, g_ref, s_ref, qb_ref, own_ref):
        q = pl.program_id(1)
        tot = g_ref[...] + s_ref[...]
        qb_ref[...] = tot.astype(BF16)

        @pl.when(q == where_ref[1])
        def _():
            own_ref[...] = tot

    grid_spec = pltpu.PrefetchScalarGridSpec(
        num_scalar_prefetch=1, grid=(nt, 4),
        in_specs=[pl.BlockSpec((None, tr, C), lambda i, q, w: (q, w[0] * nt + i, 0)),
                  pl.BlockSpec((None, tr, C), lambda i, q, w: (q, i, 0))],
        out_specs=[pl.BlockSpec((None, tr, C), lambda i, q, w: (q, i, 0)),
                   pl.BlockSpec((tr, C), lambda i, q, w: (i, 0))])
    return pl.pallas_call(
        body, grid_spec=grid_spec, name=name,
        out_shape=[jax.ShapeDtypeStruct((4, N, C), BF16), jax.ShapeDtypeStruct((N, C), F32)],
        compiler_params=_cp(("parallel", "arbitrary")),
    )(where, g, sib)


HBM = pl.BlockSpec(memory_space=pltpu.HBM)
SEM = pl.BlockSpec(memory_space=pltpu.SEMAPHORE)


def _in_hbm(a):
    return pltpu.with_memory_space_constraint(a, pltpu.HBM)


def _split_start(name, copies, arrays, n_sems, after=None):
    n = len(arrays)

    def body(*refs):
        for cp in copies(refs[:n], refs[-n - 3], refs[-n - 2]):
            cp.start()
        refs[-1][...] = jnp.zeros_like(refs[-1])

    ordered = () if after is None else (after,)
    outs = pl.pallas_call(
        body, name=name,
        out_shape=(pltpu.SemaphoreType.DMA((n_sems,)), pltpu.SemaphoreType.DMA((n_sems,)),
                   *[pltpu.HBM(a.shape, a.dtype) for a in arrays], jax.ShapeDtypeStruct((8, LANES), F32)),
        in_specs=(HBM,) * n + (ANY,) * len(ordered),
        out_specs=(SEM, SEM) + (HBM,) * n + (pl.BlockSpec(memory_space=pltpu.VMEM),),
        input_output_aliases={i: 2 + i for i in range(n)},
        compiler_params=pltpu.CompilerParams(has_side_effects=pltpu.SideEffectType.DATAFLOW_SIDE_EFFECTING),
    )(*[_in_hbm(a) for a in arrays], *ordered)
    return outs[0], outs[1], list(outs[2:2 + n]), outs[-1]


def _split_wait(name, copies, send_sems, recv_sems, arrays, after):
    n = len(arrays)

    def body(*refs):
        for cp in copies(refs[:n], refs[n], refs[n + 1]):
            cp.wait_send()
            cp.wait_recv()

    outs = pl.pallas_call(
        body, name=name,
        out_shape=tuple(pltpu.HBM(a.shape, a.dtype) for a in arrays),
        in_specs=(HBM,) * n + (SEM, SEM, ANY), out_specs=(HBM,) * n,
        input_output_aliases={i: i for i in range(n)},
        compiler_params=pltpu.CompilerParams(has_side_effects=pltpu.SideEffectType.DATAFLOW_SIDE_EFFECTING),
    )(*arrays, send_sems, recv_sems, after)
    return list(outs)


def _remote(src, dst, sems, k, to):
    send_sems, recv_sems = sems
    return pltpu.make_async_remote_copy(src_ref=src, dst_ref=dst, send_sem=send_sems.at[k], recv_sem=recv_sems.at[k],
                                        device_id=to, device_id_type=MESH)


def _chip_at(x, y, rel):
    px = 1 - x if rel & 2 else x
    py = 1 - y if rel & 1 else y
    return px, py, 2 * px + py


def _gather_in_copies(rels):
    def copies(refs, send_sems, recv_sems):
        (w,) = refs
        x, y, c = _coords()
        seg = w.shape[2] // 2
        mine = w.at[2 * x + y, :, pl.ds(c * seg, seg)]
        return [_remote(mine, mine, (send_sems, recv_sems), k, _chip_at(x, y, rel)[:2] + (c,))
                for k, rel in enumerate(rels)]
    return copies


def _gather_out_copies(refs, send_sems, recv_sems):
    (w,) = refs
    x, y, c = _coords()
    mine = w.at[2 * x + y]
    return [_remote(mine, mine, (send_sems, recv_sems), k, (px, py, c)) for k, (px, py) in enumerate(_other_chips(x, y))]


def _swap_copies(refs, send_sems, recv_sems):
    gi, go, si, so = refs
    x, y, c = _coords()
    cps = []
    for a, (src, dst) in enumerate(((gi, si), (go, so))):
        nr = dst.shape[1]
        cps.append(_remote(src.at[:, pl.ds((1 - c) * nr, nr), :], dst, (send_sems, recv_sems), a, (x, y, 1 - c)))
    return cps


def _scatter_copies(refs, send_sems, recv_sems):
    qi, qo, ri, ro = refs
    x, y, c = _coords()
    cps = []
    for k, (px, py) in enumerate(_other_chips(x, y)):
        for a, (src, dst) in enumerate(((qi, ri), (qo, ro))):
            cps.append(_remote(src.at[2 * px + py], dst.at[k], (send_sems, recv_sems), 2 * k + a, (px, py, c)))
    return cps


def _forward_copies(rels):
    def copies(refs, send_sems, recv_sems):
        (w,) = refs
        x, y, c = _coords()
        seg = w.shape[2] // 2
        cps = []
        for k, rel in enumerate(rels):
            got = w.at[_chip_at(x, y, rel)[2], :, pl.ds(c * seg, seg)]
            cps.append(_remote(got, got, (send_sems, recv_sems), k, (x, y, 1 - c)))
        return cps
    return copies


def _chip_sum(own, got, where, name):
    N, C = own.shape
    tr = min(256, N)
    nt = N // tr

    def body(where_ref, own_ref, got_ref, o_ref):
        t = own_ref[...]
        for k in range(3):
            t = t + got_ref[k].astype(F32)
        o_ref[...] = t

    grid_spec = pltpu.PrefetchScalarGridSpec(
        num_scalar_prefetch=1, grid=(nt,),
        in_specs=[pl.BlockSpec((tr, C), lambda i, w: (i, 0)), pl.BlockSpec((3, tr, C), lambda i, w: (0, i, 0))],
        out_specs=pl.BlockSpec((tr, C), lambda i, w: (w[0] * nt + i, 0)))
    return pl.pallas_call(
        body, grid_spec=grid_spec, name=name, out_shape=jax.ShapeDtypeStruct((2 * N, C), F32),
        compiler_params=_cp(("parallel",)),
    )(where, own, got)


def _join_copies(refs, send_sems, recv_sems):
    x, y, c = _coords()
    cps = []
    for a, ref in enumerate(refs):
        nr = ref.shape[0] // 2
        mine = ref.at[pl.ds(c * nr, nr), :]
        cps.append(_remote(mine, mine, (send_sems, recv_sems), a, (x, y, 1 - c)))
    return cps


def _all_reduce_small(part, token):
    R, C = part.shape

    def body(p_ref, _, o_ref, slots, send_sems, recv_sems):
        x, y, c = _coords()
        me = 4 * x + 2 * y + c
        slots[me] = p_ref[...]
        cps = []
        for k in range(1, 8):
            fx, fy, fc = (k >> 2) & 1, (k >> 1) & 1, k & 1
            peer = (1 - x if fx else x, 1 - y if fy else y, 1 - c if fc else c)
            cp = pltpu.make_async_remote_copy(src_ref=p_ref, dst_ref=slots.at[me], send_sem=send_sems.at[k - 1],
                                              recv_sem=recv_sems.at[k - 1], device_id=peer, device_id_type=MESH)
            cp.start()
            cps.append(cp)
        for cp in cps:
            cp.wait()
        t = slots[0]
        for k in range(1, 8):
            t = t + slots[k]
        o_ref[...] = t

    vm = pl.BlockSpec(memory_space=pltpu.VMEM)
    return pl.pallas_call(
        body, name="all_reduce_small", in_specs=[vm, vm], out_specs=vm,
        out_shape=jax.ShapeDtypeStruct((R, C), F32),
        scratch_shapes=[pltpu.VMEM((8, R, C), F32), pltpu.SemaphoreType.DMA((7,)), pltpu.SemaphoreType.DMA((7,))],
    )(part, token)


def _mixers_forward(z, lb_logits, hgrn_gnorm):
    slopes = _alibi_slopes(z[0].shape[1] * LANES)
    yh, states = _hgrn_fwd(z[0], lb_logits, hgrn_gnorm)
    outs, lses = [], []
    for d in DILATIONS:
        o, l = _attn_fwd(z, slopes, d)
        outs.append(o)
        lses.append(l)
    o_attn, lse, ya = _attn_merge(outs, lses, z[0])
    return yh, ya, (states, o_attn, lse, slopes)


def _backward_to_dz(z, kept, lb_logits, hgrn_gnorm, yh, ya, w_out_all, x2, tgt, fgain, h):
    states, o_attn, lse, slopes = kept
    dout, doutb, loss, dfg = _out_proj_loss(yh, ya, w_out_all, x2, tgt, fgain)
    dy = _dy_proj(doutb, w_out_all)
    g_w_out = _grad_w_out(yh, ya, doutb)
    dzh, dlogits, dgn = _hgrn_bwd(z[0], lb_logits, hgrn_gnorm, states, dy)
    do, dl, dza = _attn_gate_bwd(dy, o_attn, z[0])
    acc = None
    order = sorted(DILATIONS, reverse=True)
    for d in order[:-1]:
        acc = _attn_bwd(z, slopes, do, lse, dl, d, acc, None)
    dza = _attn_bwd(z, slopes, do, lse, dl, order[-1], acc, dza)
    sources = [dzh, dza]
    g_w_in = _grad_w_in(h, sources)
    return loss, dfg, dlogits, dgn, g_w_out, g_w_in, sources, dout


def _grad_x_half(sources, w_all, x2, rinv, norm_gain, dout, token, part, gx_prev):
    dh = _dh_proj(sources, w_all, token, part, f"dh_proj_{part}")
    return _rms_bwd(dh, x2, rinv, norm_gain, dout, part, gx_prev, f"rms_bwd_{part}")


def _local_step(x2, tgt, norm_gain, w_all, lb_logits, hgrn_gnorm, w_out_all, fgain):
    token = jnp.zeros((8, LANES), F32)
    where = jnp.zeros((2,), jnp.int32)
    h, rinv = _rms_fwd(x2, norm_gain, token)
    z = _in_proj(h, w_all, where, [(rel, half) for rel in range(4) for half in range(2)], None, token, "in_proj_all")
    yh, ya, kept = _mixers_forward(z, lb_logits, hgrn_gnorm)
    loss, dfg, dlogits, dgn, g_w_out, g_w_in, sources, dout = _backward_to_dz(
        z, kept, lb_logits, hgrn_gnorm, yh, ya, w_out_all, x2, tgt, fgain, h)
    gx, dg0 = _grad_x_half(sources, w_all, x2, rinv, norm_gain, dout, token, 0, None)
    gx, dg1 = _grad_x_half(sources, w_all, x2, rinv, norm_gain, dout, token, 1, gx)
    return loss, gx, dg0 + dg1, g_w_in, dlogits, dgn, g_w_out, dfg


def _pack_small(D, loss, dgain, dlogits, dgn, dfg):
    def row(v):
        v = v.reshape(1, -1)
        return jnp.pad(v, ((0, 0), (0, D - v.shape[1])))
    rows = [row(dgain), row(dfg), row(dlogits[0]), row(dlogits[1]), row(jnp.sum(dgn, axis=0)), row(loss)]
    rows += [jnp.zeros((1, D), F32)] * (8 - len(rows))
    return jnp.concatenate(rows, axis=0)


def kernel(x, norm_gain, w_in, lb_logits, hgrn_gnorm, w_out, final_gain, loss_target, m_norm_gain, m_w_in, m_lb_logits, m_hgrn_gnorm, m_w_out, m_final_gain, v_norm_gain, v_w_in, v_lb_logits, v_hgrn_gnorm, v_w_out, v_final_gain):
    _, S, D = x.shape
    SEG = w_in.shape[2] // 2
    x2 = x[0]
    tgt = loss_target[0]
    fgain = final_gain.reshape(1, D)
    where = jnp.stack([lax.axis_index("c"), 2 * lax.axis_index("x") + lax.axis_index("y")]).astype(jnp.int32)

    wia = _cast_into_slot(w_in[0], where, "cast_w_in")
    woa = _cast_into_slot(w_out[0], where, "cast_w_out")
    near, far = (2, 1), (3,)
    ga = _split_start("gather_near_start", _gather_in_copies(near), [wia], 2)
    h, rinv = _rms_fwd(x2, norm_gain, ga[3])
    z = _in_proj(h, ga[2][0], where, [(0, 0), (0, 1)], None, ga[3], "in_proj_own")
    (wia,) = _split_wait("gather_near_wait", _gather_in_copies(near), ga[0], ga[1], ga[2], z[0])
    gb = _split_start("gather_far_start", _gather_in_copies(far), [wia], 1)
    fa = _split_start("forward_near_start", _forward_copies(near), gb[2], 2, after=gb[3])
    z = _in_proj(h, fa[2][0], where, [(2, "mine"), (1, "mine")], z, fa[3], "in_proj_near")
    (wia,) = _split_wait("forward_near_wait", _forward_copies(near), fa[0], fa[1], fa[2], z[0])
    (wia,) = _split_wait("gather_far_wait", _gather_in_copies(far), gb[0], gb[1], [wia], z[0])
    out_sems = _split_start("gather_out_start", _gather_out_copies, [woa], 3, after=wia)
    fb = _split_start("forward_far_start", _forward_copies(far), [wia], 1, after=out_sems[3])
    z = _in_proj(h, fb[2][0], where, [(3, "mine"), (2, "sibling"), (1, "sibling")], z, fb[3], "in_proj_far")
    (wia,) = _split_wait("forward_far_wait", _forward_copies(far), fb[0], fb[1], fb[2], z[0])
    z = _in_proj(h, wia, where, [(3, "sibling")], z, fb[3], "in_proj_last")
    yh, ya, kept = _mixers_forward(z, lb_logits, hgrn_gnorm)
    (woa,) = _split_wait("gather_out_wait", _gather_out_copies, out_sems[0], out_sems[1], out_sems[2], ya)
    w_out_all = woa.reshape(2 * SEG, D)

    loss, dfg, dlogits, dgn, g_w_out, g_w_in, sources, dout = _backward_to_dz(
        z, kept, lb_logits, hgrn_gnorm, yh, ya, w_out_all, x2, tgt, fgain, h)

    sib_i = lax.empty((4, g_w_in.shape[1] // 2, g_w_in.shape[2]), F32)
    sib_o = lax.empty((4, g_w_out.shape[1] // 2, g_w_out.shape[2]), F32)
    sems = _split_start("swap_start", _swap_copies, [g_w_in, g_w_out, sib_i, sib_o], 2)
    grad_x, dg0 = _grad_x_half(sources, wia, x2, rinv, norm_gain, dout, sems[3], 0, None)
    g_w_in, g_w_out, sib_i, sib_o = _split_wait("swap_wait", _swap_copies, sems[0], sems[1], sems[2], grad_x)
    qi, own_i = _pair_sum(g_w_in, sib_i, where, "pair_sum_w_in")
    qo, own_o = _pair_sum(g_w_out, sib_o, where, "pair_sum_w_out")
    ri = lax.empty((3,) + qi.shape[1:], BF16)
    ro = lax.empty((3,) + qo.shape[1:], BF16)
    sems = _split_start("scatter_start", _scatter_copies, [qi, qo, ri, ro], 6)
    grad_x, dg1 = _grad_x_half(sources, wia, x2, rinv, norm_gain, dout, sems[3], 1, grad_x)
    _, _, got_i, got_o = _split_wait("scatter_wait", _scatter_copies, sems[0], sems[1], sems[2], grad_x)
    jn = _split_start("join_start", _join_copies, [_chip_sum(own_i, got_i, where, "chip_sum_w_in"),
                                                   _chip_sum(own_o, got_o, where, "chip_sum_w_out")], 2)
    small = _all_reduce_small(_pack_small(D, loss, dg0 + dg1, dlogits, dgn, dfg), jn[3])
    loss_sum = small[5, 0]
    d_ng, m_ng, v_ng, grad_norm_gain = _adamw(norm_gain, small[0:1, :], m_norm_gain, v_norm_gain, "adamw_norm_gain")
    d_lb, m_lb, v_lb, grad_lb_logits = _adamw(lb_logits, small[2:4, :SEG], m_lb_logits, v_lb_logits, "adamw_lb_logits")
    d_gn, m_gn, v_gn, grad_hgrn_gnorm = _adamw(hgrn_gnorm, small[4:5, :HGRN_HEAD], m_hgrn_gnorm, v_hgrn_gnorm,
                                               "adamw_hgrn_gnorm")
    d_fg, m_fg, v_fg, grad_final_gain = _adamw(fgain, small[1:2, :], m_final_gain.reshape(1, D),
                                               v_final_gain.reshape(1, D), "adamw_final_gain")
    g_w_in, g_w_out = _split_wait("join_wait", _join_copies, jn[0], jn[1], jn[2], d_fg)
    d_wi, m_wi, v_wi, grad_w_in = _adamw(w_in[0], g_w_in, m_w_in[0], v_w_in[0], "adamw_w_in")
    d_wo, m_wo, v_wo, grad_w_out = _adamw(w_out[0], g_w_out, m_w_out[0], v_w_out[0], "adamw_w_out")

    return (loss_sum, grad_x[None],
            grad_norm_gain, grad_w_in[None], grad_lb_logits, grad_hgrn_gnorm, grad_w_out[None], grad_final_gain[0],
            d_ng, d_wi[None], d_lb, d_gn, d_wo[None], d_fg[0],
            m_ng, m_wi[None], m_lb, m_gn, m_wo[None], m_fg[0],
            v_ng, v_wi[None], v_lb, v_gn, v_wo[None], v_fg[0])
```

```python
import jax
import jax.numpy as jnp
import numpy as np
from jax import lax
from jax.experimental import pallas as pl
from jax.experimental.pallas import tpu as pltpu

F32 = jnp.float32
BF16 = jnp.bfloat16
MESH = pl.DeviceIdType.MESH

NORM_EPS = 1e-6
HGRN_HEAD = 128
HGRN_CHUNK = 64
HGRN_TILE = 128
HGRN_BLOCK = 512
HGRN_HEADS_PER_STEP = 4
ATTN_HEAD = 64
LANES = 128
BAND = 128
DILATIONS = (1, 4, 16)
DEINTERLEAVE = 16
ATTN_SCALE = ATTN_HEAD ** -0.5
assert ATTN_SCALE == 0.125
ATTN_BLOCK_ELEMS = BAND * 2048
ATTN_UNROLL = 4
SEG_QKV = 4
SEG_GATE_A = 7
NEG = -1e30

ADAM_LR = 0.001
ADAM_B1 = 0.9
ADAM_B2 = 0.999
ADAM_EPS = 1e-08
ADAM_WD = 0.01
ADAM_STEP = 10

MIB = 1024 * 1024


def _cp(semantics=None, vmem_mib=48):
    return pltpu.CompilerParams(dimension_semantics=semantics, vmem_limit_bytes=vmem_mib * MIB)


def _dot(a, b):
    return jnp.dot(a, b, preferred_element_type=F32)


def _dot_nt(a, b):
    return lax.dot_general(a, b, (((1,), (1,)), ((), ())), preferred_element_type=F32)


def _dot_tn(a, b):
    return lax.dot_general(a, b, (((0,), (0,)), ((), ())), preferred_element_type=F32)


def _split3(x):
    hi = x.astype(BF16)
    r1 = x - hi.astype(F32)
    mid = r1.astype(BF16)
    lo = (r1 - mid.astype(F32)).astype(BF16)
    return hi, mid, lo


def _exact_dot(t_bf16, x):
    hi, mid, lo = _split3(x)
    return _dot(t_bf16, hi) + _dot(t_bf16, mid) + _dot(t_bf16, lo)


def _exact_dot_right(x, t_bf16):
    hi, mid, lo = _split3(x)
    return _dot(hi, t_bf16) + _dot(mid, t_bf16) + _dot(lo, t_bf16)


def _sigmoid(z):
    return jax.nn.sigmoid(z)


def _silu_and_grad(z):
    s = _sigmoid(z)
    return z * s, s * (1.0 + z * (1.0 - s))


def _seg_select(j, values):
    out = values[0]
    for t, v in enumerate(values[1:], 1):
        out = jnp.where(j == t, v, out)
    return out


def _rms_fwd(x2, gain, token):
    S, D = x2.shape
    tm = min(512, S)

    def body(x_ref, g_ref, _, h_ref, r_ref):
        x = x_ref[...]
        r = lax.rsqrt(jnp.mean(x * x, axis=-1, keepdims=True) + NORM_EPS)
        h_ref[...] = ((x * r) * g_ref[...]).astype(BF16)
        r_ref[...] = r

    return pl.pallas_call(
        body, grid=(S // tm,), name="rms_fwd",
        in_specs=[pl.BlockSpec((tm, D), lambda i: (i, 0)), pl.BlockSpec((1, D), lambda i: (0, 0)),
                  pl.BlockSpec(token.shape, lambda i: (0, 0))],
        out_specs=[pl.BlockSpec((tm, D), lambda i: (i, 0)), pl.BlockSpec((tm, 1), lambda i: (i, 0))],
        out_shape=[jax.ShapeDtypeStruct((S, D), BF16), jax.ShapeDtypeStruct((S, 1), F32)],
        compiler_params=_cp(("parallel",)),
    )(x2, gain, token)


def _in_proj(h, w_all, where, segs, z_prev, token, name):
    S, D = h.shape
    SEG = w_all.shape[2] // 2
    NLB = SEG // LANES
    tm = min(512, S)
    count = len(segs)
    DI = DEINTERLEAVE
    tu = tm // DI

    def is_qkv(seg):
        return (seg >= SEG_QKV) & (seg < SEG_QKV + 3)

    def seg_of(j, w):
        halves = {0: 0, 1: 1, "mine": w[0], "sibling": 1 - w[0]}
        cands = [2 * jnp.bitwise_xor(w[1], rel) + halves[half] for rel, half in segs]
        keys = [is_qkv(s).astype(jnp.int32) for s in cands]
        out = cands[0]
        for k in range(count):
            pos = (sum(jnp.where(keys[t] < keys[k], 1, 0) for t in range(count))
                   + sum(jnp.where(keys[t] == keys[k], 1, 0) for t in range(k)))
            out = jnp.where(pos == j, cands[k], out)
        return out

    def body(*refs):
        where_ref, h_ref, w_ref = refs[:3]
        o_ref, o16_ref = refs[-2:]
        res = _dot(h_ref[...], w_ref[...])
        for p in range(NLB):
            o_ref[p] = res[:, p * LANES:(p + 1) * LANES]

        @pl.when(is_qkv(seg_of(pl.program_id(0), where_ref)))
        def _():
            for p in range(NLB):
                for r in range(DI):
                    o16_ref[p, r] = o_ref.at[p][pl.ds(r, tu, stride=DI), :]

    def z16_map(j, i, w):
        seg = seg_of(j, w)
        return (jnp.where(is_qkv(seg), seg - SEG_QKV, 3), 0, 0, jnp.where(is_qkv(seg), i, 0), 0)

    in_specs = [pl.BlockSpec((tm, D), lambda j, i, w: (i, 0)),
                pl.BlockSpec((None, D, SEG), lambda j, i, w: (seg_of(j, w) // 2, 0, seg_of(j, w) % 2)),
                pl.BlockSpec(token.shape, lambda j, i, w: (0, 0))]
    args = [where, h, w_all, token]
    aliases = {}
    if z_prev is not None:
        in_specs += [ANY, ANY]
        args += list(z_prev)
        aliases = {4: 0, 5: 1}
    grid_spec = pltpu.PrefetchScalarGridSpec(
        num_scalar_prefetch=1, grid=(count, S // tm), in_specs=in_specs,
        out_specs=[pl.BlockSpec((None, NLB, tm, LANES), lambda j, i, w: (seg_of(j, w), 0, i, 0)),
                   pl.BlockSpec((None, NLB, DI, tu, LANES), z16_map)])
    return pl.pallas_call(
        body, grid_spec=grid_spec, name=name,
        out_shape=[jax.ShapeDtypeStruct((8, NLB, S, LANES), F32),
                   jax.ShapeDtypeStruct((4, NLB, DI, S // DI, LANES), F32)],
        input_output_aliases=aliases, compiler_params=_cp(("parallel", "parallel")),
    )(*args)


def _out_proj_loss(yh, ya, w_out, x2, tgt, fgain):
    S, D = x2.shape
    SEG = yh.shape[1]
    tm = min(256, S)
    parts = 2

    def body(yh_ref, ya_ref, w_ref, x_ref, t_ref, fg_ref, dout_ref, doutb_ref, loss_ref, dfg_ref):
        i = pl.program_id(0)

        @pl.when(i == 0)
        def _():
            loss_ref[...] = jnp.zeros_like(loss_ref)
            dfg_ref[...] = jnp.zeros_like(dfg_ref)

        fg = fg_ref[...]
        loss = jnp.zeros((1, 1), F32)
        dfg = jnp.zeros((1, D), F32)
        for rows in [pl.ds(p * (tm // parts), tm // parts) for p in range(parts)]:
            out = (x_ref[rows, :] + _dot(yh_ref[rows, :], w_ref[pl.ds(0, SEG), :])
                   + _dot(ya_ref[rows, :], w_ref[pl.ds(SEG, SEG), :]))
            r = lax.rsqrt(jnp.mean(out * out, axis=-1, keepdims=True) + NORM_EPS)
            n = out * r
            err = n * fg - t_ref[rows, :]
            loss = loss + 0.5 * jnp.sum(jnp.mean(err * err, axis=-1, keepdims=True), axis=0, keepdims=True)
            dy = err * (1.0 / D)
            dfg = dfg + jnp.sum(dy * n, axis=0, keepdims=True)
            dn = dy * fg
            dout = r * (dn - n * jnp.mean(dn * n, axis=-1, keepdims=True))
            dout_ref[rows, :] = dout
            doutb_ref[rows, :] = dout.astype(BF16)
        loss_ref[...] += loss
        dfg_ref[...] += dfg

    row = lambda i: (i, 0)
    fix = lambda i: (0, 0)
    return pl.pallas_call(
        body, grid=(S // tm,), name="out_proj_loss",
        in_specs=[pl.BlockSpec((tm, SEG), row), pl.BlockSpec((tm, SEG), row), pl.BlockSpec((2 * SEG, D), fix),
                  pl.BlockSpec((tm, D), row), pl.BlockSpec((tm, D), row), pl.BlockSpec((1, D), fix)],
        out_specs=[pl.BlockSpec((tm, D), row), pl.BlockSpec((tm, D), row), pl.BlockSpec((1, 1), fix),
                   pl.BlockSpec((1, D), fix)],
        out_shape=[jax.ShapeDtypeStruct((S, D), F32), jax.ShapeDtypeStruct((S, D), BF16),
                   jax.ShapeDtypeStruct((1, 1), F32), jax.ShapeDtypeStruct((1, D), F32)],
        compiler_params=_cp(("arbitrary",)),
    )(yh, ya, w_out, x2, tgt, fgain)


def _dy_proj(doutb, w_out):
    S, D = doutb.shape
    K = w_out.shape[0]
    tm = min(512, S)

    def body(d_ref, w_ref, o_ref):
        o_ref[...] = _dot_nt(d_ref[...], w_ref[...])

    return pl.pallas_call(
        body, grid=(S // tm,), name="dy_proj",
        in_specs=[pl.BlockSpec((tm, D), lambda i: (i, 0)), pl.BlockSpec((K, D), lambda i: (0, 0))],
        out_specs=pl.BlockSpec((tm, K), lambda i: (i, 0)),
        out_shape=jax.ShapeDtypeStruct((S, K), F32),
        compiler_params=_cp(("parallel",)),
    )(doutb, w_out)


def _grad_w_out(yh, ya, doutb):
    S, SEG = yh.shape
    D = doutb.shape[1]
    R = (2 * SEG) // 4
    nb_half = SEG // R
    tk = min(1024, S)

    def body(yh_ref, ya_ref, d_ref, o_ref):
        q = pl.program_id(0)
        k = pl.program_id(1)

        @pl.when(k == 0)
        def _():
            o_ref[...] = jnp.zeros_like(o_ref)

        @pl.when(q < nb_half)
        def _():
            o_ref[...] += _dot_tn(yh_ref[...], d_ref[...])

        @pl.when(q >= nb_half)
        def _():
            o_ref[...] += _dot_tn(ya_ref[...], d_ref[...])

    return pl.pallas_call(
        body, grid=(4, S // tk), name="grad_w_out",
        in_specs=[pl.BlockSpec((tk, R), lambda q, k: (k, jnp.minimum(q, nb_half - 1))),
                  pl.BlockSpec((tk, R), lambda q, k: (k, jnp.maximum(q - nb_half, 0))),
                  pl.BlockSpec((tk, D), lambda q, k: (k, 0))],
        out_specs=pl.BlockSpec((None, R, D), lambda q, k: (q, 0, 0)),
        out_shape=jax.ShapeDtypeStruct((4, R, D), F32),
        compiler_params=_cp(("parallel", "arbitrary")),
    )(yh, ya, doutb)


def _dz_sources(sources):
    counts = [s.shape[0] for s in sources]
    starts = [sum(counts[:k]) for k in range(len(counts))]
    assert sum(counts) == 8
    return counts, starts


def _row_part(S, part, tile):
    first = max(512, (S * 3 // 8) // 512 * 512)
    rows = first if part == 0 else S - first
    assert rows % tile == 0 and first % tile == 0
    return (0 if part == 0 else first // tile), rows // tile, rows


def _dh_proj(sources, w_all, token, part, name):
    S = sources[0].shape[1]
    D = w_all.shape[1]
    SEG = w_all.shape[2] // 2
    counts, starts = _dz_sources(sources)
    assert all(c % 2 == 0 for c in counts)
    ns = len(sources)
    tm = 1024 if all(_row_part(S, p, 1)[2] % 1024 == 0 for p in (0, 1)) else 512
    t0, nt, nrows = _row_part(S, part, tm)

    def body(*refs):
        src = refs[:ns]
        w_ref, _, o_ref = refs[ns:]
        j = pl.program_id(1)

        @pl.when(j == 0)
        def _():
            o_ref[...] = jnp.zeros_like(o_ref)

        for k in range(ns):
            @pl.when((2 * j >= starts[k]) & (2 * j < starts[k] + counts[k]))
            def _(k=k):
                o_ref[...] += (_dot_nt(src[k][0], w_ref[:, pl.ds(0, SEG)])
                               + _dot_nt(src[k][1], w_ref[:, pl.ds(SEG, SEG)]))

    def src_spec(k):
        return pl.BlockSpec((2, tm, SEG),
                            lambda i, j: (jnp.clip(j - starts[k] // 2, 0, counts[k] // 2 - 1), t0 + i, 0))

    return pl.pallas_call(
        body, grid=(nt, 4), name=name,
        in_specs=[src_spec(k) for k in range(ns)] + [pl.BlockSpec((None, D, 2 * SEG), lambda i, j: (j, 0, 0)),
                                                     pl.BlockSpec(token.shape, lambda i, j: (0, 0))],
        out_specs=pl.BlockSpec((tm, D), lambda i, j: (i, 0)),
        out_shape=jax.ShapeDtypeStruct((nrows, D), F32),
        compiler_params=_cp(("parallel", "arbitrary"), 48 if tm == 512 else 60),
    )(*sources, w_all, token)


def _rms_bwd(dh, x2, rinv, gain, dout, part, gx_prev, name):
    S, D = x2.shape
    tm = 256
    t0, nt, _ = _row_part(S, part, tm)

    def body(dh_ref, x_ref, r_ref, g_ref, dout_ref, *rest):
        gx_ref, dg_ref = rest[-2:]

        @pl.when(pl.program_id(0) == 0)
        def _():
            dg_ref[...] = jnp.zeros_like(dg_ref)

        dh = dh_ref[...]
        r = r_ref[...]
        xhat = x_ref[...] * r
        dg_ref[...] += jnp.sum(dh * xhat, axis=0, keepdims=True)
        dxn = dh * g_ref[...]
        gx_ref[...] = dout_ref[...] + r * (dxn - xhat * jnp.mean(dxn * xhat, axis=-1, keepdims=True))

    row = lambda i: (t0 + i, 0)
    fix = lambda i: (0, 0)
    in_specs = [pl.BlockSpec((tm, D), lambda i: (i, 0)), pl.BlockSpec((tm, D), row), pl.BlockSpec((tm, 1), row),
                pl.BlockSpec((1, D), fix), pl.BlockSpec((tm, D), row)]
    args = [dh, x2, rinv, gain, dout]
    aliases = {}
    if gx_prev is not None:
        in_specs.append(ANY)
        args.append(gx_prev)
        aliases = {5: 0}
    return pl.pallas_call(
        body, grid=(nt,), name=name, in_specs=in_specs,
        out_specs=[pl.BlockSpec((tm, D), row), pl.BlockSpec((1, D), fix)],
        out_shape=[jax.ShapeDtypeStruct((S, D), F32), jax.ShapeDtypeStruct((1, D), F32)],
        input_output_aliases=aliases, compiler_params=_cp(("arbitrary",)),
    )(*args)


def _grad_w_in(h, sources):
    S, D = h.shape
    SEG = sources[0].shape[2]
    counts, starts = _dz_sources(sources)
    ns = len(sources)
    tk = min(2048, S)

    def body(*refs):
        h_ref = refs[0]
        src = refs[1:1 + ns]
        o_ref = refs[1 + ns]
        j = pl.program_id(0)
        k = pl.program_id(1)

        @pl.when(k == 0)
        def _():
            o_ref[...] = jnp.zeros_like(o_ref)

        for s in range(ns):
            @pl.when((j >= starts[s]) & (j < starts[s] + counts[s]))
            def _(s=s):
                o_ref[...] += _dot_tn(h_ref[...], src[s][...])

    def src_spec(s):
        return pl.BlockSpec((None, tk, SEG),
                            lambda j, k: (jnp.clip(j - starts[s], 0, counts[s] - 1), k, 0))

    return pl.pallas_call(
        body, grid=(8, S // tk), name="grad_w_in",
        in_specs=[pl.BlockSpec((tk, D), lambda j, k: (k, 0))] + [src_spec(s) for s in range(ns)],
        out_specs=pl.BlockSpec((None, D, SEG), lambda j, k: (j // 2, 0, j % 2)),
        out_shape=jax.ShapeDtypeStruct((4, D, 2 * SEG), F32),
        compiler_params=_cp(("parallel", "arbitrary"), 48 if tk <= 1024 else 62),
    )(h, *sources)


def _lower_bound(lbl):
    l0 = lbl[0:1, :]
    l1 = lbl[1:2, :]
    m = jnp.maximum(l0, l1)
    e0 = jnp.exp(l0 - m)
    e1 = jnp.exp(l1 - m)
    return e0 / (e0 + e1)


def _tile_masks():
    row = lax.broadcasted_iota(jnp.int32, (HGRN_TILE, HGRN_TILE), 0)
    col = lax.broadcasted_iota(jnp.int32, (HGRN_TILE, HGRN_TILE), 1)
    same = (row // HGRN_CHUNK) == (col // HGRN_CHUNK)
    return same & (row >= col), same & (row <= col)


def _chunk_last(b):
    T = b.shape[0]
    b3 = b.reshape(T // HGRN_CHUNK, HGRN_CHUNK, HGRN_HEAD)
    return jnp.broadcast_to(b3[:, HGRN_CHUNK - 1:HGRN_CHUNK, :], b3.shape).reshape(T, HGRN_HEAD)


def _chunk_sum(x):
    T = x.shape[0]
    x3 = x.reshape(T // HGRN_CHUNK, HGRN_CHUNK, HGRN_HEAD)
    return jnp.broadcast_to(jnp.sum(x3, axis=1, keepdims=True), x3.shape).reshape(T, HGRN_HEAD)


def _hgrn_dims(S, SEG):
    T = min(HGRN_BLOCK, S)
    assert S % T == 0 and T % HGRN_TILE == 0
    tiles = [slice(t * HGRN_TILE, (t + 1) * HGRN_TILE) for t in range(T // HGRN_TILE)]
    chunks = [slice(c * HGRN_CHUNK, (c + 1) * HGRN_CHUNK) for c in range(T // HGRN_CHUNK)]
    return SEG // HGRN_HEAD, T, T // HGRN_CHUNK, S // T, tiles, chunks


def _hgrn_fwd(zf32, lb_logits, gnorm):
    _, NLB, S, _ = zf32.shape
    SEG = NLB * LANES
    H, T, NC, NJ, tiles, chunks = _hgrn_dims(S, SEG)
    HP = min(HGRN_HEADS_PER_STEP, H)
    assert H % HP == 0

    def body(zq_ref, zf_ref, zi_ref, zg_ref, lbl_ref, gn_ref, y_ref, st_ref, state):
        @pl.when(pl.program_id(1) == 0)
        def _():
            state[...] = jnp.zeros_like(state)

        tril, _ = _tile_masks()
        tril_bf = tril.astype(BF16)
        for hh in range(HP):
            cols = slice(hh * HGRN_HEAD, (hh + 1) * HGRN_HEAD)
            lb = _lower_bound(lbl_ref[:, cols])
            zq = zq_ref[hh]
            q = zq * _sigmoid(zq)
            f = lb + (1.0 - lb) * _sigmoid(zf_ref[hh])
            k = 1.0 - f
            logf = jnp.log(f)
            b = jnp.concatenate([_exact_dot(tril_bf, logf[t]) for t in tiles], axis=0)
            bl = _chunk_last(b)
            qd_b = (q * jnp.exp(b)).astype(BF16)
            kd_b = (k * jnp.exp(-b)).astype(BF16)
            ke_b = (k * jnp.exp(bl - b)).astype(BF16)
            v_b = zi_ref[hh].astype(BF16)
            o_intra = jnp.concatenate(
                [_dot(jnp.where(tril, _dot_nt(qd_b[t], kd_b[t]), 0.0).astype(BF16), v_b[t]) for t in tiles], axis=0)
            kvs = [_dot_tn(v_b[r], ke_b[r]) for r in chunks]
            ebl = jnp.exp(bl)
            st = state[hh]
            sts = []
            for c in range(NC):
                st_ref[c, hh] = st
                sts.append(st.astype(BF16))
                st = st * ebl[c * HGRN_CHUNK:c * HGRN_CHUNK + 1, :] + kvs[c]
            state[hh] = st
            o = o_intra + jnp.concatenate([_dot_nt(qd_b[r], sb) for r, sb in zip(chunks, sts)], axis=0)
            on = o * lax.rsqrt(jnp.mean(o * o, axis=-1, keepdims=True) + NORM_EPS) * gn_ref[...]
            zg = zg_ref[hh]
            y_ref[:, cols] = (on * (zg * _sigmoid(zg))).astype(BF16)

    def zspec(seg):
        return pl.BlockSpec((None, HP, T, HGRN_HEAD), lambda h, j: (seg, h, j, 0))

    return pl.pallas_call(
        body, grid=(H // HP, NJ), name="hgrn_fwd",
        in_specs=[zspec(0), zspec(1), zspec(2), zspec(3),
                  pl.BlockSpec((2, HP * HGRN_HEAD), lambda h, j: (0, h)),
                  pl.BlockSpec((1, HGRN_HEAD), lambda h, j: (0, 0))],
        out_specs=[pl.BlockSpec((T, HP * HGRN_HEAD), lambda h, j: (j, h)),
                   pl.BlockSpec((NC, HP, HGRN_HEAD, HGRN_HEAD), lambda h, j: (j, h, 0, 0))],
        out_shape=[jax.ShapeDtypeStruct((S, SEG), BF16),
                   jax.ShapeDtypeStruct((S // HGRN_CHUNK, H, HGRN_HEAD, HGRN_HEAD), F32)],
        scratch_shapes=[pltpu.VMEM((HP, HGRN_HEAD, HGRN_HEAD), F32)],
        compiler_params=_cp(("parallel", "arbitrary")),
    )(zf32, zf32, zf32, zf32, lb_logits, gnorm)


def _hgrn_bwd(zf32, lb_logits, gnorm, states, dy):
    _, NLB, S, _ = zf32.shape
    SEG = NLB * LANES
    H, T, NC, NJ, tiles, chunks = _hgrn_dims(S, SEG)
    C = HGRN_CHUNK
    HP = min(HGRN_HEADS_PER_STEP, H)
    assert H % HP == 0

    def body(zq_ref, zf_ref, zi_ref, zg_ref, lbl_ref, gn_ref, st_ref, dy_ref, dz_ref, dl_ref, dgn_ref, gstate):
        @pl.when(pl.program_id(1) == 0)
        def _():
            gstate[...] = jnp.zeros_like(gstate)
            dl_ref[...] = jnp.zeros_like(dl_ref)
            dgn_ref[...] = jnp.zeros_like(dgn_ref)

        gn = gn_ref[...]
        tril, triu = _tile_masks()
        tril_bf = tril.astype(BF16)
        triu_bf = triu.astype(BF16)
        for hh in range(HP):
            cols = slice(hh * HGRN_HEAD, (hh + 1) * HGRN_HEAD)
            lb = _lower_bound(lbl_ref[:, cols])
            q, dq_dz = _silu_and_grad(zq_ref[hh])
            sf = _sigmoid(zf_ref[hh])
            f = lb + (1.0 - lb) * sf
            k = 1.0 - f
            logf = jnp.log(f)
            b = jnp.concatenate([_exact_dot(tril_bf, logf[t]) for t in tiles], axis=0)
            bl = _chunk_last(b)
            eb = jnp.exp(b)
            enb = jnp.exp(-b)
            ekl = jnp.exp(bl - b)
            ebl = jnp.exp(bl)
            qd = q * eb
            kd = k * enb
            ke = k * ekl
            qd_b = qd.astype(BF16)
            kd_b = kd.astype(BF16)
            ke_b = ke.astype(BF16)
            v_b = zi_ref[hh].astype(BF16)
            sts = [st_ref[c, hh] for c in range(NC)]
            sts_b = [s.astype(BF16) for s in sts]
            a_b = [jnp.where(tril, _dot_nt(qd_b[t], kd_b[t]), 0.0).astype(BF16) for t in tiles]
            o = (jnp.concatenate([_dot(a, v_b[t]) for a, t in zip(a_b, tiles)], axis=0)
                 + jnp.concatenate([_dot_nt(qd_b[r], sb) for r, sb in zip(chunks, sts_b)], axis=0))
            rinv = lax.rsqrt(jnp.mean(o * o, axis=-1, keepdims=True) + NORM_EPS)
            ohat = o * rinv
            sg, dsg = _silu_and_grad(zg_ref[hh])
            dyv = dy_ref[:, cols]
            don = dyv * sg
            dz_ref[3, :, cols] = (dyv * (ohat * gn) * dsg).astype(BF16)
            dgn_ref[hh] += jnp.sum(don * ohat, axis=0, keepdims=True)
            dohat = don * gn
            do = rinv * (dohat - ohat * jnp.mean(dohat * ohat, axis=-1, keepdims=True))
            do_b = do.astype(BF16)
            da_b = [jnp.where(tril, _dot_nt(do_b[t], v_b[t]), 0.0).astype(BF16) for t in tiles]
            dv_intra = jnp.concatenate([_dot_tn(a, do_b[t]) for a, t in zip(a_b, tiles)], axis=0)
            dqd_intra = jnp.concatenate([_dot(da, kd_b[t]) for da, t in zip(da_b, tiles)], axis=0)
            dkd = jnp.concatenate([_dot_tn(da, qd_b[t]) for da, t in zip(da_b, tiles)], axis=0)
            dqd_inter = jnp.concatenate([_dot(do_b[r], sb) for r, sb in zip(chunks, sts_b)], axis=0)
            gks = [_dot_tn(do_b[r], qd_b[r]) for r in chunks]
            g = gstate[hh]
            gs = [None] * NC
            for c in reversed(range(NC)):
                gs[c] = g
                g = g * ebl[c * C:c * C + 1, :] + gks[c]
            gstate[hh] = g
            gs_b = [x.astype(BF16) for x in gs]
            dv = dv_intra + jnp.concatenate([_dot_nt(ke_b[r], gb) for r, gb in zip(chunks, gs_b)], axis=0)
            dz_ref[2, :, cols] = dv.astype(BF16)
            dke = jnp.concatenate([_dot(v_b[r], gb) for r, gb in zip(chunks, gs_b)], axis=0)
            debl = jnp.concatenate(
                [jnp.broadcast_to(jnp.sum(x * s, axis=0, keepdims=True), (C, HGRN_HEAD)) for x, s in zip(gs, sts)], axis=0)
            dqd = dqd_intra + dqd_inter
            dz_ref[0, :, cols] = ((dqd * eb) * dq_dz).astype(BF16)
            t_ke = dke * ke
            db = dqd * qd - dkd * kd - t_ke
            db_last = _chunk_sum(t_ke) + debl * ebl
            dk = dkd * enb + dke * ekl
            dlogf = jnp.concatenate([_exact_dot(triu_bf, db[t]) for t in tiles], axis=0) + db_last
            df = dlogf / f - dk
            dz_ref[1, :, cols] = (df * (1.0 - lb) * (sf * (1.0 - sf))).astype(BF16)
            dlb = jnp.sum(df * (1.0 - sf), axis=0, keepdims=True)
            dl0 = dlb * lb * (1.0 - lb)
            dl_ref[0:1, cols] += dl0
            dl_ref[1:2, cols] -= dl0

    def zspec(seg):
        return pl.BlockSpec((None, HP, T, HGRN_HEAD), lambda h, j: (seg, h, NJ - 1 - j, 0))

    return pl.pallas_call(
        body, grid=(H // HP, NJ), name="hgrn_bwd",
        in_specs=[zspec(0), zspec(1), zspec(2), zspec(3),
                  pl.BlockSpec((2, HP * HGRN_HEAD), lambda h, j: (0, h)),
                  pl.BlockSpec((1, HGRN_HEAD), lambda h, j: (0, 0)),
                  pl.BlockSpec((NC, HP, HGRN_HEAD, HGRN_HEAD), lambda h, j: (NJ - 1 - j, h, 0, 0)),
                  pl.BlockSpec((T, HP * HGRN_HEAD), lambda h, j: (NJ - 1 - j, h))],
        out_specs=[pl.BlockSpec((4, T, HP * HGRN_HEAD), lambda h, j: (0, NJ - 1 - j, h)),
                   pl.BlockSpec((2, HP * HGRN_HEAD), lambda h, j: (0, h)),
                   pl.BlockSpec((HP, 1, HGRN_HEAD), lambda h, j: (h, 0, 0))],
        out_shape=[jax.ShapeDtypeStruct((4, S, SEG), BF16), jax.ShapeDtypeStruct((2, SEG), F32),
                   jax.ShapeDtypeStruct((H, 1, HGRN_HEAD), F32)],
        scratch_shapes=[pltpu.VMEM((HP, HGRN_HEAD, HGRN_HEAD), F32)],
        compiler_params=_cp(("parallel", "arbitrary")),
    )(zf32, zf32, zf32, zf32, lb_logits, gnorm, states, dy)


def _alibi_slopes(seg):
    n_heads = seg // ATTN_HEAD
    s = 2.0 ** (-8.0 * np.arange(1, n_heads + 1, dtype=np.float64) / n_heads)
    return jnp.asarray(np.repeat(s, ATTN_HEAD)[None, :], F32)


def _attn_dims(S, SEG, d):
    rb = BAND * d
    assert S % rb == 0 and SEG % LANES == 0
    npb = max(1, min(SEG // LANES, ATTN_BLOCK_ELEMS // (rb * LANES)))
    assert (SEG // LANES) % npb == 0
    return rb, npb, S // rb, (SEG // LANES) // npb


def _res_rows(r, d):
    return pl.ds(0, BAND) if d == 1 else pl.ds(r, BAND, stride=d)


def _for_residues(d, fn):
    if d == 1:
        fn(0)
    else:
        def step(r, carry):
            fn(r)
            return carry
        lax.fori_loop(0, d, step, 0, unroll=ATTN_UNROLL)


def _for_groups(d, n_pairs, fn):
    def over_pairs(r):
        for g0 in range(0, n_pairs, ATTN_UNROLL):
            fn([(r, p) for p in range(g0, min(n_pairs, g0 + ATTN_UNROLL))])

    if d == 1:
        over_pairs(0)
    elif n_pairs >= ATTN_UNROLL:
        def step(r, carry):
            over_pairs(r)
            return carry
        lax.fori_loop(0, d, step, 0)
    else:
        per_group = ATTN_UNROLL // n_pairs
        assert d % per_group == 0

        def step(g, carry):
            fn([(g * per_group + i, p) for i in range(per_group) for p in range(n_pairs)])
            return carry
        lax.fori_loop(0, d // per_group, step, 0)


def _band_terms(n, d):
    i = lax.broadcasted_iota(jnp.int32, (BAND, 2 * BAND), 0)
    jj = lax.broadcasted_iota(jnp.int32, (BAND, 2 * BAND), 1)
    delta = BAND + i - jj
    valid = (delta >= 0) & (delta <= BAND) & ((n > 0) | (jj >= BAND))
    return (-d * delta).astype(F32), valid


def _head_biases(slopes, nd, valid):
    out = []
    for s in _per_head(slopes):
        s2 = jnp.concatenate([s, s], axis=1)
        out.append(jnp.where(valid, s2 * nd, NEG))
    return jnp.concatenate(out, axis=0)


def _stack_heads(x):
    lane = lax.broadcasted_iota(jnp.int32, x.shape, 1)
    zero = jnp.zeros_like(x)
    return jnp.concatenate([jnp.where(lane < ATTN_HEAD, x, zero), jnp.where(lane < ATTN_HEAD, zero, x)], axis=0)


def _unstack_heads(x2):
    first = lax.broadcasted_iota(jnp.int32, (BAND, LANES), 1) < ATTN_HEAD
    return jnp.where(first, x2[:BAND], x2[BAND:])


def _stack_per_head(x):
    a, b = _per_head(x)
    col = jnp.concatenate([a, b], axis=0)
    return jnp.concatenate([col, col], axis=1)


def _per_head(x):
    lane = lax.broadcasted_iota(jnp.int32, x.shape, 1)
    sw = pltpu.roll(x, ATTN_HEAD, 1)
    first = lane < ATTN_HEAD
    return jnp.where(first, x, sw), jnp.where(first, sw, x)


def _qkv_source(zz, d):
    z, z16 = zz
    if d == DEINTERLEAVE:
        def take(ref, p, r):
            return ref.at[p][r]

        def spec(seg, np_, row_block):
            return pl.BlockSpec((None, np_, d, BAND, LANES), lambda c, n: (seg, c, 0, row_block(c, n), 0))
        return z, z16, take, spec

    def take(ref, p, r):
        return ref.at[p][_res_rows(r, d), :]

    def spec(seg, np_, row_block):
        return pl.BlockSpec((None, np_, BAND * d, LANES), lambda c, n: (SEG_QKV + seg, c, row_block(c, n), 0))
    return z, z, take, spec


def _attn_fwd(qkv, slopes, d):
    qkv, src, take, spec = _qkv_source(qkv, d)
    _, NLB, S, _ = qkv.shape
    rb, NP, nb, ncb = _attn_dims(S, NLB * LANES, d)

    def body(q_ref, kc_ref, vc_ref, sl_ref, o_ref, l_ref, kp_ref, vp_ref):
        n = pl.program_id(1)

        @pl.when(n == 0)
        def _():
            kp_ref[...] = jnp.zeros_like(kp_ref)
            vp_ref[...] = jnp.zeros_like(vp_ref)

        nd, valid = _band_terms(n, d)
        biases = [_head_biases(sl_ref[:, p * LANES:(p + 1) * LANES], nd, valid) for p in range(NP)]

        def group(items):
            scores, values = [], []
            for r, p in items:
                kc = jnp.concatenate([take(kp_ref, p, r), take(kc_ref, p, r)], axis=0).astype(BF16)
                values.append(jnp.concatenate([take(vp_ref, p, r), take(vc_ref, p, r)], axis=0).astype(BF16))
                scores.append(_dot_nt(_stack_heads((take(q_ref, p, r) * ATTN_SCALE).astype(BF16)), kc))
            probs = []
            for (r, p), s in zip(items, scores):
                s = s + biases[p]
                m = jnp.max(s, axis=-1, keepdims=True)
                e = jnp.exp(s - m)
                den = jnp.sum(e, axis=-1, keepdims=True)
                probs.append((e.astype(BF16), den, m + jnp.log(den)))
            for (r, p), vc, (e, den, lse) in zip(items, values, probs):
                rows = _res_rows(r, d)
                o_ref.at[p][rows, :] = _unstack_heads(_dot(e, vc) / den)
                l_ref.at[p][rows, :] = _unstack_heads(jnp.broadcast_to(lse, (2 * BAND, LANES)))

        _for_groups(d, NP, group)
        kp_ref[...] = kc_ref[...]
        vp_ref[...] = vc_ref[...]

    cur = lambda c, n: n
    out = pl.BlockSpec((NP, rb, LANES), lambda c, n: (c, n, 0))
    kv_block = spec(1, NP, cur).block_shape[1:]
    return pl.pallas_call(
        body, grid=(ncb, nb), name=f"attn_fwd_d{d}",
        in_specs=[spec(0, NP, cur), spec(1, NP, cur), spec(2, NP, cur),
                  pl.BlockSpec((1, NP * LANES), lambda c, n: (0, c))],
        out_specs=[out, out],
        out_shape=[jax.ShapeDtypeStruct((NLB, S, LANES), F32)] * 2,
        scratch_shapes=[pltpu.VMEM(kv_block, F32), pltpu.VMEM(kv_block, F32)],
        compiler_params=_cp(("parallel", "arbitrary")),
    )(src, src, src, slopes)


def _attn_merge(outs, lses, zf32):
    NLB, S, _ = outs[0].shape
    SEG = NLB * LANES
    tm = min(256, S)

    def body(o1, o2, o3, l1, l2, l3, zg_ref, o_ref, lse_ref, y_ref):
        a, b, c = l1[...], l2[...], l3[...]
        m = jnp.maximum(jnp.maximum(a, b), c)
        ea, eb, ec = jnp.exp(a - m), jnp.exp(b - m), jnp.exp(c - m)
        tot = ea + eb + ec
        o = (ea / tot) * o1[...] + (eb / tot) * o2[...] + (ec / tot) * o3[...]
        o_ref[...] = o
        lse_ref[...] = m + jnp.log(tot)
        zg = zg_ref[...]
        y = (o * (zg * _sigmoid(zg))).astype(BF16)
        for p in range(NLB):
            y_ref[:, p * LANES:(p + 1) * LANES] = y[p]

    blk = pl.BlockSpec((NLB, tm, LANES), lambda i: (0, i, 0))
    return pl.pallas_call(
        body, grid=(S // tm,), name="attn_merge",
        in_specs=[blk] * 6 + [pl.BlockSpec((None, NLB, tm, LANES), lambda i: (SEG_GATE_A, 0, i, 0))],
        out_specs=[blk, blk, pl.BlockSpec((tm, SEG), lambda i: (i, 0))],
        out_shape=[jax.ShapeDtypeStruct((NLB, S, LANES), F32), jax.ShapeDtypeStruct((NLB, S, LANES), F32),
                   jax.ShapeDtypeStruct((S, SEG), BF16)],
        compiler_params=_cp(("parallel",)),
    )(*outs, *lses, zf32)


def _attn_gate_bwd(dy, o, zf32):
    NP, S, _ = o.shape
    SEG = NP * LANES
    tm = min(256, S)

    def body(dy_ref, o_ref, zg_ref, do_ref, dl_ref, dzg_ref):
        r = lax.broadcasted_iota(jnp.int32, (LANES, LANES), 0) // ATTN_HEAD
        c = lax.broadcasted_iota(jnp.int32, (LANES, LANES), 1) // ATTN_HEAD
        same_head = (r == c).astype(BF16)
        for p in range(NP):
            cols = slice(p * LANES, (p + 1) * LANES)
            sg, dsg = _silu_and_grad(zg_ref[p])
            dyv = dy_ref[:, cols]
            ov = o_ref[p]
            do = dyv * sg
            do_ref[p] = do
            dzg_ref[:, cols] = (dyv * ov * dsg).astype(BF16)
            dl_ref[p] = _exact_dot_right(do * ov, same_head)

    blk = pl.BlockSpec((NP, tm, LANES), lambda i: (0, i, 0))
    return pl.pallas_call(
        body, grid=(S // tm,), name="attn_gate_bwd",
        in_specs=[pl.BlockSpec((tm, SEG), lambda i: (i, 1)), blk,
                  pl.BlockSpec((None, NP, tm, LANES), lambda i: (SEG_GATE_A, 0, i, 0))],
        out_specs=[blk, blk, pl.BlockSpec((None, tm, SEG), lambda i: (3, i, 0))],
        out_shape=[jax.ShapeDtypeStruct((NP, S, LANES), F32), jax.ShapeDtypeStruct((NP, S, LANES), F32),
                   jax.ShapeDtypeStruct((4, S, SEG), BF16)],
        compiler_params=_cp(("parallel",)),
    )(dy, o, zf32)


def _attn_bwd(qkv, slopes, do, lse, dl, d, acc, into):
    qkv, src, take, spec = _qkv_source(qkv, d)
    _, NLB, S, _ = qkv.shape
    SEG = NLB * LANES
    rb, NP, nb, ncb = _attn_dims(S, SEG, d)
    has_acc = acc is not None
    out_dtype = F32 if into is None else into.dtype
    assert into is None or d == 1

    def body(*refs):
        q_ref, kc_ref, vc_ref, sl_ref, do_ref, lse_ref, dl_ref = refs[:7]
        acc_ref = refs[7] if has_acc else None
        out_ref, cq, ck, cv, kp_ref, vp_ref = refs[-6:]
        n = pl.program_id(1)

        def emit(r, p, dq, dk, dv):
            rows = _res_rows(r, d)
            for t, val in enumerate((dq, dk, dv)):
                if has_acc:
                    val = val + acc_ref.at[t].at[p][rows, :]
                if into is None:
                    out_ref.at[t].at[p][rows, :] = val.astype(out_dtype)
                else:
                    out_ref.at[t][rows, p * LANES:(p + 1) * LANES] = val.astype(out_dtype)

        @pl.when(n == 0)
        def _():
            cq[...] = jnp.zeros_like(cq)
            ck[...] = jnp.zeros_like(ck)
            cv[...] = jnp.zeros_like(cv)
            kp_ref[...] = jnp.zeros_like(kp_ref)
            vp_ref[...] = jnp.zeros_like(vp_ref)

        @pl.when(n < nb)
        def _():
            nd, valid = _band_terms(n, d)
            biases = [_head_biases(sl_ref[:, p * LANES:(p + 1) * LANES], nd, valid) for p in range(NP)]

            def group(items):
                first = []
                for r, p in items:
                    rows = _res_rows(r, d)
                    kc = jnp.concatenate([take(kp_ref, p, r), take(kc_ref, p, r)], axis=0).astype(BF16)
                    vc = jnp.concatenate([take(vp_ref, p, r), take(vc_ref, p, r)], axis=0).astype(BF16)
                    qs = _stack_heads((take(q_ref, p, r) * ATTN_SCALE).astype(BF16))
                    dos = _stack_heads(do_ref.at[p][rows, :].astype(BF16))
                    first.append((kc, qs, dos, _dot_nt(qs, kc), _dot_nt(dos, vc)))
                second = []
                for (r, p), (kc, qs, dos, s, dp) in zip(items, first):
                    rows = _res_rows(r, d)
                    pr = jnp.exp(s + biases[p] - _stack_per_head(lse_ref.at[p][rows, :]))
                    ds = (pr * (dp - _stack_per_head(dl_ref.at[p][rows, :]))).astype(BF16)
                    second.append((kc, qs, dos, pr.astype(BF16), ds))
                for (r, p), (kc, qs, dos, pr, ds) in zip(items, second):
                    dq = _unstack_heads(_dot(ds, kc)) * ATTN_SCALE
                    dk = _dot_tn(ds, qs)
                    dv = _dot_tn(pr, dos)
                    emit(r, p, cq[r, p], ck[r, p] + dk[:BAND, :], cv[r, p] + dv[:BAND, :])
                    cq[r, p] = dq
                    ck[r, p] = dk[BAND:, :]
                    cv[r, p] = dv[BAND:, :]

            _for_groups(d, NP, group)
            kp_ref[...] = kc_ref[...]
            vp_ref[...] = vc_ref[...]

        @pl.when(n == nb)
        def _():
            def last(r):
                for p in range(NP):
                    emit(r, p, cq[r, p], ck[r, p], cv[r, p])
            _for_residues(d, last)

    cur = lambda c, n: (c, jnp.minimum(n, nb - 1), 0)
    lag = lambda c, n: (0, c, jnp.clip(n - 1, 0, nb - 1), 0)

    at = lambda c, n: jnp.minimum(n, nb - 1)
    in_specs = [spec(0, NP, at), spec(1, NP, at), spec(2, NP, at),
                pl.BlockSpec((1, NP * LANES), lambda c, n: (0, c))] + [pl.BlockSpec((NP, rb, LANES), cur)] * 3
    kv_block = in_specs[1].block_shape[1:]
    args = [src, src, src, slopes, do, lse, dl]
    aliases = {}
    if has_acc:
        in_specs.append(pl.BlockSpec((3, NP, rb, LANES), lag))
        args.append(acc)
        if into is None:
            aliases = {7: 0}
    if into is None:
        out_sds = jax.ShapeDtypeStruct((3, NLB, S, LANES), F32)
        out_spec = pl.BlockSpec((3, NP, rb, LANES), lag)
    else:
        in_specs.append(ANY)
        args.append(into)
        aliases = {len(args) - 1: 0}
        out_sds = jax.ShapeDtypeStruct(into.shape, into.dtype)
        out_spec = pl.BlockSpec((3, rb, NP * LANES), lambda c, n: (0, jnp.clip(n - 1, 0, nb - 1), c))
    return pl.pallas_call(
        body, grid=(ncb, nb + 1), name=f"attn_bwd_d{d}",
        in_specs=in_specs, out_specs=out_spec, out_shape=out_sds,
        scratch_shapes=[pltpu.VMEM((d, NP, BAND, LANES), F32)] * 3 + [pltpu.VMEM(kv_block, F32)] * 2,
        input_output_aliases=aliases,
        compiler_params=_cp(("parallel", "arbitrary")),
    )(*args)


def _adamw(w, g, m, v, name):
    R, C = w.shape
    tr = R if R <= 256 else 256
    assert R % tr == 0

    def body(w_ref, g_ref, m_ref, v_ref, d_ref, nm_ref, nv_ref, go_ref):
        g = g_ref[...]
        nm = ADAM_B1 * m_ref[...] + (1.0 - ADAM_B1) * g
        nv = ADAM_B2 * v_ref[...] + (1.0 - ADAM_B2) * (g * g)
        m_hat = nm / (1.0 - ADAM_B1 ** ADAM_STEP)
        v_hat = nv / (1.0 - ADAM_B2 ** ADAM_STEP)
        d_ref[...] = -ADAM_LR * (m_hat / (jnp.sqrt(v_hat) + ADAM_EPS) + ADAM_WD * w_ref[...])
        nm_ref[...] = nm
        nv_ref[...] = nv
        go_ref[...] = g

    blk = pl.BlockSpec((tr, C), lambda i: (i, 0))
    sds = jax.ShapeDtypeStruct((R, C), F32)
    return pl.pallas_call(
        body, grid=(R // tr,), name=name, in_specs=[blk] * 4, out_specs=[blk] * 4, out_shape=[sds] * 4,
        compiler_params=_cp(("parallel",)),
    )(w, g, m, v)


def _coords():
    return lax.axis_index("x"), lax.axis_index("y"), lax.axis_index("c")


def _other_chips(x, y):
    return [(1 - x, y), (x, 1 - y), (1 - x, 1 - y)]


ANY = pl.BlockSpec(memory_space=pl.ANY)


def _cast_into_slot(w, where, name):
    R, C = w.shape
    tr = min(256, R)

    def body(where_ref, w_ref, o_ref):
        o_ref[...] = w_ref[...].astype(BF16)

    grid_spec = pltpu.PrefetchScalarGridSpec(
        num_scalar_prefetch=1, grid=(R // tr,),
        in_specs=[pl.BlockSpec((tr, C), lambda i, w: (i, 0))],
        out_specs=pl.BlockSpec((None, tr, C), lambda i, w: (w[1], i, 0)))
    return pl.pallas_call(
        body, grid_spec=grid_spec, name=name, out_shape=jax.ShapeDtypeStruct((4, R, C), BF16),
        compiler_params=_cp(("parallel",)),
    )(where, w)


def _pair_sum(g, sib, where, name):
    _, n2, C = g.shape
    N = n2 // 2
    tr = min(256, N)
    nt = N // tr

    def body(where_ref, g_ref, s_ref, qb_ref, own_ref):
        q = pl.program_id(1)
        tot = g_ref[...] + s_ref[...]
        qb_ref[...] = tot.astype(BF16)

        @pl.when(q == where_ref[1])
        def _():
            own_ref[...] = tot

    grid_spec = pltpu.PrefetchScalarGridSpec(
        num_scalar_prefetch=1, grid=(nt, 4),
        in_specs=[pl.BlockSpec((None, tr, C), lambda i, q, w: (q, w[0] * nt + i, 0)),
                  pl.BlockSpec((None, tr, C), lambda i, q, w: (q, i, 0))],
        out_specs=[pl.BlockSpec((None, tr, C), lambda i, q, w: (q, i, 0)),
                   pl.BlockSpec((tr, C), lambda i, q, w: (i, 0))])
    return pl.pallas_call(
        body, grid_spec=grid_spec, name=name,
        out_shape=[jax.ShapeDtypeStruct((4, N, C), BF16), jax.ShapeDtypeStruct((N, C), F32)],
        compiler_params=_cp(("parallel", "arbitrary")),
    )(where, g, sib)


HBM = pl.BlockSpec(memory_space=pltpu.HBM)
SEM = pl.BlockSpec(memory_space=pltpu.SEMAPHORE)


def _in_hbm(a):
    return pltpu.with_memory_space_constraint(a, pltpu.HBM)


def _split_start(name, copies, arrays, n_sems, after=None):
    n = len(arrays)

    def body(*refs):
        for cp in copies(refs[:n], refs[-n - 3], refs[-n - 2]):
            cp.start()
        refs[-1][...] = jnp.zeros_like(refs[-1])

    ordered = () if after is None else (after,)
    outs = pl.pallas_call(
        body, name=name,
        out_shape=(pltpu.SemaphoreType.DMA((n_sems,)), pltpu.SemaphoreType.DMA((n_sems,)),
                   *[pltpu.HBM(a.shape, a.dtype) for a in arrays], jax.ShapeDtypeStruct((8, LANES), F32)),
        in_specs=(HBM,) * n + (ANY,) * len(ordered),
        out_specs=(SEM, SEM) + (HBM,) * n + (pl.BlockSpec(memory_space=pltpu.VMEM),),
        input_output_aliases={i: 2 + i for i in range(n)},
        compiler_params=pltpu.CompilerParams(has_side_effects=pltpu.SideEffectType.DATAFLOW_SIDE_EFFECTING),
    )(*[_in_hbm(a) for a in arrays], *ordered)
    return outs[0], outs[1], list(outs[2:2 + n]), outs[-1]


def _split_wait(name, copies, send_sems, recv_sems, arrays, after):
    n = len(arrays)

    def body(*refs):
        for cp in copies(refs[:n], refs[n], refs[n + 1]):
            cp.wait_send()
            cp.wait_recv()

    outs = pl.pallas_call(
        body, name=name,
        out_shape=tuple(pltpu.HBM(a.shape, a.dtype) for a in arrays),
        in_specs=(HBM,) * n + (SEM, SEM, ANY), out_specs=(HBM,) * n,
        input_output_aliases={i: i for i in range(n)},
        compiler_params=pltpu.CompilerParams(has_side_effects=pltpu.SideEffectType.DATAFLOW_SIDE_EFFECTING),
    )(*arrays, send_sems, recv_sems, after)
    return list(outs)


def _remote(src, dst, sems, k, to):
    send_sems, recv_sems = sems
    return pltpu.make_async_remote_copy(src_ref=src, dst_ref=dst, send_sem=send_sems.at[k], recv_sem=recv_sems.at[k],
                                        device_id=to, device_id_type=MESH)


def _chip_at(x, y, rel):
    px = 1 - x if rel & 2 else x
    py = 1 - y if rel & 1 else y
    return px, py, 2 * px + py


def _gather_in_copies(rels):
    def copies(refs, send_sems, recv_sems):
        (w,) = refs
        x, y, c = _coords()
        seg = w.shape[2] // 2
        mine = w.at[2 * x + y, :, pl.ds(c * seg, seg)]
        return [_remote(mine, mine, (send_sems, recv_sems), k, _chip_at(x, y, rel)[:2] + (c,))
                for k, rel in enumerate(rels)]
    return copies


def _gather_out_copies(refs, send_sems, recv_sems):
    (w,) = refs
    x, y, c = _coords()
    mine = w.at[2 * x + y]
    return [_remote(mine, mine, (send_sems, recv_sems), k, (px, py, c)) for k, (px, py) in enumerate(_other_chips(x, y))]


def _swap_copies(refs, send_sems, recv_sems):
    gi, go, si, so = refs
    x, y, c = _coords()
    cps = []
    for a, (src, dst) in enumerate(((gi, si), (go, so))):
        nr = dst.shape[1]
        cps.append(_remote(src.at[:, pl.ds((1 - c) * nr, nr), :], dst, (send_sems, recv_sems), a, (x, y, 1 - c)))
    return cps


def _scatter_copies(refs, send_sems, recv_sems):
    qi, qo, ri, ro = refs
    x, y, c = _coords()
    cps = []
    for k, (px, py) in enumerate(_other_chips(x, y)):
        for a, (src, dst) in enumerate(((qi, ri), (qo, ro))):
            cps.append(_remote(src.at[2 * px + py], dst.at[k], (send_sems, recv_sems), 2 * k + a, (px, py, c)))
    return cps


def _forward_copies(rels):
    def copies(refs, send_sems, recv_sems):
        (w,) = refs
        x, y, c = _coords()
        seg = w.shape[2] // 2
        cps = []
        for k, rel in enumerate(rels):
            got = w.at[_chip_at(x, y, rel)[2], :, pl.ds(c * seg, seg)]
            cps.append(_remote(got, got, (send_sems, recv_sems), k, (x, y, 1 - c)))
        return cps
    return copies


def _chip_sum(own, got, where, name):
    N, C = own.shape
    tr = min(256, N)
    nt = N // tr

    def body(where_ref, own_ref, got_ref, o_ref):
        t = own_ref[...]
        for k in range(3):
            t = t + got_ref[k].astype(F32)
        o_ref[...] = t

    grid_spec = pltpu.PrefetchScalarGridSpec(
        num_scalar_prefetch=1, grid=(nt,),
        in_specs=[pl.BlockSpec((tr, C), lambda i, w: (i, 0)), pl.BlockSpec((3, tr, C), lambda i, w: (0, i, 0))],
        out_specs=pl.BlockSpec((tr, C), lambda i, w: (w[0] * nt + i, 0)))
    return pl.pallas_call(
        body, grid_spec=grid_spec, name=name, out_shape=jax.ShapeDtypeStruct((2 * N, C), F32),
        compiler_params=_cp(("parallel",)),
    )(where, own, got)


def _join_copies(refs, send_sems, recv_sems):
    x, y, c = _coords()
    cps = []
    for a, ref in enumerate(refs):
        nr = ref.shape[0] // 2
        mine = ref.at[pl.ds(c * nr, nr), :]
        cps.append(_remote(mine, mine, (send_sems, recv_sems), a, (x, y, 1 - c)))
    return cps


def _all_reduce_small(part, token):
    R, C = part.shape

    def body(p_ref, _, o_ref, slots, send_sems, recv_sems):
        x, y, c = _coords()
        me = 4 * x + 2 * y + c
        slots[me] = p_ref[...]
        cps = []
        for k in range(1, 8):
            fx, fy, fc = (k >> 2) & 1, (k >> 1) & 1, k & 1
            peer = (1 - x if fx else x, 1 - y if fy else y, 1 - c if fc else c)
            cp = pltpu.make_async_remote_copy(src_ref=p_ref, dst_ref=slots.at[me], send_sem=send_sems.at[k - 1],
                                              recv_sem=recv_sems.at[k - 1], device_id=peer, device_id_type=MESH)
            cp.start()
            cps.append(cp)
        for cp in cps:
            cp.wait()
        t = slots[0]
        for k in range(1, 8):
            t = t + slots[k]
        o_ref[...] = t

    vm = pl.BlockSpec(memory_space=pltpu.VMEM)
    return pl.pallas_call(
        body, name="all_reduce_small", in_specs=[vm, vm], out_specs=vm,
        out_shape=jax.ShapeDtypeStruct((R, C), F32),
        scratch_shapes=[pltpu.VMEM((8, R, C), F32), pltpu.SemaphoreType.DMA((7,)), pltpu.SemaphoreType.DMA((7,))],
    )(part, token)


def _mixers_forward(z, lb_logits, hgrn_gnorm):
    slopes = _alibi_slopes(z[0].shape[1] * LANES)
    yh, states = _hgrn_fwd(z[0], lb_logits, hgrn_gnorm)
    outs, lses = [], []
    for d in DILATIONS:
        o, l = _attn_fwd(z, slopes, d)
        outs.append(o)
        lses.append(l)
    o_attn, lse, ya = _attn_merge(outs, lses, z[0])
    return yh, ya, (states, o_attn, lse, slopes)


def _backward_to_dz(z, kept, lb_logits, hgrn_gnorm, yh, ya, w_out_all, x2, tgt, fgain, h):
    states, o_attn, lse, slopes = kept
    dout, doutb, loss, dfg = _out_proj_loss(yh, ya, w_out_all, x2, tgt, fgain)
    dy = _dy_proj(doutb, w_out_all)
    g_w_out = _grad_w_out(yh, ya, doutb)
    dzh, dlogits, dgn = _hgrn_bwd(z[0], lb_logits, hgrn_gnorm, states, dy)
    do, dl, dza = _attn_gate_bwd(dy, o_attn, z[0])
    acc = None
    order = sorted(DILATIONS, reverse=True)
    for d in order[:-1]:
        acc = _attn_bwd(z, slopes, do, lse, dl, d, acc, None)
    dza = _attn_bwd(z, slopes, do, lse, dl, order[-1], acc, dza)
    sources = [dzh, dza]
    g_w_in = _grad_w_in(h, sources)
    return loss, dfg, dlogits, dgn, g_w_out, g_w_in, sources, dout


def _grad_x_half(sources, w_all, x2, rinv, norm_gain, dout, token, part, gx_prev):
    dh = _dh_proj(sources, w_all, token, part, f"dh_proj_{part}")
    return _rms_bwd(dh, x2, rinv, norm_gain, dout, part, gx_prev, f"rms_bwd_{part}")


def _local_step(x2, tgt, norm_gain, w_all, lb_logits, hgrn_gnorm, w_out_all, fgain):
    token = jnp.zeros((8, LANES), F32)
    where = jnp.zeros((2,), jnp.int32)
    h, rinv = _rms_fwd(x2, norm_gain, token)
    z = _in_proj(h, w_all, where, [(rel, half) for rel in range(4) for half in range(2)], None, token, "in_proj_all")
    yh, ya, kept = _mixers_forward(z, lb_logits, hgrn_gnorm)
    loss, dfg, dlogits, dgn, g_w_out, g_w_in, sources, dout = _backward_to_dz(
        z, kept, lb_logits, hgrn_gnorm, yh, ya, w_out_all, x2, tgt, fgain, h)
    gx, dg0 = _grad_x_half(sources, w_all, x2, rinv, norm_gain, dout, token, 0, None)
    gx, dg1 = _grad_x_half(sources, w_all, x2, rinv, norm_gain, dout, token, 1, gx)
    return loss, gx, dg0 + dg1, g_w_in, dlogits, dgn, g_w_out, dfg


def _pack_small(D, loss, dgain, dlogits, dgn, dfg):
    def row(v):
        v = v.reshape(1, -1)
        return jnp.pad(v, ((0, 0), (0, D - v.shape[1])))
    rows = [row(dgain), row(dfg), row(dlogits[0]), row(dlogits[1]), row(jnp.sum(dgn, axis=0)), row(loss)]
    rows += [jnp.zeros((1, D), F32)] * (8 - len(rows))
    return jnp.concatenate(rows, axis=0)


def kernel(x, norm_gain, w_in, lb_logits, hgrn_gnorm, w_out, final_gain, loss_target, m_norm_gain, m_w_in, m_lb_logits, m_hgrn_gnorm, m_w_out, m_final_gain, v_norm_gain, v_w_in, v_lb_logits, v_hgrn_gnorm, v_w_out, v_final_gain):
    _, S, D = x.shape
    SEG = w_in.shape[2] // 2
    x2 = x[0]
    tgt = loss_target[0]
    fgain = final_gain.reshape(1, D)
    where = jnp.stack([lax.axis_index("c"), 2 * lax.axis_index("x") + lax.axis_index("y")]).astype(jnp.int32)

    wia = _cast_into_slot(w_in[0], where, "cast_w_in")
    woa = _cast_into_slot(w_out[0], where, "cast_w_out")
    near, far = (2, 1), (3,)
    ga = _split_start("gather_near_start", _gather_in_copies(near), [wia], 2)
    h, rinv = _rms_fwd(x2, norm_gain, ga[3])
    z = _in_proj(h, ga[2][0], where, [(0, 0), (0, 1)], None, ga[3], "in_proj_own")
    (wia,) = _split_wait("gather_near_wait", _gather_in_copies(near), ga[0], ga[1], ga[2], z[0])
    gb = _split_start("gather_far_start", _gather_in_copies(far), [wia], 1)
    fa = _split_start("forward_near_start", _forward_copies(near), gb[2], 2, after=gb[3])
    z = _in_proj(h, fa[2][0], where, [(2, "mine"), (1, "mine")], z, fa[3], "in_proj_near")
    (wia,) = _split_wait("forward_near_wait", _forward_copies(near), fa[0], fa[1], fa[2], z[0])
    (wia,) = _split_wait("gather_far_wait", _gather_in_copies(far), gb[0], gb[1], [wia], z[0])
    out_sems = _split_start("gather_out_start", _gather_out_copies, [woa], 3, after=wia)
    fb = _split_start("forward_far_start", _forward_copies(far), [wia], 1, after=out_sems[3])
    z = _in_proj(h, fb[2][0], where, [(3, "mine"), (2, "sibling"), (1, "sibling")], z, fb[3], "in_proj_far")
    (wia,) = _split_wait("forward_far_wait", _forward_copies(far), fb[0], fb[1], fb[2], z[0])
    z = _in_proj(h, wia, where, [(3, "sibling")], z, fb[3], "in_proj_last")
    yh, ya, kept = _mixers_forward(z, lb_logits, hgrn_gnorm)
    (woa,) = _split_wait("gather_out_wait", _gather_out_copies, out_sems[0], out_sems[1], out_sems[2], ya)
    w_out_all = woa.reshape(2 * SEG, D)

    loss, dfg, dlogits, dgn, g_w_out, g_w_in, sources, dout = _backward_to_dz(
        z, kept, lb_logits, hgrn_gnorm, yh, ya, w_out_all, x2, tgt, fgain, h)

    sib_i = lax.empty((4, g_w_in.shape[1] // 2, g_w_in.shape[2]), F32)
    sib_o = lax.empty((4, g_w_out.shape[1] // 2, g_w_out.shape[2]), F32)
    sems = _split_start("swap_start", _swap_copies, [g_w_in, g_w_out, sib_i, sib_o], 2)
    grad_x, dg0 = _grad_x_half(sources, wia, x2, rinv, norm_gain, dout, sems[3], 0, None)
    g_w_in, g_w_out, sib_i, sib_o = _split_wait("swap_wait", _swap_copies, sems[0], sems[1], sems[2], grad_x)
    qi, own_i = _pair_sum(g_w_in, sib_i, where, "pair_sum_w_in")
    qo, own_o = _pair_sum(g_w_out, sib_o, where, "pair_sum_w_out")
    ri = lax.empty((3,) + qi.shape[1:], BF16)
    ro = lax.empty((3,) + qo.shape[1:], BF16)
    sems = _split_start("scatter_start", _scatter_copies, [qi, qo, ri, ro], 6)
    grad_x, dg1 = _grad_x_half(sources, wia, x2, rinv, norm_gain, dout, sems[3], 1, grad_x)
    _, _, got_i, got_o = _split_wait("scatter_wait", _scatter_copies, sems[0], sems[1], sems[2], grad_x)
    jn = _split_start("join_start", _join_copies, [_chip_sum(own_i, got_i, where, "chip_sum_w_in"),
                                                   _chip_sum(own_o, got_o, where, "chip_sum_w_out")], 2)
    small = _all_reduce_small(_pack_small(D, loss, dg0 + dg1, dlogits, dgn, dfg), jn[3])
    loss_sum = small[5, 0]
    d_ng, m_ng, v_ng, grad_norm_gain = _adamw(norm_gain, small[0:1, :], m_norm_gain, v_norm_gain, "adamw_norm_gain")
    d_lb, m_lb, v_lb, grad_lb_logits = _adamw(lb_logits, small[2:4, :SEG], m_lb_logits, v_lb_logits, "adamw_lb_logits")
    d_gn, m_gn, v_gn, grad_hgrn_gnorm = _adamw(hgrn_gnorm, small[4:5, :HGRN_HEAD], m_hgrn_gnorm, v_hgrn_gnorm,
                                               "adamw_hgrn_gnorm")
    d_fg, m_fg, v_fg, grad_final_gain = _adamw(fgain, small[1:2, :], m_final_gain.reshape(1, D),
                                               v_final_gain.reshape(1, D), "adamw_final_gain")
    g_w_in, g_w_out = _split_wait("join_wait", _join_copies, jn[0], jn[1], jn[2], d_fg)
    d_wi, m_wi, v_wi, grad_w_in = _adamw(w_in[0], g_w_in, m_w_in[0], v_w_in[0], "adamw_w_in")
    d_wo, m_wo, v_wo, grad_w_out = _adamw(w_out[0], g_w_out, m_w_out[0], v_w_out[0], "adamw_w_out")

    return (loss_sum, grad_x[None],
            grad_norm_gain, grad_w_in[None], grad_lb_logits, grad_hgrn_gnorm, grad_w_out[None], grad_final_gain[0],
            d_ng, d_wi[None], d_lb, d_gn, d_wo[None], d_fg[0],
            m_ng, m_wi[None], m_lb, m_gn, m_wo[None], m_fg[0],
            v_ng, v_wi[None], v_lb, v_gn, v_wo[None], v_fg[0])
```

```python
import jax
import jax.numpy as jnp
import numpy as np
from jax import lax
from jax.experimental import pallas as pl
from jax.experimental.pallas import tpu as pltpu

F32 = jnp.float32
BF16 = jnp.bfloat16
MESH = pl.DeviceIdType.MESH

NORM_EPS = 1e-6
HGRN_HEAD = 128
HGRN_CHUNK = 64
HGRN_TILE = 128
HGRN_STEP_FWD = (1024, 4)
HGRN_STEP_BWD = (2048, 1)
ATTN_HEAD = 64
LANES = 128
BAND = 128
DILATIONS = (1, 4, 16)
DEINTERLEAVE = 16
ATTN_SCALE = ATTN_HEAD ** -0.5
assert ATTN_SCALE == 0.125
ATTN_BLOCK_ELEMS = BAND * 2048
ATTN_UNROLL = 4
SEG_QKV = 4
SEG_GATE_A = 7
NEG = -1e30

ADAM_LR = 0.001
ADAM_B1 = 0.9
ADAM_B2 = 0.999
ADAM_EPS = 1e-08
ADAM_WD = 0.01
ADAM_STEP = 10

MIB = 1024 * 1024


def _cp(semantics=None, vmem_mib=48):
    return pltpu.CompilerParams(dimension_semantics=semantics, vmem_limit_bytes=vmem_mib * MIB)


def _dot(a, b):
    return jnp.dot(a, b, preferred_element_type=F32)


def _dot_nt(a, b):
    return lax.dot_general(a, b, (((1,), (1,)), ((), ())), preferred_element_type=F32)


def _dot_tn(a, b):
    return lax.dot_general(a, b, (((0,), (0,)), ((), ())), preferred_element_type=F32)


def _split3(x):
    hi = x.astype(BF16)
    r1 = x - hi.astype(F32)
    mid = r1.astype(BF16)
    lo = (r1 - mid.astype(F32)).astype(BF16)
    return hi, mid, lo


def _exact_dot(t_bf16, x):
    hi, mid, lo = _split3(x)
    return _dot(t_bf16, hi) + _dot(t_bf16, mid) + _dot(t_bf16, lo)


def _exact_dot_right(x, t_bf16):
    hi, mid, lo = _split3(x)
    return _dot(hi, t_bf16) + _dot(mid, t_bf16) + _dot(lo, t_bf16)


def _sigmoid(z):
    return jax.nn.sigmoid(z)


def _silu_and_grad(z):
    s = _sigmoid(z)
    return z * s, s * (1.0 + z * (1.0 - s))


def _seg_select(j, values):
    out = values[0]
    for t, v in enumerate(values[1:], 1):
        out = jnp.where(j == t, v, out)
    return out


def _rms_fwd(x2, gain, token):
    S, D = x2.shape
    tm = min(512, S)

    def body(x_ref, g_ref, _, h_ref, r_ref):
        x = x_ref[...]
        r = lax.rsqrt(jnp.mean(x * x, axis=-1, keepdims=True) + NORM_EPS)
        h_ref[...] = ((x * r) * g_ref[...]).astype(BF16)
        r_ref[...] = r

    return pl.pallas_call(
        body, grid=(S // tm,), name="rms_fwd",
        in_specs=[pl.BlockSpec((tm, D), lambda i: (i, 0)), pl.BlockSpec((1, D), lambda i: (0, 0)),
                  pl.BlockSpec(token.shape, lambda i: (0, 0))],
        out_specs=[pl.BlockSpec((tm, D), lambda i: (i, 0)), pl.BlockSpec((tm, 1), lambda i: (i, 0))],
        out_shape=[jax.ShapeDtypeStruct((S, D), BF16), jax.ShapeDtypeStruct((S, 1), F32)],
        compiler_params=_cp(("parallel",)),
    )(x2, gain, token)


def _in_proj(h, w_all, where, segs, z_prev, token, name):
    S, D = h.shape
    SEG = w_all.shape[2] // 2
    NLB = SEG // LANES
    tm = min(512, S)
    count = len(segs)
    DI = DEINTERLEAVE
    tu = tm // DI

    def is_qkv(seg):
        return (seg >= SEG_QKV) & (seg < SEG_QKV + 3)

    def seg_of(j, w):
        halves = {0: 0, 1: 1, "mine": w[0], "sibling": 1 - w[0]}
        cands = [2 * jnp.bitwise_xor(w[1], rel) + halves[half] for rel, half in segs]
        keys = [is_qkv(s).astype(jnp.int32) for s in cands]
        out = cands[0]
        for k in range(count):
            pos = (sum(jnp.where(keys[t] < keys[k], 1, 0) for t in range(count))
                   + sum(jnp.where(keys[t] == keys[k], 1, 0) for t in range(k)))
            out = jnp.where(pos == j, cands[k], out)
        return out

    def body(*refs):
        where_ref, h_ref, w_ref = refs[:3]
        o_ref, o16_ref = refs[-2:]
        res = _dot(h_ref[...], w_ref[...])
        for p in range(NLB):
            o_ref[p] = res[:, p * LANES:(p + 1) * LANES]

        @pl.when(is_qkv(seg_of(pl.program_id(0), where_ref)))
        def _():
            for p in range(NLB):
                for r in range(DI):
                    o16_ref[p, r] = o_ref.at[p][pl.ds(r, tu, stride=DI), :]

    def z16_map(j, i, w):
        seg = seg_of(j, w)
        return (jnp.where(is_qkv(seg), seg - SEG_QKV, 3), 0, 0, jnp.where(is_qkv(seg), i, 0), 0)

    in_specs = [pl.BlockSpec((tm, D), lambda j, i, w: (i, 0)),
                pl.BlockSpec((None, D, SEG), lambda j, i, w: (seg_of(j, w) // 2, 0, seg_of(j, w) % 2)),
                pl.BlockSpec(token.shape, lambda j, i, w: (0, 0))]
    args = [where, h, w_all, token]
    aliases = {}
    if z_prev is not None:
        in_specs += [ANY, ANY]
        args += list(z_prev)
        aliases = {4: 0, 5: 1}
    grid_spec = pltpu.PrefetchScalarGridSpec(
        num_scalar_prefetch=1, grid=(count, S // tm), in_specs=in_specs,
        out_specs=[pl.BlockSpec((None, NLB, tm, LANES), lambda j, i, w: (seg_of(j, w), 0, i, 0)),
                   pl.BlockSpec((None, NLB, DI, tu, LANES), z16_map)])
    return pl.pallas_call(
        body, grid_spec=grid_spec, name=name,
        out_shape=[jax.ShapeDtypeStruct((8, NLB, S, LANES), F32),
                   jax.ShapeDtypeStruct((4, NLB, DI, S // DI, LANES), F32)],
        input_output_aliases=aliases, compiler_params=_cp(("parallel", "parallel")),
    )(*args)


def _out_proj_loss(yh, ya, w_out, x2, tgt, fgain):
    S, D = x2.shape
    SEG = yh.shape[1]
    tm = min(256, S)
    parts = 2

    def body(yh_ref, ya_ref, w_ref, x_ref, t_ref, fg_ref, dout_ref, doutb_ref, loss_ref, dfg_ref):
        i = pl.program_id(0)

        @pl.when(i == 0)
        def _():
            loss_ref[...] = jnp.zeros_like(loss_ref)
            dfg_ref[...] = jnp.zeros_like(dfg_ref)

        fg = fg_ref[...]
        loss = jnp.zeros((1, 1), F32)
        dfg = jnp.zeros((1, D), F32)
        for rows in [pl.ds(p * (tm // parts), tm // parts) for p in range(parts)]:
            out = (x_ref[rows, :] + _dot(yh_ref[rows, :], w_ref[pl.ds(0, SEG), :])
                   + _dot(ya_ref[rows, :], w_ref[pl.ds(SEG, SEG), :]))
            r = lax.rsqrt(jnp.mean(out * out, axis=-1, keepdims=True) + NORM_EPS)
            n = out * r
            err = n * fg - t_ref[rows, :]
            loss = loss + 0.5 * jnp.sum(jnp.mean(err * err, axis=-1, keepdims=True), axis=0, keepdims=True)
            dy = err * (1.0 / D)
            dfg = dfg + jnp.sum(dy * n, axis=0, keepdims=True)
            dn = dy * fg
            dout = r * (dn - n * jnp.mean(dn * n, axis=-1, keepdims=True))
            dout_ref[rows, :] = dout
            doutb_ref[rows, :] = dout.astype(BF16)
        loss_ref[...] += loss
        dfg_ref[...] += dfg

    row = lambda i: (i, 0)
    fix = lambda i: (0, 0)
    return pl.pallas_call(
        body, grid=(S // tm,), name="out_proj_loss",
        in_specs=[pl.BlockSpec((tm, SEG), row), pl.BlockSpec((tm, SEG), row), pl.BlockSpec((2 * SEG, D), fix),
                  pl.BlockSpec((tm, D), row), pl.BlockSpec((tm, D), row), pl.BlockSpec((1, D), fix)],
        out_specs=[pl.BlockSpec((tm, D), row), pl.BlockSpec((tm, D), row), pl.BlockSpec((1, 1), fix),
                   pl.BlockSpec((1, D), fix)],
        out_shape=[jax.ShapeDtypeStruct((S, D), F32), jax.ShapeDtypeStruct((S, D), BF16),
                   jax.ShapeDtypeStruct((1, 1), F32), jax.ShapeDtypeStruct((1, D), F32)],
        compiler_params=_cp(("arbitrary",)),
    )(yh, ya, w_out, x2, tgt, fgain)


def _dy_proj(doutb, w_out):
    S, D = doutb.shape
    K = w_out.shape[0]
    tm = min(512, S)

    def body(d_ref, w_ref, o_ref):
        o_ref[...] = _dot_nt(d_ref[...], w_ref[...])

    return pl.pallas_call(
        body, grid=(S // tm,), name="dy_proj",
        in_specs=[pl.BlockSpec((tm, D), lambda i: (i, 0)), pl.BlockSpec((K, D), lambda i: (0, 0))],
        out_specs=pl.BlockSpec((tm, K), lambda i: (i, 0)),
        out_shape=jax.ShapeDtypeStruct((S, K), F32),
        compiler_params=_cp(("parallel",)),
    )(doutb, w_out)


def _grad_w_out(yh, ya, doutb):
    S, SEG = yh.shape
    D = doutb.shape[1]
    R = (2 * SEG) // 4
    nb_half = SEG // R
    tk = min(1024, S)

    def body(yh_ref, ya_ref, d_ref, o_ref):
        q = pl.program_id(0)
        k = pl.program_id(1)

        @pl.when(k == 0)
        def _():
            o_ref[...] = jnp.zeros_like(o_ref)

        @pl.when(q < nb_half)
        def _():
            o_ref[...] += _dot_tn(yh_ref[...], d_ref[...])

        @pl.when(q >= nb_half)
        def _():
            o_ref[...] += _dot_tn(ya_ref[...], d_ref[...])

    return pl.pallas_call(
        body, grid=(4, S // tk), name="grad_w_out",
        in_specs=[pl.BlockSpec((tk, R), lambda q, k: (k, jnp.minimum(q, nb_half - 1))),
                  pl.BlockSpec((tk, R), lambda q, k: (k, jnp.maximum(q - nb_half, 0))),
                  pl.BlockSpec((tk, D), lambda q, k: (k, 0))],
        out_specs=pl.BlockSpec((None, R, D), lambda q, k: (q, 0, 0)),
        out_shape=jax.ShapeDtypeStruct((4, R, D), F32),
        compiler_params=_cp(("parallel", "arbitrary")),
    )(yh, ya, doutb)


def _dz_sources(sources):
    counts = [s.shape[0] for s in sources]
    starts = [sum(counts[:k]) for k in range(len(counts))]
    assert sum(counts) == 8
    return counts, starts


def _row_part(S, part, tile):
    first = max(512, (S * 3 // 8) // 512 * 512)
    rows = first if part == 0 else S - first
    assert rows % tile == 0 and first % tile == 0
    return (0 if part == 0 else first // tile), rows // tile, rows


def _dh_proj(sources, w_all, token, part, name):
    S = sources[0].shape[1]
    D = w_all.shape[1]
    SEG = w_all.shape[2] // 2
    counts, starts = _dz_sources(sources)
    assert all(c % 2 == 0 for c in counts)
    ns = len(sources)
    tm = 1024 if all(_row_part(S, p, 1)[2] % 1024 == 0 for p in (0, 1)) else 512
    t0, nt, nrows = _row_part(S, part, tm)

    def body(*refs):
        src = refs[:ns]
        w_ref, _, o_ref = refs[ns:]
        j = pl.program_id(1)

        @pl.when(j == 0)
        def _():
            o_ref[...] = jnp.zeros_like(o_ref)

        for k in range(ns):
            @pl.when((2 * j >= starts[k]) & (2 * j < starts[k] + counts[k]))
            def _(k=k):
                o_ref[...] += (_dot_nt(src[k][0], w_ref[:, pl.ds(0, SEG)])
                               + _dot_nt(src[k][1], w_ref[:, pl.ds(SEG, SEG)]))

    def src_spec(k):
        return pl.BlockSpec((2, tm, SEG),
                            lambda i, j: (jnp.clip(j - starts[k] // 2, 0, counts[k] // 2 - 1), t0 + i, 0))

    return pl.pallas_call(
        body, grid=(nt, 4), name=name,
        in_specs=[src_spec(k) for k in range(ns)] + [pl.BlockSpec((None, D, 2 * SEG), lambda i, j: (j, 0, 0)),
                                                     pl.BlockSpec(token.shape, lambda i, j: (0, 0))],
        out_specs=pl.BlockSpec((tm, D), lambda i, j: (i, 0)),
        out_shape=jax.ShapeDtypeStruct((nrows, D), F32),
        compiler_params=_cp(("parallel", "arbitrary"), 48 if tm == 512 else 60),
    )(*sources, w_all, token)


def _rms_bwd(dh, x2, rinv, gain, dout, part, gx_prev, name):
    S, D = x2.shape
    tm = 256
    t0, nt, _ = _row_part(S, part, tm)

    def body(dh_ref, x_ref, r_ref, g_ref, dout_ref, *rest):
        gx_ref, dg_ref = rest[-2:]

        @pl.when(pl.program_id(0) == 0)
        def _():
            dg_ref[...] = jnp.zeros_like(dg_ref)

        dh = dh_ref[...]
        r = r_ref[...]
        xhat = x_ref[...] * r
        dg_ref[...] += jnp.sum(dh * xhat, axis=0, keepdims=True)
        dxn = dh * g_ref[...]
        gx_ref[...] = dout_ref[...] + r * (dxn - xhat * jnp.mean(dxn * xhat, axis=-1, keepdims=True))

    row = lambda i: (t0 + i, 0)
    fix = lambda i: (0, 0)
    in_specs = [pl.BlockSpec((tm, D), lambda i: (i, 0)), pl.BlockSpec((tm, D), row), pl.BlockSpec((tm, 1), row),
                pl.BlockSpec((1, D), fix), pl.BlockSpec((tm, D), row)]
    args = [dh, x2, rinv, gain, dout]
    aliases = {}
    if gx_prev is not None:
        in_specs.append(ANY)
        args.append(gx_prev)
        aliases = {5: 0}
    return pl.pallas_call(
        body, grid=(nt,), name=name, in_specs=in_specs,
        out_specs=[pl.BlockSpec((tm, D), row), pl.BlockSpec((1, D), fix)],
        out_shape=[jax.ShapeDtypeStruct((S, D), F32), jax.ShapeDtypeStruct((1, D), F32)],
        input_output_aliases=aliases, compiler_params=_cp(("arbitrary",)),
    )(*args)


def _grad_w_in(h, sources):
    S, D = h.shape
    SEG = sources[0].shape[2]
    counts, starts = _dz_sources(sources)
    ns = len(sources)
    tk = min(2048, S)

    def body(*refs):
        h_ref = refs[0]
        src = refs[1:1 + ns]
        o_ref = refs[1 + ns]
        j = pl.program_id(0)
        k = pl.program_id(1)

        @pl.when(k == 0)
        def _():
            o_ref[...] = jnp.zeros_like(o_ref)

        for s in range(ns):
            @pl.when((j >= starts[s]) & (j < starts[s] + counts[s]))
            def _(s=s):
                o_ref[...] += _dot_tn(h_ref[...], src[s][...])

    def src_spec(s):
        return pl.BlockSpec((None, tk, SEG),
                            lambda j, k: (jnp.clip(j - starts[s], 0, counts[s] - 1), k, 0))

    return pl.pallas_call(
        body, grid=(8, S // tk), name="grad_w_in",
        in_specs=[pl.BlockSpec((tk, D), lambda j, k: (k, 0))] + [src_spec(s) for s in range(ns)],
        out_specs=pl.BlockSpec((None, D, SEG), lambda j, k: (j // 2, 0, j % 2)),
        out_shape=jax.ShapeDtypeStruct((4, D, 2 * SEG), F32),
        compiler_params=_cp(("parallel", "arbitrary"), 48 if tk <= 1024 else 62),
    )(h, *sources)


def _lower_bound(lbl):
    l0 = lbl[0:1, :]
    l1 = lbl[1:2, :]
    m = jnp.maximum(l0, l1)
    e0 = jnp.exp(l0 - m)
    e1 = jnp.exp(l1 - m)
    return e0 / (e0 + e1)


def _tile_masks():
    row = lax.broadcasted_iota(jnp.int32, (HGRN_TILE, HGRN_TILE), 0)
    col = lax.broadcasted_iota(jnp.int32, (HGRN_TILE, HGRN_TILE), 1)
    same = (row // HGRN_CHUNK) == (col // HGRN_CHUNK)
    return same & (row >= col), same & (row <= col)


def _chunk_last(b):
    T = b.shape[0]
    b3 = b.reshape(T // HGRN_CHUNK, HGRN_CHUNK, HGRN_HEAD)
    return jnp.broadcast_to(b3[:, HGRN_CHUNK - 1:HGRN_CHUNK, :], b3.shape).reshape(T, HGRN_HEAD)


def _chunk_sum(x):
    T = x.shape[0]
    x3 = x.reshape(T // HGRN_CHUNK, HGRN_CHUNK, HGRN_HEAD)
    return jnp.broadcast_to(jnp.sum(x3, axis=1, keepdims=True), x3.shape).reshape(T, HGRN_HEAD)


def _hgrn_dims(S, SEG, rows):
    T = min(rows, S)
    assert S % T == 0 and T % HGRN_TILE == 0
    tiles = [slice(t * HGRN_TILE, (t + 1) * HGRN_TILE) for t in range(T // HGRN_TILE)]
    chunks = [slice(c * HGRN_CHUNK, (c + 1) * HGRN_CHUNK) for c in range(T // HGRN_CHUNK)]
    return SEG // HGRN_HEAD, T, T // HGRN_CHUNK, S // T, tiles, chunks


def _hgrn_fwd(zf32, lb_logits, gnorm):
    _, NLB, S, _ = zf32.shape
    SEG = NLB * LANES
    H, T, NC, NJ, tiles, chunks = _hgrn_dims(S, SEG, HGRN_STEP_FWD[0])
    HP = min(HGRN_STEP_FWD[1], H)
    assert H % HP == 0

    def body(zq_ref, zf_ref, zi_ref, zg_ref, lbl_ref, gn_ref, y_ref, st_ref, state):
        @pl.when(pl.program_id(1) == 0)
        def _():
            state[...] = jnp.zeros_like(state)

        tril, _ = _tile_masks()
        tril_bf = tril.astype(BF16)
        for hh in range(HP):
            cols = slice(hh * HGRN_HEAD, (hh + 1) * HGRN_HEAD)
            lb = _lower_bound(lbl_ref[:, cols])
            zq = zq_ref[hh]
            q = zq * _sigmoid(zq)
            f = lb + (1.0 - lb) * _sigmoid(zf_ref[hh])
            k = 1.0 - f
            logf = jnp.log(f)
            b = jnp.concatenate([_exact_dot(tril_bf, logf[t]) for t in tiles], axis=0)
            bl = _chunk_last(b)
            qd_b = (q * jnp.exp(b)).astype(BF16)
            kd_b = (k * jnp.exp(-b)).astype(BF16)
            ke_b = (k * jnp.exp(bl - b)).astype(BF16)
            v_b = zi_ref[hh].astype(BF16)
            o_intra = jnp.concatenate(
                [_dot(jnp.where(tril, _dot_nt(qd_b[t], kd_b[t]), 0.0).astype(BF16), v_b[t]) for t in tiles], axis=0)
            kvs = [_dot_tn(v_b[r], ke_b[r]) for r in chunks]
            ebl = jnp.exp(bl)
            st = state[hh]
            sts = []
            for c in range(NC):
                st_ref[c, hh] = st
                sts.append(st.astype(BF16))
                st = st * ebl[c * HGRN_CHUNK:c * HGRN_CHUNK + 1, :] + kvs[c]
            state[hh] = st
            o = o_intra + jnp.concatenate([_dot_nt(qd_b[r], sb) for r, sb in zip(chunks, sts)], axis=0)
            on = o * lax.rsqrt(jnp.mean(o * o, axis=-1, keepdims=True) + NORM_EPS) * gn_ref[...]
            zg = zg_ref[hh]
            y_ref[:, cols] = (on * (zg * _sigmoid(zg))).astype(BF16)

    def zspec(seg):
        return pl.BlockSpec((None, HP, T, HGRN_HEAD), lambda h, j: (seg, h, j, 0))

    return pl.pallas_call(
        body, grid=(H // HP, NJ), name="hgrn_fwd",
        in_specs=[zspec(0), zspec(1), zspec(2), zspec(3),
                  pl.BlockSpec((2, HP * HGRN_HEAD), lambda h, j: (0, h)),
                  pl.BlockSpec((1, HGRN_HEAD), lambda h, j: (0, 0))],
        out_specs=[pl.BlockSpec((T, HP * HGRN_HEAD), lambda h, j: (j, h)),
                   pl.BlockSpec((NC, HP, HGRN_HEAD, HGRN_HEAD), lambda h, j: (j, h, 0, 0))],
        out_shape=[jax.ShapeDtypeStruct((S, SEG), BF16),
                   jax.ShapeDtypeStruct((S // HGRN_CHUNK, H, HGRN_HEAD, HGRN_HEAD), F32)],
        scratch_shapes=[pltpu.VMEM((HP, HGRN_HEAD, HGRN_HEAD), F32)],
        compiler_params=_cp(("parallel", "arbitrary")),
    )(zf32, zf32, zf32, zf32, lb_logits, gnorm)


def _hgrn_bwd(zf32, lb_logits, gnorm, states, dy):
    _, NLB, S, _ = zf32.shape
    SEG = NLB * LANES
    H, T, NC, NJ, tiles, chunks = _hgrn_dims(S, SEG, HGRN_STEP_BWD[0])
    C = HGRN_CHUNK
    HP = min(HGRN_STEP_BWD[1], H)
    assert H % HP == 0

    def body(zq_ref, zf_ref, zi_ref, zg_ref, lbl_ref, gn_ref, st_ref, dy_ref, dz_ref, dl_ref, dgn_ref, gstate):
        @pl.when(pl.program_id(1) == 0)
        def _():
            gstate[...] = jnp.zeros_like(gstate)
            dl_ref[...] = jnp.zeros_like(dl_ref)
            dgn_ref[...] = jnp.zeros_like(dgn_ref)

        gn = gn_ref[...]
        tril, triu = _tile_masks()
        tril_bf = tril.astype(BF16)
        triu_bf = triu.astype(BF16)
        for hh in range(HP):
            cols = slice(hh * HGRN_HEAD, (hh + 1) * HGRN_HEAD)
            lb = _lower_bound(lbl_ref[:, cols])
            q, dq_dz = _silu_and_grad(zq_ref[hh])
            sf = _sigmoid(zf_ref[hh])
            f = lb + (1.0 - lb) * sf
            k = 1.0 - f
            logf = jnp.log(f)
            b = jnp.concatenate([_exact_dot(tril_bf, logf[t]) for t in tiles], axis=0)
            bl = _chunk_last(b)
            eb = jnp.exp(b)
            enb = jnp.exp(-b)
            ekl = jnp.exp(bl - b)
            ebl = jnp.exp(bl)
            qd = q * eb
            kd = k * enb
            ke = k * ekl
            qd_b = qd.astype(BF16)
            kd_b = kd.astype(BF16)
            ke_b = ke.astype(BF16)
            v_b = zi_ref[hh].astype(BF16)
            sts = [st_ref[c, hh] for c in range(NC)]
            sts_b = [s.astype(BF16) for s in sts]
            a_b = [jnp.where(tril, _dot_nt(qd_b[t], kd_b[t]), 0.0).astype(BF16) for t in tiles]
            o = (jnp.concatenate([_dot(a, v_b[t]) for a, t in zip(a_b, tiles)], axis=0)
                 + jnp.concatenate([_dot_nt(qd_b[r], sb) for r, sb in zip(chunks, sts_b)], axis=0))
            rinv = lax.rsqrt(jnp.mean(o * o, axis=-1, keepdims=True) + NORM_EPS)
            ohat = o * rinv
            sg, dsg = _silu_and_grad(zg_ref[hh])
            dyv = dy_ref[:, cols]
            don = dyv * sg
            dz_ref[3, :, cols] = (dyv * (ohat * gn) * dsg).astype(BF16)
            dgn_ref[hh] += jnp.sum(don * ohat, axis=0, keepdims=True)
            dohat = don * gn
            do = rinv * (dohat - ohat * jnp.mean(dohat * ohat, axis=-1, keepdims=True))
            do_b = do.astype(BF16)
            da_b = [jnp.where(tril, _dot_nt(do_b[t], v_b[t]), 0.0).astype(BF16) for t in tiles]
            dv_intra = jnp.concatenate([_dot_tn(a, do_b[t]) for a, t in zip(a_b, tiles)], axis=0)
            dqd_intra = jnp.concatenate([_dot(da, kd_b[t]) for da, t in zip(da_b, tiles)], axis=0)
            dkd = jnp.concatenate([_dot_tn(da, qd_b[t]) for da, t in zip(da_b, tiles)], axis=0)
            dqd_inter = jnp.concatenate([_dot(do_b[r], sb) for r, sb in zip(chunks, sts_b)], axis=0)
            gks = [_dot_tn(do_b[r], qd_b[r]) for r in chunks]
            g = gstate[hh]
            gs = [None] * NC
            for c in reversed(range(NC)):
                gs[c] = g
                g = g * ebl[c * C:c * C + 1, :] + gks[c]
            gstate[hh] = g
            gs_b = [x.astype(BF16) for x in gs]
            dv = dv_intra + jnp.concatenate([_dot_nt(ke_b[r], gb) for r, gb in zip(chunks, gs_b)], axis=0)
            dz_ref[2, :, cols] = dv.astype(BF16)
            dke = jnp.concatenate([_dot(v_b[r], gb) for r, gb in zip(chunks, gs_b)], axis=0)
            debl = jnp.concatenate(
                [jnp.broadcast_to(jnp.sum(x * s, axis=0, keepdims=True), (C, HGRN_HEAD)) for x, s in zip(gs, sts)], axis=0)
            dqd = dqd_intra + dqd_inter
            dz_ref[0, :, cols] = ((dqd * eb) * dq_dz).astype(BF16)
            t_ke = dke * ke
            db = dqd * qd - dkd * kd - t_ke
            db_last = _chunk_sum(t_ke) + debl * ebl
            dk = dkd * enb + dke * ekl
            dlogf = jnp.concatenate([_exact_dot(triu_bf, db[t]) for t in tiles], axis=0) + db_last
            df = dlogf / f - dk
            dz_ref[1, :, cols] = (df * (1.0 - lb) * (sf * (1.0 - sf))).astype(BF16)
            dlb = jnp.sum(df * (1.0 - sf), axis=0, keepdims=True)
            dl0 = dlb * lb * (1.0 - lb)
            dl_ref[0:1, cols] += dl0
            dl_ref[1:2, cols] -= dl0

    def zspec(seg):
        return pl.BlockSpec((None, HP, T, HGRN_HEAD), lambda h, j: (seg, h, NJ - 1 - j, 0))

    return pl.pallas_call(
        body, grid=(H // HP, NJ), name="hgrn_bwd",
        in_specs=[zspec(0), zspec(1), zspec(2), zspec(3),
                  pl.BlockSpec((2, HP * HGRN_HEAD), lambda h, j: (0, h)),
                  pl.BlockSpec((1, HGRN_HEAD), lambda h, j: (0, 0)),
                  pl.BlockSpec((NC, HP, HGRN_HEAD, HGRN_HEAD), lambda h, j: (NJ - 1 - j, h, 0, 0)),
                  pl.BlockSpec((T, HP * HGRN_HEAD), lambda h, j: (NJ - 1 - j, h))],
        out_specs=[pl.BlockSpec((4, T, HP * HGRN_HEAD), lambda h, j: (0, NJ - 1 - j, h)),
                   pl.BlockSpec((2, HP * HGRN_HEAD), lambda h, j: (0, h)),
                   pl.BlockSpec((HP, 1, HGRN_HEAD), lambda h, j: (h, 0, 0))],
        out_shape=[jax.ShapeDtypeStruct((4, S, SEG), BF16), jax.ShapeDtypeStruct((2, SEG), F32),
                   jax.ShapeDtypeStruct((H, 1, HGRN_HEAD), F32)],
        scratch_shapes=[pltpu.VMEM((HP, HGRN_HEAD, HGRN_HEAD), F32)],
        compiler_params=_cp(("parallel", "arbitrary")),
    )(zf32, zf32, zf32, zf32, lb_logits, gnorm, states, dy)


def _alibi_slopes(seg):
    n_heads = seg // ATTN_HEAD
    s = 2.0 ** (-8.0 * np.arange(1, n_heads + 1, dtype=np.float64) / n_heads)
    return jnp.asarray(np.repeat(s, ATTN_HEAD)[None, :], F32)


def _attn_dims(S, SEG, d):
    rb = BAND * d
    assert S % rb == 0 and SEG % LANES == 0
    npb = max(1, min(SEG // LANES, ATTN_BLOCK_ELEMS // (rb * LANES)))
    assert (SEG // LANES) % npb == 0
    return rb, npb, S // rb, (SEG // LANES) // npb


def _res_rows(r, d):
    return pl.ds(0, BAND) if d == 1 else pl.ds(r, BAND, stride=d)


def _for_residues(d, fn):
    if d == 1:
        fn(0)
    else:
        def step(r, carry):
            fn(r)
            return carry
        lax.fori_loop(0, d, step, 0, unroll=ATTN_UNROLL)


def _for_groups(d, n_pairs, fn):
    def over_pairs(r):
        for g0 in range(0, n_pairs, ATTN_UNROLL):
            fn([(r, p) for p in range(g0, min(n_pairs, g0 + ATTN_UNROLL))])

    if d == 1:
        over_pairs(0)
    elif n_pairs >= ATTN_UNROLL:
        def step(r, carry):
            over_pairs(r)
            return carry
        lax.fori_loop(0, d, step, 0)
    else:
        per_group = ATTN_UNROLL // n_pairs
        assert d % per_group == 0

        def step(g, carry):
            fn([(g * per_group + i, p) for i in range(per_group) for p in range(n_pairs)])
            return carry
        lax.fori_loop(0, d // per_group, step, 0)


def _band_terms(n, d):
    i = lax.broadcasted_iota(jnp.int32, (BAND, 2 * BAND), 0)
    jj = lax.broadcasted_iota(jnp.int32, (BAND, 2 * BAND), 1)
    delta = BAND + i - jj
    valid = (delta >= 0) & (delta <= BAND) & ((n > 0) | (jj >= BAND))
    return (-d * delta).astype(F32), valid


def _head_biases(slopes, nd, valid):
    out = []
    for s in _per_head(slopes):
        s2 = jnp.concatenate([s, s], axis=1)
        out.append(jnp.where(valid, s2 * nd, NEG))
    return jnp.concatenate(out, axis=0)


def _stack_heads(x):
    lane = lax.broadcasted_iota(jnp.int32, x.shape, 1)
    zero = jnp.zeros_like(x)
    return jnp.concatenate([jnp.where(lane < ATTN_HEAD, x, zero), jnp.where(lane < ATTN_HEAD, zero, x)], axis=0)


def _unstack_heads(x2):
    first = lax.broadcasted_iota(jnp.int32, (BAND, LANES), 1) < ATTN_HEAD
    return jnp.where(first, x2[:BAND], x2[BAND:])


def _stack_per_head(x):
    a, b = _per_head(x)
    col = jnp.concatenate([a, b], axis=0)
    return jnp.concatenate([col, col], axis=1)


def _per_head(x):
    lane = lax.broadcasted_iota(jnp.int32, x.shape, 1)
    sw = pltpu.roll(x, ATTN_HEAD, 1)
    first = lane < ATTN_HEAD
    return jnp.where(first, x, sw), jnp.where(first, sw, x)


def _qkv_source(zz, d):
    z, z16 = zz
    if d == DEINTERLEAVE:
        def take(ref, p, r):
            return ref.at[p][r]

        def spec(seg, np_, row_block):
            return pl.BlockSpec((None, np_, d, BAND, LANES), lambda c, n: (seg, c, 0, row_block(c, n), 0))
        return z, z16, take, spec

    def take(ref, p, r):
        return ref.at[p][_res_rows(r, d), :]

    def spec(seg, np_, row_block):
        return pl.BlockSpec((None, np_, BAND * d, LANES), lambda c, n: (SEG_QKV + seg, c, row_block(c, n), 0))
    return z, z, take, spec


def _attn_fwd(qkv, slopes, d):
    qkv, src, take, spec = _qkv_source(qkv, d)
    _, NLB, S, _ = qkv.shape
    rb, NP, nb, ncb = _attn_dims(S, NLB * LANES, d)

    def body(q_ref, kc_ref, vc_ref, sl_ref, o_ref, l_ref, kp_ref, vp_ref):
        n = pl.program_id(1)

        @pl.when(n == 0)
        def _():
            kp_ref[...] = jnp.zeros_like(kp_ref)
            vp_ref[...] = jnp.zeros_like(vp_ref)

        nd, valid = _band_terms(n, d)
        biases = [_head_biases(sl_ref[:, p * LANES:(p + 1) * LANES], nd, valid) for p in range(NP)]

        def group(items):
            scores, values = [], []
            for r, p in items:
                kc = jnp.concatenate([take(kp_ref, p, r), take(kc_ref, p, r)], axis=0).astype(BF16)
                values.append(jnp.concatenate([take(vp_ref, p, r), take(vc_ref, p, r)], axis=0).astype(BF16))
                scores.append(_dot_nt(_stack_heads((take(q_ref, p, r) * ATTN_SCALE).astype(BF16)), kc))
            probs = []
            for (r, p), s in zip(items, scores):
                s = s + biases[p]
                m = jnp.max(s, axis=-1, keepdims=True)
                e = jnp.exp(s - m)
                den = jnp.sum(e, axis=-1, keepdims=True)
                probs.append((e.astype(BF16), den, m + jnp.log(den)))
            for (r, p), vc, (e, den, lse) in zip(items, values, probs):
                rows = _res_rows(r, d)
                o_ref.at[p][rows, :] = _unstack_heads(_dot(e, vc) / den)
                l_ref.at[p][rows, :] = _unstack_heads(jnp.broadcast_to(lse, (2 * BAND, LANES)))

        _for_groups(d, NP, group)
        kp_ref[...] = kc_ref[...]
        vp_ref[...] = vc_ref[...]

    cur = lambda c, n: n
    out = pl.BlockSpec((NP, rb, LANES), lambda c, n: (c, n, 0))
    kv_block = spec(1, NP, cur).block_shape[1:]
    return pl.pallas_call(
        body, grid=(ncb, nb), name=f"attn_fwd_d{d}",
        in_specs=[spec(0, NP, cur), spec(1, NP, cur), spec(2, NP, cur),
                  pl.BlockSpec((1, NP * LANES), lambda c, n: (0, c))],
        out_specs=[out, out],
        out_shape=[jax.ShapeDtypeStruct((NLB, S, LANES), F32)] * 2,
        scratch_shapes=[pltpu.VMEM(kv_block, F32), pltpu.VMEM(kv_block, F32)],
        compiler_params=_cp(("parallel", "arbitrary")),
    )(src, src, src, slopes)


def _attn_merge(outs, lses, zf32):
    NLB, S, _ = outs[0].shape
    SEG = NLB * LANES
    tm = min(256, S)

    def body(o1, o2, o3, l1, l2, l3, zg_ref, o_ref, lse_ref, y_ref):
        a, b, c = l1[...], l2[...], l3[...]
        m = jnp.maximum(jnp.maximum(a, b), c)
        ea, eb, ec = jnp.exp(a - m), jnp.exp(b - m), jnp.exp(c - m)
        tot = ea + eb + ec
        o = (ea / tot) * o1[...] + (eb / tot) * o2[...] + (ec / tot) * o3[...]
        o_ref[...] = o
        lse_ref[...] = m + jnp.log(tot)
        zg = zg_ref[...]
        y = (o * (zg * _sigmoid(zg))).astype(BF16)
        for p in range(NLB):
            y_ref[:, p * LANES:(p + 1) * LANES] = y[p]

    blk = pl.BlockSpec((NLB, tm, LANES), lambda i: (0, i, 0))
    return pl.pallas_call(
        body, grid=(S // tm,), name="attn_merge",
        in_specs=[blk] * 6 + [pl.BlockSpec((None, NLB, tm, LANES), lambda i: (SEG_GATE_A, 0, i, 0))],
        out_specs=[blk, blk, pl.BlockSpec((tm, SEG), lambda i: (i, 0))],
        out_shape=[jax.ShapeDtypeStruct((NLB, S, LANES), F32), jax.ShapeDtypeStruct((NLB, S, LANES), F32),
                   jax.ShapeDtypeStruct((S, SEG), BF16)],
        compiler_params=_cp(("parallel",)),
    )(*outs, *lses, zf32)


def _attn_gate_bwd(dy, o, zf32):
    NP, S, _ = o.shape
    SEG = NP * LANES
    tm = min(256, S)

    def body(dy_ref, o_ref, zg_ref, do_ref, dl_ref, dzg_ref):
        r = lax.broadcasted_iota(jnp.int32, (LANES, LANES), 0) // ATTN_HEAD
        c = lax.broadcasted_iota(jnp.int32, (LANES, LANES), 1) // ATTN_HEAD
        same_head = (r == c).astype(BF16)
        for p in range(NP):
            cols = slice(p * LANES, (p + 1) * LANES)
            sg, dsg = _silu_and_grad(zg_ref[p])
            dyv = dy_ref[:, cols]
            ov = o_ref[p]
            do = dyv * sg
            do_ref[p] = do
            dzg_ref[:, cols] = (dyv * ov * dsg).astype(BF16)
            dl_ref[p] = _exact_dot_right(do * ov, same_head)

    blk = pl.BlockSpec((NP, tm, LANES), lambda i: (0, i, 0))
    return pl.pallas_call(
        body, grid=(S // tm,), name="attn_gate_bwd",
        in_specs=[pl.BlockSpec((tm, SEG), lambda i: (i, 1)), blk,
                  pl.BlockSpec((None, NP, tm, LANES), lambda i: (SEG_GATE_A, 0, i, 0))],
        out_specs=[blk, blk, pl.BlockSpec((None, tm, SEG), lambda i: (3, i, 0))],
        out_shape=[jax.ShapeDtypeStruct((NP, S, LANES), F32), jax.ShapeDtypeStruct((NP, S, LANES), F32),
                   jax.ShapeDtypeStruct((4, S, SEG), BF16)],
        compiler_params=_cp(("parallel",)),
    )(dy, o, zf32)


def _attn_bwd(qkv, slopes, do, lse, dl, d, acc, into):
    qkv, src, take, spec = _qkv_source(qkv, d)
    _, NLB, S, _ = qkv.shape
    SEG = NLB * LANES
    rb, NP, nb, ncb = _attn_dims(S, SEG, d)
    has_acc = acc is not None
    out_dtype = F32 if into is None else into.dtype
    assert into is None or d == 1

    def body(*refs):
        q_ref, kc_ref, vc_ref, sl_ref, do_ref, lse_ref, dl_ref = refs[:7]
        acc_ref = refs[7] if has_acc else None
        out_ref, cq, ck, cv, kp_ref, vp_ref = refs[-6:]
        n = pl.program_id(1)

        def emit(r, p, dq, dk, dv):
            rows = _res_rows(r, d)
            for t, val in enumerate((dq, dk, dv)):
                if has_acc:
                    val = val + acc_ref.at[t].at[p][rows, :]
                if into is None:
                    out_ref.at[t].at[p][rows, :] = val.astype(out_dtype)
                else:
                    out_ref.at[t][rows, p * LANES:(p + 1) * LANES] = val.astype(out_dtype)

        @pl.when(n == 0)
        def _():
            cq[...] = jnp.zeros_like(cq)
            ck[...] = jnp.zeros_like(ck)
            cv[...] = jnp.zeros_like(cv)
            kp_ref[...] = jnp.zeros_like(kp_ref)
            vp_ref[...] = jnp.zeros_like(vp_ref)

        @pl.when(n < nb)
        def _():
            nd, valid = _band_terms(n, d)
            biases = [_head_biases(sl_ref[:, p * LANES:(p + 1) * LANES], nd, valid) for p in range(NP)]

            def group(items):
                first = []
                for r, p in items:
                    rows = _res_rows(r, d)
                    kc = jnp.concatenate([take(kp_ref, p, r), take(kc_ref, p, r)], axis=0).astype(BF16)
                    vc = jnp.concatenate([take(vp_ref, p, r), take(vc_ref, p, r)], axis=0).astype(BF16)
                    qs = _stack_heads((take(q_ref, p, r) * ATTN_SCALE).astype(BF16))
                    dos = _stack_heads(do_ref.at[p][rows, :].astype(BF16))
                    first.append((kc, qs, dos, _dot_nt(qs, kc), _dot_nt(dos, vc)))
                second = []
                for (r, p), (kc, qs, dos, s, dp) in zip(items, first):
                    rows = _res_rows(r, d)
                    pr = jnp.exp(s + biases[p] - _stack_per_head(lse_ref.at[p][rows, :]))
                    ds = (pr * (dp - _stack_per_head(dl_ref.at[p][rows, :]))).astype(BF16)
                    second.append((kc, qs, dos, pr.astype(BF16), ds))
                for (r, p), (kc, qs, dos, pr, ds) in zip(items, second):
                    dq = _unstack_heads(_dot(ds, kc)) * ATTN_SCALE
                    dk = _dot_tn(ds, qs)
                    dv = _dot_tn(pr, dos)
                    emit(r, p, cq[r, p], ck[r, p] + dk[:BAND, :], cv[r, p] + dv[:BAND, :])
                    cq[r, p] = dq
                    ck[r, p] = dk[BAND:, :]
                    cv[r, p] = dv[BAND:, :]

            _for_groups(d, NP, group)
            kp_ref[...] = kc_ref[...]
            vp_ref[...] = vc_ref[...]

        @pl.when(n == nb)
        def _():
            def last(r):
                for p in range(NP):
                    emit(r, p, cq[r, p], ck[r, p], cv[r, p])
            _for_residues(d, last)

    cur = lambda c, n: (c, jnp.minimum(n, nb - 1), 0)
    lag = lambda c, n: (0, c, jnp.clip(n - 1, 0, nb - 1), 0)

    at = lambda c, n: jnp.minimum(n, nb - 1)
    in_specs = [spec(0, NP, at), spec(1, NP, at), spec(2, NP, at),
                pl.BlockSpec((1, NP * LANES), lambda c, n: (0, c))] + [pl.BlockSpec((NP, rb, LANES), cur)] * 3
    kv_block = in_specs[1].block_shape[1:]
    args = [src, src, src, slopes, do, lse, dl]
    aliases = {}
    if has_acc:
        in_specs.append(pl.BlockSpec((3, NP, rb, LANES), lag))
        args.append(acc)
        if into is None:
            aliases = {7: 0}
    if into is None:
        out_sds = jax.ShapeDtypeStruct((3, NLB, S, LANES), F32)
        out_spec = pl.BlockSpec((3, NP, rb, LANES), lag)
    else:
        in_specs.append(ANY)
        args.append(into)
        aliases = {len(args) - 1: 0}
        out_sds = jax.ShapeDtypeStruct(into.shape, into.dtype)
        out_spec = pl.BlockSpec((3, rb, NP * LANES), lambda c, n: (0, jnp.clip(n - 1, 0, nb - 1), c))
    return pl.pallas_call(
        body, grid=(ncb, nb + 1), name=f"attn_bwd_d{d}",
        in_specs=in_specs, out_specs=out_spec, out_shape=out_sds,
        scratch_shapes=[pltpu.VMEM((d, NP, BAND, LANES), F32)] * 3 + [pltpu.VMEM(kv_block, F32)] * 2,
        input_output_aliases=aliases,
        compiler_params=_cp(("parallel", "arbitrary")),
    )(*args)


def _adamw(w, g, m, v, name):
    R, C = w.shape
    tr = R if R <= 256 else 256
    assert R % tr == 0

    def body(w_ref, g_ref, m_ref, v_ref, d_ref, nm_ref, nv_ref, go_ref):
        g = g_ref[...]
        nm = ADAM_B1 * m_ref[...] + (1.0 - ADAM_B1) * g
        nv = ADAM_B2 * v_ref[...] + (1.0 - ADAM_B2) * (g * g)
        m_hat = nm / (1.0 - ADAM_B1 ** ADAM_STEP)
        v_hat = nv / (1.0 - ADAM_B2 ** ADAM_STEP)
        d_ref[...] = -ADAM_LR * (m_hat / (jnp.sqrt(v_hat) + ADAM_EPS) + ADAM_WD * w_ref[...])
        nm_ref[...] = nm
        nv_ref[...] = nv
        go_ref[...] = g

    blk = pl.BlockSpec((tr, C), lambda i: (i, 0))
    sds = jax.ShapeDtypeStruct((R, C), F32)
    return pl.pallas_call(
        body, grid=(R // tr,), name=name, in_specs=[blk] * 4, out_specs=[blk] * 4, out_shape=[sds] * 4,
        compiler_params=_cp(("parallel",)),
    )(w, g, m, v)


def _coords():
    return lax.axis_index("x"), lax.axis_index("y"), lax.axis_index("c")


def _other_chips(x, y):
    return [(1 - x, y), (x, 1 - y), (1 - x, 1 - y)]


ANY = pl.BlockSpec(memory_space=pl.ANY)


def _cast_into_slot(w, where, name):
    R, C = w.shape
    tr = min(256, R)

    def body(where_ref, w_ref, o_ref):
        o_ref[...] = w_ref[...].astype(BF16)

    grid_spec = pltpu.PrefetchScalarGridSpec(
        num_scalar_prefetch=1, grid=(R // tr,),
        in_specs=[pl.BlockSpec((tr, C), lambda i, w: (i, 0))],
        out_specs=pl.BlockSpec((None, tr, C), lambda i, w: (w[1], i, 0)))
    return pl.pallas_call(
        body, grid_spec=grid_spec, name=name, out_shape=jax.ShapeDtypeStruct((4, R, C), BF16),
        compiler_params=_cp(("parallel",)),
    )(where, w)


def _pair_sum(g, sib, where, name):
    _, n2, C = g.shape
    N = n2 // 2
    tr = min(256, N)
    nt = N // tr

    def body(where_ref, g_ref, s_ref, qb_ref, own_ref):
        q = pl.program_id(1)
        tot = g_ref[...] + s_ref[...]
        qb_ref[...] = tot.astype(BF16)

        @pl.when(q == where_ref[1])
        def _():
            own_ref[...] = tot

    grid_spec = pltpu.PrefetchScalarGridSpec(
        num_scalar_prefetch=1, grid=(nt, 4),
        in_specs=[pl.BlockSpec((None, tr, C), lambda i, q, w: (q, w[0] * nt + i, 0)),
                  pl.BlockSpec((None, tr, C), lambda i, q, w: (q, i, 0))],
        out_specs=[pl.BlockSpec((None, tr, C), lambda i, q, w: (q, i, 0)),
                   pl.BlockSpec((tr, C), lambda i, q, w: (i, 0))])
    return pl.pallas_call(
        body, grid_spec=grid_spec, name=name,
        out_shape=[jax.ShapeDtypeStruct((4, N, C), BF16), jax.ShapeDtypeStruct((N, C), F32)],
        compiler_params=_cp(("parallel", "arbitrary")),
    )(where, g, sib)


HBM = pl.BlockSpec(memory_space=pltpu.HBM)
SEM = pl.BlockSpec(memory_space=pltpu.SEMAPHORE)


def _in_hbm(a):
    return pltpu.with_memory_space_constraint(a, pltpu.HBM)


def _split_start(name, copies, arrays, n_sems, after=None):
    n = len(arrays)

    def body(*refs):
        for cp in copies(refs[:n], refs[-n - 3], refs[-n - 2]):
            cp.start()
        refs[-1][...] = jnp.zeros_like(refs[-1])

    ordered = () if after is None else (after,)
    outs = pl.pallas_call(
        body, name=name,
        out_shape=(pltpu.SemaphoreType.DMA((n_sems,)), pltpu.SemaphoreType.DMA((n_sems,)),
                   *[pltpu.HBM(a.shape, a.dtype) for a in arrays], jax.ShapeDtypeStruct((8, LANES), F32)),
        in_specs=(HBM,) * n + (ANY,) * len(ordered),
        out_specs=(SEM, SEM) + (HBM,) * n + (pl.BlockSpec(memory_space=pltpu.VMEM),),
        input_output_aliases={i: 2 + i for i in range(n)},
        compiler_params=pltpu.CompilerParams(has_side_effects=pltpu.SideEffectType.DATAFLOW_SIDE_EFFECTING),
    )(*[_in_hbm(a) for a in arrays], *ordered)
    return outs[0], outs[1], list(outs[2:2 + n]), outs[-1]


def _split_wait(name, copies, send_sems, recv_sems, arrays, after):
    n = len(arrays)

    def body(*refs):
        for cp in copies(refs[:n], refs[n], refs[n + 1]):
            cp.wait_send()
            cp.wait_recv()

    outs = pl.pallas_call(
        body, name=name,
        out_shape=tuple(pltpu.HBM(a.shape, a.dtype) for a in arrays),
        in_specs=(HBM,) * n + (SEM, SEM, ANY), out_specs=(HBM,) * n,
        input_output_aliases={i: i for i in range(n)},
        compiler_params=pltpu.CompilerParams(has_side_effects=pltpu.SideEffectType.DATAFLOW_SIDE_EFFECTING),
    )(*arrays, send_sems, recv_sems, after)
    return list(outs)


def _remote(src, dst, sems, k, to):
    send_sems, recv_sems = sems
    return pltpu.make_async_remote_copy(src_ref=src, dst_ref=dst, send_sem=send_sems.at[k], recv_sem=recv_sems.at[k],
                                        device_id=to, device_id_type=MESH)


def _chip_at(x, y, rel):
    px = 1 - x if rel & 2 else x
    py = 1 - y if rel & 1 else y
    return px, py, 2 * px + py


def _gather_in_copies(rels):
    def copies(refs, send_sems, recv_sems):
        (w,) = refs
        x, y, c = _coords()
        seg = w.shape[2] // 2
        mine = w.at[2 * x + y, :, pl.ds(c * seg, seg)]
        return [_remote(mine, mine, (send_sems, recv_sems), k, _chip_at(x, y, rel)[:2] + (c,))
                for k, rel in enumerate(rels)]
    return copies


def _gather_out_copies(refs, send_sems, recv_sems):
    (w,) = refs
    x, y, c = _coords()
    mine = w.at[2 * x + y]
    return [_remote(mine, mine, (send_sems, recv_sems), k, (px, py, c)) for k, (px, py) in enumerate(_other_chips(x, y))]


def _swap_copies(refs, send_sems, recv_sems):
    gi, go, si, so = refs
    x, y, c = _coords()
    cps = []
    for a, (src, dst) in enumerate(((gi, si), (go, so))):
        nr = dst.shape[1]
        cps.append(_remote(src.at[:, pl.ds((1 - c) * nr, nr), :], dst, (send_sems, recv_sems), a, (x, y, 1 - c)))
    return cps


def _scatter_copies(refs, send_sems, recv_sems):
    qi, qo, ri, ro = refs
    x, y, c = _coords()
    cps = []
    for k, (px, py) in enumerate(_other_chips(x, y)):
        for a, (src, dst) in enumerate(((qi, ri), (qo, ro))):
            cps.append(_remote(src.at[2 * px + py], dst.at[k], (send_sems, recv_sems), 2 * k + a, (px, py, c)))
    return cps


def _forward_copies(rels):
    def copies(refs, send_sems, recv_sems):
        (w,) = refs
        x, y, c = _coords()
        seg = w.shape[2] // 2
        cps = []
        for k, rel in enumerate(rels):
            got = w.at[_chip_at(x, y, rel)[2], :, pl.ds(c * seg, seg)]
            cps.append(_remote(got, got, (send_sems, recv_sems), k, (x, y, 1 - c)))
        return cps
    return copies


def _chip_sum(own, got, where, name):
    N, C = own.shape
    tr = min(256, N)
    nt = N // tr

    def body(where_ref, own_ref, got_ref, o_ref):
        t = own_ref[...]
        for k in range(3):
            t = t + got_ref[k].astype(F32)
        o_ref[...] = t

    grid_spec = pltpu.PrefetchScalarGridSpec(
        num_scalar_prefetch=1, grid=(nt,),
        in_specs=[pl.BlockSpec((tr, C), lambda i, w: (i, 0)), pl.BlockSpec((3, tr, C), lambda i, w: (0, i, 0))],
        out_specs=pl.BlockSpec((tr, C), lambda i, w: (w[0] * nt + i, 0)))
    return pl.pallas_call(
        body, grid_spec=grid_spec, name=name, out_shape=jax.ShapeDtypeStruct((2 * N, C), F32),
        compiler_params=_cp(("parallel",)),
    )(where, own, got)


def _join_copies(refs, send_sems, recv_sems):
    x, y, c = _coords()
    cps = []
    for a, ref in enumerate(refs):
        nr = ref.shape[0] // 2
        mine = ref.at[pl.ds(c * nr, nr), :]
        cps.append(_remote(mine, mine, (send_sems, recv_sems), a, (x, y, 1 - c)))
    return cps


def _all_reduce_small(part, token):
    R, C = part.shape

    def body(p_ref, _, o_ref, slots, send_sems, recv_sems):
        x, y, c = _coords()
        me = 4 * x + 2 * y + c
        slots[me] = p_ref[...]
        cps = []
        for k in range(1, 8):
            fx, fy, fc = (k >> 2) & 1, (k >> 1) & 1, k & 1
            peer = (1 - x if fx else x, 1 - y if fy else y, 1 - c if fc else c)
            cp = pltpu.make_async_remote_copy(src_ref=p_ref, dst_ref=slots.at[me], send_sem=send_sems.at[k - 1],
                                              recv_sem=recv_sems.at[k - 1], device_id=peer, device_id_type=MESH)
            cp.start()
            cps.append(cp)
        for cp in cps:
            cp.wait()
        t = slots[0]
        for k in range(1, 8):
            t = t + slots[k]
        o_ref[...] = t

    vm = pl.BlockSpec(memory_space=pltpu.VMEM)
    return pl.pallas_call(
        body, name="all_reduce_small", in_specs=[vm, vm], out_specs=vm,
        out_shape=jax.ShapeDtypeStruct((R, C), F32),
        scratch_shapes=[pltpu.VMEM((8, R, C), F32), pltpu.SemaphoreType.DMA((7,)), pltpu.SemaphoreType.DMA((7,))],
    )(part, token)


def _mixers_forward(z, lb_logits, hgrn_gnorm):
    slopes = _alibi_slopes(z[0].shape[1] * LANES)
    yh, states = _hgrn_fwd(z[0], lb_logits, hgrn_gnorm)
    outs, lses = [], []
    for d in DILATIONS:
        o, l = _attn_fwd(z, slopes, d)
        outs.append(o)
        lses.append(l)
    o_attn, lse, ya = _attn_merge(outs, lses, z[0])
    return yh, ya, (states, o_attn, lse, slopes)


def _backward_to_dz(z, kept, lb_logits, hgrn_gnorm, yh, ya, w_out_all, x2, tgt, fgain, h):
    states, o_attn, lse, slopes = kept
    dout, doutb, loss, dfg = _out_proj_loss(yh, ya, w_out_all, x2, tgt, fgain)
    dy = _dy_proj(doutb, w_out_all)
    g_w_out = _grad_w_out(yh, ya, doutb)
    dzh, dlogits, dgn = _hgrn_bwd(z[0], lb_logits, hgrn_gnorm, states, dy)
    do, dl, dza = _attn_gate_bwd(dy, o_attn, z[0])
    acc = None
    order = sorted(DILATIONS, reverse=True)
    for d in order[:-1]:
        acc = _attn_bwd(z, slopes, do, lse, dl, d, acc, None)
    dza = _attn_bwd(z, slopes, do, lse, dl, order[-1], acc, dza)
    sources = [dzh, dza]
    g_w_in = _grad_w_in(h, sources)
    return loss, dfg, dlogits, dgn, g_w_out, g_w_in, sources, dout


def _grad_x_half(sources, w_all, x2, rinv, norm_gain, dout, token, part, gx_prev):
    dh = _dh_proj(sources, w_all, token, part, f"dh_proj_{part}")
    return _rms_bwd(dh, x2, rinv, norm_gain, dout, part, gx_prev, f"rms_bwd_{part}")


def _local_step(x2, tgt, norm_gain, w_all, lb_logits, hgrn_gnorm, w_out_all, fgain):
    token = jnp.zeros((8, LANES), F32)
    where = jnp.zeros((2,), jnp.int32)
    h, rinv = _rms_fwd(x2, norm_gain, token)
    z = _in_proj(h, w_all, where, [(rel, half) for rel in range(4) for half in range(2)], None, token, "in_proj_all")
    yh, ya, kept = _mixers_forward(z, lb_logits, hgrn_gnorm)
    loss, dfg, dlogits, dgn, g_w_out, g_w_in, sources, dout = _backward_to_dz(
        z, kept, lb_logits, hgrn_gnorm, yh, ya, w_out_all, x2, tgt, fgain, h)
    gx, dg0 = _grad_x_half(sources, w_all, x2, rinv, norm_gain, dout, token, 0, None)
    gx, dg1 = _grad_x_half(sources, w_all, x2, rinv, norm_gain, dout, token, 1, gx)
    return loss, gx, dg0 + dg1, g_w_in, dlogits, dgn, g_w_out, dfg


def _pack_small(D, loss, dgain, dlogits, dgn, dfg):
    def row(v):
        v = v.reshape(1, -1)
        return jnp.pad(v, ((0, 0), (0, D - v.shape[1])))
    rows = [row(dgain), row(dfg), row(dlogits[0]), row(dlogits[1]), row(jnp.sum(dgn, axis=0)), row(loss)]
    rows += [jnp.zeros((1, D), F32)] * (8 - len(rows))
    return jnp.concatenate(rows, axis=0)


def kernel(x, norm_gain, w_in, lb_logits, hgrn_gnorm, w_out, final_gain, loss_target, m_norm_gain, m_w_in, m_lb_logits, m_hgrn_gnorm, m_w_out, m_final_gain, v_norm_gain, v_w_in, v_lb_logits, v_hgrn_gnorm, v_w_out, v_final_gain):
    _, S, D = x.shape
    SEG = w_in.shape[2] // 2
    x2 = x[0]
    tgt = loss_target[0]
    fgain = final_gain.reshape(1, D)
    where = jnp.stack([lax.axis_index("c"), 2 * lax.axis_index("x") + lax.axis_index("y")]).astype(jnp.int32)

    wia = _cast_into_slot(w_in[0], where, "cast_w_in")
    woa = _cast_into_slot(w_out[0], where, "cast_w_out")
    near, far = (2, 1), (3,)
    ga = _split_start("gather_near_start", _gather_in_copies(near), [wia], 2)
    h, rinv = _rms_fwd(x2, norm_gain, ga[3])
    z = _in_proj(h, ga[2][0], where, [(0, 0), (0, 1)], None, ga[3], "in_proj_own")
    (wia,) = _split_wait("gather_near_wait", _gather_in_copies(near), ga[0], ga[1], ga[2], z[0])
    gb = _split_start("gather_far_start", _gather_in_copies(far), [wia], 1)
    fa = _split_start("forward_near_start", _forward_copies(near), gb[2], 2, after=gb[3])
    z = _in_proj(h, fa[2][0], where, [(2, "mine"), (1, "mine")], z, fa[3], "in_proj_near")
    (wia,) = _split_wait("forward_near_wait", _forward_copies(near), fa[0], fa[1], fa[2], z[0])
    (wia,) = _split_wait("gather_far_wait", _gather_in_copies(far), gb[0], gb[1], [wia], z[0])
    out_sems = _split_start("gather_out_start", _gather_out_copies, [woa], 3, after=wia)
    fb = _split_start("forward_far_start", _forward_copies(far), [wia], 1, after=out_sems[3])
    z = _in_proj(h, fb[2][0], where, [(3, "mine"), (2, "sibling"), (1, "sibling")], z, fb[3], "in_proj_far")
    (wia,) = _split_wait("forward_far_wait", _forward_copies(far), fb[0], fb[1], fb[2], z[0])
    z = _in_proj(h, wia, where, [(3, "sibling")], z, fb[3], "in_proj_last")
    yh, ya, kept = _mixers_forward(z, lb_logits, hgrn_gnorm)
    (woa,) = _split_wait("gather_out_wait", _gather_out_copies, out_sems[0], out_sems[1], out_sems[2], ya)
    w_out_all = woa.reshape(2 * SEG, D)

    loss, dfg, dlogits, dgn, g_w_out, g_w_in, sources, dout = _backward_to_dz(
        z, kept, lb_logits, hgrn_gnorm, yh, ya, w_out_all, x2, tgt, fgain, h)

    sib_i = lax.empty((4, g_w_in.shape[1] // 2, g_w_in.shape[2]), F32)
    sib_o = lax.empty((4, g_w_out.shape[1] // 2, g_w_out.shape[2]), F32)
    sems = _split_start("swap_start", _swap_copies, [g_w_in, g_w_out, sib_i, sib_o], 2)
    grad_x, dg0 = _grad_x_half(sources, wia, x2, rinv, norm_gain, dout, sems[3], 0, None)
    g_w_in, g_w_out, sib_i, sib_o = _split_wait("swap_wait", _swap_copies, sems[0], sems[1], sems[2], grad_x)
    qi, own_i = _pair_sum(g_w_in, sib_i, where, "pair_sum_w_in")
    qo, own_o = _pair_sum(g_w_out, sib_o, where, "pair_sum_w_out")
    ri = lax.empty((3,) + qi.shape[1:], BF16)
    ro = lax.empty((3,) + qo.shape[1:], BF16)
    sems = _split_start("scatter_start", _scatter_copies, [qi, qo, ri, ro], 6)
    grad_x, dg1 = _grad_x_half(sources, wia, x2, rinv, norm_gain, dout, sems[3], 1, grad_x)
    _, _, got_i, got_o = _split_wait("scatter_wait", _scatter_copies, sems[0], sems[1], sems[2], grad_x)
    jn = _split_start("join_start", _join_copies, [_chip_sum(own_i, got_i, where, "chip_sum_w_in"),
                                                   _chip_sum(own_o, got_o, where, "chip_sum_w_out")], 2)
    small = _all_reduce_small(_pack_small(D, loss, dg0 + dg1, dlogits, dgn, dfg), jn[3])
    loss_sum = small[5, 0]
    d_ng, m_ng, v_ng, grad_norm_gain = _adamw(norm_gain, small[0:1, :], m_norm_gain, v_norm_gain, "adamw_norm_gain")
    d_lb, m_lb, v_lb, grad_lb_logits = _adamw(lb_logits, small[2:4, :SEG], m_lb_logits, v_lb_logits, "adamw_lb_logits")
    d_gn, m_gn, v_gn, grad_hgrn_gnorm = _adamw(hgrn_gnorm, small[4:5, :HGRN_HEAD], m_hgrn_gnorm, v_hgrn_gnorm,
                                               "adamw_hgrn_gnorm")
    d_fg, m_fg, v_fg, grad_final_gain = _adamw(fgain, small[1:2, :], m_final_gain.reshape(1, D),
                                               v_final_gain.reshape(1, D), "adamw_final_gain")
    g_w_in, g_w_out = _split_wait("join_wait", _join_copies, jn[0], jn[1], jn[2], d_fg)
    d_wi, m_wi, v_wi, grad_w_in = _adamw(w_in[0], g_w_in, m_w_in[0], v_w_in[0], "adamw_w_in")
    d_wo, m_wo, v_wo, grad_w_out = _adamw(w_out[0], g_w_out, m_w_out[0], v_w_out[0], "adamw_w_out")

    return (loss_sum, grad_x[None],
            grad_norm_gain, grad_w_in[None], grad_lb_logits, grad_hgrn_gnorm, grad_w_out[None], grad_final_gain[0],
            d_ng, d_wi[None], d_lb, d_gn, d_wo[None], d_fg[0],
            m_ng, m_wi[None], m_lb, m_gn, m_wo[None], m_fg[0],
            v_ng, v_wi[None], v_lb, v_gn, v_wo[None], v_fg[0])
```

```python
import jax
import jax.numpy as jnp
import numpy as np
from jax import lax
from jax.experimental import pallas as pl
from jax.experimental.pallas import tpu as pltpu

F32 = jnp.float32
BF16 = jnp.bfloat16
MESH = pl.DeviceIdType.MESH

NORM_EPS = 1e-6
HGRN_HEAD = 128
HGRN_CHUNK = 64
HGRN_TILE = 128
HGRN_STEP_FWD = (1024, 4)
HGRN_STEP_BWD = (2048, 1)
ATTN_HEAD = 64
LANES = 128
BAND = 128
DILATIONS = (1, 4, 16)
DEINTERLEAVE = 16
ATTN_SCALE = ATTN_HEAD ** -0.5
assert ATTN_SCALE == 0.125
ATTN_BLOCK_ELEMS = BAND * 2048
ATTN_UNROLL = 4
SEG_QKV = 4
SEG_GATE_A = 7
NEG = -1e30

ADAM_LR = 0.001
ADAM_B1 = 0.9
ADAM_B2 = 0.999
ADAM_EPS = 1e-08
ADAM_WD = 0.01
ADAM_STEP = 10

MIB = 1024 * 1024


def _cp(semantics=None, vmem_mib=48):
    return pltpu.CompilerParams(dimension_semantics=semantics, vmem_limit_bytes=vmem_mib * MIB)


def _dot(a, b):
    return jnp.dot(a, b, preferred_element_type=F32)


def _dot_nt(a, b):
    return lax.dot_general(a, b, (((1,), (1,)), ((), ())), preferred_element_type=F32)


def _dot_tn(a, b):
    return lax.dot_general(a, b, (((0,), (0,)), ((), ())), preferred_element_type=F32)


def _split3(x):
    hi = x.astype(BF16)
    r1 = x - hi.astype(F32)
    mid = r1.astype(BF16)
    lo = (r1 - mid.astype(F32)).astype(BF16)
    return hi, mid, lo


def _exact_dot(t_bf16, x):
    hi, mid, lo = _split3(x)
    return _dot(t_bf16, hi) + _dot(t_bf16, mid) + _dot(t_bf16, lo)


def _exact_dot_right(x, t_bf16):
    hi, mid, lo = _split3(x)
    return _dot(hi, t_bf16) + _dot(mid, t_bf16) + _dot(lo, t_bf16)


def _sigmoid(z):
    return jax.nn.sigmoid(z)


def _silu_and_grad(z):
    s = _sigmoid(z)
    return z * s, s * (1.0 + z * (1.0 - s))


def _seg_select(j, values):
    out = values[0]
    for t, v in enumerate(values[1:], 1):
        out = jnp.where(j == t, v, out)
    return out


def _rms_fwd(x2, gain, token):
    S, D = x2.shape
    tm = min(512, S)

    def body(x_ref, g_ref, _, h_ref, r_ref):
        x = x_ref[...]
        r = lax.rsqrt(jnp.mean(x * x, axis=-1, keepdims=True) + NORM_EPS)
        h_ref[...] = ((x * r) * g_ref[...]).astype(BF16)
        r_ref[...] = r

    return pl.pallas_call(
        body, grid=(S // tm,), name="rms_fwd",
        in_specs=[pl.BlockSpec((tm, D), lambda i: (i, 0)), pl.BlockSpec((1, D), lambda i: (0, 0)),
                  pl.BlockSpec(token.shape, lambda i: (0, 0))],
        out_specs=[pl.BlockSpec((tm, D), lambda i: (i, 0)), pl.BlockSpec((tm, 1), lambda i: (i, 0))],
        out_shape=[jax.ShapeDtypeStruct((S, D), BF16), jax.ShapeDtypeStruct((S, 1), F32)],
        compiler_params=_cp(("parallel",)),
    )(x2, gain, token)


def _in_proj(h, w_all, where, segs, z_prev, token, name):
    S, D = h.shape
    SEG = w_all.shape[2] // 2
    NLB = SEG // LANES
    tm = min(1024, S)
    count = len(segs)
    DI = DEINTERLEAVE
    tu = tm // DI

    def is_qkv(seg):
        return (seg >= SEG_QKV) & (seg < SEG_QKV + 3)

    def seg_of(j, w):
        halves = {0: 0, 1: 1, "mine": w[0], "sibling": 1 - w[0]}
        cands = [2 * jnp.bitwise_xor(w[1], rel) + halves[half] for rel, half in segs]
        keys = [is_qkv(s).astype(jnp.int32) for s in cands]
        out = cands[0]
        for k in range(count):
            pos = (sum(jnp.where(keys[t] < keys[k], 1, 0) for t in range(count))
                   + sum(jnp.where(keys[t] == keys[k], 1, 0) for t in range(k)))
            out = jnp.where(pos == j, cands[k], out)
        return out

    def body(*refs):
        where_ref, h_ref, w_ref = refs[:3]
        o_ref, o16_ref = refs[-2:]
        res = _dot(h_ref[...], w_ref[...])
        for p in range(NLB):
            o_ref[p] = res[:, p * LANES:(p + 1) * LANES]

        @pl.when(is_qkv(seg_of(pl.program_id(0), where_ref)))
        def _():
            for p in range(NLB):
                for r in range(DI):
                    o16_ref[p, r] = o_ref.at[p][pl.ds(r, tu, stride=DI), :]

    def z16_map(j, i, w):
        seg = seg_of(j, w)
        return (jnp.where(is_qkv(seg), seg - SEG_QKV, 3), 0, 0, jnp.where(is_qkv(seg), i, 0), 0)

    in_specs = [pl.BlockSpec((tm, D), lambda j, i, w: (i, 0)),
                pl.BlockSpec((None, D, SEG), lambda j, i, w: (seg_of(j, w) // 2, 0, seg_of(j, w) % 2)),
                pl.BlockSpec(token.shape, lambda j, i, w: (0, 0))]
    args = [where, h, w_all, token]
    aliases = {}
    if z_prev is not None:
        in_specs += [ANY, ANY]
        args += list(z_prev)
        aliases = {4: 0, 5: 1}
    grid_spec = pltpu.PrefetchScalarGridSpec(
        num_scalar_prefetch=1, grid=(count, S // tm), in_specs=in_specs,
        out_specs=[pl.BlockSpec((None, NLB, tm, LANES), lambda j, i, w: (seg_of(j, w), 0, i, 0)),
                   pl.BlockSpec((None, NLB, DI, tu, LANES), z16_map)])
    return pl.pallas_call(
        body, grid_spec=grid_spec, name=name,
        out_shape=[jax.ShapeDtypeStruct((8, NLB, S, LANES), F32),
                   jax.ShapeDtypeStruct((4, NLB, DI, S // DI, LANES), F32)],
        input_output_aliases=aliases, compiler_params=_cp(("parallel", "parallel")),
    )(*args)


def _out_proj_loss(yh, ya, w_out, x2, tgt, fgain):
    S, D = x2.shape
    SEG = yh.shape[1]
    tm = min(256, S)
    parts = 2

    def body(yh_ref, ya_ref, w_ref, x_ref, t_ref, fg_ref, dout_ref, doutb_ref, loss_ref, dfg_ref):
        i = pl.program_id(0)

        @pl.when(i == 0)
        def _():
            loss_ref[...] = jnp.zeros_like(loss_ref)
            dfg_ref[...] = jnp.zeros_like(dfg_ref)

        fg = fg_ref[...]
        loss = jnp.zeros((1, 1), F32)
        dfg = jnp.zeros((1, D), F32)
        for rows in [pl.ds(p * (tm // parts), tm // parts) for p in range(parts)]:
            out = (x_ref[rows, :] + _dot(yh_ref[rows, :], w_ref[pl.ds(0, SEG), :])
                   + _dot(ya_ref[rows, :], w_ref[pl.ds(SEG, SEG), :]))
            r = lax.rsqrt(jnp.mean(out * out, axis=-1, keepdims=True) + NORM_EPS)
            n = out * r
            err = n * fg - t_ref[rows, :]
            loss = loss + 0.5 * jnp.sum(jnp.mean(err * err, axis=-1, keepdims=True), axis=0, keepdims=True)
            dy = err * (1.0 / D)
            dfg = dfg + jnp.sum(dy * n, axis=0, keepdims=True)
            dn = dy * fg
            dout = r * (dn - n * jnp.mean(dn * n, axis=-1, keepdims=True))
            dout_ref[rows, :] = dout
            doutb_ref[rows, :] = dout.astype(BF16)
        loss_ref[...] += loss
        dfg_ref[...] += dfg

    row = lambda i: (i, 0)
    fix = lambda i: (0, 0)
    return pl.pallas_call(
        body, grid=(S // tm,), name="out_proj_loss",
        in_specs=[pl.BlockSpec((tm, SEG), row), pl.BlockSpec((tm, SEG), row), pl.BlockSpec((2 * SEG, D), fix),
                  pl.BlockSpec((tm, D), row), pl.BlockSpec((tm, D), row), pl.BlockSpec((1, D), fix)],
        out_specs=[pl.BlockSpec((tm, D), row), pl.BlockSpec((tm, D), row), pl.BlockSpec((1, 1), fix),
                   pl.BlockSpec((1, D), fix)],
        out_shape=[jax.ShapeDtypeStruct((S, D), F32), jax.ShapeDtypeStruct((S, D), BF16),
                   jax.ShapeDtypeStruct((1, 1), F32), jax.ShapeDtypeStruct((1, D), F32)],
        compiler_params=_cp(("arbitrary",)),
    )(yh, ya, w_out, x2, tgt, fgain)


def _dy_proj(doutb, w_out):
    S, D = doutb.shape
    K = w_out.shape[0]
    tm = min(1024, S)

    def body(d_ref, w_ref, o_ref):
        o_ref[...] = _dot_nt(d_ref[...], w_ref[...])

    return pl.pallas_call(
        body, grid=(S // tm,), name="dy_proj",
        in_specs=[pl.BlockSpec((tm, D), lambda i: (i, 0)), pl.BlockSpec((K, D), lambda i: (0, 0))],
        out_specs=pl.BlockSpec((tm, K), lambda i: (i, 0)),
        out_shape=jax.ShapeDtypeStruct((S, K), F32),
        compiler_params=_cp(("parallel",)),
    )(doutb, w_out)


def _grad_w_out(yh, ya, doutb):
    S, SEG = yh.shape
    D = doutb.shape[1]
    R = (2 * SEG) // 4
    nb_half = SEG // R
    tk = min(2048, S)

    def body(yh_ref, ya_ref, d_ref, o_ref):
        q = pl.program_id(0)
        k = pl.program_id(1)

        @pl.when(k == 0)
        def _():
            o_ref[...] = jnp.zeros_like(o_ref)

        @pl.when(q < nb_half)
        def _():
            o_ref[...] += _dot_tn(yh_ref[...], d_ref[...])

        @pl.when(q >= nb_half)
        def _():
            o_ref[...] += _dot_tn(ya_ref[...], d_ref[...])

    return pl.pallas_call(
        body, grid=(4, S // tk), name="grad_w_out",
        in_specs=[pl.BlockSpec((tk, R), lambda q, k: (k, jnp.minimum(q, nb_half - 1))),
                  pl.BlockSpec((tk, R), lambda q, k: (k, jnp.maximum(q - nb_half, 0))),
                  pl.BlockSpec((tk, D), lambda q, k: (k, 0))],
        out_specs=pl.BlockSpec((None, R, D), lambda q, k: (q, 0, 0)),
        out_shape=jax.ShapeDtypeStruct((4, R, D), F32),
        compiler_params=_cp(("parallel", "arbitrary")),
    )(yh, ya, doutb)


def _dz_sources(sources):
    counts = [s.shape[0] for s in sources]
    starts = [sum(counts[:k]) for k in range(len(counts))]
    assert sum(counts) == 8
    return counts, starts


def _row_part(S, part, tile):
    first = max(512, (S * 3 // 8) // 512 * 512)
    rows = first if part == 0 else S - first
    assert rows % tile == 0 and first % tile == 0
    return (0 if part == 0 else first // tile), rows // tile, rows


def _dh_proj(sources, w_all, token, part, name):
    S = sources[0].shape[1]
    D = w_all.shape[1]
    SEG = w_all.shape[2] // 2
    counts, starts = _dz_sources(sources)
    assert all(c % 2 == 0 for c in counts)
    ns = len(sources)
    tm = 1024 if all(_row_part(S, p, 1)[2] % 1024 == 0 for p in (0, 1)) else 512
    t0, nt, nrows = _row_part(S, part, tm)

    def body(*refs):
        src = refs[:ns]
        w_ref, _, o_ref = refs[ns:]
        j = pl.program_id(1)

        @pl.when(j == 0)
        def _():
            o_ref[...] = jnp.zeros_like(o_ref)

        for k in range(ns):
            @pl.when((2 * j >= starts[k]) & (2 * j < starts[k] + counts[k]))
            def _(k=k):
                o_ref[...] += (_dot_nt(src[k][0], w_ref[:, pl.ds(0, SEG)])
                               + _dot_nt(src[k][1], w_ref[:, pl.ds(SEG, SEG)]))

    def src_spec(k):
        return pl.BlockSpec((2, tm, SEG),
                            lambda i, j: (jnp.clip(j - starts[k] // 2, 0, counts[k] // 2 - 1), t0 + i, 0))

    return pl.pallas_call(
        body, grid=(nt, 4), name=name,
        in_specs=[src_spec(k) for k in range(ns)] + [pl.BlockSpec((None, D, 2 * SEG), lambda i, j: (j, 0, 0)),
                                                     pl.BlockSpec(token.shape, lambda i, j: (0, 0))],
        out_specs=pl.BlockSpec((tm, D), lambda i, j: (i, 0)),
        out_shape=jax.ShapeDtypeStruct((nrows, D), F32),
        compiler_params=_cp(("parallel", "arbitrary"), 48 if tm == 512 else 60),
    )(*sources, w_all, token)


def _rms_bwd(dh, x2, rinv, gain, dout, part, gx_prev, name):
    S, D = x2.shape
    tm = 256
    t0, nt, _ = _row_part(S, part, tm)

    def body(dh_ref, x_ref, r_ref, g_ref, dout_ref, *rest):
        gx_ref, dg_ref = rest[-2:]

        @pl.when(pl.program_id(0) == 0)
        def _():
            dg_ref[...] = jnp.zeros_like(dg_ref)

        dh = dh_ref[...]
        r = r_ref[...]
        xhat = x_ref[...] * r
        dg_ref[...] += jnp.sum(dh * xhat, axis=0, keepdims=True)
        dxn = dh * g_ref[...]
        gx_ref[...] = dout_ref[...] + r * (dxn - xhat * jnp.mean(dxn * xhat, axis=-1, keepdims=True))

    row = lambda i: (t0 + i, 0)
    fix = lambda i: (0, 0)
    in_specs = [pl.BlockSpec((tm, D), lambda i: (i, 0)), pl.BlockSpec((tm, D), row), pl.BlockSpec((tm, 1), row),
                pl.BlockSpec((1, D), fix), pl.BlockSpec((tm, D), row)]
    args = [dh, x2, rinv, gain, dout]
    aliases = {}
    if gx_prev is not None:
        in_specs.append(ANY)
        args.append(gx_prev)
        aliases = {5: 0}
    return pl.pallas_call(
        body, grid=(nt,), name=name, in_specs=in_specs,
        out_specs=[pl.BlockSpec((tm, D), row), pl.BlockSpec((1, D), fix)],
        out_shape=[jax.ShapeDtypeStruct((S, D), F32), jax.ShapeDtypeStruct((1, D), F32)],
        input_output_aliases=aliases, compiler_params=_cp(("arbitrary",)),
    )(*args)


def _grad_w_in(h, sources):
    S, D = h.shape
    SEG = sources[0].shape[2]
    counts, starts = _dz_sources(sources)
    ns = len(sources)
    tk = min(2048, S)

    def body(*refs):
        h_ref = refs[0]
        src = refs[1:1 + ns]
        o_ref = refs[1 + ns]
        j = pl.program_id(0)
        k = pl.program_id(1)

        @pl.when(k == 0)
        def _():
            o_ref[...] = jnp.zeros_like(o_ref)

        for s in range(ns):
            @pl.when((j >= starts[s]) & (j < starts[s] + counts[s]))
            def _(s=s):
                o_ref[...] += _dot_tn(h_ref[...], src[s][...])

    def src_spec(s):
        return pl.BlockSpec((None, tk, SEG),
                            lambda j, k: (jnp.clip(j - starts[s], 0, counts[s] - 1), k, 0))

    return pl.pallas_call(
        body, grid=(8, S // tk), name="grad_w_in",
        in_specs=[pl.BlockSpec((tk, D), lambda j, k: (k, 0))] + [src_spec(s) for s in range(ns)],
        out_specs=pl.BlockSpec((None, D, SEG), lambda j, k: (j // 2, 0, j % 2)),
        out_shape=jax.ShapeDtypeStruct((4, D, 2 * SEG), F32),
        compiler_params=_cp(("parallel", "arbitrary"), 48 if tk <= 1024 else 62),
    )(h, *sources)


def _lower_bound(lbl):
    l0 = lbl[0:1, :]
    l1 = lbl[1:2, :]
    m = jnp.maximum(l0, l1)
    e0 = jnp.exp(l0 - m)
    e1 = jnp.exp(l1 - m)
    return e0 / (e0 + e1)


def _tile_masks():
    row = lax.broadcasted_iota(jnp.int32, (HGRN_TILE, HGRN_TILE), 0)
    col = lax.broadcasted_iota(jnp.int32, (HGRN_TILE, HGRN_TILE), 1)
    same = (row // HGRN_CHUNK) == (col // HGRN_CHUNK)
    return same & (row >= col), same & (row <= col)


def _chunk_last(b):
    T = b.shape[0]
    b3 = b.reshape(T // HGRN_CHUNK, HGRN_CHUNK, HGRN_HEAD)
    return jnp.broadcast_to(b3[:, HGRN_CHUNK - 1:HGRN_CHUNK, :], b3.shape).reshape(T, HGRN_HEAD)


def _chunk_sum(x):
    T = x.shape[0]
    x3 = x.reshape(T // HGRN_CHUNK, HGRN_CHUNK, HGRN_HEAD)
    return jnp.broadcast_to(jnp.sum(x3, axis=1, keepdims=True), x3.shape).reshape(T, HGRN_HEAD)


def _hgrn_dims(S, SEG, rows):
    T = min(rows, S)
    assert S % T == 0 and T % HGRN_TILE == 0
    tiles = [slice(t * HGRN_TILE, (t + 1) * HGRN_TILE) for t in range(T // HGRN_TILE)]
    chunks = [slice(c * HGRN_CHUNK, (c + 1) * HGRN_CHUNK) for c in range(T // HGRN_CHUNK)]
    return SEG // HGRN_HEAD, T, T // HGRN_CHUNK, S // T, tiles, chunks


def _hgrn_fwd(zf32, lb_logits, gnorm):
    _, NLB, S, _ = zf32.shape
    SEG = NLB * LANES
    H, T, NC, NJ, tiles, chunks = _hgrn_dims(S, SEG, HGRN_STEP_FWD[0])
    HP = min(HGRN_STEP_FWD[1], H)
    assert H % HP == 0

    def body(zq_ref, zf_ref, zi_ref, zg_ref, lbl_ref, gn_ref, y_ref, st_ref, state):
        @pl.when(pl.program_id(1) == 0)
        def _():
            state[...] = jnp.zeros_like(state)

        tril, _ = _tile_masks()
        tril_bf = tril.astype(BF16)
        for hh in range(HP):
            cols = slice(hh * HGRN_HEAD, (hh + 1) * HGRN_HEAD)
            lb = _lower_bound(lbl_ref[:, cols])
            zq = zq_ref[hh]
            q = zq * _sigmoid(zq)
            f = lb + (1.0 - lb) * _sigmoid(zf_ref[hh])
            k = 1.0 - f
            logf = jnp.log(f)
            b = jnp.concatenate([_exact_dot(tril_bf, logf[t]) for t in tiles], axis=0)
            bl = _chunk_last(b)
            qd_b = (q * jnp.exp(b)).astype(BF16)
            kd_b = (k * jnp.exp(-b)).astype(BF16)
            ke_b = (k * jnp.exp(bl - b)).astype(BF16)
            v_b = zi_ref[hh].astype(BF16)
            o_intra = jnp.concatenate(
                [_dot(jnp.where(tril, _dot_nt(qd_b[t], kd_b[t]), 0.0).astype(BF16), v_b[t]) for t in tiles], axis=0)
            kvs = [_dot_tn(v_b[r], ke_b[r]) for r in chunks]
            ebl = jnp.exp(bl)
            st = state[hh]
            sts = []
            for c in range(NC):
                st_ref[c, hh] = st
                sts.append(st.astype(BF16))
                st = st * ebl[c * HGRN_CHUNK:c * HGRN_CHUNK + 1, :] + kvs[c]
            state[hh] = st
            o = o_intra + jnp.concatenate([_dot_nt(qd_b[r], sb) for r, sb in zip(chunks, sts)], axis=0)
            on = o * lax.rsqrt(jnp.mean(o * o, axis=-1, keepdims=True) + NORM_EPS) * gn_ref[...]
            zg = zg_ref[hh]
            y_ref[:, cols] = (on * (zg * _sigmoid(zg))).astype(BF16)

    def zspec(seg):
        return pl.BlockSpec((None, HP, T, HGRN_HEAD), lambda h, j: (seg, h, j, 0))

    return pl.pallas_call(
        body, grid=(H // HP, NJ), name="hgrn_fwd",
        in_specs=[zspec(0), zspec(1), zspec(2), zspec(3),
                  pl.BlockSpec((2, HP * HGRN_HEAD), lambda h, j: (0, h)),
                  pl.BlockSpec((1, HGRN_HEAD), lambda h, j: (0, 0))],
        out_specs=[pl.BlockSpec((T, HP * HGRN_HEAD), lambda h, j: (j, h)),
                   pl.BlockSpec((NC, HP, HGRN_HEAD, HGRN_HEAD), lambda h, j: (j, h, 0, 0))],
        out_shape=[jax.ShapeDtypeStruct((S, SEG), BF16),
                   jax.ShapeDtypeStruct((S // HGRN_CHUNK, H, HGRN_HEAD, HGRN_HEAD), F32)],
        scratch_shapes=[pltpu.VMEM((HP, HGRN_HEAD, HGRN_HEAD), F32)],
        compiler_params=_cp(("parallel", "arbitrary")),
    )(zf32, zf32, zf32, zf32, lb_logits, gnorm)


def _hgrn_bwd(zf32, lb_logits, gnorm, states, dy):
    _, NLB, S, _ = zf32.shape
    SEG = NLB * LANES
    H, T, NC, NJ, tiles, chunks = _hgrn_dims(S, SEG, HGRN_STEP_BWD[0])
    C = HGRN_CHUNK
    HP = min(HGRN_STEP_BWD[1], H)
    assert H % HP == 0

    def body(zq_ref, zf_ref, zi_ref, zg_ref, lbl_ref, gn_ref, st_ref, dy_ref, dz_ref, dl_ref, dgn_ref, gstate):
        @pl.when(pl.program_id(1) == 0)
        def _():
            gstate[...] = jnp.zeros_like(gstate)
            dl_ref[...] = jnp.zeros_like(dl_ref)
            dgn_ref[...] = jnp.zeros_like(dgn_ref)

        gn = gn_ref[...]
        tril, triu = _tile_masks()
        tril_bf = tril.astype(BF16)
        triu_bf = triu.astype(BF16)
        for hh in range(HP):
            cols = slice(hh * HGRN_HEAD, (hh + 1) * HGRN_HEAD)
            lb = _lower_bound(lbl_ref[:, cols])
            q, dq_dz = _silu_and_grad(zq_ref[hh])
            sf = _sigmoid(zf_ref[hh])
            f = lb + (1.0 - lb) * sf
            k = 1.0 - f
            logf = jnp.log(f)
            b = jnp.concatenate([_exact_dot(tril_bf, logf[t]) for t in tiles], axis=0)
            bl = _chunk_last(b)
            eb = jnp.exp(b)
            enb = jnp.exp(-b)
            ekl = jnp.exp(bl - b)
            ebl = jnp.exp(bl)
            qd = q * eb
            kd = k * enb
            ke = k * ekl
            qd_b = qd.astype(BF16)
            kd_b = kd.astype(BF16)
            ke_b = ke.astype(BF16)
            v_b = zi_ref[hh].astype(BF16)
            sts = [st_ref[c, hh] for c in range(NC)]
            sts_b = [s.astype(BF16) for s in sts]
            a_b = [jnp.where(tril, _dot_nt(qd_b[t], kd_b[t]), 0.0).astype(BF16) for t in tiles]
            o = (jnp.concatenate([_dot(a, v_b[t]) for a, t in zip(a_b, tiles)], axis=0)
                 + jnp.concatenate([_dot_nt(qd_b[r], sb) for r, sb in zip(chunks, sts_b)], axis=0))
            rinv = lax.rsqrt(jnp.mean(o * o, axis=-1, keepdims=True) + NORM_EPS)
            ohat = o * rinv
            sg, dsg = _silu_and_grad(zg_ref[hh])
            dyv = dy_ref[:, cols]
            don = dyv * sg
            dz_ref[3, :, cols] = (dyv * (ohat * gn) * dsg).astype(BF16)
            dgn_ref[hh] += jnp.sum(don * ohat, axis=0, keepdims=True)
            dohat = don * gn
            do = rinv * (dohat - ohat * jnp.mean(dohat * ohat, axis=-1, keepdims=True))
            do_b = do.astype(BF16)
            da_b = [jnp.where(tril, _dot_nt(do_b[t], v_b[t]), 0.0).astype(BF16) for t in tiles]
            dv_intra = jnp.concatenate([_dot_tn(a, do_b[t]) for a, t in zip(a_b, tiles)], axis=0)
            dqd_intra = jnp.concatenate([_dot(da, kd_b[t]) for da, t in zip(da_b, tiles)], axis=0)
            dkd = jnp.concatenate([_dot_tn(da, qd_b[t]) for da, t in zip(da_b, tiles)], axis=0)
            dqd_inter = jnp.concatenate([_dot(do_b[r], sb) for r, sb in zip(chunks, sts_b)], axis=0)
            gks = [_dot_tn(do_b[r], qd_b[r]) for r in chunks]
            g = gstate[hh]
            gs = [None] * NC
            for c in reversed(range(NC)):
                gs[c] = g
                g = g * ebl[c * C:c * C + 1, :] + gks[c]
            gstate[hh] = g
            gs_b = [x.astype(BF16) for x in gs]
            dv = dv_intra + jnp.concatenate([_dot_nt(ke_b[r], gb) for r, gb in zip(chunks, gs_b)], axis=0)
            dz_ref[2, :, cols] = dv.astype(BF16)
            dke = jnp.concatenate([_dot(v_b[r], gb) for r, gb in zip(chunks, gs_b)], axis=0)
            debl = jnp.concatenate(
                [jnp.broadcast_to(jnp.sum(x * s, axis=0, keepdims=True), (C, HGRN_HEAD)) for x, s in zip(gs, sts)], axis=0)
            dqd = dqd_intra + dqd_inter
            dz_ref[0, :, cols] = ((dqd * eb) * dq_dz).astype(BF16)
            t_ke = dke * ke
            db = dqd * qd - dkd * kd - t_ke
            db_last = _chunk_sum(t_ke) + debl * ebl
            dk = dkd * enb + dke * ekl
            dlogf = jnp.concatenate([_exact_dot(triu_bf, db[t]) for t in tiles], axis=0) + db_last
            df = dlogf / f - dk
            dz_ref[1, :, cols] = (df * (1.0 - lb) * (sf * (1.0 - sf))).astype(BF16)
            dlb = jnp.sum(df * (1.0 - sf), axis=0, keepdims=True)
            dl0 = dlb * lb * (1.0 - lb)
            dl_ref[0:1, cols] += dl0
            dl_ref[1:2, cols] -= dl0

    def zspec(seg):
        return pl.BlockSpec((None, HP, T, HGRN_HEAD), lambda h, j: (seg, h, NJ - 1 - j, 0))

    return pl.pallas_call(
        body, grid=(H // HP, NJ), name="hgrn_bwd",
        in_specs=[zspec(0), zspec(1), zspec(2), zspec(3),
                  pl.BlockSpec((2, HP * HGRN_HEAD), lambda h, j: (0, h)),
                  pl.BlockSpec((1, HGRN_HEAD), lambda h, j: (0, 0)),
                  pl.BlockSpec((NC, HP, HGRN_HEAD, HGRN_HEAD), lambda h, j: (NJ - 1 - j, h, 0, 0)),
                  pl.BlockSpec((T, HP * HGRN_HEAD), lambda h, j: (NJ - 1 - j, h))],
        out_specs=[pl.BlockSpec((4, T, HP * HGRN_HEAD), lambda h, j: (0, NJ - 1 - j, h)),
                   pl.BlockSpec((2, HP * HGRN_HEAD), lambda h, j: (0, h)),
                   pl.BlockSpec((HP, 1, HGRN_HEAD), lambda h, j: (h, 0, 0))],
        out_shape=[jax.ShapeDtypeStruct((4, S, SEG), BF16), jax.ShapeDtypeStruct((2, SEG), F32),
                   jax.ShapeDtypeStruct((H, 1, HGRN_HEAD), F32)],
        scratch_shapes=[pltpu.VMEM((HP, HGRN_HEAD, HGRN_HEAD), F32)],
        compiler_params=_cp(("parallel", "arbitrary")),
    )(zf32, zf32, zf32, zf32, lb_logits, gnorm, states, dy)


def _alibi_slopes(seg):
    n_heads = seg // ATTN_HEAD
    s = 2.0 ** (-8.0 * np.arange(1, n_heads + 1, dtype=np.float64) / n_heads)
    return jnp.asarray(np.repeat(s, ATTN_HEAD)[None, :], F32)


def _attn_dims(S, SEG, d):
    rb = BAND * d
    assert S % rb == 0 and SEG % LANES == 0
    npb = max(1, min(SEG // LANES, ATTN_BLOCK_ELEMS // (rb * LANES)))
    assert (SEG // LANES) % npb == 0
    return rb, npb, S // rb, (SEG // LANES) // npb


def _res_rows(r, d):
    return pl.ds(0, BAND) if d == 1 else pl.ds(r, BAND, stride=d)


def _for_residues(d, fn):
    if d == 1:
        fn(0)
    else:
        def step(r, carry):
            fn(r)
            return carry
        lax.fori_loop(0, d, step, 0, unroll=ATTN_UNROLL)


def _for_groups(d, n_pairs, fn):
    def over_pairs(r):
        for g0 in range(0, n_pairs, ATTN_UNROLL):
            fn([(r, p) for p in range(g0, min(n_pairs, g0 + ATTN_UNROLL))])

    if d == 1:
        over_pairs(0)
    elif n_pairs >= ATTN_UNROLL:
        def step(r, carry):
            over_pairs(r)
            return carry
        lax.fori_loop(0, d, step, 0)
    else:
        per_group = ATTN_UNROLL // n_pairs
        assert d % per_group == 0

        def step(g, carry):
            fn([(g * per_group + i, p) for i in range(per_group) for p in range(n_pairs)])
            return carry
        lax.fori_loop(0, d // per_group, step, 0)


def _band_terms(n, d):
    i = lax.broadcasted_iota(jnp.int32, (BAND, 2 * BAND), 0)
    jj = lax.broadcasted_iota(jnp.int32, (BAND, 2 * BAND), 1)
    delta = BAND + i - jj
    valid = (delta >= 0) & (delta <= BAND) & ((n > 0) | (jj >= BAND))
    return (-d * delta).astype(F32), valid


def _head_biases(slopes, nd, valid):
    out = []
    for s in _per_head(slopes):
        s2 = jnp.concatenate([s, s], axis=1)
        out.append(jnp.where(valid, s2 * nd, NEG))
    return jnp.concatenate(out, axis=0)


def _stack_heads(x):
    lane = lax.broadcasted_iota(jnp.int32, x.shape, 1)
    zero = jnp.zeros_like(x)
    return jnp.concatenate([jnp.where(lane < ATTN_HEAD, x, zero), jnp.where(lane < ATTN_HEAD, zero, x)], axis=0)


def _unstack_heads(x2):
    first = lax.broadcasted_iota(jnp.int32, (BAND, LANES), 1) < ATTN_HEAD
    return jnp.where(first, x2[:BAND], x2[BAND:])


def _stack_per_head(x):
    a, b = _per_head(x)
    col = jnp.concatenate([a, b], axis=0)
    return jnp.concatenate([col, col], axis=1)


def _per_head(x):
    lane = lax.broadcasted_iota(jnp.int32, x.shape, 1)
    sw = pltpu.roll(x, ATTN_HEAD, 1)
    first = lane < ATTN_HEAD
    return jnp.where(first, x, sw), jnp.where(first, sw, x)


def _qkv_source(zz, d):
    z, z16 = zz
    if d == DEINTERLEAVE:
        def take(ref, p, r):
            return ref.at[p][r]

        def spec(seg, np_, row_block):
            return pl.BlockSpec((None, np_, d, BAND, LANES), lambda c, n: (seg, c, 0, row_block(c, n), 0))
        return z, z16, take, spec

    def take(ref, p, r):
        return ref.at[p][_res_rows(r, d), :]

    def spec(seg, np_, row_block):
        return pl.BlockSpec((None, np_, BAND * d, LANES), lambda c, n: (SEG_QKV + seg, c, row_block(c, n), 0))
    return z, z, take, spec


def _attn_fwd(qkv, slopes, d):
    qkv, src, take, spec = _qkv_source(qkv, d)
    _, NLB, S, _ = qkv.shape
    rb, NP, nb, ncb = _attn_dims(S, NLB * LANES, d)

    def body(q_ref, kc_ref, vc_ref, sl_ref, o_ref, l_ref, kp_ref, vp_ref):
        n = pl.program_id(1)

        @pl.when(n == 0)
        def _():
            kp_ref[...] = jnp.zeros_like(kp_ref)
            vp_ref[...] = jnp.zeros_like(vp_ref)

        nd, valid = _band_terms(n, d)
        biases = [_head_biases(sl_ref[:, p * LANES:(p + 1) * LANES], nd, valid) for p in range(NP)]

        def group(items):
            scores, values = [], []
            for r, p in items:
                kc = jnp.concatenate([take(kp_ref, p, r), take(kc_ref, p, r)], axis=0).astype(BF16)
                values.append(jnp.concatenate([take(vp_ref, p, r), take(vc_ref, p, r)], axis=0).astype(BF16))
                scores.append(_dot_nt(_stack_heads((take(q_ref, p, r) * ATTN_SCALE).astype(BF16)), kc))
            probs = []
            for (r, p), s in zip(items, scores):
                s = s + biases[p]
                m = jnp.max(s, axis=-1, keepdims=True)
                e = jnp.exp(s - m)
                den = jnp.sum(e, axis=-1, keepdims=True)
                probs.append((e.astype(BF16), den, m + jnp.log(den)))
            for (r, p), vc, (e, den, lse) in zip(items, values, probs):
                rows = _res_rows(r, d)
                o_ref.at[p][rows, :] = _unstack_heads(_dot(e, vc) / den)
                l_ref.at[p][rows, :] = _unstack_heads(jnp.broadcast_to(lse, (2 * BAND, LANES)))

        _for_groups(d, NP, group)
        kp_ref[...] = kc_ref[...]
        vp_ref[...] = vc_ref[...]

    cur = lambda c, n: n
    out = pl.BlockSpec((NP, rb, LANES), lambda c, n: (c, n, 0))
    kv_block = spec(1, NP, cur).block_shape[1:]
    return pl.pallas_call(
        body, grid=(ncb, nb), name=f"attn_fwd_d{d}",
        in_specs=[spec(0, NP, cur), spec(1, NP, cur), spec(2, NP, cur),
                  pl.BlockSpec((1, NP * LANES), lambda c, n: (0, c))],
        out_specs=[out, out],
        out_shape=[jax.ShapeDtypeStruct((NLB, S, LANES), F32)] * 2,
        scratch_shapes=[pltpu.VMEM(kv_block, F32), pltpu.VMEM(kv_block, F32)],
        compiler_params=_cp(("parallel", "arbitrary")),
    )(src, src, src, slopes)


def _attn_merge(outs, lses, zf32):
    NLB, S, _ = outs[0].shape
    SEG = NLB * LANES
    tm = min(256, S)

    def body(o1, o2, o3, l1, l2, l3, zg_ref, o_ref, lse_ref, y_ref):
        a, b, c = l1[...], l2[...], l3[...]
        m = jnp.maximum(jnp.maximum(a, b), c)
        ea, eb, ec = jnp.exp(a - m), jnp.exp(b - m), jnp.exp(c - m)
        tot = ea + eb + ec
        o = (ea / tot) * o1[...] + (eb / tot) * o2[...] + (ec / tot) * o3[...]
        o_ref[...] = o
        lse_ref[...] = m + jnp.log(tot)
        zg = zg_ref[...]
        y = (o * (zg * _sigmoid(zg))).astype(BF16)
        for p in range(NLB):
            y_ref[:, p * LANES:(p + 1) * LANES] = y[p]

    blk = pl.BlockSpec((NLB, tm, LANES), lambda i: (0, i, 0))
    return pl.pallas_call(
        body, grid=(S // tm,), name="attn_merge",
        in_specs=[blk] * 6 + [pl.BlockSpec((None, NLB, tm, LANES), lambda i: (SEG_GATE_A, 0, i, 0))],
        out_specs=[blk, blk, pl.BlockSpec((tm, SEG), lambda i: (i, 0))],
        out_shape=[jax.ShapeDtypeStruct((NLB, S, LANES), F32), jax.ShapeDtypeStruct((NLB, S, LANES), F32),
                   jax.ShapeDtypeStruct((S, SEG), BF16)],
        compiler_params=_cp(("parallel",)),
    )(*outs, *lses, zf32)


def _attn_gate_bwd(dy, o, zf32):
    NP, S, _ = o.shape
    SEG = NP * LANES
    tm = min(256, S)

    def body(dy_ref, o_ref, zg_ref, do_ref, dl_ref, dzg_ref):
        r = lax.broadcasted_iota(jnp.int32, (LANES, LANES), 0) // ATTN_HEAD
        c = lax.broadcasted_iota(jnp.int32, (LANES, LANES), 1) // ATTN_HEAD
        same_head = (r == c).astype(BF16)
        for p in range(NP):
            cols = slice(p * LANES, (p + 1) * LANES)
            sg, dsg = _silu_and_grad(zg_ref[p])
            dyv = dy_ref[:, cols]
            ov = o_ref[p]
            do = dyv * sg
            do_ref[p] = do
            dzg_ref[:, cols] = (dyv * ov * dsg).astype(BF16)
            dl_ref[p] = _exact_dot_right(do * ov, same_head)

    blk = pl.BlockSpec((NP, tm, LANES), lambda i: (0, i, 0))
    return pl.pallas_call(
        body, grid=(S // tm,), name="attn_gate_bwd",
        in_specs=[pl.BlockSpec((tm, SEG), lambda i: (i, 1)), blk,
                  pl.BlockSpec((None, NP, tm, LANES), lambda i: (SEG_GATE_A, 0, i, 0))],
        out_specs=[blk, blk, pl.BlockSpec((None, tm, SEG), lambda i: (3, i, 0))],
        out_shape=[jax.ShapeDtypeStruct((NP, S, LANES), F32), jax.ShapeDtypeStruct((NP, S, LANES), F32),
                   jax.ShapeDtypeStruct((4, S, SEG), BF16)],
        compiler_params=_cp(("parallel",)),
    )(dy, o, zf32)


def _attn_bwd(qkv, slopes, do, lse, dl, d, acc, into):
    qkv, src, take, spec = _qkv_source(qkv, d)
    _, NLB, S, _ = qkv.shape
    SEG = NLB * LANES
    rb, NP, nb, ncb = _attn_dims(S, SEG, d)
    has_acc = acc is not None
    out_dtype = F32 if into is None else into.dtype
    assert into is None or d == 1

    def body(*refs):
        q_ref, kc_ref, vc_ref, sl_ref, do_ref, lse_ref, dl_ref = refs[:7]
        acc_ref = refs[7] if has_acc else None
        out_ref, cq, ck, cv, kp_ref, vp_ref = refs[-6:]
        n = pl.program_id(1)

        def emit(r, p, dq, dk, dv):
            rows = _res_rows(r, d)
            for t, val in enumerate((dq, dk, dv)):
                if has_acc:
                    val = val + acc_ref.at[t].at[p][rows, :]
                if into is None:
                    out_ref.at[t].at[p][rows, :] = val.astype(out_dtype)
                else:
                    out_ref.at[t][rows, p * LANES:(p + 1) * LANES] = val.astype(out_dtype)

        @pl.when(n == 0)
        def _():
            cq[...] = jnp.zeros_like(cq)
            ck[...] = jnp.zeros_like(ck)
            cv[...] = jnp.zeros_like(cv)
            kp_ref[...] = jnp.zeros_like(kp_ref)
            vp_ref[...] = jnp.zeros_like(vp_ref)

        @pl.when(n < nb)
        def _():
            nd, valid = _band_terms(n, d)
            biases = [_head_biases(sl_ref[:, p * LANES:(p + 1) * LANES], nd, valid) for p in range(NP)]

            def group(items):
                first = []
                for r, p in items:
                    rows = _res_rows(r, d)
                    kc = jnp.concatenate([take(kp_ref, p, r), take(kc_ref, p, r)], axis=0).astype(BF16)
                    vc = jnp.concatenate([take(vp_ref, p, r), take(vc_ref, p, r)], axis=0).astype(BF16)
                    qs = _stack_heads((take(q_ref, p, r) * ATTN_SCALE).astype(BF16))
                    dos = _stack_heads(do_ref.at[p][rows, :].astype(BF16))
                    first.append((kc, qs, dos, _dot_nt(qs, kc), _dot_nt(dos, vc)))
                second = []
                for (r, p), (kc, qs, dos, s, dp) in zip(items, first):
                    rows = _res_rows(r, d)
                    pr = jnp.exp(s + biases[p] - _stack_per_head(lse_ref.at[p][rows, :]))
                    ds = (pr * (dp - _stack_per_head(dl_ref.at[p][rows, :]))).astype(BF16)
                    second.append((kc, qs, dos, pr.astype(BF16), ds))
                for (r, p), (kc, qs, dos, pr, ds) in zip(items, second):
                    dq = _unstack_heads(_dot(ds, kc)) * ATTN_SCALE
                    dk = _dot_tn(ds, qs)
                    dv = _dot_tn(pr, dos)
                    emit(r, p, cq[r, p], ck[r, p] + dk[:BAND, :], cv[r, p] + dv[:BAND, :])
                    cq[r, p] = dq
                    ck[r, p] = dk[BAND:, :]
                    cv[r, p] = dv[BAND:, :]

            _for_groups(d, NP, group)
            kp_ref[...] = kc_ref[...]
            vp_ref[...] = vc_ref[...]

        @pl.when(n == nb)
        def _():
            def last(r):
                for p in range(NP):
                    emit(r, p, cq[r, p], ck[r, p], cv[r, p])
            _for_residues(d, last)

    cur = lambda c, n: (c, jnp.minimum(n, nb - 1), 0)
    lag = lambda c, n: (0, c, jnp.clip(n - 1, 0, nb - 1), 0)

    at = lambda c, n: jnp.minimum(n, nb - 1)
    in_specs = [spec(0, NP, at), spec(1, NP, at), spec(2, NP, at),
                pl.BlockSpec((1, NP * LANES), lambda c, n: (0, c))] + [pl.BlockSpec((NP, rb, LANES), cur)] * 3
    kv_block = in_specs[1].block_shape[1:]
    args = [src, src, src, slopes, do, lse, dl]
    aliases = {}
    if has_acc:
        in_specs.append(pl.BlockSpec((3, NP, rb, LANES), lag))
        args.append(acc)
        if into is None:
            aliases = {7: 0}
    if into is None:
        out_sds = jax.ShapeDtypeStruct((3, NLB, S, LANES), F32)
        out_spec = pl.BlockSpec((3, NP, rb, LANES), lag)
    else:
        in_specs.append(ANY)
        args.append(into)
        aliases = {len(args) - 1: 0}
        out_sds = jax.ShapeDtypeStruct(into.shape, into.dtype)
        out_spec = pl.BlockSpec((3, rb, NP * LANES), lambda c, n: (0, jnp.clip(n - 1, 0, nb - 1), c))
    return pl.pallas_call(
        body, grid=(ncb, nb + 1), name=f"attn_bwd_d{d}",
        in_specs=in_specs, out_specs=out_spec, out_shape=out_sds,
        scratch_shapes=[pltpu.VMEM((d, NP, BAND, LANES), F32)] * 3 + [pltpu.VMEM(kv_block, F32)] * 2,
        input_output_aliases=aliases,
        compiler_params=_cp(("parallel", "arbitrary")),
    )(*args)


def _adamw(w, g, m, v, name):
    R, C = w.shape
    tr = R if R <= 256 else 256
    assert R % tr == 0

    def body(w_ref, g_ref, m_ref, v_ref, d_ref, nm_ref, nv_ref, go_ref):
        g = g_ref[...]
        nm = ADAM_B1 * m_ref[...] + (1.0 - ADAM_B1) * g
        nv = ADAM_B2 * v_ref[...] + (1.0 - ADAM_B2) * (g * g)
        m_hat = nm / (1.0 - ADAM_B1 ** ADAM_STEP)
        v_hat = nv / (1.0 - ADAM_B2 ** ADAM_STEP)
        d_ref[...] = -ADAM_LR * (m_hat / (jnp.sqrt(v_hat) + ADAM_EPS) + ADAM_WD * w_ref[...])
        nm_ref[...] = nm
        nv_ref[...] = nv
        go_ref[...] = g

    blk = pl.BlockSpec((tr, C), lambda i: (i, 0))
    sds = jax.ShapeDtypeStruct((R, C), F32)
    return pl.pallas_call(
        body, grid=(R // tr,), name=name, in_specs=[blk] * 4, out_specs=[blk] * 4, out_shape=[sds] * 4,
        compiler_params=_cp(("parallel",)),
    )(w, g, m, v)


def _coords():
    return lax.axis_index("x"), lax.axis_index("y"), lax.axis_index("c")


def _other_chips(x, y):
    return [(1 - x, y), (x, 1 - y), (1 - x, 1 - y)]


ANY = pl.BlockSpec(memory_space=pl.ANY)


def _cast_into_slot(w, where, name):
    R, C = w.shape
    tr = min(256, R)

    def body(where_ref, w_ref, o_ref):
        o_ref[...] = w_ref[...].astype(BF16)

    grid_spec = pltpu.PrefetchScalarGridSpec(
        num_scalar_prefetch=1, grid=(R // tr,),
        in_specs=[pl.BlockSpec((tr, C), lambda i, w: (i, 0))],
        out_specs=pl.BlockSpec((None, tr, C), lambda i, w: (w[1], i, 0)))
    return pl.pallas_call(
        body, grid_spec=grid_spec, name=name, out_shape=jax.ShapeDtypeStruct((4, R, C), BF16),
        compiler_params=_cp(("parallel",)),
    )(where, w)


def _pair_sum(g, sib, where, name):
    _, n2, C = g.shape
    N = n2 // 2
    tr = min(256, N)
    nt = N // tr

    def body(where_ref, g_ref, s_ref, qb_ref, own_ref):
        q = pl.program_id(1)
        tot = g_ref[...] + s_ref[...]
        qb_ref[...] = tot.astype(BF16)

        @pl.when(q == where_ref[1])
        def _():
            own_ref[...] = tot

    grid_spec = pltpu.PrefetchScalarGridSpec(
        num_scalar_prefetch=1, grid=(nt, 4),
        in_specs=[pl.BlockSpec((None, tr, C), lambda i, q, w: (q, w[0] * nt + i, 0)),
                  pl.BlockSpec((None, tr, C), lambda i, q, w: (q, i, 0))],
        out_specs=[pl.BlockSpec((None, tr, C), lambda i, q, w: (q, i, 0)),
                   pl.BlockSpec((tr, C), lambda i, q, w: (i, 0))])
    return pl.pallas_call(
        body, grid_spec=grid_spec, name=name,
        out_shape=[jax.ShapeDtypeStruct((4, N, C), BF16), jax.ShapeDtypeStruct((N, C), F32)],
        compiler_params=_cp(("parallel", "arbitrary")),
    )(where, g, sib)


HBM = pl.BlockSpec(memory_space=pltpu.HBM)
SEM = pl.BlockSpec(memory_space=pltpu.SEMAPHORE)


def _in_hbm(a):
    return pltpu.with_memory_space_constraint(a, pltpu.HBM)


def _split_start(name, copies, arrays, n_sems, after=None):
    n = len(arrays)

    def body(*refs):
        for cp in copies(refs[:n], refs[-n - 3], refs[-n - 2]):
            cp.start()
        refs[-1][...] = jnp.zeros_like(refs[-1])

    ordered = () if after is None else (after,)
    outs = pl.pallas_call(
        body, name=name,
        out_shape=(pltpu.SemaphoreType.DMA((n_sems,)), pltpu.SemaphoreType.DMA((n_sems,)),
                   *[pltpu.HBM(a.shape, a.dtype) for a in arrays], jax.ShapeDtypeStruct((8, LANES), F32)),
        in_specs=(HBM,) * n + (ANY,) * len(ordered),
        out_specs=(SEM, SEM) + (HBM,) * n + (pl.BlockSpec(memory_space=pltpu.VMEM),),
        input_output_aliases={i: 2 + i for i in range(n)},
        compiler_params=pltpu.CompilerParams(has_side_effects=pltpu.SideEffectType.DATAFLOW_SIDE_EFFECTING),
    )(*[_in_hbm(a) for a in arrays], *ordered)
    return outs[0], outs[1], list(outs[2:2 + n]), outs[-1]


def _split_wait(name, copies, send_sems, recv_sems, arrays, after):
    n = len(arrays)

    def body(*refs):
        for cp in copies(refs[:n], refs[n], refs[n + 1]):
            cp.wait_send()
            cp.wait_recv()

    outs = pl.pallas_call(
        body, name=name,
        out_shape=tuple(pltpu.HBM(a.shape, a.dtype) for a in arrays),
        in_specs=(HBM,) * n + (SEM, SEM, ANY), out_specs=(HBM,) * n,
        input_output_aliases={i: i for i in range(n)},
        compiler_params=pltpu.CompilerParams(has_side_effects=pltpu.SideEffectType.DATAFLOW_SIDE_EFFECTING),
    )(*arrays, send_sems, recv_sems, after)
    return list(outs)


def _remote(src, dst, sems, k, to):
    send_sems, recv_sems = sems
    return pltpu.make_async_remote_copy(src_ref=src, dst_ref=dst, send_sem=send_sems.at[k], recv_sem=recv_sems.at[k],
                                        device_id=to, device_id_type=MESH)


def _chip_at(x, y, rel):
    px = 1 - x if rel & 2 else x
    py = 1 - y if rel & 1 else y
    return px, py, 2 * px + py


def _gather_in_copies(rels):
    def copies(refs, send_sems, recv_sems):
        (w,) = refs
        x, y, c = _coords()
        seg = w.shape[2] // 2
        mine = w.at[2 * x + y, :, pl.ds(c * seg, seg)]
        return [_remote(mine, mine, (send_sems, recv_sems), k, _chip_at(x, y, rel)[:2] + (c,))
                for k, rel in enumerate(rels)]
    return copies


def _gather_out_copies(refs, send_sems, recv_sems):
    (w,) = refs
    x, y, c = _coords()
    mine = w.at[2 * x + y]
    return [_remote(mine, mine, (send_sems, recv_sems), k, (px, py, c)) for k, (px, py) in enumerate(_other_chips(x, y))]


def _swap_copies(refs, send_sems, recv_sems):
    gi, go, si, so = refs
    x, y, c = _coords()
    cps = []
    for a, (src, dst) in enumerate(((gi, si), (go, so))):
        nr = dst.shape[1]
        cps.append(_remote(src.at[:, pl.ds((1 - c) * nr, nr), :], dst, (send_sems, recv_sems), a, (x, y, 1 - c)))
    return cps


def _scatter_copies(refs, send_sems, recv_sems):
    qi, qo, ri, ro = refs
    x, y, c = _coords()
    cps = []
    for k, (px, py) in enumerate(_other_chips(x, y)):
        for a, (src, dst) in enumerate(((qi, ri), (qo, ro))):
            cps.append(_remote(src.at[2 * px + py], dst.at[k], (send_sems, recv_sems), 2 * k + a, (px, py, c)))
    return cps


def _forward_copies(rels):
    def copies(refs, send_sems, recv_sems):
        (w,) = refs
        x, y, c = _coords()
        seg = w.shape[2] // 2
        cps = []
        for k, rel in enumerate(rels):
            got = w.at[_chip_at(x, y, rel)[2], :, pl.ds(c * seg, seg)]
            cps.append(_remote(got, got, (send_sems, recv_sems), k, (x, y, 1 - c)))
        return cps
    return copies


def _chip_sum(own, got, where, name):
    N, C = own.shape
    tr = min(256, N)
    nt = N // tr

    def body(where_ref, own_ref, got_ref, o_ref):
        t = own_ref[...]
        for k in range(3):
            t = t + got_ref[k].astype(F32)
        o_ref[...] = t

    grid_spec = pltpu.PrefetchScalarGridSpec(
        num_scalar_prefetch=1, grid=(nt,),
        in_specs=[pl.BlockSpec((tr, C), lambda i, w: (i, 0)), pl.BlockSpec((3, tr, C), lambda i, w: (0, i, 0))],
        out_specs=pl.BlockSpec((tr, C), lambda i, w: (w[0] * nt + i, 0)))
    return pl.pallas_call(
        body, grid_spec=grid_spec, name=name, out_shape=jax.ShapeDtypeStruct((2 * N, C), F32),
        compiler_params=_cp(("parallel",)),
    )(where, own, got)


def _join_copies(refs, send_sems, recv_sems):
    x, y, c = _coords()
    cps = []
    for a, ref in enumerate(refs):
        nr = ref.shape[0] // 2
        mine = ref.at[pl.ds(c * nr, nr), :]
        cps.append(_remote(mine, mine, (send_sems, recv_sems), a, (x, y, 1 - c)))
    return cps


def _all_reduce_small(part, token):
    R, C = part.shape

    def body(p_ref, _, o_ref, slots, send_sems, recv_sems):
        x, y, c = _coords()
        me = 4 * x + 2 * y + c
        slots[me] = p_ref[...]
        cps = []
        for k in range(1, 8):
            fx, fy, fc = (k >> 2) & 1, (k >> 1) & 1, k & 1
            peer = (1 - x if fx else x, 1 - y if fy else y, 1 - c if fc else c)
            cp = pltpu.make_async_remote_copy(src_ref=p_ref, dst_ref=slots.at[me], send_sem=send_sems.at[k - 1],
                                              recv_sem=recv_sems.at[k - 1], device_id=peer, device_id_type=MESH)
            cp.start()
            cps.append(cp)
        for cp in cps:
            cp.wait()
        t = slots[0]
        for k in range(1, 8):
            t = t + slots[k]
        o_ref[...] = t

    vm = pl.BlockSpec(memory_space=pltpu.VMEM)
    return pl.pallas_call(
        body, name="all_reduce_small", in_specs=[vm, vm], out_specs=vm,
        out_shape=jax.ShapeDtypeStruct((R, C), F32),
        scratch_shapes=[pltpu.VMEM((8, R, C), F32), pltpu.SemaphoreType.DMA((7,)), pltpu.SemaphoreType.DMA((7,))],
    )(part, token)


def _mixers_forward(z, lb_logits, hgrn_gnorm):
    slopes = _alibi_slopes(z[0].shape[1] * LANES)
    yh, states = _hgrn_fwd(z[0], lb_logits, hgrn_gnorm)
    outs, lses = [], []
    for d in DILATIONS:
        o, l = _attn_fwd(z, slopes, d)
        outs.append(o)
        lses.append(l)
    o_attn, lse, ya = _attn_merge(outs, lses, z[0])
    return yh, ya, (states, o_attn, lse, slopes)


def _backward_to_dz(z, kept, lb_logits, hgrn_gnorm, yh, ya, w_out_all, x2, tgt, fgain, h):
    states, o_attn, lse, slopes = kept
    dout, doutb, loss, dfg = _out_proj_loss(yh, ya, w_out_all, x2, tgt, fgain)
    dy = _dy_proj(doutb, w_out_all)
    g_w_out = _grad_w_out(yh, ya, doutb)
    dzh, dlogits, dgn = _hgrn_bwd(z[0], lb_logits, hgrn_gnorm, states, dy)
    do, dl, dza = _attn_gate_bwd(dy, o_attn, z[0])
    acc = None
    order = sorted(DILATIONS, reverse=True)
    for d in order[:-1]:
        acc = _attn_bwd(z, slopes, do, lse, dl, d, acc, None)
    dza = _attn_bwd(z, slopes, do, lse, dl, order[-1], acc, dza)
    sources = [dzh, dza]
    g_w_in = _grad_w_in(h, sources)
    return loss, dfg, dlogits, dgn, g_w_out, g_w_in, sources, dout


def _grad_x_half(sources, w_all, x2, rinv, norm_gain, dout, token, part, gx_prev):
    dh = _dh_proj(sources, w_all, token, part, f"dh_proj_{part}")
    return _rms_bwd(dh, x2, rinv, norm_gain, dout, part, gx_prev, f"rms_bwd_{part}")


def _local_step(x2, tgt, norm_gain, w_all, lb_logits, hgrn_gnorm, w_out_all, fgain):
    token = jnp.zeros((8, LANES), F32)
    where = jnp.zeros((2,), jnp.int32)
    h, rinv = _rms_fwd(x2, norm_gain, token)
    z = _in_proj(h, w_all, where, [(rel, half) for rel in range(4) for half in range(2)], None, token, "in_proj_all")
    yh, ya, kept = _mixers_forward(z, lb_logits, hgrn_gnorm)
    loss, dfg, dlogits, dgn, g_w_out, g_w_in, sources, dout = _backward_to_dz(
        z, kept, lb_logits, hgrn_gnorm, yh, ya, w_out_all, x2, tgt, fgain, h)
    gx, dg0 = _grad_x_half(sources, w_all, x2, rinv, norm_gain, dout, token, 0, None)
    gx, dg1 = _grad_x_half(sources, w_all, x2, rinv, norm_gain, dout, token, 1, gx)
    return loss, gx, dg0 + dg1, g_w_in, dlogits, dgn, g_w_out, dfg


def _pack_small(D, loss, dgain, dlogits, dgn, dfg):
    def row(v):
        v = v.reshape(1, -1)
        return jnp.pad(v, ((0, 0), (0, D - v.shape[1])))
    rows = [row(dgain), row(dfg), row(dlogits[0]), row(dlogits[1]), row(jnp.sum(dgn, axis=0)), row(loss)]
    rows += [jnp.zeros((1, D), F32)] * (8 - len(rows))
    return jnp.concatenate(rows, axis=0)


def kernel(x, norm_gain, w_in, lb_logits, hgrn_gnorm, w_out, final_gain, loss_target, m_norm_gain, m_w_in, m_lb_logits, m_hgrn_gnorm, m_w_out, m_final_gain, v_norm_gain, v_w_in, v_lb_logits, v_hgrn_gnorm, v_w_out, v_final_gain):
    _, S, D = x.shape
    SEG = w_in.shape[2] // 2
    x2 = x[0]
    tgt = loss_target[0]
    fgain = final_gain.reshape(1, D)
    where = jnp.stack([lax.axis_index("c"), 2 * lax.axis_index("x") + lax.axis_index("y")]).astype(jnp.int32)

    wia = _cast_into_slot(w_in[0], where, "cast_w_in")
    woa = _cast_into_slot(w_out[0], where, "cast_w_out")
    near, far = (2, 1), (3,)
    ga = _split_start("gather_near_start", _gather_in_copies(near), [wia], 2)
    h, rinv = _rms_fwd(x2, norm_gain, ga[3])
    z = _in_proj(h, ga[2][0], where, [(0, 0), (0, 1)], None, ga[3], "in_proj_own")
    (wia,) = _split_wait("gather_near_wait", _gather_in_copies(near), ga[0], ga[1], ga[2], z[0])
    gb = _split_start("gather_far_start", _gather_in_copies(far), [wia], 1)
    fa = _split_start("forward_near_start", _forward_copies(near), gb[2], 2, after=gb[3])
    z = _in_proj(h, fa[2][0], where, [(2, "mine"), (1, "mine")], z, fa[3], "in_proj_near")
    (wia,) = _split_wait("forward_near_wait", _forward_copies(near), fa[0], fa[1], fa[2], z[0])
    (wia,) = _split_wait("gather_far_wait", _gather_in_copies(far), gb[0], gb[1], [wia], z[0])
    out_sems = _split_start("gather_out_start", _gather_out_copies, [woa], 3, after=wia)
    fb = _split_start("forward_far_start", _forward_copies(far), [wia], 1, after=out_sems[3])
    z = _in_proj(h, fb[2][0], where, [(3, "mine"), (2, "sibling"), (1, "sibling")], z, fb[3], "in_proj_far")
    (wia,) = _split_wait("forward_far_wait", _forward_copies(far), fb[0], fb[1], fb[2], z[0])
    z = _in_proj(h, wia, where, [(3, "sibling")], z, fb[3], "in_proj_last")
    yh, ya, kept = _mixers_forward(z, lb_logits, hgrn_gnorm)
    (woa,) = _split_wait("gather_out_wait", _gather_out_copies, out_sems[0], out_sems[1], out_sems[2], ya)
    w_out_all = woa.reshape(2 * SEG, D)

    loss, dfg, dlogits, dgn, g_w_out, g_w_in, sources, dout = _backward_to_dz(
        z, kept, lb_logits, hgrn_gnorm, yh, ya, w_out_all, x2, tgt, fgain, h)

    sib_i = lax.empty((4, g_w_in.shape[1] // 2, g_w_in.shape[2]), F32)
    sib_o = lax.empty((4, g_w_out.shape[1] // 2, g_w_out.shape[2]), F32)
    sems = _split_start("swap_start", _swap_copies, [g_w_in, g_w_out, sib_i, sib_o], 2)
    grad_x, dg0 = _grad_x_half(sources, wia, x2, rinv, norm_gain, dout, sems[3], 0, None)
    g_w_in, g_w_out, sib_i, sib_o = _split_wait("swap_wait", _swap_copies, sems[0], sems[1], sems[2], grad_x)
    qi, own_i = _pair_sum(g_w_in, sib_i, where, "pair_sum_w_in")
    qo, own_o = _pair_sum(g_w_out, sib_o, where, "pair_sum_w_out")
    ri = lax.empty((3,) + qi.shape[1:], BF16)
    ro = lax.empty((3,) + qo.shape[1:], BF16)
    sems = _split_start("scatter_start", _scatter_copies, [qi, qo, ri, ro], 6)
    grad_x, dg1 = _grad_x_half(sources, wia, x2, rinv, norm_gain, dout, sems[3], 1, grad_x)
    _, _, got_i, got_o = _split_wait("scatter_wait", _scatter_copies, sems[0], sems[1], sems[2], grad_x)
    jn = _split_start("join_start", _join_copies, [_chip_sum(own_i, got_i, where, "chip_sum_w_in"),
                                                   _chip_sum(own_o, got_o, where, "chip_sum_w_out")], 2)
    small = _all_reduce_small(_pack_small(D, loss, dg0 + dg1, dlogits, dgn, dfg), jn[3])
    loss_sum = small[5, 0]
    d_ng, m_ng, v_ng, grad_norm_gain = _adamw(norm_gain, small[0:1, :], m_norm_gain, v_norm_gain, "adamw_norm_gain")
    d_lb, m_lb, v_lb, grad_lb_logits = _adamw(lb_logits, small[2:4, :SEG], m_lb_logits, v_lb_logits, "adamw_lb_logits")
    d_gn, m_gn, v_gn, grad_hgrn_gnorm = _adamw(hgrn_gnorm, small[4:5, :HGRN_HEAD], m_hgrn_gnorm, v_hgrn_gnorm,
                                               "adamw_hgrn_gnorm")
    d_fg, m_fg, v_fg, grad_final_gain = _adamw(fgain, small[1:2, :], m_final_gain.reshape(1, D),
                                               v_final_gain.reshape(1, D), "adamw_final_gain")
    g_w_in, g_w_out = _split_wait("join_wait", _join_copies, jn[0], jn[1], jn[2], d_fg)
    d_wi, m_wi, v_wi, grad_w_in = _adamw(w_in[0], g_w_in, m_w_in[0], v_w_in[0], "adamw_w_in")
    d_wo, m_wo, v_wo, grad_w_out = _adamw(w_out[0], g_w_out, m_w_out[0], v_w_out[0], "adamw_w_out")

    return (loss_sum, grad_x[None],
            grad_norm_gain, grad_w_in[None], grad_lb_logits, grad_hgrn_gnorm, grad_w_out[None], grad_final_gain[0],
            d_ng, d_wi[None], d_lb, d_gn, d_wo[None], d_fg[0],
            m_ng, m_wi[None], m_lb, m_gn, m_wo[None], m_fg[0],
            v_ng, v_wi[None], v_lb, v_gn, v_wo[None], v_fg[0])
```

```python
import jax
import jax.numpy as jnp
import numpy as np
from jax import lax
from jax.experimental import pallas as pl
from jax.experimental.pallas import tpu as pltpu

F32 = jnp.float32
BF16 = jnp.bfloat16
MESH = pl.DeviceIdType.MESH

NORM_EPS = 1e-6
HGRN_HEAD = 128
HGRN_CHUNK = 64
HGRN_TILE = 128
HGRN_STEP_FWD = (1024, 4)
HGRN_STEP_BWD = (2048, 1)
ATTN_HEAD = 64
LANES = 128
BAND = 128
DILATIONS = (1, 4, 16)
DEINTERLEAVE = 16
ATTN_SCALE = ATTN_HEAD ** -0.5
assert ATTN_SCALE == 0.125
ATTN_BLOCK_ELEMS = BAND * 2048
ATTN_UNROLL = 4
SEG_QKV = 4
SEG_GATE_A = 7
NEG = -1e30

ADAM_LR = 0.001
ADAM_B1 = 0.9
ADAM_B2 = 0.999
ADAM_EPS = 1e-08
ADAM_WD = 0.01
ADAM_STEP = 10

MIB = 1024 * 1024


def _cp(semantics=None, vmem_mib=48):
    return pltpu.CompilerParams(dimension_semantics=semantics, vmem_limit_bytes=vmem_mib * MIB)


def _dot(a, b):
    return jnp.dot(a, b, preferred_element_type=F32)


def _dot_nt(a, b):
    return lax.dot_general(a, b, (((1,), (1,)), ((), ())), preferred_element_type=F32)


def _dot_tn(a, b):
    return lax.dot_general(a, b, (((0,), (0,)), ((), ())), preferred_element_type=F32)


def _split3(x):
    hi = x.astype(BF16)
    r1 = x - hi.astype(F32)
    mid = r1.astype(BF16)
    lo = (r1 - mid.astype(F32)).astype(BF16)
    return hi, mid, lo


def _exact_dot(t_bf16, x):
    hi, mid, lo = _split3(x)
    return _dot(t_bf16, hi) + _dot(t_bf16, mid) + _dot(t_bf16, lo)


def _exact_dot_right(x, t_bf16):
    hi, mid, lo = _split3(x)
    return _dot(hi, t_bf16) + _dot(mid, t_bf16) + _dot(lo, t_bf16)


def _sigmoid(z):
    return jax.nn.sigmoid(z)


def _silu_and_grad(z):
    s = _sigmoid(z)
    return z * s, s * (1.0 + z * (1.0 - s))


def _seg_select(j, values):
    out = values[0]
    for t, v in enumerate(values[1:], 1):
        out = jnp.where(j == t, v, out)
    return out


def _rms_fwd(x2, gain, token):
    S, D = x2.shape
    tm = min(512, S)

    def body(x_ref, g_ref, _, h_ref, r_ref):
        x = x_ref[...]
        r = lax.rsqrt(jnp.mean(x * x, axis=-1, keepdims=True) + NORM_EPS)
        h_ref[...] = ((x * r) * g_ref[...]).astype(BF16)
        r_ref[...] = r

    return pl.pallas_call(
        body, grid=(S // tm,), name="rms_fwd",
        in_specs=[pl.BlockSpec((tm, D), lambda i: (i, 0)), pl.BlockSpec((1, D), lambda i: (0, 0)),
                  pl.BlockSpec(token.shape, lambda i: (0, 0))],
        out_specs=[pl.BlockSpec((tm, D), lambda i: (i, 0)), pl.BlockSpec((tm, 1), lambda i: (i, 0))],
        out_shape=[jax.ShapeDtypeStruct((S, D), BF16), jax.ShapeDtypeStruct((S, 1), F32)],
        compiler_params=_cp(("parallel",)),
    )(x2, gain, token)


def _in_proj(h, w_all, where, segs, z_prev, token, name):
    S, D = h.shape
    SEG = w_all.shape[2] // 2
    NLB = SEG // LANES
    tm = min(1024, S)
    count = len(segs)
    DI = DEINTERLEAVE
    tu = tm // DI

    def is_qkv(seg):
        return (seg >= SEG_QKV) & (seg < SEG_QKV + 3)

    def seg_of(j, w):
        halves = {0: 0, 1: 1, "mine": w[0], "sibling": 1 - w[0]}
        cands = [2 * jnp.bitwise_xor(w[1], rel) + halves[half] for rel, half in segs]
        keys = [is_qkv(s).astype(jnp.int32) for s in cands]
        out = cands[0]
        for k in range(count):
            pos = (sum(jnp.where(keys[t] < keys[k], 1, 0) for t in range(count))
                   + sum(jnp.where(keys[t] == keys[k], 1, 0) for t in range(k)))
            out = jnp.where(pos == j, cands[k], out)
        return out

    def body(*refs):
        where_ref, h_ref, w_ref = refs[:3]
        o_ref, o16_ref = refs[-2:]
        res = _dot(h_ref[...], w_ref[...])
        for p in range(NLB):
            o_ref[p] = res[:, p * LANES:(p + 1) * LANES]

        @pl.when(is_qkv(seg_of(pl.program_id(0), where_ref)))
        def _():
            for p in range(NLB):
                for r in range(DI):
                    o16_ref[p, r] = o_ref.at[p][pl.ds(r, tu, stride=DI), :]

    def z16_map(j, i, w):
        seg = seg_of(j, w)
        return (jnp.where(is_qkv(seg), seg - SEG_QKV, 3), 0, 0, jnp.where(is_qkv(seg), i, 0), 0)

    in_specs = [pl.BlockSpec((tm, D), lambda j, i, w: (i, 0)),
                pl.BlockSpec((None, D, SEG), lambda j, i, w: (seg_of(j, w) // 2, 0, seg_of(j, w) % 2)),
                pl.BlockSpec(token.shape, lambda j, i, w: (0, 0))]
    args = [where, h, w_all, token]
    aliases = {}
    if z_prev is not None:
        in_specs += [ANY, ANY]
        args += list(z_prev)
        aliases = {4: 0, 5: 1}
    grid_spec = pltpu.PrefetchScalarGridSpec(
        num_scalar_prefetch=1, grid=(count, S // tm), in_specs=in_specs,
        out_specs=[pl.BlockSpec((None, NLB, tm, LANES), lambda j, i, w: (seg_of(j, w), 0, i, 0)),
                   pl.BlockSpec((None, NLB, DI, tu, LANES), z16_map)])
    return pl.pallas_call(
        body, grid_spec=grid_spec, name=name,
        out_shape=[jax.ShapeDtypeStruct((8, NLB, S, LANES), F32),
                   jax.ShapeDtypeStruct((4, NLB, DI, S // DI, LANES), F32)],
        input_output_aliases=aliases, compiler_params=_cp(("parallel", "parallel")),
    )(*args)


def _out_proj_loss(yh, ya, w_out, x2, tgt, fgain):
    S, D = x2.shape
    SEG = yh.shape[1]
    tm = min(256, S)
    parts = 2

    def body(yh_ref, ya_ref, w_ref, x_ref, t_ref, fg_ref, dout_ref, doutb_ref, loss_ref, dfg_ref):
        i = pl.program_id(0)

        @pl.when(i == 0)
        def _():
            loss_ref[...] = jnp.zeros_like(loss_ref)
            dfg_ref[...] = jnp.zeros_like(dfg_ref)

        fg = fg_ref[...]
        loss = jnp.zeros((1, 1), F32)
        dfg = jnp.zeros((1, D), F32)
        for rows in [pl.ds(p * (tm // parts), tm // parts) for p in range(parts)]:
            out = (x_ref[rows, :] + _dot(yh_ref[rows, :], w_ref[pl.ds(0, SEG), :])
                   + _dot(ya_ref[rows, :], w_ref[pl.ds(SEG, SEG), :]))
            r = lax.rsqrt(jnp.mean(out * out, axis=-1, keepdims=True) + NORM_EPS)
            n = out * r
            err = n * fg - t_ref[rows, :]
            loss = loss + 0.5 * jnp.sum(jnp.mean(err * err, axis=-1, keepdims=True), axis=0, keepdims=True)
            dy = err * (1.0 / D)
            dfg = dfg + jnp.sum(dy * n, axis=0, keepdims=True)
            dn = dy * fg
            dout = r * (dn - n * jnp.mean(dn * n, axis=-1, keepdims=True))
            dout_ref[rows, :] = dout
            doutb_ref[rows, :] = dout.astype(BF16)
        loss_ref[...] += loss
        dfg_ref[...] += dfg

    row = lambda i: (i, 0)
    fix = lambda i: (0, 0)
    return pl.pallas_call(
        body, grid=(S // tm,), name="out_proj_loss",
        in_specs=[pl.BlockSpec((tm, SEG), row), pl.BlockSpec((tm, SEG), row), pl.BlockSpec((2 * SEG, D), fix),
                  pl.BlockSpec((tm, D), row), pl.BlockSpec((tm, D), row), pl.BlockSpec((1, D), fix)],
        out_specs=[pl.BlockSpec((tm, D), row), pl.BlockSpec((tm, D), row), pl.BlockSpec((1, 1), fix),
                   pl.BlockSpec((1, D), fix)],
        out_shape=[jax.ShapeDtypeStruct((S, D), F32), jax.ShapeDtypeStruct((S, D), BF16),
                   jax.ShapeDtypeStruct((1, 1), F32), jax.ShapeDtypeStruct((1, D), F32)],
        compiler_params=_cp(("arbitrary",)),
    )(yh, ya, w_out, x2, tgt, fgain)


def _dy_proj(doutb, w_out):
    S, D = doutb.shape
    K = w_out.shape[0]
    tm = min(1024, S)

    def body(d_ref, w_ref, o_ref):
        o_ref[...] = _dot_nt(d_ref[...], w_ref[...])

    return pl.pallas_call(
        body, grid=(S // tm,), name="dy_proj",
        in_specs=[pl.BlockSpec((tm, D), lambda i: (i, 0)), pl.BlockSpec((K, D), lambda i: (0, 0))],
        out_specs=pl.BlockSpec((tm, K), lambda i: (i, 0)),
        out_shape=jax.ShapeDtypeStruct((S, K), F32),
        compiler_params=_cp(("parallel",)),
    )(doutb, w_out)


def _grad_w_out(yh, ya, doutb):
    S, SEG = yh.shape
    D = doutb.shape[1]
    R = (2 * SEG) // 4
    nb_half = SEG // R
    tk = min(2048, S)

    def body(yh_ref, ya_ref, d_ref, o_ref):
        q = pl.program_id(0)
        k = pl.program_id(1)

        @pl.when(k == 0)
        def _():
            o_ref[...] = jnp.zeros_like(o_ref)

        @pl.when(q < nb_half)
        def _():
            o_ref[...] += _dot_tn(yh_ref[...], d_ref[...])

        @pl.when(q >= nb_half)
        def _():
            o_ref[...] += _dot_tn(ya_ref[...], d_ref[...])

    return pl.pallas_call(
        body, grid=(4, S // tk), name="grad_w_out",
        in_specs=[pl.BlockSpec((tk, R), lambda q, k: (k, jnp.minimum(q, nb_half - 1))),
                  pl.BlockSpec((tk, R), lambda q, k: (k, jnp.maximum(q - nb_half, 0))),
                  pl.BlockSpec((tk, D), lambda q, k: (k, 0))],
        out_specs=pl.BlockSpec((None, R, D), lambda q, k: (q, 0, 0)),
        out_shape=jax.ShapeDtypeStruct((4, R, D), F32),
        compiler_params=_cp(("parallel", "arbitrary")),
    )(yh, ya, doutb)


def _dz_sources(sources):
    counts = [s.shape[0] for s in sources]
    starts = [sum(counts[:k]) for k in range(len(counts))]
    assert sum(counts) == 8
    return counts, starts


def _row_part(S, part, tile):
    first = max(512, (S * 3 // 8) // 512 * 512)
    rows = first if part == 0 else S - first
    assert rows % tile == 0 and first % tile == 0
    return (0 if part == 0 else first // tile), rows // tile, rows


def _dh_proj(sources, w_all, token, part, name):
    S = sources[0].shape[1]
    D = w_all.shape[1]
    SEG = w_all.shape[2] // 2
    counts, starts = _dz_sources(sources)
    assert all(c % 2 == 0 for c in counts)
    ns = len(sources)
    tm = 1024 if all(_row_part(S, p, 1)[2] % 1024 == 0 for p in (0, 1)) else 512
    t0, nt, nrows = _row_part(S, part, tm)

    def body(*refs):
        src = refs[:ns]
        w_ref, _, o_ref = refs[ns:]
        j = pl.program_id(1)

        @pl.when(j == 0)
        def _():
            o_ref[...] = jnp.zeros_like(o_ref)

        for k in range(ns):
            @pl.when((2 * j >= starts[k]) & (2 * j < starts[k] + counts[k]))
            def _(k=k):
                o_ref[...] += (_dot_nt(src[k][0], w_ref[:, pl.ds(0, SEG)])
                               + _dot_nt(src[k][1], w_ref[:, pl.ds(SEG, SEG)]))

    def src_spec(k):
        return pl.BlockSpec((2, tm, SEG),
                            lambda i, j: (jnp.clip(j - starts[k] // 2, 0, counts[k] // 2 - 1), t0 + i, 0))

    return pl.pallas_call(
        body, grid=(nt, 4), name=name,
        in_specs=[src_spec(k) for k in range(ns)] + [pl.BlockSpec((None, D, 2 * SEG), lambda i, j: (j, 0, 0)),
                                                     pl.BlockSpec(token.shape, lambda i, j: (0, 0))],
        out_specs=pl.BlockSpec((tm, D), lambda i, j: (i, 0)),
        out_shape=jax.ShapeDtypeStruct((nrows, D), F32),
        compiler_params=_cp(("parallel", "arbitrary"), 48 if tm == 512 else 60),
    )(*sources, w_all, token)


def _rms_bwd(dh, x2, rinv, gain, dout, part, gx_prev, name):
    S, D = x2.shape
    tm = 512
    t0, nt, _ = _row_part(S, part, tm)

    def body(dh_ref, x_ref, r_ref, g_ref, dout_ref, *rest):
        gx_ref, dg_ref = rest[-2:]

        @pl.when(pl.program_id(0) == 0)
        def _():
            dg_ref[...] = jnp.zeros_like(dg_ref)

        dh = dh_ref[...]
        r = r_ref[...]
        xhat = x_ref[...] * r
        dg_ref[...] += jnp.sum(dh * xhat, axis=0, keepdims=True)
        dxn = dh * g_ref[...]
        gx_ref[...] = dout_ref[...] + r * (dxn - xhat * jnp.mean(dxn * xhat, axis=-1, keepdims=True))

    row = lambda i: (t0 + i, 0)
    fix = lambda i: (0, 0)
    in_specs = [pl.BlockSpec((tm, D), lambda i: (i, 0)), pl.BlockSpec((tm, D), row), pl.BlockSpec((tm, 1), row),
                pl.BlockSpec((1, D), fix), pl.BlockSpec((tm, D), row)]
    args = [dh, x2, rinv, gain, dout]
    aliases = {}
    if gx_prev is not None:
        in_specs.append(ANY)
        args.append(gx_prev)
        aliases = {5: 0}
    return pl.pallas_call(
        body, grid=(nt,), name=name, in_specs=in_specs,
        out_specs=[pl.BlockSpec((tm, D), row), pl.BlockSpec((1, D), fix)],
        out_shape=[jax.ShapeDtypeStruct((S, D), F32), jax.ShapeDtypeStruct((1, D), F32)],
        input_output_aliases=aliases, compiler_params=_cp(("arbitrary",), 60),
    )(*args)


def _grad_w_in(h, sources):
    S, D = h.shape
    SEG = sources[0].shape[2]
    counts, starts = _dz_sources(sources)
    ns = len(sources)
    tk = min(2048, S)

    def body(*refs):
        h_ref = refs[0]
        src = refs[1:1 + ns]
        o_ref = refs[1 + ns]
        j = pl.program_id(0)
        k = pl.program_id(1)

        @pl.when(k == 0)
        def _():
            o_ref[...] = jnp.zeros_like(o_ref)

        for s in range(ns):
            @pl.when((j >= starts[s]) & (j < starts[s] + counts[s]))
            def _(s=s):
                o_ref[...] += _dot_tn(h_ref[...], src[s][...])

    def src_spec(s):
        return pl.BlockSpec((None, tk, SEG),
                            lambda j, k: (jnp.clip(j - starts[s], 0, counts[s] - 1), k, 0))

    return pl.pallas_call(
        body, grid=(8, S // tk), name="grad_w_in",
        in_specs=[pl.BlockSpec((tk, D), lambda j, k: (k, 0))] + [src_spec(s) for s in range(ns)],
        out_specs=pl.BlockSpec((None, D, SEG), lambda j, k: (j // 2, 0, j % 2)),
        out_shape=jax.ShapeDtypeStruct((4, D, 2 * SEG), F32),
        compiler_params=_cp(("parallel", "arbitrary"), 48 if tk <= 1024 else 62),
    )(h, *sources)


def _lower_bound(lbl):
    l0 = lbl[0:1, :]
    l1 = lbl[1:2, :]
    m = jnp.maximum(l0, l1)
    e0 = jnp.exp(l0 - m)
    e1 = jnp.exp(l1 - m)
    return e0 / (e0 + e1)


def _tile_masks():
    row = lax.broadcasted_iota(jnp.int32, (HGRN_TILE, HGRN_TILE), 0)
    col = lax.broadcasted_iota(jnp.int32, (HGRN_TILE, HGRN_TILE), 1)
    same = (row // HGRN_CHUNK) == (col // HGRN_CHUNK)
    return same & (row >= col), same & (row <= col)


def _chunk_last(b):
    T = b.shape[0]
    b3 = b.reshape(T // HGRN_CHUNK, HGRN_CHUNK, HGRN_HEAD)
    return jnp.broadcast_to(b3[:, HGRN_CHUNK - 1:HGRN_CHUNK, :], b3.shape).reshape(T, HGRN_HEAD)


def _chunk_sum(x):
    T = x.shape[0]
    x3 = x.reshape(T // HGRN_CHUNK, HGRN_CHUNK, HGRN_HEAD)
    return jnp.broadcast_to(jnp.sum(x3, axis=1, keepdims=True), x3.shape).reshape(T, HGRN_HEAD)


def _hgrn_dims(S, SEG, rows):
    T = min(rows, S)
    assert S % T == 0 and T % HGRN_TILE == 0
    tiles = [slice(t * HGRN_TILE, (t + 1) * HGRN_TILE) for t in range(T // HGRN_TILE)]
    chunks = [slice(c * HGRN_CHUNK, (c + 1) * HGRN_CHUNK) for c in range(T // HGRN_CHUNK)]
    return SEG // HGRN_HEAD, T, T // HGRN_CHUNK, S // T, tiles, chunks


def _hgrn_fwd(zf32, lb_logits, gnorm):
    _, NLB, S, _ = zf32.shape
    SEG = NLB * LANES
    H, T, NC, NJ, tiles, chunks = _hgrn_dims(S, SEG, HGRN_STEP_FWD[0])
    HP = min(HGRN_STEP_FWD[1], H)
    assert H % HP == 0

    def body(zq_ref, zf_ref, zi_ref, zg_ref, lbl_ref, gn_ref, y_ref, st_ref, state):
        @pl.when(pl.program_id(1) == 0)
        def _():
            state[...] = jnp.zeros_like(state)

        tril, _ = _tile_masks()
        tril_bf = tril.astype(BF16)
        for hh in range(HP):
            cols = slice(hh * HGRN_HEAD, (hh + 1) * HGRN_HEAD)
            lb = _lower_bound(lbl_ref[:, cols])
            zq = zq_ref[hh]
            q = zq * _sigmoid(zq)
            f = lb + (1.0 - lb) * _sigmoid(zf_ref[hh])
            k = 1.0 - f
            logf = jnp.log(f)
            b = jnp.concatenate([_exact_dot(tril_bf, logf[t]) for t in tiles], axis=0)
            bl = _chunk_last(b)
            qd_b = (q * jnp.exp(b)).astype(BF16)
            kd_b = (k * jnp.exp(-b)).astype(BF16)
            ke_b = (k * jnp.exp(bl - b)).astype(BF16)
            v_b = zi_ref[hh].astype(BF16)
            o_intra = jnp.concatenate(
                [_dot(jnp.where(tril, _dot_nt(qd_b[t], kd_b[t]), 0.0).astype(BF16), v_b[t]) for t in tiles], axis=0)
            kvs = [_dot_tn(v_b[r], ke_b[r]) for r in chunks]
            ebl = jnp.exp(bl)
            st = state[hh]
            sts = []
            for c in range(NC):
                st_ref[c, hh] = st
                sts.append(st.astype(BF16))
                st = st * ebl[c * HGRN_CHUNK:c * HGRN_CHUNK + 1, :] + kvs[c]
            state[hh] = st
            o = o_intra + jnp.concatenate([_dot_nt(qd_b[r], sb) for r, sb in zip(chunks, sts)], axis=0)
            on = o * lax.rsqrt(jnp.mean(o * o, axis=-1, keepdims=True) + NORM_EPS) * gn_ref[...]
            zg = zg_ref[hh]
            y_ref[:, cols] = (on * (zg * _sigmoid(zg))).astype(BF16)

    def zspec(seg):
        return pl.BlockSpec((None, HP, T, HGRN_HEAD), lambda h, j: (seg, h, j, 0))

    return pl.pallas_call(
        body, grid=(H // HP, NJ), name="hgrn_fwd",
        in_specs=[zspec(0), zspec(1), zspec(2), zspec(3),
                  pl.BlockSpec((2, HP * HGRN_HEAD), lambda h, j: (0, h)),
                  pl.BlockSpec((1, HGRN_HEAD), lambda h, j: (0, 0))],
        out_specs=[pl.BlockSpec((T, HP * HGRN_HEAD), lambda h, j: (j, h)),
                   pl.BlockSpec((NC, HP, HGRN_HEAD, HGRN_HEAD), lambda h, j: (j, h, 0, 0))],
        out_shape=[jax.ShapeDtypeStruct((S, SEG), BF16),
                   jax.ShapeDtypeStruct((S // HGRN_CHUNK, H, HGRN_HEAD, HGRN_HEAD), F32)],
        scratch_shapes=[pltpu.VMEM((HP, HGRN_HEAD, HGRN_HEAD), F32)],
        compiler_params=_cp(("parallel", "arbitrary")),
    )(zf32, zf32, zf32, zf32, lb_logits, gnorm)


def _hgrn_bwd(zf32, lb_logits, gnorm, states, dy):
    _, NLB, S, _ = zf32.shape
    SEG = NLB * LANES
    H, T, NC, NJ, tiles, chunks = _hgrn_dims(S, SEG, HGRN_STEP_BWD[0])
    C = HGRN_CHUNK
    HP = min(HGRN_STEP_BWD[1], H)
    assert H % HP == 0

    def body(zq_ref, zf_ref, zi_ref, zg_ref, lbl_ref, gn_ref, st_ref, dy_ref, dz_ref, dl_ref, dgn_ref, gstate):
        @pl.when(pl.program_id(1) == 0)
        def _():
            gstate[...] = jnp.zeros_like(gstate)
            dl_ref[...] = jnp.zeros_like(dl_ref)
            dgn_ref[...] = jnp.zeros_like(dgn_ref)

        gn = gn_ref[...]
        tril, triu = _tile_masks()
        tril_bf = tril.astype(BF16)
        triu_bf = triu.astype(BF16)
        for hh in range(HP):
            cols = slice(hh * HGRN_HEAD, (hh + 1) * HGRN_HEAD)
            lb = _lower_bound(lbl_ref[:, cols])
            q, dq_dz = _silu_and_grad(zq_ref[hh])
            sf = _sigmoid(zf_ref[hh])
            f = lb + (1.0 - lb) * sf
            k = 1.0 - f
            logf = jnp.log(f)
            b = jnp.concatenate([_exact_dot(tril_bf, logf[t]) for t in tiles], axis=0)
            bl = _chunk_last(b)
            eb = jnp.exp(b)
            enb = jnp.exp(-b)
            ekl = jnp.exp(bl - b)
            ebl = jnp.exp(bl)
            qd = q * eb
            kd = k * enb
            ke = k * ekl
            qd_b = qd.astype(BF16)
            kd_b = kd.astype(BF16)
            ke_b = ke.astype(BF16)
            v_b = zi_ref[hh].astype(BF16)
            sts = [st_ref[c, hh] for c in range(NC)]
            sts_b = [s.astype(BF16) for s in sts]
            a_b = [jnp.where(tril, _dot_nt(qd_b[t], kd_b[t]), 0.0).astype(BF16) for t in tiles]
            o = (jnp.concatenate([_dot(a, v_b[t]) for a, t in zip(a_b, tiles)], axis=0)
                 + jnp.concatenate([_dot_nt(qd_b[r], sb) for r, sb in zip(chunks, sts_b)], axis=0))
            rinv = lax.rsqrt(jnp.mean(o * o, axis=-1, keepdims=True) + NORM_EPS)
            ohat = o * rinv
            sg, dsg = _silu_and_grad(zg_ref[hh])
            dyv = dy_ref[:, cols]
            don = dyv * sg
            dz_ref[3, :, cols] = (dyv * (ohat * gn) * dsg).astype(BF16)
            dgn_ref[hh] += jnp.sum(don * ohat, axis=0, keepdims=True)
            dohat = don * gn
            do = rinv * (dohat - ohat * jnp.mean(dohat * ohat, axis=-1, keepdims=True))
            do_b = do.astype(BF16)
            da_b = [jnp.where(tril, _dot_nt(do_b[t], v_b[t]), 0.0).astype(BF16) for t in tiles]
            dv_intra = jnp.concatenate([_dot_tn(a, do_b[t]) for a, t in zip(a_b, tiles)], axis=0)
            dqd_intra = jnp.concatenate([_dot(da, kd_b[t]) for da, t in zip(da_b, tiles)], axis=0)
            dkd = jnp.concatenate([_dot_tn(da, qd_b[t]) for da, t in zip(da_b, tiles)], axis=0)
            dqd_inter = jnp.concatenate([_dot(do_b[r], sb) for r, sb in zip(chunks, sts_b)], axis=0)
            gks = [_dot_tn(do_b[r], qd_b[r]) for r in chunks]
            g = gstate[hh]
            gs = [None] * NC
            for c in reversed(range(NC)):
                gs[c] = g
                g = g * ebl[c * C:c * C + 1, :] + gks[c]
            gstate[hh] = g
            gs_b = [x.astype(BF16) for x in gs]
            dv = dv_intra + jnp.concatenate([_dot_nt(ke_b[r], gb) for r, gb in zip(chunks, gs_b)], axis=0)
            dz_ref[2, :, cols] = dv.astype(BF16)
            dke = jnp.concatenate([_dot(v_b[r], gb) for r, gb in zip(chunks, gs_b)], axis=0)
            debl = jnp.concatenate(
                [jnp.broadcast_to(jnp.sum(x * s, axis=0, keepdims=True), (C, HGRN_HEAD)) for x, s in zip(gs, sts)], axis=0)
            dqd = dqd_intra + dqd_inter
            dz_ref[0, :, cols] = ((dqd * eb) * dq_dz).astype(BF16)
            t_ke = dke * ke
            db = dqd * qd - dkd * kd - t_ke
            db_last = _chunk_sum(t_ke) + debl * ebl
            dk = dkd * enb + dke * ekl
            dlogf = jnp.concatenate([_exact_dot(triu_bf, db[t]) for t in tiles], axis=0) + db_last
            df = dlogf / f - dk
            dz_ref[1, :, cols] = (df * (1.0 - lb) * (sf * (1.0 - sf))).astype(BF16)
            dlb = jnp.sum(df * (1.0 - sf), axis=0, keepdims=True)
            dl0 = dlb * lb * (1.0 - lb)
            dl_ref[0:1, cols] += dl0
            dl_ref[1:2, cols] -= dl0

    def zspec(seg):
        return pl.BlockSpec((None, HP, T, HGRN_HEAD), lambda h, j: (seg, h, NJ - 1 - j, 0))

    return pl.pallas_call(
        body, grid=(H // HP, NJ), name="hgrn_bwd",
        in_specs=[zspec(0), zspec(1), zspec(2), zspec(3),
                  pl.BlockSpec((2, HP * HGRN_HEAD), lambda h, j: (0, h)),
                  pl.BlockSpec((1, HGRN_HEAD), lambda h, j: (0, 0)),
                  pl.BlockSpec((NC, HP, HGRN_HEAD, HGRN_HEAD), lambda h, j: (NJ - 1 - j, h, 0, 0)),
                  pl.BlockSpec((T, HP * HGRN_HEAD), lambda h, j: (NJ - 1 - j, h))],
        out_specs=[pl.BlockSpec((4, T, HP * HGRN_HEAD), lambda h, j: (0, NJ - 1 - j, h)),
                   pl.BlockSpec((2, HP * HGRN_HEAD), lambda h, j: (0, h)),
                   pl.BlockSpec((HP, 1, HGRN_HEAD), lambda h, j: (h, 0, 0))],
        out_shape=[jax.ShapeDtypeStruct((4, S, SEG), BF16), jax.ShapeDtypeStruct((2, SEG), F32),
                   jax.ShapeDtypeStruct((H, 1, HGRN_HEAD), F32)],
        scratch_shapes=[pltpu.VMEM((HP, HGRN_HEAD, HGRN_HEAD), F32)],
        compiler_params=_cp(("parallel", "arbitrary")),
    )(zf32, zf32, zf32, zf32, lb_logits, gnorm, states, dy)


def _alibi_slopes(seg):
    n_heads = seg // ATTN_HEAD
    s = 2.0 ** (-8.0 * np.arange(1, n_heads + 1, dtype=np.float64) / n_heads)
    return jnp.asarray(np.repeat(s, ATTN_HEAD)[None, :], F32)


def _attn_dims(S, SEG, d):
    rb = BAND * d
    assert S % rb == 0 and SEG % LANES == 0
    npb = max(1, min(SEG // LANES, ATTN_BLOCK_ELEMS // (rb * LANES)))
    assert (SEG // LANES) % npb == 0
    return rb, npb, S // rb, (SEG // LANES) // npb


def _res_rows(r, d):
    return pl.ds(0, BAND) if d == 1 else pl.ds(r, BAND, stride=d)


def _for_residues(d, fn):
    if d == 1:
        fn(0)
    else:
        def step(r, carry):
            fn(r)
            return carry
        lax.fori_loop(0, d, step, 0, unroll=ATTN_UNROLL)


def _for_groups(d, n_pairs, fn):
    def over_pairs(r):
        for g0 in range(0, n_pairs, ATTN_UNROLL):
            fn([(r, p) for p in range(g0, min(n_pairs, g0 + ATTN_UNROLL))])

    if d == 1:
        over_pairs(0)
    elif n_pairs >= ATTN_UNROLL:
        def step(r, carry):
            over_pairs(r)
            return carry
        lax.fori_loop(0, d, step, 0)
    else:
        per_group = ATTN_UNROLL // n_pairs
        assert d % per_group == 0

        def step(g, carry):
            fn([(g * per_group + i, p) for i in range(per_group) for p in range(n_pairs)])
            return carry
        lax.fori_loop(0, d // per_group, step, 0)


def _band_terms(n, d):
    i = lax.broadcasted_iota(jnp.int32, (BAND, 2 * BAND), 0)
    jj = lax.broadcasted_iota(jnp.int32, (BAND, 2 * BAND), 1)
    delta = BAND + i - jj
    valid = (delta >= 0) & (delta <= BAND) & ((n > 0) | (jj >= BAND))
    return (-d * delta).astype(F32), valid


def _head_biases(slopes, nd, valid):
    out = []
    for s in _per_head(slopes):
        s2 = jnp.concatenate([s, s], axis=1)
        out.append(jnp.where(valid, s2 * nd, NEG))
    return jnp.concatenate(out, axis=0)


def _stack_heads(x):
    lane = lax.broadcasted_iota(jnp.int32, x.shape, 1)
    zero = jnp.zeros_like(x)
    return jnp.concatenate([jnp.where(lane < ATTN_HEAD, x, zero), jnp.where(lane < ATTN_HEAD, zero, x)], axis=0)


def _unstack_heads(x2):
    first = lax.broadcasted_iota(jnp.int32, (BAND, LANES), 1) < ATTN_HEAD
    return jnp.where(first, x2[:BAND], x2[BAND:])


def _stack_per_head(x):
    a, b = _per_head(x)
    col = jnp.concatenate([a, b], axis=0)
    return jnp.concatenate([col, col], axis=1)


def _per_head(x):
    lane = lax.broadcasted_iota(jnp.int32, x.shape, 1)
    sw = pltpu.roll(x, ATTN_HEAD, 1)
    first = lane < ATTN_HEAD
    return jnp.where(first, x, sw), jnp.where(first, sw, x)


def _qkv_source(zz, d):
    z, z16 = zz
    if d == DEINTERLEAVE:
        def take(ref, p, r):
            return ref.at[p][r]

        def spec(seg, np_, row_block):
            return pl.BlockSpec((None, np_, d, BAND, LANES), lambda c, n: (seg, c, 0, row_block(c, n), 0))
        return z, z16, take, spec

    def take(ref, p, r):
        return ref.at[p][_res_rows(r, d), :]

    def spec(seg, np_, row_block):
        return pl.BlockSpec((None, np_, BAND * d, LANES), lambda c, n: (SEG_QKV + seg, c, row_block(c, n), 0))
    return z, z, take, spec


def _attn_fwd(qkv, slopes, d):
    qkv, src, take, spec = _qkv_source(qkv, d)
    _, NLB, S, _ = qkv.shape
    rb, NP, nb, ncb = _attn_dims(S, NLB * LANES, d)

    def body(q_ref, kc_ref, vc_ref, sl_ref, o_ref, l_ref, kp_ref, vp_ref):
        n = pl.program_id(1)

        @pl.when(n == 0)
        def _():
            kp_ref[...] = jnp.zeros_like(kp_ref)
            vp_ref[...] = jnp.zeros_like(vp_ref)

        nd, valid = _band_terms(n, d)
        biases = [_head_biases(sl_ref[:, p * LANES:(p + 1) * LANES], nd, valid) for p in range(NP)]

        def group(items):
            scores, values = [], []
            for r, p in items:
                kc = jnp.concatenate([take(kp_ref, p, r), take(kc_ref, p, r)], axis=0).astype(BF16)
                values.append(jnp.concatenate([take(vp_ref, p, r), take(vc_ref, p, r)], axis=0).astype(BF16))
                scores.append(_dot_nt(_stack_heads((take(q_ref, p, r) * ATTN_SCALE).astype(BF16)), kc))
            probs = []
            for (r, p), s in zip(items, scores):
                s = s + biases[p]
                m = jnp.max(s, axis=-1, keepdims=True)
                e = jnp.exp(s - m)
                den = jnp.sum(e, axis=-1, keepdims=True)
                probs.append((e.astype(BF16), den, m + jnp.log(den)))
            for (r, p), vc, (e, den, lse) in zip(items, values, probs):
                rows = _res_rows(r, d)
                o_ref.at[p][rows, :] = _unstack_heads(_dot(e, vc) / den)
                l_ref.at[p][rows, :] = _unstack_heads(jnp.broadcast_to(lse, (2 * BAND, LANES)))

        _for_groups(d, NP, group)
        kp_ref[...] = kc_ref[...]
        vp_ref[...] = vc_ref[...]

    cur = lambda c, n: n
    out = pl.BlockSpec((NP, rb, LANES), lambda c, n: (c, n, 0))
    kv_block = spec(1, NP, cur).block_shape[1:]
    return pl.pallas_call(
        body, grid=(ncb, nb), name=f"attn_fwd_d{d}",
        in_specs=[spec(0, NP, cur), spec(1, NP, cur), spec(2, NP, cur),
                  pl.BlockSpec((1, NP * LANES), lambda c, n: (0, c))],
        out_specs=[out, out],
        out_shape=[jax.ShapeDtypeStruct((NLB, S, LANES), F32)] * 2,
        scratch_shapes=[pltpu.VMEM(kv_block, F32), pltpu.VMEM(kv_block, F32)],
        compiler_params=_cp(("parallel", "arbitrary")),
    )(src, src, src, slopes)


def _attn_merge(outs, lses, zf32):
    NLB, S, _ = outs[0].shape
    SEG = NLB * LANES
    tm = min(256, S)

    def body(o1, o2, o3, l1, l2, l3, zg_ref, o_ref, lse_ref, y_ref):
        a, b, c = l1[...], l2[...], l3[...]
        m = jnp.maximum(jnp.maximum(a, b), c)
        ea, eb, ec = jnp.exp(a - m), jnp.exp(b - m), jnp.exp(c - m)
        tot = ea + eb + ec
        o = (ea / tot) * o1[...] + (eb / tot) * o2[...] + (ec / tot) * o3[...]
        o_ref[...] = o
        lse_ref[...] = m + jnp.log(tot)
        zg = zg_ref[...]
        y = (o * (zg * _sigmoid(zg))).astype(BF16)
        for p in range(NLB):
            y_ref[:, p * LANES:(p + 1) * LANES] = y[p]

    blk = pl.BlockSpec((NLB, tm, LANES), lambda i: (0, i, 0))
    return pl.pallas_call(
        body, grid=(S // tm,), name="attn_merge",
        in_specs=[blk] * 6 + [pl.BlockSpec((None, NLB, tm, LANES), lambda i: (SEG_GATE_A, 0, i, 0))],
        out_specs=[blk, blk, pl.BlockSpec((tm, SEG), lambda i: (i, 0))],
        out_shape=[jax.ShapeDtypeStruct((NLB, S, LANES), F32), jax.ShapeDtypeStruct((NLB, S, LANES), F32),
                   jax.ShapeDtypeStruct((S, SEG), BF16)],
        compiler_params=_cp(("parallel",)),
    )(*outs, *lses, zf32)


def _attn_gate_bwd(dy, o, zf32):
    NP, S, _ = o.shape
    SEG = NP * LANES
    tm = min(512, S)

    def body(dy_ref, o_ref, zg_ref, do_ref, dl_ref, dzg_ref):
        r = lax.broadcasted_iota(jnp.int32, (LANES, LANES), 0) // ATTN_HEAD
        c = lax.broadcasted_iota(jnp.int32, (LANES, LANES), 1) // ATTN_HEAD
        same_head = (r == c).astype(BF16)
        for p in range(NP):
            cols = slice(p * LANES, (p + 1) * LANES)
            sg, dsg = _silu_and_grad(zg_ref[p])
            dyv = dy_ref[:, cols]
            ov = o_ref[p]
            do = dyv * sg
            do_ref[p] = do
            dzg_ref[:, cols] = (dyv * ov * dsg).astype(BF16)
            dl_ref[p] = _exact_dot_right(do * ov, same_head)

    blk = pl.BlockSpec((NP, tm, LANES), lambda i: (0, i, 0))
    return pl.pallas_call(
        body, grid=(S // tm,), name="attn_gate_bwd",
        in_specs=[pl.BlockSpec((tm, SEG), lambda i: (i, 1)), blk,
                  pl.BlockSpec((None, NP, tm, LANES), lambda i: (SEG_GATE_A, 0, i, 0))],
        out_specs=[blk, blk, pl.BlockSpec((None, tm, SEG), lambda i: (3, i, 0))],
        out_shape=[jax.ShapeDtypeStruct((NP, S, LANES), F32), jax.ShapeDtypeStruct((NP, S, LANES), F32),
                   jax.ShapeDtypeStruct((4, S, SEG), BF16)],
        compiler_params=_cp(("parallel",)),
    )(dy, o, zf32)


def _attn_bwd(qkv, slopes, do, lse, dl, d, acc, into):
    qkv, src, take, spec = _qkv_source(qkv, d)
    _, NLB, S, _ = qkv.shape
    SEG = NLB * LANES
    rb, NP, nb, ncb = _attn_dims(S, SEG, d)
    has_acc = acc is not None
    out_dtype = F32 if into is None else into.dtype
    assert into is None or d == 1

    def body(*refs):
        q_ref, kc_ref, vc_ref, sl_ref, do_ref, lse_ref, dl_ref = refs[:7]
        acc_ref = refs[7] if has_acc else None
        out_ref, cq, ck, cv, kp_ref, vp_ref = refs[-6:]
        n = pl.program_id(1)

        def emit(r, p, dq, dk, dv):
            rows = _res_rows(r, d)
            for t, val in enumerate((dq, dk, dv)):
                if has_acc:
                    val = val + acc_ref.at[t].at[p][rows, :]
                if into is None:
                    out_ref.at[t].at[p][rows, :] = val.astype(out_dtype)
                else:
                    out_ref.at[t][rows, p * LANES:(p + 1) * LANES] = val.astype(out_dtype)

        @pl.when(n == 0)
        def _():
            cq[...] = jnp.zeros_like(cq)
            ck[...] = jnp.zeros_like(ck)
            cv[...] = jnp.zeros_like(cv)
            kp_ref[...] = jnp.zeros_like(kp_ref)
            vp_ref[...] = jnp.zeros_like(vp_ref)

        @pl.when(n < nb)
        def _():
            nd, valid = _band_terms(n, d)
            biases = [_head_biases(sl_ref[:, p * LANES:(p + 1) * LANES], nd, valid) for p in range(NP)]

            def group(items):
                first = []
                for r, p in items:
                    rows = _res_rows(r, d)
                    kc = jnp.concatenate([take(kp_ref, p, r), take(kc_ref, p, r)], axis=0).astype(BF16)
                    vc = jnp.concatenate([take(vp_ref, p, r), take(vc_ref, p, r)], axis=0).astype(BF16)
                    qs = _stack_heads((take(q_ref, p, r) * ATTN_SCALE).astype(BF16))
                    dos = _stack_heads(do_ref.at[p][rows, :].astype(BF16))
                    first.append((kc, qs, dos, _dot_nt(qs, kc), _dot_nt(dos, vc)))
                second = []
                for (r, p), (kc, qs, dos, s, dp) in zip(items, first):
                    rows = _res_rows(r, d)
                    pr = jnp.exp(s + biases[p] - _stack_per_head(lse_ref.at[p][rows, :]))
                    ds = (pr * (dp - _stack_per_head(dl_ref.at[p][rows, :]))).astype(BF16)
                    second.append((kc, qs, dos, pr.astype(BF16), ds))
                for (r, p), (kc, qs, dos, pr, ds) in zip(items, second):
                    dq = _unstack_heads(_dot(ds, kc)) * ATTN_SCALE
                    dk = _dot_tn(ds, qs)
                    dv = _dot_tn(pr, dos)
                    emit(r, p, cq[r, p], ck[r, p] + dk[:BAND, :], cv[r, p] + dv[:BAND, :])
                    cq[r, p] = dq
                    ck[r, p] = dk[BAND:, :]
                    cv[r, p] = dv[BAND:, :]

            _for_groups(d, NP, group)
            kp_ref[...] = kc_ref[...]
            vp_ref[...] = vc_ref[...]

        @pl.when(n == nb)
        def _():
            def last(r):
                for p in range(NP):
                    emit(r, p, cq[r, p], ck[r, p], cv[r, p])
            _for_residues(d, last)

    cur = lambda c, n: (c, jnp.minimum(n, nb - 1), 0)
    lag = lambda c, n: (0, c, jnp.clip(n - 1, 0, nb - 1), 0)

    at = lambda c, n: jnp.minimum(n, nb - 1)
    in_specs = [spec(0, NP, at), spec(1, NP, at), spec(2, NP, at),
                pl.BlockSpec((1, NP * LANES), lambda c, n: (0, c))] + [pl.BlockSpec((NP, rb, LANES), cur)] * 3
    kv_block = in_specs[1].block_shape[1:]
    args = [src, src, src, slopes, do, lse, dl]
    aliases = {}
    if has_acc:
        in_specs.append(pl.BlockSpec((3, NP, rb, LANES), lag))
        args.append(acc)
        if into is None:
            aliases = {7: 0}
    if into is None:
        out_sds = jax.ShapeDtypeStruct((3, NLB, S, LANES), F32)
        out_spec = pl.BlockSpec((3, NP, rb, LANES), lag)
    else:
        in_specs.append(ANY)
        args.append(into)
        aliases = {len(args) - 1: 0}
        out_sds = jax.ShapeDtypeStruct(into.shape, into.dtype)
        out_spec = pl.BlockSpec((3, rb, NP * LANES), lambda c, n: (0, jnp.clip(n - 1, 0, nb - 1), c))
    return pl.pallas_call(
        body, grid=(ncb, nb + 1), name=f"attn_bwd_d{d}",
        in_specs=in_specs, out_specs=out_spec, out_shape=out_sds,
        scratch_shapes=[pltpu.VMEM((d, NP, BAND, LANES), F32)] * 3 + [pltpu.VMEM(kv_block, F32)] * 2,
        input_output_aliases=aliases,
        compiler_params=_cp(("parallel", "arbitrary")),
    )(*args)


def _adamw(w, g, m, v, name):
    R, C = w.shape
    tr = R if R <= 256 else 256
    assert R % tr == 0

    def body(w_ref, g_ref, m_ref, v_ref, d_ref, nm_ref, nv_ref, go_ref):
        g = g_ref[...]
        nm = ADAM_B1 * m_ref[...] + (1.0 - ADAM_B1) * g
        nv = ADAM_B2 * v_ref[...] + (1.0 - ADAM_B2) * (g * g)
        m_hat = nm / (1.0 - ADAM_B1 ** ADAM_STEP)
        v_hat = nv / (1.0 - ADAM_B2 ** ADAM_STEP)
        d_ref[...] = -ADAM_LR * (m_hat / (jnp.sqrt(v_hat) + ADAM_EPS) + ADAM_WD * w_ref[...])
        nm_ref[...] = nm
        nv_ref[...] = nv
        go_ref[...] = g

    blk = pl.BlockSpec((tr, C), lambda i: (i, 0))
    sds = jax.ShapeDtypeStruct((R, C), F32)
    return pl.pallas_call(
        body, grid=(R // tr,), name=name, in_specs=[blk] * 4, out_specs=[blk] * 4, out_shape=[sds] * 4,
        compiler_params=_cp(("parallel",)),
    )(w, g, m, v)


def _coords():
    return lax.axis_index("x"), lax.axis_index("y"), lax.axis_index("c")


def _other_chips(x, y):
    return [(1 - x, y), (x, 1 - y), (1 - x, 1 - y)]


ANY = pl.BlockSpec(memory_space=pl.ANY)


def _cast_into_slot(w, where, name):
    R, C = w.shape
    tr = min(256, R)

    def body(where_ref, w_ref, o_ref):
        o_ref[...] = w_ref[...].astype(BF16)

    grid_spec = pltpu.PrefetchScalarGridSpec(
        num_scalar_prefetch=1, grid=(R // tr,),
        in_specs=[pl.BlockSpec((tr, C), lambda i, w: (i, 0))],
        out_specs=pl.BlockSpec((None, tr, C), lambda i, w: (w[1], i, 0)))
    return pl.pallas_call(
        body, grid_spec=grid_spec, name=name, out_shape=jax.ShapeDtypeStruct((4, R, C), BF16),
        compiler_params=_cp(("parallel",)),
    )(where, w)


def _pair_sum(g, sib, where, name):
    _, n2, C = g.shape
    N = n2 // 2
    tr = min(256, N)
    nt = N // tr

    def body(where_ref, g_ref, s_ref, qb_ref, own_ref):
        q = pl.program_id(1)
        tot = g_ref[...] + s_ref[...]
        qb_ref[...] = tot.astype(BF16)

        @pl.when(q == where_ref[1])
        def _():
            own_ref[...] = tot

    grid_spec = pltpu.PrefetchScalarGridSpec(
        num_scalar_prefetch=1, grid=(nt, 4),
        in_specs=[pl.BlockSpec((None, tr, C), lambda i, q, w: (q, w[0] * nt + i, 0)),
                  pl.BlockSpec((None, tr, C), lambda i, q, w: (q, i, 0))],
        out_specs=[pl.BlockSpec((None, tr, C), lambda i, q, w: (q, i, 0)),
                   pl.BlockSpec((tr, C), lambda i, q, w: (i, 0))])
    return pl.pallas_call(
        body, grid_spec=grid_spec, name=name,
        out_shape=[jax.ShapeDtypeStruct((4, N, C), BF16), jax.ShapeDtypeStruct((N, C), F32)],
        compiler_params=_cp(("parallel", "arbitrary")),
    )(where, g, sib)


HBM = pl.BlockSpec(memory_space=pltpu.HBM)
SEM = pl.BlockSpec(memory_space=pltpu.SEMAPHORE)


def _in_hbm(a):
    return pltpu.with_memory_space_constraint(a, pltpu.HBM)


def _split_start(name, copies, arrays, n_sems, after=None):
    n = len(arrays)

    def body(*refs):
        for cp in copies(refs[:n], refs[-n - 3], refs[-n - 2]):
            cp.start()
        refs[-1][...] = jnp.zeros_like(refs[-1])

    ordered = () if after is None else (after,)
    outs = pl.pallas_call(
        body, name=name,
        out_shape=(pltpu.SemaphoreType.DMA((n_sems,)), pltpu.SemaphoreType.DMA((n_sems,)),
                   *[pltpu.HBM(a.shape, a.dtype) for a in arrays], jax.ShapeDtypeStruct((8, LANES), F32)),
        in_specs=(HBM,) * n + (ANY,) * len(ordered),
        out_specs=(SEM, SEM) + (HBM,) * n + (pl.BlockSpec(memory_space=pltpu.VMEM),),
        input_output_aliases={i: 2 + i for i in range(n)},
        compiler_params=pltpu.CompilerParams(has_side_effects=pltpu.SideEffectType.DATAFLOW_SIDE_EFFECTING),
    )(*[_in_hbm(a) for a in arrays], *ordered)
    return outs[0], outs[1], list(outs[2:2 + n]), outs[-1]


def _split_wait(name, copies, send_sems, recv_sems, arrays, after):
    n = len(arrays)

    def body(*refs):
        for cp in copies(refs[:n], refs[n], refs[n + 1]):
            cp.wait_send()
            cp.wait_recv()

    outs = pl.pallas_call(
        body, name=name,
        out_shape=tuple(pltpu.HBM(a.shape, a.dtype) for a in arrays),
        in_specs=(HBM,) * n + (SEM, SEM, ANY), out_specs=(HBM,) * n,
        input_output_aliases={i: i for i in range(n)},
        compiler_params=pltpu.CompilerParams(has_side_effects=pltpu.SideEffectType.DATAFLOW_SIDE_EFFECTING),
    )(*arrays, send_sems, recv_sems, after)
    return list(outs)


def _remote(src, dst, sems, k, to):
    send_sems, recv_sems = sems
    return pltpu.make_async_remote_copy(src_ref=src, dst_ref=dst, send_sem=send_sems.at[k], recv_sem=recv_sems.at[k],
                                        device_id=to, device_id_type=MESH)


def _chip_at(x, y, rel):
    px = 1 - x if rel & 2 else x
    py = 1 - y if rel & 1 else y
    return px, py, 2 * px + py


def _gather_in_copies(rels):
    def copies(refs, send_sems, recv_sems):
        (w,) = refs
        x, y, c = _coords()
        seg = w.shape[2] // 2
        mine = w.at[2 * x + y, :, pl.ds(c * seg, seg)]
        return [_remote(mine, mine, (send_sems, recv_sems), k, _chip_at(x, y, rel)[:2] + (c,))
                for k, rel in enumerate(rels)]
    return copies


def _gather_out_copies(refs, send_sems, recv_sems):
    (w,) = refs
    x, y, c = _coords()
    mine = w.at[2 * x + y]
    return [_remote(mine, mine, (send_sems, recv_sems), k, (px, py, c)) for k, (px, py) in enumerate(_other_chips(x, y))]


def _swap_copies(refs, send_sems, recv_sems):
    gi, go, si, so = refs
    x, y, c = _coords()
    cps = []
    for a, (src, dst) in enumerate(((gi, si), (go, so))):
        nr = dst.shape[1]
        cps.append(_remote(src.at[:, pl.ds((1 - c) * nr, nr), :], dst, (send_sems, recv_sems), a, (x, y, 1 - c)))
    return cps


def _scatter_copies(refs, send_sems, recv_sems):
    qi, qo, ri, ro = refs
    x, y, c = _coords()
    cps = []
    for k, (px, py) in enumerate(_other_chips(x, y)):
        for a, (src, dst) in enumerate(((qi, ri), (qo, ro))):
            cps.append(_remote(src.at[2 * px + py], dst.at[k], (send_sems, recv_sems), 2 * k + a, (px, py, c)))
    return cps


def _forward_copies(rels):
    def copies(refs, send_sems, recv_sems):
        (w,) = refs
        x, y, c = _coords()
        seg = w.shape[2] // 2
        cps = []
        for k, rel in enumerate(rels):
            got = w.at[_chip_at(x, y, rel)[2], :, pl.ds(c * seg, seg)]
            cps.append(_remote(got, got, (send_sems, recv_sems), k, (x, y, 1 - c)))
        return cps
    return copies


def _chip_sum(own, got, where, name):
    N, C = own.shape
    tr = min(256, N)
    nt = N // tr

    def body(where_ref, own_ref, got_ref, o_ref):
        t = own_ref[...]
        for k in range(3):
            t = t + got_ref[k].astype(F32)
        o_ref[...] = t

    grid_spec = pltpu.PrefetchScalarGridSpec(
        num_scalar_prefetch=1, grid=(nt,),
        in_specs=[pl.BlockSpec((tr, C), lambda i, w: (i, 0)), pl.BlockSpec((3, tr, C), lambda i, w: (0, i, 0))],
        out_specs=pl.BlockSpec((tr, C), lambda i, w: (w[0] * nt + i, 0)))
    return pl.pallas_call(
        body, grid_spec=grid_spec, name=name, out_shape=jax.ShapeDtypeStruct((2 * N, C), F32),
        compiler_params=_cp(("parallel",)),
    )(where, own, got)


def _join_copies(refs, send_sems, recv_sems):
    x, y, c = _coords()
    cps = []
    for a, ref in enumerate(refs):
        nr = ref.shape[0] // 2
        mine = ref.at[pl.ds(c * nr, nr), :]
        cps.append(_remote(mine, mine, (send_sems, recv_sems), a, (x, y, 1 - c)))
    return cps


def _all_reduce_small(part, token):
    R, C = part.shape

    def body(p_ref, _, o_ref, slots, send_sems, recv_sems):
        x, y, c = _coords()
        me = 4 * x + 2 * y + c
        slots[me] = p_ref[...]
        cps = []
        for k in range(1, 8):
            fx, fy, fc = (k >> 2) & 1, (k >> 1) & 1, k & 1
            peer = (1 - x if fx else x, 1 - y if fy else y, 1 - c if fc else c)
            cp = pltpu.make_async_remote_copy(src_ref=p_ref, dst_ref=slots.at[me], send_sem=send_sems.at[k - 1],
                                              recv_sem=recv_sems.at[k - 1], device_id=peer, device_id_type=MESH)
            cp.start()
            cps.append(cp)
        for cp in cps:
            cp.wait()
        t = slots[0]
        for k in range(1, 8):
            t = t + slots[k]
        o_ref[...] = t

    vm = pl.BlockSpec(memory_space=pltpu.VMEM)
    return pl.pallas_call(
        body, name="all_reduce_small", in_specs=[vm, vm], out_specs=vm,
        out_shape=jax.ShapeDtypeStruct((R, C), F32),
        scratch_shapes=[pltpu.VMEM((8, R, C), F32), pltpu.SemaphoreType.DMA((7,)), pltpu.SemaphoreType.DMA((7,))],
    )(part, token)


def _mixers_forward(z, lb_logits, hgrn_gnorm):
    slopes = _alibi_slopes(z[0].shape[1] * LANES)
    yh, states = _hgrn_fwd(z[0], lb_logits, hgrn_gnorm)
    outs, lses = [], []
    for d in DILATIONS:
        o, l = _attn_fwd(z, slopes, d)
        outs.append(o)
        lses.append(l)
    o_attn, lse, ya = _attn_merge(outs, lses, z[0])
    return yh, ya, (states, o_attn, lse, slopes)


def _backward_to_dz(z, kept, lb_logits, hgrn_gnorm, yh, ya, w_out_all, x2, tgt, fgain, h):
    states, o_attn, lse, slopes = kept
    dout, doutb, loss, dfg = _out_proj_loss(yh, ya, w_out_all, x2, tgt, fgain)
    dy = _dy_proj(doutb, w_out_all)
    g_w_out = _grad_w_out(yh, ya, doutb)
    dzh, dlogits, dgn = _hgrn_bwd(z[0], lb_logits, hgrn_gnorm, states, dy)
    do, dl, dza = _attn_gate_bwd(dy, o_attn, z[0])
    acc = None
    order = sorted(DILATIONS, reverse=True)
    for d in order[:-1]:
        acc = _attn_bwd(z, slopes, do, lse, dl, d, acc, None)
    dza = _attn_bwd(z, slopes, do, lse, dl, order[-1], acc, dza)
    sources = [dzh, dza]
    g_w_in = _grad_w_in(h, sources)
    return loss, dfg, dlogits, dgn, g_w_out, g_w_in, sources, dout


def _grad_x_half(sources, w_all, x2, rinv, norm_gain, dout, token, part, gx_prev):
    dh = _dh_proj(sources, w_all, token, part, f"dh_proj_{part}")
    return _rms_bwd(dh, x2, rinv, norm_gain, dout, part, gx_prev, f"rms_bwd_{part}")


def _local_step(x2, tgt, norm_gain, w_all, lb_logits, hgrn_gnorm, w_out_all, fgain):
    token = jnp.zeros((8, LANES), F32)
    where = jnp.zeros((2,), jnp.int32)
    h, rinv = _rms_fwd(x2, norm_gain, token)
    z = _in_proj(h, w_all, where, [(rel, half) for rel in range(4) for half in range(2)], None, token, "in_proj_all")
    yh, ya, kept = _mixers_forward(z, lb_logits, hgrn_gnorm)
    loss, dfg, dlogits, dgn, g_w_out, g_w_in, sources, dout = _backward_to_dz(
        z, kept, lb_logits, hgrn_gnorm, yh, ya, w_out_all, x2, tgt, fgain, h)
    gx, dg0 = _grad_x_half(sources, w_all, x2, rinv, norm_gain, dout, token, 0, None)
    gx, dg1 = _grad_x_half(sources, w_all, x2, rinv, norm_gain, dout, token, 1, gx)
    return loss, gx, dg0 + dg1, g_w_in, dlogits, dgn, g_w_out, dfg


def _pack_small(D, loss, dgain, dlogits, dgn, dfg):
    def row(v):
        v = v.reshape(1, -1)
        return jnp.pad(v, ((0, 0), (0, D - v.shape[1])))
    rows = [row(dgain), row(dfg), row(dlogits[0]), row(dlogits[1]), row(jnp.sum(dgn, axis=0)), row(loss)]
    rows += [jnp.zeros((1, D), F32)] * (8 - len(rows))
    return jnp.concatenate(rows, axis=0)


def kernel(x, norm_gain, w_in, lb_logits, hgrn_gnorm, w_out, final_gain, loss_target, m_norm_gain, m_w_in, m_lb_logits, m_hgrn_gnorm, m_w_out, m_final_gain, v_norm_gain, v_w_in, v_lb_logits, v_hgrn_gnorm, v_w_out, v_final_gain):
    _, S, D = x.shape
    SEG = w_in.shape[2] // 2
    x2 = x[0]
    tgt = loss_target[0]
    fgain = final_gain.reshape(1, D)
    where = jnp.stack([lax.axis_index("c"), 2 * lax.axis_index("x") + lax.axis_index("y")]).astype(jnp.int32)

    wia = _cast_into_slot(w_in[0], where, "cast_w_in")
    woa = _cast_into_slot(w_out[0], where, "cast_w_out")
    near, far = (2, 1), (3,)
    ga = _split_start("gather_near_start", _gather_in_copies(near), [wia], 2)
    h, rinv = _rms_fwd(x2, norm_gain, ga[3])
    z = _in_proj(h, ga[2][0], where, [(0, 0), (0, 1)], None, ga[3], "in_proj_own")
    (wia,) = _split_wait("gather_near_wait", _gather_in_copies(near), ga[0], ga[1], ga[2], z[0])
    gb = _split_start("gather_far_start", _gather_in_copies(far), [wia], 1)
    fa = _split_start("forward_near_start", _forward_copies(near), gb[2], 2, after=gb[3])
    z = _in_proj(h, fa[2][0], where, [(2, "mine"), (1, "mine")], z, fa[3], "in_proj_near")
    (wia,) = _split_wait("forward_near_wait", _forward_copies(near), fa[0], fa[1], fa[2], z[0])
    (wia,) = _split_wait("gather_far_wait", _gather_in_copies(far), gb[0], gb[1], [wia], z[0])
    out_sems = _split_start("gather_out_start", _gather_out_copies, [woa], 3, after=wia)
    fb = _split_start("forward_far_start", _forward_copies(far), [wia], 1, after=out_sems[3])
    z = _in_proj(h, fb[2][0], where, [(3, "mine"), (2, "sibling"), (1, "sibling")], z, fb[3], "in_proj_far")
    (wia,) = _split_wait("forward_far_wait", _forward_copies(far), fb[0], fb[1], fb[2], z[0])
    z = _in_proj(h, wia, where, [(3, "sibling")], z, fb[3], "in_proj_last")
    yh, ya, kept = _mixers_forward(z, lb_logits, hgrn_gnorm)
    (woa,) = _split_wait("gather_out_wait", _gather_out_copies, out_sems[0], out_sems[1], out_sems[2], ya)
    w_out_all = woa.reshape(2 * SEG, D)

    loss, dfg, dlogits, dgn, g_w_out, g_w_in, sources, dout = _backward_to_dz(
        z, kept, lb_logits, hgrn_gnorm, yh, ya, w_out_all, x2, tgt, fgain, h)

    sib_i = lax.empty((4, g_w_in.shape[1] // 2, g_w_in.shape[2]), F32)
    sib_o = lax.empty((4, g_w_out.shape[1] // 2, g_w_out.shape[2]), F32)
    sems = _split_start("swap_start", _swap_copies, [g_w_in, g_w_out, sib_i, sib_o], 2)
    grad_x, dg0 = _grad_x_half(sources, wia, x2, rinv, norm_gain, dout, sems[3], 0, None)
    g_w_in, g_w_out, sib_i, sib_o = _split_wait("swap_wait", _swap_copies, sems[0], sems[1], sems[2], grad_x)
    qi, own_i = _pair_sum(g_w_in, sib_i, where, "pair_sum_w_in")
    qo, own_o = _pair_sum(g_w_out, sib_o, where, "pair_sum_w_out")
    ri = lax.empty((3,) + qi.shape[1:], BF16)
    ro = lax.empty((3,) + qo.shape[1:], BF16)
    sems = _split_start("scatter_start", _scatter_copies, [qi, qo, ri, ro], 6)
    grad_x, dg1 = _grad_x_half(sources, wia, x2, rinv, norm_gain, dout, sems[3], 1, grad_x)
    _, _, got_i, got_o = _split_wait("scatter_wait", _scatter_copies, sems[0], sems[1], sems[2], grad_x)
    jn = _split_start("join_start", _join_copies, [_chip_sum(own_i, got_i, where, "chip_sum_w_in"),
                                                   _chip_sum(own_o, got_o, where, "chip_sum_w_out")], 2)
    small = _all_reduce_small(_pack_small(D, loss, dg0 + dg1, dlogits, dgn, dfg), jn[3])
    loss_sum = small[5, 0]
    d_ng, m_ng, v_ng, grad_norm_gain = _adamw(norm_gain, small[0:1, :], m_norm_gain, v_norm_gain, "adamw_norm_gain")
    d_lb, m_lb, v_lb, grad_lb_logits = _adamw(lb_logits, small[2:4, :SEG], m_lb_logits, v_lb_logits, "adamw_lb_logits")
    d_gn, m_gn, v_gn, grad_hgrn_gnorm = _adamw(hgrn_gnorm, small[4:5, :HGRN_HEAD], m_hgrn_gnorm, v_hgrn_gnorm,
                                               "adamw_hgrn_gnorm")
    d_fg, m_fg, v_fg, grad_final_gain = _adamw(fgain, small[1:2, :], m_final_gain.reshape(1, D),
                                               v_final_gain.reshape(1, D), "adamw_final_gain")
    g_w_in, g_w_out = _split_wait("join_wait", _join_copies, jn[0], jn[1], jn[2], d_fg)
    d_wi, m_wi, v_wi, grad_w_in = _adamw(w_in[0], g_w_in, m_w_in[0], v_w_in[0], "adamw_w_in")
    d_wo, m_wo, v_wo, grad_w_out = _adamw(w_out[0], g_w_out, m_w_out[0], v_w_out[0], "adamw_w_out")

    return (loss_sum, grad_x[None],
            grad_norm_gain, grad_w_in[None], grad_lb_logits, grad_hgrn_gnorm, grad_w_out[None], grad_final_gain[0],
            d_ng, d_wi[None], d_lb, d_gn, d_wo[None], d_fg[0],
            m_ng, m_wi[None], m_lb, m_gn, m_wo[None], m_fg[0],
            v_ng, v_wi[None], v_lb, v_gn, v_wo[None], v_fg[0])
```

```python
import jax
import jax.numpy as jnp
import numpy as np
from jax import lax
from jax.experimental import pallas as pl
from jax.experimental.pallas import tpu as pltpu

F32 = jnp.float32
BF16 = jnp.bfloat16
MESH = pl.DeviceIdType.MESH

NORM_EPS = 1e-6
HGRN_HEAD = 128
HGRN_CHUNK = 64
HGRN_TILE = 128
HGRN_STEP_FWD = (1024, 4)
HGRN_STEP_BWD = (2048, 1)
ATTN_HEAD = 64
LANES = 128
BAND = 128
DILATIONS = (1, 4, 16)
DEINTERLEAVE = 16
ATTN_SCALE = ATTN_HEAD ** -0.5
assert ATTN_SCALE == 0.125
ATTN_BLOCK_ELEMS = BAND * 2048
ATTN_UNROLL = 4
SEG_QKV = 4
SEG_GATE_A = 7
NEG = -1e30

ADAM_LR = 0.001
ADAM_B1 = 0.9
ADAM_B2 = 0.999
ADAM_EPS = 1e-08
ADAM_WD = 0.01
ADAM_STEP = 10

MIB = 1024 * 1024


def _cp(semantics=None, vmem_mib=48):
    return pltpu.CompilerParams(dimension_semantics=semantics, vmem_limit_bytes=vmem_mib * MIB)


def _dot(a, b):
    return jnp.dot(a, b, preferred_element_type=F32)


def _dot_nt(a, b):
    return lax.dot_general(a, b, (((1,), (1,)), ((), ())), preferred_element_type=F32)


def _dot_tn(a, b):
    return lax.dot_general(a, b, (((0,), (0,)), ((), ())), preferred_element_type=F32)


def _split3(x):
    hi = x.astype(BF16)
    r1 = x - hi.astype(F32)
    mid = r1.astype(BF16)
    lo = (r1 - mid.astype(F32)).astype(BF16)
    return hi, mid, lo


def _exact_dot(t_bf16, x):
    hi, mid, lo = _split3(x)
    return _dot(t_bf16, hi) + _dot(t_bf16, mid) + _dot(t_bf16, lo)


def _exact_dot_right(x, t_bf16):
    hi, mid, lo = _split3(x)
    return _dot(hi, t_bf16) + _dot(mid, t_bf16) + _dot(lo, t_bf16)


def _sigmoid(z):
    return jax.nn.sigmoid(z)


def _silu_and_grad(z):
    s = _sigmoid(z)
    return z * s, s * (1.0 + z * (1.0 - s))


def _seg_select(j, values):
    out = values[0]
    for t, v in enumerate(values[1:], 1):
        out = jnp.where(j == t, v, out)
    return out


def _rms_fwd(x2, gain, token):
    S, D = x2.shape
    tm = min(512, S)

    def body(x_ref, g_ref, _, h_ref, r_ref):
        x = x_ref[...]
        r = lax.rsqrt(jnp.mean(x * x, axis=-1, keepdims=True) + NORM_EPS)
        h_ref[...] = ((x * r) * g_ref[...]).astype(BF16)
        r_ref[...] = r

    return pl.pallas_call(
        body, grid=(S // tm,), name="rms_fwd",
        in_specs=[pl.BlockSpec((tm, D), lambda i: (i, 0)), pl.BlockSpec((1, D), lambda i: (0, 0)),
                  pl.BlockSpec(token.shape, lambda i: (0, 0))],
        out_specs=[pl.BlockSpec((tm, D), lambda i: (i, 0)), pl.BlockSpec((tm, 1), lambda i: (i, 0))],
        out_shape=[jax.ShapeDtypeStruct((S, D), BF16), jax.ShapeDtypeStruct((S, 1), F32)],
        compiler_params=_cp(("parallel",)),
    )(x2, gain, token)


def _in_proj(h, w_all, where, segs, z_prev, token, name):
    S, D = h.shape
    SEG = w_all.shape[2] // 2
    NLB = SEG // LANES
    tm = min(1024, S)
    count = len(segs)
    DI = DEINTERLEAVE
    tu = tm // DI

    def is_qkv(seg):
        return (seg >= SEG_QKV) & (seg < SEG_QKV + 3)

    def seg_of(j, w):
        halves = {0: 0, 1: 1, "mine": w[0], "sibling": 1 - w[0]}
        cands = [2 * jnp.bitwise_xor(w[1], rel) + halves[half] for rel, half in segs]
        keys = [is_qkv(s).astype(jnp.int32) for s in cands]
        out = cands[0]
        for k in range(count):
            pos = (sum(jnp.where(keys[t] < keys[k], 1, 0) for t in range(count))
                   + sum(jnp.where(keys[t] == keys[k], 1, 0) for t in range(k)))
            out = jnp.where(pos == j, cands[k], out)
        return out

    def body(*refs):
        where_ref, h_ref, w_ref = refs[:3]
        o_ref, o16_ref = refs[-2:]
        res = _dot(h_ref[...], w_ref[...])
        for p in range(NLB):
            o_ref[p] = res[:, p * LANES:(p + 1) * LANES]

        @pl.when(is_qkv(seg_of(pl.program_id(0), where_ref)))
        def _():
            for p in range(NLB):
                for r in range(DI):
                    o16_ref[p, r] = o_ref.at[p][pl.ds(r, tu, stride=DI), :]

    def z16_map(j, i, w):
        seg = seg_of(j, w)
        return (jnp.where(is_qkv(seg), seg - SEG_QKV, 3), 0, 0, jnp.where(is_qkv(seg), i, 0), 0)

    in_specs = [pl.BlockSpec((tm, D), lambda j, i, w: (i, 0)),
                pl.BlockSpec((None, D, SEG), lambda j, i, w: (seg_of(j, w) // 2, 0, seg_of(j, w) % 2)),
                pl.BlockSpec(token.shape, lambda j, i, w: (0, 0))]
    args = [where, h, w_all, token]
    aliases = {}
    if z_prev is not None:
        in_specs += [ANY, ANY]
        args += list(z_prev)
        aliases = {4: 0, 5: 1}
    grid_spec = pltpu.PrefetchScalarGridSpec(
        num_scalar_prefetch=1, grid=(count, S // tm), in_specs=in_specs,
        out_specs=[pl.BlockSpec((None, NLB, tm, LANES), lambda j, i, w: (seg_of(j, w), 0, i, 0)),
                   pl.BlockSpec((None, NLB, DI, tu, LANES), z16_map)])
    return pl.pallas_call(
        body, grid_spec=grid_spec, name=name,
        out_shape=[jax.ShapeDtypeStruct((8, NLB, S, LANES), F32),
                   jax.ShapeDtypeStruct((4, NLB, DI, S // DI, LANES), F32)],
        input_output_aliases=aliases, compiler_params=_cp(("parallel", "parallel")),
    )(*args)


def _out_proj_loss(yh, ya, w_out, x2, tgt, fgain):
    S, D = x2.shape
    SEG = yh.shape[1]
    tm = min(256, S)
    parts = 2

    def body(yh_ref, ya_ref, w_ref, x_ref, t_ref, fg_ref, dout_ref, doutb_ref, loss_ref, dfg_ref):
        i = pl.program_id(0)

        @pl.when(i == 0)
        def _():
            loss_ref[...] = jnp.zeros_like(loss_ref)
            dfg_ref[...] = jnp.zeros_like(dfg_ref)

        fg = fg_ref[...]
        loss = jnp.zeros((1, 1), F32)
        dfg = jnp.zeros((1, D), F32)
        for rows in [pl.ds(p * (tm // parts), tm // parts) for p in range(parts)]:
            out = (x_ref[rows, :] + _dot(yh_ref[rows, :], w_ref[pl.ds(0, SEG), :])
                   + _dot(ya_ref[rows, :], w_ref[pl.ds(SEG, SEG), :]))
            r = lax.rsqrt(jnp.mean(out * out, axis=-1, keepdims=True) + NORM_EPS)
            n = out * r
            err = n * fg - t_ref[rows, :]
            loss = loss + 0.5 * jnp.sum(jnp.mean(err * err, axis=-1, keepdims=True), axis=0, keepdims=True)
            dy = err * (1.0 / D)
            dfg = dfg + jnp.sum(dy * n, axis=0, keepdims=True)
            dn = dy * fg
            dout = r * (dn - n * jnp.mean(dn * n, axis=-1, keepdims=True))
            dout_ref[rows, :] = dout
            doutb_ref[rows, :] = dout.astype(BF16)
        loss_ref[...] += loss
        dfg_ref[...] += dfg

    row = lambda i: (i, 0)
    fix = lambda i: (0, 0)
    return pl.pallas_call(
        body, grid=(S // tm,), name="out_proj_loss",
        in_specs=[pl.BlockSpec((tm, SEG), row), pl.BlockSpec((tm, SEG), row), pl.BlockSpec((2 * SEG, D), fix),
                  pl.BlockSpec((tm, D), row), pl.BlockSpec((tm, D), row), pl.BlockSpec((1, D), fix)],
        out_specs=[pl.BlockSpec((tm, D), row), pl.BlockSpec((tm, D), row), pl.BlockSpec((1, 1), fix),
                   pl.BlockSpec((1, D), fix)],
        out_shape=[jax.ShapeDtypeStruct((S, D), F32), jax.ShapeDtypeStruct((S, D), BF16),
                   jax.ShapeDtypeStruct((1, 1), F32), jax.ShapeDtypeStruct((1, D), F32)],
        compiler_params=_cp(("arbitrary",)),
    )(yh, ya, w_out, x2, tgt, fgain)


def _dy_proj(doutb, w_out):
    S, D = doutb.shape
    K = w_out.shape[0]
    tm = min(1024, S)

    def body(d_ref, w_ref, o_ref):
        o_ref[...] = _dot_nt(d_ref[...], w_ref[...])

    return pl.pallas_call(
        body, grid=(S // tm,), name="dy_proj",
        in_specs=[pl.BlockSpec((tm, D), lambda i: (i, 0)), pl.BlockSpec((K, D), lambda i: (0, 0))],
        out_specs=pl.BlockSpec((tm, K), lambda i: (i, 0)),
        out_shape=jax.ShapeDtypeStruct((S, K), F32),
        compiler_params=_cp(("parallel",)),
    )(doutb, w_out)


def _grad_w_out(yh, ya, doutb):
    S, SEG = yh.shape
    D = doutb.shape[1]
    R = (2 * SEG) // 4
    nb_half = SEG // R
    tk = min(2048, S)

    def body(yh_ref, ya_ref, d_ref, o_ref):
        q = pl.program_id(0)
        k = pl.program_id(1)

        @pl.when(k == 0)
        def _():
            o_ref[...] = jnp.zeros_like(o_ref)

        @pl.when(q < nb_half)
        def _():
            o_ref[...] += _dot_tn(yh_ref[...], d_ref[...])

        @pl.when(q >= nb_half)
        def _():
            o_ref[...] += _dot_tn(ya_ref[...], d_ref[...])

    return pl.pallas_call(
        body, grid=(4, S // tk), name="grad_w_out",
        in_specs=[pl.BlockSpec((tk, R), lambda q, k: (k, jnp.minimum(q, nb_half - 1))),
                  pl.BlockSpec((tk, R), lambda q, k: (k, jnp.maximum(q - nb_half, 0))),
                  pl.BlockSpec((tk, D), lambda q, k: (k, 0))],
        out_specs=pl.BlockSpec((None, R, D), lambda q, k: (q, 0, 0)),
        out_shape=jax.ShapeDtypeStruct((4, R, D), F32),
        compiler_params=_cp(("parallel", "arbitrary")),
    )(yh, ya, doutb)


def _dz_sources(sources):
    counts = [s.shape[0] for s in sources]
    starts = [sum(counts[:k]) for k in range(len(counts))]
    assert sum(counts) == 8
    return counts, starts


def _row_part(S, part, tile):
    first = max(512, (S * 3 // 8) // 512 * 512)
    rows = first if part == 0 else S - first
    assert rows % tile == 0 and first % tile == 0
    return (0 if part == 0 else first // tile), rows // tile, rows


def _dh_proj(sources, w_all, token, part, name):
    S = sources[0].shape[1]
    D = w_all.shape[1]
    SEG = w_all.shape[2] // 2
    counts, starts = _dz_sources(sources)
    assert all(c % 2 == 0 for c in counts)
    ns = len(sources)
    tm = 1024 if all(_row_part(S, p, 1)[2] % 1024 == 0 for p in (0, 1)) else 512
    t0, nt, nrows = _row_part(S, part, tm)

    def body(*refs):
        src = refs[:ns]
        w_ref, _, o_ref = refs[ns:]
        j = pl.program_id(1)

        @pl.when(j == 0)
        def _():
            o_ref[...] = jnp.zeros_like(o_ref)

        for k in range(ns):
            @pl.when((2 * j >= starts[k]) & (2 * j < starts[k] + counts[k]))
            def _(k=k):
                o_ref[...] += (_dot_nt(src[k][0], w_ref[:, pl.ds(0, SEG)])
                               + _dot_nt(src[k][1], w_ref[:, pl.ds(SEG, SEG)]))

    def src_spec(k):
        return pl.BlockSpec((2, tm, SEG),
                            lambda i, j: (jnp.clip(j - starts[k] // 2, 0, counts[k] // 2 - 1), t0 + i, 0))

    return pl.pallas_call(
        body, grid=(nt, 4), name=name,
        in_specs=[src_spec(k) for k in range(ns)] + [pl.BlockSpec((None, D, 2 * SEG), lambda i, j: (j, 0, 0)),
                                                     pl.BlockSpec(token.shape, lambda i, j: (0, 0))],
        out_specs=pl.BlockSpec((tm, D), lambda i, j: (i, 0)),
        out_shape=jax.ShapeDtypeStruct((nrows, D), F32),
        compiler_params=_cp(("parallel", "arbitrary"), 48 if tm == 512 else 60),
    )(*sources, w_all, token)


def _rms_bwd(dh, x2, rinv, gain, dout, part, gx_prev, name):
    S, D = x2.shape
    tm = 512
    t0, nt, _ = _row_part(S, part, tm)

    def body(dh_ref, x_ref, r_ref, g_ref, dout_ref, *rest):
        gx_ref, dg_ref = rest[-2:]

        @pl.when(pl.program_id(0) == 0)
        def _():
            dg_ref[...] = jnp.zeros_like(dg_ref)

        dh = dh_ref[...]
        r = r_ref[...]
        xhat = x_ref[...] * r
        dg_ref[...] += jnp.sum(dh * xhat, axis=0, keepdims=True)
        dxn = dh * g_ref[...]
        gx_ref[...] = dout_ref[...] + r * (dxn - xhat * jnp.mean(dxn * xhat, axis=-1, keepdims=True))

    row = lambda i: (t0 + i, 0)
    fix = lambda i: (0, 0)
    in_specs = [pl.BlockSpec((tm, D), lambda i: (i, 0)), pl.BlockSpec((tm, D), row), pl.BlockSpec((tm, 1), row),
                pl.BlockSpec((1, D), fix), pl.BlockSpec((tm, D), row)]
    args = [dh, x2, rinv, gain, dout]
    aliases = {}
    if gx_prev is not None:
        in_specs.append(ANY)
        args.append(gx_prev)
        aliases = {5: 0}
    return pl.pallas_call(
        body, grid=(nt,), name=name, in_specs=in_specs,
        out_specs=[pl.BlockSpec((tm, D), row), pl.BlockSpec((1, D), fix)],
        out_shape=[jax.ShapeDtypeStruct((S, D), F32), jax.ShapeDtypeStruct((1, D), F32)],
        input_output_aliases=aliases, compiler_params=_cp(("arbitrary",), 60),
    )(*args)


def _grad_w_in(h, sources):
    S, D = h.shape
    SEG = sources[0].shape[2]
    counts, starts = _dz_sources(sources)
    ns = len(sources)
    tk = min(2048, S)

    def body(*refs):
        h_ref = refs[0]
        src = refs[1:1 + ns]
        o_ref = refs[1 + ns]
        j = pl.program_id(0)
        k = pl.program_id(1)

        @pl.when(k == 0)
        def _():
            o_ref[...] = jnp.zeros_like(o_ref)

        for s in range(ns):
            @pl.when((j >= starts[s]) & (j < starts[s] + counts[s]))
            def _(s=s):
                o_ref[...] += _dot_tn(h_ref[...], src[s][...])

    def src_spec(s):
        return pl.BlockSpec((None, tk, SEG),
                            lambda j, k: (jnp.clip(j - starts[s], 0, counts[s] - 1), k, 0))

    return pl.pallas_call(
        body, grid=(8, S // tk), name="grad_w_in",
        in_specs=[pl.BlockSpec((tk, D), lambda j, k: (k, 0))] + [src_spec(s) for s in range(ns)],
        out_specs=pl.BlockSpec((None, D, SEG), lambda j, k: (j // 2, 0, j % 2)),
        out_shape=jax.ShapeDtypeStruct((4, D, 2 * SEG), F32),
        compiler_params=_cp(("parallel", "arbitrary"), 48 if tk <= 1024 else 62),
    )(h, *sources)


def _lower_bound(lbl):
    l0 = lbl[0:1, :]
    l1 = lbl[1:2, :]
    m = jnp.maximum(l0, l1)
    e0 = jnp.exp(l0 - m)
    e1 = jnp.exp(l1 - m)
    return e0 / (e0 + e1)


def _tile_masks():
    row = lax.broadcasted_iota(jnp.int32, (HGRN_TILE, HGRN_TILE), 0)
    col = lax.broadcasted_iota(jnp.int32, (HGRN_TILE, HGRN_TILE), 1)
    same = (row // HGRN_CHUNK) == (col // HGRN_CHUNK)
    return same & (row >= col), same & (row <= col)


def _chunk_last(b):
    T = b.shape[0]
    b3 = b.reshape(T // HGRN_CHUNK, HGRN_CHUNK, HGRN_HEAD)
    return jnp.broadcast_to(b3[:, HGRN_CHUNK - 1:HGRN_CHUNK, :], b3.shape).reshape(T, HGRN_HEAD)


def _chunk_sum(x):
    T = x.shape[0]
    x3 = x.reshape(T // HGRN_CHUNK, HGRN_CHUNK, HGRN_HEAD)
    return jnp.broadcast_to(jnp.sum(x3, axis=1, keepdims=True), x3.shape).reshape(T, HGRN_HEAD)


def _hgrn_dims(S, SEG, rows):
    T = min(rows, S)
    assert S % T == 0 and T % HGRN_TILE == 0
    tiles = [slice(t * HGRN_TILE, (t + 1) * HGRN_TILE) for t in range(T // HGRN_TILE)]
    chunks = [slice(c * HGRN_CHUNK, (c + 1) * HGRN_CHUNK) for c in range(T // HGRN_CHUNK)]
    return SEG // HGRN_HEAD, T, T // HGRN_CHUNK, S // T, tiles, chunks


def _hgrn_fwd(zf32, lb_logits, gnorm):
    _, NLB, S, _ = zf32.shape
    SEG = NLB * LANES
    H, T, NC, NJ, tiles, chunks = _hgrn_dims(S, SEG, HGRN_STEP_FWD[0])
    HP = min(HGRN_STEP_FWD[1], H)
    assert H % HP == 0

    def body(zq_ref, zf_ref, zi_ref, zg_ref, lbl_ref, gn_ref, y_ref, st_ref, state):
        @pl.when(pl.program_id(1) == 0)
        def _():
            state[...] = jnp.zeros_like(state)

        tril, _ = _tile_masks()
        tril_bf = tril.astype(BF16)
        for hh in range(HP):
            cols = slice(hh * HGRN_HEAD, (hh + 1) * HGRN_HEAD)
            lb = _lower_bound(lbl_ref[:, cols])
            zq = zq_ref[hh]
            q = zq * _sigmoid(zq)
            f = lb + (1.0 - lb) * _sigmoid(zf_ref[hh])
            k = 1.0 - f
            logf = jnp.log(f)
            b = jnp.concatenate([_exact_dot(tril_bf, logf[t]) for t in tiles], axis=0)
            bl = _chunk_last(b)
            qd_b = (q * jnp.exp(b)).astype(BF16)
            kd_b = (k * jnp.exp(-b)).astype(BF16)
            ke_b = (k * jnp.exp(bl - b)).astype(BF16)
            v_b = zi_ref[hh].astype(BF16)
            o_intra = jnp.concatenate(
                [_dot(jnp.where(tril, _dot_nt(qd_b[t], kd_b[t]), 0.0).astype(BF16), v_b[t]) for t in tiles], axis=0)
            kvs = [_dot_tn(v_b[r], ke_b[r]) for r in chunks]
            ebl = jnp.exp(bl)
            st = state[hh]
            sts = []
            for c in range(NC):
                st_ref[c, hh] = st
                sts.append(st.astype(BF16))
                st = st * ebl[c * HGRN_CHUNK:c * HGRN_CHUNK + 1, :] + kvs[c]
            state[hh] = st
            o = o_intra + jnp.concatenate([_dot_nt(qd_b[r], sb) for r, sb in zip(chunks, sts)], axis=0)
            on = o * lax.rsqrt(jnp.mean(o * o, axis=-1, keepdims=True) + NORM_EPS) * gn_ref[...]
            zg = zg_ref[hh]
            y_ref[:, cols] = (on * (zg * _sigmoid(zg))).astype(BF16)

    def zspec(seg):
        return pl.BlockSpec((None, HP, T, HGRN_HEAD), lambda h, j: (seg, h, j, 0))

    return pl.pallas_call(
        body, grid=(H // HP, NJ), name="hgrn_fwd",
        in_specs=[zspec(0), zspec(1), zspec(2), zspec(3),
                  pl.BlockSpec((2, HP * HGRN_HEAD), lambda h, j: (0, h)),
                  pl.BlockSpec((1, HGRN_HEAD), lambda h, j: (0, 0))],
        out_specs=[pl.BlockSpec((T, HP * HGRN_HEAD), lambda h, j: (j, h)),
                   pl.BlockSpec((NC, HP, HGRN_HEAD, HGRN_HEAD), lambda h, j: (j, h, 0, 0))],
        out_shape=[jax.ShapeDtypeStruct((S, SEG), BF16),
                   jax.ShapeDtypeStruct((S // HGRN_CHUNK, H, HGRN_HEAD, HGRN_HEAD), F32)],
        scratch_shapes=[pltpu.VMEM((HP, HGRN_HEAD, HGRN_HEAD), F32)],
        compiler_params=_cp(("parallel", "arbitrary")),
    )(zf32, zf32, zf32, zf32, lb_logits, gnorm)


def _hgrn_bwd(zf32, lb_logits, gnorm, states, dy):
    _, NLB, S, _ = zf32.shape
    SEG = NLB * LANES
    H, T, NC, NJ, tiles, chunks = _hgrn_dims(S, SEG, HGRN_STEP_BWD[0])
    C = HGRN_CHUNK
    HP = min(HGRN_STEP_BWD[1], H)
    assert H % HP == 0

    def body(zq_ref, zf_ref, zi_ref, zg_ref, lbl_ref, gn_ref, st_ref, dy_ref, dz_ref, dl_ref, dgn_ref, gstate):
        @pl.when(pl.program_id(1) == 0)
        def _():
            gstate[...] = jnp.zeros_like(gstate)
            dl_ref[...] = jnp.zeros_like(dl_ref)
            dgn_ref[...] = jnp.zeros_like(dgn_ref)

        gn = gn_ref[...]
        tril, triu = _tile_masks()
        tril_bf = tril.astype(BF16)
        triu_bf = triu.astype(BF16)
        for hh in range(HP):
            cols = slice(hh * HGRN_HEAD, (hh + 1) * HGRN_HEAD)
            lb = _lower_bound(lbl_ref[:, cols])
            q, dq_dz = _silu_and_grad(zq_ref[hh])
            sf = _sigmoid(zf_ref[hh])
            f = lb + (1.0 - lb) * sf
            k = 1.0 - f
            logf = jnp.log(f)
            b = jnp.concatenate([_exact_dot(tril_bf, logf[t]) for t in tiles], axis=0)
            bl = _chunk_last(b)
            eb = jnp.exp(b)
            enb = jnp.exp(-b)
            ekl = jnp.exp(bl - b)
            ebl = jnp.exp(bl)
            qd = q * eb
            kd = k * enb
            ke = k * ekl
            qd_b = qd.astype(BF16)
            kd_b = kd.astype(BF16)
            ke_b = ke.astype(BF16)
            v_b = zi_ref[hh].astype(BF16)
            sts = [st_ref[c, hh] for c in range(NC)]
            sts_b = [s.astype(BF16) for s in sts]
            a_b = [jnp.where(tril, _dot_nt(qd_b[t], kd_b[t]), 0.0).astype(BF16) for t in tiles]
            o = (jnp.concatenate([_dot(a, v_b[t]) for a, t in zip(a_b, tiles)], axis=0)
                 + jnp.concatenate([_dot_nt(qd_b[r], sb) for r, sb in zip(chunks, sts_b)], axis=0))
            rinv = lax.rsqrt(jnp.mean(o * o, axis=-1, keepdims=True) + NORM_EPS)
            ohat = o * rinv
            sg, dsg = _silu_and_grad(zg_ref[hh])
            dyv = dy_ref[:, cols]
            don = dyv * sg
            dz_ref[3, :, cols] = (dyv * (ohat * gn) * dsg).astype(BF16)
            dgn_ref[hh] += jnp.sum(don * ohat, axis=0, keepdims=True)
            dohat = don * gn
            do = rinv * (dohat - ohat * jnp.mean(dohat * ohat, axis=-1, keepdims=True))
            do_b = do.astype(BF16)
            da_b = [jnp.where(tril, _dot_nt(do_b[t], v_b[t]), 0.0).astype(BF16) for t in tiles]
            dv_intra = jnp.concatenate([_dot_tn(a, do_b[t]) for a, t in zip(a_b, tiles)], axis=0)
            dqd_intra = jnp.concatenate([_dot(da, kd_b[t]) for da, t in zip(da_b, tiles)], axis=0)
            dkd = jnp.concatenate([_dot_tn(da, qd_b[t]) for da, t in zip(da_b, tiles)], axis=0)
            dqd_inter = jnp.concatenate([_dot(do_b[r], sb) for r, sb in zip(chunks, sts_b)], axis=0)
            gks = [_dot_tn(do_b[r], qd_b[r]) for r in chunks]
            g = gstate[hh]
            gs = [None] * NC
            for c in reversed(range(NC)):
                gs[c] = g
                g = g * ebl[c * C:c * C + 1, :] + gks[c]
            gstate[hh] = g
            gs_b = [x.astype(BF16) for x in gs]
            dv = dv_intra + jnp.concatenate([_dot_nt(ke_b[r], gb) for r, gb in zip(chunks, gs_b)], axis=0)
            dz_ref[2, :, cols] = dv.astype(BF16)
            dke = jnp.concatenate([_dot(v_b[r], gb) for r, gb in zip(chunks, gs_b)], axis=0)
            debl = jnp.concatenate(
                [jnp.broadcast_to(jnp.sum(x * s, axis=0, keepdims=True), (C, HGRN_HEAD)) for x, s in zip(gs, sts)], axis=0)
            dqd = dqd_intra + dqd_inter
            dz_ref[0, :, cols] = ((dqd * eb) * dq_dz).astype(BF16)
            t_ke = dke * ke
            db = dqd * qd - dkd * kd - t_ke
            db_last = _chunk_sum(t_ke) + debl * ebl
            dk = dkd * enb + dke * ekl
            dlogf = jnp.concatenate([_exact_dot(triu_bf, db[t]) for t in tiles], axis=0) + db_last
            df = dlogf / f - dk
            dz_ref[1, :, cols] = (df * (1.0 - lb) * (sf * (1.0 - sf))).astype(BF16)
            dlb = jnp.sum(df * (1.0 - sf), axis=0, keepdims=True)
            dl0 = dlb * lb * (1.0 - lb)
            dl_ref[0:1, cols] += dl0
            dl_ref[1:2, cols] -= dl0

    def zspec(seg):
        return pl.BlockSpec((None, HP, T, HGRN_HEAD), lambda h, j: (seg, h, NJ - 1 - j, 0))

    return pl.pallas_call(
        body, grid=(H // HP, NJ), name="hgrn_bwd",
        in_specs=[zspec(0), zspec(1), zspec(2), zspec(3),
                  pl.BlockSpec((2, HP * HGRN_HEAD), lambda h, j: (0, h)),
                  pl.BlockSpec((1, HGRN_HEAD), lambda h, j: (0, 0)),
                  pl.BlockSpec((NC, HP, HGRN_HEAD, HGRN_HEAD), lambda h, j: (NJ - 1 - j, h, 0, 0)),
                  pl.BlockSpec((T, HP * HGRN_HEAD), lambda h, j: (NJ - 1 - j, h))],
        out_specs=[pl.BlockSpec((4, T, HP * HGRN_HEAD), lambda h, j: (0, NJ - 1 - j, h)),
                   pl.BlockSpec((2, HP * HGRN_HEAD), lambda h, j: (0, h)),
                   pl.BlockSpec((HP, 1, HGRN_HEAD), lambda h, j: (h, 0, 0))],
        out_shape=[jax.ShapeDtypeStruct((4, S, SEG), BF16), jax.ShapeDtypeStruct((2, SEG), F32),
                   jax.ShapeDtypeStruct((H, 1, HGRN_HEAD), F32)],
        scratch_shapes=[pltpu.VMEM((HP, HGRN_HEAD, HGRN_HEAD), F32)],
        compiler_params=_cp(("parallel", "arbitrary")),
    )(zf32, zf32, zf32, zf32, lb_logits, gnorm, states, dy)


def _alibi_slopes(seg):
    n_heads = seg // ATTN_HEAD
    s = 2.0 ** (-8.0 * np.arange(1, n_heads + 1, dtype=np.float64) / n_heads)
    return jnp.asarray(np.repeat(s, ATTN_HEAD)[None, :], F32)


def _attn_dims(S, SEG, d, block_elems=ATTN_BLOCK_ELEMS):
    rb = BAND * d
    assert S % rb == 0 and SEG % LANES == 0
    npb = max(1, min(SEG // LANES, block_elems // (rb * LANES)))
    assert (SEG // LANES) % npb == 0
    return rb, npb, S // rb, (SEG // LANES) // npb


def _res_rows(r, d):
    return pl.ds(0, BAND) if d == 1 else pl.ds(r, BAND, stride=d)


def _for_residues(d, fn):
    if d == 1:
        fn(0)
    else:
        def step(r, carry):
            fn(r)
            return carry
        lax.fori_loop(0, d, step, 0, unroll=ATTN_UNROLL)


def _for_groups(d, n_pairs, fn):
    def over_pairs(r):
        for g0 in range(0, n_pairs, ATTN_UNROLL):
            fn([(r, p) for p in range(g0, min(n_pairs, g0 + ATTN_UNROLL))])

    if d == 1:
        over_pairs(0)
    elif n_pairs >= ATTN_UNROLL:
        def step(r, carry):
            over_pairs(r)
            return carry
        lax.fori_loop(0, d, step, 0)
    else:
        per_group = ATTN_UNROLL // n_pairs
        assert d % per_group == 0

        def step(g, carry):
            fn([(g * per_group + i, p) for i in range(per_group) for p in range(n_pairs)])
            return carry
        lax.fori_loop(0, d // per_group, step, 0)


def _band_terms(n, d):
    i = lax.broadcasted_iota(jnp.int32, (BAND, 2 * BAND), 0)
    jj = lax.broadcasted_iota(jnp.int32, (BAND, 2 * BAND), 1)
    delta = BAND + i - jj
    valid = (delta >= 0) & (delta <= BAND) & ((n > 0) | (jj >= BAND))
    return (-d * delta).astype(F32), valid


def _head_biases(slopes, nd, valid):
    out = []
    for s in _per_head(slopes):
        s2 = jnp.concatenate([s, s], axis=1)
        out.append(jnp.where(valid, s2 * nd, NEG))
    return jnp.concatenate(out, axis=0)


def _stack_heads(x):
    lane = lax.broadcasted_iota(jnp.int32, x.shape, 1)
    zero = jnp.zeros_like(x)
    return jnp.concatenate([jnp.where(lane < ATTN_HEAD, x, zero), jnp.where(lane < ATTN_HEAD, zero, x)], axis=0)


def _unstack_heads(x2):
    first = lax.broadcasted_iota(jnp.int32, (BAND, LANES), 1) < ATTN_HEAD
    return jnp.where(first, x2[:BAND], x2[BAND:])


def _stack_per_head(x):
    a, b = _per_head(x)
    col = jnp.concatenate([a, b], axis=0)
    return jnp.concatenate([col, col], axis=1)


def _per_head(x):
    lane = lax.broadcasted_iota(jnp.int32, x.shape, 1)
    sw = pltpu.roll(x, ATTN_HEAD, 1)
    first = lane < ATTN_HEAD
    return jnp.where(first, x, sw), jnp.where(first, sw, x)


def _qkv_source(zz, d):
    z, z16 = zz
    if d == DEINTERLEAVE:
        def take(ref, p, r):
            return ref.at[p][r]

        def spec(seg, np_, row_block):
            return pl.BlockSpec((None, np_, d, BAND, LANES), lambda c, n: (seg, c, 0, row_block(c, n), 0))
        return z, z16, take, spec

    def take(ref, p, r):
        return ref.at[p][_res_rows(r, d), :]

    def spec(seg, np_, row_block):
        return pl.BlockSpec((None, np_, BAND * d, LANES), lambda c, n: (SEG_QKV + seg, c, row_block(c, n), 0))
    return z, z, take, spec


def _attn_fwd(qkv, slopes, d):
    qkv, src, take, spec = _qkv_source(qkv, d)
    _, NLB, S, _ = qkv.shape
    rb, NP, nb, ncb = _attn_dims(S, NLB * LANES, d, 2 * ATTN_BLOCK_ELEMS)

    def body(q_ref, kc_ref, vc_ref, sl_ref, o_ref, l_ref, kp_ref, vp_ref):
        n = pl.program_id(1)

        @pl.when(n == 0)
        def _():
            kp_ref[...] = jnp.zeros_like(kp_ref)
            vp_ref[...] = jnp.zeros_like(vp_ref)

        nd, valid = _band_terms(n, d)
        biases = [_head_biases(sl_ref[:, p * LANES:(p + 1) * LANES], nd, valid) for p in range(NP)]

        def group(items):
            scores, values = [], []
            for r, p in items:
                kc = jnp.concatenate([take(kp_ref, p, r), take(kc_ref, p, r)], axis=0).astype(BF16)
                values.append(jnp.concatenate([take(vp_ref, p, r), take(vc_ref, p, r)], axis=0).astype(BF16))
                scores.append(_dot_nt(_stack_heads((take(q_ref, p, r) * ATTN_SCALE).astype(BF16)), kc))
            probs = []
            for (r, p), s in zip(items, scores):
                s = s + biases[p]
                m = jnp.max(s, axis=-1, keepdims=True)
                e = jnp.exp(s - m)
                den = jnp.sum(e, axis=-1, keepdims=True)
                probs.append((e.astype(BF16), den, m + jnp.log(den)))
            for (r, p), vc, (e, den, lse) in zip(items, values, probs):
                rows = _res_rows(r, d)
                o_ref.at[p][rows, :] = _unstack_heads(_dot(e, vc) / den)
                l_ref.at[p][rows, :] = _unstack_heads(jnp.broadcast_to(lse, (2 * BAND, LANES)))

        _for_groups(d, NP, group)
        kp_ref[...] = kc_ref[...]
        vp_ref[...] = vc_ref[...]

    cur = lambda c, n: n
    out = pl.BlockSpec((NP, rb, LANES), lambda c, n: (c, n, 0))
    kv_block = spec(1, NP, cur).block_shape[1:]
    return pl.pallas_call(
        body, grid=(ncb, nb), name=f"attn_fwd_d{d}",
        in_specs=[spec(0, NP, cur), spec(1, NP, cur), spec(2, NP, cur),
                  pl.BlockSpec((1, NP * LANES), lambda c, n: (0, c))],
        out_specs=[out, out],
        out_shape=[jax.ShapeDtypeStruct((NLB, S, LANES), F32)] * 2,
        scratch_shapes=[pltpu.VMEM(kv_block, F32), pltpu.VMEM(kv_block, F32)],
        compiler_params=_cp(("parallel", "arbitrary")),
    )(src, src, src, slopes)


def _attn_merge(outs, lses, zf32):
    NLB, S, _ = outs[0].shape
    SEG = NLB * LANES
    tm = min(256, S)

    def body(o1, o2, o3, l1, l2, l3, zg_ref, o_ref, lse_ref, y_ref):
        a, b, c = l1[...], l2[...], l3[...]
        m = jnp.maximum(jnp.maximum(a, b), c)
        ea, eb, ec = jnp.exp(a - m), jnp.exp(b - m), jnp.exp(c - m)
        tot = ea + eb + ec
        o = (ea / tot) * o1[...] + (eb / tot) * o2[...] + (ec / tot) * o3[...]
        o_ref[...] = o
        lse_ref[...] = m + jnp.log(tot)
        zg = zg_ref[...]
        y = (o * (zg * _sigmoid(zg))).astype(BF16)
        for p in range(NLB):
            y_ref[:, p * LANES:(p + 1) * LANES] = y[p]

    blk = pl.BlockSpec((NLB, tm, LANES), lambda i: (0, i, 0))
    return pl.pallas_call(
        body, grid=(S // tm,), name="attn_merge",
        in_specs=[blk] * 6 + [pl.BlockSpec((None, NLB, tm, LANES), lambda i: (SEG_GATE_A, 0, i, 0))],
        out_specs=[blk, blk, pl.BlockSpec((tm, SEG), lambda i: (i, 0))],
        out_shape=[jax.ShapeDtypeStruct((NLB, S, LANES), F32), jax.ShapeDtypeStruct((NLB, S, LANES), F32),
                   jax.ShapeDtypeStruct((S, SEG), BF16)],
        compiler_params=_cp(("parallel",)),
    )(*outs, *lses, zf32)


def _attn_gate_bwd(dy, o, zf32):
    NP, S, _ = o.shape
    SEG = NP * LANES
    tm = min(512, S)

    def body(dy_ref, o_ref, zg_ref, do_ref, dl_ref, dzg_ref):
        r = lax.broadcasted_iota(jnp.int32, (LANES, LANES), 0) // ATTN_HEAD
        c = lax.broadcasted_iota(jnp.int32, (LANES, LANES), 1) // ATTN_HEAD
        same_head = (r == c).astype(BF16)
        for p in range(NP):
            cols = slice(p * LANES, (p + 1) * LANES)
            sg, dsg = _silu_and_grad(zg_ref[p])
            dyv = dy_ref[:, cols]
            ov = o_ref[p]
            do = dyv * sg
            do_ref[p] = do
            dzg_ref[:, cols] = (dyv * ov * dsg).astype(BF16)
            dl_ref[p] = _exact_dot_right(do * ov, same_head)

    blk = pl.BlockSpec((NP, tm, LANES), lambda i: (0, i, 0))
    return pl.pallas_call(
        body, grid=(S // tm,), name="attn_gate_bwd",
        in_specs=[pl.BlockSpec((tm, SEG), lambda i: (i, 1)), blk,
                  pl.BlockSpec((None, NP, tm, LANES), lambda i: (SEG_GATE_A, 0, i, 0))],
        out_specs=[blk, blk, pl.BlockSpec((None, tm, SEG), lambda i: (3, i, 0))],
        out_shape=[jax.ShapeDtypeStruct((NP, S, LANES), F32), jax.ShapeDtypeStruct((NP, S, LANES), F32),
                   jax.ShapeDtypeStruct((4, S, SEG), BF16)],
        compiler_params=_cp(("parallel",)),
    )(dy, o, zf32)


def _attn_bwd(qkv, slopes, do, lse, dl, d, acc, into):
    qkv, src, take, spec = _qkv_source(qkv, d)
    _, NLB, S, _ = qkv.shape
    SEG = NLB * LANES
    rb, NP, nb, ncb = _attn_dims(S, SEG, d)
    has_acc = acc is not None
    out_dtype = F32 if into is None else into.dtype
    assert into is None or d == 1

    def body(*refs):
        q_ref, kc_ref, vc_ref, sl_ref, do_ref, lse_ref, dl_ref = refs[:7]
        acc_ref = refs[7] if has_acc else None
        out_ref, cq, ck, cv, kp_ref, vp_ref = refs[-6:]
        n = pl.program_id(1)

        def emit(r, p, dq, dk, dv):
            rows = _res_rows(r, d)
            for t, val in enumerate((dq, dk, dv)):
                if has_acc:
                    val = val + acc_ref.at[t].at[p][rows, :]
                if into is None:
                    out_ref.at[t].at[p][rows, :] = val.astype(out_dtype)
                else:
                    out_ref.at[t][rows, p * LANES:(p + 1) * LANES] = val.astype(out_dtype)

        @pl.when(n == 0)
        def _():
            cq[...] = jnp.zeros_like(cq)
            ck[...] = jnp.zeros_like(ck)
            cv[...] = jnp.zeros_like(cv)
            kp_ref[...] = jnp.zeros_like(kp_ref)
            vp_ref[...] = jnp.zeros_like(vp_ref)

        @pl.when(n < nb)
        def _():
            nd, valid = _band_terms(n, d)
            biases = [_head_biases(sl_ref[:, p * LANES:(p + 1) * LANES], nd, valid) for p in range(NP)]

            def group(items):
                first = []
                for r, p in items:
                    rows = _res_rows(r, d)
                    kc = jnp.concatenate([take(kp_ref, p, r), take(kc_ref, p, r)], axis=0).astype(BF16)
                    vc = jnp.concatenate([take(vp_ref, p, r), take(vc_ref, p, r)], axis=0).astype(BF16)
                    qs = _stack_heads((take(q_ref, p, r) * ATTN_SCALE).astype(BF16))
                    dos = _stack_heads(do_ref.at[p][rows, :].astype(BF16))
                    first.append((kc, qs, dos, _dot_nt(qs, kc), _dot_nt(dos, vc)))
                second = []
                for (r, p), (kc, qs, dos, s, dp) in zip(items, first):
                    rows = _res_rows(r, d)
                    pr = jnp.exp(s + biases[p] - _stack_per_head(lse_ref.at[p][rows, :]))
                    ds = (pr * (dp - _stack_per_head(dl_ref.at[p][rows, :]))).astype(BF16)
                    second.append((kc, qs, dos, pr.astype(BF16), ds))
                for (r, p), (kc, qs, dos, pr, ds) in zip(items, second):
                    dq = _unstack_heads(_dot(ds, kc)) * ATTN_SCALE
                    dk = _dot_tn(ds, qs)
                    dv = _dot_tn(pr, dos)
                    emit(r, p, cq[r, p], ck[r, p] + dk[:BAND, :], cv[r, p] + dv[:BAND, :])
                    cq[r, p] = dq
                    ck[r, p] = dk[BAND:, :]
                    cv[r, p] = dv[BAND:, :]

            _for_groups(d, NP, group)
            kp_ref[...] = kc_ref[...]
            vp_ref[...] = vc_ref[...]

        @pl.when(n == nb)
        def _():
            def last(r):
                for p in range(NP):
                    emit(r, p, cq[r, p], ck[r, p], cv[r, p])
            _for_residues(d, last)

    cur = lambda c, n: (c, jnp.minimum(n, nb - 1), 0)
    lag = lambda c, n: (0, c, jnp.clip(n - 1, 0, nb - 1), 0)

    at = lambda c, n: jnp.minimum(n, nb - 1)
    in_specs = [spec(0, NP, at), spec(1, NP, at), spec(2, NP, at),
                pl.BlockSpec((1, NP * LANES), lambda c, n: (0, c))] + [pl.BlockSpec((NP, rb, LANES), cur)] * 3
    kv_block = in_specs[1].block_shape[1:]
    args = [src, src, src, slopes, do, lse, dl]
    aliases = {}
    if has_acc:
        in_specs.append(pl.BlockSpec((3, NP, rb, LANES), lag))
        args.append(acc)
        if into is None:
            aliases = {7: 0}
    if into is None:
        out_sds = jax.ShapeDtypeStruct((3, NLB, S, LANES), F32)
        out_spec = pl.BlockSpec((3, NP, rb, LANES), lag)
    else:
        in_specs.append(ANY)
        args.append(into)
        aliases = {len(args) - 1: 0}
        out_sds = jax.ShapeDtypeStruct(into.shape, into.dtype)
        out_spec = pl.BlockSpec((3, rb, NP * LANES), lambda c, n: (0, jnp.clip(n - 1, 0, nb - 1), c))
    return pl.pallas_call(
        body, grid=(ncb, nb + 1), name=f"attn_bwd_d{d}",
        in_specs=in_specs, out_specs=out_spec, out_shape=out_sds,
        scratch_shapes=[pltpu.VMEM((d, NP, BAND, LANES), F32)] * 3 + [pltpu.VMEM(kv_block, F32)] * 2,
        input_output_aliases=aliases,
        compiler_params=_cp(("parallel", "arbitrary")),
    )(*args)


def _adamw(w, g, m, v, name):
    R, C = w.shape
    tr = R if R <= 256 else 256
    assert R % tr == 0

    def body(w_ref, g_ref, m_ref, v_ref, d_ref, nm_ref, nv_ref, go_ref):
        g = g_ref[...]
        nm = ADAM_B1 * m_ref[...] + (1.0 - ADAM_B1) * g
        nv = ADAM_B2 * v_ref[...] + (1.0 - ADAM_B2) * (g * g)
        m_hat = nm / (1.0 - ADAM_B1 ** ADAM_STEP)
        v_hat = nv / (1.0 - ADAM_B2 ** ADAM_STEP)
        d_ref[...] = -ADAM_LR * (m_hat / (jnp.sqrt(v_hat) + ADAM_EPS) + ADAM_WD * w_ref[...])
        nm_ref[...] = nm
        nv_ref[...] = nv
        go_ref[...] = g

    blk = pl.BlockSpec((tr, C), lambda i: (i, 0))
    sds = jax.ShapeDtypeStruct((R, C), F32)
    return pl.pallas_call(
        body, grid=(R // tr,), name=name, in_specs=[blk] * 4, out_specs=[blk] * 4, out_shape=[sds] * 4,
        compiler_params=_cp(("parallel",)),
    )(w, g, m, v)


def _coords():
    return lax.axis_index("x"), lax.axis_index("y"), lax.axis_index("c")


def _other_chips(x, y):
    return [(1 - x, y), (x, 1 - y), (1 - x, 1 - y)]


ANY = pl.BlockSpec(memory_space=pl.ANY)


def _cast_into_slot(w, where, name):
    R, C = w.shape
    tr = min(256, R)

    def body(where_ref, w_ref, o_ref):
        o_ref[...] = w_ref[...].astype(BF16)

    grid_spec = pltpu.PrefetchScalarGridSpec(
        num_scalar_prefetch=1, grid=(R // tr,),
        in_specs=[pl.BlockSpec((tr, C), lambda i, w: (i, 0))],
        out_specs=pl.BlockSpec((None, tr, C), lambda i, w: (w[1], i, 0)))
    return pl.pallas_call(
        body, grid_spec=grid_spec, name=name, out_shape=jax.ShapeDtypeStruct((4, R, C), BF16),
        compiler_params=_cp(("parallel",)),
    )(where, w)


def _pair_sum(g, sib, where, name):
    _, n2, C = g.shape
    N = n2 // 2
    tr = min(256, N)
    nt = N // tr

    def body(where_ref, g_ref, s_ref, qb_ref, own_ref):
        q = pl.program_id(1)
        tot = g_ref[...] + s_ref[...]
        qb_ref[...] = tot.astype(BF16)

        @pl.when(q == where_ref[1])
        def _():
            own_ref[...] = tot

    grid_spec = pltpu.PrefetchScalarGridSpec(
        num_scalar_prefetch=1, grid=(nt, 4),
        in_specs=[pl.BlockSpec((None, tr, C), lambda i, q, w: (q, w[0] * nt + i, 0)),
                  pl.BlockSpec((None, tr, C), lambda i, q, w: (q, i, 0))],
        out_specs=[pl.BlockSpec((None, tr, C), lambda i, q, w: (q, i, 0)),
                   pl.BlockSpec((tr, C), lambda i, q, w: (i, 0))])
    return pl.pallas_call(
        body, grid_spec=grid_spec, name=name,
        out_shape=[jax.ShapeDtypeStruct((4, N, C), BF16), jax.ShapeDtypeStruct((N, C), F32)],
        compiler_params=_cp(("parallel", "arbitrary")),
    )(where, g, sib)


HBM = pl.BlockSpec(memory_space=pltpu.HBM)
SEM = pl.BlockSpec(memory_space=pltpu.SEMAPHORE)


def _in_hbm(a):
    return pltpu.with_memory_space_constraint(a, pltpu.HBM)


def _split_start(name, copies, arrays, n_sems, after=None):
    n = len(arrays)

    def body(*refs):
        for cp in copies(refs[:n], refs[-n - 3], refs[-n - 2]):
            cp.start()
        refs[-1][...] = jnp.zeros_like(refs[-1])

    ordered = () if after is None else (after,)
    outs = pl.pallas_call(
        body, name=name,
        out_shape=(pltpu.SemaphoreType.DMA((n_sems,)), pltpu.SemaphoreType.DMA((n_sems,)),
                   *[pltpu.HBM(a.shape, a.dtype) for a in arrays], jax.ShapeDtypeStruct((8, LANES), F32)),
        in_specs=(HBM,) * n + (ANY,) * len(ordered),
        out_specs=(SEM, SEM) + (HBM,) * n + (pl.BlockSpec(memory_space=pltpu.VMEM),),
        input_output_aliases={i: 2 + i for i in range(n)},
        compiler_params=pltpu.CompilerParams(has_side_effects=pltpu.SideEffectType.DATAFLOW_SIDE_EFFECTING),
    )(*[_in_hbm(a) for a in arrays], *ordered)
    return outs[0], outs[1], list(outs[2:2 + n]), outs[-1]


def _split_wait(name, copies, send_sems, recv_sems, arrays, after):
    n = len(arrays)

    def body(*refs):
        for cp in copies(refs[:n], refs[n], refs[n + 1]):
            cp.wait_send()
            cp.wait_recv()

    outs = pl.pallas_call(
        body, name=name,
        out_shape=tuple(pltpu.HBM(a.shape, a.dtype) for a in arrays),
        in_specs=(HBM,) * n + (SEM, SEM, ANY), out_specs=(HBM,) * n,
        input_output_aliases={i: i for i in range(n)},
        compiler_params=pltpu.CompilerParams(has_side_effects=pltpu.SideEffectType.DATAFLOW_SIDE_EFFECTING),
    )(*arrays, send_sems, recv_sems, after)
    return list(outs)


def _remote(src, dst, sems, k, to):
    send_sems, recv_sems = sems
    return pltpu.make_async_remote_copy(src_ref=src, dst_ref=dst, send_sem=send_sems.at[k], recv_sem=recv_sems.at[k],
                                        device_id=to, device_id_type=MESH)


def _chip_at(x, y, rel):
    px = 1 - x if rel & 2 else x
    py = 1 - y if rel & 1 else y
    return px, py, 2 * px + py


def _gather_in_copies(rels):
    def copies(refs, send_sems, recv_sems):
        (w,) = refs
        x, y, c = _coords()
        seg = w.shape[2] // 2
        mine = w.at[2 * x + y, :, pl.ds(c * seg, seg)]
        return [_remote(mine, mine, (send_sems, recv_sems), k, _chip_at(x, y, rel)[:2] + (c,))
                for k, rel in enumerate(rels)]
    return copies


def _gather_out_copies(refs, send_sems, recv_sems):
    (w,) = refs
    x, y, c = _coords()
    mine = w.at[2 * x + y]
    return [_remote(mine, mine, (send_sems, recv_sems), k, (px, py, c)) for k, (px, py) in enumerate(_other_chips(x, y))]


def _swap_copies(refs, send_sems, recv_sems):
    gi, go, si, so = refs
    x, y, c = _coords()
    cps = []
    for a, (src, dst) in enumerate(((gi, si), (go, so))):
        nr = dst.shape[1]
        cps.append(_remote(src.at[:, pl.ds((1 - c) * nr, nr), :], dst, (send_sems, recv_sems), a, (x, y, 1 - c)))
    return cps


def _scatter_copies(refs, send_sems, recv_sems):
    qi, qo, ri, ro = refs
    x, y, c = _coords()
    cps = []
    for k, (px, py) in enumerate(_other_chips(x, y)):
        for a, (src, dst) in enumerate(((qi, ri), (qo, ro))):
            cps.append(_remote(src.at[2 * px + py], dst.at[k], (send_sems, recv_sems), 2 * k + a, (px, py, c)))
    return cps


def _forward_copies(rels):
    def copies(refs, send_sems, recv_sems):
        (w,) = refs
        x, y, c = _coords()
        seg = w.shape[2] // 2
        cps = []
        for k, rel in enumerate(rels):
            got = w.at[_chip_at(x, y, rel)[2], :, pl.ds(c * seg, seg)]
            cps.append(_remote(got, got, (send_sems, recv_sems), k, (x, y, 1 - c)))
        return cps
    return copies


def _chip_sum(own, got, where, name):
    N, C = own.shape
    tr = min(256, N)
    nt = N // tr

    def body(where_ref, own_ref, got_ref, o_ref):
        t = own_ref[...]
        for k in range(3):
            t = t + got_ref[k].astype(F32)
        o_ref[...] = t

    grid_spec = pltpu.PrefetchScalarGridSpec(
        num_scalar_prefetch=1, grid=(nt,),
        in_specs=[pl.BlockSpec((tr, C), lambda i, w: (i, 0)), pl.BlockSpec((3, tr, C), lambda i, w: (0, i, 0))],
        out_specs=pl.BlockSpec((tr, C), lambda i, w: (w[0] * nt + i, 0)))
    return pl.pallas_call(
        body, grid_spec=grid_spec, name=name, out_shape=jax.ShapeDtypeStruct((2 * N, C), F32),
        compiler_params=_cp(("parallel",)),
    )(where, own, got)


def _join_copies(refs, send_sems, recv_sems):
    x, y, c = _coords()
    cps = []
    for a, ref in enumerate(refs):
        nr = ref.shape[0] // 2
        mine = ref.at[pl.ds(c * nr, nr), :]
        cps.append(_remote(mine, mine, (send_sems, recv_sems), a, (x, y, 1 - c)))
    return cps


def _all_reduce_small(part, token):
    R, C = part.shape

    def body(p_ref, _, o_ref, slots, send_sems, recv_sems):
        x, y, c = _coords()
        me = 4 * x + 2 * y + c
        slots[me] = p_ref[...]
        cps = []
        for k in range(1, 8):
            fx, fy, fc = (k >> 2) & 1, (k >> 1) & 1, k & 1
            peer = (1 - x if fx else x, 1 - y if fy else y, 1 - c if fc else c)
            cp = pltpu.make_async_remote_copy(src_ref=p_ref, dst_ref=slots.at[me], send_sem=send_sems.at[k - 1],
                                              recv_sem=recv_sems.at[k - 1], device_id=peer, device_id_type=MESH)
            cp.start()
            cps.append(cp)
        for cp in cps:
            cp.wait()
        t = slots[0]
        for k in range(1, 8):
            t = t + slots[k]
        o_ref[...] = t

    vm = pl.BlockSpec(memory_space=pltpu.VMEM)
    return pl.pallas_call(
        body, name="all_reduce_small", in_specs=[vm, vm], out_specs=vm,
        out_shape=jax.ShapeDtypeStruct((R, C), F32),
        scratch_shapes=[pltpu.VMEM((8, R, C), F32), pltpu.SemaphoreType.DMA((7,)), pltpu.SemaphoreType.DMA((7,))],
    )(part, token)


def _mixers_forward(z, lb_logits, hgrn_gnorm):
    slopes = _alibi_slopes(z[0].shape[1] * LANES)
    yh, states = _hgrn_fwd(z[0], lb_logits, hgrn_gnorm)
    outs, lses = [], []
    for d in DILATIONS:
        o, l = _attn_fwd(z, slopes, d)
        outs.append(o)
        lses.append(l)
    o_attn, lse, ya = _attn_merge(outs, lses, z[0])
    return yh, ya, (states, o_attn, lse, slopes)


def _backward_to_dz(z, kept, lb_logits, hgrn_gnorm, yh, ya, w_out_all, x2, tgt, fgain, h):
    states, o_attn, lse, slopes = kept
    dout, doutb, loss, dfg = _out_proj_loss(yh, ya, w_out_all, x2, tgt, fgain)
    dy = _dy_proj(doutb, w_out_all)
    g_w_out = _grad_w_out(yh, ya, doutb)
    dzh, dlogits, dgn = _hgrn_bwd(z[0], lb_logits, hgrn_gnorm, states, dy)
    do, dl, dza = _attn_gate_bwd(dy, o_attn, z[0])
    acc = None
    order = sorted(DILATIONS, reverse=True)
    for d in order[:-1]:
        acc = _attn_bwd(z, slopes, do, lse, dl, d, acc, None)
    dza = _attn_bwd(z, slopes, do, lse, dl, order[-1], acc, dza)
    sources = [dzh, dza]
    g_w_in = _grad_w_in(h, sources)
    return loss, dfg, dlogits, dgn, g_w_out, g_w_in, sources, dout


def _grad_x_half(sources, w_all, x2, rinv, norm_gain, dout, token, part, gx_prev):
    dh = _dh_proj(sources, w_all, token, part, f"dh_proj_{part}")
    return _rms_bwd(dh, x2, rinv, norm_gain, dout, part, gx_prev, f"rms_bwd_{part}")


def _local_step(x2, tgt, norm_gain, w_all, lb_logits, hgrn_gnorm, w_out_all, fgain):
    token = jnp.zeros((8, LANES), F32)
    where = jnp.zeros((2,), jnp.int32)
    h, rinv = _rms_fwd(x2, norm_gain, token)
    z = _in_proj(h, w_all, where, [(rel, half) for rel in range(4) for half in range(2)], None, token, "in_proj_all")
    yh, ya, kept = _mixers_forward(z, lb_logits, hgrn_gnorm)
    loss, dfg, dlogits, dgn, g_w_out, g_w_in, sources, dout = _backward_to_dz(
        z, kept, lb_logits, hgrn_gnorm, yh, ya, w_out_all, x2, tgt, fgain, h)
    gx, dg0 = _grad_x_half(sources, w_all, x2, rinv, norm_gain, dout, token, 0, None)
    gx, dg1 = _grad_x_half(sources, w_all, x2, rinv, norm_gain, dout, token, 1, gx)
    return loss, gx, dg0 + dg1, g_w_in, dlogits, dgn, g_w_out, dfg


def _pack_small(D, loss, dgain, dlogits, dgn, dfg):
    def row(v):
        v = v.reshape(1, -1)
        return jnp.pad(v, ((0, 0), (0, D - v.shape[1])))
    rows = [row(dgain), row(dfg), row(dlogits[0]), row(dlogits[1]), row(jnp.sum(dgn, axis=0)), row(loss)]
    rows += [jnp.zeros((1, D), F32)] * (8 - len(rows))
    return jnp.concatenate(rows, axis=0)


def kernel(x, norm_gain, w_in, lb_logits, hgrn_gnorm, w_out, final_gain, loss_target, m_norm_gain, m_w_in, m_lb_logits, m_hgrn_gnorm, m_w_out, m_final_gain, v_norm_gain, v_w_in, v_lb_logits, v_hgrn_gnorm, v_w_out, v_final_gain):
    _, S, D = x.shape
    SEG = w_in.shape[2] // 2
    x2 = x[0]
    tgt = loss_target[0]
    fgain = final_gain.reshape(1, D)
    where = jnp.stack([lax.axis_index("c"), 2 * lax.axis_index("x") + lax.axis_index("y")]).astype(jnp.int32)

    wia = _cast_into_slot(w_in[0], where, "cast_w_in")
    woa = _cast_into_slot(w_out[0], where, "cast_w_out")
    near, far = (2, 1), (3,)
    ga = _split_start("gather_near_start", _gather_in_copies(near), [wia], 2)
    h, rinv = _rms_fwd(x2, norm_gain, ga[3])
    z = _in_proj(h, ga[2][0], where, [(0, 0), (0, 1)], None, ga[3], "in_proj_own")
    (wia,) = _split_wait("gather_near_wait", _gather_in_copies(near), ga[0], ga[1], ga[2], z[0])
    gb = _split_start("gather_far_start", _gather_in_copies(far), [wia], 1)
    fa = _split_start("forward_near_start", _forward_copies(near), gb[2], 2, after=gb[3])
    z = _in_proj(h, fa[2][0], where, [(2, "mine"), (1, "mine")], z, fa[3], "in_proj_near")
    (wia,) = _split_wait("forward_near_wait", _forward_copies(near), fa[0], fa[1], fa[2], z[0])
    (wia,) = _split_wait("gather_far_wait", _gather_in_copies(far), gb[0], gb[1], [wia], z[0])
    out_sems = _split_start("gather_out_start", _gather_out_copies, [woa], 3, after=wia)
    fb = _split_start("forward_far_start", _forward_copies(far), [wia], 1, after=out_sems[3])
    z = _in_proj(h, fb[2][0], where, [(3, "mine"), (2, "sibling"), (1, "sibling")], z, fb[3], "in_proj_far")
    (wia,) = _split_wait("forward_far_wait", _forward_copies(far), fb[0], fb[1], fb[2], z[0])
    z = _in_proj(h, wia, where, [(3, "sibling")], z, fb[3], "in_proj_last")
    yh, ya, kept = _mixers_forward(z, lb_logits, hgrn_gnorm)
    (woa,) = _split_wait("gather_out_wait", _gather_out_copies, out_sems[0], out_sems[1], out_sems[2], ya)
    w_out_all = woa.reshape(2 * SEG, D)

    loss, dfg, dlogits, dgn, g_w_out, g_w_in, sources, dout = _backward_to_dz(
        z, kept, lb_logits, hgrn_gnorm, yh, ya, w_out_all, x2, tgt, fgain, h)

    sib_i = lax.empty((4, g_w_in.shape[1] // 2, g_w_in.shape[2]), F32)
    sib_o = lax.empty((4, g_w_out.shape[1] // 2, g_w_out.shape[2]), F32)
    sems = _split_start("swap_start", _swap_copies, [g_w_in, g_w_out, sib_i, sib_o], 2)
    grad_x, dg0 = _grad_x_half(sources, wia, x2, rinv, norm_gain, dout, sems[3], 0, None)
    g_w_in, g_w_out, sib_i, sib_o = _split_wait("swap_wait", _swap_copies, sems[0], sems[1], sems[2], grad_x)
    qi, own_i = _pair_sum(g_w_in, sib_i, where, "pair_sum_w_in")
    qo, own_o = _pair_sum(g_w_out, sib_o, where, "pair_sum_w_out")
    ri = lax.empty((3,) + qi.shape[1:], BF16)
    ro = lax.empty((3,) + qo.shape[1:], BF16)
    sems = _split_start("scatter_start", _scatter_copies, [qi, qo, ri, ro], 6)
    grad_x, dg1 = _grad_x_half(sources, wia, x2, rinv, norm_gain, dout, sems[3], 1, grad_x)
    _, _, got_i, got_o = _split_wait("scatter_wait", _scatter_copies, sems[0], sems[1], sems[2], grad_x)
    jn = _split_start("join_start", _join_copies, [_chip_sum(own_i, got_i, where, "chip_sum_w_in"),
                                                   _chip_sum(own_o, got_o, where, "chip_sum_w_out")], 2)
    small = _all_reduce_small(_pack_small(D, loss, dg0 + dg1, dlogits, dgn, dfg), jn[3])
    loss_sum = small[5, 0]
    d_ng, m_ng, v_ng, grad_norm_gain = _adamw(norm_gain, small[0:1, :], m_norm_gain, v_norm_gain, "adamw_norm_gain")
    d_lb, m_lb, v_lb, grad_lb_logits = _adamw(lb_logits, small[2:4, :SEG], m_lb_logits, v_lb_logits, "adamw_lb_logits")
    d_gn, m_gn, v_gn, grad_hgrn_gnorm = _adamw(hgrn_gnorm, small[4:5, :HGRN_HEAD], m_hgrn_gnorm, v_hgrn_gnorm,
                                               "adamw_hgrn_gnorm")
    d_fg, m_fg, v_fg, grad_final_gain = _adamw(fgain, small[1:2, :], m_final_gain.reshape(1, D),
                                               v_final_gain.reshape(1, D), "adamw_final_gain")
    g_w_in, g_w_out = _split_wait("join_wait", _join_copies, jn[0], jn[1], jn[2], d_fg)
    d_wi, m_wi, v_wi, grad_w_in = _adamw(w_in[0], g_w_in, m_w_in[0], v_w_in[0], "adamw_w_in")
    d_wo, m_wo, v_wo, grad_w_out = _adamw(w_out[0], g_w_out, m_w_out[0], v_w_out[0], "adamw_w_out")

    return (loss_sum, grad_x[None],
            grad_norm_gain, grad_w_in[None], grad_lb_logits, grad_hgrn_gnorm, grad_w_out[None], grad_final_gain[0],
            d_ng, d_wi[None], d_lb, d_gn, d_wo[None], d_fg[0],
            m_ng, m_wi[None], m_lb, m_gn, m_wo[None], m_fg[0],
            v_ng, v_wi[None], v_lb, v_gn, v_wo[None], v_fg[0])
```

```python
import jax
import jax.numpy as jnp
import numpy as np
from jax import lax
from jax.experimental import pallas as pl
from jax.experimental.pallas import tpu as pltpu

F32 = jnp.float32
BF16 = jnp.bfloat16
MESH = pl.DeviceIdType.MESH

NORM_EPS = 1e-6
HGRN_HEAD = 128
HGRN_CHUNK = 64
HGRN_TILE = 128
HGRN_STEP_FWD = (1024, 4)
HGRN_STEP_BWD = (2048, 1)
ATTN_HEAD = 64
LANES = 128
BAND = 128
DILATIONS = (1, 4, 16)
DEINTERLEAVE = 16
ATTN_SCALE = ATTN_HEAD ** -0.5
assert ATTN_SCALE == 0.125
ATTN_BLOCK_ELEMS = BAND * 2048
ATTN_UNROLL = 4
SEG_QKV = 4
SEG_GATE_A = 7
NEG = -1e30

ADAM_LR = 0.001
ADAM_B1 = 0.9
ADAM_B2 = 0.999
ADAM_EPS = 1e-08
ADAM_WD = 0.01
ADAM_STEP = 10

MIB = 1024 * 1024


def _cp(semantics=None, vmem_mib=48):
    return pltpu.CompilerParams(dimension_semantics=semantics, vmem_limit_bytes=vmem_mib * MIB)


def _dot(a, b):
    return jnp.dot(a, b, preferred_element_type=F32)


def _dot_nt(a, b):
    return lax.dot_general(a, b, (((1,), (1,)), ((), ())), preferred_element_type=F32)


def _dot_tn(a, b):
    return lax.dot_general(a, b, (((0,), (0,)), ((), ())), preferred_element_type=F32)


def _split3(x):
    hi = x.astype(BF16)
    r1 = x - hi.astype(F32)
    mid = r1.astype(BF16)
    lo = (r1 - mid.astype(F32)).astype(BF16)
    return hi, mid, lo


def _exact_dot(t_bf16, x):
    hi, mid, lo = _split3(x)
    return _dot(t_bf16, hi) + _dot(t_bf16, mid) + _dot(t_bf16, lo)


def _exact_dot_right(x, t_bf16):
    hi, mid, lo = _split3(x)
    return _dot(hi, t_bf16) + _dot(mid, t_bf16) + _dot(lo, t_bf16)


def _sigmoid(z):
    return jax.nn.sigmoid(z)


def _silu_and_grad(z):
    s = _sigmoid(z)
    return z * s, s * (1.0 + z * (1.0 - s))


def _seg_select(j, values):
    out = values[0]
    for t, v in enumerate(values[1:], 1):
        out = jnp.where(j == t, v, out)
    return out


def _rms_fwd(x2, gain, token):
    S, D = x2.shape
    tm = min(512, S)

    def body(x_ref, g_ref, _, h_ref, r_ref):
        x = x_ref[...]
        r = lax.rsqrt(jnp.mean(x * x, axis=-1, keepdims=True) + NORM_EPS)
        h_ref[...] = ((x * r) * g_ref[...]).astype(BF16)
        r_ref[...] = r

    return pl.pallas_call(
        body, grid=(S // tm,), name="rms_fwd",
        in_specs=[pl.BlockSpec((tm, D), lambda i: (i, 0)), pl.BlockSpec((1, D), lambda i: (0, 0)),
                  pl.BlockSpec(token.shape, lambda i: (0, 0))],
        out_specs=[pl.BlockSpec((tm, D), lambda i: (i, 0)), pl.BlockSpec((tm, 1), lambda i: (i, 0))],
        out_shape=[jax.ShapeDtypeStruct((S, D), BF16), jax.ShapeDtypeStruct((S, 1), F32)],
        compiler_params=_cp(("parallel",)),
    )(x2, gain, token)


def _in_proj(h, w_all, where, segs, z_prev, token, name):
    S, D = h.shape
    SEG = w_all.shape[2] // 2
    NLB = SEG // LANES
    tm = min(1024, S)
    count = len(segs)
    DI = DEINTERLEAVE
    tu = tm // DI

    def is_qkv(seg):
        return (seg >= SEG_QKV) & (seg < SEG_QKV + 3)

    def seg_of(j, w):
        halves = {0: 0, 1: 1, "mine": w[0], "sibling": 1 - w[0]}
        cands = [2 * jnp.bitwise_xor(w[1], rel) + halves[half] for rel, half in segs]
        keys = [is_qkv(s).astype(jnp.int32) for s in cands]
        out = cands[0]
        for k in range(count):
            pos = (sum(jnp.where(keys[t] < keys[k], 1, 0) for t in range(count))
                   + sum(jnp.where(keys[t] == keys[k], 1, 0) for t in range(k)))
            out = jnp.where(pos == j, cands[k], out)
        return out

    def body(*refs):
        where_ref, h_ref, w_ref = refs[:3]
        o_ref, o16_ref = refs[-2:]
        res = _dot(h_ref[...], w_ref[...])
        for p in range(NLB):
            o_ref[p] = res[:, p * LANES:(p + 1) * LANES]

        @pl.when(is_qkv(seg_of(pl.program_id(0), where_ref)))
        def _():
            for p in range(NLB):
                for r in range(DI):
                    o16_ref[p, r] = o_ref.at[p][pl.ds(r, tu, stride=DI), :]

    def z16_map(j, i, w):
        seg = seg_of(j, w)
        return (jnp.where(is_qkv(seg), seg - SEG_QKV, 3), 0, 0, jnp.where(is_qkv(seg), i, 0), 0)

    in_specs = [pl.BlockSpec((tm, D), lambda j, i, w: (i, 0)),
                pl.BlockSpec((None, D, SEG), lambda j, i, w: (seg_of(j, w) // 2, 0, seg_of(j, w) % 2)),
                pl.BlockSpec(token.shape, lambda j, i, w: (0, 0))]
    args = [where, h, w_all, token]
    aliases = {}
    if z_prev is not None:
        in_specs += [ANY, ANY]
        args += list(z_prev)
        aliases = {4: 0, 5: 1}
    grid_spec = pltpu.PrefetchScalarGridSpec(
        num_scalar_prefetch=1, grid=(count, S // tm), in_specs=in_specs,
        out_specs=[pl.BlockSpec((None, NLB, tm, LANES), lambda j, i, w: (seg_of(j, w), 0, i, 0)),
                   pl.BlockSpec((None, NLB, DI, tu, LANES), z16_map)])
    return pl.pallas_call(
        body, grid_spec=grid_spec, name=name,
        out_shape=[jax.ShapeDtypeStruct((8, NLB, S, LANES), F32),
                   jax.ShapeDtypeStruct((4, NLB, DI, S // DI, LANES), F32)],
        input_output_aliases=aliases, compiler_params=_cp(("parallel", "parallel")),
    )(*args)


def _out_proj_loss(yh, ya, w_out, x2, tgt, fgain):
    S, D = x2.shape
    SEG = yh.shape[1]
    tm = min(256, S)
    parts = 2

    def body(yh_ref, ya_ref, w_ref, x_ref, t_ref, fg_ref, dout_ref, doutb_ref, loss_ref, dfg_ref):
        i = pl.program_id(0)

        @pl.when(i == 0)
        def _():
            loss_ref[...] = jnp.zeros_like(loss_ref)
            dfg_ref[...] = jnp.zeros_like(dfg_ref)

        fg = fg_ref[...]
        loss = jnp.zeros((1, 1), F32)
        dfg = jnp.zeros((1, D), F32)
        for rows in [pl.ds(p * (tm // parts), tm // parts) for p in range(parts)]:
            out = (x_ref[rows, :] + _dot(yh_ref[rows, :], w_ref[pl.ds(0, SEG), :])
                   + _dot(ya_ref[rows, :], w_ref[pl.ds(SEG, SEG), :]))
            r = lax.rsqrt(jnp.mean(out * out, axis=-1, keepdims=True) + NORM_EPS)
            n = out * r
            err = n * fg - t_ref[rows, :]
            loss = loss + 0.5 * jnp.sum(jnp.mean(err * err, axis=-1, keepdims=True), axis=0, keepdims=True)
            dy = err * (1.0 / D)
            dfg = dfg + jnp.sum(dy * n, axis=0, keepdims=True)
            dn = dy * fg
            dout = r * (dn - n * jnp.mean(dn * n, axis=-1, keepdims=True))
            dout_ref[rows, :] = dout
            doutb_ref[rows, :] = dout.astype(BF16)
        loss_ref[...] += loss
        dfg_ref[...] += dfg

    row = lambda i: (i, 0)
    fix = lambda i: (0, 0)
    return pl.pallas_call(
        body, grid=(S // tm,), name="out_proj_loss",
        in_specs=[pl.BlockSpec((tm, SEG), row), pl.BlockSpec((tm, SEG), row), pl.BlockSpec((2 * SEG, D), fix),
                  pl.BlockSpec((tm, D), row), pl.BlockSpec((tm, D), row), pl.BlockSpec((1, D), fix)],
        out_specs=[pl.BlockSpec((tm, D), row), pl.BlockSpec((tm, D), row), pl.BlockSpec((1, 1), fix),
                   pl.BlockSpec((1, D), fix)],
        out_shape=[jax.ShapeDtypeStruct((S, D), F32), jax.ShapeDtypeStruct((S, D), BF16),
                   jax.ShapeDtypeStruct((1, 1), F32), jax.ShapeDtypeStruct((1, D), F32)],
        compiler_params=_cp(("arbitrary",)),
    )(yh, ya, w_out, x2, tgt, fgain)


def _dy_proj(doutb, w_out):
    S, D = doutb.shape
    K = w_out.shape[0]
    tm = min(1024, S)

    def body(d_ref, w_ref, o_ref):
        o_ref[...] = _dot_nt(d_ref[...], w_ref[...])

    return pl.pallas_call(
        body, grid=(S // tm,), name="dy_proj",
        in_specs=[pl.BlockSpec((tm, D), lambda i: (i, 0)), pl.BlockSpec((K, D), lambda i: (0, 0))],
        out_specs=pl.BlockSpec((tm, K), lambda i: (i, 0)),
        out_shape=jax.ShapeDtypeStruct((S, K), F32),
        compiler_params=_cp(("parallel",)),
    )(doutb, w_out)


def _grad_w_out(yh, ya, doutb):
    S, SEG = yh.shape
    D = doutb.shape[1]
    R = (2 * SEG) // 4
    nb_half = SEG // R
    tk = min(2048, S)

    def body(yh_ref, ya_ref, d_ref, o_ref):
        q = pl.program_id(0)
        k = pl.program_id(1)

        @pl.when(k == 0)
        def _():
            o_ref[...] = jnp.zeros_like(o_ref)

        @pl.when(q < nb_half)
        def _():
            o_ref[...] += _dot_tn(yh_ref[...], d_ref[...])

        @pl.when(q >= nb_half)
        def _():
            o_ref[...] += _dot_tn(ya_ref[...], d_ref[...])

    return pl.pallas_call(
        body, grid=(4, S // tk), name="grad_w_out",
        in_specs=[pl.BlockSpec((tk, R), lambda q, k: (k, jnp.minimum(q, nb_half - 1))),
                  pl.BlockSpec((tk, R), lambda q, k: (k, jnp.maximum(q - nb_half, 0))),
                  pl.BlockSpec((tk, D), lambda q, k: (k, 0))],
        out_specs=pl.BlockSpec((None, R, D), lambda q, k: (q, 0, 0)),
        out_shape=jax.ShapeDtypeStruct((4, R, D), F32),
        compiler_params=_cp(("parallel", "arbitrary")),
    )(yh, ya, doutb)


def _dz_sources(sources):
    counts = [s.shape[0] for s in sources]
    starts = [sum(counts[:k]) for k in range(len(counts))]
    assert sum(counts) == 8
    return counts, starts


def _row_part(S, part, tile):
    first = max(512, (S * 3 // 8) // 512 * 512)
    rows = first if part == 0 else S - first
    assert rows % tile == 0 and first % tile == 0
    return (0 if part == 0 else first // tile), rows // tile, rows


def _dh_proj(sources, w_all, token, part, name):
    S = sources[0].shape[1]
    D = w_all.shape[1]
    SEG = w_all.shape[2] // 2
    counts, starts = _dz_sources(sources)
    assert all(c % 2 == 0 for c in counts)
    ns = len(sources)
    tm = 1024 if all(_row_part(S, p, 1)[2] % 1024 == 0 for p in (0, 1)) else 512
    t0, nt, nrows = _row_part(S, part, tm)

    def body(*refs):
        src = refs[:ns]
        w_ref, _, o_ref = refs[ns:]
        j = pl.program_id(1)

        @pl.when(j == 0)
        def _():
            o_ref[...] = jnp.zeros_like(o_ref)

        for k in range(ns):
            @pl.when((2 * j >= starts[k]) & (2 * j < starts[k] + counts[k]))
            def _(k=k):
                o_ref[...] += (_dot_nt(src[k][0], w_ref[:, pl.ds(0, SEG)])
                               + _dot_nt(src[k][1], w_ref[:, pl.ds(SEG, SEG)]))

    def src_spec(k):
        return pl.BlockSpec((2, tm, SEG),
                            lambda i, j: (jnp.clip(j - starts[k] // 2, 0, counts[k] // 2 - 1), t0 + i, 0))

    return pl.pallas_call(
        body, grid=(nt, 4), name=name,
        in_specs=[src_spec(k) for k in range(ns)] + [pl.BlockSpec((None, D, 2 * SEG), lambda i, j: (j, 0, 0)),
                                                     pl.BlockSpec(token.shape, lambda i, j: (0, 0))],
        out_specs=pl.BlockSpec((tm, D), lambda i, j: (i, 0)),
        out_shape=jax.ShapeDtypeStruct((nrows, D), F32),
        compiler_params=_cp(("parallel", "arbitrary"), 48 if tm == 512 else 60),
    )(*sources, w_all, token)


def _rms_bwd(dh, x2, rinv, gain, dout, part, gx_prev, name):
    S, D = x2.shape
    tm = 256
    t0, nt, _ = _row_part(S, part, tm)

    def body(dh_ref, x_ref, r_ref, g_ref, dout_ref, *rest):
        gx_ref, dg_ref = rest[-2:]

        @pl.when(pl.program_id(0) == 0)
        def _():
            dg_ref[...] = jnp.zeros_like(dg_ref)

        dh = dh_ref[...]
        r = r_ref[...]
        xhat = x_ref[...] * r
        dg_ref[...] += jnp.sum(dh * xhat, axis=0, keepdims=True)
        dxn = dh * g_ref[...]
        gx_ref[...] = dout_ref[...] + r * (dxn - xhat * jnp.mean(dxn * xhat, axis=-1, keepdims=True))

    row = lambda i: (t0 + i, 0)
    fix = lambda i: (0, 0)
    in_specs = [pl.BlockSpec((tm, D), lambda i: (i, 0)), pl.BlockSpec((tm, D), row), pl.BlockSpec((tm, 1), row),
                pl.BlockSpec((1, D), fix), pl.BlockSpec((tm, D), row)]
    args = [dh, x2, rinv, gain, dout]
    aliases = {}
    if gx_prev is not None:
        in_specs.append(ANY)
        args.append(gx_prev)
        aliases = {5: 0}
    return pl.pallas_call(
        body, grid=(nt,), name=name, in_specs=in_specs,
        out_specs=[pl.BlockSpec((tm, D), row), pl.BlockSpec((1, D), fix)],
        out_shape=[jax.ShapeDtypeStruct((S, D), F32), jax.ShapeDtypeStruct((1, D), F32)],
        input_output_aliases=aliases, compiler_params=_cp(("arbitrary",)),
    )(*args)


def _grad_w_in(h, sources):
    S, D = h.shape
    SEG = sources[0].shape[2]
    counts, starts = _dz_sources(sources)
    ns = len(sources)
    tk = min(2048, S)

    def body(*refs):
        h_ref = refs[0]
        src = refs[1:1 + ns]
        o_ref = refs[1 + ns]
        j = pl.program_id(0)
        k = pl.program_id(1)

        @pl.when(k == 0)
        def _():
            o_ref[...] = jnp.zeros_like(o_ref)

        for s in range(ns):
            @pl.when((j >= starts[s]) & (j < starts[s] + counts[s]))
            def _(s=s):
                o_ref[...] += _dot_tn(h_ref[...], src[s][...])

    def src_spec(s):
        return pl.BlockSpec((None, tk, SEG),
                            lambda j, k: (jnp.clip(j - starts[s], 0, counts[s] - 1), k, 0))

    return pl.pallas_call(
        body, grid=(8, S // tk), name="grad_w_in",
        in_specs=[pl.BlockSpec((tk, D), lambda j, k: (k, 0))] + [src_spec(s) for s in range(ns)],
        out_specs=pl.BlockSpec((None, D, SEG), lambda j, k: (j // 2, 0, j % 2)),
        out_shape=jax.ShapeDtypeStruct((4, D, 2 * SEG), F32),
        compiler_params=_cp(("parallel", "arbitrary"), 48 if tk <= 1024 else 62),
    )(h, *sources)


def _lower_bound(lbl):
    l0 = lbl[0:1, :]
    l1 = lbl[1:2, :]
    m = jnp.maximum(l0, l1)
    e0 = jnp.exp(l0 - m)
    e1 = jnp.exp(l1 - m)
    return e0 / (e0 + e1)


def _tile_masks():
    row = lax.broadcasted_iota(jnp.int32, (HGRN_TILE, HGRN_TILE), 0)
    col = lax.broadcasted_iota(jnp.int32, (HGRN_TILE, HGRN_TILE), 1)
    same = (row // HGRN_CHUNK) == (col // HGRN_CHUNK)
    return same & (row >= col), same & (row <= col)


def _chunk_last(b):
    T = b.shape[0]
    b3 = b.reshape(T // HGRN_CHUNK, HGRN_CHUNK, HGRN_HEAD)
    return jnp.broadcast_to(b3[:, HGRN_CHUNK - 1:HGRN_CHUNK, :], b3.shape).reshape(T, HGRN_HEAD)


def _chunk_sum(x):
    T = x.shape[0]
    x3 = x.reshape(T // HGRN_CHUNK, HGRN_CHUNK, HGRN_HEAD)
    return jnp.broadcast_to(jnp.sum(x3, axis=1, keepdims=True), x3.shape).reshape(T, HGRN_HEAD)


def _hgrn_dims(S, SEG, rows):
    T = min(rows, S)
    assert S % T == 0 and T % HGRN_TILE == 0
    tiles = [slice(t * HGRN_TILE, (t + 1) * HGRN_TILE) for t in range(T // HGRN_TILE)]
    chunks = [slice(c * HGRN_CHUNK, (c + 1) * HGRN_CHUNK) for c in range(T // HGRN_CHUNK)]
    return SEG // HGRN_HEAD, T, T // HGRN_CHUNK, S // T, tiles, chunks


def _hgrn_fwd(zf32, lb_logits, gnorm):
    _, NLB, S, _ = zf32.shape
    SEG = NLB * LANES
    H, T, NC, NJ, tiles, chunks = _hgrn_dims(S, SEG, HGRN_STEP_FWD[0])
    HP = min(HGRN_STEP_FWD[1], H)
    assert H % HP == 0

    def body(zq_ref, zf_ref, zi_ref, zg_ref, lbl_ref, gn_ref, y_ref, st_ref, state):
        @pl.when(pl.program_id(1) == 0)
        def _():
            state[...] = jnp.zeros_like(state)

        tril, _ = _tile_masks()
        tril_bf = tril.astype(BF16)
        for hh in range(HP):
            cols = slice(hh * HGRN_HEAD, (hh + 1) * HGRN_HEAD)
            lb = _lower_bound(lbl_ref[:, cols])
            zq = zq_ref[hh]
            q = zq * _sigmoid(zq)
            f = lb + (1.0 - lb) * _sigmoid(zf_ref[hh])
            k = 1.0 - f
            logf = jnp.log(f)
            b = jnp.concatenate([_exact_dot(tril_bf, logf[t]) for t in tiles], axis=0)
            bl = _chunk_last(b)
            qd_b = (q * jnp.exp(b)).astype(BF16)
            kd_b = (k * jnp.exp(-b)).astype(BF16)
            ke_b = (k * jnp.exp(bl - b)).astype(BF16)
            v_b = zi_ref[hh].astype(BF16)
            o_intra = jnp.concatenate(
                [_dot(jnp.where(tril, _dot_nt(qd_b[t], kd_b[t]), 0.0).astype(BF16), v_b[t]) for t in tiles], axis=0)
            kvs = [_dot_tn(v_b[r], ke_b[r]) for r in chunks]
            ebl = jnp.exp(bl)
            st = state[hh]
            sts = []
            for c in range(NC):
                st_ref[c, hh] = st
                sts.append(st.astype(BF16))
                st = st * ebl[c * HGRN_CHUNK:c * HGRN_CHUNK + 1, :] + kvs[c]
            state[hh] = st
            o = o_intra + jnp.concatenate([_dot_nt(qd_b[r], sb) for r, sb in zip(chunks, sts)], axis=0)
            on = o * lax.rsqrt(jnp.mean(o * o, axis=-1, keepdims=True) + NORM_EPS) * gn_ref[...]
            zg = zg_ref[hh]
            y_ref[:, cols] = (on * (zg * _sigmoid(zg))).astype(BF16)

    def zspec(seg):
        return pl.BlockSpec((None, HP, T, HGRN_HEAD), lambda h, j: (seg, h, j, 0))

    return pl.pallas_call(
        body, grid=(H // HP, NJ), name="hgrn_fwd",
        in_specs=[zspec(0), zspec(1), zspec(2), zspec(3),
                  pl.BlockSpec((2, HP * HGRN_HEAD), lambda h, j: (0, h)),
                  pl.BlockSpec((1, HGRN_HEAD), lambda h, j: (0, 0))],
        out_specs=[pl.BlockSpec((T, HP * HGRN_HEAD), lambda h, j: (j, h)),
                   pl.BlockSpec((NC, HP, HGRN_HEAD, HGRN_HEAD), lambda h, j: (j, h, 0, 0))],
        out_shape=[jax.ShapeDtypeStruct((S, SEG), BF16),
                   jax.ShapeDtypeStruct((S // HGRN_CHUNK, H, HGRN_HEAD, HGRN_HEAD), F32)],
        scratch_shapes=[pltpu.VMEM((HP, HGRN_HEAD, HGRN_HEAD), F32)],
        compiler_params=_cp(("parallel", "arbitrary")),
    )(zf32, zf32, zf32, zf32, lb_logits, gnorm)


def _hgrn_bwd(zf32, lb_logits, gnorm, states, dy):
    _, NLB, S, _ = zf32.shape
    SEG = NLB * LANES
    H, T, NC, NJ, tiles, chunks = _hgrn_dims(S, SEG, HGRN_STEP_BWD[0])
    C = HGRN_CHUNK
    HP = min(HGRN_STEP_BWD[1], H)
    assert H % HP == 0

    def body(zq_ref, zf_ref, zi_ref, zg_ref, lbl_ref, gn_ref, st_ref, dy_ref, dz_ref, dl_ref, dgn_ref, gstate):
        @pl.when(pl.program_id(1) == 0)
        def _():
            gstate[...] = jnp.zeros_like(gstate)
            dl_ref[...] = jnp.zeros_like(dl_ref)
            dgn_ref[...] = jnp.zeros_like(dgn_ref)

        gn = gn_ref[...]
        tril, triu = _tile_masks()
        tril_bf = tril.astype(BF16)
        triu_bf = triu.astype(BF16)
        for hh in range(HP):
            cols = slice(hh * HGRN_HEAD, (hh + 1) * HGRN_HEAD)
            lb = _lower_bound(lbl_ref[:, cols])
            q, dq_dz = _silu_and_grad(zq_ref[hh])
            sf = _sigmoid(zf_ref[hh])
            f = lb + (1.0 - lb) * sf
            k = 1.0 - f
            logf = jnp.log(f)
            b = jnp.concatenate([_exact_dot(tril_bf, logf[t]) for t in tiles], axis=0)
            bl = _chunk_last(b)
            eb = jnp.exp(b)
            enb = jnp.exp(-b)
            ekl = jnp.exp(bl - b)
            ebl = jnp.exp(bl)
            qd = q * eb
            kd = k * enb
            ke = k * ekl
            qd_b = qd.astype(BF16)
            kd_b = kd.astype(BF16)
            ke_b = ke.astype(BF16)
            v_b = zi_ref[hh].astype(BF16)
            sts = [st_ref[c, hh] for c in range(NC)]
            sts_b = [s.astype(BF16) for s in sts]
            a_b = [jnp.where(tril, _dot_nt(qd_b[t], kd_b[t]), 0.0).astype(BF16) for t in tiles]
            o = (jnp.concatenate([_dot(a, v_b[t]) for a, t in zip(a_b, tiles)], axis=0)
                 + jnp.concatenate([_dot_nt(qd_b[r], sb) for r, sb in zip(chunks, sts_b)], axis=0))
            rinv = lax.rsqrt(jnp.mean(o * o, axis=-1, keepdims=True) + NORM_EPS)
            ohat = o * rinv
            sg, dsg = _silu_and_grad(zg_ref[hh])
            dyv = dy_ref[:, cols]
            don = dyv * sg
            dz_ref[3, :, cols] = (dyv * (ohat * gn) * dsg).astype(BF16)
            dgn_ref[hh] += jnp.sum(don * ohat, axis=0, keepdims=True)
            dohat = don * gn
            do = rinv * (dohat - ohat * jnp.mean(dohat * ohat, axis=-1, keepdims=True))
            do_b = do.astype(BF16)
            da_b = [jnp.where(tril, _dot_nt(do_b[t], v_b[t]), 0.0).astype(BF16) for t in tiles]
            dv_intra = jnp.concatenate([_dot_tn(a, do_b[t]) for a, t in zip(a_b, tiles)], axis=0)
            dqd_intra = jnp.concatenate([_dot(da, kd_b[t]) for da, t in zip(da_b, tiles)], axis=0)
            dkd = jnp.concatenate([_dot_tn(da, qd_b[t]) for da, t in zip(da_b, tiles)], axis=0)
            dqd_inter = jnp.concatenate([_dot(do_b[r], sb) for r, sb in zip(chunks, sts_b)], axis=0)
            gks = [_dot_tn(do_b[r], qd_b[r]) for r in chunks]
            g = gstate[hh]
            gs = [None] * NC
            for c in reversed(range(NC)):
                gs[c] = g
                g = g * ebl[c * C:c * C + 1, :] + gks[c]
            gstate[hh] = g
            gs_b = [x.astype(BF16) for x in gs]
            dv = dv_intra + jnp.concatenate([_dot_nt(ke_b[r], gb) for r, gb in zip(chunks, gs_b)], axis=0)
            dz_ref[2, :, cols] = dv.astype(BF16)
            dke = jnp.concatenate([_dot(v_b[r], gb) for r, gb in zip(chunks, gs_b)], axis=0)
            debl = jnp.concatenate(
                [jnp.broadcast_to(jnp.sum(x * s, axis=0, keepdims=True), (C, HGRN_HEAD)) for x, s in zip(gs, sts)], axis=0)
            dqd = dqd_intra + dqd_inter
            dz_ref[0, :, cols] = ((dqd * eb) * dq_dz).astype(BF16)
            t_ke = dke * ke
            db = dqd * qd - dkd * kd - t_ke
            db_last = _chunk_sum(t_ke) + debl * ebl
            dk = dkd * enb + dke * ekl
            dlogf = jnp.concatenate([_exact_dot(triu_bf, db[t]) for t in tiles], axis=0) + db_last
            df = dlogf / f - dk
            dz_ref[1, :, cols] = (df * (1.0 - lb) * (sf * (1.0 - sf))).astype(BF16)
            dlb = jnp.sum(df * (1.0 - sf), axis=0, keepdims=True)
            dl0 = dlb * lb * (1.0 - lb)
            dl_ref[0:1, cols] += dl0
            dl_ref[1:2, cols] -= dl0

    def zspec(seg):
        return pl.BlockSpec((None, HP, T, HGRN_HEAD), lambda h, j: (seg, h, NJ - 1 - j, 0))

    return pl.pallas_call(
        body, grid=(H // HP, NJ), name="hgrn_bwd",
        in_specs=[zspec(0), zspec(1), zspec(2), zspec(3),
                  pl.BlockSpec((2, HP * HGRN_HEAD), lambda h, j: (0, h)),
                  pl.BlockSpec((1, HGRN_HEAD), lambda h, j: (0, 0)),
                  pl.BlockSpec((NC, HP, HGRN_HEAD, HGRN_HEAD), lambda h, j: (NJ - 1 - j, h, 0, 0)),
                  pl.BlockSpec((T, HP * HGRN_HEAD), lambda h, j: (NJ - 1 - j, h))],
        out_specs=[pl.BlockSpec((4, T, HP * HGRN_HEAD), lambda h, j: (0, NJ - 1 - j, h)),
                   pl.BlockSpec((2, HP * HGRN_HEAD), lambda h, j: (0, h)),
                   pl.BlockSpec((HP, 1, HGRN_HEAD), lambda h, j: (h, 0, 0))],
        out_shape=[jax.ShapeDtypeStruct((4, S, SEG), BF16), jax.ShapeDtypeStruct((2, SEG), F32),
                   jax.ShapeDtypeStruct((H, 1, HGRN_HEAD), F32)],
        scratch_shapes=[pltpu.VMEM((HP, HGRN_HEAD, HGRN_HEAD), F32)],
        compiler_params=_cp(("parallel", "arbitrary")),
    )(zf32, zf32, zf32, zf32, lb_logits, gnorm, states, dy)


def _alibi_slopes(seg):
    n_heads = seg // ATTN_HEAD
    s = 2.0 ** (-8.0 * np.arange(1, n_heads + 1, dtype=np.float64) / n_heads)
    return jnp.asarray(np.repeat(s, ATTN_HEAD)[None, :], F32)


def _attn_dims(S, SEG, d, block_elems=ATTN_BLOCK_ELEMS):
    rb = BAND * d
    assert S % rb == 0 and SEG % LANES == 0
    npb = max(1, min(SEG // LANES, block_elems // (rb * LANES)))
    assert (SEG // LANES) % npb == 0
    return rb, npb, S // rb, (SEG // LANES) // npb


def _res_rows(r, d):
    return pl.ds(0, BAND) if d == 1 else pl.ds(r, BAND, stride=d)


def _for_residues(d, fn):
    if d == 1:
        fn(0)
    else:
        def step(r, carry):
            fn(r)
            return carry
        lax.fori_loop(0, d, step, 0, unroll=ATTN_UNROLL)


def _for_groups(d, n_pairs, fn):
    def over_pairs(r):
        for g0 in range(0, n_pairs, ATTN_UNROLL):
            fn([(r, p) for p in range(g0, min(n_pairs, g0 + ATTN_UNROLL))])

    if d == 1:
        over_pairs(0)
    elif n_pairs >= ATTN_UNROLL:
        def step(r, carry):
            over_pairs(r)
            return carry
        lax.fori_loop(0, d, step, 0)
    else:
        per_group = ATTN_UNROLL // n_pairs
        assert d % per_group == 0

        def step(g, carry):
            fn([(g * per_group + i, p) for i in range(per_group) for p in range(n_pairs)])
            return carry
        lax.fori_loop(0, d // per_group, step, 0)


def _band_terms(n, d):
    i = lax.broadcasted_iota(jnp.int32, (BAND, 2 * BAND), 0)
    jj = lax.broadcasted_iota(jnp.int32, (BAND, 2 * BAND), 1)
    delta = BAND + i - jj
    valid = (delta >= 0) & (delta <= BAND) & ((n > 0) | (jj >= BAND))
    return (-d * delta).astype(F32), valid


def _head_biases(slopes, nd, valid):
    out = []
    for s in _per_head(slopes):
        s2 = jnp.concatenate([s, s], axis=1)
        out.append(jnp.where(valid, s2 * nd, NEG))
    return jnp.concatenate(out, axis=0)


def _stack_heads(x):
    lane = lax.broadcasted_iota(jnp.int32, x.shape, 1)
    zero = jnp.zeros_like(x)
    return jnp.concatenate([jnp.where(lane < ATTN_HEAD, x, zero), jnp.where(lane < ATTN_HEAD, zero, x)], axis=0)


def _unstack_heads(x2):
    first = lax.broadcasted_iota(jnp.int32, (BAND, LANES), 1) < ATTN_HEAD
    return jnp.where(first, x2[:BAND], x2[BAND:])


def _stack_per_head(x):
    a, b = _per_head(x)
    col = jnp.concatenate([a, b], axis=0)
    return jnp.concatenate([col, col], axis=1)


def _per_head(x):
    lane = lax.broadcasted_iota(jnp.int32, x.shape, 1)
    sw = pltpu.roll(x, ATTN_HEAD, 1)
    first = lane < ATTN_HEAD
    return jnp.where(first, x, sw), jnp.where(first, sw, x)


def _qkv_source(zz, d):
    z, z16 = zz
    if d == DEINTERLEAVE:
        def take(ref, p, r):
            return ref.at[p][r]

        def spec(seg, np_, row_block):
            return pl.BlockSpec((None, np_, d, BAND, LANES), lambda c, n: (seg, c, 0, row_block(c, n), 0))
        return z, z16, take, spec

    def take(ref, p, r):
        return ref.at[p][_res_rows(r, d), :]

    def spec(seg, np_, row_block):
        return pl.BlockSpec((None, np_, BAND * d, LANES), lambda c, n: (SEG_QKV + seg, c, row_block(c, n), 0))
    return z, z, take, spec


def _attn_fwd(qkv, slopes, d):
    qkv, src, take, spec = _qkv_source(qkv, d)
    _, NLB, S, _ = qkv.shape
    rb, NP, nb, ncb = _attn_dims(S, NLB * LANES, d, ATTN_BLOCK_ELEMS * (1 if d == DEINTERLEAVE else 2))

    def body(q_ref, kc_ref, vc_ref, sl_ref, o_ref, l_ref, kp_ref, vp_ref):
        n = pl.program_id(1)

        @pl.when(n == 0)
        def _():
            kp_ref[...] = jnp.zeros_like(kp_ref)
            vp_ref[...] = jnp.zeros_like(vp_ref)

        nd, valid = _band_terms(n, d)
        biases = [_head_biases(sl_ref[:, p * LANES:(p + 1) * LANES], nd, valid) for p in range(NP)]

        def group(items):
            scores, values = [], []
            for r, p in items:
                kc = jnp.concatenate([take(kp_ref, p, r), take(kc_ref, p, r)], axis=0).astype(BF16)
                values.append(jnp.concatenate([take(vp_ref, p, r), take(vc_ref, p, r)], axis=0).astype(BF16))
                scores.append(_dot_nt(_stack_heads((take(q_ref, p, r) * ATTN_SCALE).astype(BF16)), kc))
            probs = []
            for (r, p), s in zip(items, scores):
                s = s + biases[p]
                m = jnp.max(s, axis=-1, keepdims=True)
                e = jnp.exp(s - m)
                den = jnp.sum(e, axis=-1, keepdims=True)
                probs.append((e.astype(BF16), den, m + jnp.log(den)))
            for (r, p), vc, (e, den, lse) in zip(items, values, probs):
                rows = _res_rows(r, d)
                o_ref.at[p][rows, :] = _unstack_heads(_dot(e, vc) / den)
                l_ref.at[p][rows, :] = _unstack_heads(jnp.broadcast_to(lse, (2 * BAND, LANES)))

        _for_groups(d, NP, group)
        kp_ref[...] = kc_ref[...]
        vp_ref[...] = vc_ref[...]

    cur = lambda c, n: n
    out = pl.BlockSpec((NP, rb, LANES), lambda c, n: (c, n, 0))
    kv_block = spec(1, NP, cur).block_shape[1:]
    return pl.pallas_call(
        body, grid=(ncb, nb), name=f"attn_fwd_d{d}",
        in_specs=[spec(0, NP, cur), spec(1, NP, cur), spec(2, NP, cur),
                  pl.BlockSpec((1, NP * LANES), lambda c, n: (0, c))],
        out_specs=[out, out],
        out_shape=[jax.ShapeDtypeStruct((NLB, S, LANES), F32)] * 2,
        scratch_shapes=[pltpu.VMEM(kv_block, F32), pltpu.VMEM(kv_block, F32)],
        compiler_params=_cp(("parallel", "arbitrary")),
    )(src, src, src, slopes)


def _attn_merge(outs, lses, zf32):
    NLB, S, _ = outs[0].shape
    SEG = NLB * LANES
    tm = min(256, S)

    def body(o1, o2, o3, l1, l2, l3, zg_ref, o_ref, lse_ref, y_ref):
        a, b, c = l1[...], l2[...], l3[...]
        m = jnp.maximum(jnp.maximum(a, b), c)
        ea, eb, ec = jnp.exp(a - m), jnp.exp(b - m), jnp.exp(c - m)
        tot = ea + eb + ec
        o = (ea / tot) * o1[...] + (eb / tot) * o2[...] + (ec / tot) * o3[...]
        o_ref[...] = o
        lse_ref[...] = m + jnp.log(tot)
        zg = zg_ref[...]
        y = (o * (zg * _sigmoid(zg))).astype(BF16)
        for p in range(NLB):
            y_ref[:, p * LANES:(p + 1) * LANES] = y[p]

    blk = pl.BlockSpec((NLB, tm, LANES), lambda i: (0, i, 0))
    return pl.pallas_call(
        body, grid=(S // tm,), name="attn_merge",
        in_specs=[blk] * 6 + [pl.BlockSpec((None, NLB, tm, LANES), lambda i: (SEG_GATE_A, 0, i, 0))],
        out_specs=[blk, blk, pl.BlockSpec((tm, SEG), lambda i: (i, 0))],
        out_shape=[jax.ShapeDtypeStruct((NLB, S, LANES), F32), jax.ShapeDtypeStruct((NLB, S, LANES), F32),
                   jax.ShapeDtypeStruct((S, SEG), BF16)],
        compiler_params=_cp(("parallel",)),
    )(*outs, *lses, zf32)


def _attn_gate_bwd(dy, o, zf32):
    NP, S, _ = o.shape
    SEG = NP * LANES
    tm = min(512, S)

    def body(dy_ref, o_ref, zg_ref, do_ref, dl_ref, dzg_ref):
        r = lax.broadcasted_iota(jnp.int32, (LANES, LANES), 0) // ATTN_HEAD
        c = lax.broadcasted_iota(jnp.int32, (LANES, LANES), 1) // ATTN_HEAD
        same_head = (r == c).astype(BF16)
        for p in range(NP):
            cols = slice(p * LANES, (p + 1) * LANES)
            sg, dsg = _silu_and_grad(zg_ref[p])
            dyv = dy_ref[:, cols]
            ov = o_ref[p]
            do = dyv * sg
            do_ref[p] = do
            dzg_ref[:, cols] = (dyv * ov * dsg).astype(BF16)
            dl_ref[p] = _exact_dot_right(do * ov, same_head)

    blk = pl.BlockSpec((NP, tm, LANES), lambda i: (0, i, 0))
    return pl.pallas_call(
        body, grid=(S // tm,), name="attn_gate_bwd",
        in_specs=[pl.BlockSpec((tm, SEG), lambda i: (i, 1)), blk,
                  pl.BlockSpec((None, NP, tm, LANES), lambda i: (SEG_GATE_A, 0, i, 0))],
        out_specs=[blk, blk, pl.BlockSpec((None, tm, SEG), lambda i: (3, i, 0))],
        out_shape=[jax.ShapeDtypeStruct((NP, S, LANES), F32), jax.ShapeDtypeStruct((NP, S, LANES), F32),
                   jax.ShapeDtypeStruct((4, S, SEG), BF16)],
        compiler_params=_cp(("parallel",)),
    )(dy, o, zf32)


def _attn_bwd(qkv, slopes, do, lse, dl, d, acc, into):
    qkv, src, take, spec = _qkv_source(qkv, d)
    _, NLB, S, _ = qkv.shape
    SEG = NLB * LANES
    rb, NP, nb, ncb = _attn_dims(S, SEG, d)
    has_acc = acc is not None
    out_dtype = F32 if into is None else into.dtype
    assert into is None or d == 1

    def body(*refs):
        q_ref, kc_ref, vc_ref, sl_ref, do_ref, lse_ref, dl_ref = refs[:7]
        acc_ref = refs[7] if has_acc else None
        out_ref, cq, ck, cv, kp_ref, vp_ref = refs[-6:]
        n = pl.program_id(1)

        def emit(r, p, dq, dk, dv):
            rows = _res_rows(r, d)
            for t, val in enumerate((dq, dk, dv)):
                if has_acc:
                    val = val + acc_ref.at[t].at[p][rows, :]
                if into is None:
                    out_ref.at[t].at[p][rows, :] = val.astype(out_dtype)
                else:
                    out_ref.at[t][rows, p * LANES:(p + 1) * LANES] = val.astype(out_dtype)

        @pl.when(n == 0)
        def _():
            cq[...] = jnp.zeros_like(cq)
            ck[...] = jnp.zeros_like(ck)
            cv[...] = jnp.zeros_like(cv)
            kp_ref[...] = jnp.zeros_like(kp_ref)
            vp_ref[...] = jnp.zeros_like(vp_ref)

        @pl.when(n < nb)
        def _():
            nd, valid = _band_terms(n, d)
            biases = [_head_biases(sl_ref[:, p * LANES:(p + 1) * LANES], nd, valid) for p in range(NP)]

            def group(items):
                first = []
                for r, p in items:
                    rows = _res_rows(r, d)
                    kc = jnp.concatenate([take(kp_ref, p, r), take(kc_ref, p, r)], axis=0).astype(BF16)
                    vc = jnp.concatenate([take(vp_ref, p, r), take(vc_ref, p, r)], axis=0).astype(BF16)
                    qs = _stack_heads((take(q_ref, p, r) * ATTN_SCALE).astype(BF16))
                    dos = _stack_heads(do_ref.at[p][rows, :].astype(BF16))
                    first.append((kc, qs, dos, _dot_nt(qs, kc), _dot_nt(dos, vc)))
                second = []
                for (r, p), (kc, qs, dos, s, dp) in zip(items, first):
                    rows = _res_rows(r, d)
                    pr = jnp.exp(s + biases[p] - _stack_per_head(lse_ref.at[p][rows, :]))
                    ds = (pr * (dp - _stack_per_head(dl_ref.at[p][rows, :]))).astype(BF16)
                    second.append((kc, qs, dos, pr.astype(BF16), ds))
                for (r, p), (kc, qs, dos, pr, ds) in zip(items, second):
                    dq = _unstack_heads(_dot(ds, kc)) * ATTN_SCALE
                    dk = _dot_tn(ds, qs)
                    dv = _dot_tn(pr, dos)
                    emit(r, p, cq[r, p], ck[r, p] + dk[:BAND, :], cv[r, p] + dv[:BAND, :])
                    cq[r, p] = dq
                    ck[r, p] = dk[BAND:, :]
                    cv[r, p] = dv[BAND:, :]

            _for_groups(d, NP, group)
            kp_ref[...] = kc_ref[...]
            vp_ref[...] = vc_ref[...]

        @pl.when(n == nb)
        def _():
            def last(r):
                for p in range(NP):
                    emit(r, p, cq[r, p], ck[r, p], cv[r, p])
            _for_residues(d, last)

    cur = lambda c, n: (c, jnp.minimum(n, nb - 1), 0)
    lag = lambda c, n: (0, c, jnp.clip(n - 1, 0, nb - 1), 0)

    at = lambda c, n: jnp.minimum(n, nb - 1)
    in_specs = [spec(0, NP, at), spec(1, NP, at), spec(2, NP, at),
                pl.BlockSpec((1, NP * LANES), lambda c, n: (0, c))] + [pl.BlockSpec((NP, rb, LANES), cur)] * 3
    kv_block = in_specs[1].block_shape[1:]
    args = [src, src, src, slopes, do, lse, dl]
    aliases = {}
    if has_acc:
        in_specs.append(pl.BlockSpec((3, NP, rb, LANES), lag))
        args.append(acc)
        if into is None:
            aliases = {7: 0}
    if into is None:
        out_sds = jax.ShapeDtypeStruct((3, NLB, S, LANES), F32)
        out_spec = pl.BlockSpec((3, NP, rb, LANES), lag)
    else:
        in_specs.append(ANY)
        args.append(into)
        aliases = {len(args) - 1: 0}
        out_sds = jax.ShapeDtypeStruct(into.shape, into.dtype)
        out_spec = pl.BlockSpec((3, rb, NP * LANES), lambda c, n: (0, jnp.clip(n - 1, 0, nb - 1), c))
    return pl.pallas_call(
        body, grid=(ncb, nb + 1), name=f"attn_bwd_d{d}",
        in_specs=in_specs, out_specs=out_spec, out_shape=out_sds,
        scratch_shapes=[pltpu.VMEM((d, NP, BAND, LANES), F32)] * 3 + [pltpu.VMEM(kv_block, F32)] * 2,
        input_output_aliases=aliases,
        compiler_params=_cp(("parallel", "arbitrary")),
    )(*args)


def _adamw(w, g, m, v, name):
    R, C = w.shape
    tr = R if R <= 256 else 256
    assert R % tr == 0

    def body(w_ref, g_ref, m_ref, v_ref, d_ref, nm_ref, nv_ref, go_ref):
        g = g_ref[...]
        nm = ADAM_B1 * m_ref[...] + (1.0 - ADAM_B1) * g
        nv = ADAM_B2 * v_ref[...] + (1.0 - ADAM_B2) * (g * g)
        m_hat = nm / (1.0 - ADAM_B1 ** ADAM_STEP)
        v_hat = nv / (1.0 - ADAM_B2 ** ADAM_STEP)
        d_ref[...] = -ADAM_LR * (m_hat / (jnp.sqrt(v_hat) + ADAM_EPS) + ADAM_WD * w_ref[...])
        nm_ref[...] = nm
        nv_ref[...] = nv
        go_ref[...] = g

    blk = pl.BlockSpec((tr, C), lambda i: (i, 0))
    sds = jax.ShapeDtypeStruct((R, C), F32)
    return pl.pallas_call(
        body, grid=(R // tr,), name=name, in_specs=[blk] * 4, out_specs=[blk] * 4, out_shape=[sds] * 4,
        compiler_params=_cp(("parallel",)),
    )(w, g, m, v)


def _coords():
    return lax.axis_index("x"), lax.axis_index("y"), lax.axis_index("c")


def _other_chips(x, y):
    return [(1 - x, y), (x, 1 - y), (1 - x, 1 - y)]


ANY = pl.BlockSpec(memory_space=pl.ANY)


def _cast_into_slot(w, where, name):
    R, C = w.shape
    tr = min(256, R)

    def body(where_ref, w_ref, o_ref):
        o_ref[...] = w_ref[...].astype(BF16)

    grid_spec = pltpu.PrefetchScalarGridSpec(
        num_scalar_prefetch=1, grid=(R // tr,),
        in_specs=[pl.BlockSpec((tr, C), lambda i, w: (i, 0))],
        out_specs=pl.BlockSpec((None, tr, C), lambda i, w: (w[1], i, 0)))
    return pl.pallas_call(
        body, grid_spec=grid_spec, name=name, out_shape=jax.ShapeDtypeStruct((4, R, C), BF16),
        compiler_params=_cp(("parallel",)),
    )(where, w)


def _pair_sum(g, sib, where, name):
    _, n2, C = g.shape
    N = n2 // 2
    tr = min(256, N)
    nt = N // tr

    def body(where_ref, g_ref, s_ref, qb_ref, own_ref):
        q = pl.program_id(1)
        tot = g_ref[...] + s_ref[...]
        qb_ref[...] = tot.astype(BF16)

        @pl.when(q == where_ref[1])
        def _():
            own_ref[...] = tot

    grid_spec = pltpu.PrefetchScalarGridSpec(
        num_scalar_prefetch=1, grid=(nt, 4),
        in_specs=[pl.BlockSpec((None, tr, C), lambda i, q, w: (q, w[0] * nt + i, 0)),
                  pl.BlockSpec((None, tr, C), lambda i, q, w: (q, i, 0))],
        out_specs=[pl.BlockSpec((None, tr, C), lambda i, q, w: (q, i, 0)),
                   pl.BlockSpec((tr, C), lambda i, q, w: (i, 0))])
    return pl.pallas_call(
        body, grid_spec=grid_spec, name=name,
        out_shape=[jax.ShapeDtypeStruct((4, N, C), BF16), jax.ShapeDtypeStruct((N, C), F32)],
        compiler_params=_cp(("parallel", "arbitrary")),
    )(where, g, sib)


HBM = pl.BlockSpec(memory_space=pltpu.HBM)
SEM = pl.BlockSpec(memory_space=pltpu.SEMAPHORE)


def _in_hbm(a):
    return pltpu.with_memory_space_constraint(a, pltpu.HBM)


def _split_start(name, copies, arrays, n_sems, after=None):
    n = len(arrays)

    def body(*refs):
        for cp in copies(refs[:n], refs[-n - 3], refs[-n - 2]):
            cp.start()
        refs[-1][...] = jnp.zeros_like(refs[-1])

    ordered = () if after is None else (after,)
    outs = pl.pallas_call(
        body, name=name,
        out_shape=(pltpu.SemaphoreType.DMA((n_sems,)), pltpu.SemaphoreType.DMA((n_sems,)),
                   *[pltpu.HBM(a.shape, a.dtype) for a in arrays], jax.ShapeDtypeStruct((8, LANES), F32)),
        in_specs=(HBM,) * n + (ANY,) * len(ordered),
        out_specs=(SEM, SEM) + (HBM,) * n + (pl.BlockSpec(memory_space=pltpu.VMEM),),
        input_output_aliases={i: 2 + i for i in range(n)},
        compiler_params=pltpu.CompilerParams(has_side_effects=pltpu.SideEffectType.DATAFLOW_SIDE_EFFECTING),
    )(*[_in_hbm(a) for a in arrays], *ordered)
    return outs[0], outs[1], list(outs[2:2 + n]), outs[-1]


def _split_wait(name, copies, send_sems, recv_sems, arrays, after):
    n = len(arrays)

    def body(*refs):
        for cp in copies(refs[:n], refs[n], refs[n + 1]):
            cp.wait_send()
            cp.wait_recv()

    outs = pl.pallas_call(
        body, name=name,
        out_shape=tuple(pltpu.HBM(a.shape, a.dtype) for a in arrays),
        in_specs=(HBM,) * n + (SEM, SEM, ANY), out_specs=(HBM,) * n,
        input_output_aliases={i: i for i in range(n)},
        compiler_params=pltpu.CompilerParams(has_side_effects=pltpu.SideEffectType.DATAFLOW_SIDE_EFFECTING),
    )(*arrays, send_sems, recv_sems, after)
    return list(outs)


def _remote(src, dst, sems, k, to):
    send_sems, recv_sems = sems
    return pltpu.make_async_remote_copy(src_ref=src, dst_ref=dst, send_sem=send_sems.at[k], recv_sem=recv_sems.at[k],
                                        device_id=to, device_id_type=MESH)


def _chip_at(x, y, rel):
    px = 1 - x if rel & 2 else x
    py = 1 - y if rel & 1 else y
    return px, py, 2 * px + py


def _gather_in_copies(rels):
    def copies(refs, send_sems, recv_sems):
        (w,) = refs
        x, y, c = _coords()
        seg = w.shape[2] // 2
        mine = w.at[2 * x + y, :, pl.ds(c * seg, seg)]
        return [_remote(mine, mine, (send_sems, recv_sems), k, _chip_at(x, y, rel)[:2] + (c,))
                for k, rel in enumerate(rels)]
    return copies


def _gather_out_copies(refs, send_sems, recv_sems):
    (w,) = refs
    x, y, c = _coords()
    mine = w.at[2 * x + y]
    return [_remote(mine, mine, (send_sems, recv_sems), k, (px, py, c)) for k, (px, py) in enumerate(_other_chips(x, y))]


def _swap_copies(refs, send_sems, recv_sems):
    gi, go, si, so = refs
    x, y, c = _coords()
    cps = []
    for a, (src, dst) in enumerate(((gi, si), (go, so))):
        nr = dst.shape[1]
        cps.append(_remote(src.at[:, pl.ds((1 - c) * nr, nr), :], dst, (send_sems, recv_sems), a, (x, y, 1 - c)))
    return cps


def _scatter_copies(refs, send_sems, recv_sems):
    qi, qo, ri, ro = refs
    x, y, c = _coords()
    cps = []
    for k, (px, py) in enumerate(_other_chips(x, y)):
        for a, (src, dst) in enumerate(((qi, ri), (qo, ro))):
            cps.append(_remote(src.at[2 * px + py], dst.at[k], (send_sems, recv_sems), 2 * k + a, (px, py, c)))
    return cps


def _forward_copies(rels):
    def copies(refs, send_sems, recv_sems):
        (w,) = refs
        x, y, c = _coords()
        seg = w.shape[2] // 2
        cps = []
        for k, rel in enumerate(rels):
            got = w.at[_chip_at(x, y, rel)[2], :, pl.ds(c * seg, seg)]
            cps.append(_remote(got, got, (send_sems, recv_sems), k, (x, y, 1 - c)))
        return cps
    return copies


def _chip_sum(own, got, where, name):
    N, C = own.shape
    tr = min(256, N)
    nt = N // tr

    def body(where_ref, own_ref, got_ref, o_ref):
        t = own_ref[...]
        for k in range(3):
            t = t + got_ref[k].astype(F32)
        o_ref[...] = t

    grid_spec = pltpu.PrefetchScalarGridSpec(
        num_scalar_prefetch=1, grid=(nt,),
        in_specs=[pl.BlockSpec((tr, C), lambda i, w: (i, 0)), pl.BlockSpec((3, tr, C), lambda i, w: (0, i, 0))],
        out_specs=pl.BlockSpec((tr, C), lambda i, w: (w[0] * nt + i, 0)))
    return pl.pallas_call(
        body, grid_spec=grid_spec, name=name, out_shape=jax.ShapeDtypeStruct((2 * N, C), F32),
        compiler_params=_cp(("parallel",)),
    )(where, own, got)


def _join_copies(refs, send_sems, recv_sems):
    x, y, c = _coords()
    cps = []
    for a, ref in enumerate(refs):
        nr = ref.shape[0] // 2
        mine = ref.at[pl.ds(c * nr, nr), :]
        cps.append(_remote(mine, mine, (send_sems, recv_sems), a, (x, y, 1 - c)))
    return cps


def _all_reduce_small(part, token):
    R, C = part.shape

    def body(p_ref, _, o_ref, slots, send_sems, recv_sems):
        x, y, c = _coords()
        me = 4 * x + 2 * y + c
        slots[me] = p_ref[...]
        cps = []
        for k in range(1, 8):
            fx, fy, fc = (k >> 2) & 1, (k >> 1) & 1, k & 1
            peer = (1 - x if fx else x, 1 - y if fy else y, 1 - c if fc else c)
            cp = pltpu.make_async_remote_copy(src_ref=p_ref, dst_ref=slots.at[me], send_sem=send_sems.at[k - 1],
                                              recv_sem=recv_sems.at[k - 1], device_id=peer, device_id_type=MESH)
            cp.start()
            cps.append(cp)
        for cp in cps:
            cp.wait()
        t = slots[0]
        for k in range(1, 8):
            t = t + slots[k]
        o_ref[...] = t

    vm = pl.BlockSpec(memory_space=pltpu.VMEM)
    return pl.pallas_call(
        body, name="all_reduce_small", in_specs=[vm, vm], out_specs=vm,
        out_shape=jax.ShapeDtypeStruct((R, C), F32),
        scratch_shapes=[pltpu.VMEM((8, R, C), F32), pltpu.SemaphoreType.DMA((7,)), pltpu.SemaphoreType.DMA((7,))],
    )(part, token)


def _mixers_forward(z, lb_logits, hgrn_gnorm):
    slopes = _alibi_slopes(z[0].shape[1] * LANES)
    yh, states = _hgrn_fwd(z[0], lb_logits, hgrn_gnorm)
    outs, lses = [], []
    for d in DILATIONS:
        o, l = _attn_fwd(z, slopes, d)
        outs.append(o)
        lses.append(l)
    o_attn, lse, ya = _attn_merge(outs, lses, z[0])
    return yh, ya, (states, o_attn, lse, slopes)


def _backward_to_dz(z, kept, lb_logits, hgrn_gnorm, yh, ya, w_out_all, x2, tgt, fgain, h):
    states, o_attn, lse, slopes = kept
    dout, doutb, loss, dfg = _out_proj_loss(yh, ya, w_out_all, x2, tgt, fgain)
    dy = _dy_proj(doutb, w_out_all)
    g_w_out = _grad_w_out(yh, ya, doutb)
    dzh, dlogits, dgn = _hgrn_bwd(z[0], lb_logits, hgrn_gnorm, states, dy)
    do, dl, dza = _attn_gate_bwd(dy, o_attn, z[0])
    acc = None
    order = sorted(DILATIONS, reverse=True)
    for d in order[:-1]:
        acc = _attn_bwd(z, slopes, do, lse, dl, d, acc, None)
    dza = _attn_bwd(z, slopes, do, lse, dl, order[-1], acc, dza)
    sources = [dzh, dza]
    g_w_in = _grad_w_in(h, sources)
    return loss, dfg, dlogits, dgn, g_w_out, g_w_in, sources, dout


def _grad_x_half(sources, w_all, x2, rinv, norm_gain, dout, token, part, gx_prev):
    dh = _dh_proj(sources, w_all, token, part, f"dh_proj_{part}")
    return _rms_bwd(dh, x2, rinv, norm_gain, dout, part, gx_prev, f"rms_bwd_{part}")


def _local_step(x2, tgt, norm_gain, w_all, lb_logits, hgrn_gnorm, w_out_all, fgain):
    token = jnp.zeros((8, LANES), F32)
    where = jnp.zeros((2,), jnp.int32)
    h, rinv = _rms_fwd(x2, norm_gain, token)
    z = _in_proj(h, w_all, where, [(rel, half) for rel in range(4) for half in range(2)], None, token, "in_proj_all")
    yh, ya, kept = _mixers_forward(z, lb_logits, hgrn_gnorm)
    loss, dfg, dlogits, dgn, g_w_out, g_w_in, sources, dout = _backward_to_dz(
        z, kept, lb_logits, hgrn_gnorm, yh, ya, w_out_all, x2, tgt, fgain, h)
    gx, dg0 = _grad_x_half(sources, w_all, x2, rinv, norm_gain, dout, token, 0, None)
    gx, dg1 = _grad_x_half(sources, w_all, x2, rinv, norm_gain, dout, token, 1, gx)
    return loss, gx, dg0 + dg1, g_w_in, dlogits, dgn, g_w_out, dfg


def _pack_small(D, loss, dgain, dlogits, dgn, dfg):
    def row(v):
        v = v.reshape(1, -1)
        return jnp.pad(v, ((0, 0), (0, D - v.shape[1])))
    rows = [row(dgain), row(dfg), row(dlogits[0]), row(dlogits[1]), row(jnp.sum(dgn, axis=0)), row(loss)]
    rows += [jnp.zeros((1, D), F32)] * (8 - len(rows))
    return jnp.concatenate(rows, axis=0)


def kernel(x, norm_gain, w_in, lb_logits, hgrn_gnorm, w_out, final_gain, loss_target, m_norm_gain, m_w_in, m_lb_logits, m_hgrn_gnorm, m_w_out, m_final_gain, v_norm_gain, v_w_in, v_lb_logits, v_hgrn_gnorm, v_w_out, v_final_gain):
    _, S, D = x.shape
    SEG = w_in.shape[2] // 2
    x2 = x[0]
    tgt = loss_target[0]
    fgain = final_gain.reshape(1, D)
    where = jnp.stack([lax.axis_index("c"), 2 * lax.axis_index("x") + lax.axis_index("y")]).astype(jnp.int32)

    wia = _cast_into_slot(w_in[0], where, "cast_w_in")
    woa = _cast_into_slot(w_out[0], where, "cast_w_out")
    near, far = (2, 1), (3,)
    ga = _split_start("gather_near_start", _gather_in_copies(near), [wia], 2)
    h, rinv = _rms_fwd(x2, norm_gain, ga[3])
    z = _in_proj(h, ga[2][0], where, [(0, 0), (0, 1)], None, ga[3], "in_proj_own")
    (wia,) = _split_wait("gather_near_wait", _gather_in_copies(near), ga[0], ga[1], ga[2], z[0])
    gb = _split_start("gather_far_start", _gather_in_copies(far), [wia], 1)
    fa = _split_start("forward_near_start", _forward_copies(near), gb[2], 2, after=gb[3])
    z = _in_proj(h, fa[2][0], where, [(2, "mine"), (1, "mine")], z, fa[3], "in_proj_near")
    (wia,) = _split_wait("forward_near_wait", _forward_copies(near), fa[0], fa[1], fa[2], z[0])
    (wia,) = _split_wait("gather_far_wait", _gather_in_copies(far), gb[0], gb[1], [wia], z[0])
    out_sems = _split_start("gather_out_start", _gather_out_copies, [woa], 3, after=wia)
    fb = _split_start("forward_far_start", _forward_copies(far), [wia], 1, after=out_sems[3])
    z = _in_proj(h, fb[2][0], where, [(3, "mine"), (2, "sibling"), (1, "sibling")], z, fb[3], "in_proj_far")
    (wia,) = _split_wait("forward_far_wait", _forward_copies(far), fb[0], fb[1], fb[2], z[0])
    z = _in_proj(h, wia, where, [(3, "sibling")], z, fb[3], "in_proj_last")
    yh, ya, kept = _mixers_forward(z, lb_logits, hgrn_gnorm)
    (woa,) = _split_wait("gather_out_wait", _gather_out_copies, out_sems[0], out_sems[1], out_sems[2], ya)
    w_out_all = woa.reshape(2 * SEG, D)

    loss, dfg, dlogits, dgn, g_w_out, g_w_in, sources, dout = _backward_to_dz(
        z, kept, lb_logits, hgrn_gnorm, yh, ya, w_out_all, x2, tgt, fgain, h)

    sib_i = lax.empty((4, g_w_in.shape[1] // 2, g_w_in.shape[2]), F32)
    sib_o = lax.empty((4, g_w_out.shape[1] // 2, g_w_out.shape[2]), F32)
    sems = _split_start("swap_start", _swap_copies, [g_w_in, g_w_out, sib_i, sib_o], 2)
    grad_x, dg0 = _grad_x_half(sources, wia, x2, rinv, norm_gain, dout, sems[3], 0, None)
    g_w_in, g_w_out, sib_i, sib_o = _split_wait("swap_wait", _swap_copies, sems[0], sems[1], sems[2], grad_x)
    qi, own_i = _pair_sum(g_w_in, sib_i, where, "pair_sum_w_in")
    qo, own_o = _pair_sum(g_w_out, sib_o, where, "pair_sum_w_out")
    ri = lax.empty((3,) + qi.shape[1:], BF16)
    ro = lax.empty((3,) + qo.shape[1:], BF16)
    sems = _split_start("scatter_start", _scatter_copies, [qi, qo, ri, ro], 6)
    grad_x, dg1 = _grad_x_half(sources, wia, x2, rinv, norm_gain, dout, sems[3], 1, grad_x)
    _, _, got_i, got_o = _split_wait("scatter_wait", _scatter_copies, sems[0], sems[1], sems[2], grad_x)
    jn = _split_start("join_start", _join_copies, [_chip_sum(own_i, got_i, where, "chip_sum_w_in"),
                                                   _chip_sum(own_o, got_o, where, "chip_sum_w_out")], 2)
    small = _all_reduce_small(_pack_small(D, loss, dg0 + dg1, dlogits, dgn, dfg), jn[3])
    loss_sum = small[5, 0]
    d_ng, m_ng, v_ng, grad_norm_gain = _adamw(norm_gain, small[0:1, :], m_norm_gain, v_norm_gain, "adamw_norm_gain")
    d_lb, m_lb, v_lb, grad_lb_logits = _adamw(lb_logits, small[2:4, :SEG], m_lb_logits, v_lb_logits, "adamw_lb_logits")
    d_gn, m_gn, v_gn, grad_hgrn_gnorm = _adamw(hgrn_gnorm, small[4:5, :HGRN_HEAD], m_hgrn_gnorm, v_hgrn_gnorm,
                                               "adamw_hgrn_gnorm")
    d_fg, m_fg, v_fg, grad_final_gain = _adamw(fgain, small[1:2, :], m_final_gain.reshape(1, D),
                                               v_final_gain.reshape(1, D), "adamw_final_gain")
    g_w_in, g_w_out = _split_wait("join_wait", _join_copies, jn[0], jn[1], jn[2], d_fg)
    d_wi, m_wi, v_wi, grad_w_in = _adamw(w_in[0], g_w_in, m_w_in[0], v_w_in[0], "adamw_w_in")
    d_wo, m_wo, v_wo, grad_w_out = _adamw(w_out[0], g_w_out, m_w_out[0], v_w_out[0], "adamw_w_out")

    return (loss_sum, grad_x[None],
            grad_norm_gain, grad_w_in[None], grad_lb_logits, grad_hgrn_gnorm, grad_w_out[None], grad_final_gain[0],
            d_ng, d_wi[None], d_lb, d_gn, d_wo[None], d_fg[0],
            m_ng, m_wi[None], m_lb, m_gn, m_wo[None], m_fg[0],
            v_ng, v_wi[None], v_lb, v_gn, v_wo[None], v_fg[0])
```

```python
import jax
import jax.numpy as jnp
import numpy as np
from jax import lax
from jax.experimental import pallas as pl
from jax.experimental.pallas import tpu as pltpu

F32 = jnp.float32
BF16 = jnp.bfloat16
MESH = pl.DeviceIdType.MESH

NORM_EPS = 1e-6
HGRN_HEAD = 128
HGRN_CHUNK = 64
HGRN_TILE = 128
HGRN_STEP_FWD = (1024, 4)
HGRN_STEP_BWD = (2048, 1)
ATTN_HEAD = 64
LANES = 128
BAND = 128
DILATIONS = (1, 4, 16)
DEINTERLEAVE = 16
ATTN_SCALE = ATTN_HEAD ** -0.5
assert ATTN_SCALE == 0.125
ATTN_BLOCK_ELEMS = BAND * 2048
ATTN_UNROLL = 4
SEG_QKV = 4
SEG_GATE_A = 7
NEG = -1e30

ADAM_LR = 0.001
ADAM_B1 = 0.9
ADAM_B2 = 0.999
ADAM_EPS = 1e-08
ADAM_WD = 0.01
ADAM_STEP = 10

MIB = 1024 * 1024


def _cp(semantics=None, vmem_mib=48):
    return pltpu.CompilerParams(dimension_semantics=semantics, vmem_limit_bytes=vmem_mib * MIB)


def _dot(a, b):
    return jnp.dot(a, b, preferred_element_type=F32)


def _dot_nt(a, b):
    return lax.dot_general(a, b, (((1,), (1,)), ((), ())), preferred_element_type=F32)


def _dot_tn(a, b):
    return lax.dot_general(a, b, (((0,), (0,)), ((), ())), preferred_element_type=F32)


def _split3(x):
    hi = x.astype(BF16)
    r1 = x - hi.astype(F32)
    mid = r1.astype(BF16)
    lo = (r1 - mid.astype(F32)).astype(BF16)
    return hi, mid, lo


def _exact_dot(t_bf16, x):
    hi, mid, lo = _split3(x)
    return _dot(t_bf16, hi) + _dot(t_bf16, mid) + _dot(t_bf16, lo)


def _exact_dot_right(x, t_bf16):
    hi, mid, lo = _split3(x)
    return _dot(hi, t_bf16) + _dot(mid, t_bf16) + _dot(lo, t_bf16)


def _sigmoid(z):
    return jax.nn.sigmoid(z)


def _silu_and_grad(z):
    s = _sigmoid(z)
    return z * s, s * (1.0 + z * (1.0 - s))


def _seg_select(j, values):
    out = values[0]
    for t, v in enumerate(values[1:], 1):
        out = jnp.where(j == t, v, out)
    return out


def _rms_fwd(x2, gain, token):
    S, D = x2.shape
    tm = min(512, S)

    def body(x_ref, g_ref, _, h_ref, r_ref):
        x = x_ref[...]
        r = lax.rsqrt(jnp.mean(x * x, axis=-1, keepdims=True) + NORM_EPS)
        h_ref[...] = ((x * r) * g_ref[...]).astype(BF16)
        r_ref[...] = r

    return pl.pallas_call(
        body, grid=(S // tm,), name="rms_fwd",
        in_specs=[pl.BlockSpec((tm, D), lambda i: (i, 0)), pl.BlockSpec((1, D), lambda i: (0, 0)),
                  pl.BlockSpec(token.shape, lambda i: (0, 0))],
        out_specs=[pl.BlockSpec((tm, D), lambda i: (i, 0)), pl.BlockSpec((tm, 1), lambda i: (i, 0))],
        out_shape=[jax.ShapeDtypeStruct((S, D), BF16), jax.ShapeDtypeStruct((S, 1), F32)],
        compiler_params=_cp(("parallel",)),
    )(x2, gain, token)


def _in_proj(h, w_all, where, segs, z_prev, token, name):
    S, D = h.shape
    SEG = w_all.shape[2] // 2
    NLB = SEG // LANES
    tm = min(1024, S)
    count = len(segs)
    DI = DEINTERLEAVE
    tu = tm // DI

    def is_qkv(seg):
        return (seg >= SEG_QKV) & (seg < SEG_QKV + 3)

    def seg_of(j, w):
        halves = {0: 0, 1: 1, "mine": w[0], "sibling": 1 - w[0]}
        cands = [2 * jnp.bitwise_xor(w[1], rel) + halves[half] for rel, half in segs]
        keys = [is_qkv(s).astype(jnp.int32) for s in cands]
        out = cands[0]
        for k in range(count):
            pos = (sum(jnp.where(keys[t] < keys[k], 1, 0) for t in range(count))
                   + sum(jnp.where(keys[t] == keys[k], 1, 0) for t in range(k)))
            out = jnp.where(pos == j, cands[k], out)
        return out

    def body(*refs):
        where_ref, h_ref, w_ref = refs[:3]
        o_ref, o16_ref = refs[-2:]
        res = _dot(h_ref[...], w_ref[...])
        for p in range(NLB):
            o_ref[p] = res[:, p * LANES:(p + 1) * LANES]

        @pl.when(is_qkv(seg_of(pl.program_id(0), where_ref)))
        def _():
            for p in range(NLB):
                for r in range(DI):
                    o16_ref[p, r] = o_ref.at[p][pl.ds(r, tu, stride=DI), :]

    def z16_map(j, i, w):
        seg = seg_of(j, w)
        return (jnp.where(is_qkv(seg), seg - SEG_QKV, 3), 0, 0, jnp.where(is_qkv(seg), i, 0), 0)

    in_specs = [pl.BlockSpec((tm, D), lambda j, i, w: (i, 0)),
                pl.BlockSpec((None, D, SEG), lambda j, i, w: (seg_of(j, w) // 2, 0, seg_of(j, w) % 2)),
                pl.BlockSpec(token.shape, lambda j, i, w: (0, 0))]
    args = [where, h, w_all, token]
    aliases = {}
    if z_prev is not None:
        in_specs += [ANY, ANY]
        args += list(z_prev)
        aliases = {4: 0, 5: 1}
    grid_spec = pltpu.PrefetchScalarGridSpec(
        num_scalar_prefetch=1, grid=(count, S // tm), in_specs=in_specs,
        out_specs=[pl.BlockSpec((None, NLB, tm, LANES), lambda j, i, w: (seg_of(j, w), 0, i, 0)),
                   pl.BlockSpec((None, NLB, DI, tu, LANES), z16_map)])
    return pl.pallas_call(
        body, grid_spec=grid_spec, name=name,
        out_shape=[jax.ShapeDtypeStruct((8, NLB, S, LANES), F32),
                   jax.ShapeDtypeStruct((4, NLB, DI, S // DI, LANES), F32)],
        input_output_aliases=aliases, compiler_params=_cp(("parallel", "parallel")),
    )(*args)


def _out_proj_loss(yh, ya, w_out, x2, tgt, fgain):
    S, D = x2.shape
    SEG = yh.shape[1]
    tm = min(256, S)
    parts = 2

    def body(yh_ref, ya_ref, w_ref, x_ref, t_ref, fg_ref, dout_ref, doutb_ref, loss_ref, dfg_ref):
        i = pl.program_id(0)

        @pl.when(i == 0)
        def _():
            loss_ref[...] = jnp.zeros_like(loss_ref)
            dfg_ref[...] = jnp.zeros_like(dfg_ref)

        fg = fg_ref[...]
        loss = jnp.zeros((1, 1), F32)
        dfg = jnp.zeros((1, D), F32)
        for rows in [pl.ds(p * (tm // parts), tm // parts) for p in range(parts)]:
            out = (x_ref[rows, :] + _dot(yh_ref[rows, :], w_ref[pl.ds(0, SEG), :])
                   + _dot(ya_ref[rows, :], w_ref[pl.ds(SEG, SEG), :]))
            r = lax.rsqrt(jnp.mean(out * out, axis=-1, keepdims=True) + NORM_EPS)
            n = out * r
            err = n * fg - t_ref[rows, :]
            loss = loss + 0.5 * jnp.sum(jnp.mean(err * err, axis=-1, keepdims=True), axis=0, keepdims=True)
            dy = err * (1.0 / D)
            dfg = dfg + jnp.sum(dy * n, axis=0, keepdims=True)
            dn = dy * fg
            dout = r * (dn - n * jnp.mean(dn * n, axis=-1, keepdims=True))
            dout_ref[rows, :] = dout
            doutb_ref[rows, :] = dout.astype(BF16)
        loss_ref[...] += loss
        dfg_ref[...] += dfg

    row = lambda i: (i, 0)
    fix = lambda i: (0, 0)
    return pl.pallas_call(
        body, grid=(S // tm,), name="out_proj_loss",
        in_specs=[pl.BlockSpec((tm, SEG), row), pl.BlockSpec((tm, SEG), row), pl.BlockSpec((2 * SEG, D), fix),
                  pl.BlockSpec((tm, D), row), pl.BlockSpec((tm, D), row), pl.BlockSpec((1, D), fix)],
        out_specs=[pl.BlockSpec((tm, D), row), pl.BlockSpec((tm, D), row), pl.BlockSpec((1, 1), fix),
                   pl.BlockSpec((1, D), fix)],
        out_shape=[jax.ShapeDtypeStruct((S, D), F32), jax.ShapeDtypeStruct((S, D), BF16),
                   jax.ShapeDtypeStruct((1, 1), F32), jax.ShapeDtypeStruct((1, D), F32)],
        compiler_params=_cp(("arbitrary",)),
    )(yh, ya, w_out, x2, tgt, fgain)


def _dy_proj_gate(doutb, w_out, o, zf32):
    S, D = doutb.shape
    NP = o.shape[0]
    SEG = NP * LANES
    tm = min(512, S)

    def body(d_ref, w_ref, o_ref, zg_ref, dyh_ref, do_ref, dl_ref, dzg_ref):
        d = d_ref[...]
        dyh_ref[...] = _dot_nt(d, w_ref[pl.ds(0, SEG), :])
        dya = _dot_nt(d, w_ref[pl.ds(SEG, SEG), :])
        r = lax.broadcasted_iota(jnp.int32, (LANES, LANES), 0) // ATTN_HEAD
        c = lax.broadcasted_iota(jnp.int32, (LANES, LANES), 1) // ATTN_HEAD
        same_head = (r == c).astype(BF16)
        for p in range(NP):
            cols = slice(p * LANES, (p + 1) * LANES)
            sg, dsg = _silu_and_grad(zg_ref[p])
            dyv = dya[:, cols]
            ov = o_ref[p]
            do = dyv * sg
            do_ref[p] = do
            dzg_ref[:, cols] = (dyv * ov * dsg).astype(BF16)
            dl_ref[p] = _exact_dot_right(do * ov, same_head)

    blk = pl.BlockSpec((NP, tm, LANES), lambda i: (0, i, 0))
    return pl.pallas_call(
        body, grid=(S // tm,), name="dy_proj_gate",
        in_specs=[pl.BlockSpec((tm, D), lambda i: (i, 0)), pl.BlockSpec((2 * SEG, D), lambda i: (0, 0)), blk,
                  pl.BlockSpec((None, NP, tm, LANES), lambda i: (SEG_GATE_A, 0, i, 0))],
        out_specs=[pl.BlockSpec((tm, SEG), lambda i: (i, 0)), blk, blk,
                   pl.BlockSpec((None, tm, SEG), lambda i: (3, i, 0))],
        out_shape=[jax.ShapeDtypeStruct((S, SEG), F32), jax.ShapeDtypeStruct((NP, S, LANES), F32),
                   jax.ShapeDtypeStruct((NP, S, LANES), F32), jax.ShapeDtypeStruct((4, S, SEG), BF16)],
        compiler_params=_cp(("parallel",), 60),
    )(doutb, w_out, o, zf32)


def _grad_w_out(yh, ya, doutb):
    S, SEG = yh.shape
    D = doutb.shape[1]
    R = (2 * SEG) // 4
    nb_half = SEG // R
    tk = min(2048, S)

    def body(yh_ref, ya_ref, d_ref, o_ref):
        q = pl.program_id(0)
        k = pl.program_id(1)

        @pl.when(k == 0)
        def _():
            o_ref[...] = jnp.zeros_like(o_ref)

        @pl.when(q < nb_half)
        def _():
            o_ref[...] += _dot_tn(yh_ref[...], d_ref[...])

        @pl.when(q >= nb_half)
        def _():
            o_ref[...] += _dot_tn(ya_ref[...], d_ref[...])

    return pl.pallas_call(
        body, grid=(4, S // tk), name="grad_w_out",
        in_specs=[pl.BlockSpec((tk, R), lambda q, k: (k, jnp.minimum(q, nb_half - 1))),
                  pl.BlockSpec((tk, R), lambda q, k: (k, jnp.maximum(q - nb_half, 0))),
                  pl.BlockSpec((tk, D), lambda q, k: (k, 0))],
        out_specs=pl.BlockSpec((None, R, D), lambda q, k: (q, 0, 0)),
        out_shape=jax.ShapeDtypeStruct((4, R, D), F32),
        compiler_params=_cp(("parallel", "arbitrary")),
    )(yh, ya, doutb)


def _dz_sources(sources):
    counts = [s.shape[0] for s in sources]
    starts = [sum(counts[:k]) for k in range(len(counts))]
    assert sum(counts) == 8
    return counts, starts


def _row_part(S, part, tile):
    first = max(512, (S * 3 // 8) // 512 * 512)
    rows = first if part == 0 else S - first
    assert rows % tile == 0 and first % tile == 0
    return (0 if part == 0 else first // tile), rows // tile, rows


def _dh_proj(sources, w_all, token, part, name):
    S = sources[0].shape[1]
    D = w_all.shape[1]
    SEG = w_all.shape[2] // 2
    counts, starts = _dz_sources(sources)
    assert all(c % 2 == 0 for c in counts)
    ns = len(sources)
    tm = 1024 if all(_row_part(S, p, 1)[2] % 1024 == 0 for p in (0, 1)) else 512
    t0, nt, nrows = _row_part(S, part, tm)

    def body(*refs):
        src = refs[:ns]
        w_ref, _, o_ref = refs[ns:]
        j = pl.program_id(1)

        @pl.when(j == 0)
        def _():
            o_ref[...] = jnp.zeros_like(o_ref)

        for k in range(ns):
            @pl.when((2 * j >= starts[k]) & (2 * j < starts[k] + counts[k]))
            def _(k=k):
                o_ref[...] += (_dot_nt(src[k][0], w_ref[:, pl.ds(0, SEG)])
                               + _dot_nt(src[k][1], w_ref[:, pl.ds(SEG, SEG)]))

    def src_spec(k):
        return pl.BlockSpec((2, tm, SEG),
                            lambda i, j: (jnp.clip(j - starts[k] // 2, 0, counts[k] // 2 - 1), t0 + i, 0))

    return pl.pallas_call(
        body, grid=(nt, 4), name=name,
        in_specs=[src_spec(k) for k in range(ns)] + [pl.BlockSpec((None, D, 2 * SEG), lambda i, j: (j, 0, 0)),
                                                     pl.BlockSpec(token.shape, lambda i, j: (0, 0))],
        out_specs=pl.BlockSpec((tm, D), lambda i, j: (i, 0)),
        out_shape=jax.ShapeDtypeStruct((nrows, D), F32),
        compiler_params=_cp(("parallel", "arbitrary"), 48 if tm == 512 else 60),
    )(*sources, w_all, token)


def _rms_bwd(dh, x2, rinv, gain, dout, part, gx_prev, name):
    S, D = x2.shape
    tm = 512
    t0, nt, _ = _row_part(S, part, tm)

    def body(dh_ref, x_ref, r_ref, g_ref, dout_ref, *rest):
        gx_ref, dg_ref = rest[-2:]

        @pl.when(pl.program_id(0) == 0)
        def _():
            dg_ref[...] = jnp.zeros_like(dg_ref)

        dh = dh_ref[...]
        r = r_ref[...]
        xhat = x_ref[...] * r
        dg_ref[...] += jnp.sum(dh * xhat, axis=0, keepdims=True)
        dxn = dh * g_ref[...]
        gx_ref[...] = dout_ref[...] + r * (dxn - xhat * jnp.mean(dxn * xhat, axis=-1, keepdims=True))

    row = lambda i: (t0 + i, 0)
    fix = lambda i: (0, 0)
    in_specs = [pl.BlockSpec((tm, D), lambda i: (i, 0)), pl.BlockSpec((tm, D), row), pl.BlockSpec((tm, 1), row),
                pl.BlockSpec((1, D), fix), pl.BlockSpec((tm, D), row)]
    args = [dh, x2, rinv, gain, dout]
    aliases = {}
    if gx_prev is not None:
        in_specs.append(ANY)
        args.append(gx_prev)
        aliases = {5: 0}
    return pl.pallas_call(
        body, grid=(nt,), name=name, in_specs=in_specs,
        out_specs=[pl.BlockSpec((tm, D), row), pl.BlockSpec((1, D), fix)],
        out_shape=[jax.ShapeDtypeStruct((S, D), F32), jax.ShapeDtypeStruct((1, D), F32)],
        input_output_aliases=aliases, compiler_params=_cp(("arbitrary",), 60),
    )(*args)


def _grad_w_in(h, sources):
    S, D = h.shape
    SEG = sources[0].shape[2]
    counts, starts = _dz_sources(sources)
    ns = len(sources)
    tk = min(2048, S)

    def body(*refs):
        h_ref = refs[0]
        src = refs[1:1 + ns]
        o_ref = refs[1 + ns]
        j = pl.program_id(0)
        k = pl.program_id(1)

        @pl.when(k == 0)
        def _():
            o_ref[...] = jnp.zeros_like(o_ref)

        for s in range(ns):
            @pl.when((j >= starts[s]) & (j < starts[s] + counts[s]))
            def _(s=s):
                o_ref[...] += _dot_tn(h_ref[...], src[s][...])

    def src_spec(s):
        return pl.BlockSpec((None, tk, SEG),
                            lambda j, k: (jnp.clip(j - starts[s], 0, counts[s] - 1), k, 0))

    return pl.pallas_call(
        body, grid=(8, S // tk), name="grad_w_in",
        in_specs=[pl.BlockSpec((tk, D), lambda j, k: (k, 0))] + [src_spec(s) for s in range(ns)],
        out_specs=pl.BlockSpec((None, D, SEG), lambda j, k: (j // 2, 0, j % 2)),
        out_shape=jax.ShapeDtypeStruct((4, D, 2 * SEG), F32),
        compiler_params=_cp(("parallel", "arbitrary"), 48 if tk <= 1024 else 62),
    )(h, *sources)


def _lower_bound(lbl):
    l0 = lbl[0:1, :]
    l1 = lbl[1:2, :]
    m = jnp.maximum(l0, l1)
    e0 = jnp.exp(l0 - m)
    e1 = jnp.exp(l1 - m)
    return e0 / (e0 + e1)


def _tile_masks():
    row = lax.broadcasted_iota(jnp.int32, (HGRN_TILE, HGRN_TILE), 0)
    col = lax.broadcasted_iota(jnp.int32, (HGRN_TILE, HGRN_TILE), 1)
    same = (row // HGRN_CHUNK) == (col // HGRN_CHUNK)
    return same & (row >= col), same & (row <= col)


def _chunk_last(b):
    T = b.shape[0]
    b3 = b.reshape(T // HGRN_CHUNK, HGRN_CHUNK, HGRN_HEAD)
    return jnp.broadcast_to(b3[:, HGRN_CHUNK - 1:HGRN_CHUNK, :], b3.shape).reshape(T, HGRN_HEAD)


def _chunk_sum(x):
    T = x.shape[0]
    x3 = x.reshape(T // HGRN_CHUNK, HGRN_CHUNK, HGRN_HEAD)
    return jnp.broadcast_to(jnp.sum(x3, axis=1, keepdims=True), x3.shape).reshape(T, HGRN_HEAD)


def _hgrn_dims(S, SEG, rows):
    T = min(rows, S)
    assert S % T == 0 and T % HGRN_TILE == 0
    tiles = [slice(t * HGRN_TILE, (t + 1) * HGRN_TILE) for t in range(T // HGRN_TILE)]
    chunks = [slice(c * HGRN_CHUNK, (c + 1) * HGRN_CHUNK) for c in range(T // HGRN_CHUNK)]
    return SEG // HGRN_HEAD, T, T // HGRN_CHUNK, S // T, tiles, chunks


def _hgrn_fwd(zf32, lb_logits, gnorm):
    _, NLB, S, _ = zf32.shape
    SEG = NLB * LANES
    H, T, NC, NJ, tiles, chunks = _hgrn_dims(S, SEG, HGRN_STEP_FWD[0])
    HP = min(HGRN_STEP_FWD[1], H)
    assert H % HP == 0

    def body(zq_ref, zf_ref, zi_ref, zg_ref, lbl_ref, gn_ref, y_ref, st_ref, state):
        @pl.when(pl.program_id(1) == 0)
        def _():
            state[...] = jnp.zeros_like(state)

        tril, _ = _tile_masks()
        tril_bf = tril.astype(BF16)
        for hh in range(HP):
            cols = slice(hh * HGRN_HEAD, (hh + 1) * HGRN_HEAD)
            lb = _lower_bound(lbl_ref[:, cols])
            zq = zq_ref[hh]
            q = zq * _sigmoid(zq)
            f = lb + (1.0 - lb) * _sigmoid(zf_ref[hh])
            k = 1.0 - f
            logf = jnp.log(f)
            b = jnp.concatenate([_exact_dot(tril_bf, logf[t]) for t in tiles], axis=0)
            bl = _chunk_last(b)
            qd_b = (q * jnp.exp(b)).astype(BF16)
            kd_b = (k * jnp.exp(-b)).astype(BF16)
            ke_b = (k * jnp.exp(bl - b)).astype(BF16)
            v_b = zi_ref[hh].astype(BF16)
            o_intra = jnp.concatenate(
                [_dot(jnp.where(tril, _dot_nt(qd_b[t], kd_b[t]), 0.0).astype(BF16), v_b[t]) for t in tiles], axis=0)
            kvs = [_dot_tn(v_b[r], ke_b[r]) for r in chunks]
            ebl = jnp.exp(bl)
            st = state[hh]
            sts = []
            for c in range(NC):
                st_ref[c, hh] = st
                sts.append(st.astype(BF16))
                st = st * ebl[c * HGRN_CHUNK:c * HGRN_CHUNK + 1, :] + kvs[c]
            state[hh] = st
            o = o_intra + jnp.concatenate([_dot_nt(qd_b[r], sb) for r, sb in zip(chunks, sts)], axis=0)
            on = o * lax.rsqrt(jnp.mean(o * o, axis=-1, keepdims=True) + NORM_EPS) * gn_ref[...]
            zg = zg_ref[hh]
            y_ref[:, cols] = (on * (zg * _sigmoid(zg))).astype(BF16)

    def zspec(seg):
        return pl.BlockSpec((None, HP, T, HGRN_HEAD), lambda h, j: (seg, h, j, 0))

    return pl.pallas_call(
        body, grid=(H // HP, NJ), name="hgrn_fwd",
        in_specs=[zspec(0), zspec(1), zspec(2), zspec(3),
                  pl.BlockSpec((2, HP * HGRN_HEAD), lambda h, j: (0, h)),
                  pl.BlockSpec((1, HGRN_HEAD), lambda h, j: (0, 0))],
        out_specs=[pl.BlockSpec((T, HP * HGRN_HEAD), lambda h, j: (j, h)),
                   pl.BlockSpec((NC, HP, HGRN_HEAD, HGRN_HEAD), lambda h, j: (j, h, 0, 0))],
        out_shape=[jax.ShapeDtypeStruct((S, SEG), BF16),
                   jax.ShapeDtypeStruct((S // HGRN_CHUNK, H, HGRN_HEAD, HGRN_HEAD), F32)],
        scratch_shapes=[pltpu.VMEM((HP, HGRN_HEAD, HGRN_HEAD), F32)],
        compiler_params=_cp(("parallel", "arbitrary")),
    )(zf32, zf32, zf32, zf32, lb_logits, gnorm)


def _hgrn_bwd(zf32, lb_logits, gnorm, states, dy):
    _, NLB, S, _ = zf32.shape
    SEG = NLB * LANES
    H, T, NC, NJ, tiles, chunks = _hgrn_dims(S, SEG, HGRN_STEP_BWD[0])
    C = HGRN_CHUNK
    HP = min(HGRN_STEP_BWD[1], H)
    assert H % HP == 0

    def body(zq_ref, zf_ref, zi_ref, zg_ref, lbl_ref, gn_ref, st_ref, dy_ref, dz_ref, dl_ref, dgn_ref, gstate):
        @pl.when(pl.program_id(1) == 0)
        def _():
            gstate[...] = jnp.zeros_like(gstate)
            dl_ref[...] = jnp.zeros_like(dl_ref)
            dgn_ref[...] = jnp.zeros_like(dgn_ref)

        gn = gn_ref[...]
        tril, triu = _tile_masks()
        tril_bf = tril.astype(BF16)
        triu_bf = triu.astype(BF16)
        for hh in range(HP):
            cols = slice(hh * HGRN_HEAD, (hh + 1) * HGRN_HEAD)
            lb = _lower_bound(lbl_ref[:, cols])
            q, dq_dz = _silu_and_grad(zq_ref[hh])
            sf = _sigmoid(zf_ref[hh])
            f = lb + (1.0 - lb) * sf
            k = 1.0 - f
            logf = jnp.log(f)
            b = jnp.concatenate([_exact_dot(tril_bf, logf[t]) for t in tiles], axis=0)
            bl = _chunk_last(b)
            eb = jnp.exp(b)
            enb = jnp.exp(-b)
            ekl = jnp.exp(bl - b)
            ebl = jnp.exp(bl)
            qd = q * eb
            kd = k * enb
            ke = k * ekl
            qd_b = qd.astype(BF16)
            kd_b = kd.astype(BF16)
            ke_b = ke.astype(BF16)
            v_b = zi_ref[hh].astype(BF16)
            sts = [st_ref[c, hh] for c in range(NC)]
            sts_b = [s.astype(BF16) for s in sts]
            a_b = [jnp.where(tril, _dot_nt(qd_b[t], kd_b[t]), 0.0).astype(BF16) for t in tiles]
            o = (jnp.concatenate([_dot(a, v_b[t]) for a, t in zip(a_b, tiles)], axis=0)
                 + jnp.concatenate([_dot_nt(qd_b[r], sb) for r, sb in zip(chunks, sts_b)], axis=0))
            rinv = lax.rsqrt(jnp.mean(o * o, axis=-1, keepdims=True) + NORM_EPS)
            ohat = o * rinv
            sg, dsg = _silu_and_grad(zg_ref[hh])
            dyv = dy_ref[:, cols]
            don = dyv * sg
            dz_ref[3, :, cols] = (dyv * (ohat * gn) * dsg).astype(BF16)
            dgn_ref[hh] += jnp.sum(don * ohat, axis=0, keepdims=True)
            dohat = don * gn
            do = rinv * (dohat - ohat * jnp.mean(dohat * ohat, axis=-1, keepdims=True))
            do_b = do.astype(BF16)
            da_b = [jnp.where(tril, _dot_nt(do_b[t], v_b[t]), 0.0).astype(BF16) for t in tiles]
            dv_intra = jnp.concatenate([_dot_tn(a, do_b[t]) for a, t in zip(a_b, tiles)], axis=0)
            dqd_intra = jnp.concatenate([_dot(da, kd_b[t]) for da, t in zip(da_b, tiles)], axis=0)
            dkd = jnp.concatenate([_dot_tn(da, qd_b[t]) for da, t in zip(da_b, tiles)], axis=0)
            dqd_inter = jnp.concatenate([_dot(do_b[r], sb) for r, sb in zip(chunks, sts_b)], axis=0)
            gks = [_dot_tn(do_b[r], qd_b[r]) for r in chunks]
            g = gstate[hh]
            gs = [None] * NC
            for c in reversed(range(NC)):
                gs[c] = g
                g = g * ebl[c * C:c * C + 1, :] + gks[c]
            gstate[hh] = g
            gs_b = [x.astype(BF16) for x in gs]
            dv = dv_intra + jnp.concatenate([_dot_nt(ke_b[r], gb) for r, gb in zip(chunks, gs_b)], axis=0)
            dz_ref[2, :, cols] = dv.astype(BF16)
            dke = jnp.concatenate([_dot(v_b[r], gb) for r, gb in zip(chunks, gs_b)], axis=0)
            debl = jnp.concatenate(
                [jnp.broadcast_to(jnp.sum(x * s, axis=0, keepdims=True), (C, HGRN_HEAD)) for x, s in zip(gs, sts)], axis=0)
            dqd = dqd_intra + dqd_inter
            dz_ref[0, :, cols] = ((dqd * eb) * dq_dz).astype(BF16)
            t_ke = dke * ke
            db = dqd * qd - dkd * kd - t_ke
            db_last = _chunk_sum(t_ke) + debl * ebl
            dk = dkd * enb + dke * ekl
            dlogf = jnp.concatenate([_exact_dot(triu_bf, db[t]) for t in tiles], axis=0) + db_last
            df = dlogf / f - dk
            dz_ref[1, :, cols] = (df * (1.0 - lb) * (sf * (1.0 - sf))).astype(BF16)
            dlb = jnp.sum(df * (1.0 - sf), axis=0, keepdims=True)
            dl0 = dlb * lb * (1.0 - lb)
            dl_ref[0:1, cols] += dl0
            dl_ref[1:2, cols] -= dl0

    def zspec(seg):
        return pl.BlockSpec((None, HP, T, HGRN_HEAD), lambda h, j: (seg, h, NJ - 1 - j, 0))

    return pl.pallas_call(
        body, grid=(H // HP, NJ), name="hgrn_bwd",
        in_specs=[zspec(0), zspec(1), zspec(2), zspec(3),
                  pl.BlockSpec((2, HP * HGRN_HEAD), lambda h, j: (0, h)),
                  pl.BlockSpec((1, HGRN_HEAD), lambda h, j: (0, 0)),
                  pl.BlockSpec((NC, HP, HGRN_HEAD, HGRN_HEAD), lambda h, j: (NJ - 1 - j, h, 0, 0)),
                  pl.BlockSpec((T, HP * HGRN_HEAD), lambda h, j: (NJ - 1 - j, h))],
        out_specs=[pl.BlockSpec((4, T, HP * HGRN_HEAD), lambda h, j: (0, NJ - 1 - j, h)),
                   pl.BlockSpec((2, HP * HGRN_HEAD), lambda h, j: (0, h)),
                   pl.BlockSpec((HP, 1, HGRN_HEAD), lambda h, j: (h, 0, 0))],
        out_shape=[jax.ShapeDtypeStruct((4, S, SEG), BF16), jax.ShapeDtypeStruct((2, SEG), F32),
                   jax.ShapeDtypeStruct((H, 1, HGRN_HEAD), F32)],
        scratch_shapes=[pltpu.VMEM((HP, HGRN_HEAD, HGRN_HEAD), F32)],
        compiler_params=_cp(("parallel", "arbitrary")),
    )(zf32, zf32, zf32, zf32, lb_logits, gnorm, states, dy)


def _alibi_slopes(seg):
    n_heads = seg // ATTN_HEAD
    s = 2.0 ** (-8.0 * np.arange(1, n_heads + 1, dtype=np.float64) / n_heads)
    return jnp.asarray(np.repeat(s, ATTN_HEAD)[None, :], F32)


def _attn_dims(S, SEG, d, block_elems=ATTN_BLOCK_ELEMS):
    rb = BAND * d
    assert S % rb == 0 and SEG % LANES == 0
    npb = max(1, min(SEG // LANES, block_elems // (rb * LANES)))
    assert (SEG // LANES) % npb == 0
    return rb, npb, S // rb, (SEG // LANES) // npb


def _res_rows(r, d):
    return pl.ds(0, BAND) if d == 1 else pl.ds(r, BAND, stride=d)


def _for_residues(d, fn):
    if d == 1:
        fn(0)
    else:
        def step(r, carry):
            fn(r)
            return carry
        lax.fori_loop(0, d, step, 0, unroll=ATTN_UNROLL)


def _for_groups(d, n_pairs, fn):
    def over_pairs(r):
        for g0 in range(0, n_pairs, ATTN_UNROLL):
            fn([(r, p) for p in range(g0, min(n_pairs, g0 + ATTN_UNROLL))])

    if d == 1:
        over_pairs(0)
    elif n_pairs >= ATTN_UNROLL:
        def step(r, carry):
            over_pairs(r)
            return carry
        lax.fori_loop(0, d, step, 0)
    else:
        per_group = ATTN_UNROLL // n_pairs
        assert d % per_group == 0

        def step(g, carry):
            fn([(g * per_group + i, p) for i in range(per_group) for p in range(n_pairs)])
            return carry
        lax.fori_loop(0, d // per_group, step, 0)


def _band_terms(n, d):
    i = lax.broadcasted_iota(jnp.int32, (BAND, 2 * BAND), 0)
    jj = lax.broadcasted_iota(jnp.int32, (BAND, 2 * BAND), 1)
    delta = BAND + i - jj
    valid = (delta >= 0) & (delta <= BAND) & ((n > 0) | (jj >= BAND))
    return (-d * delta).astype(F32), valid


def _head_biases(slopes, nd, valid):
    out = []
    for s in _per_head(slopes):
        s2 = jnp.concatenate([s, s], axis=1)
        out.append(jnp.where(valid, s2 * nd, NEG))
    return jnp.concatenate(out, axis=0)


def _stack_heads(x):
    lane = lax.broadcasted_iota(jnp.int32, x.shape, 1)
    zero = jnp.zeros_like(x)
    return jnp.concatenate([jnp.where(lane < ATTN_HEAD, x, zero), jnp.where(lane < ATTN_HEAD, zero, x)], axis=0)


def _unstack_heads(x2):
    first = lax.broadcasted_iota(jnp.int32, (BAND, LANES), 1) < ATTN_HEAD
    return jnp.where(first, x2[:BAND], x2[BAND:])


def _stack_per_head(x):
    a, b = _per_head(x)
    col = jnp.concatenate([a, b], axis=0)
    return jnp.concatenate([col, col], axis=1)


def _per_head(x):
    lane = lax.broadcasted_iota(jnp.int32, x.shape, 1)
    sw = pltpu.roll(x, ATTN_HEAD, 1)
    first = lane < ATTN_HEAD
    return jnp.where(first, x, sw), jnp.where(first, sw, x)


def _qkv_source(zz, d):
    z, z16 = zz
    if d == DEINTERLEAVE:
        def take(ref, p, r):
            return ref.at[p][r]

        def spec(seg, np_, row_block):
            return pl.BlockSpec((None, np_, d, BAND, LANES), lambda c, n: (seg, c, 0, row_block(c, n), 0))
        return z, z16, take, spec

    def take(ref, p, r):
        return ref.at[p][_res_rows(r, d), :]

    def spec(seg, np_, row_block):
        return pl.BlockSpec((None, np_, BAND * d, LANES), lambda c, n: (SEG_QKV + seg, c, row_block(c, n), 0))
    return z, z, take, spec


def _attn_fwd(qkv, slopes, d):
    qkv, src, take, spec = _qkv_source(qkv, d)
    _, NLB, S, _ = qkv.shape
    rb, NP, nb, ncb = _attn_dims(S, NLB * LANES, d, 2 * ATTN_BLOCK_ELEMS)

    def body(q_ref, kc_ref, vc_ref, sl_ref, o_ref, l_ref, kp_ref, vp_ref):
        n = pl.program_id(1)

        @pl.when(n == 0)
        def _():
            kp_ref[...] = jnp.zeros_like(kp_ref)
            vp_ref[...] = jnp.zeros_like(vp_ref)

        nd, valid = _band_terms(n, d)
        biases = [_head_biases(sl_ref[:, p * LANES:(p + 1) * LANES], nd, valid) for p in range(NP)]

        def group(items):
            scores, values = [], []
            for r, p in items:
                kc = jnp.concatenate([take(kp_ref, p, r), take(kc_ref, p, r)], axis=0).astype(BF16)
                values.append(jnp.concatenate([take(vp_ref, p, r), take(vc_ref, p, r)], axis=0).astype(BF16))
                scores.append(_dot_nt(_stack_heads((take(q_ref, p, r) * ATTN_SCALE).astype(BF16)), kc))
            probs = []
            for (r, p), s in zip(items, scores):
                s = s + biases[p]
                m = jnp.max(s, axis=-1, keepdims=True)
                e = jnp.exp(s - m)
                den = jnp.sum(e, axis=-1, keepdims=True)
                probs.append((e.astype(BF16), den, m + jnp.log(den)))
            for (r, p), vc, (e, den, lse) in zip(items, values, probs):
                rows = _res_rows(r, d)
                o_ref.at[p][rows, :] = _unstack_heads(_dot(e, vc) / den)
                l_ref.at[p][rows, :] = _unstack_heads(jnp.broadcast_to(lse, (2 * BAND, LANES)))

        _for_groups(d, NP, group)
        kp_ref[...] = kc_ref[...]
        vp_ref[...] = vc_ref[...]

    cur = lambda c, n: n
    out = pl.BlockSpec((NP, rb, LANES), lambda c, n: (c, n, 0))
    kv_block = spec(1, NP, cur).block_shape[1:]
    return pl.pallas_call(
        body, grid=(ncb, nb), name=f"attn_fwd_d{d}",
        in_specs=[spec(0, NP, cur), spec(1, NP, cur), spec(2, NP, cur),
                  pl.BlockSpec((1, NP * LANES), lambda c, n: (0, c))],
        out_specs=[out, out],
        out_shape=[jax.ShapeDtypeStruct((NLB, S, LANES), F32)] * 2,
        scratch_shapes=[pltpu.VMEM(kv_block, F32), pltpu.VMEM(kv_block, F32)],
        compiler_params=_cp(("parallel", "arbitrary")),
    )(src, src, src, slopes)


def _attn_merge(outs, lses, zf32):
    NLB, S, _ = outs[0].shape
    SEG = NLB * LANES
    tm = min(256, S)

    def body(o1, o2, o3, l1, l2, l3, zg_ref, o_ref, lse_ref, y_ref):
        a, b, c = l1[...], l2[...], l3[...]
        m = jnp.maximum(jnp.maximum(a, b), c)
        ea, eb, ec = jnp.exp(a - m), jnp.exp(b - m), jnp.exp(c - m)
        tot = ea + eb + ec
        o = (ea / tot) * o1[...] + (eb / tot) * o2[...] + (ec / tot) * o3[...]
        o_ref[...] = o
        lse_ref[...] = m + jnp.log(tot)
        zg = zg_ref[...]
        y = (o * (zg * _sigmoid(zg))).astype(BF16)
        for p in range(NLB):
            y_ref[:, p * LANES:(p + 1) * LANES] = y[p]

    blk = pl.BlockSpec((NLB, tm, LANES), lambda i: (0, i, 0))
    return pl.pallas_call(
        body, grid=(S // tm,), name="attn_merge",
        in_specs=[blk] * 6 + [pl.BlockSpec((None, NLB, tm, LANES), lambda i: (SEG_GATE_A, 0, i, 0))],
        out_specs=[blk, blk, pl.BlockSpec((tm, SEG), lambda i: (i, 0))],
        out_shape=[jax.ShapeDtypeStruct((NLB, S, LANES), F32), jax.ShapeDtypeStruct((NLB, S, LANES), F32),
                   jax.ShapeDtypeStruct((S, SEG), BF16)],
        compiler_params=_cp(("parallel",)),
    )(*outs, *lses, zf32)


def _attn_bwd(qkv, slopes, do, lse, dl, d, acc, into):
    qkv, src, take, spec = _qkv_source(qkv, d)
    _, NLB, S, _ = qkv.shape
    SEG = NLB * LANES
    rb, NP, nb, ncb = _attn_dims(S, SEG, d)
    has_acc = acc is not None
    out_dtype = F32 if into is None else into.dtype
    assert into is None or d == 1

    def body(*refs):
        q_ref, kc_ref, vc_ref, sl_ref, do_ref, lse_ref, dl_ref = refs[:7]
        acc_ref = refs[7] if has_acc else None
        out_ref, cq, ck, cv, kp_ref, vp_ref = refs[-6:]
        n = pl.program_id(1)

        def emit(r, p, dq, dk, dv):
            rows = _res_rows(r, d)
            for t, val in enumerate((dq, dk, dv)):
                if has_acc:
                    val = val + acc_ref.at[t].at[p][rows, :]
                if into is None:
                    out_ref.at[t].at[p][rows, :] = val.astype(out_dtype)
                else:
                    out_ref.at[t][rows, p * LANES:(p + 1) * LANES] = val.astype(out_dtype)

        @pl.when(n == 0)
        def _():
            cq[...] = jnp.zeros_like(cq)
            ck[...] = jnp.zeros_like(ck)
            cv[...] = jnp.zeros_like(cv)
            kp_ref[...] = jnp.zeros_like(kp_ref)
            vp_ref[...] = jnp.zeros_like(vp_ref)

        @pl.when(n < nb)
        def _():
            nd, valid = _band_terms(n, d)
            biases = [_head_biases(sl_ref[:, p * LANES:(p + 1) * LANES], nd, valid) for p in range(NP)]

            def group(items):
                first = []
                for r, p in items:
                    rows = _res_rows(r, d)
                    kc = jnp.concatenate([take(kp_ref, p, r), take(kc_ref, p, r)], axis=0).astype(BF16)
                    vc = jnp.concatenate([take(vp_ref, p, r), take(vc_ref, p, r)], axis=0).astype(BF16)
                    qs = _stack_heads((take(q_ref, p, r) * ATTN_SCALE).astype(BF16))
                    dos = _stack_heads(do_ref.at[p][rows, :].astype(BF16))
                    first.append((kc, qs, dos, _dot_nt(qs, kc), _dot_nt(dos, vc)))
                second = []
                for (r, p), (kc, qs, dos, s, dp) in zip(items, first):
                    rows = _res_rows(r, d)
                    pr = jnp.exp(s + biases[p] - _stack_per_head(lse_ref.at[p][rows, :]))
                    ds = (pr * (dp - _stack_per_head(dl_ref.at[p][rows, :]))).astype(BF16)
                    second.append((kc, qs, dos, pr.astype(BF16), ds))
                for (r, p), (kc, qs, dos, pr, ds) in zip(items, second):
                    dq = _unstack_heads(_dot(ds, kc)) * ATTN_SCALE
                    dk = _dot_tn(ds, qs)
                    dv = _dot_tn(pr, dos)
                    emit(r, p, cq[r, p], ck[r, p] + dk[:BAND, :], cv[r, p] + dv[:BAND, :])
                    cq[r, p] = dq
                    ck[r, p] = dk[BAND:, :]
                    cv[r, p] = dv[BAND:, :]

            _for_groups(d, NP, group)
            kp_ref[...] = kc_ref[...]
            vp_ref[...] = vc_ref[...]

        @pl.when(n == nb)
        def _():
            def last(r):
                for p in range(NP):
                    emit(r, p, cq[r, p], ck[r, p], cv[r, p])
            _for_residues(d, last)

    cur = lambda c, n: (c, jnp.minimum(n, nb - 1), 0)
    lag = lambda c, n: (0, c, jnp.clip(n - 1, 0, nb - 1), 0)

    at = lambda c, n: jnp.minimum(n, nb - 1)
    in_specs = [spec(0, NP, at), spec(1, NP, at), spec(2, NP, at),
                pl.BlockSpec((1, NP * LANES), lambda c, n: (0, c))] + [pl.BlockSpec((NP, rb, LANES), cur)] * 3
    kv_block = in_specs[1].block_shape[1:]
    args = [src, src, src, slopes, do, lse, dl]
    aliases = {}
    if has_acc:
        in_specs.append(pl.BlockSpec((3, NP, rb, LANES), lag))
        args.append(acc)
        if into is None:
            aliases = {7: 0}
    if into is None:
        out_sds = jax.ShapeDtypeStruct((3, NLB, S, LANES), F32)
        out_spec = pl.BlockSpec((3, NP, rb, LANES), lag)
    else:
        in_specs.append(ANY)
        args.append(into)
        aliases = {len(args) - 1: 0}
        out_sds = jax.ShapeDtypeStruct(into.shape, into.dtype)
        out_spec = pl.BlockSpec((3, rb, NP * LANES), lambda c, n: (0, jnp.clip(n - 1, 0, nb - 1), c))
    return pl.pallas_call(
        body, grid=(ncb, nb + 1), name=f"attn_bwd_d{d}",
        in_specs=in_specs, out_specs=out_spec, out_shape=out_sds,
        scratch_shapes=[pltpu.VMEM((d, NP, BAND, LANES), F32)] * 3 + [pltpu.VMEM(kv_block, F32)] * 2,
        input_output_aliases=aliases,
        compiler_params=_cp(("parallel", "arbitrary")),
    )(*args)


def _adamw(w, g, m, v, name):
    R, C = w.shape
    tr = R if R <= 256 else 256
    assert R % tr == 0

    def body(w_ref, g_ref, m_ref, v_ref, d_ref, nm_ref, nv_ref, go_ref):
        g = g_ref[...]
        nm = ADAM_B1 * m_ref[...] + (1.0 - ADAM_B1) * g
        nv = ADAM_B2 * v_ref[...] + (1.0 - ADAM_B2) * (g * g)
        m_hat = nm / (1.0 - ADAM_B1 ** ADAM_STEP)
        v_hat = nv / (1.0 - ADAM_B2 ** ADAM_STEP)
        d_ref[...] = -ADAM_LR * (m_hat / (jnp.sqrt(v_hat) + ADAM_EPS) + ADAM_WD * w_ref[...])
        nm_ref[...] = nm
        nv_ref[...] = nv
        go_ref[...] = g

    blk = pl.BlockSpec((tr, C), lambda i: (i, 0))
    sds = jax.ShapeDtypeStruct((R, C), F32)
    return pl.pallas_call(
        body, grid=(R // tr,), name=name, in_specs=[blk] * 4, out_specs=[blk] * 4, out_shape=[sds] * 4,
        compiler_params=_cp(("parallel",)),
    )(w, g, m, v)


def _coords():
    return lax.axis_index("x"), lax.axis_index("y"), lax.axis_index("c")


def _other_chips(x, y):
    return [(1 - x, y), (x, 1 - y), (1 - x, 1 - y)]


ANY = pl.BlockSpec(memory_space=pl.ANY)


def _cast_into_slot(w, where, name):
    R, C = w.shape
    tr = min(256, R)

    def body(where_ref, w_ref, o_ref):
        o_ref[...] = w_ref[...].astype(BF16)

    grid_spec = pltpu.PrefetchScalarGridSpec(
        num_scalar_prefetch=1, grid=(R // tr,),
        in_specs=[pl.BlockSpec((tr, C), lambda i, w: (i, 0))],
        out_specs=pl.BlockSpec((None, tr, C), lambda i, w: (w[1], i, 0)))
    return pl.pallas_call(
        body, grid_spec=grid_spec, name=name, out_shape=jax.ShapeDtypeStruct((4, R, C), BF16),
        compiler_params=_cp(("parallel",)),
    )(where, w)


def _pair_sum(g, sib, where, name):
    _, n2, C = g.shape
    N = n2 // 2
    tr = min(256, N)
    nt = N // tr

    def body(where_ref, g_ref, s_ref, qb_ref, own_ref):
        q = pl.program_id(1)
        tot = g_ref[...] + s_ref[...]
        qb_ref[...] = tot.astype(BF16)

        @pl.when(q == where_ref[1])
        def _():
            own_ref[...] = tot

    grid_spec = pltpu.PrefetchScalarGridSpec(
        num_scalar_prefetch=1, grid=(nt, 4),
        in_specs=[pl.BlockSpec((None, tr, C), lambda i, q, w: (q, w[0] * nt + i, 0)),
                  pl.BlockSpec((None, tr, C), lambda i, q, w: (q, i, 0))],
        out_specs=[pl.BlockSpec((None, tr, C), lambda i, q, w: (q, i, 0)),
                   pl.BlockSpec((tr, C), lambda i, q, w: (i, 0))])
    return pl.pallas_call(
        body, grid_spec=grid_spec, name=name,
        out_shape=[jax.ShapeDtypeStruct((4, N, C), BF16), jax.ShapeDtypeStruct((N, C), F32)],
        compiler_params=_cp(("parallel", "arbitrary")),
    )(where, g, sib)


HBM = pl.BlockSpec(memory_space=pltpu.HBM)
SEM = pl.BlockSpec(memory_space=pltpu.SEMAPHORE)


def _in_hbm(a):
    return pltpu.with_memory_space_constraint(a, pltpu.HBM)


def _split_start(name, copies, arrays, n_sems, after=None):
    n = len(arrays)

    def body(*refs):
        for cp in copies(refs[:n], refs[-n - 3], refs[-n - 2]):
            cp.start()
        refs[-1][...] = jnp.zeros_like(refs[-1])

    ordered = () if after is None else (after,)
    outs = pl.pallas_call(
        body, name=name,
        out_shape=(pltpu.SemaphoreType.DMA((n_sems,)), pltpu.SemaphoreType.DMA((n_sems,)),
                   *[pltpu.HBM(a.shape, a.dtype) for a in arrays], jax.ShapeDtypeStruct((8, LANES), F32)),
        in_specs=(HBM,) * n + (ANY,) * len(ordered),
        out_specs=(SEM, SEM) + (HBM,) * n + (pl.BlockSpec(memory_space=pltpu.VMEM),),
        input_output_aliases={i: 2 + i for i in range(n)},
        compiler_params=pltpu.CompilerParams(has_side_effects=pltpu.SideEffectType.DATAFLOW_SIDE_EFFECTING),
    )(*[_in_hbm(a) for a in arrays], *ordered)
    return outs[0], outs[1], list(outs[2:2 + n]), outs[-1]


def _split_wait(name, copies, send_sems, recv_sems, arrays, after):
    n = len(arrays)

    def body(*refs):
        for cp in copies(refs[:n], refs[n], refs[n + 1]):
            cp.wait_send()
            cp.wait_recv()

    outs = pl.pallas_call(
        body, name=name,
        out_shape=tuple(pltpu.HBM(a.shape, a.dtype) for a in arrays),
        in_specs=(HBM,) * n + (SEM, SEM, ANY), out_specs=(HBM,) * n,
        input_output_aliases={i: i for i in range(n)},
        compiler_params=pltpu.CompilerParams(has_side_effects=pltpu.SideEffectType.DATAFLOW_SIDE_EFFECTING),
    )(*arrays, send_sems, recv_sems, after)
    return list(outs)


def _remote(src, dst, sems, k, to):
    send_sems, recv_sems = sems
    return pltpu.make_async_remote_copy(src_ref=src, dst_ref=dst, send_sem=send_sems.at[k], recv_sem=recv_sems.at[k],
                                        device_id=to, device_id_type=MESH)


def _chip_at(x, y, rel):
    px = 1 - x if rel & 2 else x
    py = 1 - y if rel & 1 else y
    return px, py, 2 * px + py


def _gather_in_copies(rels):
    def copies(refs, send_sems, recv_sems):
        (w,) = refs
        x, y, c = _coords()
        seg = w.shape[2] // 2
        mine = w.at[2 * x + y, :, pl.ds(c * seg, seg)]
        return [_remote(mine, mine, (send_sems, recv_sems), k, _chip_at(x, y, rel)[:2] + (c,))
                for k, rel in enumerate(rels)]
    return copies


def _gather_out_copies(refs, send_sems, recv_sems):
    (w,) = refs
    x, y, c = _coords()
    mine = w.at[2 * x + y]
    return [_remote(mine, mine, (send_sems, recv_sems), k, (px, py, c)) for k, (px, py) in enumerate(_other_chips(x, y))]


def _swap_copies(refs, send_sems, recv_sems):
    gi, go, si, so = refs
    x, y, c = _coords()
    cps = []
    for a, (src, dst) in enumerate(((gi, si), (go, so))):
        nr = dst.shape[1]
        cps.append(_remote(src.at[:, pl.ds((1 - c) * nr, nr), :], dst, (send_sems, recv_sems), a, (x, y, 1 - c)))
    return cps


def _scatter_copies(refs, send_sems, recv_sems):
    qi, qo, ri, ro = refs
    x, y, c = _coords()
    cps = []
    for k, (px, py) in enumerate(_other_chips(x, y)):
        for a, (src, dst) in enumerate(((qi, ri), (qo, ro))):
            cps.append(_remote(src.at[2 * px + py], dst.at[k], (send_sems, recv_sems), 2 * k + a, (px, py, c)))
    return cps


def _forward_copies(rels):
    def copies(refs, send_sems, recv_sems):
        (w,) = refs
        x, y, c = _coords()
        seg = w.shape[2] // 2
        cps = []
        for k, rel in enumerate(rels):
            got = w.at[_chip_at(x, y, rel)[2], :, pl.ds(c * seg, seg)]
            cps.append(_remote(got, got, (send_sems, recv_sems), k, (x, y, 1 - c)))
        return cps
    return copies


def _chip_sum(own, got, where, name):
    N, C = own.shape
    tr = min(256, N)
    nt = N // tr

    def body(where_ref, own_ref, got_ref, o_ref):
        t = own_ref[...]
        for k in range(3):
            t = t + got_ref[k].astype(F32)
        o_ref[...] = t

    grid_spec = pltpu.PrefetchScalarGridSpec(
        num_scalar_prefetch=1, grid=(nt,),
        in_specs=[pl.BlockSpec((tr, C), lambda i, w: (i, 0)), pl.BlockSpec((3, tr, C), lambda i, w: (0, i, 0))],
        out_specs=pl.BlockSpec((tr, C), lambda i, w: (w[0] * nt + i, 0)))
    return pl.pallas_call(
        body, grid_spec=grid_spec, name=name, out_shape=jax.ShapeDtypeStruct((2 * N, C), F32),
        compiler_params=_cp(("parallel",)),
    )(where, own, got)


def _join_copies(refs, send_sems, recv_sems):
    x, y, c = _coords()
    cps = []
    for a, ref in enumerate(refs):
        nr = ref.shape[0] // 2
        mine = ref.at[pl.ds(c * nr, nr), :]
        cps.append(_remote(mine, mine, (send_sems, recv_sems), a, (x, y, 1 - c)))
    return cps


def _all_reduce_small(part, token):
    R, C = part.shape

    def body(p_ref, _, o_ref, slots, send_sems, recv_sems):
        x, y, c = _coords()
        me = 4 * x + 2 * y + c
        slots[me] = p_ref[...]
        cps = []
        for k in range(1, 8):
            fx, fy, fc = (k >> 2) & 1, (k >> 1) & 1, k & 1
            peer = (1 - x if fx else x, 1 - y if fy else y, 1 - c if fc else c)
            cp = pltpu.make_async_remote_copy(src_ref=p_ref, dst_ref=slots.at[me], send_sem=send_sems.at[k - 1],
                                              recv_sem=recv_sems.at[k - 1], device_id=peer, device_id_type=MESH)
            cp.start()
            cps.append(cp)
        for cp in cps:
            cp.wait()
        t = slots[0]
        for k in range(1, 8):
            t = t + slots[k]
        o_ref[...] = t

    vm = pl.BlockSpec(memory_space=pltpu.VMEM)
    return pl.pallas_call(
        body, name="all_reduce_small", in_specs=[vm, vm], out_specs=vm,
        out_shape=jax.ShapeDtypeStruct((R, C), F32),
        scratch_shapes=[pltpu.VMEM((8, R, C), F32), pltpu.SemaphoreType.DMA((7,)), pltpu.SemaphoreType.DMA((7,))],
    )(part, token)


def _mixers_forward(z, lb_logits, hgrn_gnorm):
    slopes = _alibi_slopes(z[0].shape[1] * LANES)
    yh, states = _hgrn_fwd(z[0], lb_logits, hgrn_gnorm)
    outs, lses = [], []
    for d in DILATIONS:
        o, l = _attn_fwd(z, slopes, d)
        outs.append(o)
        lses.append(l)
    o_attn, lse, ya = _attn_merge(outs, lses, z[0])
    return yh, ya, (states, o_attn, lse, slopes)


def _backward_to_dz(z, kept, lb_logits, hgrn_gnorm, yh, ya, w_out_all, x2, tgt, fgain, h):
    states, o_attn, lse, slopes = kept
    dout, doutb, loss, dfg = _out_proj_loss(yh, ya, w_out_all, x2, tgt, fgain)
    dy, do, dl, dza = _dy_proj_gate(doutb, w_out_all, o_attn, z[0])
    g_w_out = _grad_w_out(yh, ya, doutb)
    dzh, dlogits, dgn = _hgrn_bwd(z[0], lb_logits, hgrn_gnorm, states, dy)
    acc = None
    order = sorted(DILATIONS, reverse=True)
    for d in order[:-1]:
        acc = _attn_bwd(z, slopes, do, lse, dl, d, acc, None)
    dza = _attn_bwd(z, slopes, do, lse, dl, order[-1], acc, dza)
    sources = [dzh, dza]
    g_w_in = _grad_w_in(h, sources)
    return loss, dfg, dlogits, dgn, g_w_out, g_w_in, sources, dout


def _grad_x_half(sources, w_all, x2, rinv, norm_gain, dout, token, part, gx_prev):
    dh = _dh_proj(sources, w_all, token, part, f"dh_proj_{part}")
    return _rms_bwd(dh, x2, rinv, norm_gain, dout, part, gx_prev, f"rms_bwd_{part}")


def _local_step(x2, tgt, norm_gain, w_all, lb_logits, hgrn_gnorm, w_out_all, fgain):
    token = jnp.zeros((8, LANES), F32)
    where = jnp.zeros((2,), jnp.int32)
    h, rinv = _rms_fwd(x2, norm_gain, token)
    z = _in_proj(h, w_all, where, [(rel, half) for rel in range(4) for half in range(2)], None, token, "in_proj_all")
    yh, ya, kept = _mixers_forward(z, lb_logits, hgrn_gnorm)
    loss, dfg, dlogits, dgn, g_w_out, g_w_in, sources, dout = _backward_to_dz(
        z, kept, lb_logits, hgrn_gnorm, yh, ya, w_out_all, x2, tgt, fgain, h)
    gx, dg0 = _grad_x_half(sources, w_all, x2, rinv, norm_gain, dout, token, 0, None)
    gx, dg1 = _grad_x_half(sources, w_all, x2, rinv, norm_gain, dout, token, 1, gx)
    return loss, gx, dg0 + dg1, g_w_in, dlogits, dgn, g_w_out, dfg


def _pack_small(D, loss, dgain, dlogits, dgn, dfg):
    def row(v):
        v = v.reshape(1, -1)
        return jnp.pad(v, ((0, 0), (0, D - v.shape[1])))
    rows = [row(dgain), row(dfg), row(dlogits[0]), row(dlogits[1]), row(jnp.sum(dgn, axis=0)), row(loss)]
    rows += [jnp.zeros((1, D), F32)] * (8 - len(rows))
    return jnp.concatenate(rows, axis=0)


def kernel(x, norm_gain, w_in, lb_logits, hgrn_gnorm, w_out, final_gain, loss_target, m_norm_gain, m_w_in, m_lb_logits, m_hgrn_gnorm, m_w_out, m_final_gain, v_norm_gain, v_w_in, v_lb_logits, v_hgrn_gnorm, v_w_out, v_final_gain):
    _, S, D = x.shape
    SEG = w_in.shape[2] // 2
    x2 = x[0]
    tgt = loss_target[0]
    fgain = final_gain.reshape(1, D)
    where = jnp.stack([lax.axis_index("c"), 2 * lax.axis_index("x") + lax.axis_index("y")]).astype(jnp.int32)

    wia = _cast_into_slot(w_in[0], where, "cast_w_in")
    woa = _cast_into_slot(w_out[0], where, "cast_w_out")
    near, far = (2, 1), (3,)
    ga = _split_start("gather_near_start", _gather_in_copies(near), [wia], 2)
    h, rinv = _rms_fwd(x2, norm_gain, ga[3])
    z = _in_proj(h, ga[2][0], where, [(0, 0), (0, 1)], None, ga[3], "in_proj_own")
    (wia,) = _split_wait("gather_near_wait", _gather_in_copies(near), ga[0], ga[1], ga[2], z[0])
    gb = _split_start("gather_far_start", _gather_in_copies(far), [wia], 1)
    fa = _split_start("forward_near_start", _forward_copies(near), gb[2], 2, after=gb[3])
    z = _in_proj(h, fa[2][0], where, [(2, "mine"), (1, "mine")], z, fa[3], "in_proj_near")
    (wia,) = _split_wait("forward_near_wait", _forward_copies(near), fa[0], fa[1], fa[2], z[0])
    (wia,) = _split_wait("gather_far_wait", _gather_in_copies(far), gb[0], gb[1], [wia], z[0])
    out_sems = _split_start("gather_out_start", _gather_out_copies, [woa], 3, after=wia)
    fb = _split_start("forward_far_start", _forward_copies(far), [wia], 1, after=out_sems[3])
    z = _in_proj(h, fb[2][0], where, [(3, "mine"), (2, "sibling"), (1, "sibling")], z, fb[3], "in_proj_far")
    (wia,) = _split_wait("forward_far_wait", _forward_copies(far), fb[0], fb[1], fb[2], z[0])
    z = _in_proj(h, wia, where, [(3, "sibling")], z, fb[3], "in_proj_last")
    yh, ya, kept = _mixers_forward(z, lb_logits, hgrn_gnorm)
    (woa,) = _split_wait("gather_out_wait", _gather_out_copies, out_sems[0], out_sems[1], out_sems[2], ya)
    w_out_all = woa.reshape(2 * SEG, D)

    loss, dfg, dlogits, dgn, g_w_out, g_w_in, sources, dout = _backward_to_dz(
        z, kept, lb_logits, hgrn_gnorm, yh, ya, w_out_all, x2, tgt, fgain, h)

    sib_i = lax.empty((4, g_w_in.shape[1] // 2, g_w_in.shape[2]), F32)
    sib_o = lax.empty((4, g_w_out.shape[1] // 2, g_w_out.shape[2]), F32)
    sems = _split_start("swap_start", _swap_copies, [g_w_in, g_w_out, sib_i, sib_o], 2)
    grad_x, dg0 = _grad_x_half(sources, wia, x2, rinv, norm_gain, dout, sems[3], 0, None)
    g_w_in, g_w_out, sib_i, sib_o = _split_wait("swap_wait", _swap_copies, sems[0], sems[1], sems[2], grad_x)
    qi, own_i = _pair_sum(g_w_in, sib_i, where, "pair_sum_w_in")
    qo, own_o = _pair_sum(g_w_out, sib_o, where, "pair_sum_w_out")
    ri = lax.empty((3,) + qi.shape[1:], BF16)
    ro = lax.empty((3,) + qo.shape[1:], BF16)
    sems = _split_start("scatter_start", _scatter_copies, [qi, qo, ri, ro], 6)
    grad_x, dg1 = _grad_x_half(sources, wia, x2, rinv, norm_gain, dout, sems[3], 1, grad_x)
    _, _, got_i, got_o = _split_wait("scatter_wait", _scatter_copies, sems[0], sems[1], sems[2], grad_x)
    jn = _split_start("join_start", _join_copies, [_chip_sum(own_i, got_i, where, "chip_sum_w_in"),
                                                   _chip_sum(own_o, got_o, where, "chip_sum_w_out")], 2)
    small = _all_reduce_small(_pack_small(D, loss, dg0 + dg1, dlogits, dgn, dfg), jn[3])
    loss_sum = small[5, 0]
    d_ng, m_ng, v_ng, grad_norm_gain = _adamw(norm_gain, small[0:1, :], m_norm_gain, v_norm_gain, "adamw_norm_gain")
    d_lb, m_lb, v_lb, grad_lb_logits = _adamw(lb_logits, small[2:4, :SEG], m_lb_logits, v_lb_logits, "adamw_lb_logits")
    d_gn, m_gn, v_gn, grad_hgrn_gnorm = _adamw(hgrn_gnorm, small[4:5, :HGRN_HEAD], m_hgrn_gnorm, v_hgrn_gnorm,
                                               "adamw_hgrn_gnorm")
    d_fg, m_fg, v_fg, grad_final_gain = _adamw(fgain, small[1:2, :], m_final_gain.reshape(1, D),
                                               v_final_gain.reshape(1, D), "adamw_final_gain")
    g_w_in, g_w_out = _split_wait("join_wait", _join_copies, jn[0], jn[1], jn[2], d_fg)
    d_wi, m_wi, v_wi, grad_w_in = _adamw(w_in[0], g_w_in, m_w_in[0], v_w_in[0], "adamw_w_in")
    d_wo, m_wo, v_wo, grad_w_out = _adamw(w_out[0], g_w_out, m_w_out[0], v_w_out[0], "adamw_w_out")

    return (loss_sum, grad_x[None],
            grad_norm_gain, grad_w_in[None], grad_lb_logits, grad_hgrn_gnorm, grad_w_out[None], grad_final_gain[0],
            d_ng, d_wi[None], d_lb, d_gn, d_wo[None], d_fg[0],
            m_ng, m_wi[None], m_lb, m_gn, m_wo[None], m_fg[0],
            v_ng, v_wi[None], v_lb, v_gn, v_wo[None], v_fg[0])
```

```python
import jax
import jax.numpy as jnp
import numpy as np
from jax import lax
from jax.experimental import pallas as pl
from jax.experimental.pallas import tpu as pltpu

F32 = jnp.float32
BF16 = jnp.bfloat16
MESH = pl.DeviceIdType.MESH

NORM_EPS = 1e-6
HGRN_HEAD = 128
HGRN_CHUNK = 64
HGRN_TILE = 128
HGRN_STEP_FWD = (1024, 4)
HGRN_STEP_BWD = (2048, 1)
ATTN_HEAD = 64
LANES = 128
BAND = 128
DILATIONS = (1, 4, 16)
DEINTERLEAVE = 16
ATTN_SCALE = ATTN_HEAD ** -0.5
assert ATTN_SCALE == 0.125
ATTN_BLOCK_ELEMS = BAND * 2048
ATTN_UNROLL = 4
SEG_QKV = 4
SEG_GATE_A = 7
NEG = -1e30

ADAM_LR = 0.001
ADAM_B1 = 0.9
ADAM_B2 = 0.999
ADAM_EPS = 1e-08
ADAM_WD = 0.01
ADAM_STEP = 10

MIB = 1024 * 1024


def _cp(semantics=None, vmem_mib=48):
    return pltpu.CompilerParams(dimension_semantics=semantics, vmem_limit_bytes=vmem_mib * MIB)


def _dot(a, b):
    return jnp.dot(a, b, preferred_element_type=F32)


def _dot_nt(a, b):
    return lax.dot_general(a, b, (((1,), (1,)), ((), ())), preferred_element_type=F32)


def _dot_tn(a, b):
    return lax.dot_general(a, b, (((0,), (0,)), ((), ())), preferred_element_type=F32)


def _split3(x):
    hi = x.astype(BF16)
    r1 = x - hi.astype(F32)
    mid = r1.astype(BF16)
    lo = (r1 - mid.astype(F32)).astype(BF16)
    return hi, mid, lo


def _exact_dot(t_bf16, x):
    hi, mid, lo = _split3(x)
    return _dot(t_bf16, hi) + _dot(t_bf16, mid) + _dot(t_bf16, lo)


def _exact_dot_right(x, t_bf16):
    hi, mid, lo = _split3(x)
    return _dot(hi, t_bf16) + _dot(mid, t_bf16) + _dot(lo, t_bf16)


def _sigmoid(z):
    return jax.nn.sigmoid(z)


def _silu_and_grad(z):
    s = _sigmoid(z)
    return z * s, s * (1.0 + z * (1.0 - s))


def _seg_select(j, values):
    out = values[0]
    for t, v in enumerate(values[1:], 1):
        out = jnp.where(j == t, v, out)
    return out


def _rms_fwd(x2, gain, token):
    S, D = x2.shape
    tm = min(512, S)

    def body(x_ref, g_ref, _, h_ref, r_ref):
        x = x_ref[...]
        r = lax.rsqrt(jnp.mean(x * x, axis=-1, keepdims=True) + NORM_EPS)
        h_ref[...] = ((x * r) * g_ref[...]).astype(BF16)
        r_ref[...] = r

    return pl.pallas_call(
        body, grid=(S // tm,), name="rms_fwd",
        in_specs=[pl.BlockSpec((tm, D), lambda i: (i, 0)), pl.BlockSpec((1, D), lambda i: (0, 0)),
                  pl.BlockSpec(token.shape, lambda i: (0, 0))],
        out_specs=[pl.BlockSpec((tm, D), lambda i: (i, 0)), pl.BlockSpec((tm, 1), lambda i: (i, 0))],
        out_shape=[jax.ShapeDtypeStruct((S, D), BF16), jax.ShapeDtypeStruct((S, 1), F32)],
        compiler_params=_cp(("parallel",)),
    )(x2, gain, token)


def _in_proj(h, w_all, where, segs, z_prev, token, name):
    S, D = h.shape
    SEG = w_all.shape[2] // 2
    NLB = SEG // LANES
    tm = min(1024, S)
    count = len(segs)
    DI = DEINTERLEAVE
    tu = tm // DI

    def is_qkv(seg):
        return (seg >= SEG_QKV) & (seg < SEG_QKV + 3)

    def seg_of(j, w):
        halves = {0: 0, 1: 1, "mine": w[0], "sibling": 1 - w[0]}
        cands = [2 * jnp.bitwise_xor(w[1], rel) + halves[half] for rel, half in segs]
        keys = [is_qkv(s).astype(jnp.int32) for s in cands]
        out = cands[0]
        for k in range(count):
            pos = (sum(jnp.where(keys[t] < keys[k], 1, 0) for t in range(count))
                   + sum(jnp.where(keys[t] == keys[k], 1, 0) for t in range(k)))
            out = jnp.where(pos == j, cands[k], out)
        return out

    def body(*refs):
        where_ref, h_ref, w_ref = refs[:3]
        o_ref, o16_ref = refs[-2:]
        res = _dot(h_ref[...], w_ref[...])
        for p in range(NLB):
            o_ref[p] = res[:, p * LANES:(p + 1) * LANES]

        @pl.when(is_qkv(seg_of(pl.program_id(0), where_ref)))
        def _():
            for p in range(NLB):
                for r in range(DI):
                    o16_ref[p, r] = o_ref.at[p][pl.ds(r, tu, stride=DI), :]

    def z16_map(j, i, w):
        seg = seg_of(j, w)
        return (jnp.where(is_qkv(seg), seg - SEG_QKV, 3), 0, 0, jnp.where(is_qkv(seg), i, 0), 0)

    in_specs = [pl.BlockSpec((tm, D), lambda j, i, w: (i, 0)),
                pl.BlockSpec((None, D, SEG), lambda j, i, w: (seg_of(j, w) // 2, 0, seg_of(j, w) % 2)),
                pl.BlockSpec(token.shape, lambda j, i, w: (0, 0))]
    args = [where, h, w_all, token]
    aliases = {}
    if z_prev is not None:
        in_specs += [ANY, ANY]
        args += list(z_prev)
        aliases = {4: 0, 5: 1}
    grid_spec = pltpu.PrefetchScalarGridSpec(
        num_scalar_prefetch=1, grid=(count, S // tm), in_specs=in_specs,
        out_specs=[pl.BlockSpec((None, NLB, tm, LANES), lambda j, i, w: (seg_of(j, w), 0, i, 0)),
                   pl.BlockSpec((None, NLB, DI, tu, LANES), z16_map)])
    return pl.pallas_call(
        body, grid_spec=grid_spec, name=name,
        out_shape=[jax.ShapeDtypeStruct((8, NLB, S, LANES), F32),
                   jax.ShapeDtypeStruct((4, NLB, DI, S // DI, LANES), F32)],
        input_output_aliases=aliases, compiler_params=_cp(("parallel", "parallel")),
    )(*args)


def _out_proj_loss(yh, outs, lses, zf32, w_out, x2, tgt, fgain):
    S, D = x2.shape
    SEG = yh.shape[1]
    NLB = SEG // LANES
    tm = min(256, S)
    parts = 2

    def body(yh_ref, o1, o2, o3, l1, l2, l3, zg_ref, w_ref, x_ref, t_ref, fg_ref,
             dout_ref, doutb_ref, loss_ref, dfg_ref, oa_ref, lse_ref, ya_ref):
        i = pl.program_id(0)

        @pl.when(i == 0)
        def _():
            loss_ref[...] = jnp.zeros_like(loss_ref)
            dfg_ref[...] = jnp.zeros_like(dfg_ref)

        fg = fg_ref[...]
        loss = jnp.zeros((1, 1), F32)
        dfg = jnp.zeros((1, D), F32)
        for rows in [pl.ds(p * (tm // parts), tm // parts) for p in range(parts)]:
            a, b, c = l1[:, rows, :], l2[:, rows, :], l3[:, rows, :]
            m = jnp.maximum(jnp.maximum(a, b), c)
            ea, eb, ec = jnp.exp(a - m), jnp.exp(b - m), jnp.exp(c - m)
            tot = ea + eb + ec
            o = (ea / tot) * o1[:, rows, :] + (eb / tot) * o2[:, rows, :] + (ec / tot) * o3[:, rows, :]
            oa_ref[:, rows, :] = o
            lse_ref[:, rows, :] = m + jnp.log(tot)
            zg = zg_ref[:, rows, :]
            y3 = (o * (zg * _sigmoid(zg))).astype(BF16)
            ya = jnp.concatenate([y3[p] for p in range(NLB)], axis=1)
            ya_ref[rows, :] = ya
            out = (x_ref[rows, :] + _dot(yh_ref[rows, :], w_ref[pl.ds(0, SEG), :])
                   + _dot(ya, w_ref[pl.ds(SEG, SEG), :]))
            r = lax.rsqrt(jnp.mean(out * out, axis=-1, keepdims=True) + NORM_EPS)
            n = out * r
            err = n * fg - t_ref[rows, :]
            loss = loss + 0.5 * jnp.sum(jnp.mean(err * err, axis=-1, keepdims=True), axis=0, keepdims=True)
            dy = err * (1.0 / D)
            dfg = dfg + jnp.sum(dy * n, axis=0, keepdims=True)
            dn = dy * fg
            dout = r * (dn - n * jnp.mean(dn * n, axis=-1, keepdims=True))
            dout_ref[rows, :] = dout
            doutb_ref[rows, :] = dout.astype(BF16)
        loss_ref[...] += loss
        dfg_ref[...] += dfg

    row = lambda i: (i, 0)
    fix = lambda i: (0, 0)
    blk = pl.BlockSpec((NLB, tm, LANES), lambda i: (0, i, 0))
    return pl.pallas_call(
        body, grid=(S // tm,), name="out_proj_loss",
        in_specs=[pl.BlockSpec((tm, SEG), row)] + [blk] * 6
        + [pl.BlockSpec((None, NLB, tm, LANES), lambda i: (SEG_GATE_A, 0, i, 0)), pl.BlockSpec((2 * SEG, D), fix),
           pl.BlockSpec((tm, D), row), pl.BlockSpec((tm, D), row), pl.BlockSpec((1, D), fix)],
        out_specs=[pl.BlockSpec((tm, D), row), pl.BlockSpec((tm, D), row), pl.BlockSpec((1, 1), fix),
                   pl.BlockSpec((1, D), fix), blk, blk, pl.BlockSpec((tm, SEG), row)],
        out_shape=[jax.ShapeDtypeStruct((S, D), F32), jax.ShapeDtypeStruct((S, D), BF16),
                   jax.ShapeDtypeStruct((1, 1), F32), jax.ShapeDtypeStruct((1, D), F32),
                   jax.ShapeDtypeStruct((NLB, S, LANES), F32), jax.ShapeDtypeStruct((NLB, S, LANES), F32),
                   jax.ShapeDtypeStruct((S, SEG), BF16)],
        compiler_params=_cp(("arbitrary",), 60),
    )(yh, *outs, *lses, zf32, w_out, x2, tgt, fgain)


def _dy_proj_gate(doutb, w_out, o, zf32):
    S, D = doutb.shape
    NP = o.shape[0]
    SEG = NP * LANES
    tm = min(512, S)

    def body(d_ref, w_ref, o_ref, zg_ref, dyh_ref, do_ref, dl_ref, dzg_ref):
        d = d_ref[...]
        dyh_ref[...] = _dot_nt(d, w_ref[pl.ds(0, SEG), :])
        dya = _dot_nt(d, w_ref[pl.ds(SEG, SEG), :])
        r = lax.broadcasted_iota(jnp.int32, (LANES, LANES), 0) // ATTN_HEAD
        c = lax.broadcasted_iota(jnp.int32, (LANES, LANES), 1) // ATTN_HEAD
        same_head = (r == c).astype(BF16)
        for p in range(NP):
            cols = slice(p * LANES, (p + 1) * LANES)
            sg, dsg = _silu_and_grad(zg_ref[p])
            dyv = dya[:, cols]
            ov = o_ref[p]
            do = dyv * sg
            do_ref[p] = do
            dzg_ref[:, cols] = (dyv * ov * dsg).astype(BF16)
            dl_ref[p] = _exact_dot_right(do * ov, same_head)

    blk = pl.BlockSpec((NP, tm, LANES), lambda i: (0, i, 0))
    return pl.pallas_call(
        body, grid=(S // tm,), name="dy_proj_gate",
        in_specs=[pl.BlockSpec((tm, D), lambda i: (i, 0)), pl.BlockSpec((2 * SEG, D), lambda i: (0, 0)), blk,
                  pl.BlockSpec((None, NP, tm, LANES), lambda i: (SEG_GATE_A, 0, i, 0))],
        out_specs=[pl.BlockSpec((tm, SEG), lambda i: (i, 0)), blk, blk,
                   pl.BlockSpec((None, tm, SEG), lambda i: (3, i, 0))],
        out_shape=[jax.ShapeDtypeStruct((S, SEG), F32), jax.ShapeDtypeStruct((NP, S, LANES), F32),
                   jax.ShapeDtypeStruct((NP, S, LANES), F32), jax.ShapeDtypeStruct((4, S, SEG), BF16)],
        compiler_params=_cp(("parallel",), 60),
    )(doutb, w_out, o, zf32)


def _grad_w_out(yh, ya, doutb):
    S, SEG = yh.shape
    D = doutb.shape[1]
    R = (2 * SEG) // 4
    nb_half = SEG // R
    tk = min(2048, S)

    def body(yh_ref, ya_ref, d_ref, o_ref):
        q = pl.program_id(0)
        k = pl.program_id(1)

        @pl.when(k == 0)
        def _():
            o_ref[...] = jnp.zeros_like(o_ref)

        @pl.when(q < nb_half)
        def _():
            o_ref[...] += _dot_tn(yh_ref[...], d_ref[...])

        @pl.when(q >= nb_half)
        def _():
            o_ref[...] += _dot_tn(ya_ref[...], d_ref[...])

    return pl.pallas_call(
        body, grid=(4, S // tk), name="grad_w_out",
        in_specs=[pl.BlockSpec((tk, R), lambda q, k: (k, jnp.minimum(q, nb_half - 1))),
                  pl.BlockSpec((tk, R), lambda q, k: (k, jnp.maximum(q - nb_half, 0))),
                  pl.BlockSpec((tk, D), lambda q, k: (k, 0))],
        out_specs=pl.BlockSpec((None, R, D), lambda q, k: (q, 0, 0)),
        out_shape=jax.ShapeDtypeStruct((4, R, D), F32),
        compiler_params=_cp(("parallel", "arbitrary")),
    )(yh, ya, doutb)


def _dz_sources(sources):
    counts = [s.shape[0] for s in sources]
    starts = [sum(counts[:k]) for k in range(len(counts))]
    assert sum(counts) == 8
    return counts, starts


def _row_part(S, part, tile):
    first = max(512, (S * 3 // 8) // 512 * 512)
    rows = first if part == 0 else S - first
    assert rows % tile == 0 and first % tile == 0
    return (0 if part == 0 else first // tile), rows // tile, rows


def _dh_proj(sources, w_all, token, part, name):
    S = sources[0].shape[1]
    D = w_all.shape[1]
    SEG = w_all.shape[2] // 2
    counts, starts = _dz_sources(sources)
    assert all(c % 2 == 0 for c in counts)
    ns = len(sources)
    tm = 1024 if all(_row_part(S, p, 1)[2] % 1024 == 0 for p in (0, 1)) else 512
    t0, nt, nrows = _row_part(S, part, tm)

    def body(*refs):
        src = refs[:ns]
        w_ref, _, o_ref = refs[ns:]
        j = pl.program_id(1)

        @pl.when(j == 0)
        def _():
            o_ref[...] = jnp.zeros_like(o_ref)

        for k in range(ns):
            @pl.when((2 * j >= starts[k]) & (2 * j < starts[k] + counts[k]))
            def _(k=k):
                o_ref[...] += (_dot_nt(src[k][0], w_ref[:, pl.ds(0, SEG)])
                               + _dot_nt(src[k][1], w_ref[:, pl.ds(SEG, SEG)]))

    def src_spec(k):
        return pl.BlockSpec((2, tm, SEG),
                            lambda i, j: (jnp.clip(j - starts[k] // 2, 0, counts[k] // 2 - 1), t0 + i, 0))

    return pl.pallas_call(
        body, grid=(nt, 4), name=name,
        in_specs=[src_spec(k) for k in range(ns)] + [pl.BlockSpec((None, D, 2 * SEG), lambda i, j: (j, 0, 0)),
                                                     pl.BlockSpec(token.shape, lambda i, j: (0, 0))],
        out_specs=pl.BlockSpec((tm, D), lambda i, j: (i, 0)),
        out_shape=jax.ShapeDtypeStruct((nrows, D), F32),
        compiler_params=_cp(("parallel", "arbitrary"), 48 if tm == 512 else 60),
    )(*sources, w_all, token)


def _rms_bwd(dh, x2, rinv, gain, dout, part, gx_prev, name):
    S, D = x2.shape
    tm = 512
    t0, nt, _ = _row_part(S, part, tm)

    def body(dh_ref, x_ref, r_ref, g_ref, dout_ref, *rest):
        gx_ref, dg_ref = rest[-2:]

        @pl.when(pl.program_id(0) == 0)
        def _():
            dg_ref[...] = jnp.zeros_like(dg_ref)

        dh = dh_ref[...]
        r = r_ref[...]
        xhat = x_ref[...] * r
        dg_ref[...] += jnp.sum(dh * xhat, axis=0, keepdims=True)
        dxn = dh * g_ref[...]
        gx_ref[...] = dout_ref[...] + r * (dxn - xhat * jnp.mean(dxn * xhat, axis=-1, keepdims=True))

    row = lambda i: (t0 + i, 0)
    fix = lambda i: (0, 0)
    in_specs = [pl.BlockSpec((tm, D), lambda i: (i, 0)), pl.BlockSpec((tm, D), row), pl.BlockSpec((tm, 1), row),
                pl.BlockSpec((1, D), fix), pl.BlockSpec((tm, D), row)]
    args = [dh, x2, rinv, gain, dout]
    aliases = {}
    if gx_prev is not None:
        in_specs.append(ANY)
        args.append(gx_prev)
        aliases = {5: 0}
    return pl.pallas_call(
        body, grid=(nt,), name=name, in_specs=in_specs,
        out_specs=[pl.BlockSpec((tm, D), row), pl.BlockSpec((1, D), fix)],
        out_shape=[jax.ShapeDtypeStruct((S, D), F32), jax.ShapeDtypeStruct((1, D), F32)],
        input_output_aliases=aliases, compiler_params=_cp(("arbitrary",), 60),
    )(*args)


def _grad_w_in(h, sources):
    S, D = h.shape
    SEG = sources[0].shape[2]
    counts, starts = _dz_sources(sources)
    ns = len(sources)
    tk = min(2048, S)

    def body(*refs):
        h_ref = refs[0]
        src = refs[1:1 + ns]
        o_ref = refs[1 + ns]
        j = pl.program_id(0)
        k = pl.program_id(1)

        @pl.when(k == 0)
        def _():
            o_ref[...] = jnp.zeros_like(o_ref)

        for s in range(ns):
            @pl.when((j >= starts[s]) & (j < starts[s] + counts[s]))
            def _(s=s):
                o_ref[...] += _dot_tn(h_ref[...], src[s][...])

    def src_spec(s):
        return pl.BlockSpec((None, tk, SEG),
                            lambda j, k: (jnp.clip(j - starts[s], 0, counts[s] - 1), k, 0))

    return pl.pallas_call(
        body, grid=(8, S // tk), name="grad_w_in",
        in_specs=[pl.BlockSpec((tk, D), lambda j, k: (k, 0))] + [src_spec(s) for s in range(ns)],
        out_specs=pl.BlockSpec((None, D, SEG), lambda j, k: (j // 2, 0, j % 2)),
        out_shape=jax.ShapeDtypeStruct((4, D, 2 * SEG), F32),
        compiler_params=_cp(("parallel", "arbitrary"), 48 if tk <= 1024 else 62),
    )(h, *sources)


def _lower_bound(lbl):
    l0 = lbl[0:1, :]
    l1 = lbl[1:2, :]
    m = jnp.maximum(l0, l1)
    e0 = jnp.exp(l0 - m)
    e1 = jnp.exp(l1 - m)
    return e0 / (e0 + e1)


def _tile_masks():
    row = lax.broadcasted_iota(jnp.int32, (HGRN_TILE, HGRN_TILE), 0)
    col = lax.broadcasted_iota(jnp.int32, (HGRN_TILE, HGRN_TILE), 1)
    same = (row // HGRN_CHUNK) == (col // HGRN_CHUNK)
    return same & (row >= col), same & (row <= col)


def _chunk_last(b):
    T = b.shape[0]
    b3 = b.reshape(T // HGRN_CHUNK, HGRN_CHUNK, HGRN_HEAD)
    return jnp.broadcast_to(b3[:, HGRN_CHUNK - 1:HGRN_CHUNK, :], b3.shape).reshape(T, HGRN_HEAD)


def _chunk_sum(x):
    T = x.shape[0]
    x3 = x.reshape(T // HGRN_CHUNK, HGRN_CHUNK, HGRN_HEAD)
    return jnp.broadcast_to(jnp.sum(x3, axis=1, keepdims=True), x3.shape).reshape(T, HGRN_HEAD)


def _hgrn_dims(S, SEG, rows):
    T = min(rows, S)
    assert S % T == 0 and T % HGRN_TILE == 0
    tiles = [slice(t * HGRN_TILE, (t + 1) * HGRN_TILE) for t in range(T // HGRN_TILE)]
    chunks = [slice(c * HGRN_CHUNK, (c + 1) * HGRN_CHUNK) for c in range(T // HGRN_CHUNK)]
    return SEG // HGRN_HEAD, T, T // HGRN_CHUNK, S // T, tiles, chunks


def _hgrn_fwd(zf32, lb_logits, gnorm):
    _, NLB, S, _ = zf32.shape
    SEG = NLB * LANES
    H, T, NC, NJ, tiles, chunks = _hgrn_dims(S, SEG, HGRN_STEP_FWD[0])
    HP = min(HGRN_STEP_FWD[1], H)
    assert H % HP == 0

    def body(zq_ref, zf_ref, zi_ref, zg_ref, lbl_ref, gn_ref, y_ref, st_ref, state):
        @pl.when(pl.program_id(1) == 0)
        def _():
            state[...] = jnp.zeros_like(state)

        tril, _ = _tile_masks()
        tril_bf = tril.astype(BF16)
        for hh in range(HP):
            cols = slice(hh * HGRN_HEAD, (hh + 1) * HGRN_HEAD)
            lb = _lower_bound(lbl_ref[:, cols])
            zq = zq_ref[hh]
            q = zq * _sigmoid(zq)
            f = lb + (1.0 - lb) * _sigmoid(zf_ref[hh])
            k = 1.0 - f
            logf = jnp.log(f)
            b = jnp.concatenate([_exact_dot(tril_bf, logf[t]) for t in tiles], axis=0)
            bl = _chunk_last(b)
            qd_b = (q * jnp.exp(b)).astype(BF16)
            kd_b = (k * jnp.exp(-b)).astype(BF16)
            ke_b = (k * jnp.exp(bl - b)).astype(BF16)
            v_b = zi_ref[hh].astype(BF16)
            o_intra = jnp.concatenate(
                [_dot(jnp.where(tril, _dot_nt(qd_b[t], kd_b[t]), 0.0).astype(BF16), v_b[t]) for t in tiles], axis=0)
            kvs = [_dot_tn(v_b[r], ke_b[r]) for r in chunks]
            ebl = jnp.exp(bl)
            st = state[hh]
            sts = []
            for c in range(NC):
                st_ref[c, hh] = st
                sts.append(st.astype(BF16))
                st = st * ebl[c * HGRN_CHUNK:c * HGRN_CHUNK + 1, :] + kvs[c]
            state[hh] = st
            o = o_intra + jnp.concatenate([_dot_nt(qd_b[r], sb) for r, sb in zip(chunks, sts)], axis=0)
            on = o * lax.rsqrt(jnp.mean(o * o, axis=-1, keepdims=True) + NORM_EPS) * gn_ref[...]
            zg = zg_ref[hh]
            y_ref[:, cols] = (on * (zg * _sigmoid(zg))).astype(BF16)

    def zspec(seg):
        return pl.BlockSpec((None, HP, T, HGRN_HEAD), lambda h, j: (seg, h, j, 0))

    return pl.pallas_call(
        body, grid=(H // HP, NJ), name="hgrn_fwd",
        in_specs=[zspec(0), zspec(1), zspec(2), zspec(3),
                  pl.BlockSpec((2, HP * HGRN_HEAD), lambda h, j: (0, h)),
                  pl.BlockSpec((1, HGRN_HEAD), lambda h, j: (0, 0))],
        out_specs=[pl.BlockSpec((T, HP * HGRN_HEAD), lambda h, j: (j, h)),
                   pl.BlockSpec((NC, HP, HGRN_HEAD, HGRN_HEAD), lambda h, j: (j, h, 0, 0))],
        out_shape=[jax.ShapeDtypeStruct((S, SEG), BF16),
                   jax.ShapeDtypeStruct((S // HGRN_CHUNK, H, HGRN_HEAD, HGRN_HEAD), F32)],
        scratch_shapes=[pltpu.VMEM((HP, HGRN_HEAD, HGRN_HEAD), F32)],
        compiler_params=_cp(("parallel", "arbitrary")),
    )(zf32, zf32, zf32, zf32, lb_logits, gnorm)


def _hgrn_bwd(zf32, lb_logits, gnorm, states, dy):
    _, NLB, S, _ = zf32.shape
    SEG = NLB * LANES
    H, T, NC, NJ, tiles, chunks = _hgrn_dims(S, SEG, HGRN_STEP_BWD[0])
    C = HGRN_CHUNK
    HP = min(HGRN_STEP_BWD[1], H)
    assert H % HP == 0

    def body(zq_ref, zf_ref, zi_ref, zg_ref, lbl_ref, gn_ref, st_ref, dy_ref, dz_ref, dl_ref, dgn_ref, gstate):
        @pl.when(pl.program_id(1) == 0)
        def _():
            gstate[...] = jnp.zeros_like(gstate)
            dl_ref[...] = jnp.zeros_like(dl_ref)
            dgn_ref[...] = jnp.zeros_like(dgn_ref)

        gn = gn_ref[...]
        tril, triu = _tile_masks()
        tril_bf = tril.astype(BF16)
        triu_bf = triu.astype(BF16)
        for hh in range(HP):
            cols = slice(hh * HGRN_HEAD, (hh + 1) * HGRN_HEAD)
            lb = _lower_bound(lbl_ref[:, cols])
            q, dq_dz = _silu_and_grad(zq_ref[hh])
            sf = _sigmoid(zf_ref[hh])
            f = lb + (1.0 - lb) * sf
            k = 1.0 - f
            logf = jnp.log(f)
            b = jnp.concatenate([_exact_dot(tril_bf, logf[t]) for t in tiles], axis=0)
            bl = _chunk_last(b)
            eb = jnp.exp(b)
            enb = jnp.exp(-b)
            ekl = jnp.exp(bl - b)
            ebl = jnp.exp(bl)
            qd = q * eb
            kd = k * enb
            ke = k * ekl
            qd_b = qd.astype(BF16)
            kd_b = kd.astype(BF16)
            ke_b = ke.astype(BF16)
            v_b = zi_ref[hh].astype(BF16)
            sts = [st_ref[c, hh] for c in range(NC)]
            sts_b = [s.astype(BF16) for s in sts]
            a_b = [jnp.where(tril, _dot_nt(qd_b[t], kd_b[t]), 0.0).astype(BF16) for t in tiles]
            o = (jnp.concatenate([_dot(a, v_b[t]) for a, t in zip(a_b, tiles)], axis=0)
                 + jnp.concatenate([_dot_nt(qd_b[r], sb) for r, sb in zip(chunks, sts_b)], axis=0))
            rinv = lax.rsqrt(jnp.mean(o * o, axis=-1, keepdims=True) + NORM_EPS)
            ohat = o * rinv
            sg, dsg = _silu_and_grad(zg_ref[hh])
            dyv = dy_ref[:, cols]
            don = dyv * sg
            dz_ref[3, :, cols] = (dyv * (ohat * gn) * dsg).astype(BF16)
            dgn_ref[hh] += jnp.sum(don * ohat, axis=0, keepdims=True)
            dohat = don * gn
            do = rinv * (dohat - ohat * jnp.mean(dohat * ohat, axis=-1, keepdims=True))
            do_b = do.astype(BF16)
            da_b = [jnp.where(tril, _dot_nt(do_b[t], v_b[t]), 0.0).astype(BF16) for t in tiles]
            dv_intra = jnp.concatenate([_dot_tn(a, do_b[t]) for a, t in zip(a_b, tiles)], axis=0)
            dqd_intra = jnp.concatenate([_dot(da, kd_b[t]) for da, t in zip(da_b, tiles)], axis=0)
            dkd = jnp.concatenate([_dot_tn(da, qd_b[t]) for da, t in zip(da_b, tiles)], axis=0)
            dqd_inter = jnp.concatenate([_dot(do_b[r], sb) for r, sb in zip(chunks, sts_b)], axis=0)
            gks = [_dot_tn(do_b[r], qd_b[r]) for r in chunks]
            g = gstate[hh]
            gs = [None] * NC
            for c in reversed(range(NC)):
                gs[c] = g
                g = g * ebl[c * C:c * C + 1, :] + gks[c]
            gstate[hh] = g
            gs_b = [x.astype(BF16) for x in gs]
            dv = dv_intra + jnp.concatenate([_dot_nt(ke_b[r], gb) for r, gb in zip(chunks, gs_b)], axis=0)
            dz_ref[2, :, cols] = dv.astype(BF16)
            dke = jnp.concatenate([_dot(v_b[r], gb) for r, gb in zip(chunks, gs_b)], axis=0)
            debl = jnp.concatenate(
                [jnp.broadcast_to(jnp.sum(x * s, axis=0, keepdims=True), (C, HGRN_HEAD)) for x, s in zip(gs, sts)], axis=0)
            dqd = dqd_intra + dqd_inter
            dz_ref[0, :, cols] = ((dqd * eb) * dq_dz).astype(BF16)
            t_ke = dke * ke
            db = dqd * qd - dkd * kd - t_ke
            db_last = _chunk_sum(t_ke) + debl * ebl
            dk = dkd * enb + dke * ekl
            dlogf = jnp.concatenate([_exact_dot(triu_bf, db[t]) for t in tiles], axis=0) + db_last
            df = dlogf / f - dk
            dz_ref[1, :, cols] = (df * (1.0 - lb) * (sf * (1.0 - sf))).astype(BF16)
            dlb = jnp.sum(df * (1.0 - sf), axis=0, keepdims=True)
            dl0 = dlb * lb * (1.0 - lb)
            dl_ref[0:1, cols] += dl0
            dl_ref[1:2, cols] -= dl0

    def zspec(seg):
        return pl.BlockSpec((None, HP, T, HGRN_HEAD), lambda h, j: (seg, h, NJ - 1 - j, 0))

    return pl.pallas_call(
        body, grid=(H // HP, NJ), name="hgrn_bwd",
        in_specs=[zspec(0), zspec(1), zspec(2), zspec(3),
                  pl.BlockSpec((2, HP * HGRN_HEAD), lambda h, j: (0, h)),
                  pl.BlockSpec((1, HGRN_HEAD), lambda h, j: (0, 0)),
                  pl.BlockSpec((NC, HP, HGRN_HEAD, HGRN_HEAD), lambda h, j: (NJ - 1 - j, h, 0, 0)),
                  pl.BlockSpec((T, HP * HGRN_HEAD), lambda h, j: (NJ - 1 - j, h))],
        out_specs=[pl.BlockSpec((4, T, HP * HGRN_HEAD), lambda h, j: (0, NJ - 1 - j, h)),
                   pl.BlockSpec((2, HP * HGRN_HEAD), lambda h, j: (0, h)),
                   pl.BlockSpec((HP, 1, HGRN_HEAD), lambda h, j: (h, 0, 0))],
        out_shape=[jax.ShapeDtypeStruct((4, S, SEG), BF16), jax.ShapeDtypeStruct((2, SEG), F32),
                   jax.ShapeDtypeStruct((H, 1, HGRN_HEAD), F32)],
        scratch_shapes=[pltpu.VMEM((HP, HGRN_HEAD, HGRN_HEAD), F32)],
        compiler_params=_cp(("parallel", "arbitrary")),
    )(zf32, zf32, zf32, zf32, lb_logits, gnorm, states, dy)


def _alibi_slopes(seg):
    n_heads = seg // ATTN_HEAD
    s = 2.0 ** (-8.0 * np.arange(1, n_heads + 1, dtype=np.float64) / n_heads)
    return jnp.asarray(np.repeat(s, ATTN_HEAD)[None, :], F32)


def _attn_dims(S, SEG, d, block_elems=ATTN_BLOCK_ELEMS):
    rb = BAND * d
    assert S % rb == 0 and SEG % LANES == 0
    npb = max(1, min(SEG // LANES, block_elems // (rb * LANES)))
    assert (SEG // LANES) % npb == 0
    return rb, npb, S // rb, (SEG // LANES) // npb


def _res_rows(r, d):
    return pl.ds(0, BAND) if d == 1 else pl.ds(r, BAND, stride=d)


def _for_residues(d, fn):
    if d == 1:
        fn(0)
    else:
        def step(r, carry):
            fn(r)
            return carry
        lax.fori_loop(0, d, step, 0, unroll=ATTN_UNROLL)


def _for_groups(d, n_pairs, fn):
    def over_pairs(r):
        for g0 in range(0, n_pairs, ATTN_UNROLL):
            fn([(r, p) for p in range(g0, min(n_pairs, g0 + ATTN_UNROLL))])

    if d == 1:
        over_pairs(0)
    elif n_pairs >= ATTN_UNROLL:
        def step(r, carry):
            over_pairs(r)
            return carry
        lax.fori_loop(0, d, step, 0)
    else:
        per_group = ATTN_UNROLL // n_pairs
        assert d % per_group == 0

        def step(g, carry):
            fn([(g * per_group + i, p) for i in range(per_group) for p in range(n_pairs)])
            return carry
        lax.fori_loop(0, d // per_group, step, 0)


def _band_terms(n, d):
    i = lax.broadcasted_iota(jnp.int32, (BAND, 2 * BAND), 0)
    jj = lax.broadcasted_iota(jnp.int32, (BAND, 2 * BAND), 1)
    delta = BAND + i - jj
    valid = (delta >= 0) & (delta <= BAND) & ((n > 0) | (jj >= BAND))
    return (-d * delta).astype(F32), valid


def _head_biases(slopes, nd, valid):
    out = []
    for s in _per_head(slopes):
        s2 = jnp.concatenate([s, s], axis=1)
        out.append(jnp.where(valid, s2 * nd, NEG))
    return jnp.concatenate(out, axis=0)


def _stack_heads(x):
    lane = lax.broadcasted_iota(jnp.int32, x.shape, 1)
    zero = jnp.zeros_like(x)
    return jnp.concatenate([jnp.where(lane < ATTN_HEAD, x, zero), jnp.where(lane < ATTN_HEAD, zero, x)], axis=0)


def _unstack_heads(x2):
    first = lax.broadcasted_iota(jnp.int32, (BAND, LANES), 1) < ATTN_HEAD
    return jnp.where(first, x2[:BAND], x2[BAND:])


def _stack_per_head(x):
    a, b = _per_head(x)
    col = jnp.concatenate([a, b], axis=0)
    return jnp.concatenate([col, col], axis=1)


def _per_head(x):
    lane = lax.broadcasted_iota(jnp.int32, x.shape, 1)
    sw = pltpu.roll(x, ATTN_HEAD, 1)
    first = lane < ATTN_HEAD
    return jnp.where(first, x, sw), jnp.where(first, sw, x)


def _qkv_source(zz, d):
    z, z16 = zz
    if d == DEINTERLEAVE:
        def take(ref, p, r):
            return ref.at[p][r]

        def spec(seg, np_, row_block):
            return pl.BlockSpec((None, np_, d, BAND, LANES), lambda c, n: (seg, c, 0, row_block(c, n), 0))
        return z, z16, take, spec

    def take(ref, p, r):
        return ref.at[p][_res_rows(r, d), :]

    def spec(seg, np_, row_block):
        return pl.BlockSpec((None, np_, BAND * d, LANES), lambda c, n: (SEG_QKV + seg, c, row_block(c, n), 0))
    return z, z, take, spec


def _attn_fwd(qkv, slopes, d):
    qkv, src, take, spec = _qkv_source(qkv, d)
    _, NLB, S, _ = qkv.shape
    rb, NP, nb, ncb = _attn_dims(S, NLB * LANES, d, 2 * ATTN_BLOCK_ELEMS)

    def body(q_ref, kc_ref, vc_ref, sl_ref, o_ref, l_ref, kp_ref, vp_ref):
        n = pl.program_id(1)

        @pl.when(n == 0)
        def _():
            kp_ref[...] = jnp.zeros_like(kp_ref)
            vp_ref[...] = jnp.zeros_like(vp_ref)

        nd, valid = _band_terms(n, d)
        biases = [_head_biases(sl_ref[:, p * LANES:(p + 1) * LANES], nd, valid) for p in range(NP)]

        def group(items):
            scores, values = [], []
            for r, p in items:
                kc = jnp.concatenate([take(kp_ref, p, r), take(kc_ref, p, r)], axis=0).astype(BF16)
                values.append(jnp.concatenate([take(vp_ref, p, r), take(vc_ref, p, r)], axis=0).astype(BF16))
                scores.append(_dot_nt(_stack_heads((take(q_ref, p, r) * ATTN_SCALE).astype(BF16)), kc))
            probs = []
            for (r, p), s in zip(items, scores):
                s = s + biases[p]
                m = jnp.max(s, axis=-1, keepdims=True)
                e = jnp.exp(s - m)
                den = jnp.sum(e, axis=-1, keepdims=True)
                probs.append((e.astype(BF16), den, m + jnp.log(den)))
            for (r, p), vc, (e, den, lse) in zip(items, values, probs):
                rows = _res_rows(r, d)
                o_ref.at[p][rows, :] = _unstack_heads(_dot(e, vc) / den)
                l_ref.at[p][rows, :] = _unstack_heads(jnp.broadcast_to(lse, (2 * BAND, LANES)))

        _for_groups(d, NP, group)
        kp_ref[...] = kc_ref[...]
        vp_ref[...] = vc_ref[...]

    cur = lambda c, n: n
    out = pl.BlockSpec((NP, rb, LANES), lambda c, n: (c, n, 0))
    kv_block = spec(1, NP, cur).block_shape[1:]
    return pl.pallas_call(
        body, grid=(ncb, nb), name=f"attn_fwd_d{d}",
        in_specs=[spec(0, NP, cur), spec(1, NP, cur), spec(2, NP, cur),
                  pl.BlockSpec((1, NP * LANES), lambda c, n: (0, c))],
        out_specs=[out, out],
        out_shape=[jax.ShapeDtypeStruct((NLB, S, LANES), F32)] * 2,
        scratch_shapes=[pltpu.VMEM(kv_block, F32), pltpu.VMEM(kv_block, F32)],
        compiler_params=_cp(("parallel", "arbitrary")),
    )(src, src, src, slopes)


def _attn_bwd(qkv, slopes, do, lse, dl, d, acc, into):
    qkv, src, take, spec = _qkv_source(qkv, d)
    _, NLB, S, _ = qkv.shape
    SEG = NLB * LANES
    rb, NP, nb, ncb = _attn_dims(S, SEG, d)
    has_acc = acc is not None
    out_dtype = F32 if into is None else into.dtype
    assert into is None or d == 1

    def body(*refs):
        q_ref, kc_ref, vc_ref, sl_ref, do_ref, lse_ref, dl_ref = refs[:7]
        acc_ref = refs[7] if has_acc else None
        out_ref, cq, ck, cv, kp_ref, vp_ref = refs[-6:]
        n = pl.program_id(1)

        def emit(r, p, dq, dk, dv):
            rows = _res_rows(r, d)
            for t, val in enumerate((dq, dk, dv)):
                if has_acc:
                    val = val + acc_ref.at[t].at[p][rows, :]
                if into is None:
                    out_ref.at[t].at[p][rows, :] = val.astype(out_dtype)
                else:
                    out_ref.at[t][rows, p * LANES:(p + 1) * LANES] = val.astype(out_dtype)

        @pl.when(n == 0)
        def _():
            cq[...] = jnp.zeros_like(cq)
            ck[...] = jnp.zeros_like(ck)
            cv[...] = jnp.zeros_like(cv)
            kp_ref[...] = jnp.zeros_like(kp_ref)
            vp_ref[...] = jnp.zeros_like(vp_ref)

        @pl.when(n < nb)
        def _():
            nd, valid = _band_terms(n, d)
            biases = [_head_biases(sl_ref[:, p * LANES:(p + 1) * LANES], nd, valid) for p in range(NP)]

            def group(items):
                first = []
                for r, p in items:
                    rows = _res_rows(r, d)
                    kc = jnp.concatenate([take(kp_ref, p, r), take(kc_ref, p, r)], axis=0).astype(BF16)
                    vc = jnp.concatenate([take(vp_ref, p, r), take(vc_ref, p, r)], axis=0).astype(BF16)
                    qs = _stack_heads((take(q_ref, p, r) * ATTN_SCALE).astype(BF16))
                    dos = _stack_heads(do_ref.at[p][rows, :].astype(BF16))
                    first.append((kc, qs, dos, _dot_nt(qs, kc), _dot_nt(dos, vc)))
                second = []
                for (r, p), (kc, qs, dos, s, dp) in zip(items, first):
                    rows = _res_rows(r, d)
                    pr = jnp.exp(s + biases[p] - _stack_per_head(lse_ref.at[p][rows, :]))
                    ds = (pr * (dp - _stack_per_head(dl_ref.at[p][rows, :]))).astype(BF16)
                    second.append((kc, qs, dos, pr.astype(BF16), ds))
                for (r, p), (kc, qs, dos, pr, ds) in zip(items, second):
                    dq = _unstack_heads(_dot(ds, kc)) * ATTN_SCALE
                    dk = _dot_tn(ds, qs)
                    dv = _dot_tn(pr, dos)
                    emit(r, p, cq[r, p], ck[r, p] + dk[:BAND, :], cv[r, p] + dv[:BAND, :])
                    cq[r, p] = dq
                    ck[r, p] = dk[BAND:, :]
                    cv[r, p] = dv[BAND:, :]

            _for_groups(d, NP, group)
            kp_ref[...] = kc_ref[...]
            vp_ref[...] = vc_ref[...]

        @pl.when(n == nb)
        def _():
            def last(r):
                for p in range(NP):
                    emit(r, p, cq[r, p], ck[r, p], cv[r, p])
            _for_residues(d, last)

    cur = lambda c, n: (c, jnp.minimum(n, nb - 1), 0)
    lag = lambda c, n: (0, c, jnp.clip(n - 1, 0, nb - 1), 0)

    at = lambda c, n: jnp.minimum(n, nb - 1)
    in_specs = [spec(0, NP, at), spec(1, NP, at), spec(2, NP, at),
                pl.BlockSpec((1, NP * LANES), lambda c, n: (0, c))] + [pl.BlockSpec((NP, rb, LANES), cur)] * 3
    kv_block = in_specs[1].block_shape[1:]
    args = [src, src, src, slopes, do, lse, dl]
    aliases = {}
    if has_acc:
        in_specs.append(pl.BlockSpec((3, NP, rb, LANES), lag))
        args.append(acc)
        if into is None:
            aliases = {7: 0}
    if into is None:
        out_sds = jax.ShapeDtypeStruct((3, NLB, S, LANES), F32)
        out_spec = pl.BlockSpec((3, NP, rb, LANES), lag)
    else:
        in_specs.append(ANY)
        args.append(into)
        aliases = {len(args) - 1: 0}
        out_sds = jax.ShapeDtypeStruct(into.shape, into.dtype)
        out_spec = pl.BlockSpec((3, rb, NP * LANES), lambda c, n: (0, jnp.clip(n - 1, 0, nb - 1), c))
    return pl.pallas_call(
        body, grid=(ncb, nb + 1), name=f"attn_bwd_d{d}",
        in_specs=in_specs, out_specs=out_spec, out_shape=out_sds,
        scratch_shapes=[pltpu.VMEM((d, NP, BAND, LANES), F32)] * 3 + [pltpu.VMEM(kv_block, F32)] * 2,
        input_output_aliases=aliases,
        compiler_params=_cp(("parallel", "arbitrary")),
    )(*args)


def _adamw(w, g, m, v, name):
    R, C = w.shape
    tr = R if R <= 256 else 256
    assert R % tr == 0

    def body(w_ref, g_ref, m_ref, v_ref, d_ref, nm_ref, nv_ref, go_ref):
        g = g_ref[...]
        nm = ADAM_B1 * m_ref[...] + (1.0 - ADAM_B1) * g
        nv = ADAM_B2 * v_ref[...] + (1.0 - ADAM_B2) * (g * g)
        m_hat = nm / (1.0 - ADAM_B1 ** ADAM_STEP)
        v_hat = nv / (1.0 - ADAM_B2 ** ADAM_STEP)
        d_ref[...] = -ADAM_LR * (m_hat / (jnp.sqrt(v_hat) + ADAM_EPS) + ADAM_WD * w_ref[...])
        nm_ref[...] = nm
        nv_ref[...] = nv
        go_ref[...] = g

    blk = pl.BlockSpec((tr, C), lambda i: (i, 0))
    sds = jax.ShapeDtypeStruct((R, C), F32)
    return pl.pallas_call(
        body, grid=(R // tr,), name=name, in_specs=[blk] * 4, out_specs=[blk] * 4, out_shape=[sds] * 4,
        compiler_params=_cp(("parallel",)),
    )(w, g, m, v)


def _coords():
    return lax.axis_index("x"), lax.axis_index("y"), lax.axis_index("c")


def _other_chips(x, y):
    return [(1 - x, y), (x, 1 - y), (1 - x, 1 - y)]


ANY = pl.BlockSpec(memory_space=pl.ANY)


def _cast_into_slot(w, where, name):
    R, C = w.shape
    tr = min(256, R)

    def body(where_ref, w_ref, o_ref):
        o_ref[...] = w_ref[...].astype(BF16)

    grid_spec = pltpu.PrefetchScalarGridSpec(
        num_scalar_prefetch=1, grid=(R // tr,),
        in_specs=[pl.BlockSpec((tr, C), lambda i, w: (i, 0))],
        out_specs=pl.BlockSpec((None, tr, C), lambda i, w: (w[1], i, 0)))
    return pl.pallas_call(
        body, grid_spec=grid_spec, name=name, out_shape=jax.ShapeDtypeStruct((4, R, C), BF16),
        compiler_params=_cp(("parallel",)),
    )(where, w)


def _pair_sum(g, sib, where, name):
    _, n2, C = g.shape
    N = n2 // 2
    tr = min(256, N)
    nt = N // tr

    def body(where_ref, g_ref, s_ref, qb_ref, own_ref):
        q = pl.program_id(1)
        tot = g_ref[...] + s_ref[...]
        qb_ref[...] = tot.astype(BF16)

        @pl.when(q == where_ref[1])
        def _():
            own_ref[...] = tot

    grid_spec = pltpu.PrefetchScalarGridSpec(
        num_scalar_prefetch=1, grid=(nt, 4),
        in_specs=[pl.BlockSpec((None, tr, C), lambda i, q, w: (q, w[0] * nt + i, 0)),
                  pl.BlockSpec((None, tr, C), lambda i, q, w: (q, i, 0))],
        out_specs=[pl.BlockSpec((None, tr, C), lambda i, q, w: (q, i, 0)),
                   pl.BlockSpec((tr, C), lambda i, q, w: (i, 0))])
    return pl.pallas_call(
        body, grid_spec=grid_spec, name=name,
        out_shape=[jax.ShapeDtypeStruct((4, N, C), BF16), jax.ShapeDtypeStruct((N, C), F32)],
        compiler_params=_cp(("parallel", "arbitrary")),
    )(where, g, sib)


HBM = pl.BlockSpec(memory_space=pltpu.HBM)
SEM = pl.BlockSpec(memory_space=pltpu.SEMAPHORE)


def _in_hbm(a):
    return pltpu.with_memory_space_constraint(a, pltpu.HBM)


def _split_start(name, copies, arrays, n_sems, after=None):
    n = len(arrays)

    def body(*refs):
        for cp in copies(refs[:n], refs[-n - 3], refs[-n - 2]):
            cp.start()
        refs[-1][...] = jnp.zeros_like(refs[-1])

    ordered = () if after is None else (after,)
    outs = pl.pallas_call(
        body, name=name,
        out_shape=(pltpu.SemaphoreType.DMA((n_sems,)), pltpu.SemaphoreType.DMA((n_sems,)),
                   *[pltpu.HBM(a.shape, a.dtype) for a in arrays], jax.ShapeDtypeStruct((8, LANES), F32)),
        in_specs=(HBM,) * n + (ANY,) * len(ordered),
        out_specs=(SEM, SEM) + (HBM,) * n + (pl.BlockSpec(memory_space=pltpu.VMEM),),
        input_output_aliases={i: 2 + i for i in range(n)},
        compiler_params=pltpu.CompilerParams(has_side_effects=pltpu.SideEffectType.DATAFLOW_SIDE_EFFECTING),
    )(*[_in_hbm(a) for a in arrays], *ordered)
    return outs[0], outs[1], list(outs[2:2 + n]), outs[-1]


def _split_wait(name, copies, send_sems, recv_sems, arrays, after):
    n = len(arrays)

    def body(*refs):
        for cp in copies(refs[:n], refs[n], refs[n + 1]):
            cp.wait_send()
            cp.wait_recv()

    outs = pl.pallas_call(
        body, name=name,
        out_shape=tuple(pltpu.HBM(a.shape, a.dtype) for a in arrays),
        in_specs=(HBM,) * n + (SEM, SEM, ANY), out_specs=(HBM,) * n,
        input_output_aliases={i: i for i in range(n)},
        compiler_params=pltpu.CompilerParams(has_side_effects=pltpu.SideEffectType.DATAFLOW_SIDE_EFFECTING),
    )(*arrays, send_sems, recv_sems, after)
    return list(outs)


def _remote(src, dst, sems, k, to):
    send_sems, recv_sems = sems
    return pltpu.make_async_remote_copy(src_ref=src, dst_ref=dst, send_sem=send_sems.at[k], recv_sem=recv_sems.at[k],
                                        device_id=to, device_id_type=MESH)


def _chip_at(x, y, rel):
    px = 1 - x if rel & 2 else x
    py = 1 - y if rel & 1 else y
    return px, py, 2 * px + py


def _gather_in_copies(rels):
    def copies(refs, send_sems, recv_sems):
        (w,) = refs
        x, y, c = _coords()
        seg = w.shape[2] // 2
        mine = w.at[2 * x + y, :, pl.ds(c * seg, seg)]
        return [_remote(mine, mine, (send_sems, recv_sems), k, _chip_at(x, y, rel)[:2] + (c,))
                for k, rel in enumerate(rels)]
    return copies


def _gather_out_copies(refs, send_sems, recv_sems):
    (w,) = refs
    x, y, c = _coords()
    mine = w.at[2 * x + y]
    return [_remote(mine, mine, (send_sems, recv_sems), k, (px, py, c)) for k, (px, py) in enumerate(_other_chips(x, y))]


def _swap_copies(refs, send_sems, recv_sems):
    gi, go, si, so = refs
    x, y, c = _coords()
    cps = []
    for a, (src, dst) in enumerate(((gi, si), (go, so))):
        nr = dst.shape[1]
        cps.append(_remote(src.at[:, pl.ds((1 - c) * nr, nr), :], dst, (send_sems, recv_sems), a, (x, y, 1 - c)))
    return cps


def _scatter_copies(refs, send_sems, recv_sems):
    qi, qo, ri, ro = refs
    x, y, c = _coords()
    cps = []
    for k, (px, py) in enumerate(_other_chips(x, y)):
        for a, (src, dst) in enumerate(((qi, ri), (qo, ro))):
            cps.append(_remote(src.at[2 * px + py], dst.at[k], (send_sems, recv_sems), 2 * k + a, (px, py, c)))
    return cps


def _forward_copies(rels):
    def copies(refs, send_sems, recv_sems):
        (w,) = refs
        x, y, c = _coords()
        seg = w.shape[2] // 2
        cps = []
        for k, rel in enumerate(rels):
            got = w.at[_chip_at(x, y, rel)[2], :, pl.ds(c * seg, seg)]
            cps.append(_remote(got, got, (send_sems, recv_sems), k, (x, y, 1 - c)))
        return cps
    return copies


def _chip_sum(own, got, where, name):
    N, C = own.shape
    tr = min(256, N)
    nt = N // tr

    def body(where_ref, own_ref, got_ref, o_ref):
        t = own_ref[...]
        for k in range(3):
            t = t + got_ref[k].astype(F32)
        o_ref[...] = t

    grid_spec = pltpu.PrefetchScalarGridSpec(
        num_scalar_prefetch=1, grid=(nt,),
        in_specs=[pl.BlockSpec((tr, C), lambda i, w: (i, 0)), pl.BlockSpec((3, tr, C), lambda i, w: (0, i, 0))],
        out_specs=pl.BlockSpec((tr, C), lambda i, w: (w[0] * nt + i, 0)))
    return pl.pallas_call(
        body, grid_spec=grid_spec, name=name, out_shape=jax.ShapeDtypeStruct((2 * N, C), F32),
        compiler_params=_cp(("parallel",)),
    )(where, own, got)


def _join_copies(refs, send_sems, recv_sems):
    x, y, c = _coords()
    cps = []
    for a, ref in enumerate(refs):
        nr = ref.shape[0] // 2
        mine = ref.at[pl.ds(c * nr, nr), :]
        cps.append(_remote(mine, mine, (send_sems, recv_sems), a, (x, y, 1 - c)))
    return cps


def _all_reduce_small(part, token):
    R, C = part.shape

    def body(p_ref, _, o_ref, slots, send_sems, recv_sems):
        x, y, c = _coords()
        me = 4 * x + 2 * y + c
        slots[me] = p_ref[...]
        cps = []
        for k in range(1, 8):
            fx, fy, fc = (k >> 2) & 1, (k >> 1) & 1, k & 1
            peer = (1 - x if fx else x, 1 - y if fy else y, 1 - c if fc else c)
            cp = pltpu.make_async_remote_copy(src_ref=p_ref, dst_ref=slots.at[me], send_sem=send_sems.at[k - 1],
                                              recv_sem=recv_sems.at[k - 1], device_id=peer, device_id_type=MESH)
            cp.start()
            cps.append(cp)
        for cp in cps:
            cp.wait()
        t = slots[0]
        for k in range(1, 8):
            t = t + slots[k]
        o_ref[...] = t

    vm = pl.BlockSpec(memory_space=pltpu.VMEM)
    return pl.pallas_call(
        body, name="all_reduce_small", in_specs=[vm, vm], out_specs=vm,
        out_shape=jax.ShapeDtypeStruct((R, C), F32),
        scratch_shapes=[pltpu.VMEM((8, R, C), F32), pltpu.SemaphoreType.DMA((7,)), pltpu.SemaphoreType.DMA((7,))],
    )(part, token)


def _mixers_forward(z, lb_logits, hgrn_gnorm):
    slopes = _alibi_slopes(z[0].shape[1] * LANES)
    yh, states = _hgrn_fwd(z[0], lb_logits, hgrn_gnorm)
    outs, lses = [], []
    for d in DILATIONS:
        o, l = _attn_fwd(z, slopes, d)
        outs.append(o)
        lses.append(l)
    return yh, (outs, lses), (states, slopes)


def _backward_to_dz(z, kept, lb_logits, hgrn_gnorm, yh, patterns, w_out_all, x2, tgt, fgain, h):
    states, slopes = kept
    dout, doutb, loss, dfg, o_attn, lse, ya = _out_proj_loss(yh, patterns[0], patterns[1], z[0], w_out_all, x2, tgt,
                                                             fgain)
    dy, do, dl, dza = _dy_proj_gate(doutb, w_out_all, o_attn, z[0])
    g_w_out = _grad_w_out(yh, ya, doutb)
    dzh, dlogits, dgn = _hgrn_bwd(z[0], lb_logits, hgrn_gnorm, states, dy)
    acc = None
    order = sorted(DILATIONS, reverse=True)
    for d in order[:-1]:
        acc = _attn_bwd(z, slopes, do, lse, dl, d, acc, None)
    dza = _attn_bwd(z, slopes, do, lse, dl, order[-1], acc, dza)
    sources = [dzh, dza]
    g_w_in = _grad_w_in(h, sources)
    return loss, dfg, dlogits, dgn, g_w_out, g_w_in, sources, dout


def _grad_x_half(sources, w_all, x2, rinv, norm_gain, dout, token, part, gx_prev):
    dh = _dh_proj(sources, w_all, token, part, f"dh_proj_{part}")
    return _rms_bwd(dh, x2, rinv, norm_gain, dout, part, gx_prev, f"rms_bwd_{part}")


def _local_step(x2, tgt, norm_gain, w_all, lb_logits, hgrn_gnorm, w_out_all, fgain):
    token = jnp.zeros((8, LANES), F32)
    where = jnp.zeros((2,), jnp.int32)
    h, rinv = _rms_fwd(x2, norm_gain, token)
    z = _in_proj(h, w_all, where, [(rel, half) for rel in range(4) for half in range(2)], None, token, "in_proj_all")
    yh, patterns, kept = _mixers_forward(z, lb_logits, hgrn_gnorm)
    loss, dfg, dlogits, dgn, g_w_out, g_w_in, sources, dout = _backward_to_dz(
        z, kept, lb_logits, hgrn_gnorm, yh, patterns, w_out_all, x2, tgt, fgain, h)
    gx, dg0 = _grad_x_half(sources, w_all, x2, rinv, norm_gain, dout, token, 0, None)
    gx, dg1 = _grad_x_half(sources, w_all, x2, rinv, norm_gain, dout, token, 1, gx)
    return loss, gx, dg0 + dg1, g_w_in, dlogits, dgn, g_w_out, dfg


def _pack_small(D, loss, dgain, dlogits, dgn, dfg):
    def row(v):
        v = v.reshape(1, -1)
        return jnp.pad(v, ((0, 0), (0, D - v.shape[1])))
    rows = [row(dgain), row(dfg), row(dlogits[0]), row(dlogits[1]), row(jnp.sum(dgn, axis=0)), row(loss)]
    rows += [jnp.zeros((1, D), F32)] * (8 - len(rows))
    return jnp.concatenate(rows, axis=0)


def kernel(x, norm_gain, w_in, lb_logits, hgrn_gnorm, w_out, final_gain, loss_target, m_norm_gain, m_w_in, m_lb_logits, m_hgrn_gnorm, m_w_out, m_final_gain, v_norm_gain, v_w_in, v_lb_logits, v_hgrn_gnorm, v_w_out, v_final_gain):
    _, S, D = x.shape
    SEG = w_in.shape[2] // 2
    x2 = x[0]
    tgt = loss_target[0]
    fgain = final_gain.reshape(1, D)
    where = jnp.stack([lax.axis_index("c"), 2 * lax.axis_index("x") + lax.axis_index("y")]).astype(jnp.int32)

    wia = _cast_into_slot(w_in[0], where, "cast_w_in")
    woa = _cast_into_slot(w_out[0], where, "cast_w_out")
    near, far = (2, 1), (3,)
    ga = _split_start("gather_near_start", _gather_in_copies(near), [wia], 2)
    h, rinv = _rms_fwd(x2, norm_gain, ga[3])
    z = _in_proj(h, ga[2][0], where, [(0, 0), (0, 1)], None, ga[3], "in_proj_own")
    (wia,) = _split_wait("gather_near_wait", _gather_in_copies(near), ga[0], ga[1], ga[2], z[0])
    gb = _split_start("gather_far_start", _gather_in_copies(far), [wia], 1)
    fa = _split_start("forward_near_start", _forward_copies(near), gb[2], 2, after=gb[3])
    z = _in_proj(h, fa[2][0], where, [(2, "mine"), (1, "mine")], z, fa[3], "in_proj_near")
    (wia,) = _split_wait("forward_near_wait", _forward_copies(near), fa[0], fa[1], fa[2], z[0])
    (wia,) = _split_wait("gather_far_wait", _gather_in_copies(far), gb[0], gb[1], [wia], z[0])
    out_sems = _split_start("gather_out_start", _gather_out_copies, [woa], 3, after=wia)
    fb = _split_start("forward_far_start", _forward_copies(far), [wia], 1, after=out_sems[3])
    z = _in_proj(h, fb[2][0], where, [(3, "mine"), (2, "sibling"), (1, "sibling")], z, fb[3], "in_proj_far")
    (wia,) = _split_wait("forward_far_wait", _forward_copies(far), fb[0], fb[1], fb[2], z[0])
    z = _in_proj(h, wia, where, [(3, "sibling")], z, fb[3], "in_proj_last")
    yh, patterns, kept = _mixers_forward(z, lb_logits, hgrn_gnorm)
    (woa,) = _split_wait("gather_out_wait", _gather_out_copies, out_sems[0], out_sems[1], out_sems[2],
                         patterns[1][-1])
    w_out_all = woa.reshape(2 * SEG, D)

    loss, dfg, dlogits, dgn, g_w_out, g_w_in, sources, dout = _backward_to_dz(
        z, kept, lb_logits, hgrn_gnorm, yh, patterns, w_out_all, x2, tgt, fgain, h)

    sib_i = lax.empty((4, g_w_in.shape[1] // 2, g_w_in.shape[2]), F32)
    sib_o = lax.empty((4, g_w_out.shape[1] // 2, g_w_out.shape[2]), F32)
    sems = _split_start("swap_start", _swap_copies, [g_w_in, g_w_out, sib_i, sib_o], 2)
    grad_x, dg0 = _grad_x_half(sources, wia, x2, rinv, norm_gain, dout, sems[3], 0, None)
    g_w_in, g_w_out, sib_i, sib_o = _split_wait("swap_wait", _swap_copies, sems[0], sems[1], sems[2], grad_x)
    qi, own_i = _pair_sum(g_w_in, sib_i, where, "pair_sum_w_in")
    qo, own_o = _pair_sum(g_w_out, sib_o, where, "pair_sum_w_out")
    ri = lax.empty((3,) + qi.shape[1:], BF16)
    ro = lax.empty((3,) + qo.shape[1:], BF16)
    sems = _split_start("scatter_start", _scatter_copies, [qi, qo, ri, ro], 6)
    grad_x, dg1 = _grad_x_half(sources, wia, x2, rinv, norm_gain, dout, sems[3], 1, grad_x)
    _, _, got_i, got_o = _split_wait("scatter_wait", _scatter_copies, sems[0], sems[1], sems[2], grad_x)
    jn = _split_start("join_start", _join_copies, [_chip_sum(own_i, got_i, where, "chip_sum_w_in"),
                                                   _chip_sum(own_o, got_o, where, "chip_sum_w_out")], 2)
    small = _all_reduce_small(_pack_small(D, loss, dg0 + dg1, dlogits, dgn, dfg), jn[3])
    loss_sum = small[5, 0]
    d_ng, m_ng, v_ng, grad_norm_gain = _adamw(norm_gain, small[0:1, :], m_norm_gain, v_norm_gain, "adamw_norm_gain")
    d_lb, m_lb, v_lb, grad_lb_logits = _adamw(lb_logits, small[2:4, :SEG], m_lb_logits, v_lb_logits, "adamw_lb_logits")
    d_gn, m_gn, v_gn, grad_hgrn_gnorm = _adamw(hgrn_gnorm, small[4:5, :HGRN_HEAD], m_hgrn_gnorm, v_hgrn_gnorm,
                                               "adamw_hgrn_gnorm")
    d_fg, m_fg, v_fg, grad_final_gain = _adamw(fgain, small[1:2, :], m_final_gain.reshape(1, D),
                                               v_final_gain.reshape(1, D), "adamw_final_gain")
    g_w_in, g_w_out = _split_wait("join_wait", _join_copies, jn[0], jn[1], jn[2], d_fg)
    d_wi, m_wi, v_wi, grad_w_in = _adamw(w_in[0], g_w_in, m_w_in[0], v_w_in[0], "adamw_w_in")
    d_wo, m_wo, v_wo, grad_w_out = _adamw(w_out[0], g_w_out, m_w_out[0], v_w_out[0], "adamw_w_out")

    return (loss_sum, grad_x[None],
            grad_norm_gain, grad_w_in[None], grad_lb_logits, grad_hgrn_gnorm, grad_w_out[None], grad_final_gain[0],
            d_ng, d_wi[None], d_lb, d_gn, d_wo[None], d_fg[0],
            m_ng, m_wi[None], m_lb, m_gn, m_wo[None], m_fg[0],
            v_ng, v_wi[None], v_lb, v_gn, v_wo[None], v_fg[0])
```
